```python
import math
import jax, jax.numpy as jnp
from jax import lax
import numpy as np

D_MODEL = 1024
BATCH = 16
SEQ = 4096
DEPTH = 2

N_A = DEPTH // 2
N_B = DEPTH - N_A

E_A = D_MODEL
CONV_WIDTH = 3
CONV_GROUPS = 16

N_HEADS = 8
QK_NOPE = 64
QK_ROPE = 32
V_HEAD = 64
KV_RANK = 256
Q_RANK = 384
E_B = N_HEADS * V_HEAD
ROPE_THETA = 10000.0
Q_BLOCK = 128
SOFTMAX_SCALE = 1.0 / math.sqrt(QK_NOPE + QK_ROPE)

EPS = 1e-6

kernel_name = "yoco_shortconv_mla_hybrid"


def rms_norm(x, g):
    xf = x.astype(jnp.float32)
    y = xf * lax.rsqrt(jnp.mean(xf * xf, axis=-1, keepdims=True) + EPS)
    return (y * g.astype(jnp.float32)).astype(x.dtype)


def rope_tables(positions, dtype):
    inv_freq = ROPE_THETA ** (-jnp.arange(0, QK_ROPE, 2, dtype=jnp.float32) / QK_ROPE)
    ang = positions.astype(jnp.float32)[..., None] * inv_freq
    return jnp.cos(ang).astype(dtype), jnp.sin(ang).astype(dtype)


def apply_rope(x, cos, sin):
    x1, x2 = jnp.split(x, 2, axis=-1)
    return jnp.concatenate([x1 * cos - x2 * sin, x1 * sin + x2 * cos], axis=-1)


def conv_mixer(x, norm_g, w_in, conv_w, w_out):
    S = x.shape[1]
    h = rms_norm(x, norm_g)
    proj = h @ w_in
    b, c, u, g = jnp.split(proj, 4, axis=-1)
    v = c * u
    vp = jnp.pad(v, ((0, 0), (CONV_WIDTH - 1, 0), (0, 0)))
    conv = conv_w[0] * vp[:, 0:S]
    for k in range(1, CONV_WIDTH):
        conv = conv + conv_w[k] * vp[:, k:k + S]
    y = jax.nn.silu(g) * b * conv
    return y @ w_out


def shared_kv(s, kv_norm, w_dkv, ckv_norm, w_ukv, cos, sin):
    Bsz, S, _ = s.shape
    h = rms_norm(s, kv_norm)
    ckr = h @ w_dkv
    c_kv, k_rope = ckr[..., :KV_RANK], ckr[..., KV_RANK:]
    c_kv = rms_norm(c_kv, ckv_norm)
    kv = (c_kv @ w_ukv).reshape(Bsz, S, N_HEADS, QK_NOPE + V_HEAD)
    k_nope, v = kv[..., :QK_NOPE], kv[..., QK_NOPE:]
    k_rope = apply_rope(k_rope, cos, sin)
    return k_nope, k_rope, v


def causal_attention(q_nope, q_rope, k_nope, k_rope, v):
    Bsz, S, H, _ = q_nope.shape
    nb = S // Q_BLOCK
    kpos = jnp.arange(S)

    def one_block(args):
        qn, qr, i = args
        sc = (jnp.einsum('bqhd,bkhd->bhqk', qn, k_nope)
              + jnp.einsum('bqhr,bkr->bhqk', qr, k_rope))
        sc = sc.astype(jnp.float32) * SOFTMAX_SCALE
        qpos = i * Q_BLOCK + jnp.arange(Q_BLOCK)
        mask = kpos[None, :] <= qpos[:, None]
        sc = jnp.where(mask[None, None], sc, -jnp.inf)
        p = jax.nn.softmax(sc, axis=-1).astype(v.dtype)
        return jnp.einsum('bhqk,bkhv->bqhv', p, v)

    qn_b = q_nope.reshape(Bsz, nb, Q_BLOCK, H, QK_NOPE).transpose(1, 0, 2, 3, 4)
    qr_b = q_rope.reshape(Bsz, nb, Q_BLOCK, H, QK_ROPE).transpose(1, 0, 2, 3, 4)
    out = lax.map(one_block, (qn_b, qr_b, jnp.arange(nb)))
    return out.transpose(1, 0, 2, 3, 4).reshape(Bsz, S, H, V_HEAD)


def mla_layer(x, norm_g, w_in, q_norm, w_uq, w_out, k_nope, k_rope, v, cos, sin):
    Bsz, S, _ = x.shape
    h = rms_norm(x, norm_g)
    proj = h @ w_in
    c_q, g = proj[..., :Q_RANK], proj[..., Q_RANK:]
    q = (rms_norm(c_q, q_norm) @ w_uq).reshape(Bsz, S, N_HEADS, QK_NOPE + QK_ROPE)
    q_nope, q_rope = q[..., :QK_NOPE], q[..., QK_NOPE:]
    q_rope = apply_rope(q_rope, cos[:, :, None, :], sin[:, :, None, :])
    o = causal_attention(q_nope, q_rope, k_nope, k_rope, v).reshape(Bsz, S, E_B)
    return (o * jax.nn.silu(g)) @ w_out


def _fwd_setup_inputs(seed: int = 0) -> dict:
    key = jax.random.key(seed)
    ks = jax.random.split(key, 16)
    f32 = jnp.float32

    def w(k, shape, fan_in):
        return jax.random.normal(k, shape, f32) * (fan_in ** -0.5)

    def gain(k, shape):
        return 1.0 + 0.1 * jax.random.normal(k, shape, f32)

    x = jax.random.normal(ks[0], (BATCH, SEQ, D_MODEL), f32)
    positions = jnp.broadcast_to(jnp.arange(SEQ, dtype=jnp.int32), (BATCH, SEQ))
    return {
        "x": x,
        "positions": positions,
        "a_norm": gain(ks[1], (N_A, D_MODEL)),
        "a_w_in": w(ks[2], (N_A, D_MODEL, 4 * E_A), D_MODEL),
        "a_conv": w(ks[3], (N_A, CONV_WIDTH, E_A), CONV_WIDTH),
        "a_w_out": w(ks[4], (N_A, E_A, D_MODEL), E_A),
        "kv_norm": gain(ks[5], (D_MODEL,)),
        "w_dkv": w(ks[6], (D_MODEL, KV_RANK + QK_ROPE), D_MODEL),
        "ckv_norm": gain(ks[7], (KV_RANK,)),
        "w_ukv": w(ks[8], (KV_RANK, N_HEADS * (QK_NOPE + V_HEAD)), KV_RANK),
        "b_norm": gain(ks[9], (N_B, D_MODEL)),
        "b_w_in": w(ks[10], (N_B, D_MODEL, Q_RANK + E_B), D_MODEL),
        "b_q_norm": gain(ks[11], (N_B, Q_RANK)),
        "b_w_uq": w(ks[12], (N_B, Q_RANK, N_HEADS * (QK_NOPE + QK_ROPE)), Q_RANK),
        "b_w_out": w(ks[13], (N_B, E_B, D_MODEL), E_B),
        "final_norm": gain(ks[14], (D_MODEL,)),
    }


def _fwd_reference(x, positions, a_norm, a_w_in, a_conv, a_w_out, kv_norm, w_dkv, ckv_norm,
              w_ukv, b_norm, b_w_in, b_q_norm, b_w_uq, b_w_out, final_norm):
    cos, sin = rope_tables(positions, x.dtype)
    k_nope = k_rope = v = None
    for layer in range(DEPTH):
        if layer < N_A:
            x = x + conv_mixer(x, a_norm[layer], a_w_in[layer], a_conv[layer], a_w_out[layer])
            if layer == N_A - 1:
                k_nope, k_rope, v = shared_kv(x, kv_norm, w_dkv, ckv_norm, w_ukv, cos, sin)
        else:
            j = layer - N_A
            x = x + mla_layer(x, b_norm[j], b_w_in[j], b_q_norm[j], b_w_uq[j], b_w_out[j],
                              k_nope, k_rope, v, cos, sin)
    return rms_norm(x, final_norm)


import jax as _jax
import jax.numpy as _jnp

TWIN_FORMAT = 'train_step'
FWD_PARAMS = ['x', 'positions', 'a_norm', 'a_w_in', 'a_conv', 'a_w_out', 'kv_norm', 'w_dkv', 'ckv_norm', 'w_ukv', 'b_norm', 'b_w_in', 'b_q_norm', 'b_w_uq', 'b_w_out', 'final_norm']
TWIN_WEIGHTS = ['a_norm', 'a_w_in', 'a_conv', 'a_w_out', 'kv_norm', 'w_dkv', 'ckv_norm', 'w_ukv', 'b_norm', 'b_w_in', 'b_q_norm', 'b_w_uq', 'b_w_out', 'final_norm']
TWIN_DIFF_INPUT = 'x'
TWIN_INPUTS = ['x', 'positions', 'a_norm', 'a_w_in', 'a_conv', 'a_w_out', 'kv_norm', 'w_dkv', 'ckv_norm', 'w_ukv', 'b_norm', 'b_w_in', 'b_q_norm', 'b_w_uq', 'b_w_out', 'final_norm', 'loss_target', 'm_a_norm', 'm_a_w_in', 'm_a_conv', 'm_a_w_out', 'm_kv_norm', 'm_w_dkv', 'm_ckv_norm', 'm_w_ukv', 'm_b_norm', 'm_b_w_in', 'm_b_q_norm', 'm_b_w_uq', 'm_b_w_out', 'm_final_norm', 'v_a_norm', 'v_a_w_in', 'v_a_conv', 'v_a_w_out', 'v_kv_norm', 'v_w_dkv', 'v_ckv_norm', 'v_w_ukv', 'v_b_norm', 'v_b_w_in', 'v_b_q_norm', 'v_b_w_uq', 'v_b_w_out', 'v_final_norm']
TWIN_OUTPUTS = ['loss', 'grad_x', 'grad_a_norm', 'grad_a_w_in', 'grad_a_conv', 'grad_a_w_out', 'grad_kv_norm', 'grad_w_dkv', 'grad_ckv_norm', 'grad_w_ukv', 'grad_b_norm', 'grad_b_w_in', 'grad_b_q_norm', 'grad_b_w_uq', 'grad_b_w_out', 'grad_final_norm', 'delta_a_norm', 'delta_a_w_in', 'delta_a_conv', 'delta_a_w_out', 'delta_kv_norm', 'delta_w_dkv', 'delta_ckv_norm', 'delta_w_ukv', 'delta_b_norm', 'delta_b_w_in', 'delta_b_q_norm', 'delta_b_w_uq', 'delta_b_w_out', 'delta_final_norm', 'new_m_a_norm', 'new_m_a_w_in', 'new_m_a_conv', 'new_m_a_w_out', 'new_m_kv_norm', 'new_m_w_dkv', 'new_m_ckv_norm', 'new_m_w_ukv', 'new_m_b_norm', 'new_m_b_w_in', 'new_m_b_q_norm', 'new_m_b_w_uq', 'new_m_b_w_out', 'new_m_final_norm', 'new_v_a_norm', 'new_v_a_w_in', 'new_v_a_conv', 'new_v_a_w_out', 'new_v_kv_norm', 'new_v_w_dkv', 'new_v_ckv_norm', 'new_v_w_ukv', 'new_v_b_norm', 'new_v_b_w_in', 'new_v_b_q_norm', 'new_v_b_w_uq', 'new_v_b_w_out', 'new_v_final_norm']
TWIN_LEAF_KINDS = {'loss': 'loss', 'grad_x': 'grad_x', 'grad_a_norm': 'grad_w', 'grad_a_w_in': 'grad_w', 'grad_a_conv': 'grad_w', 'grad_a_w_out': 'grad_w', 'grad_kv_norm': 'grad_w', 'grad_w_dkv': 'grad_w', 'grad_ckv_norm': 'grad_w', 'grad_w_ukv': 'grad_w', 'grad_b_norm': 'grad_w', 'grad_b_w_in': 'grad_w', 'grad_b_q_norm': 'grad_w', 'grad_b_w_uq': 'grad_w', 'grad_b_w_out': 'grad_w', 'grad_final_norm': 'grad_w', 'delta_a_norm': 'delta_w', 'delta_a_w_in': 'delta_w', 'delta_a_conv': 'delta_w', 'delta_a_w_out': 'delta_w', 'delta_kv_norm': 'delta_w', 'delta_w_dkv': 'delta_w', 'delta_ckv_norm': 'delta_w', 'delta_w_ukv': 'delta_w', 'delta_b_norm': 'delta_w', 'delta_b_w_in': 'delta_w', 'delta_b_q_norm': 'delta_w', 'delta_b_w_uq': 'delta_w', 'delta_b_w_out': 'delta_w', 'delta_final_norm': 'delta_w', 'new_m_a_norm': 'new_m', 'new_m_a_w_in': 'new_m', 'new_m_a_conv': 'new_m', 'new_m_a_w_out': 'new_m', 'new_m_kv_norm': 'new_m', 'new_m_w_dkv': 'new_m', 'new_m_ckv_norm': 'new_m', 'new_m_w_ukv': 'new_m', 'new_m_b_norm': 'new_m', 'new_m_b_w_in': 'new_m', 'new_m_b_q_norm': 'new_m', 'new_m_b_w_uq': 'new_m', 'new_m_b_w_out': 'new_m', 'new_m_final_norm': 'new_m', 'new_v_a_norm': 'new_v', 'new_v_a_w_in': 'new_v', 'new_v_a_conv': 'new_v', 'new_v_a_w_out': 'new_v', 'new_v_kv_norm': 'new_v', 'new_v_w_dkv': 'new_v', 'new_v_ckv_norm': 'new_v', 'new_v_w_ukv': 'new_v', 'new_v_b_norm': 'new_v', 'new_v_b_w_in': 'new_v', 'new_v_b_q_norm': 'new_v', 'new_v_b_w_uq': 'new_v', 'new_v_b_w_out': 'new_v', 'new_v_final_norm': 'new_v'}


def _forward(args):
    return _fwd_reference(*[args[k] for k in FWD_PARAMS])


def _output_shape():
    out = _jax.eval_shape(lambda: _forward(_fwd_setup_inputs(0)))
    return out.shape, out.dtype

N_MICROBATCH = 1
ADAM_LR = 0.001
ADAM_B1 = 0.9
ADAM_B2 = 0.999
ADAM_EPS = 1e-08
ADAM_WD = 0.01
ADAM_STEP = 10
PER_EXAMPLE_BATCH_AXIS = {'x': 0, 'positions': 0, 'loss_target': 0}
SHARED_INPUTS = []
_WEIGHT_DTYPES = {'a_norm': _jnp.float32, 'a_w_in': _jnp.float32, 'a_conv': _jnp.float32, 'a_w_out': _jnp.float32, 'kv_norm': _jnp.float32, 'w_dkv': _jnp.float32, 'ckv_norm': _jnp.float32, 'w_ukv': _jnp.float32, 'b_norm': _jnp.float32, 'b_w_in': _jnp.float32, 'b_q_norm': _jnp.float32, 'b_w_uq': _jnp.float32, 'b_w_out': _jnp.float32, 'final_norm': _jnp.float32}
MOMENT_SCALE = {'a_norm': 2.686845e-01, 'a_w_in': 1.318099e-01, 'a_conv': 1.341426e-01, 'a_w_out': 1.370906e-01, 'kv_norm': 3.832719e-02, 'w_dkv': 6.942250e-02, 'ckv_norm': 7.248649e-02, 'w_ukv': 3.548447e-02, 'b_norm': 3.623126e-02, 'b_w_in': 4.061323e-02, 'b_q_norm': 3.795120e-02, 'b_w_uq': 2.779597e-02, 'b_w_out': 2.837796e-02, 'final_norm': 6.424022e+01}


def _to_microbatches(a, axis):
    t = _jnp.moveaxis(a, axis, 0)
    t = t.reshape((N_MICROBATCH, t.shape[0] // N_MICROBATCH) + t.shape[1:])
    return _jnp.moveaxis(t, 1, axis + 1)


def setup_inputs(seed: int = 0) -> dict:
    inp = _fwd_setup_inputs(seed)
    key = _jax.random.fold_in(_jax.random.key(seed), 7919)
    shape, _ = _output_shape()
    out = dict(inp)
    out["loss_target"] = _jax.random.normal(_jax.random.fold_in(key, 0), shape, _jnp.float32)
    for i, name in enumerate(TWIN_WEIGHTS):
        w = inp[name].astype(_jnp.float32)
        if MOMENT_SCALE is None:
            s = _jnp.sqrt(_jnp.mean(_jnp.square(w)) + 1e-30)
        else:
            s = MOMENT_SCALE[name]
        km, kv = _jax.random.split(_jax.random.fold_in(key, i + 1))
        out[name] = w
        out["m_" + name] = s * _jax.random.normal(km, w.shape, _jnp.float32)
        out["v_" + name] = (s * s) * _jax.random.uniform(kv, w.shape, _jnp.float32, 0.5, 1.5)
    if N_MICROBATCH > 1:
        for name, axis in PER_EXAMPLE_BATCH_AXIS.items():
            out[name] = _to_microbatches(out[name], axis)
    return {'x': out['x'], 'positions': out['positions'], 'a_norm': out['a_norm'], 'a_w_in': out['a_w_in'], 'a_conv': out['a_conv'], 'a_w_out': out['a_w_out'], 'kv_norm': out['kv_norm'], 'w_dkv': out['w_dkv'], 'ckv_norm': out['ckv_norm'], 'w_ukv': out['w_ukv'], 'b_norm': out['b_norm'], 'b_w_in': out['b_w_in'], 'b_q_norm': out['b_q_norm'], 'b_w_uq': out['b_w_uq'], 'b_w_out': out['b_w_out'], 'final_norm': out['final_norm'], 'loss_target': out['loss_target'], 'm_a_norm': out['m_a_norm'], 'm_a_w_in': out['m_a_w_in'], 'm_a_conv': out['m_a_conv'], 'm_a_w_out': out['m_a_w_out'], 'm_kv_norm': out['m_kv_norm'], 'm_w_dkv': out['m_w_dkv'], 'm_ckv_norm': out['m_ckv_norm'], 'm_w_ukv': out['m_w_ukv'], 'm_b_norm': out['m_b_norm'], 'm_b_w_in': out['m_b_w_in'], 'm_b_q_norm': out['m_b_q_norm'], 'm_b_w_uq': out['m_b_w_uq'], 'm_b_w_out': out['m_b_w_out'], 'm_final_norm': out['m_final_norm'], 'v_a_norm': out['v_a_norm'], 'v_a_w_in': out['v_a_w_in'], 'v_a_conv': out['v_a_conv'], 'v_a_w_out': out['v_a_w_out'], 'v_kv_norm': out['v_kv_norm'], 'v_w_dkv': out['v_w_dkv'], 'v_ckv_norm': out['v_ckv_norm'], 'v_w_ukv': out['v_w_ukv'], 'v_b_norm': out['v_b_norm'], 'v_b_w_in': out['v_b_w_in'], 'v_b_q_norm': out['v_b_q_norm'], 'v_b_w_uq': out['v_b_w_uq'], 'v_b_w_out': out['v_b_w_out'], 'v_final_norm': out['v_final_norm']}


def _loss(weights, diff, rest, loss_target):
    with _jax.named_scope("forward"):
        args = {**rest, TWIN_DIFF_INPUT: diff, **{k: w.astype(_WEIGHT_DTYPES[k]) for k, w in weights.items()}}
        y = _forward(args)
    with _jax.named_scope("loss_head"):
        err = _jnp.square(y.astype(_jnp.float32) - loss_target)
        return 0.5 * _jnp.sum(_jnp.mean(err, axis=-1)) if err.ndim else 0.5 * err


def _adamw(w, g, m, v):
    m = ADAM_B1 * m + (1.0 - ADAM_B1) * g
    v = ADAM_B2 * v + (1.0 - ADAM_B2) * _jnp.square(g)
    m_hat = m / (1.0 - ADAM_B1 ** ADAM_STEP)
    v_hat = v / (1.0 - ADAM_B2 ** ADAM_STEP)
    delta = -ADAM_LR * (m_hat / (_jnp.sqrt(v_hat) + ADAM_EPS) + ADAM_WD * w)
    return delta, m, v


def reference(x, positions, a_norm, a_w_in, a_conv, a_w_out, kv_norm, w_dkv, ckv_norm, w_ukv, b_norm, b_w_in, b_q_norm, b_w_uq, b_w_out, final_norm, loss_target, m_a_norm, m_a_w_in, m_a_conv, m_a_w_out, m_kv_norm, m_w_dkv, m_ckv_norm, m_w_ukv, m_b_norm, m_b_w_in, m_b_q_norm, m_b_w_uq, m_b_w_out, m_final_norm, v_a_norm, v_a_w_in, v_a_conv, v_a_w_out, v_kv_norm, v_w_dkv, v_ckv_norm, v_w_ukv, v_b_norm, v_b_w_in, v_b_q_norm, v_b_w_uq, v_b_w_out, v_final_norm):
    given = dict(x=x, positions=positions, a_norm=a_norm, a_w_in=a_w_in, a_conv=a_conv, a_w_out=a_w_out, kv_norm=kv_norm, w_dkv=w_dkv, ckv_norm=ckv_norm, w_ukv=w_ukv, b_norm=b_norm, b_w_in=b_w_in, b_q_norm=b_q_norm, b_w_uq=b_w_uq, b_w_out=b_w_out, final_norm=final_norm, loss_target=loss_target, m_a_norm=m_a_norm, m_a_w_in=m_a_w_in, m_a_conv=m_a_conv, m_a_w_out=m_a_w_out, m_kv_norm=m_kv_norm, m_w_dkv=m_w_dkv, m_ckv_norm=m_ckv_norm, m_w_ukv=m_w_ukv, m_b_norm=m_b_norm, m_b_w_in=m_b_w_in, m_b_q_norm=m_b_q_norm, m_b_w_uq=m_b_w_uq, m_b_w_out=m_b_w_out, m_final_norm=m_final_norm, v_a_norm=v_a_norm, v_a_w_in=v_a_w_in, v_a_conv=v_a_conv, v_a_w_out=v_a_w_out, v_kv_norm=v_kv_norm, v_w_dkv=v_w_dkv, v_ckv_norm=v_ckv_norm, v_w_ukv=v_w_ukv, v_b_norm=v_b_norm, v_b_w_in=v_b_w_in, v_b_q_norm=v_b_q_norm, v_b_w_uq=v_b_w_uq, v_b_w_out=v_b_w_out, v_final_norm=v_final_norm)
    weights = {n: given[n] for n in TWIN_WEIGHTS}
    shared = {n: given[n] for n in SHARED_INPUTS}
    per_example = {n: given[n] for n in ['x', 'positions']}
    grad_fn = _jax.value_and_grad(_loss, argnums=(0, 1))

    def one_microbatch(ex, loss_target):
        ex = dict(ex)
        diff = ex.pop(TWIN_DIFF_INPUT)
        return grad_fn(weights, diff, {**shared, **ex}, loss_target)

    if N_MICROBATCH == 1:
        loss, (grad_w, grad_x) = one_microbatch(per_example, given["loss_target"])
    else:
        def body(carry, xs):
            loss_sum, grad_sum = carry
            l_k, (gw_k, gx_k) = one_microbatch(xs[0], xs[1])
            with _jax.named_scope("update"):
                return (loss_sum + l_k, _jax.tree.map(_jnp.add, grad_sum, gw_k)), gx_k

        init = (_jnp.zeros((), _jnp.float32), _jax.tree.map(_jnp.zeros_like, weights))
        (loss, grad_w), grad_x = _jax.lax.scan(body, init, (per_example, given["loss_target"]))
    with _jax.named_scope("update"):
        delta_w, new_m, new_v = {}, {}, {}
        for n in TWIN_WEIGHTS:
            delta_w[n], new_m[n], new_v[n] = _adamw(weights[n], grad_w[n], given["m_" + n], given["v_" + n])
    return (loss, grad_x, *[grad_w[n] for n in TWIN_WEIGHTS], *[delta_w[n] for n in TWIN_WEIGHTS],
            *[new_m[n] for n in TWIN_WEIGHTS], *[new_v[n] for n in TWIN_WEIGHTS])
```

```python
import functools
import math

import numpy as np
import jax
import jax.numpy as jnp
from jax import lax
from jax.experimental import pallas as pl
from jax.experimental.pallas import tpu as pltpu

F32 = jnp.float32
BF16 = jnp.bfloat16

D_MODEL = 1024
N_HEADS = 8
QK_NOPE = 64
QK_ROPE = 32
V_HEAD = 64
KV_RANK = 256
Q_RANK = 384
E_B = N_HEADS * V_HEAD
HEAD_PAD = 128
CKR_PAD = KV_RANK + HEAD_PAD
ROPE_LO = QK_NOPE
ROPE_HALF = QK_ROPE // 2
ROPE_THETA = 10000.0
SOFTMAX_SCALE = 1.0 / math.sqrt(QK_NOPE + QK_ROPE)
EPS = 1e-6
N_DEV = 8

ADAM_LR = 0.001
ADAM_B1 = 0.9
ADAM_B2 = 0.999
ADAM_EPS = 1e-08
ADAM_WD = 0.01
ADAM_STEP = 10

ROW_TILE = 256
ATTN_TILE = 256
VMEM_LIMIT = 56 * 1024 * 1024

MESH = pl.DeviceIdType.MESH
NT = (((1,), (1,)), ((), ()))
TN = (((0,), (0,)), ((), ()))


def _dot(a, b):
    return jnp.dot(a.astype(BF16), b.astype(BF16), preferred_element_type=F32)


def _dot_nt(a, b):
    return lax.dot_general(a.astype(BF16), b.astype(BF16), NT, preferred_element_type=F32)


def _dot_tn(a, b):
    return lax.dot_general(a.astype(BF16), b.astype(BF16), TN, preferred_element_type=F32)


def _rstd(x):
    return lax.rsqrt(jnp.mean(x * x, axis=-1, keepdims=True) + EPS)


def _norm_bwd(a, xh, r):
    return r * (a - xh * jnp.mean(a * xh, axis=-1, keepdims=True))


def _silu_parts(g):
    sg = jax.nn.sigmoid(g)
    return g * sg, sg * (1.0 + g * (1.0 - sg))


def _rope_consts():
    inv = (ROPE_THETA ** (-np.arange(0, QK_ROPE, 2, dtype=np.float32) / QK_ROPE)).astype(np.float32)
    t = np.zeros((8, HEAD_PAD), np.float32)
    t[0, ROPE_LO:ROPE_LO + ROPE_HALF] = inv
    t[0, ROPE_LO + ROPE_HALF:ROPE_LO + QK_ROPE] = inv
    t[1, ROPE_LO:ROPE_LO + ROPE_HALF] = -1.0
    t[2, ROPE_LO + ROPE_HALF:ROPE_LO + QK_ROPE] = 1.0
    return jnp.asarray(t)


def _rope_tables(pos, rc):
    ang = pos * rc[0:1, :]
    cosv = jnp.cos(ang)
    sinv = jnp.sin(ang)
    return cosv, sinv * rc[1:2, :], sinv * rc[2:3, :]


def _rope(x, ct, s1, s2):
    up = pltpu.roll(x, HEAD_PAD - ROPE_HALF, 1)
    dn = pltpu.roll(x, ROPE_HALF, 1)
    return x * ct + up * s1 + dn * s2


def _const_spec(shape):
    nd = len(shape)
    return pl.BlockSpec(shape, lambda *_: (0,) * nd, pipeline_mode=pl.Buffered(1))


def _acc_spec(shape):
    nd = len(shape)
    return pl.BlockSpec(shape, lambda *_: (0,) * nd)


def _params(n_axes):
    return pltpu.CompilerParams(dimension_semantics=("arbitrary",) * n_axes, vmem_limit_bytes=VMEM_LIMIT)


def _place():
    x, y, c = lax.axis_index("x"), lax.axis_index("y"), lax.axis_index("c")
    return x, y, c, 4 * x + 2 * y + c


def _peer(x, y, c, mask):
    px = 1 - x if mask & 4 else x
    py = 1 - y if mask & 2 else y
    pc = 1 - c if mask & 1 else c
    return (px, py, pc), 4 * px + 2 * py + pc


def _all_gather(shards):
    n = len(shards)

    def body(*refs):
        ins, outs = refs[:n], refs[n:2 * n]
        send_sems, recv_sems, local_sems = refs[2 * n:]
        x, y, c, me = _place()
        copies = []
        for w in range(n):
            mine = pltpu.make_async_copy(ins[w], outs[w].at[me], local_sems.at[w])
            mine.start()
            copies.append(mine)
            for mask in range(1, N_DEV):
                peer, peer_idx = _peer(x, y, c, mask)
                send = pltpu.make_async_remote_copy(
                    src_ref=ins[w], dst_ref=outs[w].at[me], send_sem=send_sems.at[w, mask - 1],
                    recv_sem=recv_sems.at[w, mask - 1], device_id=peer, device_id_type=MESH)
                send.start()
                recv = pltpu.make_async_remote_copy(
                    src_ref=ins[w], dst_ref=outs[w].at[peer_idx], send_sem=send_sems.at[w, mask - 1],
                    recv_sem=recv_sems.at[w, mask - 1], device_id=peer, device_id_type=MESH)
                copies.append(recv)
        for cp in copies:
            cp.wait()

    any_spec = pl.BlockSpec(memory_space=pl.ANY)
    return pl.pallas_call(
        body, name="weight_all_gather",
        out_shape=[jax.ShapeDtypeStruct((N_DEV,) + s.shape, s.dtype) for s in shards],
        in_specs=[any_spec] * n, out_specs=[any_spec] * n,
        scratch_shapes=[pltpu.SemaphoreType.DMA((n, N_DEV - 1)), pltpu.SemaphoreType.DMA((n, N_DEV - 1)),
                        pltpu.SemaphoreType.DMA((n,))],
    )(*shards)


def _grad_exchange(stacks, small):
    n = len(stacks)

    def body(*refs):
        ins, small_in = refs[:n], refs[n]
        outs, small_out = refs[n + 1:2 * n + 1], refs[2 * n + 1]
        send_sems, recv_sems, local_sems = refs[2 * n + 2:]
        x, y, c, me = _place()
        copies = []
        for w in range(n + 1):
            if w < n:
                mine = pltpu.make_async_copy(ins[w].at[me], outs[w].at[0], local_sems.at[w])
            else:
                mine = pltpu.make_async_copy(small_in, small_out.at[me], local_sems.at[w])
            mine.start()
            copies.append(mine)
            for mask in range(1, N_DEV):
                peer, peer_idx = _peer(x, y, c, mask)
                if w < n:
                    src, dst, landed = ins[w].at[peer_idx], outs[w].at[mask], outs[w].at[mask]
                else:
                    src, dst, landed = small_in, small_out.at[me], small_out.at[peer_idx]
                send = pltpu.make_async_remote_copy(
                    src_ref=src, dst_ref=dst, send_sem=send_sems.at[w, mask - 1],
                    recv_sem=recv_sems.at[w, mask - 1], device_id=peer, device_id_type=MESH)
                send.start()
                recv = pltpu.make_async_remote_copy(
                    src_ref=src, dst_ref=landed, send_sem=send_sems.at[w, mask - 1],
                    recv_sem=recv_sems.at[w, mask - 1], device_id=peer, device_id_type=MESH)
                copies.append(recv)
        for cp in copies:
            cp.wait()

    any_spec = pl.BlockSpec(memory_space=pl.ANY)
    outs = pl.pallas_call(
        body, name="grad_exchange",
        out_shape=[jax.ShapeDtypeStruct(s.shape, s.dtype) for s in stacks]
        + [jax.ShapeDtypeStruct((N_DEV,) + small.shape, small.dtype)],
        in_specs=[any_spec] * (n + 1), out_specs=[any_spec] * (n + 1),
        scratch_shapes=[pltpu.SemaphoreType.DMA((n + 1, N_DEV - 1)), pltpu.SemaphoreType.DMA((n + 1, N_DEV - 1)),
                        pltpu.SemaphoreType.DMA((n + 1,))],
    )(*stacks, small)
    return outs[:n], outs[n]


def _conv_fwd(x, a_norm, w_in, conv_w, w_out, n_seq, seq):
    tm = ROW_TILE
    nt = seq // tm
    n_col = w_in.shape[2]

    def body(x_ref, an_ref, win_ref, cw_ref, wout_ref, x1_ref, proj_ref, conv_ref, prev_ref):
        @pl.when(pl.program_id(1) == 0)
        def _():
            prev_ref[...] = jnp.zeros_like(prev_ref)

        xv = x_ref[...]
        h = (xv * _rstd(xv) * an_ref[...]).astype(BF16)
        for d in range(N_DEV):
            proj_ref[:, d * n_col:(d + 1) * n_col] = jnp.dot(h, win_ref[d], preferred_element_type=F32)
        b = proj_ref[:, 0:D_MODEL]
        v = proj_ref[:, D_MODEL:2 * D_MODEL] * proj_ref[:, 2 * D_MODEL:3 * D_MODEL]
        g = proj_ref[:, 3 * D_MODEL:4 * D_MODEL]
        w0, w1, w2 = cw_ref[0:1, :], cw_ref[1:2, :], cw_ref[2:3, :]
        conv_ref[...] = w0 * pltpu.roll(v, 2, 0) + w1 * pltpu.roll(v, 1, 0) + w2 * v
        rows = lax.broadcasted_iota(jnp.int32, (8, D_MODEL), 0)
        p8, v8 = prev_ref[...], v[0:8]
        back1 = jnp.where(rows < 1, pltpu.roll(p8, 1, 0), pltpu.roll(v8, 1, 0))
        back2 = jnp.where(rows < 2, pltpu.roll(p8, 2, 0), pltpu.roll(v8, 2, 0))
        conv_ref[0:8, :] = w0 * back2 + w1 * back1 + w2 * v8
        prev_ref[...] = v[tm - 8:tm]
        silu, _ = _silu_parts(g)
        yv = silu * b * conv_ref[...]
        x1_ref[...] = xv + _dot(yv, wout_ref[...])

    tok = lambda width: pl.BlockSpec((tm, width), lambda s, i: (s * nt + i, 0))
    t = n_seq * seq
    return pl.pallas_call(
        body, name="conv_mixer_fwd", grid=(n_seq, nt),
        in_specs=[tok(D_MODEL), _const_spec((1, D_MODEL)), _const_spec(w_in.shape), _const_spec((3, D_MODEL)),
                  _const_spec(w_out.shape)],
        out_specs=[tok(D_MODEL), tok(4 * D_MODEL), tok(D_MODEL)],
        out_shape=[jax.ShapeDtypeStruct((t, D_MODEL), F32), jax.ShapeDtypeStruct((t, 4 * D_MODEL), F32),
                   jax.ShapeDtypeStruct((t, D_MODEL), F32)],
        scratch_shapes=[pltpu.VMEM((8, D_MODEL), F32)],
        compiler_params=_params(2),
    )(x, a_norm, w_in, conv_w, w_out)


def _conv_bwd(dx1, x, proj, conv, a_norm, w_in, conv_w, w_out, n_seq, seq):
    tm = ROW_TILE
    nt = seq // tm
    n_col = w_in.shape[2]

    def body(dx1_ref, x_ref, proj_ref, conv_ref, an_ref, win_ref, cw_ref, wout_ref,
             dx_ref, h_ref, dproj_ref, dwout_ref, dan_ref, dcw_ref, next_ref, d1_ref, d2_ref):
        first = jnp.logical_and(pl.program_id(0) == 0, pl.program_id(1) == 0)

        @pl.when(first)
        def _():
            dwout_ref[...] = jnp.zeros_like(dwout_ref)
            dan_ref[...] = jnp.zeros_like(dan_ref)
            dcw_ref[...] = jnp.zeros_like(dcw_ref)

        @pl.when(pl.program_id(1) == 0)
        def _():
            next_ref[...] = jnp.zeros_like(next_ref)

        dx1v = dx1_ref[...]
        dy = _dot_nt(dx1v, wout_ref[...])
        b = proj_ref[:, 0:D_MODEL]
        cc = proj_ref[:, D_MODEL:2 * D_MODEL]
        u = proj_ref[:, 2 * D_MODEL:3 * D_MODEL]
        g = proj_ref[:, 3 * D_MODEL:4 * D_MODEL]
        cv = conv_ref[...]
        silu, dsilu = _silu_parts(g)
        dwout_ref[...] += _dot_tn(silu * b * cv, dx1v)
        dproj_ref[:, 3 * D_MODEL:4 * D_MODEL] = (dy * b * cv * dsilu).astype(BF16)
        dproj_ref[:, 0:D_MODEL] = (dy * silu * cv).astype(BF16)
        dconv = dy * silu * b
        d1_ref[...] = pltpu.roll(dconv, tm - 1, 0)
        d2_ref[...] = pltpu.roll(dconv, tm - 2, 0)
        rows = lax.broadcasted_iota(jnp.int32, (8, D_MODEL), 0)
        n8, c8 = next_ref[...], dconv[tm - 8:tm]
        d1_ref[tm - 8:tm, :] = jnp.where(rows >= 7, pltpu.roll(n8, 7, 0), pltpu.roll(c8, 7, 0))
        d2_ref[tm - 8:tm, :] = jnp.where(rows >= 6, pltpu.roll(n8, 6, 0), pltpu.roll(c8, 6, 0))
        next_ref[...] = dconv[0:8]
        d1, d2 = d1_ref[...], d2_ref[...]
        v = cc * u
        dcw_ref[0:1, :] += jnp.sum(d2 * v, axis=0, keepdims=True)
        dcw_ref[1:2, :] += jnp.sum(d1 * v, axis=0, keepdims=True)
        dcw_ref[2:3, :] += jnp.sum(dconv * v, axis=0, keepdims=True)
        dv = cw_ref[0:1, :] * d2 + cw_ref[1:2, :] * d1 + cw_ref[2:3, :] * dconv
        dproj_ref[:, D_MODEL:2 * D_MODEL] = (dv * u).astype(BF16)
        dproj_ref[:, 2 * D_MODEL:3 * D_MODEL] = (dv * cc).astype(BF16)
        dh = jnp.zeros((tm, D_MODEL), F32)
        for d in range(N_DEV):
            dh += lax.dot_general(dproj_ref[:, d * n_col:(d + 1) * n_col], win_ref[d], NT, preferred_element_type=F32)
        xv = x_ref[...]
        r = _rstd(xv)
        xh = xv * r
        h_ref[...] = (xh * an_ref[...]).astype(BF16)
        dan_ref[...] += jnp.sum(dh * xh, axis=0, keepdims=True)
        dx_ref[...] = dx1v + _norm_bwd(dh * an_ref[...], xh, r)

    tok = lambda width: pl.BlockSpec((tm, width), lambda s, i: (s * nt + nt - 1 - i, 0))
    t = n_seq * seq
    return pl.pallas_call(
        body, name="conv_mixer_bwd", grid=(n_seq, nt),
        in_specs=[tok(D_MODEL), tok(D_MODEL), tok(4 * D_MODEL), tok(D_MODEL), _const_spec((1, D_MODEL)),
                  _const_spec(w_in.shape), _const_spec((3, D_MODEL)), _const_spec(w_out.shape)],
        out_specs=[tok(D_MODEL), tok(D_MODEL), tok(4 * D_MODEL), _acc_spec((D_MODEL, D_MODEL)),
                   _acc_spec((1, D_MODEL)), _acc_spec((8, D_MODEL))],
        out_shape=[jax.ShapeDtypeStruct((t, D_MODEL), F32), jax.ShapeDtypeStruct((t, D_MODEL), BF16),
                   jax.ShapeDtypeStruct((t, 4 * D_MODEL), BF16), jax.ShapeDtypeStruct((D_MODEL, D_MODEL), F32),
                   jax.ShapeDtypeStruct((1, D_MODEL), F32), jax.ShapeDtypeStruct((8, D_MODEL), F32)],
        scratch_shapes=[pltpu.VMEM((8, D_MODEL), F32), pltpu.VMEM((tm, D_MODEL), F32), pltpu.VMEM((tm, D_MODEL), F32)],
        compiler_params=_params(2),
    )(dx1, x, proj, conv, a_norm, w_in, conv_w, w_out)


def _stacked_weight_grad(a, b, n_col):
    t, r = a.shape
    bt = 512

    def body(a_ref, b_ref, o_ref):
        @pl.when(pl.program_id(1) == 0)
        def _():
            o_ref[...] = jnp.zeros_like(o_ref)

        o_ref[...] += lax.dot_general(a_ref[...], b_ref[...], TN, preferred_element_type=F32)

    return pl.pallas_call(
        body, name="stacked_weight_grad", grid=(N_DEV, t // bt),
        in_specs=[pl.BlockSpec((bt, r), lambda d, k: (k, 0)), pl.BlockSpec((bt, n_col), lambda d, k: (k, d))],
        out_specs=pl.BlockSpec((None, r, n_col), lambda d, k: (d, 0, 0)),
        out_shape=jax.ShapeDtypeStruct((N_DEV, r, n_col), F32),
        compiler_params=_params(2),
    )(a, b)


def _proj_fwd(x1, pos, rc, kv_norm, w_dkv, ckv_norm, w_ukv, b_norm, b_w_in, q_norm, w_uq):
    t = x1.shape[0]
    tm = ROW_TILE

    def body(x1_ref, pos_ref, rc_ref, kvn_ref, wdkv_ref, ckvn_ref, wukv_ref, bn_ref, bwin_ref, qn_ref, wuq_ref,
             ckr_ref, cq_ref, gb_ref, q_ref, k_ref, v_ref):
        ct, s1, s2 = _rope_tables(pos_ref[...], rc_ref[...])
        xv = x1_ref[...]
        xh = xv * _rstd(xv)
        ckr = _dot(xh * kvn_ref[...], wdkv_ref[...])
        ckr_ref[...] = ckr
        ckv = ckr[:, 0:KV_RANK]
        kv = _dot(ckv * _rstd(ckv) * ckvn_ref[...], wukv_ref[...])
        k_rope = _rope(ckr[:, KV_RANK:CKR_PAD], ct, s1, s2)
        lane = lax.broadcasted_iota(jnp.int32, (tm, HEAD_PAD), 1)
        low = lane < QK_NOPE
        for h in range(N_HEADS):
            k_ref[:, h * HEAD_PAD:(h + 1) * HEAD_PAD] = jnp.where(
                low, kv[:, h * HEAD_PAD:(h + 1) * HEAD_PAD], k_rope).astype(BF16)
        for p in range(N_HEADS // 2):
            even = kv[:, (2 * p) * HEAD_PAD:(2 * p + 1) * HEAD_PAD]
            odd = kv[:, (2 * p + 1) * HEAD_PAD:(2 * p + 2) * HEAD_PAD]
            v_ref[:, p * HEAD_PAD:(p + 1) * HEAD_PAD] = jnp.where(low, pltpu.roll(even, V_HEAD, 1), odd).astype(BF16)
        pb = _dot(xh * bn_ref[...], bwin_ref[...])
        cq = pb[:, 0:Q_RANK]
        cq_ref[...] = cq
        gb_ref[...] = pb[:, Q_RANK:Q_RANK + E_B]
        q = _dot(cq * _rstd(cq) * qn_ref[...], wuq_ref[...])
        for h in range(N_HEADS):
            q_ref[:, h * HEAD_PAD:(h + 1) * HEAD_PAD] = _rope(
                q[:, h * HEAD_PAD:(h + 1) * HEAD_PAD], ct, s1, s2).astype(BF16)

    tok = lambda width: pl.BlockSpec((tm, width), lambda i: (i, 0))
    weights = [kv_norm, w_dkv, ckv_norm, w_ukv, b_norm, b_w_in, q_norm, w_uq]
    wide = N_HEADS * HEAD_PAD
    return pl.pallas_call(
        body, name="latent_proj_fwd", grid=(t // tm,),
        in_specs=[tok(D_MODEL), tok(1), _const_spec(rc.shape)] + [_const_spec(w.shape) for w in weights],
        out_specs=[tok(CKR_PAD), tok(Q_RANK), tok(E_B), tok(wide), tok(wide), tok(E_B)],
        out_shape=[jax.ShapeDtypeStruct((t, CKR_PAD), F32), jax.ShapeDtypeStruct((t, Q_RANK), F32),
                   jax.ShapeDtypeStruct((t, E_B), F32), jax.ShapeDtypeStruct((t, wide), BF16),
                   jax.ShapeDtypeStruct((t, wide), BF16), jax.ShapeDtypeStruct((t, E_B), BF16)],
        compiler_params=_params(1),
    )(x1, pos, rc, *weights)


def _proj_bwd(dq, dk, dv, dgb, cq, ckr, x1, dx2, pos, rc, kv_norm, w_dkv, ckv_norm, w_ukv, b_norm, b_w_in, q_norm, w_uq):
    t = x1.shape[0]
    tm = ROW_TILE
    wide = N_HEADS * HEAD_PAD

    def body(dq_ref, dk_ref, dv_ref, dgb_ref, cq_ref, ckr_ref, x1_ref, dx2_ref, pos_ref, rc_ref,
             kvn_ref, wdkv_ref, ckvn_ref, wukv_ref, bn_ref, bwin_ref, qn_ref, wuq_ref,
             dx1_ref, dwuq_ref, dwbin_ref, dwukv_ref, dwdkv_ref, dqn_ref, dbn_ref, dckvn_ref, dkvn_ref, dqu_ref, dkv_ref):
        @pl.when(pl.program_id(0) == 0)
        def _():
            for ref in (dwuq_ref, dwbin_ref, dwukv_ref, dwdkv_ref, dqn_ref, dbn_ref, dckvn_ref, dkvn_ref):
                ref[...] = jnp.zeros_like(ref)

        ct, s1, s2 = _rope_tables(pos_ref[...], rc_ref[...])
        lane = lax.broadcasted_iota(jnp.int32, (tm, HEAD_PAD), 1)
        low = lane < QK_NOPE
        for h in range(N_HEADS):
            dqu_ref[:, h * HEAD_PAD:(h + 1) * HEAD_PAD] = _rope(
                dq_ref[:, h * HEAD_PAD:(h + 1) * HEAD_PAD], ct, -s1, -s2).astype(BF16)
        cq = cq_ref[...]
        rq = _rstd(cq)
        cqh = cq * rq
        cqn = (cqh * qn_ref[...]).astype(BF16)
        for h in range(N_HEADS):
            dwuq_ref[h] += lax.dot_general(cqn, dqu_ref[:, h * HEAD_PAD:(h + 1) * HEAD_PAD], TN, preferred_element_type=F32)
        dcqn = lax.dot_general(dqu_ref[...], wuq_ref[...], NT, preferred_element_type=F32)
        dqn_ref[...] += jnp.sum(dcqn * cqh, axis=0, keepdims=True)
        dcq = _norm_bwd(dcqn * qn_ref[...], cqh, rq)
        dpb = jnp.concatenate([dcq, dgb_ref[...]], axis=1).astype(BF16)
        xv = x1_ref[...]
        r = _rstd(xv)
        xh = xv * r
        dwbin_ref[...] += _dot_tn(xh * bn_ref[...], dpb)
        dh3 = lax.dot_general(dpb, bwin_ref[...], NT, preferred_element_type=F32)
        dk_rope = jnp.zeros((tm, HEAD_PAD), F32)
        for p in range(N_HEADS // 2):
            dvp = dv_ref[:, p * HEAD_PAD:(p + 1) * HEAD_PAD]
            dk_even = dk_ref[:, (2 * p) * HEAD_PAD:(2 * p + 1) * HEAD_PAD]
            dk_odd = dk_ref[:, (2 * p + 1) * HEAD_PAD:(2 * p + 2) * HEAD_PAD]
            dkv_ref[:, (2 * p) * HEAD_PAD:(2 * p + 1) * HEAD_PAD] = jnp.where(low, dk_even, pltpu.roll(dvp, V_HEAD, 1)).astype(BF16)
            dkv_ref[:, (2 * p + 1) * HEAD_PAD:(2 * p + 2) * HEAD_PAD] = jnp.where(low, dk_odd, dvp).astype(BF16)
            dk_rope += dk_even + dk_odd
        rope_lanes = jnp.logical_and(lane >= ROPE_LO, lane < ROPE_LO + QK_ROPE)
        dk_rope = jnp.where(rope_lanes, _rope(dk_rope, ct, -s1, -s2), 0.0)
        ckv = ckr_ref[:, 0:KV_RANK]
        rk = _rstd(ckv)
        ckh = ckv * rk
        ckvn = (ckh * ckvn_ref[...]).astype(BF16)
        for h in range(N_HEADS):
            dwukv_ref[h] += lax.dot_general(ckvn, dkv_ref[:, h * HEAD_PAD:(h + 1) * HEAD_PAD], TN, preferred_element_type=F32)
        dckvn = lax.dot_general(dkv_ref[...], wukv_ref[...], NT, preferred_element_type=F32)
        dckvn_ref[...] += jnp.sum(dckvn * ckh, axis=0, keepdims=True)
        dckv = _norm_bwd(dckvn * ckvn_ref[...], ckh, rk)
        dckr = jnp.concatenate([dckv, dk_rope], axis=1).astype(BF16)
        dwdkv_ref[...] += _dot_tn(xh * kvn_ref[...], dckr)
        dh2 = lax.dot_general(dckr, wdkv_ref[...], NT, preferred_element_type=F32)
        dkvn_ref[...] += jnp.sum(dh2 * xh, axis=0, keepdims=True)
        dbn_ref[...] += jnp.sum(dh3 * xh, axis=0, keepdims=True)
        dx1_ref[...] = dx2_ref[...] + _norm_bwd(dh2 * kvn_ref[...] + dh3 * bn_ref[...], xh, r)

    tok = lambda width: pl.BlockSpec((tm, width), lambda i: (i, 0))
    weights = [kv_norm, w_dkv, ckv_norm, w_ukv, b_norm, b_w_in, q_norm, w_uq]
    acc_shapes = [(N_HEADS, Q_RANK, HEAD_PAD), (D_MODEL, Q_RANK + E_B), (N_HEADS, KV_RANK, HEAD_PAD), (D_MODEL, CKR_PAD),
                  (1, Q_RANK), (1, D_MODEL), (1, KV_RANK), (1, D_MODEL)]
    return pl.pallas_call(
        body, name="latent_proj_bwd", grid=(t // tm,),
        in_specs=[tok(wide), tok(wide), tok(E_B), tok(E_B), tok(Q_RANK), tok(CKR_PAD), tok(D_MODEL), tok(D_MODEL), tok(1),
                  _const_spec(rc.shape)] + [_const_spec(w.shape) for w in weights],
        out_specs=[tok(D_MODEL)] + [_acc_spec(s) for s in acc_shapes],
        out_shape=[jax.ShapeDtypeStruct((t, D_MODEL), F32)] + [jax.ShapeDtypeStruct(s, F32) for s in acc_shapes],
        scratch_shapes=[pltpu.VMEM((tm, wide), BF16), pltpu.VMEM((tm, wide), BF16)],
        compiler_params=_params(1),
    )(dq, dk, dv, dgb, cq, ckr, x1, dx2, pos, rc, *weights)


def _attn_fwd(q, k, v, n_seq, seq):
    tq = ATTN_TILE
    nq = seq // tq
    pair = 2 * HEAD_PAD

    def body(q_ref, k_ref, v_ref, o_ref, lc_ref, lr_ref):
        i = pl.program_id(2)
        row = lax.broadcasted_iota(jnp.int32, (tq, tq), 0)
        col = lax.broadcasted_iota(jnp.int32, (tq, tq), 1)
        causal = col <= row
        halves = []
        for hh in range(2):
            qh = q_ref[:, hh * HEAD_PAD:(hh + 1) * HEAD_PAD]

            def step(j, carry, diagonal, hh=hh, qh=qh):
                m, l, acc = carry
                start = pl.multiple_of(j * tq, tq)
                s = lax.dot_general(qh, k_ref[pl.ds(start, tq), hh * HEAD_PAD:(hh + 1) * HEAD_PAD], NT,
                                    preferred_element_type=F32) * SOFTMAX_SCALE
                if diagonal:
                    s = jnp.where(causal, s, -jnp.inf)
                m_new = jnp.maximum(m, jnp.max(s, axis=-1, keepdims=True))
                alpha = jnp.exp(m - m_new)
                p = jnp.exp(s - m_new)
                l = alpha * l + jnp.sum(p, axis=-1, keepdims=True)
                acc = alpha * acc + jnp.dot(p.astype(BF16), v_ref[pl.ds(start, tq), :], preferred_element_type=F32)
                return m_new, l, acc

            init = (jnp.full((tq, 1), -jnp.inf, F32), jnp.zeros((tq, 1), F32), jnp.zeros((tq, HEAD_PAD), F32))
            carry = lax.fori_loop(0, i, functools.partial(step, diagonal=False), init)
            m, l, acc = step(i, carry, True)
            halves.append(acc / l)
            lse = m + jnp.log(l)
            lc_ref[hh] = lse
            lr_ref[hh] = jnp.broadcast_to(lse, (tq, HEAD_PAD)).T[0:8, :]
        lane = lax.broadcasted_iota(jnp.int32, (tq, HEAD_PAD), 1)
        o_ref[...] = jnp.where(lane < V_HEAD, halves[0], halves[1])

    t = n_seq * seq
    return pl.pallas_call(
        body, name="attention_fwd", grid=(n_seq, N_HEADS // 2, nq),
        in_specs=[pl.BlockSpec((tq, pair), lambda s, p, i: (s * nq + i, p)),
                  pl.BlockSpec((seq, pair), lambda s, p, i: (s, p)),
                  pl.BlockSpec((seq, HEAD_PAD), lambda s, p, i: (s, p))],
        out_specs=[pl.BlockSpec((tq, HEAD_PAD), lambda s, p, i: (s * nq + i, p)),
                   pl.BlockSpec((None, 2, tq, 1), lambda s, p, i: (s, p, i, 0)),
                   pl.BlockSpec((None, 2, 8, tq), lambda s, p, i: (s, p, 0, i))],
        out_shape=[jax.ShapeDtypeStruct((t, E_B), F32), jax.ShapeDtypeStruct((n_seq, N_HEADS, seq, 1), F32),
                   jax.ShapeDtypeStruct((n_seq, N_HEADS, 8, seq), F32)],
        compiler_params=_params(3),
    )(q, k, v)


def _attn_bwd_kv(q, k, v, do, lse_row, delta_row, n_seq, seq):
    tk = ATTN_TILE
    nk = seq // tk
    pair = 2 * HEAD_PAD

    def body(q_ref, k_ref, v_ref, do_ref, lr_ref, dr_ref, dk_ref, dv_ref):
        j = pl.program_id(2)
        row = lax.broadcasted_iota(jnp.int32, (tk, tk), 0)
        col = lax.broadcasted_iota(jnp.int32, (tk, tk), 1)
        causal = col >= row
        lane = lax.broadcasted_iota(jnp.int32, (tk, HEAD_PAD), 1)
        dvs = []
        for hh in range(2):
            kh = k_ref[:, hh * HEAD_PAD:(hh + 1) * HEAD_PAD]
            mine = (lane < V_HEAD) if hh == 0 else (lane >= V_HEAD)
            vh = jnp.where(mine, v_ref[...], jnp.zeros_like(v_ref[...]))

            def step(i, carry, diagonal, hh=hh, kh=kh, vh=vh):
                dk_acc, dv_acc = carry
                start = pl.multiple_of(i * tk, tk)
                qi = q_ref[pl.ds(start, tk), hh * HEAD_PAD:(hh + 1) * HEAD_PAD]
                doi = do_ref[pl.ds(start, tk), :]
                st = lax.dot_general(kh, qi, NT, preferred_element_type=F32) * SOFTMAX_SCALE
                pt = jnp.exp(st - lr_ref[hh, 0:1, pl.ds(start, tk)])
                if diagonal:
                    pt = jnp.where(causal, pt, 0.0)
                dv_acc = dv_acc + jnp.dot(pt.astype(BF16), doi, preferred_element_type=F32)
                dpt = lax.dot_general(vh, doi, NT, preferred_element_type=F32)
                dst = pt * (dpt - dr_ref[hh, 0:1, pl.ds(start, tk)]) * SOFTMAX_SCALE
                dk_acc = dk_acc + jnp.dot(dst.astype(BF16), qi, preferred_element_type=F32)
                return dk_acc, dv_acc

            init = (jnp.zeros((tk, HEAD_PAD), F32), jnp.zeros((tk, HEAD_PAD), F32))
            carry = step(j, init, True)
            dk_acc, dv_acc = lax.fori_loop(j + 1, nk, functools.partial(step, diagonal=False), carry)
            dk_ref[:, hh * HEAD_PAD:(hh + 1) * HEAD_PAD] = dk_acc
            dvs.append(dv_acc)
        dv_ref[...] = jnp.where(lane < V_HEAD, dvs[0], dvs[1])

    t = n_seq * seq
    return pl.pallas_call(
        body, name="attention_bwd_kv", grid=(n_seq, N_HEADS // 2, nk),
        in_specs=[pl.BlockSpec((seq, pair), lambda s, p, j: (s, p)),
                  pl.BlockSpec((tk, pair), lambda s, p, j: (s * nk + j, p)),
                  pl.BlockSpec((tk, HEAD_PAD), lambda s, p, j: (s * nk + j, p)),
                  pl.BlockSpec((seq, HEAD_PAD), lambda s, p, j: (s, p)),
                  pl.BlockSpec((None, 2, 8, seq), lambda s, p, j: (s, p, 0, 0)),
                  pl.BlockSpec((None, 2, 8, seq), lambda s, p, j: (s, p, 0, 0))],
        out_specs=[pl.BlockSpec((tk, pair), lambda s, p, j: (s * nk + j, p)),
                   pl.BlockSpec((tk, HEAD_PAD), lambda s, p, j: (s * nk + j, p))],
        out_shape=[jax.ShapeDtypeStruct((t, N_HEADS * HEAD_PAD), F32), jax.ShapeDtypeStruct((t, E_B), F32)],
        compiler_params=_params(3),
    )(q, k, v, do, lse_row, delta_row)


def _attn_bwd_q(q, k, v, do, lse_col, delta_col, n_seq, seq):
    tq = ATTN_TILE
    nq = seq // tq
    pair = 2 * HEAD_PAD

    def body(q_ref, k_ref, v_ref, do_ref, lc_ref, dc_ref, dq_ref):
        i = pl.program_id(2)
        row = lax.broadcasted_iota(jnp.int32, (tq, tq), 0)
        col = lax.broadcasted_iota(jnp.int32, (tq, tq), 1)
        causal = col <= row
        lane = lax.broadcasted_iota(jnp.int32, (tq, HEAD_PAD), 1)
        for hh in range(2):
            qh = q_ref[:, hh * HEAD_PAD:(hh + 1) * HEAD_PAD]
            mine = (lane < V_HEAD) if hh == 0 else (lane >= V_HEAD)
            doh = jnp.where(mine, do_ref[...], jnp.zeros_like(do_ref[...]))
            lse = lc_ref[hh]
            delta = dc_ref[hh]

            def step(j, dq_acc, diagonal, hh=hh, qh=qh, doh=doh, lse=lse, delta=delta):
                start = pl.multiple_of(j * tq, tq)
                kj = k_ref[pl.ds(start, tq), hh * HEAD_PAD:(hh + 1) * HEAD_PAD]
                s = lax.dot_general(qh, kj, NT, preferred_element_type=F32) * SOFTMAX_SCALE
                p = jnp.exp(s - lse)
                if diagonal:
                    p = jnp.where(causal, p, 0.0)
                dp = lax.dot_general(doh, v_ref[pl.ds(start, tq), :], NT, preferred_element_type=F32)
                ds = p * (dp - delta) * SOFTMAX_SCALE
                return dq_acc + jnp.dot(ds.astype(BF16), kj, preferred_element_type=F32)

            dq_acc = lax.fori_loop(0, i, functools.partial(step, diagonal=False), jnp.zeros((tq, HEAD_PAD), F32))
            dq_ref[:, hh * HEAD_PAD:(hh + 1) * HEAD_PAD] = step(i, dq_acc, True)

    t = n_seq * seq
    return pl.pallas_call(
        body, name="attention_bwd_q", grid=(n_seq, N_HEADS // 2, nq),
        in_specs=[pl.BlockSpec((tq, pair), lambda s, p, i: (s * nq + i, p)),
                  pl.BlockSpec((seq, pair), lambda s, p, i: (s, p)),
                  pl.BlockSpec((seq, HEAD_PAD), lambda s, p, i: (s, p)),
                  pl.BlockSpec((tq, HEAD_PAD), lambda s, p, i: (s * nq + i, p)),
                  pl.BlockSpec((None, 2, tq, 1), lambda s, p, i: (s, p, i, 0)),
                  pl.BlockSpec((None, 2, tq, 1), lambda s, p, i: (s, p, i, 0))],
        out_specs=pl.BlockSpec((tq, pair), lambda s, p, i: (s * nq + i, p)),
        out_shape=jax.ShapeDtypeStruct((t, N_HEADS * HEAD_PAD), F32),
        compiler_params=_params(3),
    )(q, k, v, do, lse_col, delta_col)


def _head_tail(o, gb, x1, w_out, final_norm, target, n_seq, seq):
    tm = ROW_TILE
    nt = seq // tm
    n_col = D_MODEL // N_DEV

    def body(o_ref, gb_ref, x1_ref, wout_ref, fn_ref, tgt_ref,
             dx2_ref, do_ref, dgb_ref, dc_ref, dr_ref, dwout_ref, dfn_ref, loss_ref):
        first = jnp.logical_and(pl.program_id(0) == 0, pl.program_id(1) == 0)

        @pl.when(first)
        def _():
            dwout_ref[...] = jnp.zeros_like(dwout_ref)
            dfn_ref[...] = jnp.zeros_like(dfn_ref)
            loss_ref[...] = jnp.zeros_like(loss_ref)

        ov, g = o_ref[...], gb_ref[...]
        silu, dsilu = _silu_parts(g)
        gated = (ov * silu).astype(BF16)
        x2 = x1_ref[...] + jnp.dot(gated, wout_ref[...], preferred_element_type=F32)
        r = _rstd(x2)
        xh = x2 * r
        err = xh * fn_ref[...] - tgt_ref[...]
        loss_ref[...] += 0.5 * jnp.sum(jnp.mean(err * err, axis=-1, keepdims=True), axis=0, keepdims=True)
        dy = err / D_MODEL
        dfn_ref[...] += jnp.sum(dy * xh, axis=0, keepdims=True)
        dx2 = _norm_bwd(dy * fn_ref[...], xh, r)
        dx2_ref[...] = dx2
        dx2b = dx2.astype(BF16)
        dw = lax.dot_general(gated, dx2b, TN, preferred_element_type=F32)
        for d in range(N_DEV):
            dwout_ref[d] += dw[:, d * n_col:(d + 1) * n_col]
        dgated = lax.dot_general(dx2b, wout_ref[...], NT, preferred_element_type=F32)
        do = dgated * silu
        do_ref[...] = do.astype(BF16)
        dgb_ref[...] = dgated * ov * dsilu
        prod = do * ov
        lane = lax.broadcasted_iota(jnp.int32, (tm, HEAD_PAD), 1)
        for p in range(N_HEADS // 2):
            blk = prod[:, p * HEAD_PAD:(p + 1) * HEAD_PAD]
            for hh in range(2):
                mine = (lane < V_HEAD) if hh == 0 else (lane >= V_HEAD)
                delta = jnp.sum(jnp.where(mine, blk, 0.0), axis=-1, keepdims=True)
                dc_ref[2 * p + hh] = delta
                dr_ref[2 * p + hh] = jnp.broadcast_to(delta, (tm, HEAD_PAD)).T[0:8, :]

    tok = lambda width: pl.BlockSpec((tm, width), lambda s, i: (s * nt + i, 0))
    t = n_seq * seq
    return pl.pallas_call(
        body, name="head_tail", grid=(n_seq, nt),
        in_specs=[tok(E_B), tok(E_B), tok(D_MODEL), _const_spec(w_out.shape), _const_spec((1, D_MODEL)), tok(D_MODEL)],
        out_specs=[tok(D_MODEL), tok(E_B), tok(E_B),
                   pl.BlockSpec((None, N_HEADS, tm, 1), lambda s, i: (s, 0, i, 0)),
                   pl.BlockSpec((None, N_HEADS, 8, tm), lambda s, i: (s, 0, 0, i)),
                   _acc_spec((N_DEV, E_B, n_col)), _acc_spec((1, D_MODEL)), _acc_spec((1, 1))],
        out_shape=[jax.ShapeDtypeStruct((t, D_MODEL), F32), jax.ShapeDtypeStruct((t, E_B), BF16),
                   jax.ShapeDtypeStruct((t, E_B), F32), jax.ShapeDtypeStruct((n_seq, N_HEADS, seq, 1), F32),
                   jax.ShapeDtypeStruct((n_seq, N_HEADS, 8, seq), F32), jax.ShapeDtypeStruct((N_DEV, E_B, n_col), F32),
                   jax.ShapeDtypeStruct((1, D_MODEL), F32), jax.ShapeDtypeStruct((1, 1), F32)],
        compiler_params=_params(2),
    )(o, gb, x1, w_out, final_norm, target)


def _adamw_math(w, g, m, v):
    m = ADAM_B1 * m + (1.0 - ADAM_B1) * g
    v = ADAM_B2 * v + (1.0 - ADAM_B2) * jnp.square(g)
    m_hat = m / (1.0 - ADAM_B1 ** ADAM_STEP)
    v_hat = v / (1.0 - ADAM_B2 ** ADAM_STEP)
    delta = -ADAM_LR * (m_hat / (jnp.sqrt(v_hat) + ADAM_EPS) + ADAM_WD * w)
    return delta, m, v


def _adamw_reduce(parts, w, m, v):
    rows, cols = w.shape
    br = 256 if rows % 256 == 0 else 128

    def body(p_ref, w_ref, m_ref, v_ref, g_ref, d_ref, nm_ref, nv_ref):
        g = p_ref[0]
        for k in range(1, N_DEV):
            g = g + p_ref[k]
        g_ref[...] = g
        d_ref[...], nm_ref[...], nv_ref[...] = _adamw_math(w_ref[...], g, m_ref[...], v_ref[...])

    blk = pl.BlockSpec((br, cols), lambda i: (i, 0))
    return pl.pallas_call(
        body, name="adamw_reduce", grid=(rows // br,),
        in_specs=[pl.BlockSpec((N_DEV, br, cols), lambda i: (0, i, 0)), blk, blk, blk],
        out_specs=[blk] * 4, out_shape=[jax.ShapeDtypeStruct((rows, cols), F32)] * 4,
        compiler_params=_params(1),
    )(parts, w, m, v)


def _sum_parts(parts):
    def body(p_ref, o_ref):
        g = p_ref[0]
        for k in range(1, N_DEV):
            g = g + p_ref[k]
        o_ref[...] = g

    return pl.pallas_call(body, name="small_grad_sum", out_shape=jax.ShapeDtypeStruct(parts.shape[1:], F32))(parts)


def _adamw_small(gs, ws, ms, vs):
    n = len(gs)

    def body(*refs):
        ins, outs = refs[:4 * n], refs[4 * n:]
        for j in range(n):
            g_ref, w_ref, m_ref, v_ref = ins[j], ins[n + j], ins[2 * n + j], ins[3 * n + j]
            outs[3 * j][...], outs[3 * j + 1][...], outs[3 * j + 2][...] = _adamw_math(
                w_ref[...], g_ref[...], m_ref[...], v_ref[...])

    shapes = [jax.ShapeDtypeStruct(g.shape, F32) for g in gs for _ in range(3)]
    flat = pl.pallas_call(body, name="adamw_small", out_shape=shapes)(*gs, *ws, *ms, *vs)
    return [tuple(flat[3 * j:3 * j + 3]) for j in range(n)]


SMALL_ROWS = 8
LOSS_LANE = D_MODEL - 1


def _pack_small(g_fn, g_kvn, g_bn, g_an, g_ckvn, g_qn, g_conv, loss_part):
    def body(fn_ref, kvn_ref, bn_ref, an_ref, ckvn_ref, qn_ref, conv_ref, loss_ref, o_ref):
        o_ref[0:1, :] = fn_ref[...]
        o_ref[1:2, :] = kvn_ref[...]
        o_ref[2:3, :] = bn_ref[...]
        o_ref[3:4, :] = an_ref[...]
        lane = lax.broadcasted_iota(jnp.int32, (1, D_MODEL), 1)
        o_ref[4:5, :] = jnp.where(lane == LOSS_LANE, loss_ref[...], 0.0)
        o_ref[4:5, 0:KV_RANK] = ckvn_ref[...]
        o_ref[4:5, KV_RANK:KV_RANK + Q_RANK] = qn_ref[...]
        o_ref[5:8, :] = conv_ref[0:3, :]

    return pl.pallas_call(body, name="pack_small", out_shape=jax.ShapeDtypeStruct((SMALL_ROWS, D_MODEL), F32))(
        g_fn, g_kvn, g_bn, g_an, g_ckvn, g_qn, g_conv, loss_part)


def _pad_cols(a, width):
    return jnp.pad(a, ((0, 0), (0, width - a.shape[1])))


def _dkv_to_padded(a):
    r = a.shape[0]
    z = jnp.zeros((r, ROPE_LO), a.dtype)
    z2 = jnp.zeros((r, HEAD_PAD - ROPE_LO - QK_ROPE), a.dtype)
    return jnp.concatenate([a[:, :KV_RANK], z, a[:, KV_RANK:], z2], axis=1)


def _dkv_from_padded(a):
    return jnp.concatenate([a[:, :KV_RANK], a[:, KV_RANK + ROPE_LO:KV_RANK + ROPE_LO + QK_ROPE]], axis=1)


def _unstack_cols(a):
    return jnp.transpose(a, (1, 0, 2)).reshape(a.shape[1], N_DEV * a.shape[2])


def kernel(x, positions, a_norm, a_w_in, a_conv, a_w_out, kv_norm, w_dkv, ckv_norm, w_ukv, b_norm, b_w_in, b_q_norm, b_w_uq, b_w_out, final_norm, loss_target, m_a_norm, m_a_w_in, m_a_conv, m_a_w_out, m_kv_norm, m_w_dkv, m_ckv_norm, m_w_ukv, m_b_norm, m_b_w_in, m_b_q_norm, m_b_w_uq, m_b_w_out, m_final_norm, v_a_norm, v_a_w_in, v_a_conv, v_a_w_out, v_kv_norm, v_w_dkv, v_ckv_norm, v_w_ukv, v_b_norm, v_b_w_in, v_b_q_norm, v_b_w_uq, v_b_w_out, v_final_norm):
    n_seq, seq, _ = x.shape
    t = n_seq * seq
    me = 4 * lax.axis_index("x") + 2 * lax.axis_index("y") + lax.axis_index("c")

    big = {
        "a_w_in": (a_w_in[0], m_a_w_in[0], v_a_w_in[0]),
        "a_w_out": (a_w_out[0], m_a_w_out[0], v_a_w_out[0]),
        "w_dkv": tuple(_dkv_to_padded(a) for a in (w_dkv, m_w_dkv, v_w_dkv)),
        "w_ukv": (w_ukv, m_w_ukv, v_w_ukv),
        "b_w_in": (b_w_in[0], m_b_w_in[0], v_b_w_in[0]),
        "b_w_uq": tuple(_pad_cols(a[0], HEAD_PAD) for a in (b_w_uq, m_b_w_uq, v_b_w_uq)),
        "b_w_out": (b_w_out[0], m_b_w_out[0], v_b_w_out[0]),
    }
    names = list(big)
    gathered = _all_gather([big[n][0].astype(BF16) for n in names] + [a_norm, a_conv[0]])
    full = dict(zip(names, gathered[:-2]))
    a_norm_f = gathered[-2].reshape(1, D_MODEL)
    a_conv_f = _unstack_cols(gathered[-1])
    w_a_in = full["a_w_in"]
    w_a_out = full["a_w_out"].reshape(D_MODEL, D_MODEL)
    w_dkv_f = full["w_dkv"].reshape(D_MODEL, CKR_PAD)
    w_ukv_f = _unstack_cols(full["w_ukv"])
    w_b_in = full["b_w_in"].reshape(D_MODEL, Q_RANK + E_B)
    w_uq_f = _unstack_cols(full["b_w_uq"])
    w_b_out = _unstack_cols(full["b_w_out"])

    x2d = x.reshape(t, D_MODEL)
    tgt = loss_target.reshape(t, D_MODEL)
    pos = positions.astype(F32).reshape(t, 1)
    rc = _rope_consts()
    kvn, ckvn, bn, qn, fn = kv_norm.reshape(1, -1), ckv_norm.reshape(1, -1), b_norm, b_q_norm, final_norm.reshape(1, -1)
    layer_b = (kvn, w_dkv_f, ckvn, w_ukv_f, bn, w_b_in, qn, w_uq_f)

    x1, proj, conv = _conv_fwd(x2d, a_norm_f, w_a_in, a_conv_f, w_a_out, n_seq, seq)
    ckr, cq, gb, q, k, v = _proj_fwd(x1, pos, rc, *layer_b)
    o, lse_col, lse_row = _attn_fwd(q, k, v, n_seq, seq)
    dx2, do, dgb, delta_col, delta_row, g_b_out, g_fn, loss_part = _head_tail(o, gb, x1, w_b_out, fn, tgt, n_seq, seq)
    dk, dv = _attn_bwd_kv(q, k, v, do, lse_row, delta_row, n_seq, seq)
    dq = _attn_bwd_q(q, k, v, do, lse_col, delta_col, n_seq, seq)
    dx1, g_uq, g_b_in, g_ukv, g_dkv, g_qn, g_bn, g_ckvn, g_kvn = _proj_bwd(
        dq, dk, dv, dgb, cq, ckr, x1, dx2, pos, rc, *layer_b)
    dx, h_a, dproj, g_a_out, g_an, g_conv = _conv_bwd(dx1, x2d, proj, conv, a_norm_f, w_a_in, a_conv_f, w_a_out, n_seq, seq)
    g_a_in = _stacked_weight_grad(h_a, dproj, D_MODEL * 4 // N_DEV)

    stacks = {
        "a_w_in": g_a_in,
        "a_w_out": g_a_out.reshape(N_DEV, D_MODEL // N_DEV, D_MODEL),
        "w_dkv": g_dkv.reshape(N_DEV, D_MODEL // N_DEV, CKR_PAD),
        "w_ukv": g_ukv,
        "b_w_in": g_b_in.reshape(N_DEV, D_MODEL // N_DEV, Q_RANK + E_B),
        "b_w_uq": g_uq,
        "b_w_out": g_b_out,
    }
    small = _pack_small(g_fn, g_kvn, g_bn, g_an, g_ckvn, g_qn, g_conv, loss_part)
    parts, small_parts = _grad_exchange([stacks[n] for n in names], small)

    outs = {}
    for n, p in zip(names, parts):
        w, m, vv = big[n]
        outs[n] = _adamw_reduce(p, w, m, vv)
    outs["w_dkv"] = tuple(_dkv_from_padded(a) for a in outs["w_dkv"])
    outs["b_w_uq"] = tuple(a[:, :QK_NOPE + QK_ROPE] for a in outs["b_w_uq"])
    for n in ("a_w_in", "a_w_out", "b_w_in", "b_w_uq", "b_w_out"):
        outs[n] = tuple(a[None] for a in outs[n])

    total = _sum_parts(small_parts)
    loss = total[4, LOSS_LANE]
    shard = D_MODEL // N_DEV
    g_small = {
        "final_norm": total[0], "kv_norm": total[1], "b_norm": total[2:3],
        "a_norm": lax.dynamic_slice_in_dim(total[3:4], me * shard, shard, axis=1),
        "ckv_norm": total[4, 0:KV_RANK], "b_q_norm": total[4:5, KV_RANK:KV_RANK + Q_RANK],
        "a_conv": lax.dynamic_slice_in_dim(total[5:8], me * shard, shard, axis=1)[None],
    }
    small_state = {
        "final_norm": (final_norm, m_final_norm, v_final_norm), "kv_norm": (kv_norm, m_kv_norm, v_kv_norm),
        "b_norm": (b_norm, m_b_norm, v_b_norm), "a_norm": (a_norm, m_a_norm, v_a_norm),
        "ckv_norm": (ckv_norm, m_ckv_norm, v_ckv_norm), "b_q_norm": (b_q_norm, m_b_q_norm, v_b_q_norm),
        "a_conv": (a_conv, m_a_conv, v_a_conv),
    }
    small_names = list(g_small)
    as2d = lambda a: a.reshape(-1, a.shape[-1])
    upd = _adamw_small([as2d(g_small[n]) for n in small_names],
                       *[[as2d(small_state[n][j]) for n in small_names] for j in range(3)])
    for n, u in zip(small_names, upd):
        outs[n] = (g_small[n],) + tuple(a.reshape(g_small[n].shape) for a in u)

    order = ["a_norm", "a_w_in", "a_conv", "a_w_out", "kv_norm", "w_dkv", "ckv_norm", "w_ukv", "b_norm", "b_w_in",
             "b_q_norm", "b_w_uq", "b_w_out", "final_norm"]
    result = [loss, dx.reshape(n_seq, seq, D_MODEL)]
    for j in range(4):
        result += [outs[n][j] for n in order]
    return tuple(result)
```

```python
import functools
import math

import numpy as np
import jax
import jax.numpy as jnp
from jax import lax
from jax.experimental import pallas as pl
from jax.experimental.pallas import tpu as pltpu

F32 = jnp.float32
BF16 = jnp.bfloat16

D_MODEL = 1024
N_HEADS = 8
QK_NOPE = 64
QK_ROPE = 32
V_HEAD = 64
KV_RANK = 256
Q_RANK = 384
E_B = N_HEADS * V_HEAD
HEAD_PAD = 128
CKR_PAD = KV_RANK + HEAD_PAD
ROPE_LO = QK_NOPE
ROPE_HALF = QK_ROPE // 2
ROPE_THETA = 10000.0
SOFTMAX_SCALE = 1.0 / math.sqrt(QK_NOPE + QK_ROPE)
EPS = 1e-6
N_DEV = 8

ADAM_LR = 0.001
ADAM_B1 = 0.9
ADAM_B2 = 0.999
ADAM_EPS = 1e-08
ADAM_WD = 0.01
ADAM_STEP = 10

ROW_TILE = 256
ATTN_FWD_TILES = (512, 512)
ATTN_BWD_KV_TILES = (512, 512)
ATTN_BWD_Q_TILES = (512, 512)
VMEM_LIMIT = 56 * 1024 * 1024

MESH = pl.DeviceIdType.MESH
NT = (((1,), (1,)), ((), ()))
TN = (((0,), (0,)), ((), ()))


def _dot(a, b):
    return jnp.dot(a.astype(BF16), b.astype(BF16), preferred_element_type=F32)


def _dot_nt(a, b):
    return lax.dot_general(a.astype(BF16), b.astype(BF16), NT, preferred_element_type=F32)


def _dot_tn(a, b):
    return lax.dot_general(a.astype(BF16), b.astype(BF16), TN, preferred_element_type=F32)


def _rstd(x):
    return lax.rsqrt(jnp.mean(x * x, axis=-1, keepdims=True) + EPS)


def _norm_bwd(a, xh, r):
    return r * (a - xh * jnp.mean(a * xh, axis=-1, keepdims=True))


def _silu_parts(g):
    sg = jax.nn.sigmoid(g)
    return g * sg, sg * (1.0 + g * (1.0 - sg))


def _rope_consts():
    inv = (ROPE_THETA ** (-np.arange(0, QK_ROPE, 2, dtype=np.float32) / QK_ROPE)).astype(np.float32)
    t = np.zeros((8, HEAD_PAD), np.float32)
    t[0, ROPE_LO:ROPE_LO + ROPE_HALF] = inv
    t[0, ROPE_LO + ROPE_HALF:ROPE_LO + QK_ROPE] = inv
    t[1, ROPE_LO:ROPE_LO + ROPE_HALF] = -1.0
    t[2, ROPE_LO + ROPE_HALF:ROPE_LO + QK_ROPE] = 1.0
    return jnp.asarray(t)


def _rope_tables(pos, rc):
    ang = pos * rc[0:1, :]
    cosv = jnp.cos(ang)
    sinv = jnp.sin(ang)
    return cosv, sinv * rc[1:2, :], sinv * rc[2:3, :]


def _rope(x, ct, s1, s2):
    up = pltpu.roll(x, HEAD_PAD - ROPE_HALF, 1)
    dn = pltpu.roll(x, ROPE_HALF, 1)
    return x * ct + up * s1 + dn * s2


def _const_spec(shape):
    nd = len(shape)
    return pl.BlockSpec(shape, lambda *_: (0,) * nd, pipeline_mode=pl.Buffered(1))


def _acc_spec(shape):
    nd = len(shape)
    return pl.BlockSpec(shape, lambda *_: (0,) * nd)


def _params(n_axes):
    return pltpu.CompilerParams(dimension_semantics=("arbitrary",) * n_axes, vmem_limit_bytes=VMEM_LIMIT)


def _place():
    x, y, c = lax.axis_index("x"), lax.axis_index("y"), lax.axis_index("c")
    return x, y, c, 4 * x + 2 * y + c


def _peer(x, y, c, mask):
    px = 1 - x if mask & 4 else x
    py = 1 - y if mask & 2 else y
    pc = 1 - c if mask & 1 else c
    return (px, py, pc), 4 * px + 2 * py + pc


def _all_gather(shards):
    n = len(shards)

    def body(*refs):
        ins, outs = refs[:n], refs[n:2 * n]
        send_sems, recv_sems, local_sems = refs[2 * n:]
        x, y, c, me = _place()
        copies = []
        for w in range(n):
            mine = pltpu.make_async_copy(ins[w], outs[w].at[me], local_sems.at[w])
            mine.start()
            copies.append(mine)
            for mask in range(1, N_DEV):
                peer, peer_idx = _peer(x, y, c, mask)
                send = pltpu.make_async_remote_copy(
                    src_ref=ins[w], dst_ref=outs[w].at[me], send_sem=send_sems.at[w, mask - 1],
                    recv_sem=recv_sems.at[w, mask - 1], device_id=peer, device_id_type=MESH)
                send.start()
                recv = pltpu.make_async_remote_copy(
                    src_ref=ins[w], dst_ref=outs[w].at[peer_idx], send_sem=send_sems.at[w, mask - 1],
                    recv_sem=recv_sems.at[w, mask - 1], device_id=peer, device_id_type=MESH)
                copies.append(recv)
        for cp in copies:
            cp.wait()

    any_spec = pl.BlockSpec(memory_space=pl.ANY)
    return pl.pallas_call(
        body, name="weight_all_gather",
        out_shape=[jax.ShapeDtypeStruct((N_DEV,) + s.shape, s.dtype) for s in shards],
        in_specs=[any_spec] * n, out_specs=[any_spec] * n,
        scratch_shapes=[pltpu.SemaphoreType.DMA((n, N_DEV - 1)), pltpu.SemaphoreType.DMA((n, N_DEV - 1)),
                        pltpu.SemaphoreType.DMA((n,))],
    )(*shards)


def _grad_exchange(stacks, small):
    n = len(stacks)

    def body(*refs):
        ins, small_in = refs[:n], refs[n]
        outs, small_out = refs[n + 1:2 * n + 1], refs[2 * n + 1]
        send_sems, recv_sems, local_sems = refs[2 * n + 2:]
        x, y, c, me = _place()
        copies = []
        for w in range(n + 1):
            if w < n:
                mine = pltpu.make_async_copy(ins[w].at[me], outs[w].at[0], local_sems.at[w])
            else:
                mine = pltpu.make_async_copy(small_in, small_out.at[me], local_sems.at[w])
            mine.start()
            copies.append(mine)
            for mask in range(1, N_DEV):
                peer, peer_idx = _peer(x, y, c, mask)
                if w < n:
                    src, dst, landed = ins[w].at[peer_idx], outs[w].at[mask], outs[w].at[mask]
                else:
                    src, dst, landed = small_in, small_out.at[me], small_out.at[peer_idx]
                send = pltpu.make_async_remote_copy(
                    src_ref=src, dst_ref=dst, send_sem=send_sems.at[w, mask - 1],
                    recv_sem=recv_sems.at[w, mask - 1], device_id=peer, device_id_type=MESH)
                send.start()
                recv = pltpu.make_async_remote_copy(
                    src_ref=src, dst_ref=landed, send_sem=send_sems.at[w, mask - 1],
                    recv_sem=recv_sems.at[w, mask - 1], device_id=peer, device_id_type=MESH)
                copies.append(recv)
        for cp in copies:
            cp.wait()

    any_spec = pl.BlockSpec(memory_space=pl.ANY)
    outs = pl.pallas_call(
        body, name="grad_exchange",
        out_shape=[jax.ShapeDtypeStruct(s.shape, s.dtype) for s in stacks]
        + [jax.ShapeDtypeStruct((N_DEV,) + small.shape, small.dtype)],
        in_specs=[any_spec] * (n + 1), out_specs=[any_spec] * (n + 1),
        scratch_shapes=[pltpu.SemaphoreType.DMA((n + 1, N_DEV - 1)), pltpu.SemaphoreType.DMA((n + 1, N_DEV - 1)),
                        pltpu.SemaphoreType.DMA((n + 1,))],
    )(*stacks, small)
    return outs[:n], outs[n]


def _conv_fwd(x, a_norm, w_in, conv_w, w_out, n_seq, seq):
    tm = ROW_TILE
    nt = seq // tm
    n_col = w_in.shape[2]

    def body(x_ref, an_ref, win_ref, cw_ref, wout_ref, x1_ref, proj_ref, conv_ref, prev_ref):
        @pl.when(pl.program_id(1) == 0)
        def _():
            prev_ref[...] = jnp.zeros_like(prev_ref)

        xv = x_ref[...]
        h = (xv * _rstd(xv) * an_ref[...]).astype(BF16)
        for d in range(N_DEV):
            proj_ref[:, d * n_col:(d + 1) * n_col] = jnp.dot(h, win_ref[d], preferred_element_type=F32)
        b = proj_ref[:, 0:D_MODEL]
        v = proj_ref[:, D_MODEL:2 * D_MODEL] * proj_ref[:, 2 * D_MODEL:3 * D_MODEL]
        g = proj_ref[:, 3 * D_MODEL:4 * D_MODEL]
        w0, w1, w2 = cw_ref[0:1, :], cw_ref[1:2, :], cw_ref[2:3, :]
        conv_ref[...] = w0 * pltpu.roll(v, 2, 0) + w1 * pltpu.roll(v, 1, 0) + w2 * v
        rows = lax.broadcasted_iota(jnp.int32, (8, D_MODEL), 0)
        p8, v8 = prev_ref[...], v[0:8]
        back1 = jnp.where(rows < 1, pltpu.roll(p8, 1, 0), pltpu.roll(v8, 1, 0))
        back2 = jnp.where(rows < 2, pltpu.roll(p8, 2, 0), pltpu.roll(v8, 2, 0))
        conv_ref[0:8, :] = w0 * back2 + w1 * back1 + w2 * v8
        prev_ref[...] = v[tm - 8:tm]
        silu, _ = _silu_parts(g)
        yv = silu * b * conv_ref[...]
        x1_ref[...] = xv + _dot(yv, wout_ref[...])

    tok = lambda width: pl.BlockSpec((tm, width), lambda s, i: (s * nt + i, 0))
    t = n_seq * seq
    return pl.pallas_call(
        body, name="conv_mixer_fwd", grid=(n_seq, nt),
        in_specs=[tok(D_MODEL), _const_spec((1, D_MODEL)), _const_spec(w_in.shape), _const_spec((3, D_MODEL)),
                  _const_spec(w_out.shape)],
        out_specs=[tok(D_MODEL), tok(4 * D_MODEL), tok(D_MODEL)],
        out_shape=[jax.ShapeDtypeStruct((t, D_MODEL), F32), jax.ShapeDtypeStruct((t, 4 * D_MODEL), F32),
                   jax.ShapeDtypeStruct((t, D_MODEL), F32)],
        scratch_shapes=[pltpu.VMEM((8, D_MODEL), F32)],
        compiler_params=_params(2),
    )(x, a_norm, w_in, conv_w, w_out)


def _conv_bwd(dx1, x, proj, conv, a_norm, w_in, conv_w, w_out, n_seq, seq):
    tm = ROW_TILE
    nt = seq // tm
    n_col = w_in.shape[2]

    def body(dx1_ref, x_ref, proj_ref, conv_ref, an_ref, win_ref, cw_ref, wout_ref,
             dx_ref, h_ref, dproj_ref, dwout_ref, dan_ref, dcw_ref, next_ref, d1_ref, d2_ref):
        first = jnp.logical_and(pl.program_id(0) == 0, pl.program_id(1) == 0)

        @pl.when(first)
        def _():
            dwout_ref[...] = jnp.zeros_like(dwout_ref)
            dan_ref[...] = jnp.zeros_like(dan_ref)
            dcw_ref[...] = jnp.zeros_like(dcw_ref)

        @pl.when(pl.program_id(1) == 0)
        def _():
            next_ref[...] = jnp.zeros_like(next_ref)

        dx1v = dx1_ref[...]
        dy = _dot_nt(dx1v, wout_ref[...])
        b = proj_ref[:, 0:D_MODEL]
        cc = proj_ref[:, D_MODEL:2 * D_MODEL]
        u = proj_ref[:, 2 * D_MODEL:3 * D_MODEL]
        g = proj_ref[:, 3 * D_MODEL:4 * D_MODEL]
        cv = conv_ref[...]
        silu, dsilu = _silu_parts(g)
        dwout_ref[...] += _dot_tn(silu * b * cv, dx1v)
        dproj_ref[:, 3 * D_MODEL:4 * D_MODEL] = (dy * b * cv * dsilu).astype(BF16)
        dproj_ref[:, 0:D_MODEL] = (dy * silu * cv).astype(BF16)
        dconv = dy * silu * b
        d1_ref[...] = pltpu.roll(dconv, tm - 1, 0)
        d2_ref[...] = pltpu.roll(dconv, tm - 2, 0)
        rows = lax.broadcasted_iota(jnp.int32, (8, D_MODEL), 0)
        n8, c8 = next_ref[...], dconv[tm - 8:tm]
        d1_ref[tm - 8:tm, :] = jnp.where(rows >= 7, pltpu.roll(n8, 7, 0), pltpu.roll(c8, 7, 0))
        d2_ref[tm - 8:tm, :] = jnp.where(rows >= 6, pltpu.roll(n8, 6, 0), pltpu.roll(c8, 6, 0))
        next_ref[...] = dconv[0:8]
        d1, d2 = d1_ref[...], d2_ref[...]
        v = cc * u
        dcw_ref[0:1, :] += jnp.sum(d2 * v, axis=0, keepdims=True)
        dcw_ref[1:2, :] += jnp.sum(d1 * v, axis=0, keepdims=True)
        dcw_ref[2:3, :] += jnp.sum(dconv * v, axis=0, keepdims=True)
        dv = cw_ref[0:1, :] * d2 + cw_ref[1:2, :] * d1 + cw_ref[2:3, :] * dconv
        dproj_ref[:, D_MODEL:2 * D_MODEL] = (dv * u).astype(BF16)
        dproj_ref[:, 2 * D_MODEL:3 * D_MODEL] = (dv * cc).astype(BF16)
        dh = jnp.zeros((tm, D_MODEL), F32)
        for d in range(N_DEV):
            dh += lax.dot_general(dproj_ref[:, d * n_col:(d + 1) * n_col], win_ref[d], NT, preferred_element_type=F32)
        xv = x_ref[...]
        r = _rstd(xv)
        xh = xv * r
        h_ref[...] = (xh * an_ref[...]).astype(BF16)
        dan_ref[...] += jnp.sum(dh * xh, axis=0, keepdims=True)
        dx_ref[...] = dx1v + _norm_bwd(dh * an_ref[...], xh, r)

    tok = lambda width: pl.BlockSpec((tm, width), lambda s, i: (s * nt + nt - 1 - i, 0))
    t = n_seq * seq
    return pl.pallas_call(
        body, name="conv_mixer_bwd", grid=(n_seq, nt),
        in_specs=[tok(D_MODEL), tok(D_MODEL), tok(4 * D_MODEL), tok(D_MODEL), _const_spec((1, D_MODEL)),
                  _const_spec(w_in.shape), _const_spec((3, D_MODEL)), _const_spec(w_out.shape)],
        out_specs=[tok(D_MODEL), tok(D_MODEL), tok(4 * D_MODEL), _acc_spec((D_MODEL, D_MODEL)),
                   _acc_spec((1, D_MODEL)), _acc_spec((8, D_MODEL))],
        out_shape=[jax.ShapeDtypeStruct((t, D_MODEL), F32), jax.ShapeDtypeStruct((t, D_MODEL), BF16),
                   jax.ShapeDtypeStruct((t, 4 * D_MODEL), BF16), jax.ShapeDtypeStruct((D_MODEL, D_MODEL), F32),
                   jax.ShapeDtypeStruct((1, D_MODEL), F32), jax.ShapeDtypeStruct((8, D_MODEL), F32)],
        scratch_shapes=[pltpu.VMEM((8, D_MODEL), F32), pltpu.VMEM((tm, D_MODEL), F32), pltpu.VMEM((tm, D_MODEL), F32)],
        compiler_params=_params(2),
    )(dx1, x, proj, conv, a_norm, w_in, conv_w, w_out)


def _stacked_weight_grad(a, b, n_col):
    t, r = a.shape
    bt = 512

    def body(a_ref, b_ref, o_ref):
        @pl.when(pl.program_id(1) == 0)
        def _():
            o_ref[...] = jnp.zeros_like(o_ref)

        o_ref[...] += lax.dot_general(a_ref[...], b_ref[...], TN, preferred_element_type=F32)

    return pl.pallas_call(
        body, name="stacked_weight_grad", grid=(N_DEV, t // bt),
        in_specs=[pl.BlockSpec((bt, r), lambda d, k: (k, 0)), pl.BlockSpec((bt, n_col), lambda d, k: (k, d))],
        out_specs=pl.BlockSpec((None, r, n_col), lambda d, k: (d, 0, 0)),
        out_shape=jax.ShapeDtypeStruct((N_DEV, r, n_col), F32),
        compiler_params=_params(2),
    )(a, b)


def _proj_fwd(x1, pos, rc, kv_norm, w_dkv, ckv_norm, w_ukv, b_norm, b_w_in, q_norm, w_uq):
    t = x1.shape[0]
    tm = ROW_TILE

    def body(x1_ref, pos_ref, rc_ref, kvn_ref, wdkv_ref, ckvn_ref, wukv_ref, bn_ref, bwin_ref, qn_ref, wuq_ref,
             ckr_ref, cq_ref, gb_ref, q_ref, k_ref, v_ref):
        ct, s1, s2 = _rope_tables(pos_ref[...], rc_ref[...])
        xv = x1_ref[...]
        xh = xv * _rstd(xv)
        ckr = _dot(xh * kvn_ref[...], wdkv_ref[...])
        ckr_ref[...] = ckr
        ckv = ckr[:, 0:KV_RANK]
        kv = _dot(ckv * _rstd(ckv) * ckvn_ref[...], wukv_ref[...])
        k_rope = _rope(ckr[:, KV_RANK:CKR_PAD], ct, s1, s2)
        lane = lax.broadcasted_iota(jnp.int32, (tm, HEAD_PAD), 1)
        low = lane < QK_NOPE
        for h in range(N_HEADS):
            k_ref[:, h * HEAD_PAD:(h + 1) * HEAD_PAD] = jnp.where(
                low, kv[:, h * HEAD_PAD:(h + 1) * HEAD_PAD], k_rope).astype(BF16)
        for p in range(N_HEADS // 2):
            even = kv[:, (2 * p) * HEAD_PAD:(2 * p + 1) * HEAD_PAD]
            odd = kv[:, (2 * p + 1) * HEAD_PAD:(2 * p + 2) * HEAD_PAD]
            v_ref[:, p * HEAD_PAD:(p + 1) * HEAD_PAD] = jnp.where(low, pltpu.roll(even, V_HEAD, 1), odd).astype(BF16)
        pb = _dot(xh * bn_ref[...], bwin_ref[...])
        cq = pb[:, 0:Q_RANK]
        cq_ref[...] = cq
        gb_ref[...] = pb[:, Q_RANK:Q_RANK + E_B]
        q = _dot(cq * _rstd(cq) * qn_ref[...], wuq_ref[...])
        for h in range(N_HEADS):
            q_ref[:, h * HEAD_PAD:(h + 1) * HEAD_PAD] = _rope(
                q[:, h * HEAD_PAD:(h + 1) * HEAD_PAD], ct, s1, s2).astype(BF16)

    tok = lambda width: pl.BlockSpec((tm, width), lambda i: (i, 0))
    weights = [kv_norm, w_dkv, ckv_norm, w_ukv, b_norm, b_w_in, q_norm, w_uq]
    wide = N_HEADS * HEAD_PAD
    return pl.pallas_call(
        body, name="latent_proj_fwd", grid=(t // tm,),
        in_specs=[tok(D_MODEL), tok(1), _const_spec(rc.shape)] + [_const_spec(w.shape) for w in weights],
        out_specs=[tok(CKR_PAD), tok(Q_RANK), tok(E_B), tok(wide), tok(wide), tok(E_B)],
        out_shape=[jax.ShapeDtypeStruct((t, CKR_PAD), F32), jax.ShapeDtypeStruct((t, Q_RANK), F32),
                   jax.ShapeDtypeStruct((t, E_B), F32), jax.ShapeDtypeStruct((t, wide), BF16),
                   jax.ShapeDtypeStruct((t, wide), BF16), jax.ShapeDtypeStruct((t, E_B), BF16)],
        compiler_params=_params(1),
    )(x1, pos, rc, *weights)


def _proj_bwd(dq, dk, dv, dgb, cq, ckr, x1, dx2, pos, rc, kv_norm, w_dkv, ckv_norm, w_ukv, b_norm, b_w_in, q_norm, w_uq):
    t = x1.shape[0]
    tm = ROW_TILE
    wide = N_HEADS * HEAD_PAD

    def body(dq_ref, dk_ref, dv_ref, dgb_ref, cq_ref, ckr_ref, x1_ref, dx2_ref, pos_ref, rc_ref,
             kvn_ref, wdkv_ref, ckvn_ref, wukv_ref, bn_ref, bwin_ref, qn_ref, wuq_ref,
             dx1_ref, dwuq_ref, dwbin_ref, dwukv_ref, dwdkv_ref, dqn_ref, dbn_ref, dckvn_ref, dkvn_ref, dqu_ref, dkv_ref):
        @pl.when(pl.program_id(0) == 0)
        def _():
            for ref in (dwuq_ref, dwbin_ref, dwukv_ref, dwdkv_ref, dqn_ref, dbn_ref, dckvn_ref, dkvn_ref):
                ref[...] = jnp.zeros_like(ref)

        ct, s1, s2 = _rope_tables(pos_ref[...], rc_ref[...])
        lane = lax.broadcasted_iota(jnp.int32, (tm, HEAD_PAD), 1)
        low = lane < QK_NOPE
        for h in range(N_HEADS):
            dqu_ref[:, h * HEAD_PAD:(h + 1) * HEAD_PAD] = _rope(
                dq_ref[:, h * HEAD_PAD:(h + 1) * HEAD_PAD], ct, -s1, -s2).astype(BF16)
        cq = cq_ref[...]
        rq = _rstd(cq)
        cqh = cq * rq
        cqn = (cqh * qn_ref[...]).astype(BF16)
        for h in range(N_HEADS):
            dwuq_ref[h] += lax.dot_general(cqn, dqu_ref[:, h * HEAD_PAD:(h + 1) * HEAD_PAD], TN, preferred_element_type=F32)
        dcqn = lax.dot_general(dqu_ref[...], wuq_ref[...], NT, preferred_element_type=F32)
        dqn_ref[...] += jnp.sum(dcqn * cqh, axis=0, keepdims=True)
        dcq = _norm_bwd(dcqn * qn_ref[...], cqh, rq)
        dpb = jnp.concatenate([dcq, dgb_ref[...]], axis=1).astype(BF16)
        xv = x1_ref[...]
        r = _rstd(xv)
        xh = xv * r
        dwbin_ref[...] += _dot_tn(xh * bn_ref[...], dpb)
        dh3 = lax.dot_general(dpb, bwin_ref[...], NT, preferred_element_type=F32)
        dk_rope = jnp.zeros((tm, HEAD_PAD), F32)
        for p in range(N_HEADS // 2):
            dvp = dv_ref[:, p * HEAD_PAD:(p + 1) * HEAD_PAD]
            dk_even = dk_ref[:, (2 * p) * HEAD_PAD:(2 * p + 1) * HEAD_PAD]
            dk_odd = dk_ref[:, (2 * p + 1) * HEAD_PAD:(2 * p + 2) * HEAD_PAD]
            dkv_ref[:, (2 * p) * HEAD_PAD:(2 * p + 1) * HEAD_PAD] = jnp.where(low, dk_even, pltpu.roll(dvp, V_HEAD, 1)).astype(BF16)
            dkv_ref[:, (2 * p + 1) * HEAD_PAD:(2 * p + 2) * HEAD_PAD] = jnp.where(low, dk_odd, dvp).astype(BF16)
            dk_rope += dk_even + dk_odd
        rope_lanes = jnp.logical_and(lane >= ROPE_LO, lane < ROPE_LO + QK_ROPE)
        dk_rope = jnp.where(rope_lanes, _rope(dk_rope, ct, -s1, -s2), 0.0)
        ckv = ckr_ref[:, 0:KV_RANK]
        rk = _rstd(ckv)
        ckh = ckv * rk
        ckvn = (ckh * ckvn_ref[...]).astype(BF16)
        for h in range(N_HEADS):
            dwukv_ref[h] += lax.dot_general(ckvn, dkv_ref[:, h * HEAD_PAD:(h + 1) * HEAD_PAD], TN, preferred_element_type=F32)
        dckvn = lax.dot_general(dkv_ref[...], wukv_ref[...], NT, preferred_element_type=F32)
        dckvn_ref[...] += jnp.sum(dckvn * ckh, axis=0, keepdims=True)
        dckv = _norm_bwd(dckvn * ckvn_ref[...], ckh, rk)
        dckr = jnp.concatenate([dckv, dk_rope], axis=1).astype(BF16)
        dwdkv_ref[...] += _dot_tn(xh * kvn_ref[...], dckr)
        dh2 = lax.dot_general(dckr, wdkv_ref[...], NT, preferred_element_type=F32)
        dkvn_ref[...] += jnp.sum(dh2 * xh, axis=0, keepdims=True)
        dbn_ref[...] += jnp.sum(dh3 * xh, axis=0, keepdims=True)
        dx1_ref[...] = dx2_ref[...] + _norm_bwd(dh2 * kvn_ref[...] + dh3 * bn_ref[...], xh, r)

    tok = lambda width: pl.BlockSpec((tm, width), lambda i: (i, 0))
    weights = [kv_norm, w_dkv, ckv_norm, w_ukv, b_norm, b_w_in, q_norm, w_uq]
    acc_shapes = [(N_HEADS, Q_RANK, HEAD_PAD), (D_MODEL, Q_RANK + E_B), (N_HEADS, KV_RANK, HEAD_PAD), (D_MODEL, CKR_PAD),
                  (1, Q_RANK), (1, D_MODEL), (1, KV_RANK), (1, D_MODEL)]
    return pl.pallas_call(
        body, name="latent_proj_bwd", grid=(t // tm,),
        in_specs=[tok(wide), tok(wide), tok(E_B), tok(E_B), tok(Q_RANK), tok(CKR_PAD), tok(D_MODEL), tok(D_MODEL), tok(1),
                  _const_spec(rc.shape)] + [_const_spec(w.shape) for w in weights],
        out_specs=[tok(D_MODEL)] + [_acc_spec(s) for s in acc_shapes],
        out_shape=[jax.ShapeDtypeStruct((t, D_MODEL), F32)] + [jax.ShapeDtypeStruct(s, F32) for s in acc_shapes],
        scratch_shapes=[pltpu.VMEM((tm, wide), BF16), pltpu.VMEM((tm, wide), BF16)],
        compiler_params=_params(1),
    )(dq, dk, dv, dgb, cq, ckr, x1, dx2, pos, rc, *weights)


def _attn_fwd(q, k, v, n_seq, seq):
    tq, tk = ATTN_FWD_TILES
    ratio = tq // tk
    nq = seq // tq
    pair = 2 * HEAD_PAD

    def body(q_ref, k_ref, v_ref, o_ref, lc_ref, lr_ref):
        i = pl.program_id(2)
        row = lax.broadcasted_iota(jnp.int32, (tq, tk), 0)
        col = lax.broadcasted_iota(jnp.int32, (tq, tk), 1)
        qs = [q_ref[:, hh * HEAD_PAD:(hh + 1) * HEAD_PAD] for hh in range(2)]

        def step(j, carry, shift):
            start = pl.multiple_of(j * tk, tk)
            vj = v_ref[pl.ds(start, tk), :]
            out = []
            for hh in range(2):
                m, l, acc = carry[hh]
                s = lax.dot_general(qs[hh], k_ref[pl.ds(start, tk), hh * HEAD_PAD:(hh + 1) * HEAD_PAD], NT,
                                    preferred_element_type=F32) * SOFTMAX_SCALE
                if shift is not None:
                    s = jnp.where(col + shift <= row, s, -jnp.inf)
                m_new = jnp.maximum(m, jnp.max(s, axis=-1, keepdims=True))
                alpha = jnp.exp(m - m_new)
                p = jnp.exp(s - m_new)
                l = alpha * l + jnp.sum(p, axis=-1, keepdims=True)
                acc = alpha * acc + jnp.dot(p.astype(BF16), vj, preferred_element_type=F32)
                out.append((m_new, l, acc))
            return tuple(out)

        one = (jnp.full((tq, 1), -jnp.inf, F32), jnp.zeros((tq, 1), F32), jnp.zeros((tq, HEAD_PAD), F32))
        carry = lax.fori_loop(0, i * ratio, functools.partial(step, shift=None), (one, one))
        for d in range(ratio):
            carry = step(i * ratio + d, carry, d * tk)
        halves = []
        for hh in range(2):
            m, l, acc = carry[hh]
            halves.append(acc / l)
            lse = m + jnp.log(l)
            lc_ref[hh] = lse
            lr_ref[hh] = jnp.broadcast_to(lse, (tq, HEAD_PAD)).T[0:8, :]
        lane = lax.broadcasted_iota(jnp.int32, (tq, HEAD_PAD), 1)
        o_ref[...] = jnp.where(lane < V_HEAD, halves[0], halves[1])

    t = n_seq * seq
    return pl.pallas_call(
        body, name="attention_fwd", grid=(n_seq, N_HEADS // 2, nq),
        in_specs=[pl.BlockSpec((tq, pair), lambda s, p, i: (s * nq + i, p)),
                  pl.BlockSpec((seq, pair), lambda s, p, i: (s, p)),
                  pl.BlockSpec((seq, HEAD_PAD), lambda s, p, i: (s, p))],
        out_specs=[pl.BlockSpec((tq, HEAD_PAD), lambda s, p, i: (s * nq + i, p)),
                   pl.BlockSpec((None, 2, tq, 1), lambda s, p, i: (s, p, i, 0)),
                   pl.BlockSpec((None, 2, 8, tq), lambda s, p, i: (s, p, 0, i))],
        out_shape=[jax.ShapeDtypeStruct((t, E_B), F32), jax.ShapeDtypeStruct((n_seq, N_HEADS, seq, 1), F32),
                   jax.ShapeDtypeStruct((n_seq, N_HEADS, 8, seq), F32)],
        compiler_params=_params(3),
    )(q, k, v)


def _attn_bwd_kv(q, k, v, do, lse_row, delta_row, n_seq, seq):
    tk, tq = ATTN_BWD_KV_TILES
    ratio = tk // tq
    nk = seq // tk
    n_inner = seq // tq
    pair = 2 * HEAD_PAD

    def body(q_ref, k_ref, v_ref, do_ref, lr_ref, dr_ref, dk_ref, dv_ref):
        j = pl.program_id(2)
        row = lax.broadcasted_iota(jnp.int32, (tk, tq), 0)
        col = lax.broadcasted_iota(jnp.int32, (tk, tq), 1)
        lane = lax.broadcasted_iota(jnp.int32, (tk, HEAD_PAD), 1)
        ks = [k_ref[:, hh * HEAD_PAD:(hh + 1) * HEAD_PAD] for hh in range(2)]
        vv = v_ref[...]
        vs = [jnp.where(lane < V_HEAD, vv, jnp.zeros_like(vv)), jnp.where(lane >= V_HEAD, vv, jnp.zeros_like(vv))]

        def step(i, carry, shift):
            start = pl.multiple_of(i * tq, tq)
            doi = do_ref[pl.ds(start, tq), :]
            out = []
            for hh in range(2):
                dk_acc, dv_acc = carry[hh]
                qi = q_ref[pl.ds(start, tq), hh * HEAD_PAD:(hh + 1) * HEAD_PAD]
                st = lax.dot_general(ks[hh], qi, NT, preferred_element_type=F32) * SOFTMAX_SCALE
                pt = jnp.exp(st - lr_ref[hh, 0:1, pl.ds(start, tq)])
                if shift is not None:
                    pt = jnp.where(col + shift >= row, pt, 0.0)
                dv_acc = dv_acc + jnp.dot(pt.astype(BF16), doi, preferred_element_type=F32)
                dpt = lax.dot_general(vs[hh], doi, NT, preferred_element_type=F32)
                dst = pt * (dpt - dr_ref[hh, 0:1, pl.ds(start, tq)]) * SOFTMAX_SCALE
                dk_acc = dk_acc + jnp.dot(dst.astype(BF16), qi, preferred_element_type=F32)
                out.append((dk_acc, dv_acc))
            return tuple(out)

        one = (jnp.zeros((tk, HEAD_PAD), F32), jnp.zeros((tk, HEAD_PAD), F32))
        carry = (one, one)
        for d in range(ratio):
            carry = step(j * ratio + d, carry, d * tq)
        carry = lax.fori_loop((j + 1) * ratio, n_inner, functools.partial(step, shift=None), carry)
        for hh in range(2):
            dk_ref[:, hh * HEAD_PAD:(hh + 1) * HEAD_PAD] = carry[hh][0]
        dv_ref[...] = jnp.where(lane < V_HEAD, carry[0][1], carry[1][1])

    t = n_seq * seq
    return pl.pallas_call(
        body, name="attention_bwd_kv", grid=(n_seq, N_HEADS // 2, nk),
        in_specs=[pl.BlockSpec((seq, pair), lambda s, p, j: (s, p)),
                  pl.BlockSpec((tk, pair), lambda s, p, j: (s * nk + j, p)),
                  pl.BlockSpec((tk, HEAD_PAD), lambda s, p, j: (s * nk + j, p)),
                  pl.BlockSpec((seq, HEAD_PAD), lambda s, p, j: (s, p)),
                  pl.BlockSpec((None, 2, 8, seq), lambda s, p, j: (s, p, 0, 0)),
                  pl.BlockSpec((None, 2, 8, seq), lambda s, p, j: (s, p, 0, 0))],
        out_specs=[pl.BlockSpec((tk, pair), lambda s, p, j: (s * nk + j, p)),
                   pl.BlockSpec((tk, HEAD_PAD), lambda s, p, j: (s * nk + j, p))],
        out_shape=[jax.ShapeDtypeStruct((t, N_HEADS * HEAD_PAD), F32), jax.ShapeDtypeStruct((t, E_B), F32)],
        compiler_params=_params(3),
    )(q, k, v, do, lse_row, delta_row)


def _attn_bwd_q(q, k, v, do, lse_col, delta_col, n_seq, seq):
    tq, tk = ATTN_BWD_Q_TILES
    ratio = tq // tk
    nq = seq // tq
    pair = 2 * HEAD_PAD

    def body(q_ref, k_ref, v_ref, do_ref, lc_ref, dc_ref, dq_ref):
        i = pl.program_id(2)
        row = lax.broadcasted_iota(jnp.int32, (tq, tk), 0)
        col = lax.broadcasted_iota(jnp.int32, (tq, tk), 1)
        lane = lax.broadcasted_iota(jnp.int32, (tq, HEAD_PAD), 1)
        qs = [q_ref[:, hh * HEAD_PAD:(hh + 1) * HEAD_PAD] for hh in range(2)]
        dd = do_ref[...]
        dos = [jnp.where(lane < V_HEAD, dd, jnp.zeros_like(dd)), jnp.where(lane >= V_HEAD, dd, jnp.zeros_like(dd))]
        lses = [lc_ref[hh] for hh in range(2)]
        deltas = [dc_ref[hh] for hh in range(2)]

        def step(j, carry, shift):
            start = pl.multiple_of(j * tk, tk)
            vj = v_ref[pl.ds(start, tk), :]
            out = []
            for hh in range(2):
                kj = k_ref[pl.ds(start, tk), hh * HEAD_PAD:(hh + 1) * HEAD_PAD]
                s = lax.dot_general(qs[hh], kj, NT, preferred_element_type=F32) * SOFTMAX_SCALE
                p = jnp.exp(s - lses[hh])
                if shift is not None:
                    p = jnp.where(col + shift <= row, p, 0.0)
                dp = lax.dot_general(dos[hh], vj, NT, preferred_element_type=F32)
                ds = p * (dp - deltas[hh]) * SOFTMAX_SCALE
                out.append(carry[hh] + jnp.dot(ds.astype(BF16), kj, preferred_element_type=F32))
            return tuple(out)

        zero = jnp.zeros((tq, HEAD_PAD), F32)
        carry = lax.fori_loop(0, i * ratio, functools.partial(step, shift=None), (zero, zero))
        for d in range(ratio):
            carry = step(i * ratio + d, carry, d * tk)
        for hh in range(2):
            dq_ref[:, hh * HEAD_PAD:(hh + 1) * HEAD_PAD] = carry[hh]

    t = n_seq * seq
    return pl.pallas_call(
        body, name="attention_bwd_q", grid=(n_seq, N_HEADS // 2, nq),
        in_specs=[pl.BlockSpec((tq, pair), lambda s, p, i: (s * nq + i, p)),
                  pl.BlockSpec((seq, pair), lambda s, p, i: (s, p)),
                  pl.BlockSpec((seq, HEAD_PAD), lambda s, p, i: (s, p)),
                  pl.BlockSpec((tq, HEAD_PAD), lambda s, p, i: (s * nq + i, p)),
                  pl.BlockSpec((None, 2, tq, 1), lambda s, p, i: (s, p, i, 0)),
                  pl.BlockSpec((None, 2, tq, 1), lambda s, p, i: (s, p, i, 0))],
        out_specs=pl.BlockSpec((tq, pair), lambda s, p, i: (s * nq + i, p)),
        out_shape=jax.ShapeDtypeStruct((t, N_HEADS * HEAD_PAD), F32),
        compiler_params=_params(3),
    )(q, k, v, do, lse_col, delta_col)


def _head_tail(o, gb, x1, w_out, final_norm, target, n_seq, seq):
    tm = ROW_TILE
    nt = seq // tm
    n_col = D_MODEL // N_DEV

    def body(o_ref, gb_ref, x1_ref, wout_ref, fn_ref, tgt_ref,
             dx2_ref, do_ref, dgb_ref, dc_ref, dr_ref, dwout_ref, dfn_ref, loss_ref):
        first = jnp.logical_and(pl.program_id(0) == 0, pl.program_id(1) == 0)

        @pl.when(first)
        def _():
            dwout_ref[...] = jnp.zeros_like(dwout_ref)
            dfn_ref[...] = jnp.zeros_like(dfn_ref)
            loss_ref[...] = jnp.zeros_like(loss_ref)

        ov, g = o_ref[...], gb_ref[...]
        silu, dsilu = _silu_parts(g)
        gated = (ov * silu).astype(BF16)
        x2 = x1_ref[...] + jnp.dot(gated, wout_ref[...], preferred_element_type=F32)
        r = _rstd(x2)
        xh = x2 * r
        err = xh * fn_ref[...] - tgt_ref[...]
        loss_ref[...] += 0.5 * jnp.sum(jnp.mean(err * err, axis=-1, keepdims=True), axis=0, keepdims=True)
        dy = err / D_MODEL
        dfn_ref[...] += jnp.sum(dy * xh, axis=0, keepdims=True)
        dx2 = _norm_bwd(dy * fn_ref[...], xh, r)
        dx2_ref[...] = dx2
        dx2b = dx2.astype(BF16)
        dw = lax.dot_general(gated, dx2b, TN, preferred_element_type=F32)
        for d in range(N_DEV):
            dwout_ref[d] += dw[:, d * n_col:(d + 1) * n_col]
        dgated = lax.dot_general(dx2b, wout_ref[...], NT, preferred_element_type=F32)
        do = dgated * silu
        do_ref[...] = do.astype(BF16)
        dgb_ref[...] = dgated * ov * dsilu
        prod = do * ov
        lane = lax.broadcasted_iota(jnp.int32, (tm, HEAD_PAD), 1)
        for p in range(N_HEADS // 2):
            blk = prod[:, p * HEAD_PAD:(p + 1) * HEAD_PAD]
            for hh in range(2):
                mine = (lane < V_HEAD) if hh == 0 else (lane >= V_HEAD)
                delta = jnp.sum(jnp.where(mine, blk, 0.0), axis=-1, keepdims=True)
                dc_ref[2 * p + hh] = delta
                dr_ref[2 * p + hh] = jnp.broadcast_to(delta, (tm, HEAD_PAD)).T[0:8, :]

    tok = lambda width: pl.BlockSpec((tm, width), lambda s, i: (s * nt + i, 0))
    t = n_seq * seq
    return pl.pallas_call(
        body, name="head_tail", grid=(n_seq, nt),
        in_specs=[tok(E_B), tok(E_B), tok(D_MODEL), _const_spec(w_out.shape), _const_spec((1, D_MODEL)), tok(D_MODEL)],
        out_specs=[tok(D_MODEL), tok(E_B), tok(E_B),
                   pl.BlockSpec((None, N_HEADS, tm, 1), lambda s, i: (s, 0, i, 0)),
                   pl.BlockSpec((None, N_HEADS, 8, tm), lambda s, i: (s, 0, 0, i)),
                   _acc_spec((N_DEV, E_B, n_col)), _acc_spec((1, D_MODEL)), _acc_spec((1, 1))],
        out_shape=[jax.ShapeDtypeStruct((t, D_MODEL), F32), jax.ShapeDtypeStruct((t, E_B), BF16),
                   jax.ShapeDtypeStruct((t, E_B), F32), jax.ShapeDtypeStruct((n_seq, N_HEADS, seq, 1), F32),
                   jax.ShapeDtypeStruct((n_seq, N_HEADS, 8, seq), F32), jax.ShapeDtypeStruct((N_DEV, E_B, n_col), F32),
                   jax.ShapeDtypeStruct((1, D_MODEL), F32), jax.ShapeDtypeStruct((1, 1), F32)],
        compiler_params=_params(2),
    )(o, gb, x1, w_out, final_norm, target)


def _adamw_math(w, g, m, v):
    m = ADAM_B1 * m + (1.0 - ADAM_B1) * g
    v = ADAM_B2 * v + (1.0 - ADAM_B2) * jnp.square(g)
    m_hat = m / (1.0 - ADAM_B1 ** ADAM_STEP)
    v_hat = v / (1.0 - ADAM_B2 ** ADAM_STEP)
    delta = -ADAM_LR * (m_hat / (jnp.sqrt(v_hat) + ADAM_EPS) + ADAM_WD * w)
    return delta, m, v


def _adamw_reduce(parts, w, m, v):
    rows, cols = w.shape
    br = 256 if rows % 256 == 0 else 128

    def body(p_ref, w_ref, m_ref, v_ref, g_ref, d_ref, nm_ref, nv_ref):
        g = p_ref[0]
        for k in range(1, N_DEV):
            g = g + p_ref[k]
        g_ref[...] = g
        d_ref[...], nm_ref[...], nv_ref[...] = _adamw_math(w_ref[...], g, m_ref[...], v_ref[...])

    blk = pl.BlockSpec((br, cols), lambda i: (i, 0))
    return pl.pallas_call(
        body, name="adamw_reduce", grid=(rows // br,),
        in_specs=[pl.BlockSpec((N_DEV, br, cols), lambda i: (0, i, 0)), blk, blk, blk],
        out_specs=[blk] * 4, out_shape=[jax.ShapeDtypeStruct((rows, cols), F32)] * 4,
        compiler_params=_params(1),
    )(parts, w, m, v)


def _sum_parts(parts):
    def body(p_ref, o_ref):
        g = p_ref[0]
        for k in range(1, N_DEV):
            g = g + p_ref[k]
        o_ref[...] = g

    return pl.pallas_call(body, name="small_grad_sum", out_shape=jax.ShapeDtypeStruct(parts.shape[1:], F32))(parts)


def _adamw_small(gs, ws, ms, vs):
    n = len(gs)

    def body(*refs):
        ins, outs = refs[:4 * n], refs[4 * n:]
        for j in range(n):
            g_ref, w_ref, m_ref, v_ref = ins[j], ins[n + j], ins[2 * n + j], ins[3 * n + j]
            outs[3 * j][...], outs[3 * j + 1][...], outs[3 * j + 2][...] = _adamw_math(
                w_ref[...], g_ref[...], m_ref[...], v_ref[...])

    shapes = [jax.ShapeDtypeStruct(g.shape, F32) for g in gs for _ in range(3)]
    flat = pl.pallas_call(body, name="adamw_small", out_shape=shapes)(*gs, *ws, *ms, *vs)
    return [tuple(flat[3 * j:3 * j + 3]) for j in range(n)]


SMALL_ROWS = 8
LOSS_LANE = D_MODEL - 1


def _pack_small(g_fn, g_kvn, g_bn, g_an, g_ckvn, g_qn, g_conv, loss_part):
    def body(fn_ref, kvn_ref, bn_ref, an_ref, ckvn_ref, qn_ref, conv_ref, loss_ref, o_ref):
        o_ref[0:1, :] = fn_ref[...]
        o_ref[1:2, :] = kvn_ref[...]
        o_ref[2:3, :] = bn_ref[...]
        o_ref[3:4, :] = an_ref[...]
        lane = lax.broadcasted_iota(jnp.int32, (1, D_MODEL), 1)
        o_ref[4:5, :] = jnp.where(lane == LOSS_LANE, loss_ref[...], 0.0)
        o_ref[4:5, 0:KV_RANK] = ckvn_ref[...]
        o_ref[4:5, KV_RANK:KV_RANK + Q_RANK] = qn_ref[...]
        o_ref[5:8, :] = conv_ref[0:3, :]

    return pl.pallas_call(body, name="pack_small", out_shape=jax.ShapeDtypeStruct((SMALL_ROWS, D_MODEL), F32))(
        g_fn, g_kvn, g_bn, g_an, g_ckvn, g_qn, g_conv, loss_part)


def _pad_cols(a, width):
    return jnp.pad(a, ((0, 0), (0, width - a.shape[1])))


def _dkv_to_padded(a):
    r = a.shape[0]
    z = jnp.zeros((r, ROPE_LO), a.dtype)
    z2 = jnp.zeros((r, HEAD_PAD - ROPE_LO - QK_ROPE), a.dtype)
    return jnp.concatenate([a[:, :KV_RANK], z, a[:, KV_RANK:], z2], axis=1)


def _dkv_from_padded(a):
    return jnp.concatenate([a[:, :KV_RANK], a[:, KV_RANK + ROPE_LO:KV_RANK + ROPE_LO + QK_ROPE]], axis=1)


def _unstack_cols(a):
    return jnp.transpose(a, (1, 0, 2)).reshape(a.shape[1], N_DEV * a.shape[2])


def kernel(x, positions, a_norm, a_w_in, a_conv, a_w_out, kv_norm, w_dkv, ckv_norm, w_ukv, b_norm, b_w_in, b_q_norm, b_w_uq, b_w_out, final_norm, loss_target, m_a_norm, m_a_w_in, m_a_conv, m_a_w_out, m_kv_norm, m_w_dkv, m_ckv_norm, m_w_ukv, m_b_norm, m_b_w_in, m_b_q_norm, m_b_w_uq, m_b_w_out, m_final_norm, v_a_norm, v_a_w_in, v_a_conv, v_a_w_out, v_kv_norm, v_w_dkv, v_ckv_norm, v_w_ukv, v_b_norm, v_b_w_in, v_b_q_norm, v_b_w_uq, v_b_w_out, v_final_norm):
    n_seq, seq, _ = x.shape
    t = n_seq * seq
    me = 4 * lax.axis_index("x") + 2 * lax.axis_index("y") + lax.axis_index("c")

    big = {
        "a_w_in": (a_w_in[0], m_a_w_in[0], v_a_w_in[0]),
        "a_w_out": (a_w_out[0], m_a_w_out[0], v_a_w_out[0]),
        "w_dkv": tuple(_dkv_to_padded(a) for a in (w_dkv, m_w_dkv, v_w_dkv)),
        "w_ukv": (w_ukv, m_w_ukv, v_w_ukv),
        "b_w_in": (b_w_in[0], m_b_w_in[0], v_b_w_in[0]),
        "b_w_uq": tuple(_pad_cols(a[0], HEAD_PAD) for a in (b_w_uq, m_b_w_uq, v_b_w_uq)),
        "b_w_out": (b_w_out[0], m_b_w_out[0], v_b_w_out[0]),
    }
    names = list(big)
    gathered = _all_gather([big[n][0].astype(BF16) for n in names] + [a_norm, a_conv[0]])
    full = dict(zip(names, gathered[:-2]))
    a_norm_f = gathered[-2].reshape(1, D_MODEL)
    a_conv_f = _unstack_cols(gathered[-1])
    w_a_in = full["a_w_in"]
    w_a_out = full["a_w_out"].reshape(D_MODEL, D_MODEL)
    w_dkv_f = full["w_dkv"].reshape(D_MODEL, CKR_PAD)
    w_ukv_f = _unstack_cols(full["w_ukv"])
    w_b_in = full["b_w_in"].reshape(D_MODEL, Q_RANK + E_B)
    w_uq_f = _unstack_cols(full["b_w_uq"])
    w_b_out = _unstack_cols(full["b_w_out"])

    x2d = x.reshape(t, D_MODEL)
    tgt = loss_target.reshape(t, D_MODEL)
    pos = positions.astype(F32).reshape(t, 1)
    rc = _rope_consts()
    kvn, ckvn, bn, qn, fn = kv_norm.reshape(1, -1), ckv_norm.reshape(1, -1), b_norm, b_q_norm, final_norm.reshape(1, -1)
    layer_b = (kvn, w_dkv_f, ckvn, w_ukv_f, bn, w_b_in, qn, w_uq_f)

    x1, proj, conv = _conv_fwd(x2d, a_norm_f, w_a_in, a_conv_f, w_a_out, n_seq, seq)
    ckr, cq, gb, q, k, v = _proj_fwd(x1, pos, rc, *layer_b)
    o, lse_col, lse_row = _attn_fwd(q, k, v, n_seq, seq)
    dx2, do, dgb, delta_col, delta_row, g_b_out, g_fn, loss_part = _head_tail(o, gb, x1, w_b_out, fn, tgt, n_seq, seq)
    dk, dv = _attn_bwd_kv(q, k, v, do, lse_row, delta_row, n_seq, seq)
    dq = _attn_bwd_q(q, k, v, do, lse_col, delta_col, n_seq, seq)
    dx1, g_uq, g_b_in, g_ukv, g_dkv, g_qn, g_bn, g_ckvn, g_kvn = _proj_bwd(
        dq, dk, dv, dgb, cq, ckr, x1, dx2, pos, rc, *layer_b)
    dx, h_a, dproj, g_a_out, g_an, g_conv = _conv_bwd(dx1, x2d, proj, conv, a_norm_f, w_a_in, a_conv_f, w_a_out, n_seq, seq)
    g_a_in = _stacked_weight_grad(h_a, dproj, D_MODEL * 4 // N_DEV)

    stacks = {
        "a_w_in": g_a_in,
        "a_w_out": g_a_out.reshape(N_DEV, D_MODEL // N_DEV, D_MODEL),
        "w_dkv": g_dkv.reshape(N_DEV, D_MODEL // N_DEV, CKR_PAD),
        "w_ukv": g_ukv,
        "b_w_in": g_b_in.reshape(N_DEV, D_MODEL // N_DEV, Q_RANK + E_B),
        "b_w_uq": g_uq,
        "b_w_out": g_b_out,
    }
    small = _pack_small(g_fn, g_kvn, g_bn, g_an, g_ckvn, g_qn, g_conv, loss_part)
    parts, small_parts = _grad_exchange([stacks[n] for n in names], small)

    outs = {}
    for n, p in zip(names, parts):
        w, m, vv = big[n]
        outs[n] = _adamw_reduce(p, w, m, vv)
    outs["w_dkv"] = tuple(_dkv_from_padded(a) for a in outs["w_dkv"])
    outs["b_w_uq"] = tuple(a[:, :QK_NOPE + QK_ROPE] for a in outs["b_w_uq"])
    for n in ("a_w_in", "a_w_out", "b_w_in", "b_w_uq", "b_w_out"):
        outs[n] = tuple(a[None] for a in outs[n])

    total = _sum_parts(small_parts)
    loss = total[4, LOSS_LANE]
    shard = D_MODEL // N_DEV
    g_small = {
        "final_norm": total[0], "kv_norm": total[1], "b_norm": total[2:3],
        "a_norm": lax.dynamic_slice_in_dim(total[3:4], me * shard, shard, axis=1),
        "ckv_norm": total[4, 0:KV_RANK], "b_q_norm": total[4:5, KV_RANK:KV_RANK + Q_RANK],
        "a_conv": lax.dynamic_slice_in_dim(total[5:8], me * shard, shard, axis=1)[None],
    }
    small_state = {
        "final_norm": (final_norm, m_final_norm, v_final_norm), "kv_norm": (kv_norm, m_kv_norm, v_kv_norm),
        "b_norm": (b_norm, m_b_norm, v_b_norm), "a_norm": (a_norm, m_a_norm, v_a_norm),
        "ckv_norm": (ckv_norm, m_ckv_norm, v_ckv_norm), "b_q_norm": (b_q_norm, m_b_q_norm, v_b_q_norm),
        "a_conv": (a_conv, m_a_conv, v_a_conv),
    }
    small_names = list(g_small)
    as2d = lambda a: a.reshape(-1, a.shape[-1])
    upd = _adamw_small([as2d(g_small[n]) for n in small_names],
                       *[[as2d(small_state[n][j]) for n in small_names] for j in range(3)])
    for n, u in zip(small_names, upd):
        outs[n] = (g_small[n],) + tuple(a.reshape(g_small[n].shape) for a in u)

    order = ["a_norm", "a_w_in", "a_conv", "a_w_out", "kv_norm", "w_dkv", "ckv_norm", "w_ukv", "b_norm", "b_w_in",
             "b_q_norm", "b_w_uq", "b_w_out", "final_norm"]
    result = [loss, dx.reshape(n_seq, seq, D_MODEL)]
    for j in range(4):
        result += [outs[n][j] for n in order]
    return tuple(result)
```

```python
import functools
import math

import numpy as np
import jax
import jax.numpy as jnp
from jax import lax
from jax.experimental import pallas as pl
from jax.experimental.pallas import tpu as pltpu

F32 = jnp.float32
BF16 = jnp.bfloat16

D_MODEL = 1024
N_HEADS = 8
QK_NOPE = 64
QK_ROPE = 32
V_HEAD = 64
KV_RANK = 256
Q_RANK = 384
E_B = N_HEADS * V_HEAD
HEAD_PAD = 128
CKR_PAD = KV_RANK + HEAD_PAD
ROPE_LO = QK_NOPE
ROPE_HALF = QK_ROPE // 2
ROPE_THETA = 10000.0
SOFTMAX_SCALE = 1.0 / math.sqrt(QK_NOPE + QK_ROPE)
EPS = 1e-6
N_DEV = 8

ADAM_LR = 0.001
ADAM_B1 = 0.9
ADAM_B2 = 0.999
ADAM_EPS = 1e-08
ADAM_WD = 0.01
ADAM_STEP = 10

ROW_TILE = 256
ATTN_FWD_TILES = (512, 512)
ATTN_BWD_TILES = (512, 512)
VMEM_LIMIT = 56 * 1024 * 1024

MESH = pl.DeviceIdType.MESH
NT = (((1,), (1,)), ((), ()))
TN = (((0,), (0,)), ((), ()))


def _dot(a, b):
    return jnp.dot(a.astype(BF16), b.astype(BF16), preferred_element_type=F32)


def _dot_nt(a, b):
    return lax.dot_general(a.astype(BF16), b.astype(BF16), NT, preferred_element_type=F32)


def _dot_tn(a, b):
    return lax.dot_general(a.astype(BF16), b.astype(BF16), TN, preferred_element_type=F32)


def _rstd(x):
    return lax.rsqrt(jnp.mean(x * x, axis=-1, keepdims=True) + EPS)


def _norm_bwd(a, xh, r):
    return r * (a - xh * jnp.mean(a * xh, axis=-1, keepdims=True))


def _silu_parts(g):
    sg = jax.nn.sigmoid(g)
    return g * sg, sg * (1.0 + g * (1.0 - sg))


def _rope_consts():
    inv = (ROPE_THETA ** (-np.arange(0, QK_ROPE, 2, dtype=np.float32) / QK_ROPE)).astype(np.float32)
    t = np.zeros((8, HEAD_PAD), np.float32)
    t[0, ROPE_LO:ROPE_LO + ROPE_HALF] = inv
    t[0, ROPE_LO + ROPE_HALF:ROPE_LO + QK_ROPE] = inv
    t[1, ROPE_LO:ROPE_LO + ROPE_HALF] = -1.0
    t[2, ROPE_LO + ROPE_HALF:ROPE_LO + QK_ROPE] = 1.0
    return jnp.asarray(t)


def _rope_tables(pos, rc):
    ang = pos * rc[0:1, :]
    cosv = jnp.cos(ang)
    sinv = jnp.sin(ang)
    return cosv, sinv * rc[1:2, :], sinv * rc[2:3, :]


def _rope(x, ct, s1, s2):
    up = pltpu.roll(x, HEAD_PAD - ROPE_HALF, 1)
    dn = pltpu.roll(x, ROPE_HALF, 1)
    return x * ct + up * s1 + dn * s2


def _const_spec(shape):
    nd = len(shape)
    return pl.BlockSpec(shape, lambda *_: (0,) * nd, pipeline_mode=pl.Buffered(1))


def _acc_spec(shape):
    nd = len(shape)
    return pl.BlockSpec(shape, lambda *_: (0,) * nd)


def _params(n_axes):
    return pltpu.CompilerParams(dimension_semantics=("arbitrary",) * n_axes, vmem_limit_bytes=VMEM_LIMIT)


def _place():
    x, y, c = lax.axis_index("x"), lax.axis_index("y"), lax.axis_index("c")
    return x, y, c, 4 * x + 2 * y + c


def _peer(x, y, c, mask):
    px = 1 - x if mask & 4 else x
    py = 1 - y if mask & 2 else y
    pc = 1 - c if mask & 1 else c
    return (px, py, pc), 4 * px + 2 * py + pc


def _all_gather(shards):
    n = len(shards)

    def body(*refs):
        ins, outs = refs[:n], refs[n:2 * n]
        send_sems, recv_sems, local_sems = refs[2 * n:]
        x, y, c, me = _place()
        copies = []
        for w in range(n):
            mine = pltpu.make_async_copy(ins[w], outs[w].at[me], local_sems.at[w])
            mine.start()
            copies.append(mine)
            for mask in range(1, N_DEV):
                peer, peer_idx = _peer(x, y, c, mask)
                send = pltpu.make_async_remote_copy(
                    src_ref=ins[w], dst_ref=outs[w].at[me], send_sem=send_sems.at[w, mask - 1],
                    recv_sem=recv_sems.at[w, mask - 1], device_id=peer, device_id_type=MESH)
                send.start()
                recv = pltpu.make_async_remote_copy(
                    src_ref=ins[w], dst_ref=outs[w].at[peer_idx], send_sem=send_sems.at[w, mask - 1],
                    recv_sem=recv_sems.at[w, mask - 1], device_id=peer, device_id_type=MESH)
                copies.append(recv)
        for cp in copies:
            cp.wait()

    any_spec = pl.BlockSpec(memory_space=pl.ANY)
    return pl.pallas_call(
        body, name="weight_all_gather",
        out_shape=[jax.ShapeDtypeStruct((N_DEV,) + s.shape, s.dtype) for s in shards],
        in_specs=[any_spec] * n, out_specs=[any_spec] * n,
        scratch_shapes=[pltpu.SemaphoreType.DMA((n, N_DEV - 1)), pltpu.SemaphoreType.DMA((n, N_DEV - 1)),
                        pltpu.SemaphoreType.DMA((n,))],
    )(*shards)


def _grad_exchange(stacks, small):
    n = len(stacks)

    def body(*refs):
        ins, small_in = refs[:n], refs[n]
        outs, small_out = refs[n + 1:2 * n + 1], refs[2 * n + 1]
        send_sems, recv_sems, local_sems = refs[2 * n + 2:]
        x, y, c, me = _place()
        copies = []
        for w in range(n + 1):
            if w < n:
                mine = pltpu.make_async_copy(ins[w].at[me], outs[w].at[0], local_sems.at[w])
            else:
                mine = pltpu.make_async_copy(small_in, small_out.at[me], local_sems.at[w])
            mine.start()
            copies.append(mine)
            for mask in range(1, N_DEV):
                peer, peer_idx = _peer(x, y, c, mask)
                if w < n:
                    src, dst, landed = ins[w].at[peer_idx], outs[w].at[mask], outs[w].at[mask]
                else:
                    src, dst, landed = small_in, small_out.at[me], small_out.at[peer_idx]
                send = pltpu.make_async_remote_copy(
                    src_ref=src, dst_ref=dst, send_sem=send_sems.at[w, mask - 1],
                    recv_sem=recv_sems.at[w, mask - 1], device_id=peer, device_id_type=MESH)
                send.start()
                recv = pltpu.make_async_remote_copy(
                    src_ref=src, dst_ref=landed, send_sem=send_sems.at[w, mask - 1],
                    recv_sem=recv_sems.at[w, mask - 1], device_id=peer, device_id_type=MESH)
                copies.append(recv)
        for cp in copies:
            cp.wait()

    any_spec = pl.BlockSpec(memory_space=pl.ANY)
    outs = pl.pallas_call(
        body, name="grad_exchange",
        out_shape=[jax.ShapeDtypeStruct(s.shape, s.dtype) for s in stacks]
        + [jax.ShapeDtypeStruct((N_DEV,) + small.shape, small.dtype)],
        in_specs=[any_spec] * (n + 1), out_specs=[any_spec] * (n + 1),
        scratch_shapes=[pltpu.SemaphoreType.DMA((n + 1, N_DEV - 1)), pltpu.SemaphoreType.DMA((n + 1, N_DEV - 1)),
                        pltpu.SemaphoreType.DMA((n + 1,))],
    )(*stacks, small)
    return outs[:n], outs[n]


def _conv_fwd(x, a_norm, w_in, conv_w, w_out, n_seq, seq):
    tm = ROW_TILE
    nt = seq // tm
    n_col = w_in.shape[2]

    def body(x_ref, an_ref, win_ref, cw_ref, wout_ref, x1_ref, proj_ref, conv_ref, prev_ref):
        @pl.when(pl.program_id(1) == 0)
        def _():
            prev_ref[...] = jnp.zeros_like(prev_ref)

        xv = x_ref[...]
        h = (xv * _rstd(xv) * an_ref[...]).astype(BF16)
        for d in range(N_DEV):
            proj_ref[:, d * n_col:(d + 1) * n_col] = jnp.dot(h, win_ref[d], preferred_element_type=F32)
        b = proj_ref[:, 0:D_MODEL]
        v = proj_ref[:, D_MODEL:2 * D_MODEL] * proj_ref[:, 2 * D_MODEL:3 * D_MODEL]
        g = proj_ref[:, 3 * D_MODEL:4 * D_MODEL]
        w0, w1, w2 = cw_ref[0:1, :], cw_ref[1:2, :], cw_ref[2:3, :]
        conv_ref[...] = w0 * pltpu.roll(v, 2, 0) + w1 * pltpu.roll(v, 1, 0) + w2 * v
        rows = lax.broadcasted_iota(jnp.int32, (8, D_MODEL), 0)
        p8, v8 = prev_ref[...], v[0:8]
        back1 = jnp.where(rows < 1, pltpu.roll(p8, 1, 0), pltpu.roll(v8, 1, 0))
        back2 = jnp.where(rows < 2, pltpu.roll(p8, 2, 0), pltpu.roll(v8, 2, 0))
        conv_ref[0:8, :] = w0 * back2 + w1 * back1 + w2 * v8
        prev_ref[...] = v[tm - 8:tm]
        silu, _ = _silu_parts(g)
        yv = silu * b * conv_ref[...]
        x1_ref[...] = xv + _dot(yv, wout_ref[...])

    tok = lambda width: pl.BlockSpec((tm, width), lambda s, i: (s * nt + i, 0))
    t = n_seq * seq
    return pl.pallas_call(
        body, name="conv_mixer_fwd", grid=(n_seq, nt),
        in_specs=[tok(D_MODEL), _const_spec((1, D_MODEL)), _const_spec(w_in.shape), _const_spec((3, D_MODEL)),
                  _const_spec(w_out.shape)],
        out_specs=[tok(D_MODEL), tok(4 * D_MODEL), tok(D_MODEL)],
        out_shape=[jax.ShapeDtypeStruct((t, D_MODEL), F32), jax.ShapeDtypeStruct((t, 4 * D_MODEL), F32),
                   jax.ShapeDtypeStruct((t, D_MODEL), F32)],
        scratch_shapes=[pltpu.VMEM((8, D_MODEL), F32)],
        compiler_params=_params(2),
    )(x, a_norm, w_in, conv_w, w_out)


def _conv_bwd(dx1, x, proj, conv, a_norm, w_in, conv_w, w_out, n_seq, seq):
    tm = ROW_TILE
    nt = seq // tm
    n_col = w_in.shape[2]

    def body(dx1_ref, x_ref, proj_ref, conv_ref, an_ref, win_ref, cw_ref, wout_ref,
             dx_ref, h_ref, dproj_ref, dwout_ref, dan_ref, dcw_ref, next_ref, d1_ref, d2_ref):
        first = jnp.logical_and(pl.program_id(0) == 0, pl.program_id(1) == 0)

        @pl.when(first)
        def _():
            dwout_ref[...] = jnp.zeros_like(dwout_ref)
            dan_ref[...] = jnp.zeros_like(dan_ref)
            dcw_ref[...] = jnp.zeros_like(dcw_ref)

        @pl.when(pl.program_id(1) == 0)
        def _():
            next_ref[...] = jnp.zeros_like(next_ref)

        dx1v = dx1_ref[...]
        dy = _dot_nt(dx1v, wout_ref[...])
        b = proj_ref[:, 0:D_MODEL]
        cc = proj_ref[:, D_MODEL:2 * D_MODEL]
        u = proj_ref[:, 2 * D_MODEL:3 * D_MODEL]
        g = proj_ref[:, 3 * D_MODEL:4 * D_MODEL]
        cv = conv_ref[...]
        silu, dsilu = _silu_parts(g)
        dwout_ref[...] += _dot_tn(silu * b * cv, dx1v)
        dproj_ref[:, 3 * D_MODEL:4 * D_MODEL] = (dy * b * cv * dsilu).astype(BF16)
        dproj_ref[:, 0:D_MODEL] = (dy * silu * cv).astype(BF16)
        dconv = dy * silu * b
        d1_ref[...] = pltpu.roll(dconv, tm - 1, 0)
        d2_ref[...] = pltpu.roll(dconv, tm - 2, 0)
        rows = lax.broadcasted_iota(jnp.int32, (8, D_MODEL), 0)
        n8, c8 = next_ref[...], dconv[tm - 8:tm]
        d1_ref[tm - 8:tm, :] = jnp.where(rows >= 7, pltpu.roll(n8, 7, 0), pltpu.roll(c8, 7, 0))
        d2_ref[tm - 8:tm, :] = jnp.where(rows >= 6, pltpu.roll(n8, 6, 0), pltpu.roll(c8, 6, 0))
        next_ref[...] = dconv[0:8]
        d1, d2 = d1_ref[...], d2_ref[...]
        v = cc * u
        dcw_ref[0:1, :] += jnp.sum(d2 * v, axis=0, keepdims=True)
        dcw_ref[1:2, :] += jnp.sum(d1 * v, axis=0, keepdims=True)
        dcw_ref[2:3, :] += jnp.sum(dconv * v, axis=0, keepdims=True)
        dv = cw_ref[0:1, :] * d2 + cw_ref[1:2, :] * d1 + cw_ref[2:3, :] * dconv
        dproj_ref[:, D_MODEL:2 * D_MODEL] = (dv * u).astype(BF16)
        dproj_ref[:, 2 * D_MODEL:3 * D_MODEL] = (dv * cc).astype(BF16)
        dh = jnp.zeros((tm, D_MODEL), F32)
        for d in range(N_DEV):
            dh += lax.dot_general(dproj_ref[:, d * n_col:(d + 1) * n_col], win_ref[d], NT, preferred_element_type=F32)
        xv = x_ref[...]
        r = _rstd(xv)
        xh = xv * r
        h_ref[...] = (xh * an_ref[...]).astype(BF16)
        dan_ref[...] += jnp.sum(dh * xh, axis=0, keepdims=True)
        dx_ref[...] = dx1v + _norm_bwd(dh * an_ref[...], xh, r)

    tok = lambda width: pl.BlockSpec((tm, width), lambda s, i: (s * nt + nt - 1 - i, 0))
    t = n_seq * seq
    return pl.pallas_call(
        body, name="conv_mixer_bwd", grid=(n_seq, nt),
        in_specs=[tok(D_MODEL), tok(D_MODEL), tok(4 * D_MODEL), tok(D_MODEL), _const_spec((1, D_MODEL)),
                  _const_spec(w_in.shape), _const_spec((3, D_MODEL)), _const_spec(w_out.shape)],
        out_specs=[tok(D_MODEL), tok(D_MODEL), tok(4 * D_MODEL), _acc_spec((D_MODEL, D_MODEL)),
                   _acc_spec((1, D_MODEL)), _acc_spec((8, D_MODEL))],
        out_shape=[jax.ShapeDtypeStruct((t, D_MODEL), F32), jax.ShapeDtypeStruct((t, D_MODEL), BF16),
                   jax.ShapeDtypeStruct((t, 4 * D_MODEL), BF16), jax.ShapeDtypeStruct((D_MODEL, D_MODEL), F32),
                   jax.ShapeDtypeStruct((1, D_MODEL), F32), jax.ShapeDtypeStruct((8, D_MODEL), F32)],
        scratch_shapes=[pltpu.VMEM((8, D_MODEL), F32), pltpu.VMEM((tm, D_MODEL), F32), pltpu.VMEM((tm, D_MODEL), F32)],
        compiler_params=_params(2),
    )(dx1, x, proj, conv, a_norm, w_in, conv_w, w_out)


def _stacked_weight_grad(a, b, n_col):
    t, r = a.shape
    bt = 512

    def body(a_ref, b_ref, o_ref):
        @pl.when(pl.program_id(1) == 0)
        def _():
            o_ref[...] = jnp.zeros_like(o_ref)

        o_ref[...] += lax.dot_general(a_ref[...], b_ref[...], TN, preferred_element_type=F32)

    return pl.pallas_call(
        body, name="stacked_weight_grad", grid=(N_DEV, t // bt),
        in_specs=[pl.BlockSpec((bt, r), lambda d, k: (k, 0)), pl.BlockSpec((bt, n_col), lambda d, k: (k, d))],
        out_specs=pl.BlockSpec((None, r, n_col), lambda d, k: (d, 0, 0)),
        out_shape=jax.ShapeDtypeStruct((N_DEV, r, n_col), F32),
        compiler_params=_params(2),
    )(a, b)


def _proj_fwd(x1, pos, rc, kv_norm, w_dkv, ckv_norm, w_ukv, b_norm, b_w_in, q_norm, w_uq):
    t = x1.shape[0]
    tm = ROW_TILE

    def body(x1_ref, pos_ref, rc_ref, kvn_ref, wdkv_ref, ckvn_ref, wukv_ref, bn_ref, bwin_ref, qn_ref, wuq_ref,
             ckr_ref, cq_ref, gb_ref, q_ref, k_ref, v_ref):
        ct, s1, s2 = _rope_tables(pos_ref[...], rc_ref[...])
        xv = x1_ref[...]
        xh = xv * _rstd(xv)
        ckr = _dot(xh * kvn_ref[...], wdkv_ref[...])
        ckr_ref[...] = ckr
        ckv = ckr[:, 0:KV_RANK]
        kv = _dot(ckv * _rstd(ckv) * ckvn_ref[...], wukv_ref[...])
        k_rope = _rope(ckr[:, KV_RANK:CKR_PAD], ct, s1, s2)
        lane = lax.broadcasted_iota(jnp.int32, (tm, HEAD_PAD), 1)
        low = lane < QK_NOPE
        for h in range(N_HEADS):
            k_ref[:, h * HEAD_PAD:(h + 1) * HEAD_PAD] = jnp.where(
                low, kv[:, h * HEAD_PAD:(h + 1) * HEAD_PAD], k_rope).astype(BF16)
        for p in range(N_HEADS // 2):
            even = kv[:, (2 * p) * HEAD_PAD:(2 * p + 1) * HEAD_PAD]
            odd = kv[:, (2 * p + 1) * HEAD_PAD:(2 * p + 2) * HEAD_PAD]
            v_ref[:, p * HEAD_PAD:(p + 1) * HEAD_PAD] = jnp.where(low, pltpu.roll(even, V_HEAD, 1), odd).astype(BF16)
        pb = _dot(xh * bn_ref[...], bwin_ref[...])
        cq = pb[:, 0:Q_RANK]
        cq_ref[...] = cq
        gb_ref[...] = pb[:, Q_RANK:Q_RANK + E_B]
        q = _dot(cq * _rstd(cq) * qn_ref[...], wuq_ref[...])
        for h in range(N_HEADS):
            q_ref[:, h * HEAD_PAD:(h + 1) * HEAD_PAD] = _rope(
                q[:, h * HEAD_PAD:(h + 1) * HEAD_PAD], ct, s1, s2).astype(BF16)

    tok = lambda width: pl.BlockSpec((tm, width), lambda i: (i, 0))
    weights = [kv_norm, w_dkv, ckv_norm, w_ukv, b_norm, b_w_in, q_norm, w_uq]
    wide = N_HEADS * HEAD_PAD
    return pl.pallas_call(
        body, name="latent_proj_fwd", grid=(t // tm,),
        in_specs=[tok(D_MODEL), tok(1), _const_spec(rc.shape)] + [_const_spec(w.shape) for w in weights],
        out_specs=[tok(CKR_PAD), tok(Q_RANK), tok(E_B), tok(wide), tok(wide), tok(E_B)],
        out_shape=[jax.ShapeDtypeStruct((t, CKR_PAD), F32), jax.ShapeDtypeStruct((t, Q_RANK), F32),
                   jax.ShapeDtypeStruct((t, E_B), F32), jax.ShapeDtypeStruct((t, wide), BF16),
                   jax.ShapeDtypeStruct((t, wide), BF16), jax.ShapeDtypeStruct((t, E_B), BF16)],
        compiler_params=_params(1),
    )(x1, pos, rc, *weights)


def _proj_bwd(dq, dk, dv, dgb, cq, ckr, x1, dx2, pos, rc, kv_norm, w_dkv, ckv_norm, w_ukv, b_norm, b_w_in, q_norm, w_uq):
    t = x1.shape[0]
    tm = ROW_TILE
    wide = N_HEADS * HEAD_PAD

    def body(dq_ref, dk_ref, dv_ref, dgb_ref, cq_ref, ckr_ref, x1_ref, dx2_ref, pos_ref, rc_ref,
             kvn_ref, wdkv_ref, ckvn_ref, wukv_ref, bn_ref, bwin_ref, qn_ref, wuq_ref,
             dx1_ref, dwuq_ref, dwbin_ref, dwukv_ref, dwdkv_ref, dqn_ref, dbn_ref, dckvn_ref, dkvn_ref, dqu_ref, dkv_ref):
        @pl.when(pl.program_id(0) == 0)
        def _():
            for ref in (dwuq_ref, dwbin_ref, dwukv_ref, dwdkv_ref, dqn_ref, dbn_ref, dckvn_ref, dkvn_ref):
                ref[...] = jnp.zeros_like(ref)

        ct, s1, s2 = _rope_tables(pos_ref[...], rc_ref[...])
        lane = lax.broadcasted_iota(jnp.int32, (tm, HEAD_PAD), 1)
        low = lane < QK_NOPE
        for h in range(N_HEADS):
            dqu_ref[:, h * HEAD_PAD:(h + 1) * HEAD_PAD] = _rope(
                dq_ref[:, h * HEAD_PAD:(h + 1) * HEAD_PAD], ct, -s1, -s2).astype(BF16)
        cq = cq_ref[...]
        rq = _rstd(cq)
        cqh = cq * rq
        cqn = (cqh * qn_ref[...]).astype(BF16)
        for h in range(N_HEADS):
            dwuq_ref[h] += lax.dot_general(cqn, dqu_ref[:, h * HEAD_PAD:(h + 1) * HEAD_PAD], TN, preferred_element_type=F32)
        dcqn = lax.dot_general(dqu_ref[...], wuq_ref[...], NT, preferred_element_type=F32)
        dqn_ref[...] += jnp.sum(dcqn * cqh, axis=0, keepdims=True)
        dcq = _norm_bwd(dcqn * qn_ref[...], cqh, rq)
        dpb = jnp.concatenate([dcq, dgb_ref[...]], axis=1).astype(BF16)
        xv = x1_ref[...]
        r = _rstd(xv)
        xh = xv * r
        dwbin_ref[...] += _dot_tn(xh * bn_ref[...], dpb)
        dh3 = lax.dot_general(dpb, bwin_ref[...], NT, preferred_element_type=F32)
        dk_rope = jnp.zeros((tm, HEAD_PAD), F32)
        for p in range(N_HEADS // 2):
            dvp = dv_ref[:, p * HEAD_PAD:(p + 1) * HEAD_PAD]
            dk_even = dk_ref[:, (2 * p) * HEAD_PAD:(2 * p + 1) * HEAD_PAD]
            dk_odd = dk_ref[:, (2 * p + 1) * HEAD_PAD:(2 * p + 2) * HEAD_PAD]
            dkv_ref[:, (2 * p) * HEAD_PAD:(2 * p + 1) * HEAD_PAD] = jnp.where(low, dk_even, pltpu.roll(dvp, V_HEAD, 1)).astype(BF16)
            dkv_ref[:, (2 * p + 1) * HEAD_PAD:(2 * p + 2) * HEAD_PAD] = jnp.where(low, dk_odd, dvp).astype(BF16)
            dk_rope += dk_even + dk_odd
        rope_lanes = jnp.logical_and(lane >= ROPE_LO, lane < ROPE_LO + QK_ROPE)
        dk_rope = jnp.where(rope_lanes, _rope(dk_rope, ct, -s1, -s2), 0.0)
        ckv = ckr_ref[:, 0:KV_RANK]
        rk = _rstd(ckv)
        ckh = ckv * rk
        ckvn = (ckh * ckvn_ref[...]).astype(BF16)
        for h in range(N_HEADS):
            dwukv_ref[h] += lax.dot_general(ckvn, dkv_ref[:, h * HEAD_PAD:(h + 1) * HEAD_PAD], TN, preferred_element_type=F32)
        dckvn = lax.dot_general(dkv_ref[...], wukv_ref[...], NT, preferred_element_type=F32)
        dckvn_ref[...] += jnp.sum(dckvn * ckh, axis=0, keepdims=True)
        dckv = _norm_bwd(dckvn * ckvn_ref[...], ckh, rk)
        dckr = jnp.concatenate([dckv, dk_rope], axis=1).astype(BF16)
        dwdkv_ref[...] += _dot_tn(xh * kvn_ref[...], dckr)
        dh2 = lax.dot_general(dckr, wdkv_ref[...], NT, preferred_element_type=F32)
        dkvn_ref[...] += jnp.sum(dh2 * xh, axis=0, keepdims=True)
        dbn_ref[...] += jnp.sum(dh3 * xh, axis=0, keepdims=True)
        dx1_ref[...] = dx2_ref[...] + _norm_bwd(dh2 * kvn_ref[...] + dh3 * bn_ref[...], xh, r)

    tok = lambda width: pl.BlockSpec((tm, width), lambda i: (i, 0))
    weights = [kv_norm, w_dkv, ckv_norm, w_ukv, b_norm, b_w_in, q_norm, w_uq]
    acc_shapes = [(N_HEADS, Q_RANK, HEAD_PAD), (D_MODEL, Q_RANK + E_B), (N_HEADS, KV_RANK, HEAD_PAD), (D_MODEL, CKR_PAD),
                  (1, Q_RANK), (1, D_MODEL), (1, KV_RANK), (1, D_MODEL)]
    return pl.pallas_call(
        body, name="latent_proj_bwd", grid=(t // tm,),
        in_specs=[tok(wide), tok(wide), tok(E_B), tok(E_B), tok(Q_RANK), tok(CKR_PAD), tok(D_MODEL), tok(D_MODEL), tok(1),
                  _const_spec(rc.shape)] + [_const_spec(w.shape) for w in weights],
        out_specs=[tok(D_MODEL)] + [_acc_spec(s) for s in acc_shapes],
        out_shape=[jax.ShapeDtypeStruct((t, D_MODEL), F32)] + [jax.ShapeDtypeStruct(s, F32) for s in acc_shapes],
        scratch_shapes=[pltpu.VMEM((tm, wide), BF16), pltpu.VMEM((tm, wide), BF16)],
        compiler_params=_params(1),
    )(dq, dk, dv, dgb, cq, ckr, x1, dx2, pos, rc, *weights)


def _attn_fwd(q, k, v, n_seq, seq):
    tq, tk = ATTN_FWD_TILES
    ratio = tq // tk
    nq = seq // tq
    pair = 2 * HEAD_PAD

    def body(q_ref, k_ref, v_ref, o_ref, lr_ref):
        i = pl.program_id(2)
        row = lax.broadcasted_iota(jnp.int32, (tq, tk), 0)
        col = lax.broadcasted_iota(jnp.int32, (tq, tk), 1)
        qs = [q_ref[:, hh * HEAD_PAD:(hh + 1) * HEAD_PAD] for hh in range(2)]

        def step(j, carry, shift):
            start = pl.multiple_of(j * tk, tk)
            vj = v_ref[pl.ds(start, tk), :]
            out = []
            for hh in range(2):
                m, l, acc = carry[hh]
                s = lax.dot_general(qs[hh], k_ref[pl.ds(start, tk), hh * HEAD_PAD:(hh + 1) * HEAD_PAD], NT,
                                    preferred_element_type=F32) * SOFTMAX_SCALE
                if shift is not None:
                    s = jnp.where(col + shift <= row, s, -jnp.inf)
                m_new = jnp.maximum(m, jnp.max(s, axis=-1, keepdims=True))
                alpha = jnp.exp(m - m_new)
                p = jnp.exp(s - m_new)
                l = alpha * l + jnp.sum(p, axis=-1, keepdims=True)
                acc = alpha * acc + jnp.dot(p.astype(BF16), vj, preferred_element_type=F32)
                out.append((m_new, l, acc))
            return tuple(out)

        one = (jnp.full((tq, 1), -jnp.inf, F32), jnp.zeros((tq, 1), F32), jnp.zeros((tq, HEAD_PAD), F32))
        carry = lax.fori_loop(0, i * ratio, functools.partial(step, shift=None), (one, one))
        for d in range(ratio):
            carry = step(i * ratio + d, carry, d * tk)
        halves = []
        for hh in range(2):
            m, l, acc = carry[hh]
            halves.append(acc / l)
            lse = m + jnp.log(l)
            lr_ref[hh] = jnp.broadcast_to(lse, (tq, HEAD_PAD)).T[0:8, :]
        lane = lax.broadcasted_iota(jnp.int32, (tq, HEAD_PAD), 1)
        o_ref[...] = jnp.where(lane < V_HEAD, halves[0], halves[1])

    t = n_seq * seq
    return pl.pallas_call(
        body, name="attention_fwd", grid=(n_seq, N_HEADS // 2, nq),
        in_specs=[pl.BlockSpec((tq, pair), lambda s, p, i: (s * nq + i, p)),
                  pl.BlockSpec((seq, pair), lambda s, p, i: (s, p)),
                  pl.BlockSpec((seq, HEAD_PAD), lambda s, p, i: (s, p))],
        out_specs=[pl.BlockSpec((tq, HEAD_PAD), lambda s, p, i: (s * nq + i, p)),
                   pl.BlockSpec((None, 2, 8, tq), lambda s, p, i: (s, p, 0, i))],
        out_shape=[jax.ShapeDtypeStruct((t, E_B), F32), jax.ShapeDtypeStruct((n_seq, N_HEADS, 8, seq), F32)],
        compiler_params=_params(3),
    )(q, k, v)


def _attn_bwd(q, k, v, do, lse_row, delta_row, n_seq, seq):
    tk, tq = ATTN_BWD_TILES
    ratio = tk // tq
    nk = seq // tk
    n_inner = seq // tq
    pair = 2 * HEAD_PAD

    def body(q_ref, k_ref, v_ref, do_ref, lr_ref, dr_ref, dk_ref, dv_ref, dq_ref):
        j = pl.program_id(2)

        @pl.when(j == 0)
        def _():
            dq_ref[...] = jnp.zeros_like(dq_ref)

        row = lax.broadcasted_iota(jnp.int32, (tk, tq), 0)
        col = lax.broadcasted_iota(jnp.int32, (tk, tq), 1)
        lane = lax.broadcasted_iota(jnp.int32, (tk, HEAD_PAD), 1)
        ks = [k_ref[:, hh * HEAD_PAD:(hh + 1) * HEAD_PAD] for hh in range(2)]
        vv = v_ref[...]
        vs = [jnp.where(lane < V_HEAD, vv, jnp.zeros_like(vv)), jnp.where(lane >= V_HEAD, vv, jnp.zeros_like(vv))]

        def step(i, carry, shift):
            start = pl.multiple_of(i * tq, tq)
            doi = do_ref[pl.ds(start, tq), :]
            out = []
            for hh in range(2):
                dk_acc, dv_acc = carry[hh]
                qi = q_ref[pl.ds(start, tq), hh * HEAD_PAD:(hh + 1) * HEAD_PAD]
                st = lax.dot_general(ks[hh], qi, NT, preferred_element_type=F32) * SOFTMAX_SCALE
                pt = jnp.exp(st - lr_ref[hh, 0:1, pl.ds(start, tq)])
                if shift is not None:
                    pt = jnp.where(col + shift >= row, pt, 0.0)
                dv_acc = dv_acc + jnp.dot(pt.astype(BF16), doi, preferred_element_type=F32)
                dpt = lax.dot_general(vs[hh], doi, NT, preferred_element_type=F32)
                dst = (pt * (dpt - dr_ref[hh, 0:1, pl.ds(start, tq)]) * SOFTMAX_SCALE).astype(BF16)
                dk_acc = dk_acc + jnp.dot(dst, qi, preferred_element_type=F32)
                dq_ref[pl.ds(start, tq), hh * HEAD_PAD:(hh + 1) * HEAD_PAD] += lax.dot_general(
                    dst, ks[hh], TN, preferred_element_type=F32)
                out.append((dk_acc, dv_acc))
            return tuple(out)

        one = (jnp.zeros((tk, HEAD_PAD), F32), jnp.zeros((tk, HEAD_PAD), F32))
        carry = (one, one)
        for d in range(ratio):
            carry = step(j * ratio + d, carry, d * tq)
        carry = lax.fori_loop((j + 1) * ratio, n_inner, functools.partial(step, shift=None), carry)
        for hh in range(2):
            dk_ref[:, hh * HEAD_PAD:(hh + 1) * HEAD_PAD] = carry[hh][0]
        dv_ref[...] = jnp.where(lane < V_HEAD, carry[0][1], carry[1][1])

    t = n_seq * seq
    wide = N_HEADS * HEAD_PAD
    return pl.pallas_call(
        body, name="attention_bwd", grid=(n_seq, N_HEADS // 2, nk),
        in_specs=[pl.BlockSpec((seq, pair), lambda s, p, j: (s, p)),
                  pl.BlockSpec((tk, pair), lambda s, p, j: (s * nk + j, p)),
                  pl.BlockSpec((tk, HEAD_PAD), lambda s, p, j: (s * nk + j, p)),
                  pl.BlockSpec((seq, HEAD_PAD), lambda s, p, j: (s, p)),
                  pl.BlockSpec((None, 2, 8, seq), lambda s, p, j: (s, p, 0, 0)),
                  pl.BlockSpec((None, 2, 8, seq), lambda s, p, j: (s, p, 0, 0))],
        out_specs=[pl.BlockSpec((tk, pair), lambda s, p, j: (s * nk + j, p)),
                   pl.BlockSpec((tk, HEAD_PAD), lambda s, p, j: (s * nk + j, p)),
                   pl.BlockSpec((seq, pair), lambda s, p, j: (s, p))],
        out_shape=[jax.ShapeDtypeStruct((t, wide), F32), jax.ShapeDtypeStruct((t, E_B), F32),
                   jax.ShapeDtypeStruct((t, wide), F32)],
        compiler_params=_params(3),
    )(q, k, v, do, lse_row, delta_row)


def _head_tail(o, gb, x1, w_out, final_norm, target, n_seq, seq):
    tm = ROW_TILE
    nt = seq // tm
    n_col = D_MODEL // N_DEV

    def body(o_ref, gb_ref, x1_ref, wout_ref, fn_ref, tgt_ref,
             dx2_ref, do_ref, dgb_ref, dr_ref, dwout_ref, dfn_ref, loss_ref):
        first = jnp.logical_and(pl.program_id(0) == 0, pl.program_id(1) == 0)

        @pl.when(first)
        def _():
            dwout_ref[...] = jnp.zeros_like(dwout_ref)
            dfn_ref[...] = jnp.zeros_like(dfn_ref)
            loss_ref[...] = jnp.zeros_like(loss_ref)

        ov, g = o_ref[...], gb_ref[...]
        silu, dsilu = _silu_parts(g)
        gated = (ov * silu).astype(BF16)
        x2 = x1_ref[...] + jnp.dot(gated, wout_ref[...], preferred_element_type=F32)
        r = _rstd(x2)
        xh = x2 * r
        err = xh * fn_ref[...] - tgt_ref[...]
        loss_ref[...] += 0.5 * jnp.sum(jnp.mean(err * err, axis=-1, keepdims=True), axis=0, keepdims=True)
        dy = err / D_MODEL
        dfn_ref[...] += jnp.sum(dy * xh, axis=0, keepdims=True)
        dx2 = _norm_bwd(dy * fn_ref[...], xh, r)
        dx2_ref[...] = dx2
        dx2b = dx2.astype(BF16)
        dw = lax.dot_general(gated, dx2b, TN, preferred_element_type=F32)
        for d in range(N_DEV):
            dwout_ref[d] += dw[:, d * n_col:(d + 1) * n_col]
        dgated = lax.dot_general(dx2b, wout_ref[...], NT, preferred_element_type=F32)
        do = dgated * silu
        do_ref[...] = do.astype(BF16)
        dgb_ref[...] = dgated * ov * dsilu
        prod = do * ov
        lane = lax.broadcasted_iota(jnp.int32, (tm, HEAD_PAD), 1)
        for p in range(N_HEADS // 2):
            blk = prod[:, p * HEAD_PAD:(p + 1) * HEAD_PAD]
            for hh in range(2):
                mine = (lane < V_HEAD) if hh == 0 else (lane >= V_HEAD)
                delta = jnp.sum(jnp.where(mine, blk, 0.0), axis=-1, keepdims=True)
                dr_ref[2 * p + hh] = jnp.broadcast_to(delta, (tm, HEAD_PAD)).T[0:8, :]

    tok = lambda width: pl.BlockSpec((tm, width), lambda s, i: (s * nt + i, 0))
    t = n_seq * seq
    return pl.pallas_call(
        body, name="head_tail", grid=(n_seq, nt),
        in_specs=[tok(E_B), tok(E_B), tok(D_MODEL), _const_spec(w_out.shape), _const_spec((1, D_MODEL)), tok(D_MODEL)],
        out_specs=[tok(D_MODEL), tok(E_B), tok(E_B),
                   pl.BlockSpec((None, N_HEADS, 8, tm), lambda s, i: (s, 0, 0, i)),
                   _acc_spec((N_DEV, E_B, n_col)), _acc_spec((1, D_MODEL)), _acc_spec((1, 1))],
        out_shape=[jax.ShapeDtypeStruct((t, D_MODEL), F32), jax.ShapeDtypeStruct((t, E_B), BF16),
                   jax.ShapeDtypeStruct((t, E_B), F32),
                   jax.ShapeDtypeStruct((n_seq, N_HEADS, 8, seq), F32), jax.ShapeDtypeStruct((N_DEV, E_B, n_col), F32),
                   jax.ShapeDtypeStruct((1, D_MODEL), F32), jax.ShapeDtypeStruct((1, 1), F32)],
        compiler_params=_params(2),
    )(o, gb, x1, w_out, final_norm, target)


def _adamw_math(w, g, m, v):
    m = ADAM_B1 * m + (1.0 - ADAM_B1) * g
    v = ADAM_B2 * v + (1.0 - ADAM_B2) * jnp.square(g)
    m_hat = m / (1.0 - ADAM_B1 ** ADAM_STEP)
    v_hat = v / (1.0 - ADAM_B2 ** ADAM_STEP)
    delta = -ADAM_LR * (m_hat / (jnp.sqrt(v_hat) + ADAM_EPS) + ADAM_WD * w)
    return delta, m, v


def _adamw_reduce(parts, w, m, v):
    rows, cols = w.shape
    br = 256 if rows % 256 == 0 else 128

    def body(p_ref, w_ref, m_ref, v_ref, g_ref, d_ref, nm_ref, nv_ref):
        g = p_ref[0]
        for k in range(1, N_DEV):
            g = g + p_ref[k]
        g_ref[...] = g
        d_ref[...], nm_ref[...], nv_ref[...] = _adamw_math(w_ref[...], g, m_ref[...], v_ref[...])

    blk = pl.BlockSpec((br, cols), lambda i: (i, 0))
    return pl.pallas_call(
        body, name="adamw_reduce", grid=(rows // br,),
        in_specs=[pl.BlockSpec((N_DEV, br, cols), lambda i: (0, i, 0)), blk, blk, blk],
        out_specs=[blk] * 4, out_shape=[jax.ShapeDtypeStruct((rows, cols), F32)] * 4,
        compiler_params=_params(1),
    )(parts, w, m, v)


def _sum_parts(parts):
    def body(p_ref, o_ref):
        g = p_ref[0]
        for k in range(1, N_DEV):
            g = g + p_ref[k]
        o_ref[...] = g

    return pl.pallas_call(body, name="small_grad_sum", out_shape=jax.ShapeDtypeStruct(parts.shape[1:], F32))(parts)


def _adamw_small(gs, ws, ms, vs):
    n = len(gs)

    def body(*refs):
        ins, outs = refs[:4 * n], refs[4 * n:]
        for j in range(n):
            g_ref, w_ref, m_ref, v_ref = ins[j], ins[n + j], ins[2 * n + j], ins[3 * n + j]
            outs[3 * j][...], outs[3 * j + 1][...], outs[3 * j + 2][...] = _adamw_math(
                w_ref[...], g_ref[...], m_ref[...], v_ref[...])

    shapes = [jax.ShapeDtypeStruct(g.shape, F32) for g in gs for _ in range(3)]
    flat = pl.pallas_call(body, name="adamw_small", out_shape=shapes)(*gs, *ws, *ms, *vs)
    return [tuple(flat[3 * j:3 * j + 3]) for j in range(n)]


SMALL_ROWS = 8
LOSS_LANE = D_MODEL - 1


def _pack_small(g_fn, g_kvn, g_bn, g_an, g_ckvn, g_qn, g_conv, loss_part):
    def body(fn_ref, kvn_ref, bn_ref, an_ref, ckvn_ref, qn_ref, conv_ref, loss_ref, o_ref):
        o_ref[0:1, :] = fn_ref[...]
        o_ref[1:2, :] = kvn_ref[...]
        o_ref[2:3, :] = bn_ref[...]
        o_ref[3:4, :] = an_ref[...]
        lane = lax.broadcasted_iota(jnp.int32, (1, D_MODEL), 1)
        o_ref[4:5, :] = jnp.where(lane == LOSS_LANE, loss_ref[...], 0.0)
        o_ref[4:5, 0:KV_RANK] = ckvn_ref[...]
        o_ref[4:5, KV_RANK:KV_RANK + Q_RANK] = qn_ref[...]
        o_ref[5:8, :] = conv_ref[0:3, :]

    return pl.pallas_call(body, name="pack_small", out_shape=jax.ShapeDtypeStruct((SMALL_ROWS, D_MODEL), F32))(
        g_fn, g_kvn, g_bn, g_an, g_ckvn, g_qn, g_conv, loss_part)


def _pad_cols(a, width):
    return jnp.pad(a, ((0, 0), (0, width - a.shape[1])))


def _dkv_to_padded(a):
    r = a.shape[0]
    z = jnp.zeros((r, ROPE_LO), a.dtype)
    z2 = jnp.zeros((r, HEAD_PAD - ROPE_LO - QK_ROPE), a.dtype)
    return jnp.concatenate([a[:, :KV_RANK], z, a[:, KV_RANK:], z2], axis=1)


def _dkv_from_padded(a):
    return jnp.concatenate([a[:, :KV_RANK], a[:, KV_RANK + ROPE_LO:KV_RANK + ROPE_LO + QK_ROPE]], axis=1)


def _unstack_cols(a):
    return jnp.transpose(a, (1, 0, 2)).reshape(a.shape[1], N_DEV * a.shape[2])


def kernel(x, positions, a_norm, a_w_in, a_conv, a_w_out, kv_norm, w_dkv, ckv_norm, w_ukv, b_norm, b_w_in, b_q_norm, b_w_uq, b_w_out, final_norm, loss_target, m_a_norm, m_a_w_in, m_a_conv, m_a_w_out, m_kv_norm, m_w_dkv, m_ckv_norm, m_w_ukv, m_b_norm, m_b_w_in, m_b_q_norm, m_b_w_uq, m_b_w_out, m_final_norm, v_a_norm, v_a_w_in, v_a_conv, v_a_w_out, v_kv_norm, v_w_dkv, v_ckv_norm, v_w_ukv, v_b_norm, v_b_w_in, v_b_q_norm, v_b_w_uq, v_b_w_out, v_final_norm):
    n_seq, seq, _ = x.shape
    t = n_seq * seq
    me = 4 * lax.axis_index("x") + 2 * lax.axis_index("y") + lax.axis_index("c")

    big = {
        "a_w_in": (a_w_in[0], m_a_w_in[0], v_a_w_in[0]),
        "a_w_out": (a_w_out[0], m_a_w_out[0], v_a_w_out[0]),
        "w_dkv": tuple(_dkv_to_padded(a) for a in (w_dkv, m_w_dkv, v_w_dkv)),
        "w_ukv": (w_ukv, m_w_ukv, v_w_ukv),
        "b_w_in": (b_w_in[0], m_b_w_in[0], v_b_w_in[0]),
        "b_w_uq": tuple(_pad_cols(a[0], HEAD_PAD) for a in (b_w_uq, m_b_w_uq, v_b_w_uq)),
        "b_w_out": (b_w_out[0], m_b_w_out[0], v_b_w_out[0]),
    }
    names = list(big)
    gathered = _all_gather([big[n][0].astype(BF16) for n in names] + [a_norm, a_conv[0]])
    full = dict(zip(names, gathered[:-2]))
    a_norm_f = gathered[-2].reshape(1, D_MODEL)
    a_conv_f = _unstack_cols(gathered[-1])
    w_a_in = full["a_w_in"]
    w_a_out = full["a_w_out"].reshape(D_MODEL, D_MODEL)
    w_dkv_f = full["w_dkv"].reshape(D_MODEL, CKR_PAD)
    w_ukv_f = _unstack_cols(full["w_ukv"])
    w_b_in = full["b_w_in"].reshape(D_MODEL, Q_RANK + E_B)
    w_uq_f = _unstack_cols(full["b_w_uq"])
    w_b_out = _unstack_cols(full["b_w_out"])

    x2d = x.reshape(t, D_MODEL)
    tgt = loss_target.reshape(t, D_MODEL)
    pos = positions.astype(F32).reshape(t, 1)
    rc = _rope_consts()
    kvn, ckvn, bn, qn, fn = kv_norm.reshape(1, -1), ckv_norm.reshape(1, -1), b_norm, b_q_norm, final_norm.reshape(1, -1)
    layer_b = (kvn, w_dkv_f, ckvn, w_ukv_f, bn, w_b_in, qn, w_uq_f)

    x1, proj, conv = _conv_fwd(x2d, a_norm_f, w_a_in, a_conv_f, w_a_out, n_seq, seq)
    ckr, cq, gb, q, k, v = _proj_fwd(x1, pos, rc, *layer_b)
    o, lse_row = _attn_fwd(q, k, v, n_seq, seq)
    dx2, do, dgb, delta_row, g_b_out, g_fn, loss_part = _head_tail(o, gb, x1, w_b_out, fn, tgt, n_seq, seq)
    dk, dv, dq = _attn_bwd(q, k, v, do, lse_row, delta_row, n_seq, seq)
    dx1, g_uq, g_b_in, g_ukv, g_dkv, g_qn, g_bn, g_ckvn, g_kvn = _proj_bwd(
        dq, dk, dv, dgb, cq, ckr, x1, dx2, pos, rc, *layer_b)
    dx, h_a, dproj, g_a_out, g_an, g_conv = _conv_bwd(dx1, x2d, proj, conv, a_norm_f, w_a_in, a_conv_f, w_a_out, n_seq, seq)
    g_a_in = _stacked_weight_grad(h_a, dproj, D_MODEL * 4 // N_DEV)

    stacks = {
        "a_w_in": g_a_in,
        "a_w_out": g_a_out.reshape(N_DEV, D_MODEL // N_DEV, D_MODEL),
        "w_dkv": g_dkv.reshape(N_DEV, D_MODEL // N_DEV, CKR_PAD),
        "w_ukv": g_ukv,
        "b_w_in": g_b_in.reshape(N_DEV, D_MODEL // N_DEV, Q_RANK + E_B),
        "b_w_uq": g_uq,
        "b_w_out": g_b_out,
    }
    small = _pack_small(g_fn, g_kvn, g_bn, g_an, g_ckvn, g_qn, g_conv, loss_part)
    parts, small_parts = _grad_exchange([stacks[n] for n in names], small)

    outs = {}
    for n, p in zip(names, parts):
        w, m, vv = big[n]
        outs[n] = _adamw_reduce(p, w, m, vv)
    outs["w_dkv"] = tuple(_dkv_from_padded(a) for a in outs["w_dkv"])
    outs["b_w_uq"] = tuple(a[:, :QK_NOPE + QK_ROPE] for a in outs["b_w_uq"])
    for n in ("a_w_in", "a_w_out", "b_w_in", "b_w_uq", "b_w_out"):
        outs[n] = tuple(a[None] for a in outs[n])

    total = _sum_parts(small_parts)
    loss = total[4, LOSS_LANE]
    shard = D_MODEL // N_DEV
    g_small = {
        "final_norm": total[0], "kv_norm": total[1], "b_norm": total[2:3],
        "a_norm": lax.dynamic_slice_in_dim(total[3:4], me * shard, shard, axis=1),
        "ckv_norm": total[4, 0:KV_RANK], "b_q_norm": total[4:5, KV_RANK:KV_RANK + Q_RANK],
        "a_conv": lax.dynamic_slice_in_dim(total[5:8], me * shard, shard, axis=1)[None],
    }
    small_state = {
        "final_norm": (final_norm, m_final_norm, v_final_norm), "kv_norm": (kv_norm, m_kv_norm, v_kv_norm),
        "b_norm": (b_norm, m_b_norm, v_b_norm), "a_norm": (a_norm, m_a_norm, v_a_norm),
        "ckv_norm": (ckv_norm, m_ckv_norm, v_ckv_norm), "b_q_norm": (b_q_norm, m_b_q_norm, v_b_q_norm),
        "a_conv": (a_conv, m_a_conv, v_a_conv),
    }
    small_names = list(g_small)
    as2d = lambda a: a.reshape(-1, a.shape[-1])
    upd = _adamw_small([as2d(g_small[n]) for n in small_names],
                       *[[as2d(small_state[n][j]) for n in small_names] for j in range(3)])
    for n, u in zip(small_names, upd):
        outs[n] = (g_small[n],) + tuple(a.reshape(g_small[n].shape) for a in u)

    order = ["a_norm", "a_w_in", "a_conv", "a_w_out", "kv_norm", "w_dkv", "ckv_norm", "w_ukv", "b_norm", "b_w_in",
             "b_q_norm", "b_w_uq", "b_w_out", "final_norm"]
    result = [loss, dx.reshape(n_seq, seq, D_MODEL)]
    for j in range(4):
        result += [outs[n][j] for n in order]
    return tuple(result)
```

```python
import functools
import math

import numpy as np
import jax
import jax.numpy as jnp
from jax import lax
from jax.experimental import pallas as pl
from jax.experimental.pallas import tpu as pltpu

F32 = jnp.float32
BF16 = jnp.bfloat16

D_MODEL = 1024
N_HEADS = 8
QK_NOPE = 64
QK_ROPE = 32
V_HEAD = 64
KV_RANK = 256
Q_RANK = 384
E_B = N_HEADS * V_HEAD
HEAD_PAD = 128
CKR_PAD = KV_RANK + HEAD_PAD
ROPE_LO = QK_NOPE
ROPE_HALF = QK_ROPE // 2
ROPE_THETA = 10000.0
SOFTMAX_SCALE = 1.0 / math.sqrt(QK_NOPE + QK_ROPE)
EPS = 1e-6
N_DEV = 8

ADAM_LR = 0.001
ADAM_B1 = 0.9
ADAM_B2 = 0.999
ADAM_EPS = 1e-08
ADAM_WD = 0.01
ADAM_STEP = 10

ROW_TILE = 256
ATTN_FWD_TILES = (512, 512)
ATTN_BWD_TILES = (512, 512)
VMEM_LIMIT = 56 * 1024 * 1024

MESH = pl.DeviceIdType.MESH
NT = (((1,), (1,)), ((), ()))
TN = (((0,), (0,)), ((), ()))


def _dot(a, b):
    return jnp.dot(a.astype(BF16), b.astype(BF16), preferred_element_type=F32)


def _dot_nt(a, b):
    return lax.dot_general(a.astype(BF16), b.astype(BF16), NT, preferred_element_type=F32)


def _dot_tn(a, b):
    return lax.dot_general(a.astype(BF16), b.astype(BF16), TN, preferred_element_type=F32)


def _rstd(x):
    return lax.rsqrt(jnp.mean(x * x, axis=-1, keepdims=True) + EPS)


def _norm_bwd(a, xh, r):
    return r * (a - xh * jnp.mean(a * xh, axis=-1, keepdims=True))


def _silu_parts(g):
    sg = jax.nn.sigmoid(g)
    return g * sg, sg * (1.0 + g * (1.0 - sg))


def _rope_consts():
    inv = (ROPE_THETA ** (-np.arange(0, QK_ROPE, 2, dtype=np.float32) / QK_ROPE)).astype(np.float32)
    t = np.zeros((8, HEAD_PAD), np.float32)
    t[0, ROPE_LO:ROPE_LO + ROPE_HALF] = inv
    t[0, ROPE_LO + ROPE_HALF:ROPE_LO + QK_ROPE] = inv
    t[1, ROPE_LO:ROPE_LO + ROPE_HALF] = -1.0
    t[2, ROPE_LO + ROPE_HALF:ROPE_LO + QK_ROPE] = 1.0
    return jnp.asarray(t)


def _rope_tables(pos, rc):
    ang = pos * rc[0:1, :]
    cosv = jnp.cos(ang)
    sinv = jnp.sin(ang)
    return cosv, sinv * rc[1:2, :], sinv * rc[2:3, :]


def _rope(x, ct, s1, s2):
    up = pltpu.roll(x, HEAD_PAD - ROPE_HALF, 1)
    dn = pltpu.roll(x, ROPE_HALF, 1)
    return x * ct + up * s1 + dn * s2


def _const_spec(shape):
    nd = len(shape)
    return pl.BlockSpec(shape, lambda *_: (0,) * nd, pipeline_mode=pl.Buffered(1))


def _acc_spec(shape):
    nd = len(shape)
    return pl.BlockSpec(shape, lambda *_: (0,) * nd)


def _params(n_axes):
    return pltpu.CompilerParams(dimension_semantics=("arbitrary",) * n_axes, vmem_limit_bytes=VMEM_LIMIT)


def _place():
    x, y, c = lax.axis_index("x"), lax.axis_index("y"), lax.axis_index("c")
    return x, y, c, 4 * x + 2 * y + c


def _peer(x, y, c, mask):
    px = 1 - x if mask & 4 else x
    py = 1 - y if mask & 2 else y
    pc = 1 - c if mask & 1 else c
    return (px, py, pc), 4 * px + 2 * py + pc


ANY_SPEC = pl.BlockSpec(memory_space=pl.ANY)


def _comm_sems(n):
    return [pltpu.SemaphoreType.DMA((n, N_DEV - 1)), pltpu.SemaphoreType.DMA((n, N_DEV - 1)), pltpu.SemaphoreType.DMA((n,))]


def _gather_copies(ins, outs, sems):
    send_sems, recv_sems, local_sems = sems
    x, y, c, me = _place()
    starts, waits = [], []
    for w in range(len(ins)):
        mine = pltpu.make_async_copy(ins[w], outs[w].at[me], local_sems.at[w])
        starts.append(mine)
        waits.append(mine)
        for mask in range(1, N_DEV):
            peer, peer_idx = _peer(x, y, c, mask)
            starts.append(pltpu.make_async_remote_copy(
                src_ref=ins[w], dst_ref=outs[w].at[me], send_sem=send_sems.at[w, mask - 1],
                recv_sem=recv_sems.at[w, mask - 1], device_id=peer, device_id_type=MESH))
            waits.append(pltpu.make_async_remote_copy(
                src_ref=ins[w], dst_ref=outs[w].at[peer_idx], send_sem=send_sems.at[w, mask - 1],
                recv_sem=recv_sems.at[w, mask - 1], device_id=peer, device_id_type=MESH))
    return starts, waits


def _scatter_copies(ins, outs, sems):
    send_sems, recv_sems, local_sems = sems
    x, y, c, me = _place()
    copies = []
    for w in range(len(ins)):
        copies.append(pltpu.make_async_copy(ins[w].at[me], outs[w].at[0], local_sems.at[w]))
        for mask in range(1, N_DEV):
            peer, peer_idx = _peer(x, y, c, mask)
            copies.append(pltpu.make_async_remote_copy(
                src_ref=ins[w].at[peer_idx], dst_ref=outs[w].at[mask], send_sem=send_sems.at[w, mask - 1],
                recv_sem=recv_sems.at[w, mask - 1], device_id=peer, device_id_type=MESH))
    return copies, copies


def _stacked(arrays):
    return [jax.ShapeDtypeStruct((N_DEV,) + a.shape, a.dtype) for a in arrays]


def _all_gather(shards):
    n = len(shards)

    def body(*refs):
        starts, waits = _gather_copies(refs[:n], refs[n:2 * n], refs[2 * n:])
        for cp in starts:
            cp.start()
        for cp in waits:
            cp.wait()

    return pl.pallas_call(
        body, name="weight_all_gather", out_shape=_stacked(shards),
        in_specs=[ANY_SPEC] * n, out_specs=[ANY_SPEC] * n, scratch_shapes=_comm_sems(n),
    )(*shards)


def _conv_fwd(x, a_norm, w_in, conv_w, w_out, n_seq, seq, later_shards):
    tm = ROW_TILE
    nt = seq // tm
    n_col = w_in.shape[2]
    n_later = len(later_shards)

    def body(x_ref, an_ref, win_ref, cw_ref, wout_ref, *rest):
        shard_refs, rest = rest[:n_later], rest[n_later:]
        x1_ref, proj_ref, conv_ref = rest[:3]
        stack_refs, rest = rest[3:3 + n_later], rest[3 + n_later:]
        prev_ref, sems = rest[0], rest[1:]
        step = pl.program_id(0) * nt + pl.program_id(1)

        @pl.when(step == 0)
        def _():
            for cp in _gather_copies(shard_refs, stack_refs, sems)[0]:
                cp.start()

        @pl.when(pl.program_id(1) == 0)
        def _():
            prev_ref[...] = jnp.zeros_like(prev_ref)

        xv = x_ref[...]
        h = (xv * _rstd(xv) * an_ref[...]).astype(BF16)
        for d in range(N_DEV):
            proj_ref[:, d * n_col:(d + 1) * n_col] = jnp.dot(h, win_ref[d], preferred_element_type=F32)
        b = proj_ref[:, 0:D_MODEL]
        v = proj_ref[:, D_MODEL:2 * D_MODEL] * proj_ref[:, 2 * D_MODEL:3 * D_MODEL]
        g = proj_ref[:, 3 * D_MODEL:4 * D_MODEL]
        w0, w1, w2 = cw_ref[0:1, :], cw_ref[1:2, :], cw_ref[2:3, :]
        conv_ref[...] = w0 * pltpu.roll(v, 2, 0) + w1 * pltpu.roll(v, 1, 0) + w2 * v
        rows = lax.broadcasted_iota(jnp.int32, (8, D_MODEL), 0)
        p8, v8 = prev_ref[...], v[0:8]
        back1 = jnp.where(rows < 1, pltpu.roll(p8, 1, 0), pltpu.roll(v8, 1, 0))
        back2 = jnp.where(rows < 2, pltpu.roll(p8, 2, 0), pltpu.roll(v8, 2, 0))
        conv_ref[0:8, :] = w0 * back2 + w1 * back1 + w2 * v8
        prev_ref[...] = v[tm - 8:tm]
        silu, _ = _silu_parts(g)
        yv = silu * b * conv_ref[...]
        x1_ref[...] = xv + _dot(yv, wout_ref[...])

        @pl.when(step == n_seq * nt - 1)
        def _():
            for cp in _gather_copies(shard_refs, stack_refs, sems)[1]:
                cp.wait()

    tok = lambda width: pl.BlockSpec((tm, width), lambda s, i: (s * nt + i, 0))
    t = n_seq * seq
    outs = pl.pallas_call(
        body, name="conv_mixer_fwd", grid=(n_seq, nt),
        in_specs=[tok(D_MODEL), _const_spec((1, D_MODEL)), _const_spec(w_in.shape), _const_spec((3, D_MODEL)),
                  _const_spec(w_out.shape)] + [ANY_SPEC] * n_later,
        out_specs=[tok(D_MODEL), tok(4 * D_MODEL), tok(D_MODEL)] + [ANY_SPEC] * n_later,
        out_shape=[jax.ShapeDtypeStruct((t, D_MODEL), F32), jax.ShapeDtypeStruct((t, 4 * D_MODEL), F32),
                   jax.ShapeDtypeStruct((t, D_MODEL), F32)] + _stacked(later_shards),
        scratch_shapes=[pltpu.VMEM((8, D_MODEL), F32)] + _comm_sems(n_later),
        compiler_params=_params(2),
    )(x, a_norm, w_in, conv_w, w_out, *later_shards)
    return outs[0], outs[1], outs[2], outs[3:]


def _conv_bwd(dx1, x, proj, conv, a_norm, w_in, conv_w, w_out, n_seq, seq, ready_stacks):
    tm = ROW_TILE
    nt = seq // tm
    n_col = w_in.shape[2]
    n_ready = len(ready_stacks)

    def body(dx1_ref, x_ref, proj_ref, conv_ref, an_ref, win_ref, cw_ref, wout_ref, *rest):
        ready_refs, rest = rest[:n_ready], rest[n_ready:]
        dx_ref, h_ref, dproj_ref, dwout_ref, dan_ref, dcw_ref = rest[:6]
        part_refs, rest = rest[6:6 + n_ready], rest[6 + n_ready:]
        next_ref, d1_ref, d2_ref = rest[:3]
        sems = rest[3:]
        step = pl.program_id(0) * nt + pl.program_id(1)
        first = step == 0

        @pl.when(first)
        def _():
            for cp in _scatter_copies(ready_refs, part_refs, sems)[0]:
                cp.start()
            dwout_ref[...] = jnp.zeros_like(dwout_ref)
            dan_ref[...] = jnp.zeros_like(dan_ref)
            dcw_ref[...] = jnp.zeros_like(dcw_ref)

        @pl.when(pl.program_id(1) == 0)
        def _():
            next_ref[...] = jnp.zeros_like(next_ref)

        dx1v = dx1_ref[...]
        dy = _dot_nt(dx1v, wout_ref[...])
        b = proj_ref[:, 0:D_MODEL]
        cc = proj_ref[:, D_MODEL:2 * D_MODEL]
        u = proj_ref[:, 2 * D_MODEL:3 * D_MODEL]
        g = proj_ref[:, 3 * D_MODEL:4 * D_MODEL]
        cv = conv_ref[...]
        silu, dsilu = _silu_parts(g)
        dwout_ref[...] += _dot_tn(silu * b * cv, dx1v)
        dproj_ref[:, 3 * D_MODEL:4 * D_MODEL] = (dy * b * cv * dsilu).astype(BF16)
        dproj_ref[:, 0:D_MODEL] = (dy * silu * cv).astype(BF16)
        dconv = dy * silu * b
        d1_ref[...] = pltpu.roll(dconv, tm - 1, 0)
        d2_ref[...] = pltpu.roll(dconv, tm - 2, 0)
        rows = lax.broadcasted_iota(jnp.int32, (8, D_MODEL), 0)
        n8, c8 = next_ref[...], dconv[tm - 8:tm]
        d1_ref[tm - 8:tm, :] = jnp.where(rows >= 7, pltpu.roll(n8, 7, 0), pltpu.roll(c8, 7, 0))
        d2_ref[tm - 8:tm, :] = jnp.where(rows >= 6, pltpu.roll(n8, 6, 0), pltpu.roll(c8, 6, 0))
        next_ref[...] = dconv[0:8]
        d1, d2 = d1_ref[...], d2_ref[...]
        v = cc * u
        dcw_ref[0:1, :] += jnp.sum(d2 * v, axis=0, keepdims=True)
        dcw_ref[1:2, :] += jnp.sum(d1 * v, axis=0, keepdims=True)
        dcw_ref[2:3, :] += jnp.sum(dconv * v, axis=0, keepdims=True)
        dv = cw_ref[0:1, :] * d2 + cw_ref[1:2, :] * d1 + cw_ref[2:3, :] * dconv
        dproj_ref[:, D_MODEL:2 * D_MODEL] = (dv * u).astype(BF16)
        dproj_ref[:, 2 * D_MODEL:3 * D_MODEL] = (dv * cc).astype(BF16)
        dh = jnp.zeros((tm, D_MODEL), F32)
        for d in range(N_DEV):
            dh += lax.dot_general(dproj_ref[:, d * n_col:(d + 1) * n_col], win_ref[d], NT, preferred_element_type=F32)
        xv = x_ref[...]
        r = _rstd(xv)
        xh = xv * r
        h_ref[...] = (xh * an_ref[...]).astype(BF16)
        dan_ref[...] += jnp.sum(dh * xh, axis=0, keepdims=True)
        dx_ref[...] = dx1v + _norm_bwd(dh * an_ref[...], xh, r)

        @pl.when(step == n_seq * nt - 1)
        def _():
            for cp in _scatter_copies(ready_refs, part_refs, sems)[1]:
                cp.wait()

    tok = lambda width: pl.BlockSpec((tm, width), lambda s, i: (s * nt + nt - 1 - i, 0))
    t = n_seq * seq
    outs = pl.pallas_call(
        body, name="conv_mixer_bwd", grid=(n_seq, nt),
        in_specs=[tok(D_MODEL), tok(D_MODEL), tok(4 * D_MODEL), tok(D_MODEL), _const_spec((1, D_MODEL)),
                  _const_spec(w_in.shape), _const_spec((3, D_MODEL)), _const_spec(w_out.shape)] + [ANY_SPEC] * n_ready,
        out_specs=[tok(D_MODEL), tok(D_MODEL), tok(4 * D_MODEL), _acc_spec((D_MODEL, D_MODEL)),
                   _acc_spec((1, D_MODEL)), _acc_spec((8, D_MODEL))] + [ANY_SPEC] * n_ready,
        out_shape=[jax.ShapeDtypeStruct((t, D_MODEL), F32), jax.ShapeDtypeStruct((t, D_MODEL), BF16),
                   jax.ShapeDtypeStruct((t, 4 * D_MODEL), BF16), jax.ShapeDtypeStruct((D_MODEL, D_MODEL), F32),
                   jax.ShapeDtypeStruct((1, D_MODEL), F32), jax.ShapeDtypeStruct((8, D_MODEL), F32)]
        + [jax.ShapeDtypeStruct(a.shape, a.dtype) for a in ready_stacks],
        scratch_shapes=[pltpu.VMEM((8, D_MODEL), F32), pltpu.VMEM((tm, D_MODEL), F32), pltpu.VMEM((tm, D_MODEL), F32)]
        + _comm_sems(n_ready),
        compiler_params=_params(2),
    )(dx1, x, proj, conv, a_norm, w_in, conv_w, w_out, *ready_stacks)
    return outs[:6], outs[6:]


def _w_in_grad_exchange(a, b, n_col, me, ready_stacks, small):
    t, r = a.shape
    bt = 512
    nk = t // bt
    n_ready = len(ready_stacks)
    last_remote = N_DEV - 2

    def body(me_ref, a_ref, b_ref, *rest):
        ready_refs, small_ref, rest = rest[:n_ready], rest[n_ready], rest[n_ready + 1:]
        own_ref, got_ref, rest = rest[0], rest[1], rest[2:]
        part_refs, small_all_ref, rest = rest[:n_ready], rest[n_ready], rest[n_ready + 1:]
        acc_ref, stage_ref, shard_send, shard_recv = rest[:4]
        scatter_sems, gather_sems = rest[4:7], rest[7:10]
        s, k = pl.program_id(0), pl.program_id(1)
        x, y, c, me = _place()

        @pl.when(jnp.logical_and(s == 0, k == 0))
        def _():
            for cp in _scatter_copies(ready_refs, part_refs, scatter_sems)[0]:
                cp.start()
            for cp in _gather_copies([small_ref], [small_all_ref], gather_sems)[0]:
                cp.start()

        @pl.when(k == 0)
        def _():
            acc_ref[...] = jnp.zeros_like(acc_ref)

        acc_ref[...] += lax.dot_general(a_ref[...], b_ref[...], TN, preferred_element_type=F32)

        def shard_copy(step):
            owner = (me + 1 + step) % N_DEV
            slot = jnp.bitwise_xor(me, owner) - 1
            return pltpu.make_async_remote_copy(
                src_ref=stage_ref.at[step % 2], dst_ref=got_ref.at[slot], send_sem=shard_send.at[step],
                recv_sem=shard_recv.at[slot], device_id=(owner // 4, (owner // 2) % 2, owner % 2), device_id_type=MESH)

        for step in range(last_remote + 1):
            @pl.when(jnp.logical_and(s == step, k == nk - 1))
            def _(step=step):
                if step >= 2:
                    shard_copy(step - 2).wait_send()
                stage_ref[step % 2] = acc_ref[...].astype(BF16)
                shard_copy(step).start()

        @pl.when(jnp.logical_and(s == N_DEV - 1, k == nk - 1))
        def _():
            own_ref[...] = acc_ref[...]
            shard_copy(last_remote - 1).wait_send()
            shard_copy(last_remote).wait_send()
            for slot in range(N_DEV - 1):
                pltpu.make_async_remote_copy(
                    src_ref=stage_ref.at[0], dst_ref=got_ref.at[slot], send_sem=shard_send.at[0],
                    recv_sem=shard_recv.at[slot], device_id=(x, y, c), device_id_type=MESH).wait_recv()
            for cp in _scatter_copies(ready_refs, part_refs, scatter_sems)[1]:
                cp.wait()
            for cp in _gather_copies([small_ref], [small_all_ref], gather_sems)[1]:
                cp.wait()

    grid_spec = pltpu.PrefetchScalarGridSpec(
        num_scalar_prefetch=1, grid=(N_DEV, nk),
        in_specs=[pl.BlockSpec((bt, r), lambda s, k, me: (k, 0)),
                  pl.BlockSpec((bt, n_col), lambda s, k, me: (k, (me[0] + 1 + s) % N_DEV))] + [ANY_SPEC] * (n_ready + 1),
        out_specs=[pl.BlockSpec((r, n_col), lambda s, k, me: (0, 0)), ANY_SPEC] + [ANY_SPEC] * (n_ready + 1),
        scratch_shapes=[pltpu.VMEM((r, n_col), F32), pltpu.VMEM((2, r, n_col), BF16),
                        pltpu.SemaphoreType.DMA((N_DEV - 1,)), pltpu.SemaphoreType.DMA((N_DEV - 1,))]
        + _comm_sems(n_ready) + _comm_sems(1))
    outs = pl.pallas_call(
        body, name="w_in_grad_exchange", grid_spec=grid_spec,
        out_shape=[jax.ShapeDtypeStruct((r, n_col), F32), jax.ShapeDtypeStruct((N_DEV - 1, r, n_col), BF16)]
        + [jax.ShapeDtypeStruct(p.shape, p.dtype) for p in ready_stacks] + _stacked([small]),
        compiler_params=_params(2),
    )(me, a, b, *ready_stacks, small)
    return outs[0], outs[1], outs[2:2 + n_ready], outs[2 + n_ready]


def _proj_fwd(x1, pos, rc, kv_norm, w_dkv, ckv_norm, w_ukv, b_norm, b_w_in, q_norm, w_uq):
    t = x1.shape[0]
    tm = ROW_TILE

    def body(x1_ref, pos_ref, rc_ref, kvn_ref, wdkv_ref, ckvn_ref, wukv_ref, bn_ref, bwin_ref, qn_ref, wuq_ref,
             ckr_ref, cq_ref, gb_ref, q_ref, k_ref, v_ref):
        ct, s1, s2 = _rope_tables(pos_ref[...], rc_ref[...])
        xv = x1_ref[...]
        xh = xv * _rstd(xv)
        ckr = _dot(xh * kvn_ref[...], wdkv_ref[...])
        ckr_ref[...] = ckr
        ckv = ckr[:, 0:KV_RANK]
        kv = _dot(ckv * _rstd(ckv) * ckvn_ref[...], wukv_ref[...])
        k_rope = _rope(ckr[:, KV_RANK:CKR_PAD], ct, s1, s2)
        lane = lax.broadcasted_iota(jnp.int32, (tm, HEAD_PAD), 1)
        low = lane < QK_NOPE
        for h in range(N_HEADS):
            k_ref[:, h * HEAD_PAD:(h + 1) * HEAD_PAD] = jnp.where(
                low, kv[:, h * HEAD_PAD:(h + 1) * HEAD_PAD], k_rope).astype(BF16)
        for p in range(N_HEADS // 2):
            even = kv[:, (2 * p) * HEAD_PAD:(2 * p + 1) * HEAD_PAD]
            odd = kv[:, (2 * p + 1) * HEAD_PAD:(2 * p + 2) * HEAD_PAD]
            v_ref[:, p * HEAD_PAD:(p + 1) * HEAD_PAD] = jnp.where(low, pltpu.roll(even, V_HEAD, 1), odd).astype(BF16)
        pb = _dot(xh * bn_ref[...], bwin_ref[...])
        cq = pb[:, 0:Q_RANK]
        cq_ref[...] = cq
        gb_ref[...] = pb[:, Q_RANK:Q_RANK + E_B]
        q = _dot(cq * _rstd(cq) * qn_ref[...], wuq_ref[...])
        for h in range(N_HEADS):
            q_ref[:, h * HEAD_PAD:(h + 1) * HEAD_PAD] = _rope(
                q[:, h * HEAD_PAD:(h + 1) * HEAD_PAD], ct, s1, s2).astype(BF16)

    tok = lambda width: pl.BlockSpec((tm, width), lambda i: (i, 0))
    weights = [kv_norm, w_dkv, ckv_norm, w_ukv, b_norm, b_w_in, q_norm, w_uq]
    wide = N_HEADS * HEAD_PAD
    return pl.pallas_call(
        body, name="latent_proj_fwd", grid=(t // tm,),
        in_specs=[tok(D_MODEL), tok(1), _const_spec(rc.shape)] + [_const_spec(w.shape) for w in weights],
        out_specs=[tok(CKR_PAD), tok(Q_RANK), tok(E_B), tok(wide), tok(wide), tok(E_B)],
        out_shape=[jax.ShapeDtypeStruct((t, CKR_PAD), F32), jax.ShapeDtypeStruct((t, Q_RANK), F32),
                   jax.ShapeDtypeStruct((t, E_B), F32), jax.ShapeDtypeStruct((t, wide), BF16),
                   jax.ShapeDtypeStruct((t, wide), BF16), jax.ShapeDtypeStruct((t, E_B), BF16)],
        compiler_params=_params(1),
    )(x1, pos, rc, *weights)


def _proj_bwd(dq, dk, dv, dgb, cq, ckr, x1, dx2, pos, rc, kv_norm, w_dkv, ckv_norm, w_ukv, b_norm, b_w_in, q_norm, w_uq):
    t = x1.shape[0]
    tm = ROW_TILE
    wide = N_HEADS * HEAD_PAD

    def body(dq_ref, dk_ref, dv_ref, dgb_ref, cq_ref, ckr_ref, x1_ref, dx2_ref, pos_ref, rc_ref,
             kvn_ref, wdkv_ref, ckvn_ref, wukv_ref, bn_ref, bwin_ref, qn_ref, wuq_ref,
             dx1_ref, dwuq_ref, dwbin_ref, dwukv_ref, dwdkv_ref, dqn_ref, dbn_ref, dckvn_ref, dkvn_ref, dqu_ref, dkv_ref):
        @pl.when(pl.program_id(0) == 0)
        def _():
            for ref in (dwuq_ref, dwbin_ref, dwukv_ref, dwdkv_ref, dqn_ref, dbn_ref, dckvn_ref, dkvn_ref):
                ref[...] = jnp.zeros_like(ref)

        ct, s1, s2 = _rope_tables(pos_ref[...], rc_ref[...])
        lane = lax.broadcasted_iota(jnp.int32, (tm, HEAD_PAD), 1)
        low = lane < QK_NOPE
        for h in range(N_HEADS):
            dqu_ref[:, h * HEAD_PAD:(h + 1) * HEAD_PAD] = _rope(
                dq_ref[:, h * HEAD_PAD:(h + 1) * HEAD_PAD], ct, -s1, -s2).astype(BF16)
        cq = cq_ref[...]
        rq = _rstd(cq)
        cqh = cq * rq
        cqn = (cqh * qn_ref[...]).astype(BF16)
        for h in range(N_HEADS):
            dwuq_ref[h] += lax.dot_general(cqn, dqu_ref[:, h * HEAD_PAD:(h + 1) * HEAD_PAD], TN, preferred_element_type=F32)
        dcqn = lax.dot_general(dqu_ref[...], wuq_ref[...], NT, preferred_element_type=F32)
        dqn_ref[...] += jnp.sum(dcqn * cqh, axis=0, keepdims=True)
        dcq = _norm_bwd(dcqn * qn_ref[...], cqh, rq)
        dpb = jnp.concatenate([dcq, dgb_ref[...]], axis=1).astype(BF16)
        xv = x1_ref[...]
        r = _rstd(xv)
        xh = xv * r
        dwbin_ref[...] += _dot_tn(xh * bn_ref[...], dpb)
        dh3 = lax.dot_general(dpb, bwin_ref[...], NT, preferred_element_type=F32)
        dk_rope = jnp.zeros((tm, HEAD_PAD), F32)
        for p in range(N_HEADS // 2):
            dvp = dv_ref[:, p * HEAD_PAD:(p + 1) * HEAD_PAD]
            dk_even = dk_ref[:, (2 * p) * HEAD_PAD:(2 * p + 1) * HEAD_PAD]
            dk_odd = dk_ref[:, (2 * p + 1) * HEAD_PAD:(2 * p + 2) * HEAD_PAD]
            dkv_ref[:, (2 * p) * HEAD_PAD:(2 * p + 1) * HEAD_PAD] = jnp.where(low, dk_even, pltpu.roll(dvp, V_HEAD, 1)).astype(BF16)
            dkv_ref[:, (2 * p + 1) * HEAD_PAD:(2 * p + 2) * HEAD_PAD] = jnp.where(low, dk_odd, dvp).astype(BF16)
            dk_rope += dk_even + dk_odd
        rope_lanes = jnp.logical_and(lane >= ROPE_LO, lane < ROPE_LO + QK_ROPE)
        dk_rope = jnp.where(rope_lanes, _rope(dk_rope, ct, -s1, -s2), 0.0)
        ckv = ckr_ref[:, 0:KV_RANK]
        rk = _rstd(ckv)
        ckh = ckv * rk
        ckvn = (ckh * ckvn_ref[...]).astype(BF16)
        for h in range(N_HEADS):
            dwukv_ref[h] += lax.dot_general(ckvn, dkv_ref[:, h * HEAD_PAD:(h + 1) * HEAD_PAD], TN, preferred_element_type=F32)
        dckvn = lax.dot_general(dkv_ref[...], wukv_ref[...], NT, preferred_element_type=F32)
        dckvn_ref[...] += jnp.sum(dckvn * ckh, axis=0, keepdims=True)
        dckv = _norm_bwd(dckvn * ckvn_ref[...], ckh, rk)
        dckr = jnp.concatenate([dckv, dk_rope], axis=1).astype(BF16)
        dwdkv_ref[...] += _dot_tn(xh * kvn_ref[...], dckr)
        dh2 = lax.dot_general(dckr, wdkv_ref[...], NT, preferred_element_type=F32)
        dkvn_ref[...] += jnp.sum(dh2 * xh, axis=0, keepdims=True)
        dbn_ref[...] += jnp.sum(dh3 * xh, axis=0, keepdims=True)
        dx1_ref[...] = dx2_ref[...] + _norm_bwd(dh2 * kvn_ref[...] + dh3 * bn_ref[...], xh, r)

    tok = lambda width: pl.BlockSpec((tm, width), lambda i: (i, 0))
    weights = [kv_norm, w_dkv, ckv_norm, w_ukv, b_norm, b_w_in, q_norm, w_uq]
    acc_shapes = [(N_HEADS, Q_RANK, HEAD_PAD), (D_MODEL, Q_RANK + E_B), (N_HEADS, KV_RANK, HEAD_PAD), (D_MODEL, CKR_PAD),
                  (1, Q_RANK), (1, D_MODEL), (1, KV_RANK), (1, D_MODEL)]
    return pl.pallas_call(
        body, name="latent_proj_bwd", grid=(t // tm,),
        in_specs=[tok(wide), tok(wide), tok(E_B), tok(E_B), tok(Q_RANK), tok(CKR_PAD), tok(D_MODEL), tok(D_MODEL), tok(1),
                  _const_spec(rc.shape)] + [_const_spec(w.shape) for w in weights],
        out_specs=[tok(D_MODEL)] + [_acc_spec(s) for s in acc_shapes],
        out_shape=[jax.ShapeDtypeStruct((t, D_MODEL), F32)] + [jax.ShapeDtypeStruct(s, F32) for s in acc_shapes],
        scratch_shapes=[pltpu.VMEM((tm, wide), BF16), pltpu.VMEM((tm, wide), BF16)],
        compiler_params=_params(1),
    )(dq, dk, dv, dgb, cq, ckr, x1, dx2, pos, rc, *weights)


def _attn_fwd(q, k, v, n_seq, seq):
    tq, tk = ATTN_FWD_TILES
    ratio = tq // tk
    nq = seq // tq
    pair = 2 * HEAD_PAD

    def body(q_ref, k_ref, v_ref, o_ref, lr_ref):
        i = pl.program_id(2)
        row = lax.broadcasted_iota(jnp.int32, (tq, tk), 0)
        col = lax.broadcasted_iota(jnp.int32, (tq, tk), 1)
        qs = [q_ref[:, hh * HEAD_PAD:(hh + 1) * HEAD_PAD] for hh in range(2)]

        def step(j, carry, shift):
            start = pl.multiple_of(j * tk, tk)
            vj = v_ref[pl.ds(start, tk), :]
            out = []
            for hh in range(2):
                m, l, acc = carry[hh]
                s = lax.dot_general(qs[hh], k_ref[pl.ds(start, tk), hh * HEAD_PAD:(hh + 1) * HEAD_PAD], NT,
                                    preferred_element_type=F32) * SOFTMAX_SCALE
                if shift is not None:
                    s = jnp.where(col + shift <= row, s, -jnp.inf)
                m_new = jnp.maximum(m, jnp.max(s, axis=-1, keepdims=True))
                alpha = jnp.exp(m - m_new)
                p = jnp.exp(s - m_new)
                l = alpha * l + jnp.sum(p, axis=-1, keepdims=True)
                acc = alpha * acc + jnp.dot(p.astype(BF16), vj, preferred_element_type=F32)
                out.append((m_new, l, acc))
            return tuple(out)

        one = (jnp.full((tq, 1), -jnp.inf, F32), jnp.zeros((tq, 1), F32), jnp.zeros((tq, HEAD_PAD), F32))
        carry = lax.fori_loop(0, i * ratio, functools.partial(step, shift=None), (one, one))
        for d in range(ratio):
            carry = step(i * ratio + d, carry, d * tk)
        halves = []
        for hh in range(2):
            m, l, acc = carry[hh]
            halves.append(acc / l)
            lse = m + jnp.log(l)
            lr_ref[hh] = jnp.broadcast_to(lse, (tq, HEAD_PAD)).T[0:8, :]
        lane = lax.broadcasted_iota(jnp.int32, (tq, HEAD_PAD), 1)
        o_ref[...] = jnp.where(lane < V_HEAD, halves[0], halves[1])

    t = n_seq * seq
    return pl.pallas_call(
        body, name="attention_fwd", grid=(n_seq, N_HEADS // 2, nq),
        in_specs=[pl.BlockSpec((tq, pair), lambda s, p, i: (s * nq + i, p)),
                  pl.BlockSpec((seq, pair), lambda s, p, i: (s, p)),
                  pl.BlockSpec((seq, HEAD_PAD), lambda s, p, i: (s, p))],
        out_specs=[pl.BlockSpec((tq, HEAD_PAD), lambda s, p, i: (s * nq + i, p)),
                   pl.BlockSpec((None, 2, 8, tq), lambda s, p, i: (s, p, 0, i))],
        out_shape=[jax.ShapeDtypeStruct((t, E_B), F32), jax.ShapeDtypeStruct((n_seq, N_HEADS, 8, seq), F32)],
        compiler_params=_params(3),
    )(q, k, v)


def _attn_bwd(q, k, v, do, lse_row, delta_row, n_seq, seq):
    tk, tq = ATTN_BWD_TILES
    ratio = tk // tq
    nk = seq // tk
    n_inner = seq // tq
    pair = 2 * HEAD_PAD

    def body(q_ref, k_ref, v_ref, do_ref, lr_ref, dr_ref, dk_ref, dv_ref, dq_ref):
        j = pl.program_id(2)

        @pl.when(j == 0)
        def _():
            dq_ref[...] = jnp.zeros_like(dq_ref)

        row = lax.broadcasted_iota(jnp.int32, (tk, tq), 0)
        col = lax.broadcasted_iota(jnp.int32, (tk, tq), 1)
        lane = lax.broadcasted_iota(jnp.int32, (tk, HEAD_PAD), 1)
        ks = [k_ref[:, hh * HEAD_PAD:(hh + 1) * HEAD_PAD] for hh in range(2)]
        vv = v_ref[...]
        vs = [jnp.where(lane < V_HEAD, vv, jnp.zeros_like(vv)), jnp.where(lane >= V_HEAD, vv, jnp.zeros_like(vv))]

        def step(i, carry, shift):
            start = pl.multiple_of(i * tq, tq)
            doi = do_ref[pl.ds(start, tq), :]
            out = []
            for hh in range(2):
                dk_acc, dv_acc = carry[hh]
                qi = q_ref[pl.ds(start, tq), hh * HEAD_PAD:(hh + 1) * HEAD_PAD]
                st = lax.dot_general(ks[hh], qi, NT, preferred_element_type=F32) * SOFTMAX_SCALE
                pt = jnp.exp(st - lr_ref[hh, 0:1, pl.ds(start, tq)])
                if shift is not None:
                    pt = jnp.where(col + shift >= row, pt, 0.0)
                dv_acc = dv_acc + jnp.dot(pt.astype(BF16), doi, preferred_element_type=F32)
                dpt = lax.dot_general(vs[hh], doi, NT, preferred_element_type=F32)
                dst = (pt * (dpt - dr_ref[hh, 0:1, pl.ds(start, tq)]) * SOFTMAX_SCALE).astype(BF16)
                dk_acc = dk_acc + jnp.dot(dst, qi, preferred_element_type=F32)
                dq_ref[pl.ds(start, tq), hh * HEAD_PAD:(hh + 1) * HEAD_PAD] += lax.dot_general(
                    dst, ks[hh], TN, preferred_element_type=F32)
                out.append((dk_acc, dv_acc))
            return tuple(out)

        one = (jnp.zeros((tk, HEAD_PAD), F32), jnp.zeros((tk, HEAD_PAD), F32))
        carry = (one, one)
        for d in range(ratio):
            carry = step(j * ratio + d, carry, d * tq)
        carry = lax.fori_loop((j + 1) * ratio, n_inner, functools.partial(step, shift=None), carry)
        for hh in range(2):
            dk_ref[:, hh * HEAD_PAD:(hh + 1) * HEAD_PAD] = carry[hh][0]
        dv_ref[...] = jnp.where(lane < V_HEAD, carry[0][1], carry[1][1])

    t = n_seq * seq
    wide = N_HEADS * HEAD_PAD
    return pl.pallas_call(
        body, name="attention_bwd", grid=(n_seq, N_HEADS // 2, nk),
        in_specs=[pl.BlockSpec((seq, pair), lambda s, p, j: (s, p)),
                  pl.BlockSpec((tk, pair), lambda s, p, j: (s * nk + j, p)),
                  pl.BlockSpec((tk, HEAD_PAD), lambda s, p, j: (s * nk + j, p)),
                  pl.BlockSpec((seq, HEAD_PAD), lambda s, p, j: (s, p)),
                  pl.BlockSpec((None, 2, 8, seq), lambda s, p, j: (s, p, 0, 0)),
                  pl.BlockSpec((None, 2, 8, seq), lambda s, p, j: (s, p, 0, 0))],
        out_specs=[pl.BlockSpec((tk, pair), lambda s, p, j: (s * nk + j, p)),
                   pl.BlockSpec((tk, HEAD_PAD), lambda s, p, j: (s * nk + j, p)),
                   pl.BlockSpec((seq, pair), lambda s, p, j: (s, p))],
        out_shape=[jax.ShapeDtypeStruct((t, wide), F32), jax.ShapeDtypeStruct((t, E_B), F32),
                   jax.ShapeDtypeStruct((t, wide), F32)],
        compiler_params=_params(3),
    )(q, k, v, do, lse_row, delta_row)


def _head_tail(o, gb, x1, w_out, final_norm, target, n_seq, seq):
    tm = ROW_TILE
    nt = seq // tm
    n_col = D_MODEL // N_DEV

    def body(o_ref, gb_ref, x1_ref, wout_ref, fn_ref, tgt_ref,
             dx2_ref, do_ref, dgb_ref, dr_ref, dwout_ref, dfn_ref, loss_ref):
        first = jnp.logical_and(pl.program_id(0) == 0, pl.program_id(1) == 0)

        @pl.when(first)
        def _():
            dwout_ref[...] = jnp.zeros_like(dwout_ref)
            dfn_ref[...] = jnp.zeros_like(dfn_ref)
            loss_ref[...] = jnp.zeros_like(loss_ref)

        ov, g = o_ref[...], gb_ref[...]
        silu, dsilu = _silu_parts(g)
        gated = (ov * silu).astype(BF16)
        x2 = x1_ref[...] + jnp.dot(gated, wout_ref[...], preferred_element_type=F32)
        r = _rstd(x2)
        xh = x2 * r
        err = xh * fn_ref[...] - tgt_ref[...]
        loss_ref[...] += 0.5 * jnp.sum(jnp.mean(err * err, axis=-1, keepdims=True), axis=0, keepdims=True)
        dy = err / D_MODEL
        dfn_ref[...] += jnp.sum(dy * xh, axis=0, keepdims=True)
        dx2 = _norm_bwd(dy * fn_ref[...], xh, r)
        dx2_ref[...] = dx2
        dx2b = dx2.astype(BF16)
        dw = lax.dot_general(gated, dx2b, TN, preferred_element_type=F32)
        for d in range(N_DEV):
            dwout_ref[d] += dw[:, d * n_col:(d + 1) * n_col]
        dgated = lax.dot_general(dx2b, wout_ref[...], NT, preferred_element_type=F32)
        do = dgated * silu
        do_ref[...] = do.astype(BF16)
        dgb_ref[...] = dgated * ov * dsilu
        prod = do * ov
        lane = lax.broadcasted_iota(jnp.int32, (tm, HEAD_PAD), 1)
        for p in range(N_HEADS // 2):
            blk = prod[:, p * HEAD_PAD:(p + 1) * HEAD_PAD]
            for hh in range(2):
                mine = (lane < V_HEAD) if hh == 0 else (lane >= V_HEAD)
                delta = jnp.sum(jnp.where(mine, blk, 0.0), axis=-1, keepdims=True)
                dr_ref[2 * p + hh] = jnp.broadcast_to(delta, (tm, HEAD_PAD)).T[0:8, :]

    tok = lambda width: pl.BlockSpec((tm, width), lambda s, i: (s * nt + i, 0))
    t = n_seq * seq
    return pl.pallas_call(
        body, name="head_tail", grid=(n_seq, nt),
        in_specs=[tok(E_B), tok(E_B), tok(D_MODEL), _const_spec(w_out.shape), _const_spec((1, D_MODEL)), tok(D_MODEL)],
        out_specs=[tok(D_MODEL), tok(E_B), tok(E_B),
                   pl.BlockSpec((None, N_HEADS, 8, tm), lambda s, i: (s, 0, 0, i)),
                   _acc_spec((N_DEV, E_B, n_col)), _acc_spec((1, D_MODEL)), _acc_spec((1, 1))],
        out_shape=[jax.ShapeDtypeStruct((t, D_MODEL), F32), jax.ShapeDtypeStruct((t, E_B), BF16),
                   jax.ShapeDtypeStruct((t, E_B), F32),
                   jax.ShapeDtypeStruct((n_seq, N_HEADS, 8, seq), F32), jax.ShapeDtypeStruct((N_DEV, E_B, n_col), F32),
                   jax.ShapeDtypeStruct((1, D_MODEL), F32), jax.ShapeDtypeStruct((1, 1), F32)],
        compiler_params=_params(2),
    )(o, gb, x1, w_out, final_norm, target)


def _adamw_math(w, g, m, v):
    m = ADAM_B1 * m + (1.0 - ADAM_B1) * g
    v = ADAM_B2 * v + (1.0 - ADAM_B2) * jnp.square(g)
    m_hat = m / (1.0 - ADAM_B1 ** ADAM_STEP)
    v_hat = v / (1.0 - ADAM_B2 ** ADAM_STEP)
    delta = -ADAM_LR * (m_hat / (jnp.sqrt(v_hat) + ADAM_EPS) + ADAM_WD * w)
    return delta, m, v


def _adamw_reduce(own, parts, w, m, v):
    rows, cols = w.shape
    br = 256 if rows % 256 == 0 else 128
    n_parts = parts.shape[0]

    def body(*refs):
        p_ref, w_ref, m_ref, v_ref, g_ref, d_ref, nm_ref, nv_ref = refs[-8:]
        g = p_ref[0].astype(F32) if own is None else refs[0][...] + p_ref[0].astype(F32)
        for k in range(1, n_parts):
            g = g + p_ref[k].astype(F32)
        g_ref[...] = g
        d_ref[...], nm_ref[...], nv_ref[...] = _adamw_math(w_ref[...], g, m_ref[...], v_ref[...])

    blk = pl.BlockSpec((br, cols), lambda i: (i, 0))
    first = [] if own is None else [own]
    return pl.pallas_call(
        body, name="adamw_reduce", grid=(rows // br,),
        in_specs=[blk] * len(first) + [pl.BlockSpec((n_parts, br, cols), lambda i: (0, i, 0)), blk, blk, blk],
        out_specs=[blk] * 4, out_shape=[jax.ShapeDtypeStruct((rows, cols), F32)] * 4,
        compiler_params=_params(1),
    )(*first, parts, w, m, v)


def _sum_parts(parts):
    def body(p_ref, o_ref):
        g = p_ref[0]
        for k in range(1, N_DEV):
            g = g + p_ref[k]
        o_ref[...] = g

    return pl.pallas_call(body, name="small_grad_sum", out_shape=jax.ShapeDtypeStruct(parts.shape[1:], F32))(parts)


def _adamw_small(gs, ws, ms, vs):
    n = len(gs)

    def body(*refs):
        ins, outs = refs[:4 * n], refs[4 * n:]
        for j in range(n):
            g_ref, w_ref, m_ref, v_ref = ins[j], ins[n + j], ins[2 * n + j], ins[3 * n + j]
            outs[3 * j][...], outs[3 * j + 1][...], outs[3 * j + 2][...] = _adamw_math(
                w_ref[...], g_ref[...], m_ref[...], v_ref[...])

    shapes = [jax.ShapeDtypeStruct(g.shape, F32) for g in gs for _ in range(3)]
    flat = pl.pallas_call(body, name="adamw_small", out_shape=shapes)(*gs, *ws, *ms, *vs)
    return [tuple(flat[3 * j:3 * j + 3]) for j in range(n)]


SMALL_ROWS = 8
LOSS_LANE = D_MODEL - 1


def _pack_small(g_fn, g_kvn, g_bn, g_an, g_ckvn, g_qn, g_conv, loss_part):
    def body(fn_ref, kvn_ref, bn_ref, an_ref, ckvn_ref, qn_ref, conv_ref, loss_ref, o_ref):
        o_ref[0:1, :] = fn_ref[...]
        o_ref[1:2, :] = kvn_ref[...]
        o_ref[2:3, :] = bn_ref[...]
        o_ref[3:4, :] = an_ref[...]
        lane = lax.broadcasted_iota(jnp.int32, (1, D_MODEL), 1)
        o_ref[4:5, :] = jnp.where(lane == LOSS_LANE, loss_ref[...], 0.0)
        o_ref[4:5, 0:KV_RANK] = ckvn_ref[...]
        o_ref[4:5, KV_RANK:KV_RANK + Q_RANK] = qn_ref[...]
        o_ref[5:8, :] = conv_ref[0:3, :]

    return pl.pallas_call(body, name="pack_small", out_shape=jax.ShapeDtypeStruct((SMALL_ROWS, D_MODEL), F32))(
        g_fn, g_kvn, g_bn, g_an, g_ckvn, g_qn, g_conv, loss_part)


def _pad_cols(a, width):
    return jnp.pad(a, ((0, 0), (0, width - a.shape[1])))


def _dkv_to_padded(a):
    r = a.shape[0]
    z = jnp.zeros((r, ROPE_LO), a.dtype)
    z2 = jnp.zeros((r, HEAD_PAD - ROPE_LO - QK_ROPE), a.dtype)
    return jnp.concatenate([a[:, :KV_RANK], z, a[:, KV_RANK:], z2], axis=1)


def _dkv_from_padded(a):
    return jnp.concatenate([a[:, :KV_RANK], a[:, KV_RANK + ROPE_LO:KV_RANK + ROPE_LO + QK_ROPE]], axis=1)


def _unstack_cols(a):
    return jnp.transpose(a, (1, 0, 2)).reshape(a.shape[1], N_DEV * a.shape[2])


def kernel(x, positions, a_norm, a_w_in, a_conv, a_w_out, kv_norm, w_dkv, ckv_norm, w_ukv, b_norm, b_w_in, b_q_norm, b_w_uq, b_w_out, final_norm, loss_target, m_a_norm, m_a_w_in, m_a_conv, m_a_w_out, m_kv_norm, m_w_dkv, m_ckv_norm, m_w_ukv, m_b_norm, m_b_w_in, m_b_q_norm, m_b_w_uq, m_b_w_out, m_final_norm, v_a_norm, v_a_w_in, v_a_conv, v_a_w_out, v_kv_norm, v_w_dkv, v_ckv_norm, v_w_ukv, v_b_norm, v_b_w_in, v_b_q_norm, v_b_w_uq, v_b_w_out, v_final_norm):
    n_seq, seq, _ = x.shape
    t = n_seq * seq
    me = 4 * lax.axis_index("x") + 2 * lax.axis_index("y") + lax.axis_index("c")

    big = {
        "a_w_in": (a_w_in[0], m_a_w_in[0], v_a_w_in[0]),
        "a_w_out": (a_w_out[0], m_a_w_out[0], v_a_w_out[0]),
        "w_dkv": tuple(_dkv_to_padded(a) for a in (w_dkv, m_w_dkv, v_w_dkv)),
        "w_ukv": (w_ukv, m_w_ukv, v_w_ukv),
        "b_w_in": (b_w_in[0], m_b_w_in[0], v_b_w_in[0]),
        "b_w_uq": tuple(_pad_cols(a[0], HEAD_PAD) for a in (b_w_uq, m_b_w_uq, v_b_w_uq)),
        "b_w_out": (b_w_out[0], m_b_w_out[0], v_b_w_out[0]),
    }
    names = list(big)
    first_names, later_names = names[:2], names[2:]
    gathered = _all_gather([big[n][0].astype(BF16) for n in first_names] + [a_norm, a_conv[0]])
    a_norm_f = gathered[2].reshape(1, D_MODEL)
    a_conv_f = _unstack_cols(gathered[3])
    w_a_in = gathered[0]
    w_a_out = gathered[1].reshape(D_MODEL, D_MODEL)

    x2d = x.reshape(t, D_MODEL)
    tgt = loss_target.reshape(t, D_MODEL)
    pos = positions.astype(F32).reshape(t, 1)
    rc = _rope_consts()
    kvn, ckvn, bn, qn, fn = kv_norm.reshape(1, -1), ckv_norm.reshape(1, -1), b_norm, b_q_norm, final_norm.reshape(1, -1)

    x1, proj, conv, later = _conv_fwd(x2d, a_norm_f, w_a_in, a_conv_f, w_a_out, n_seq, seq,
                                      [big[n][0].astype(BF16) for n in later_names])
    full = dict(zip(later_names, later))
    w_dkv_f = full["w_dkv"].reshape(D_MODEL, CKR_PAD)
    w_ukv_f = _unstack_cols(full["w_ukv"])
    w_b_in = full["b_w_in"].reshape(D_MODEL, Q_RANK + E_B)
    w_uq_f = _unstack_cols(full["b_w_uq"])
    w_b_out = _unstack_cols(full["b_w_out"])
    layer_b = (kvn, w_dkv_f, ckvn, w_ukv_f, bn, w_b_in, qn, w_uq_f)
    ckr, cq, gb, q, k, v = _proj_fwd(x1, pos, rc, *layer_b)
    o, lse_row = _attn_fwd(q, k, v, n_seq, seq)
    dx2, do, dgb, delta_row, g_b_out, g_fn, loss_part = _head_tail(o, gb, x1, w_b_out, fn, tgt, n_seq, seq)
    dk, dv, dq = _attn_bwd(q, k, v, do, lse_row, delta_row, n_seq, seq)
    dx1, g_uq, g_b_in, g_ukv, g_dkv, g_qn, g_bn, g_ckvn, g_kvn = _proj_bwd(
        dq, dk, dv, dgb, cq, ckr, x1, dx2, pos, rc, *layer_b)
    stacks = {
        "w_dkv": g_dkv.reshape(N_DEV, D_MODEL // N_DEV, CKR_PAD),
        "w_ukv": g_ukv,
        "b_w_in": g_b_in.reshape(N_DEV, D_MODEL // N_DEV, Q_RANK + E_B),
        "b_w_uq": g_uq,
        "b_w_out": g_b_out,
    }
    (dx, h_a, dproj, g_a_out, g_an, g_conv), later_parts = _conv_bwd(
        dx1, x2d, proj, conv, a_norm_f, w_a_in, a_conv_f, w_a_out, n_seq, seq, [stacks[n] for n in later_names])
    parts = dict(zip(later_names, later_parts))
    small = _pack_small(g_fn, g_kvn, g_bn, g_an, g_ckvn, g_qn, g_conv, loss_part)
    a_in_own, a_in_got, (parts["a_w_out"],), small_parts = _w_in_grad_exchange(
        h_a, dproj, D_MODEL * 4 // N_DEV, me.reshape(1).astype(jnp.int32),
        [g_a_out.reshape(N_DEV, D_MODEL // N_DEV, D_MODEL)], small)

    outs = {"a_w_in": _adamw_reduce(a_in_own, a_in_got, *big["a_w_in"])}
    for n in names[1:]:
        outs[n] = _adamw_reduce(None, parts[n], *big[n])
    outs["w_dkv"] = tuple(_dkv_from_padded(a) for a in outs["w_dkv"])
    outs["b_w_uq"] = tuple(a[:, :QK_NOPE + QK_ROPE] for a in outs["b_w_uq"])
    for n in ("a_w_in", "a_w_out", "b_w_in", "b_w_uq", "b_w_out"):
        outs[n] = tuple(a[None] for a in outs[n])

    total = _sum_parts(small_parts)
    loss = total[4, LOSS_LANE]
    shard = D_MODEL // N_DEV
    g_small = {
        "final_norm": total[0], "kv_norm": total[1], "b_norm": total[2:3],
        "a_norm": lax.dynamic_slice_in_dim(total[3:4], me * shard, shard, axis=1),
        "ckv_norm": total[4, 0:KV_RANK], "b_q_norm": total[4:5, KV_RANK:KV_RANK + Q_RANK],
        "a_conv": lax.dynamic_slice_in_dim(total[5:8], me * shard, shard, axis=1)[None],
    }
    small_state = {
        "final_norm": (final_norm, m_final_norm, v_final_norm), "kv_norm": (kv_norm, m_kv_norm, v_kv_norm),
        "b_norm": (b_norm, m_b_norm, v_b_norm), "a_norm": (a_norm, m_a_norm, v_a_norm),
        "ckv_norm": (ckv_norm, m_ckv_norm, v_ckv_norm), "b_q_norm": (b_q_norm, m_b_q_norm, v_b_q_norm),
        "a_conv": (a_conv, m_a_conv, v_a_conv),
    }
    small_names = list(g_small)
    as2d = lambda a: a.reshape(-1, a.shape[-1])
    upd = _adamw_small([as2d(g_small[n]) for n in small_names],
                       *[[as2d(small_state[n][j]) for n in small_names] for j in range(3)])
    for n, u in zip(small_names, upd):
        outs[n] = (g_small[n],) + tuple(a.reshape(g_small[n].shape) for a in u)

    order = ["a_norm", "a_w_in", "a_conv", "a_w_out", "kv_norm", "w_dkv", "ckv_norm", "w_ukv", "b_norm", "b_w_in",
             "b_q_norm", "b_w_uq", "b_w_out", "final_norm"]
    result = [loss, dx.reshape(n_seq, seq, D_MODEL)]
    for j in range(4):
        result += [outs[n][j] for n in order]
    return tuple(result)
```

```python
import functools
import math

import numpy as np
import jax
import jax.numpy as jnp
from jax import lax
from jax.experimental import pallas as pl
from jax.experimental.pallas import tpu as pltpu

F32 = jnp.float32
BF16 = jnp.bfloat16

D_MODEL = 1024
N_HEADS = 8
QK_NOPE = 64
QK_ROPE = 32
V_HEAD = 64
KV_RANK = 256
Q_RANK = 384
E_B = N_HEADS * V_HEAD
HEAD_PAD = 128
CKR_PAD = KV_RANK + HEAD_PAD
ROPE_LO = QK_NOPE
ROPE_HALF = QK_ROPE // 2
ROPE_THETA = 10000.0
SOFTMAX_SCALE = 1.0 / math.sqrt(QK_NOPE + QK_ROPE)
LOG2_E = math.log2(math.e)
LN_2 = math.log(2.0)
EPS = 1e-6
N_DEV = 8

ADAM_LR = 0.001
ADAM_B1 = 0.9
ADAM_B2 = 0.999
ADAM_EPS = 1e-08
ADAM_WD = 0.01
ADAM_STEP = 10

ROW_TILE = 256
ATTN_FWD_TILES = (512, 512)
ATTN_BWD_TILES = (512, 512)
VMEM_LIMIT = 56 * 1024 * 1024

MESH = pl.DeviceIdType.MESH
NT = (((1,), (1,)), ((), ()))
TN = (((0,), (0,)), ((), ()))


def _dot(a, b):
    return jnp.dot(a.astype(BF16), b.astype(BF16), preferred_element_type=F32)


def _dot_nt(a, b):
    return lax.dot_general(a.astype(BF16), b.astype(BF16), NT, preferred_element_type=F32)


def _dot_tn(a, b):
    return lax.dot_general(a.astype(BF16), b.astype(BF16), TN, preferred_element_type=F32)


def _rstd(x):
    return lax.rsqrt(jnp.mean(x * x, axis=-1, keepdims=True) + EPS)


def _norm_bwd(a, xh, r):
    return r * (a - xh * jnp.mean(a * xh, axis=-1, keepdims=True))


def _silu_parts(g):
    sg = jax.nn.sigmoid(g)
    return g * sg, sg * (1.0 + g * (1.0 - sg))


def _rope_consts():
    inv = (ROPE_THETA ** (-np.arange(0, QK_ROPE, 2, dtype=np.float32) / QK_ROPE)).astype(np.float32)
    t = np.zeros((8, HEAD_PAD), np.float32)
    t[0, ROPE_LO:ROPE_LO + ROPE_HALF] = inv
    t[0, ROPE_LO + ROPE_HALF:ROPE_LO + QK_ROPE] = inv
    t[1, ROPE_LO:ROPE_LO + ROPE_HALF] = -1.0
    t[2, ROPE_LO + ROPE_HALF:ROPE_LO + QK_ROPE] = 1.0
    return jnp.asarray(t)


def _rope_tables(pos, rc):
    ang = pos * rc[0:1, :]
    cosv = jnp.cos(ang)
    sinv = jnp.sin(ang)
    return cosv, sinv * rc[1:2, :], sinv * rc[2:3, :]


def _rope(x, ct, s1, s2):
    up = pltpu.roll(x, HEAD_PAD - ROPE_HALF, 1)
    dn = pltpu.roll(x, ROPE_HALF, 1)
    return x * ct + up * s1 + dn * s2


def _const_spec(shape):
    nd = len(shape)
    return pl.BlockSpec(shape, lambda *_: (0,) * nd, pipeline_mode=pl.Buffered(1))


def _acc_spec(shape):
    nd = len(shape)
    return pl.BlockSpec(shape, lambda *_: (0,) * nd)


def _params(n_axes):
    return pltpu.CompilerParams(dimension_semantics=("arbitrary",) * n_axes, vmem_limit_bytes=VMEM_LIMIT)


def _place():
    x, y, c = lax.axis_index("x"), lax.axis_index("y"), lax.axis_index("c")
    return x, y, c, 4 * x + 2 * y + c


def _peer(x, y, c, mask):
    px = 1 - x if mask & 4 else x
    py = 1 - y if mask & 2 else y
    pc = 1 - c if mask & 1 else c
    return (px, py, pc), 4 * px + 2 * py + pc


ANY_SPEC = pl.BlockSpec(memory_space=pl.ANY)


def _comm_sems(n):
    return [pltpu.SemaphoreType.DMA((n, N_DEV - 1)), pltpu.SemaphoreType.DMA((n, N_DEV - 1)), pltpu.SemaphoreType.DMA((n,))]


def _gather_copies(ins, outs, sems):
    send_sems, recv_sems, local_sems = sems
    x, y, c, me = _place()
    starts, waits = [], []
    for w in range(len(ins)):
        mine = pltpu.make_async_copy(ins[w], outs[w].at[me], local_sems.at[w])
        starts.append(mine)
        waits.append(mine)
        for mask in range(1, N_DEV):
            peer, peer_idx = _peer(x, y, c, mask)
            starts.append(pltpu.make_async_remote_copy(
                src_ref=ins[w], dst_ref=outs[w].at[me], send_sem=send_sems.at[w, mask - 1],
                recv_sem=recv_sems.at[w, mask - 1], device_id=peer, device_id_type=MESH))
            waits.append(pltpu.make_async_remote_copy(
                src_ref=ins[w], dst_ref=outs[w].at[peer_idx], send_sem=send_sems.at[w, mask - 1],
                recv_sem=recv_sems.at[w, mask - 1], device_id=peer, device_id_type=MESH))
    return starts, waits


def _scatter_copies(ins, outs, sems):
    send_sems, recv_sems, local_sems = sems
    x, y, c, me = _place()
    copies = []
    for w in range(len(ins)):
        copies.append(pltpu.make_async_copy(ins[w].at[me], outs[w].at[0], local_sems.at[w]))
        for mask in range(1, N_DEV):
            peer, peer_idx = _peer(x, y, c, mask)
            copies.append(pltpu.make_async_remote_copy(
                src_ref=ins[w].at[peer_idx], dst_ref=outs[w].at[mask], send_sem=send_sems.at[w, mask - 1],
                recv_sem=recv_sems.at[w, mask - 1], device_id=peer, device_id_type=MESH))
    return copies, copies


def _stacked(arrays):
    return [jax.ShapeDtypeStruct((N_DEV,) + a.shape, a.dtype) for a in arrays]


def _all_gather(shards):
    n = len(shards)

    def body(*refs):
        starts, waits = _gather_copies(refs[:n], refs[n:2 * n], refs[2 * n:])
        for cp in starts:
            cp.start()
        for cp in waits:
            cp.wait()

    return pl.pallas_call(
        body, name="weight_all_gather", out_shape=_stacked(shards),
        in_specs=[ANY_SPEC] * n, out_specs=[ANY_SPEC] * n, scratch_shapes=_comm_sems(n),
    )(*shards)


def _conv_fwd(x, a_norm, w_in, conv_w, w_out, n_seq, seq, later_shards):
    tm = ROW_TILE
    nt = seq // tm
    n_col = w_in.shape[2]
    n_later = len(later_shards)

    def body(x_ref, an_ref, win_ref, cw_ref, wout_ref, *rest):
        shard_refs, rest = rest[:n_later], rest[n_later:]
        x1_ref, proj_ref, conv_ref = rest[:3]
        stack_refs, rest = rest[3:3 + n_later], rest[3 + n_later:]
        prev_ref, sems = rest[0], rest[1:]
        step = pl.program_id(0) * nt + pl.program_id(1)

        @pl.when(step == 0)
        def _():
            for cp in _gather_copies(shard_refs, stack_refs, sems)[0]:
                cp.start()

        @pl.when(pl.program_id(1) == 0)
        def _():
            prev_ref[...] = jnp.zeros_like(prev_ref)

        xv = x_ref[...]
        h = (xv * _rstd(xv) * an_ref[...]).astype(BF16)
        for d in range(N_DEV):
            proj_ref[:, d * n_col:(d + 1) * n_col] = jnp.dot(h, win_ref[d], preferred_element_type=F32)
        b = proj_ref[:, 0:D_MODEL]
        v = proj_ref[:, D_MODEL:2 * D_MODEL] * proj_ref[:, 2 * D_MODEL:3 * D_MODEL]
        g = proj_ref[:, 3 * D_MODEL:4 * D_MODEL]
        w0, w1, w2 = cw_ref[0:1, :], cw_ref[1:2, :], cw_ref[2:3, :]
        conv_ref[...] = w0 * pltpu.roll(v, 2, 0) + w1 * pltpu.roll(v, 1, 0) + w2 * v
        rows = lax.broadcasted_iota(jnp.int32, (8, D_MODEL), 0)
        p8, v8 = prev_ref[...], v[0:8]
        back1 = jnp.where(rows < 1, pltpu.roll(p8, 1, 0), pltpu.roll(v8, 1, 0))
        back2 = jnp.where(rows < 2, pltpu.roll(p8, 2, 0), pltpu.roll(v8, 2, 0))
        conv_ref[0:8, :] = w0 * back2 + w1 * back1 + w2 * v8
        prev_ref[...] = v[tm - 8:tm]
        silu, _ = _silu_parts(g)
        yv = silu * b * conv_ref[...]
        x1_ref[...] = xv + _dot(yv, wout_ref[...])

        @pl.when(step == n_seq * nt - 1)
        def _():
            for cp in _gather_copies(shard_refs, stack_refs, sems)[1]:
                cp.wait()

    tok = lambda width: pl.BlockSpec((tm, width), lambda s, i: (s * nt + i, 0))
    t = n_seq * seq
    outs = pl.pallas_call(
        body, name="conv_mixer_fwd", grid=(n_seq, nt),
        in_specs=[tok(D_MODEL), _const_spec((1, D_MODEL)), _const_spec(w_in.shape), _const_spec((3, D_MODEL)),
                  _const_spec(w_out.shape)] + [ANY_SPEC] * n_later,
        out_specs=[tok(D_MODEL), tok(4 * D_MODEL), tok(D_MODEL)] + [ANY_SPEC] * n_later,
        out_shape=[jax.ShapeDtypeStruct((t, D_MODEL), F32), jax.ShapeDtypeStruct((t, 4 * D_MODEL), F32),
                   jax.ShapeDtypeStruct((t, D_MODEL), F32)] + _stacked(later_shards),
        scratch_shapes=[pltpu.VMEM((8, D_MODEL), F32)] + _comm_sems(n_later),
        compiler_params=_params(2),
    )(x, a_norm, w_in, conv_w, w_out, *later_shards)
    return outs[0], outs[1], outs[2], outs[3:]


def _conv_bwd(dx1, x, proj, conv, a_norm, w_in, conv_w, w_out, n_seq, seq, ready_stacks):
    tm = ROW_TILE
    nt = seq // tm
    n_col = w_in.shape[2]
    n_ready = len(ready_stacks)

    def body(dx1_ref, x_ref, proj_ref, conv_ref, an_ref, win_ref, cw_ref, wout_ref, *rest):
        ready_refs, rest = rest[:n_ready], rest[n_ready:]
        dx_ref, h_ref, dproj_ref, dwout_ref, dan_ref, dcw_ref = rest[:6]
        part_refs, rest = rest[6:6 + n_ready], rest[6 + n_ready:]
        next_ref, d1_ref, d2_ref = rest[:3]
        sems = rest[3:]
        step = pl.program_id(0) * nt + pl.program_id(1)
        first = step == 0

        @pl.when(first)
        def _():
            for cp in _scatter_copies(ready_refs, part_refs, sems)[0]:
                cp.start()
            dwout_ref[...] = jnp.zeros_like(dwout_ref)
            dan_ref[...] = jnp.zeros_like(dan_ref)
            dcw_ref[...] = jnp.zeros_like(dcw_ref)

        @pl.when(pl.program_id(1) == 0)
        def _():
            next_ref[...] = jnp.zeros_like(next_ref)

        dx1v = dx1_ref[...]
        dy = _dot_nt(dx1v, wout_ref[...])
        b = proj_ref[:, 0:D_MODEL]
        cc = proj_ref[:, D_MODEL:2 * D_MODEL]
        u = proj_ref[:, 2 * D_MODEL:3 * D_MODEL]
        g = proj_ref[:, 3 * D_MODEL:4 * D_MODEL]
        cv = conv_ref[...]
        silu, dsilu = _silu_parts(g)
        dwout_ref[...] += _dot_tn(silu * b * cv, dx1v)
        dproj_ref[:, 3 * D_MODEL:4 * D_MODEL] = (dy * b * cv * dsilu).astype(BF16)
        dproj_ref[:, 0:D_MODEL] = (dy * silu * cv).astype(BF16)
        dconv = dy * silu * b
        d1_ref[...] = pltpu.roll(dconv, tm - 1, 0)
        d2_ref[...] = pltpu.roll(dconv, tm - 2, 0)
        rows = lax.broadcasted_iota(jnp.int32, (8, D_MODEL), 0)
        n8, c8 = next_ref[...], dconv[tm - 8:tm]
        d1_ref[tm - 8:tm, :] = jnp.where(rows >= 7, pltpu.roll(n8, 7, 0), pltpu.roll(c8, 7, 0))
        d2_ref[tm - 8:tm, :] = jnp.where(rows >= 6, pltpu.roll(n8, 6, 0), pltpu.roll(c8, 6, 0))
        next_ref[...] = dconv[0:8]
        d1, d2 = d1_ref[...], d2_ref[...]
        v = cc * u
        dcw_ref[0:1, :] += jnp.sum(d2 * v, axis=0, keepdims=True)
        dcw_ref[1:2, :] += jnp.sum(d1 * v, axis=0, keepdims=True)
        dcw_ref[2:3, :] += jnp.sum(dconv * v, axis=0, keepdims=True)
        dv = cw_ref[0:1, :] * d2 + cw_ref[1:2, :] * d1 + cw_ref[2:3, :] * dconv
        dproj_ref[:, D_MODEL:2 * D_MODEL] = (dv * u).astype(BF16)
        dproj_ref[:, 2 * D_MODEL:3 * D_MODEL] = (dv * cc).astype(BF16)
        dh = jnp.zeros((tm, D_MODEL), F32)
        for d in range(N_DEV):
            dh += lax.dot_general(dproj_ref[:, d * n_col:(d + 1) * n_col], win_ref[d], NT, preferred_element_type=F32)
        xv = x_ref[...]
        r = _rstd(xv)
        xh = xv * r
        h_ref[...] = (xh * an_ref[...]).astype(BF16)
        dan_ref[...] += jnp.sum(dh * xh, axis=0, keepdims=True)
        dx_ref[...] = dx1v + _norm_bwd(dh * an_ref[...], xh, r)

        @pl.when(step == n_seq * nt - 1)
        def _():
            for cp in _scatter_copies(ready_refs, part_refs, sems)[1]:
                cp.wait()

    tok = lambda width: pl.BlockSpec((tm, width), lambda s, i: (s * nt + nt - 1 - i, 0))
    t = n_seq * seq
    outs = pl.pallas_call(
        body, name="conv_mixer_bwd", grid=(n_seq, nt),
        in_specs=[tok(D_MODEL), tok(D_MODEL), tok(4 * D_MODEL), tok(D_MODEL), _const_spec((1, D_MODEL)),
                  _const_spec(w_in.shape), _const_spec((3, D_MODEL)), _const_spec(w_out.shape)] + [ANY_SPEC] * n_ready,
        out_specs=[tok(D_MODEL), tok(D_MODEL), tok(4 * D_MODEL), _acc_spec((D_MODEL, D_MODEL)),
                   _acc_spec((1, D_MODEL)), _acc_spec((8, D_MODEL))] + [ANY_SPEC] * n_ready,
        out_shape=[jax.ShapeDtypeStruct((t, D_MODEL), F32), jax.ShapeDtypeStruct((t, D_MODEL), BF16),
                   jax.ShapeDtypeStruct((t, 4 * D_MODEL), BF16), jax.ShapeDtypeStruct((D_MODEL, D_MODEL), F32),
                   jax.ShapeDtypeStruct((1, D_MODEL), F32), jax.ShapeDtypeStruct((8, D_MODEL), F32)]
        + [jax.ShapeDtypeStruct(a.shape, a.dtype) for a in ready_stacks],
        scratch_shapes=[pltpu.VMEM((8, D_MODEL), F32), pltpu.VMEM((tm, D_MODEL), F32), pltpu.VMEM((tm, D_MODEL), F32)]
        + _comm_sems(n_ready),
        compiler_params=_params(2),
    )(dx1, x, proj, conv, a_norm, w_in, conv_w, w_out, *ready_stacks)
    return outs[:6], outs[6:]


def _w_in_grad_exchange(a, b, n_col, me, ready_stacks, small):
    t, r = a.shape
    bt = 512
    nk = t // bt
    n_ready = len(ready_stacks)
    last_remote = N_DEV - 2

    def body(me_ref, a_ref, b_ref, *rest):
        ready_refs, small_ref, rest = rest[:n_ready], rest[n_ready], rest[n_ready + 1:]
        own_ref, got_ref, rest = rest[0], rest[1], rest[2:]
        part_refs, small_all_ref, rest = rest[:n_ready], rest[n_ready], rest[n_ready + 1:]
        acc_ref, stage_ref, shard_send, shard_recv = rest[:4]
        scatter_sems, gather_sems = rest[4:7], rest[7:10]
        s, k = pl.program_id(0), pl.program_id(1)
        x, y, c, me = _place()

        @pl.when(jnp.logical_and(s == 0, k == 0))
        def _():
            for cp in _scatter_copies(ready_refs, part_refs, scatter_sems)[0]:
                cp.start()
            for cp in _gather_copies([small_ref], [small_all_ref], gather_sems)[0]:
                cp.start()

        @pl.when(k == 0)
        def _():
            acc_ref[...] = jnp.zeros_like(acc_ref)

        acc_ref[...] += lax.dot_general(a_ref[...], b_ref[...], TN, preferred_element_type=F32)

        def shard_copy(step):
            owner = (me + 1 + step) % N_DEV
            slot = jnp.bitwise_xor(me, owner) - 1
            return pltpu.make_async_remote_copy(
                src_ref=stage_ref.at[step % 2], dst_ref=got_ref.at[slot], send_sem=shard_send.at[step],
                recv_sem=shard_recv.at[slot], device_id=(owner // 4, (owner // 2) % 2, owner % 2), device_id_type=MESH)

        for step in range(last_remote + 1):
            @pl.when(jnp.logical_and(s == step, k == nk - 1))
            def _(step=step):
                if step >= 2:
                    shard_copy(step - 2).wait_send()
                stage_ref[step % 2] = acc_ref[...].astype(BF16)
                shard_copy(step).start()

        @pl.when(jnp.logical_and(s == N_DEV - 1, k == nk - 1))
        def _():
            own_ref[...] = acc_ref[...]
            shard_copy(last_remote - 1).wait_send()
            shard_copy(last_remote).wait_send()
            for slot in range(N_DEV - 1):
                pltpu.make_async_remote_copy(
                    src_ref=stage_ref.at[0], dst_ref=got_ref.at[slot], send_sem=shard_send.at[0],
                    recv_sem=shard_recv.at[slot], device_id=(x, y, c), device_id_type=MESH).wait_recv()
            for cp in _scatter_copies(ready_refs, part_refs, scatter_sems)[1]:
                cp.wait()
            for cp in _gather_copies([small_ref], [small_all_ref], gather_sems)[1]:
                cp.wait()

    grid_spec = pltpu.PrefetchScalarGridSpec(
        num_scalar_prefetch=1, grid=(N_DEV, nk),
        in_specs=[pl.BlockSpec((bt, r), lambda s, k, me: (k, 0)),
                  pl.BlockSpec((bt, n_col), lambda s, k, me: (k, (me[0] + 1 + s) % N_DEV))] + [ANY_SPEC] * (n_ready + 1),
        out_specs=[pl.BlockSpec((r, n_col), lambda s, k, me: (0, 0)), ANY_SPEC] + [ANY_SPEC] * (n_ready + 1),
        scratch_shapes=[pltpu.VMEM((r, n_col), F32), pltpu.VMEM((2, r, n_col), BF16),
                        pltpu.SemaphoreType.DMA((N_DEV - 1,)), pltpu.SemaphoreType.DMA((N_DEV - 1,))]
        + _comm_sems(n_ready) + _comm_sems(1))
    outs = pl.pallas_call(
        body, name="w_in_grad_exchange", grid_spec=grid_spec,
        out_shape=[jax.ShapeDtypeStruct((r, n_col), F32), jax.ShapeDtypeStruct((N_DEV - 1, r, n_col), BF16)]
        + [jax.ShapeDtypeStruct(p.shape, p.dtype) for p in ready_stacks] + _stacked([small]),
        compiler_params=_params(2),
    )(me, a, b, *ready_stacks, small)
    return outs[0], outs[1], outs[2:2 + n_ready], outs[2 + n_ready]


def _proj_fwd(x1, pos, rc, kv_norm, w_dkv, ckv_norm, w_ukv, b_norm, b_w_in, q_norm, w_uq):
    t = x1.shape[0]
    tm = ROW_TILE

    def body(x1_ref, pos_ref, rc_ref, kvn_ref, wdkv_ref, ckvn_ref, wukv_ref, bn_ref, bwin_ref, qn_ref, wuq_ref,
             ckr_ref, cq_ref, gb_ref, q_ref, k_ref, v_ref):
        ct, s1, s2 = _rope_tables(pos_ref[...], rc_ref[...])
        xv = x1_ref[...]
        xh = xv * _rstd(xv)
        ckr = _dot(xh * kvn_ref[...], wdkv_ref[...])
        ckr_ref[...] = ckr
        ckv = ckr[:, 0:KV_RANK]
        kv = _dot(ckv * _rstd(ckv) * ckvn_ref[...], wukv_ref[...])
        k_rope = _rope(ckr[:, KV_RANK:CKR_PAD], ct, s1, s2)
        lane = lax.broadcasted_iota(jnp.int32, (tm, HEAD_PAD), 1)
        low = lane < QK_NOPE
        for h in range(N_HEADS):
            kv_h = kv[:, h * HEAD_PAD:(h + 1) * HEAD_PAD]
            k_ref[:, h * HEAD_PAD:(h + 1) * HEAD_PAD] = jnp.where(low, kv_h, k_rope).astype(BF16)
            v_ref[:, h * HEAD_PAD:(h + 1) * HEAD_PAD] = jnp.where(low, pltpu.roll(kv_h, V_HEAD, 1), 1.0).astype(BF16)
        pb = _dot(xh * bn_ref[...], bwin_ref[...])
        cq = pb[:, 0:Q_RANK]
        cq_ref[...] = cq
        gb_ref[...] = pb[:, Q_RANK:Q_RANK + E_B]
        q = _dot(cq * _rstd(cq) * qn_ref[...], wuq_ref[...])
        for h in range(N_HEADS):
            q_ref[:, h * HEAD_PAD:(h + 1) * HEAD_PAD] = _rope(
                q[:, h * HEAD_PAD:(h + 1) * HEAD_PAD], ct, s1, s2).astype(BF16)

    tok = lambda width: pl.BlockSpec((tm, width), lambda i: (i, 0))
    weights = [kv_norm, w_dkv, ckv_norm, w_ukv, b_norm, b_w_in, q_norm, w_uq]
    wide = N_HEADS * HEAD_PAD
    return pl.pallas_call(
        body, name="latent_proj_fwd", grid=(t // tm,),
        in_specs=[tok(D_MODEL), tok(1), _const_spec(rc.shape)] + [_const_spec(w.shape) for w in weights],
        out_specs=[tok(CKR_PAD), tok(Q_RANK), tok(E_B), tok(wide), tok(wide), tok(wide)],
        out_shape=[jax.ShapeDtypeStruct((t, CKR_PAD), F32), jax.ShapeDtypeStruct((t, Q_RANK), F32),
                   jax.ShapeDtypeStruct((t, E_B), F32), jax.ShapeDtypeStruct((t, wide), BF16),
                   jax.ShapeDtypeStruct((t, wide), BF16), jax.ShapeDtypeStruct((t, wide), BF16)],
        compiler_params=_params(1),
    )(x1, pos, rc, *weights)


def _proj_bwd(dq, dk, dv, dgb, cq, ckr, x1, dx2, pos, rc, kv_norm, w_dkv, ckv_norm, w_ukv, b_norm, b_w_in, q_norm, w_uq):
    t = x1.shape[0]
    tm = ROW_TILE
    wide = N_HEADS * HEAD_PAD

    def body(dq_ref, dk_ref, dv_ref, dgb_ref, cq_ref, ckr_ref, x1_ref, dx2_ref, pos_ref, rc_ref,
             kvn_ref, wdkv_ref, ckvn_ref, wukv_ref, bn_ref, bwin_ref, qn_ref, wuq_ref,
             dx1_ref, dwuq_ref, dwbin_ref, dwukv_ref, dwdkv_ref, dqn_ref, dbn_ref, dckvn_ref, dkvn_ref, dqu_ref, dkv_ref):
        @pl.when(pl.program_id(0) == 0)
        def _():
            for ref in (dwuq_ref, dwbin_ref, dwukv_ref, dwdkv_ref, dqn_ref, dbn_ref, dckvn_ref, dkvn_ref):
                ref[...] = jnp.zeros_like(ref)

        ct, s1, s2 = _rope_tables(pos_ref[...], rc_ref[...])
        lane = lax.broadcasted_iota(jnp.int32, (tm, HEAD_PAD), 1)
        low = lane < QK_NOPE
        for h in range(N_HEADS):
            dqu_ref[:, h * HEAD_PAD:(h + 1) * HEAD_PAD] = _rope(
                dq_ref[:, h * HEAD_PAD:(h + 1) * HEAD_PAD], ct, -s1, -s2).astype(BF16)
        cq = cq_ref[...]
        rq = _rstd(cq)
        cqh = cq * rq
        cqn = (cqh * qn_ref[...]).astype(BF16)
        for h in range(N_HEADS):
            dwuq_ref[h] += lax.dot_general(cqn, dqu_ref[:, h * HEAD_PAD:(h + 1) * HEAD_PAD], TN, preferred_element_type=F32)
        dcqn = lax.dot_general(dqu_ref[...], wuq_ref[...], NT, preferred_element_type=F32)
        dqn_ref[...] += jnp.sum(dcqn * cqh, axis=0, keepdims=True)
        dcq = _norm_bwd(dcqn * qn_ref[...], cqh, rq)
        dpb = jnp.concatenate([dcq, dgb_ref[...]], axis=1).astype(BF16)
        xv = x1_ref[...]
        r = _rstd(xv)
        xh = xv * r
        dwbin_ref[...] += _dot_tn(xh * bn_ref[...], dpb)
        dh3 = lax.dot_general(dpb, bwin_ref[...], NT, preferred_element_type=F32)
        dk_rope = jnp.zeros((tm, HEAD_PAD), F32)
        for h in range(N_HEADS):
            dk_h = dk_ref[:, h * HEAD_PAD:(h + 1) * HEAD_PAD]
            dv_h = dv_ref[:, h * HEAD_PAD:(h + 1) * HEAD_PAD]
            dkv_ref[:, h * HEAD_PAD:(h + 1) * HEAD_PAD] = jnp.where(low, dk_h, pltpu.roll(dv_h, V_HEAD, 1)).astype(BF16)
            dk_rope += dk_h
        rope_lanes = jnp.logical_and(lane >= ROPE_LO, lane < ROPE_LO + QK_ROPE)
        dk_rope = jnp.where(rope_lanes, _rope(dk_rope, ct, -s1, -s2), 0.0)
        ckv = ckr_ref[:, 0:KV_RANK]
        rk = _rstd(ckv)
        ckh = ckv * rk
        ckvn = (ckh * ckvn_ref[...]).astype(BF16)
        for h in range(N_HEADS):
            dwukv_ref[h] += lax.dot_general(ckvn, dkv_ref[:, h * HEAD_PAD:(h + 1) * HEAD_PAD], TN, preferred_element_type=F32)
        dckvn = lax.dot_general(dkv_ref[...], wukv_ref[...], NT, preferred_element_type=F32)
        dckvn_ref[...] += jnp.sum(dckvn * ckh, axis=0, keepdims=True)
        dckv = _norm_bwd(dckvn * ckvn_ref[...], ckh, rk)
        dckr = jnp.concatenate([dckv, dk_rope], axis=1).astype(BF16)
        dwdkv_ref[...] += _dot_tn(xh * kvn_ref[...], dckr)
        dh2 = lax.dot_general(dckr, wdkv_ref[...], NT, preferred_element_type=F32)
        dkvn_ref[...] += jnp.sum(dh2 * xh, axis=0, keepdims=True)
        dbn_ref[...] += jnp.sum(dh3 * xh, axis=0, keepdims=True)
        dx1_ref[...] = dx2_ref[...] + _norm_bwd(dh2 * kvn_ref[...] + dh3 * bn_ref[...], xh, r)

    tok = lambda width: pl.BlockSpec((tm, width), lambda i: (i, 0))
    weights = [kv_norm, w_dkv, ckv_norm, w_ukv, b_norm, b_w_in, q_norm, w_uq]
    acc_shapes = [(N_HEADS, Q_RANK, HEAD_PAD), (D_MODEL, Q_RANK + E_B), (N_HEADS, KV_RANK, HEAD_PAD), (D_MODEL, CKR_PAD),
                  (1, Q_RANK), (1, D_MODEL), (1, KV_RANK), (1, D_MODEL)]
    return pl.pallas_call(
        body, name="latent_proj_bwd", grid=(t // tm,),
        in_specs=[tok(wide), tok(wide), tok(wide), tok(E_B), tok(Q_RANK), tok(CKR_PAD), tok(D_MODEL), tok(D_MODEL), tok(1),
                  _const_spec(rc.shape)] + [_const_spec(w.shape) for w in weights],
        out_specs=[tok(D_MODEL)] + [_acc_spec(s) for s in acc_shapes],
        out_shape=[jax.ShapeDtypeStruct((t, D_MODEL), F32)] + [jax.ShapeDtypeStruct(s, F32) for s in acc_shapes],
        scratch_shapes=[pltpu.VMEM((tm, wide), BF16), pltpu.VMEM((tm, wide), BF16)],
        compiler_params=_params(1),
    )(dq, dk, dv, dgb, cq, ckr, x1, dx2, pos, rc, *weights)


def _attn_fwd(q, k, v, n_seq, seq):
    tq, tk = ATTN_FWD_TILES
    ratio = tq // tk
    nq = seq // tq
    pair = 2 * HEAD_PAD

    def body(q_ref, k_ref, v_ref, o_ref, lr_ref):
        i = pl.program_id(2)
        row = lax.broadcasted_iota(jnp.int32, (tq, tk), 0)
        col = lax.broadcasted_iota(jnp.int32, (tq, tk), 1)
        qs = [q_ref[:, hh * HEAD_PAD:(hh + 1) * HEAD_PAD] for hh in range(2)]

        def step(j, carry, shift):
            start = pl.multiple_of(j * tk, tk)
            out = []
            for hh in range(2):
                m, acc = carry[hh]
                s = lax.dot_general(qs[hh], k_ref[pl.ds(start, tk), hh * HEAD_PAD:(hh + 1) * HEAD_PAD], NT,
                                    preferred_element_type=F32) * (SOFTMAX_SCALE * LOG2_E)
                if shift is not None:
                    s = jnp.where(col + shift <= row, s, -jnp.inf)
                m_new = jnp.maximum(m, jnp.max(s, axis=-1, keepdims=True))
                p = jnp.exp2(s - m_new)
                acc = jnp.exp2(m - m_new) * acc + jnp.dot(
                    p.astype(BF16), v_ref[pl.ds(start, tk), hh * HEAD_PAD:(hh + 1) * HEAD_PAD], preferred_element_type=F32)
                out.append((m_new, acc))
            return tuple(out)

        one = (jnp.full((tq, 1), -jnp.inf, F32), jnp.zeros((tq, HEAD_PAD), F32))
        carry = lax.fori_loop(0, i * ratio, functools.partial(step, shift=None), (one, one))
        for d in range(ratio):
            carry = step(i * ratio + d, carry, d * tk)
        lane = lax.broadcasted_iota(jnp.int32, (tq, HEAD_PAD), 1)
        low = lane < V_HEAD
        halves = []
        for hh in range(2):
            m, acc = carry[hh]
            swapped = pltpu.roll(acc, V_HEAD, 1)
            halves.append(acc / swapped)
            lse = (m * LN_2) + jnp.log(jnp.where(low, swapped, acc))
            lr_ref[hh] = lse.T[0:8, :]
        o_ref[...] = jnp.where(low, halves[0], pltpu.roll(halves[1], V_HEAD, 1))

    t = n_seq * seq
    return pl.pallas_call(
        body, name="attention_fwd", grid=(n_seq, N_HEADS // 2, nq),
        in_specs=[pl.BlockSpec((tq, pair), lambda s, p, i: (s * nq + i, p)),
                  pl.BlockSpec((seq, pair), lambda s, p, i: (s, p)),
                  pl.BlockSpec((seq, pair), lambda s, p, i: (s, p))],
        out_specs=[pl.BlockSpec((tq, HEAD_PAD), lambda s, p, i: (s * nq + i, p)),
                   pl.BlockSpec((None, 2, 8, tq), lambda s, p, i: (s, p, 0, i))],
        out_shape=[jax.ShapeDtypeStruct((t, E_B), F32), jax.ShapeDtypeStruct((n_seq, N_HEADS, 8, seq), F32)],
        compiler_params=_params(3),
    )(q, k, v)


def _attn_bwd(q, k, v, do, lse_row, delta_row, n_seq, seq):
    tk, tq = ATTN_BWD_TILES
    ratio = tk // tq
    nk = seq // tk
    n_inner = seq // tq
    pair = 2 * HEAD_PAD

    def body(q_ref, k_ref, v_ref, do_ref, lr_ref, dr_ref, dk_ref, dv_ref, dq_ref):
        j = pl.program_id(2)

        @pl.when(j == 0)
        def _():
            dq_ref[...] = jnp.zeros_like(dq_ref)

        row = lax.broadcasted_iota(jnp.int32, (tk, tq), 0)
        col = lax.broadcasted_iota(jnp.int32, (tk, tq), 1)
        ks = [k_ref[:, hh * HEAD_PAD:(hh + 1) * HEAD_PAD] for hh in range(2)]
        vs = [v_ref[:, hh * HEAD_PAD:(hh + 1) * HEAD_PAD] for hh in range(2)]

        def step(i, carry, shift):
            start = pl.multiple_of(i * tq, tq)
            out = []
            for hh in range(2):
                dk_acc, dv_acc = carry[hh]
                qi = q_ref[pl.ds(start, tq), hh * HEAD_PAD:(hh + 1) * HEAD_PAD]
                doi = do_ref[pl.ds(start, tq), hh * HEAD_PAD:(hh + 1) * HEAD_PAD]
                st = lax.dot_general(ks[hh], qi, NT, preferred_element_type=F32) * SOFTMAX_SCALE
                pt = jnp.exp(st - lr_ref[hh, 0:1, pl.ds(start, tq)])
                if shift is not None:
                    pt = jnp.where(col + shift >= row, pt, 0.0)
                dv_acc = dv_acc + jnp.dot(pt.astype(BF16), doi, preferred_element_type=F32)
                dpt = lax.dot_general(vs[hh], doi, NT, preferred_element_type=F32)
                dst = (pt * (dpt - dr_ref[hh, 0:1, pl.ds(start, tq)]) * SOFTMAX_SCALE).astype(BF16)
                dk_acc = dk_acc + jnp.dot(dst, qi, preferred_element_type=F32)
                dq_ref[pl.ds(start, tq), hh * HEAD_PAD:(hh + 1) * HEAD_PAD] += lax.dot_general(
                    dst, ks[hh], TN, preferred_element_type=F32)
                out.append((dk_acc, dv_acc))
            return tuple(out)

        one = (jnp.zeros((tk, HEAD_PAD), F32), jnp.zeros((tk, HEAD_PAD), F32))
        carry = (one, one)
        for d in range(ratio):
            carry = step(j * ratio + d, carry, d * tq)
        carry = lax.fori_loop((j + 1) * ratio, n_inner, functools.partial(step, shift=None), carry)
        for hh in range(2):
            dk_ref[:, hh * HEAD_PAD:(hh + 1) * HEAD_PAD] = carry[hh][0]
            dv_ref[:, hh * HEAD_PAD:(hh + 1) * HEAD_PAD] = carry[hh][1]

    t = n_seq * seq
    wide = N_HEADS * HEAD_PAD
    return pl.pallas_call(
        body, name="attention_bwd", grid=(n_seq, N_HEADS // 2, nk),
        in_specs=[pl.BlockSpec((seq, pair), lambda s, p, j: (s, p)),
                  pl.BlockSpec((tk, pair), lambda s, p, j: (s * nk + j, p)),
                  pl.BlockSpec((tk, pair), lambda s, p, j: (s * nk + j, p)),
                  pl.BlockSpec((seq, pair), lambda s, p, j: (s, p)),
                  pl.BlockSpec((None, 2, 8, seq), lambda s, p, j: (s, p, 0, 0)),
                  pl.BlockSpec((None, 2, 8, seq), lambda s, p, j: (s, p, 0, 0))],
        out_specs=[pl.BlockSpec((tk, pair), lambda s, p, j: (s * nk + j, p)),
                   pl.BlockSpec((tk, pair), lambda s, p, j: (s * nk + j, p)),
                   pl.BlockSpec((seq, pair), lambda s, p, j: (s, p))],
        out_shape=[jax.ShapeDtypeStruct((t, wide), F32), jax.ShapeDtypeStruct((t, wide), F32),
                   jax.ShapeDtypeStruct((t, wide), F32)],
        compiler_params=_params(3),
    )(q, k, v, do, lse_row, delta_row)


def _head_tail(o, gb, x1, w_out, final_norm, target, n_seq, seq):
    tm = ROW_TILE
    nt = seq // tm
    n_col = D_MODEL // N_DEV

    def body(o_ref, gb_ref, x1_ref, wout_ref, fn_ref, tgt_ref,
             dx2_ref, do_ref, dgb_ref, dr_ref, dwout_ref, dfn_ref, loss_ref):
        first = jnp.logical_and(pl.program_id(0) == 0, pl.program_id(1) == 0)

        @pl.when(first)
        def _():
            dwout_ref[...] = jnp.zeros_like(dwout_ref)
            dfn_ref[...] = jnp.zeros_like(dfn_ref)
            loss_ref[...] = jnp.zeros_like(loss_ref)

        ov, g = o_ref[...], gb_ref[...]
        silu, dsilu = _silu_parts(g)
        gated = (ov * silu).astype(BF16)
        x2 = x1_ref[...] + jnp.dot(gated, wout_ref[...], preferred_element_type=F32)
        r = _rstd(x2)
        xh = x2 * r
        err = xh * fn_ref[...] - tgt_ref[...]
        loss_ref[...] += 0.5 * jnp.sum(jnp.mean(err * err, axis=-1, keepdims=True), axis=0, keepdims=True)
        dy = err / D_MODEL
        dfn_ref[...] += jnp.sum(dy * xh, axis=0, keepdims=True)
        dx2 = _norm_bwd(dy * fn_ref[...], xh, r)
        dx2_ref[...] = dx2
        dx2b = dx2.astype(BF16)
        dw = lax.dot_general(gated, dx2b, TN, preferred_element_type=F32)
        for d in range(N_DEV):
            dwout_ref[d] += dw[:, d * n_col:(d + 1) * n_col]
        dgated = lax.dot_general(dx2b, wout_ref[...], NT, preferred_element_type=F32)
        do = dgated * silu
        dgb_ref[...] = dgated * ov * dsilu
        prod = do * ov
        lane = lax.broadcasted_iota(jnp.int32, (tm, HEAD_PAD), 1)
        low = lane < V_HEAD
        for p in range(N_HEADS // 2):
            blk = prod[:, p * HEAD_PAD:(p + 1) * HEAD_PAD]
            do_pair = do[:, p * HEAD_PAD:(p + 1) * HEAD_PAD]
            for hh in range(2):
                h = 2 * p + hh
                mine = do_pair if hh == 0 else pltpu.roll(do_pair, V_HEAD, 1)
                do_ref[:, h * HEAD_PAD:(h + 1) * HEAD_PAD] = jnp.where(low, mine, 0.0).astype(BF16)
                delta = jnp.sum(jnp.where(low if hh == 0 else ~low, blk, 0.0), axis=-1, keepdims=True)
                dr_ref[h] = jnp.broadcast_to(delta, (tm, HEAD_PAD)).T[0:8, :]

    tok = lambda width: pl.BlockSpec((tm, width), lambda s, i: (s * nt + i, 0))
    t = n_seq * seq
    return pl.pallas_call(
        body, name="head_tail", grid=(n_seq, nt),
        in_specs=[tok(E_B), tok(E_B), tok(D_MODEL), _const_spec(w_out.shape), _const_spec((1, D_MODEL)), tok(D_MODEL)],
        out_specs=[tok(D_MODEL), tok(N_HEADS * HEAD_PAD), tok(E_B),
                   pl.BlockSpec((None, N_HEADS, 8, tm), lambda s, i: (s, 0, 0, i)),
                   _acc_spec((N_DEV, E_B, n_col)), _acc_spec((1, D_MODEL)), _acc_spec((1, 1))],
        out_shape=[jax.ShapeDtypeStruct((t, D_MODEL), F32), jax.ShapeDtypeStruct((t, N_HEADS * HEAD_PAD), BF16),
                   jax.ShapeDtypeStruct((t, E_B), F32),
                   jax.ShapeDtypeStruct((n_seq, N_HEADS, 8, seq), F32), jax.ShapeDtypeStruct((N_DEV, E_B, n_col), F32),
                   jax.ShapeDtypeStruct((1, D_MODEL), F32), jax.ShapeDtypeStruct((1, 1), F32)],
        compiler_params=_params(2),
    )(o, gb, x1, w_out, final_norm, target)


def _adamw_math(w, g, m, v):
    m = ADAM_B1 * m + (1.0 - ADAM_B1) * g
    v = ADAM_B2 * v + (1.0 - ADAM_B2) * jnp.square(g)
    m_hat = m / (1.0 - ADAM_B1 ** ADAM_STEP)
    v_hat = v / (1.0 - ADAM_B2 ** ADAM_STEP)
    delta = -ADAM_LR * (m_hat / (jnp.sqrt(v_hat) + ADAM_EPS) + ADAM_WD * w)
    return delta, m, v


def _adamw_reduce(own, parts, w, m, v):
    rows, cols = w.shape
    br = 256 if rows % 256 == 0 else 128
    n_parts = parts.shape[0]

    def body(*refs):
        p_ref, w_ref, m_ref, v_ref, g_ref, d_ref, nm_ref, nv_ref = refs[-8:]
        g = p_ref[0].astype(F32) if own is None else refs[0][...] + p_ref[0].astype(F32)
        for k in range(1, n_parts):
            g = g + p_ref[k].astype(F32)
        g_ref[...] = g
        d_ref[...], nm_ref[...], nv_ref[...] = _adamw_math(w_ref[...], g, m_ref[...], v_ref[...])

    blk = pl.BlockSpec((br, cols), lambda i: (i, 0))
    first = [] if own is None else [own]
    return pl.pallas_call(
        body, name="adamw_reduce", grid=(rows // br,),
        in_specs=[blk] * len(first) + [pl.BlockSpec((n_parts, br, cols), lambda i: (0, i, 0)), blk, blk, blk],
        out_specs=[blk] * 4, out_shape=[jax.ShapeDtypeStruct((rows, cols), F32)] * 4,
        compiler_params=_params(1),
    )(*first, parts, w, m, v)


def _sum_parts(parts):
    def body(p_ref, o_ref):
        g = p_ref[0]
        for k in range(1, N_DEV):
            g = g + p_ref[k]
        o_ref[...] = g

    return pl.pallas_call(body, name="small_grad_sum", out_shape=jax.ShapeDtypeStruct(parts.shape[1:], F32))(parts)


def _adamw_small(gs, ws, ms, vs):
    n = len(gs)

    def body(*refs):
        ins, outs = refs[:4 * n], refs[4 * n:]
        for j in range(n):
            g_ref, w_ref, m_ref, v_ref = ins[j], ins[n + j], ins[2 * n + j], ins[3 * n + j]
            outs[3 * j][...], outs[3 * j + 1][...], outs[3 * j + 2][...] = _adamw_math(
                w_ref[...], g_ref[...], m_ref[...], v_ref[...])

    shapes = [jax.ShapeDtypeStruct(g.shape, F32) for g in gs for _ in range(3)]
    flat = pl.pallas_call(body, name="adamw_small", out_shape=shapes)(*gs, *ws, *ms, *vs)
    return [tuple(flat[3 * j:3 * j + 3]) for j in range(n)]


SMALL_ROWS = 8
LOSS_LANE = D_MODEL - 1


def _pack_small(g_fn, g_kvn, g_bn, g_an, g_ckvn, g_qn, g_conv, loss_part):
    def body(fn_ref, kvn_ref, bn_ref, an_ref, ckvn_ref, qn_ref, conv_ref, loss_ref, o_ref):
        o_ref[0:1, :] = fn_ref[...]
        o_ref[1:2, :] = kvn_ref[...]
        o_ref[2:3, :] = bn_ref[...]
        o_ref[3:4, :] = an_ref[...]
        lane = lax.broadcasted_iota(jnp.int32, (1, D_MODEL), 1)
        o_ref[4:5, :] = jnp.where(lane == LOSS_LANE, loss_ref[...], 0.0)
        o_ref[4:5, 0:KV_RANK] = ckvn_ref[...]
        o_ref[4:5, KV_RANK:KV_RANK + Q_RANK] = qn_ref[...]
        o_ref[5:8, :] = conv_ref[0:3, :]

    return pl.pallas_call(body, name="pack_small", out_shape=jax.ShapeDtypeStruct((SMALL_ROWS, D_MODEL), F32))(
        g_fn, g_kvn, g_bn, g_an, g_ckvn, g_qn, g_conv, loss_part)


def _pad_cols(a, width):
    return jnp.pad(a, ((0, 0), (0, width - a.shape[1])))


def _dkv_to_padded(a):
    r = a.shape[0]
    z = jnp.zeros((r, ROPE_LO), a.dtype)
    z2 = jnp.zeros((r, HEAD_PAD - ROPE_LO - QK_ROPE), a.dtype)
    return jnp.concatenate([a[:, :KV_RANK], z, a[:, KV_RANK:], z2], axis=1)


def _dkv_from_padded(a):
    return jnp.concatenate([a[:, :KV_RANK], a[:, KV_RANK + ROPE_LO:KV_RANK + ROPE_LO + QK_ROPE]], axis=1)


def _unstack_cols(a):
    return jnp.transpose(a, (1, 0, 2)).reshape(a.shape[1], N_DEV * a.shape[2])


def kernel(x, positions, a_norm, a_w_in, a_conv, a_w_out, kv_norm, w_dkv, ckv_norm, w_ukv, b_norm, b_w_in, b_q_norm, b_w_uq, b_w_out, final_norm, loss_target, m_a_norm, m_a_w_in, m_a_conv, m_a_w_out, m_kv_norm, m_w_dkv, m_ckv_norm, m_w_ukv, m_b_norm, m_b_w_in, m_b_q_norm, m_b_w_uq, m_b_w_out, m_final_norm, v_a_norm, v_a_w_in, v_a_conv, v_a_w_out, v_kv_norm, v_w_dkv, v_ckv_norm, v_w_ukv, v_b_norm, v_b_w_in, v_b_q_norm, v_b_w_uq, v_b_w_out, v_final_norm):
    n_seq, seq, _ = x.shape
    t = n_seq * seq
    me = 4 * lax.axis_index("x") + 2 * lax.axis_index("y") + lax.axis_index("c")

    big = {
        "a_w_in": (a_w_in[0], m_a_w_in[0], v_a_w_in[0]),
        "a_w_out": (a_w_out[0], m_a_w_out[0], v_a_w_out[0]),
        "w_dkv": tuple(_dkv_to_padded(a) for a in (w_dkv, m_w_dkv, v_w_dkv)),
        "w_ukv": (w_ukv, m_w_ukv, v_w_ukv),
        "b_w_in": (b_w_in[0], m_b_w_in[0], v_b_w_in[0]),
        "b_w_uq": tuple(_pad_cols(a[0], HEAD_PAD) for a in (b_w_uq, m_b_w_uq, v_b_w_uq)),
        "b_w_out": (b_w_out[0], m_b_w_out[0], v_b_w_out[0]),
    }
    names = list(big)
    first_names, later_names = names[:2], names[2:]
    gathered = _all_gather([big[n][0].astype(BF16) for n in first_names] + [a_norm, a_conv[0]])
    a_norm_f = gathered[2].reshape(1, D_MODEL)
    a_conv_f = _unstack_cols(gathered[3])
    w_a_in = gathered[0]
    w_a_out = gathered[1].reshape(D_MODEL, D_MODEL)

    x2d = x.reshape(t, D_MODEL)
    tgt = loss_target.reshape(t, D_MODEL)
    pos = positions.astype(F32).reshape(t, 1)
    rc = _rope_consts()
    kvn, ckvn, bn, qn, fn = kv_norm.reshape(1, -1), ckv_norm.reshape(1, -1), b_norm, b_q_norm, final_norm.reshape(1, -1)

    x1, proj, conv, later = _conv_fwd(x2d, a_norm_f, w_a_in, a_conv_f, w_a_out, n_seq, seq,
                                      [big[n][0].astype(BF16) for n in later_names])
    full = dict(zip(later_names, later))
    w_dkv_f = full["w_dkv"].reshape(D_MODEL, CKR_PAD)
    w_ukv_f = _unstack_cols(full["w_ukv"])
    w_b_in = full["b_w_in"].reshape(D_MODEL, Q_RANK + E_B)
    w_uq_f = _unstack_cols(full["b_w_uq"])
    w_b_out = _unstack_cols(full["b_w_out"])
    layer_b = (kvn, w_dkv_f, ckvn, w_ukv_f, bn, w_b_in, qn, w_uq_f)
    ckr, cq, gb, q, k, v = _proj_fwd(x1, pos, rc, *layer_b)
    o, lse_row = _attn_fwd(q, k, v, n_seq, seq)
    dx2, do, dgb, delta_row, g_b_out, g_fn, loss_part = _head_tail(o, gb, x1, w_b_out, fn, tgt, n_seq, seq)
    dk, dv, dq = _attn_bwd(q, k, v, do, lse_row, delta_row, n_seq, seq)
    dx1, g_uq, g_b_in, g_ukv, g_dkv, g_qn, g_bn, g_ckvn, g_kvn = _proj_bwd(
        dq, dk, dv, dgb, cq, ckr, x1, dx2, pos, rc, *layer_b)
    stacks = {
        "w_dkv": g_dkv.reshape(N_DEV, D_MODEL // N_DEV, CKR_PAD),
        "w_ukv": g_ukv,
        "b_w_in": g_b_in.reshape(N_DEV, D_MODEL // N_DEV, Q_RANK + E_B),
        "b_w_uq": g_uq,
        "b_w_out": g_b_out,
    }
    (dx, h_a, dproj, g_a_out, g_an, g_conv), later_parts = _conv_bwd(
        dx1, x2d, proj, conv, a_norm_f, w_a_in, a_conv_f, w_a_out, n_seq, seq, [stacks[n] for n in later_names])
    parts = dict(zip(later_names, later_parts))
    small = _pack_small(g_fn, g_kvn, g_bn, g_an, g_ckvn, g_qn, g_conv, loss_part)
    a_in_own, a_in_got, (parts["a_w_out"],), small_parts = _w_in_grad_exchange(
        h_a, dproj, D_MODEL * 4 // N_DEV, me.reshape(1).astype(jnp.int32),
        [g_a_out.reshape(N_DEV, D_MODEL // N_DEV, D_MODEL)], small)

    outs = {"a_w_in": _adamw_reduce(a_in_own, a_in_got, *big["a_w_in"])}
    for n in names[1:]:
        outs[n] = _adamw_reduce(None, parts[n], *big[n])
    outs["w_dkv"] = tuple(_dkv_from_padded(a) for a in outs["w_dkv"])
    outs["b_w_uq"] = tuple(a[:, :QK_NOPE + QK_ROPE] for a in outs["b_w_uq"])
    for n in ("a_w_in", "a_w_out", "b_w_in", "b_w_uq", "b_w_out"):
        outs[n] = tuple(a[None] for a in outs[n])

    total = _sum_parts(small_parts)
    loss = total[4, LOSS_LANE]
    shard = D_MODEL // N_DEV
    g_small = {
        "final_norm": total[0], "kv_norm": total[1], "b_norm": total[2:3],
        "a_norm": lax.dynamic_slice_in_dim(total[3:4], me * shard, shard, axis=1),
        "ckv_norm": total[4, 0:KV_RANK], "b_q_norm": total[4:5, KV_RANK:KV_RANK + Q_RANK],
        "a_conv": lax.dynamic_slice_in_dim(total[5:8], me * shard, shard, axis=1)[None],
    }
    small_state = {
        "final_norm": (final_norm, m_final_norm, v_final_norm), "kv_norm": (kv_norm, m_kv_norm, v_kv_norm),
        "b_norm": (b_norm, m_b_norm, v_b_norm), "a_norm": (a_norm, m_a_norm, v_a_norm),
        "ckv_norm": (ckv_norm, m_ckv_norm, v_ckv_norm), "b_q_norm": (b_q_norm, m_b_q_norm, v_b_q_norm),
        "a_conv": (a_conv, m_a_conv, v_a_conv),
    }
    small_names = list(g_small)
    as2d = lambda a: a.reshape(-1, a.shape[-1])
    upd = _adamw_small([as2d(g_small[n]) for n in small_names],
                       *[[as2d(small_state[n][j]) for n in small_names] for j in range(3)])
    for n, u in zip(small_names, upd):
        outs[n] = (g_small[n],) + tuple(a.reshape(g_small[n].shape) for a in u)

    order = ["a_norm", "a_w_in", "a_conv", "a_w_out", "kv_norm", "w_dkv", "ckv_norm", "w_ukv", "b_norm", "b_w_in",
             "b_q_norm", "b_w_uq", "b_w_out", "final_norm"]
    result = [loss, dx.reshape(n_seq, seq, D_MODEL)]
    for j in range(4):
        result += [outs[n][j] for n in order]
    return tuple(result)
```

```python
import functools
import math

import numpy as np
import jax
import jax.numpy as jnp
from jax import lax
from jax.experimental import pallas as pl
from jax.experimental.pallas import tpu as pltpu

F32 = jnp.float32
BF16 = jnp.bfloat16

D_MODEL = 1024
N_HEADS = 8
QK_NOPE = 64
QK_ROPE = 32
V_HEAD = 64
KV_RANK = 256
Q_RANK = 384
E_B = N_HEADS * V_HEAD
HEAD_PAD = 128
CKR_PAD = KV_RANK + HEAD_PAD
ROPE_LO = QK_NOPE
ROPE_HALF = QK_ROPE // 2
ROPE_THETA = 10000.0
SOFTMAX_SCALE = 1.0 / math.sqrt(QK_NOPE + QK_ROPE)
LOG2_E = math.log2(math.e)
LN_2 = math.log(2.0)
EPS = 1e-6
N_DEV = 8

ADAM_LR = 0.001
ADAM_B1 = 0.9
ADAM_B2 = 0.999
ADAM_EPS = 1e-08
ADAM_WD = 0.01
ADAM_STEP = 10

ROW_TILE = 256
ATTN_FWD_TILES = (512, 512)
ATTN_BWD_TILES = (512, 512)
VMEM_LIMIT = 56 * 1024 * 1024

MESH = pl.DeviceIdType.MESH
NT = (((1,), (1,)), ((), ()))
TN = (((0,), (0,)), ((), ()))


def _dot(a, b):
    return jnp.dot(a.astype(BF16), b.astype(BF16), preferred_element_type=F32)


def _dot_nt(a, b):
    return lax.dot_general(a.astype(BF16), b.astype(BF16), NT, preferred_element_type=F32)


def _dot_tn(a, b):
    return lax.dot_general(a.astype(BF16), b.astype(BF16), TN, preferred_element_type=F32)


def _rstd(x):
    return lax.rsqrt(jnp.mean(x * x, axis=-1, keepdims=True) + EPS)


def _norm_bwd(a, xh, r):
    return r * (a - xh * jnp.mean(a * xh, axis=-1, keepdims=True))


def _silu_parts(g):
    sg = jax.nn.sigmoid(g)
    return g * sg, sg * (1.0 + g * (1.0 - sg))


def _rope_consts():
    inv = (ROPE_THETA ** (-np.arange(0, QK_ROPE, 2, dtype=np.float32) / QK_ROPE)).astype(np.float32)
    t = np.zeros((8, HEAD_PAD), np.float32)
    t[0, ROPE_LO:ROPE_LO + ROPE_HALF] = inv
    t[0, ROPE_LO + ROPE_HALF:ROPE_LO + QK_ROPE] = inv
    t[1, ROPE_LO:ROPE_LO + ROPE_HALF] = -1.0
    t[2, ROPE_LO + ROPE_HALF:ROPE_LO + QK_ROPE] = 1.0
    return jnp.asarray(t)


def _rope_tables(pos, rc):
    ang = pos * rc[0:1, :]
    cosv = jnp.cos(ang)
    sinv = jnp.sin(ang)
    return cosv, sinv * rc[1:2, :], sinv * rc[2:3, :]


def _rope(x, ct, s1, s2):
    up = pltpu.roll(x, HEAD_PAD - ROPE_HALF, 1)
    dn = pltpu.roll(x, ROPE_HALF, 1)
    return x * ct + up * s1 + dn * s2


def _const_spec(shape):
    nd = len(shape)
    return pl.BlockSpec(shape, lambda *_: (0,) * nd, pipeline_mode=pl.Buffered(1))


def _acc_spec(shape):
    nd = len(shape)
    return pl.BlockSpec(shape, lambda *_: (0,) * nd)


def _params(n_axes):
    return pltpu.CompilerParams(dimension_semantics=("arbitrary",) * n_axes, vmem_limit_bytes=VMEM_LIMIT)


def _place():
    x, y, c = lax.axis_index("x"), lax.axis_index("y"), lax.axis_index("c")
    return x, y, c, 4 * x + 2 * y + c


def _peer(x, y, c, mask):
    px = 1 - x if mask & 4 else x
    py = 1 - y if mask & 2 else y
    pc = 1 - c if mask & 1 else c
    return (px, py, pc), 4 * px + 2 * py + pc


ANY_SPEC = pl.BlockSpec(memory_space=pl.ANY)


def _comm_sems(n):
    return [pltpu.SemaphoreType.DMA((n, N_DEV - 1)), pltpu.SemaphoreType.DMA((n, N_DEV - 1)), pltpu.SemaphoreType.DMA((n,))]


def _gather_copies(ins, outs, sems):
    send_sems, recv_sems, local_sems = sems
    x, y, c, me = _place()
    starts, waits = [], []
    for w in range(len(ins)):
        mine = pltpu.make_async_copy(ins[w], outs[w].at[me], local_sems.at[w])
        starts.append(mine)
        waits.append(mine)
        for mask in range(1, N_DEV):
            peer, peer_idx = _peer(x, y, c, mask)
            starts.append(pltpu.make_async_remote_copy(
                src_ref=ins[w], dst_ref=outs[w].at[me], send_sem=send_sems.at[w, mask - 1],
                recv_sem=recv_sems.at[w, mask - 1], device_id=peer, device_id_type=MESH))
            waits.append(pltpu.make_async_remote_copy(
                src_ref=ins[w], dst_ref=outs[w].at[peer_idx], send_sem=send_sems.at[w, mask - 1],
                recv_sem=recv_sems.at[w, mask - 1], device_id=peer, device_id_type=MESH))
    return starts, waits


def _scatter_copies(ins, outs, sems):
    send_sems, recv_sems, local_sems = sems
    x, y, c, me = _place()
    copies = []
    for w in range(len(ins)):
        copies.append(pltpu.make_async_copy(ins[w].at[me], outs[w].at[0], local_sems.at[w]))
        for mask in range(1, N_DEV):
            peer, peer_idx = _peer(x, y, c, mask)
            copies.append(pltpu.make_async_remote_copy(
                src_ref=ins[w].at[peer_idx], dst_ref=outs[w].at[mask], send_sem=send_sems.at[w, mask - 1],
                recv_sem=recv_sems.at[w, mask - 1], device_id=peer, device_id_type=MESH))
    return copies, copies


def _stacked(arrays):
    return [jax.ShapeDtypeStruct((N_DEV,) + a.shape, a.dtype) for a in arrays]


def _all_gather(shards):
    n = len(shards)

    def body(*refs):
        ins, outs = refs[:n], refs[n:2 * n]
        send_sems, recv_sems, local_sems = refs[2 * n:]
        x, y, c, me = _place()
        sibling = (x, y, 1 - c)
        chips = [(1 - x, y), (x, 1 - y), (1 - x, 1 - y)]

        def copy(w, k, block, to, src=None):
            idx = 4 * block[0] + 2 * block[1] + block[2]
            return pltpu.make_async_remote_copy(
                src_ref=outs[w].at[idx] if src is None else src, dst_ref=outs[w].at[idx],
                send_sem=send_sems.at[w, k], recv_sem=recv_sems.at[w, k], device_id=to, device_id_type=MESH)

        local, sent = [], []
        for w in range(n):
            mine = pltpu.make_async_copy(ins[w], outs[w].at[me], local_sems.at[w])
            mine.start()
            local.append(mine)
            first = [copy(w, 0, (x, y, c), sibling, src=ins[w])]
            first += [copy(w, 1 + j, (x, y, c), (*chip, c), src=ins[w]) for j, chip in enumerate(chips)]
            for cp in first:
                cp.start()
            sent += first
        for w in range(n):
            for j, chip in enumerate(chips):
                copy(w, 1 + j, (*chip, c), (x, y, c)).wait_recv()
                onward = copy(w, 4 + j, (*chip, c), sibling)
                onward.start()
                sent.append(onward)
        for w in range(n):
            copy(w, 0, sibling, (x, y, c)).wait_recv()
            for j, chip in enumerate(chips):
                copy(w, 4 + j, (*chip, 1 - c), (x, y, c)).wait_recv()
        for cp in sent:
            cp.wait_send()
        for cp in local:
            cp.wait()

    return pl.pallas_call(
        body, name="weight_all_gather", out_shape=_stacked(shards),
        in_specs=[ANY_SPEC] * n, out_specs=[ANY_SPEC] * n, scratch_shapes=_comm_sems(n),
    )(*shards)


def _conv_fwd(x, a_norm, w_in, conv_w, w_out, n_seq, seq, later_shards):
    tm = ROW_TILE
    nt = seq // tm
    n_col = w_in.shape[2]
    n_later = len(later_shards)

    def body(x_ref, an_ref, win_ref, cw_ref, wout_ref, *rest):
        shard_refs, rest = rest[:n_later], rest[n_later:]
        x1_ref, proj_ref, conv_ref = rest[:3]
        stack_refs, rest = rest[3:3 + n_later], rest[3 + n_later:]
        prev_ref, sems = rest[0], rest[1:]
        step = pl.program_id(0) * nt + pl.program_id(1)

        @pl.when(step == 0)
        def _():
            for cp in _gather_copies(shard_refs, stack_refs, sems)[0]:
                cp.start()

        @pl.when(pl.program_id(1) == 0)
        def _():
            prev_ref[...] = jnp.zeros_like(prev_ref)

        xv = x_ref[...]
        h = (xv * _rstd(xv) * an_ref[...]).astype(BF16)
        for d in range(N_DEV):
            proj_ref[:, d * n_col:(d + 1) * n_col] = jnp.dot(h, win_ref[d], preferred_element_type=F32)
        b = proj_ref[:, 0:D_MODEL]
        v = proj_ref[:, D_MODEL:2 * D_MODEL] * proj_ref[:, 2 * D_MODEL:3 * D_MODEL]
        g = proj_ref[:, 3 * D_MODEL:4 * D_MODEL]
        w0, w1, w2 = cw_ref[0:1, :], cw_ref[1:2, :], cw_ref[2:3, :]
        conv_ref[...] = w0 * pltpu.roll(v, 2, 0) + w1 * pltpu.roll(v, 1, 0) + w2 * v
        rows = lax.broadcasted_iota(jnp.int32, (8, D_MODEL), 0)
        p8, v8 = prev_ref[...], v[0:8]
        back1 = jnp.where(rows < 1, pltpu.roll(p8, 1, 0), pltpu.roll(v8, 1, 0))
        back2 = jnp.where(rows < 2, pltpu.roll(p8, 2, 0), pltpu.roll(v8, 2, 0))
        conv_ref[0:8, :] = w0 * back2 + w1 * back1 + w2 * v8
        prev_ref[...] = v[tm - 8:tm]
        silu, _ = _silu_parts(g)
        yv = silu * b * conv_ref[...]
        x1_ref[...] = xv + _dot(yv, wout_ref[...])

        @pl.when(step == n_seq * nt - 1)
        def _():
            for cp in _gather_copies(shard_refs, stack_refs, sems)[1]:
                cp.wait()

    tok = lambda width: pl.BlockSpec((tm, width), lambda s, i: (s * nt + i, 0))
    t = n_seq * seq
    outs = pl.pallas_call(
        body, name="conv_mixer_fwd", grid=(n_seq, nt),
        in_specs=[tok(D_MODEL), _const_spec((1, D_MODEL)), _const_spec(w_in.shape), _const_spec((3, D_MODEL)),
                  _const_spec(w_out.shape)] + [ANY_SPEC] * n_later,
        out_specs=[tok(D_MODEL), tok(4 * D_MODEL), tok(D_MODEL)] + [ANY_SPEC] * n_later,
        out_shape=[jax.ShapeDtypeStruct((t, D_MODEL), F32), jax.ShapeDtypeStruct((t, 4 * D_MODEL), F32),
                   jax.ShapeDtypeStruct((t, D_MODEL), F32)] + _stacked(later_shards),
        scratch_shapes=[pltpu.VMEM((8, D_MODEL), F32)] + _comm_sems(n_later),
        compiler_params=_params(2),
    )(x, a_norm, w_in, conv_w, w_out, *later_shards)
    return outs[0], outs[1], outs[2], outs[3:]


def _conv_bwd(dx1, x, proj, conv, a_norm, w_in, conv_w, w_out, n_seq, seq, ready_stacks):
    tm = ROW_TILE
    nt = seq // tm
    n_col = w_in.shape[2]
    n_ready = len(ready_stacks)

    def body(dx1_ref, x_ref, proj_ref, conv_ref, an_ref, win_ref, cw_ref, wout_ref, *rest):
        ready_refs, rest = rest[:n_ready], rest[n_ready:]
        dx_ref, h_ref, dproj_ref, dwout_ref, dan_ref, dcw_ref = rest[:6]
        part_refs, rest = rest[6:6 + n_ready], rest[6 + n_ready:]
        next_ref, d1_ref, d2_ref = rest[:3]
        sems = rest[3:]
        step = pl.program_id(0) * nt + pl.program_id(1)
        first = step == 0

        @pl.when(first)
        def _():
            for cp in _scatter_copies(ready_refs, part_refs, sems)[0]:
                cp.start()
            dwout_ref[...] = jnp.zeros_like(dwout_ref)
            dan_ref[...] = jnp.zeros_like(dan_ref)
            dcw_ref[...] = jnp.zeros_like(dcw_ref)

        @pl.when(pl.program_id(1) == 0)
        def _():
            next_ref[...] = jnp.zeros_like(next_ref)

        dx1v = dx1_ref[...]
        dy = _dot_nt(dx1v, wout_ref[...])
        b = proj_ref[:, 0:D_MODEL]
        cc = proj_ref[:, D_MODEL:2 * D_MODEL]
        u = proj_ref[:, 2 * D_MODEL:3 * D_MODEL]
        g = proj_ref[:, 3 * D_MODEL:4 * D_MODEL]
        cv = conv_ref[...]
        silu, dsilu = _silu_parts(g)
        dwout_ref[...] += _dot_tn(silu * b * cv, dx1v)
        dproj_ref[:, 3 * D_MODEL:4 * D_MODEL] = (dy * b * cv * dsilu).astype(BF16)
        dproj_ref[:, 0:D_MODEL] = (dy * silu * cv).astype(BF16)
        dconv = dy * silu * b
        d1_ref[...] = pltpu.roll(dconv, tm - 1, 0)
        d2_ref[...] = pltpu.roll(dconv, tm - 2, 0)
        rows = lax.broadcasted_iota(jnp.int32, (8, D_MODEL), 0)
        n8, c8 = next_ref[...], dconv[tm - 8:tm]
        d1_ref[tm - 8:tm, :] = jnp.where(rows >= 7, pltpu.roll(n8, 7, 0), pltpu.roll(c8, 7, 0))
        d2_ref[tm - 8:tm, :] = jnp.where(rows >= 6, pltpu.roll(n8, 6, 0), pltpu.roll(c8, 6, 0))
        next_ref[...] = dconv[0:8]
        d1, d2 = d1_ref[...], d2_ref[...]
        v = cc * u
        dcw_ref[0:1, :] += jnp.sum(d2 * v, axis=0, keepdims=True)
        dcw_ref[1:2, :] += jnp.sum(d1 * v, axis=0, keepdims=True)
        dcw_ref[2:3, :] += jnp.sum(dconv * v, axis=0, keepdims=True)
        dv = cw_ref[0:1, :] * d2 + cw_ref[1:2, :] * d1 + cw_ref[2:3, :] * dconv
        dproj_ref[:, D_MODEL:2 * D_MODEL] = (dv * u).astype(BF16)
        dproj_ref[:, 2 * D_MODEL:3 * D_MODEL] = (dv * cc).astype(BF16)
        dh = jnp.zeros((tm, D_MODEL), F32)
        for d in range(N_DEV):
            dh += lax.dot_general(dproj_ref[:, d * n_col:(d + 1) * n_col], win_ref[d], NT, preferred_element_type=F32)
        xv = x_ref[...]
        r = _rstd(xv)
        xh = xv * r
        h_ref[...] = (xh * an_ref[...]).astype(BF16)
        dan_ref[...] += jnp.sum(dh * xh, axis=0, keepdims=True)
        dx_ref[...] = dx1v + _norm_bwd(dh * an_ref[...], xh, r)

        @pl.when(step == n_seq * nt - 1)
        def _():
            for cp in _scatter_copies(ready_refs, part_refs, sems)[1]:
                cp.wait()

    tok = lambda width: pl.BlockSpec((tm, width), lambda s, i: (s * nt + nt - 1 - i, 0))
    t = n_seq * seq
    outs = pl.pallas_call(
        body, name="conv_mixer_bwd", grid=(n_seq, nt),
        in_specs=[tok(D_MODEL), tok(D_MODEL), tok(4 * D_MODEL), tok(D_MODEL), _const_spec((1, D_MODEL)),
                  _const_spec(w_in.shape), _const_spec((3, D_MODEL)), _const_spec(w_out.shape)] + [ANY_SPEC] * n_ready,
        out_specs=[tok(D_MODEL), tok(D_MODEL), tok(4 * D_MODEL), _acc_spec((D_MODEL, D_MODEL)),
                   _acc_spec((1, D_MODEL)), _acc_spec((8, D_MODEL))] + [ANY_SPEC] * n_ready,
        out_shape=[jax.ShapeDtypeStruct((t, D_MODEL), F32), jax.ShapeDtypeStruct((t, D_MODEL), BF16),
                   jax.ShapeDtypeStruct((t, 4 * D_MODEL), BF16), jax.ShapeDtypeStruct((D_MODEL, D_MODEL), F32),
                   jax.ShapeDtypeStruct((1, D_MODEL), F32), jax.ShapeDtypeStruct((8, D_MODEL), F32)]
        + [jax.ShapeDtypeStruct(a.shape, a.dtype) for a in ready_stacks],
        scratch_shapes=[pltpu.VMEM((8, D_MODEL), F32), pltpu.VMEM((tm, D_MODEL), F32), pltpu.VMEM((tm, D_MODEL), F32)]
        + _comm_sems(n_ready),
        compiler_params=_params(2),
    )(dx1, x, proj, conv, a_norm, w_in, conv_w, w_out, *ready_stacks)
    return outs[:6], outs[6:]


def _w_in_grad_exchange(a, b, n_col, me, ready_stacks, small):
    t, r = a.shape
    bt = 1024
    nk = t // bt
    n_ready = len(ready_stacks)
    last_remote = N_DEV - 2

    def body(me_ref, a_ref, b_ref, *rest):
        ready_refs, small_ref, rest = rest[:n_ready], rest[n_ready], rest[n_ready + 1:]
        own_ref, got_ref, rest = rest[0], rest[1], rest[2:]
        part_refs, small_all_ref, rest = rest[:n_ready], rest[n_ready], rest[n_ready + 1:]
        acc_ref, stage_ref, shard_send, shard_recv = rest[:4]
        scatter_sems, gather_sems = rest[4:7], rest[7:10]
        s, k = pl.program_id(0), pl.program_id(1)
        x, y, c, me = _place()

        @pl.when(jnp.logical_and(s == 0, k == 0))
        def _():
            for cp in _scatter_copies(ready_refs, part_refs, scatter_sems)[0]:
                cp.start()
            for cp in _gather_copies([small_ref], [small_all_ref], gather_sems)[0]:
                cp.start()

        @pl.when(k == 0)
        def _():
            acc_ref[...] = jnp.zeros_like(acc_ref)

        acc_ref[...] += lax.dot_general(a_ref[...], b_ref[...], TN, preferred_element_type=F32)

        def shard_copy(step):
            owner = (me + 1 + step) % N_DEV
            slot = jnp.bitwise_xor(me, owner) - 1
            return pltpu.make_async_remote_copy(
                src_ref=stage_ref.at[step % 2], dst_ref=got_ref.at[slot], send_sem=shard_send.at[step],
                recv_sem=shard_recv.at[slot], device_id=(owner // 4, (owner // 2) % 2, owner % 2), device_id_type=MESH)

        for step in range(last_remote + 1):
            @pl.when(jnp.logical_and(s == step, k == nk - 1))
            def _(step=step):
                if step >= 2:
                    shard_copy(step - 2).wait_send()
                stage_ref[step % 2] = acc_ref[...].astype(BF16)
                shard_copy(step).start()

        @pl.when(jnp.logical_and(s == N_DEV - 1, k == nk - 1))
        def _():
            own_ref[...] = acc_ref[...]
            shard_copy(last_remote - 1).wait_send()
            shard_copy(last_remote).wait_send()
            for slot in range(N_DEV - 1):
                pltpu.make_async_remote_copy(
                    src_ref=stage_ref.at[0], dst_ref=got_ref.at[slot], send_sem=shard_send.at[0],
                    recv_sem=shard_recv.at[slot], device_id=(x, y, c), device_id_type=MESH).wait_recv()
            for cp in _scatter_copies(ready_refs, part_refs, scatter_sems)[1]:
                cp.wait()
            for cp in _gather_copies([small_ref], [small_all_ref], gather_sems)[1]:
                cp.wait()

    grid_spec = pltpu.PrefetchScalarGridSpec(
        num_scalar_prefetch=1, grid=(N_DEV, nk),
        in_specs=[pl.BlockSpec((bt, r), lambda s, k, me: (k, 0)),
                  pl.BlockSpec((bt, n_col), lambda s, k, me: (k, (me[0] + 1 + s) % N_DEV))] + [ANY_SPEC] * (n_ready + 1),
        out_specs=[pl.BlockSpec((r, n_col), lambda s, k, me: (0, 0)), ANY_SPEC] + [ANY_SPEC] * (n_ready + 1),
        scratch_shapes=[pltpu.VMEM((r, n_col), F32), pltpu.VMEM((2, r, n_col), BF16),
                        pltpu.SemaphoreType.DMA((N_DEV - 1,)), pltpu.SemaphoreType.DMA((N_DEV - 1,))]
        + _comm_sems(n_ready) + _comm_sems(1))
    outs = pl.pallas_call(
        body, name="w_in_grad_exchange", grid_spec=grid_spec,
        out_shape=[jax.ShapeDtypeStruct((r, n_col), F32), jax.ShapeDtypeStruct((N_DEV - 1, r, n_col), BF16)]
        + [jax.ShapeDtypeStruct(p.shape, p.dtype) for p in ready_stacks] + _stacked([small]),
        compiler_params=_params(2),
    )(me, a, b, *ready_stacks, small)
    return outs[0], outs[1], outs[2:2 + n_ready], outs[2 + n_ready]


def _proj_fwd(x1, pos, rc, kv_norm, w_dkv, ckv_norm, w_ukv, b_norm, b_w_in, q_norm, w_uq):
    t = x1.shape[0]
    tm = ROW_TILE

    def body(x1_ref, pos_ref, rc_ref, kvn_ref, wdkv_ref, ckvn_ref, wukv_ref, bn_ref, bwin_ref, qn_ref, wuq_ref,
             ckr_ref, cq_ref, gb_ref, q_ref, k_ref, v_ref):
        ct, s1, s2 = _rope_tables(pos_ref[...], rc_ref[...])
        xv = x1_ref[...]
        xh = xv * _rstd(xv)
        ckr = _dot(xh * kvn_ref[...], wdkv_ref[...])
        ckr_ref[...] = ckr
        ckv = ckr[:, 0:KV_RANK]
        kv = _dot(ckv * _rstd(ckv) * ckvn_ref[...], wukv_ref[...])
        k_rope = _rope(ckr[:, KV_RANK:CKR_PAD], ct, s1, s2)
        lane = lax.broadcasted_iota(jnp.int32, (tm, HEAD_PAD), 1)
        low = lane < QK_NOPE
        for h in range(N_HEADS):
            kv_h = kv[:, h * HEAD_PAD:(h + 1) * HEAD_PAD]
            k_ref[:, h * HEAD_PAD:(h + 1) * HEAD_PAD] = jnp.where(low, kv_h, k_rope).astype(BF16)
            v_ref[:, h * HEAD_PAD:(h + 1) * HEAD_PAD] = jnp.where(low, pltpu.roll(kv_h, V_HEAD, 1), 1.0).astype(BF16)
        pb = _dot(xh * bn_ref[...], bwin_ref[...])
        cq = pb[:, 0:Q_RANK]
        cq_ref[...] = cq
        gb_ref[...] = pb[:, Q_RANK:Q_RANK + E_B]
        q = _dot(cq * _rstd(cq) * qn_ref[...], wuq_ref[...])
        for h in range(N_HEADS):
            q_ref[:, h * HEAD_PAD:(h + 1) * HEAD_PAD] = _rope(
                q[:, h * HEAD_PAD:(h + 1) * HEAD_PAD], ct, s1, s2).astype(BF16)

    tok = lambda width: pl.BlockSpec((tm, width), lambda i: (i, 0))
    weights = [kv_norm, w_dkv, ckv_norm, w_ukv, b_norm, b_w_in, q_norm, w_uq]
    wide = N_HEADS * HEAD_PAD
    return pl.pallas_call(
        body, name="latent_proj_fwd", grid=(t // tm,),
        in_specs=[tok(D_MODEL), tok(1), _const_spec(rc.shape)] + [_const_spec(w.shape) for w in weights],
        out_specs=[tok(CKR_PAD), tok(Q_RANK), tok(E_B), tok(wide), tok(wide), tok(wide)],
        out_shape=[jax.ShapeDtypeStruct((t, CKR_PAD), F32), jax.ShapeDtypeStruct((t, Q_RANK), F32),
                   jax.ShapeDtypeStruct((t, E_B), F32), jax.ShapeDtypeStruct((t, wide), BF16),
                   jax.ShapeDtypeStruct((t, wide), BF16), jax.ShapeDtypeStruct((t, wide), BF16)],
        compiler_params=_params(1),
    )(x1, pos, rc, *weights)


def _proj_bwd(dq, dk, dv, dgb, cq, ckr, x1, dx2, pos, rc, kv_norm, w_dkv, ckv_norm, w_ukv, b_norm, b_w_in, q_norm, w_uq):
    t = x1.shape[0]
    tm = ROW_TILE
    wide = N_HEADS * HEAD_PAD

    def body(dq_ref, dk_ref, dv_ref, dgb_ref, cq_ref, ckr_ref, x1_ref, dx2_ref, pos_ref, rc_ref,
             kvn_ref, wdkv_ref, ckvn_ref, wukv_ref, bn_ref, bwin_ref, qn_ref, wuq_ref,
             dx1_ref, dwuq_ref, dwbin_ref, dwukv_ref, dwdkv_ref, dqn_ref, dbn_ref, dckvn_ref, dkvn_ref, dqu_ref, dkv_ref):
        @pl.when(pl.program_id(0) == 0)
        def _():
            for ref in (dwuq_ref, dwbin_ref, dwukv_ref, dwdkv_ref, dqn_ref, dbn_ref, dckvn_ref, dkvn_ref):
                ref[...] = jnp.zeros_like(ref)

        ct, s1, s2 = _rope_tables(pos_ref[...], rc_ref[...])
        lane = lax.broadcasted_iota(jnp.int32, (tm, HEAD_PAD), 1)
        low = lane < QK_NOPE
        for h in range(N_HEADS):
            dqu_ref[:, h * HEAD_PAD:(h + 1) * HEAD_PAD] = _rope(
                dq_ref[:, h * HEAD_PAD:(h + 1) * HEAD_PAD], ct, -s1, -s2).astype(BF16)
        cq = cq_ref[...]
        rq = _rstd(cq)
        cqh = cq * rq
        cqn = (cqh * qn_ref[...]).astype(BF16)
        for h in range(N_HEADS):
            dwuq_ref[h] += lax.dot_general(cqn, dqu_ref[:, h * HEAD_PAD:(h + 1) * HEAD_PAD], TN, preferred_element_type=F32)
        dcqn = lax.dot_general(dqu_ref[...], wuq_ref[...], NT, preferred_element_type=F32)
        dqn_ref[...] += jnp.sum(dcqn * cqh, axis=0, keepdims=True)
        dcq = _norm_bwd(dcqn * qn_ref[...], cqh, rq)
        dpb = jnp.concatenate([dcq, dgb_ref[...]], axis=1).astype(BF16)
        xv = x1_ref[...]
        r = _rstd(xv)
        xh = xv * r
        dwbin_ref[...] += _dot_tn(xh * bn_ref[...], dpb)
        dh3 = lax.dot_general(dpb, bwin_ref[...], NT, preferred_element_type=F32)
        dk_rope = jnp.zeros((tm, HEAD_PAD), F32)
        for h in range(N_HEADS):
            dk_h = dk_ref[:, h * HEAD_PAD:(h + 1) * HEAD_PAD]
            dv_h = dv_ref[:, h * HEAD_PAD:(h + 1) * HEAD_PAD]
            dkv_ref[:, h * HEAD_PAD:(h + 1) * HEAD_PAD] = jnp.where(low, dk_h, pltpu.roll(dv_h, V_HEAD, 1)).astype(BF16)
            dk_rope += dk_h
        rope_lanes = jnp.logical_and(lane >= ROPE_LO, lane < ROPE_LO + QK_ROPE)
        dk_rope = jnp.where(rope_lanes, _rope(dk_rope, ct, -s1, -s2), 0.0)
        ckv = ckr_ref[:, 0:KV_RANK]
        rk = _rstd(ckv)
        ckh = ckv * rk
        ckvn = (ckh * ckvn_ref[...]).astype(BF16)
        for h in range(N_HEADS):
            dwukv_ref[h] += lax.dot_general(ckvn, dkv_ref[:, h * HEAD_PAD:(h + 1) * HEAD_PAD], TN, preferred_element_type=F32)
        dckvn = lax.dot_general(dkv_ref[...], wukv_ref[...], NT, preferred_element_type=F32)
        dckvn_ref[...] += jnp.sum(dckvn * ckh, axis=0, keepdims=True)
        dckv = _norm_bwd(dckvn * ckvn_ref[...], ckh, rk)
        dckr = jnp.concatenate([dckv, dk_rope], axis=1).astype(BF16)
        dwdkv_ref[...] += _dot_tn(xh * kvn_ref[...], dckr)
        dh2 = lax.dot_general(dckr, wdkv_ref[...], NT, preferred_element_type=F32)
        dkvn_ref[...] += jnp.sum(dh2 * xh, axis=0, keepdims=True)
        dbn_ref[...] += jnp.sum(dh3 * xh, axis=0, keepdims=True)
        dx1_ref[...] = dx2_ref[...] + _norm_bwd(dh2 * kvn_ref[...] + dh3 * bn_ref[...], xh, r)

    tok = lambda width: pl.BlockSpec((tm, width), lambda i: (i, 0))
    weights = [kv_norm, w_dkv, ckv_norm, w_ukv, b_norm, b_w_in, q_norm, w_uq]
    acc_shapes = [(N_HEADS, Q_RANK, HEAD_PAD), (D_MODEL, Q_RANK + E_B), (N_HEADS, KV_RANK, HEAD_PAD), (D_MODEL, CKR_PAD),
                  (1, Q_RANK), (1, D_MODEL), (1, KV_RANK), (1, D_MODEL)]
    return pl.pallas_call(
        body, name="latent_proj_bwd", grid=(t // tm,),
        in_specs=[tok(wide), tok(wide), tok(wide), tok(E_B), tok(Q_RANK), tok(CKR_PAD), tok(D_MODEL), tok(D_MODEL), tok(1),
                  _const_spec(rc.shape)] + [_const_spec(w.shape) for w in weights],
        out_specs=[tok(D_MODEL)] + [_acc_spec(s) for s in acc_shapes],
        out_shape=[jax.ShapeDtypeStruct((t, D_MODEL), F32)] + [jax.ShapeDtypeStruct(s, F32) for s in acc_shapes],
        scratch_shapes=[pltpu.VMEM((tm, wide), BF16), pltpu.VMEM((tm, wide), BF16)],
        compiler_params=_params(1),
    )(dq, dk, dv, dgb, cq, ckr, x1, dx2, pos, rc, *weights)


def _attn_fwd(q, k, v, n_seq, seq):
    tq, tk = ATTN_FWD_TILES
    ratio = tq // tk
    nq = seq // tq
    pair = 2 * HEAD_PAD

    def body(q_ref, k_ref, v_ref, o_ref, lr_ref):
        i = pl.program_id(2)
        row = lax.broadcasted_iota(jnp.int32, (tq, tk), 0)
        col = lax.broadcasted_iota(jnp.int32, (tq, tk), 1)
        qs = [q_ref[:, hh * HEAD_PAD:(hh + 1) * HEAD_PAD] for hh in range(2)]

        def step(j, carry, shift):
            start = pl.multiple_of(j * tk, tk)
            out = []
            for hh in range(2):
                m, acc = carry[hh]
                s = lax.dot_general(qs[hh], k_ref[pl.ds(start, tk), hh * HEAD_PAD:(hh + 1) * HEAD_PAD], NT,
                                    preferred_element_type=F32) * (SOFTMAX_SCALE * LOG2_E)
                if shift is not None:
                    s = jnp.where(col + shift <= row, s, -jnp.inf)
                m_new = jnp.maximum(m, jnp.max(s, axis=-1, keepdims=True))
                p = jnp.exp2(s - m_new)
                acc = jnp.exp2(m - m_new) * acc + jnp.dot(
                    p.astype(BF16), v_ref[pl.ds(start, tk), hh * HEAD_PAD:(hh + 1) * HEAD_PAD], preferred_element_type=F32)
                out.append((m_new, acc))
            return tuple(out)

        one = (jnp.full((tq, 1), -jnp.inf, F32), jnp.zeros((tq, HEAD_PAD), F32))
        carry = lax.fori_loop(0, i * ratio, functools.partial(step, shift=None), (one, one))
        for d in range(ratio):
            carry = step(i * ratio + d, carry, d * tk)
        lane = lax.broadcasted_iota(jnp.int32, (tq, HEAD_PAD), 1)
        low = lane < V_HEAD
        halves = []
        for hh in range(2):
            m, acc = carry[hh]
            swapped = pltpu.roll(acc, V_HEAD, 1)
            halves.append(acc / swapped)
            lse = (m * LN_2) + jnp.log(jnp.where(low, swapped, acc))
            lr_ref[hh] = lse.T[0:8, :]
        o_ref[...] = jnp.where(low, halves[0], pltpu.roll(halves[1], V_HEAD, 1))

    t = n_seq * seq
    return pl.pallas_call(
        body, name="attention_fwd", grid=(n_seq, N_HEADS // 2, nq),
        in_specs=[pl.BlockSpec((tq, pair), lambda s, p, i: (s * nq + i, p)),
                  pl.BlockSpec((seq, pair), lambda s, p, i: (s, p)),
                  pl.BlockSpec((seq, pair), lambda s, p, i: (s, p))],
        out_specs=[pl.BlockSpec((tq, HEAD_PAD), lambda s, p, i: (s * nq + i, p)),
                   pl.BlockSpec((None, 2, 8, tq), lambda s, p, i: (s, p, 0, i))],
        out_shape=[jax.ShapeDtypeStruct((t, E_B), F32), jax.ShapeDtypeStruct((n_seq, N_HEADS, 8, seq), F32)],
        compiler_params=_params(3),
    )(q, k, v)


def _attn_bwd(q, k, v, do, lse_row, delta_row, n_seq, seq):
    tk, tq = ATTN_BWD_TILES
    ratio = tk // tq
    nk = seq // tk
    n_inner = seq // tq
    pair = 2 * HEAD_PAD

    def body(q_ref, k_ref, v_ref, do_ref, lr_ref, dr_ref, dk_ref, dv_ref, dq_ref):
        j = pl.program_id(2)

        @pl.when(j == 0)
        def _():
            dq_ref[...] = jnp.zeros_like(dq_ref)

        row = lax.broadcasted_iota(jnp.int32, (tk, tq), 0)
        col = lax.broadcasted_iota(jnp.int32, (tk, tq), 1)
        ks = [k_ref[:, hh * HEAD_PAD:(hh + 1) * HEAD_PAD] for hh in range(2)]
        vs = [v_ref[:, hh * HEAD_PAD:(hh + 1) * HEAD_PAD] for hh in range(2)]

        def step(i, carry, shift):
            start = pl.multiple_of(i * tq, tq)
            out = []
            for hh in range(2):
                dk_acc, dv_acc = carry[hh]
                qi = q_ref[pl.ds(start, tq), hh * HEAD_PAD:(hh + 1) * HEAD_PAD]
                doi = do_ref[pl.ds(start, tq), hh * HEAD_PAD:(hh + 1) * HEAD_PAD]
                st = lax.dot_general(ks[hh], qi, NT, preferred_element_type=F32) * SOFTMAX_SCALE
                pt = jnp.exp(st - lr_ref[hh, 0:1, pl.ds(start, tq)])
                if shift is not None:
                    pt = jnp.where(col + shift >= row, pt, 0.0)
                dv_acc = dv_acc + jnp.dot(pt.astype(BF16), doi, preferred_element_type=F32)
                dpt = lax.dot_general(vs[hh], doi, NT, preferred_element_type=F32)
                dst = (pt * (dpt - dr_ref[hh, 0:1, pl.ds(start, tq)]) * SOFTMAX_SCALE).astype(BF16)
                dk_acc = dk_acc + jnp.dot(dst, qi, preferred_element_type=F32)
                dq_ref[pl.ds(start, tq), hh * HEAD_PAD:(hh + 1) * HEAD_PAD] += lax.dot_general(
                    dst, ks[hh], TN, preferred_element_type=F32)
                out.append((dk_acc, dv_acc))
            return tuple(out)

        one = (jnp.zeros((tk, HEAD_PAD), F32), jnp.zeros((tk, HEAD_PAD), F32))
        carry = (one, one)
        for d in range(ratio):
            carry = step(j * ratio + d, carry, d * tq)
        carry = lax.fori_loop((j + 1) * ratio, n_inner, functools.partial(step, shift=None), carry)
        for hh in range(2):
            dk_ref[:, hh * HEAD_PAD:(hh + 1) * HEAD_PAD] = carry[hh][0]
            dv_ref[:, hh * HEAD_PAD:(hh + 1) * HEAD_PAD] = carry[hh][1]

    t = n_seq * seq
    wide = N_HEADS * HEAD_PAD
    return pl.pallas_call(
        body, name="attention_bwd", grid=(n_seq, N_HEADS // 2, nk),
        in_specs=[pl.BlockSpec((seq, pair), lambda s, p, j: (s, p)),
                  pl.BlockSpec((tk, pair), lambda s, p, j: (s * nk + j, p)),
                  pl.BlockSpec((tk, pair), lambda s, p, j: (s * nk + j, p)),
                  pl.BlockSpec((seq, pair), lambda s, p, j: (s, p)),
                  pl.BlockSpec((None, 2, 8, seq), lambda s, p, j: (s, p, 0, 0)),
                  pl.BlockSpec((None, 2, 8, seq), lambda s, p, j: (s, p, 0, 0))],
        out_specs=[pl.BlockSpec((tk, pair), lambda s, p, j: (s * nk + j, p)),
                   pl.BlockSpec((tk, pair), lambda s, p, j: (s * nk + j, p)),
                   pl.BlockSpec((seq, pair), lambda s, p, j: (s, p))],
        out_shape=[jax.ShapeDtypeStruct((t, wide), F32), jax.ShapeDtypeStruct((t, wide), F32),
                   jax.ShapeDtypeStruct((t, wide), F32)],
        compiler_params=_params(3),
    )(q, k, v, do, lse_row, delta_row)


def _head_tail(o, gb, x1, w_out, final_norm, target, n_seq, seq):
    tm = ROW_TILE
    nt = seq // tm
    n_col = D_MODEL // N_DEV

    def body(o_ref, gb_ref, x1_ref, wout_ref, fn_ref, tgt_ref,
             dx2_ref, do_ref, dgb_ref, dr_ref, dwout_ref, dfn_ref, loss_ref):
        first = jnp.logical_and(pl.program_id(0) == 0, pl.program_id(1) == 0)

        @pl.when(first)
        def _():
            dwout_ref[...] = jnp.zeros_like(dwout_ref)
            dfn_ref[...] = jnp.zeros_like(dfn_ref)
            loss_ref[...] = jnp.zeros_like(loss_ref)

        ov, g = o_ref[...], gb_ref[...]
        silu, dsilu = _silu_parts(g)
        gated = (ov * silu).astype(BF16)
        x2 = x1_ref[...] + jnp.dot(gated, wout_ref[...], preferred_element_type=F32)
        r = _rstd(x2)
        xh = x2 * r
        err = xh * fn_ref[...] - tgt_ref[...]
        loss_ref[...] += 0.5 * jnp.sum(jnp.mean(err * err, axis=-1, keepdims=True), axis=0, keepdims=True)
        dy = err / D_MODEL
        dfn_ref[...] += jnp.sum(dy * xh, axis=0, keepdims=True)
        dx2 = _norm_bwd(dy * fn_ref[...], xh, r)
        dx2_ref[...] = dx2
        dx2b = dx2.astype(BF16)
        dw = lax.dot_general(gated, dx2b, TN, preferred_element_type=F32)
        for d in range(N_DEV):
            dwout_ref[d] += dw[:, d * n_col:(d + 1) * n_col]
        dgated = lax.dot_general(dx2b, wout_ref[...], NT, preferred_element_type=F32)
        do = dgated * silu
        dgb_ref[...] = dgated * ov * dsilu
        prod = do * ov
        lane = lax.broadcasted_iota(jnp.int32, (tm, HEAD_PAD), 1)
        low = lane < V_HEAD
        for p in range(N_HEADS // 2):
            blk = prod[:, p * HEAD_PAD:(p + 1) * HEAD_PAD]
            do_pair = do[:, p * HEAD_PAD:(p + 1) * HEAD_PAD]
            for hh in range(2):
                h = 2 * p + hh
                mine = do_pair if hh == 0 else pltpu.roll(do_pair, V_HEAD, 1)
                do_ref[:, h * HEAD_PAD:(h + 1) * HEAD_PAD] = jnp.where(low, mine, 0.0).astype(BF16)
                delta = jnp.sum(jnp.where(low if hh == 0 else ~low, blk, 0.0), axis=-1, keepdims=True)
                dr_ref[h] = jnp.broadcast_to(delta, (tm, HEAD_PAD)).T[0:8, :]

    tok = lambda width: pl.BlockSpec((tm, width), lambda s, i: (s * nt + i, 0))
    t = n_seq * seq
    return pl.pallas_call(
        body, name="head_tail", grid=(n_seq, nt),
        in_specs=[tok(E_B), tok(E_B), tok(D_MODEL), _const_spec(w_out.shape), _const_spec((1, D_MODEL)), tok(D_MODEL)],
        out_specs=[tok(D_MODEL), tok(N_HEADS * HEAD_PAD), tok(E_B),
                   pl.BlockSpec((None, N_HEADS, 8, tm), lambda s, i: (s, 0, 0, i)),
                   _acc_spec((N_DEV, E_B, n_col)), _acc_spec((1, D_MODEL)), _acc_spec((1, 1))],
        out_shape=[jax.ShapeDtypeStruct((t, D_MODEL), F32), jax.ShapeDtypeStruct((t, N_HEADS * HEAD_PAD), BF16),
                   jax.ShapeDtypeStruct((t, E_B), F32),
                   jax.ShapeDtypeStruct((n_seq, N_HEADS, 8, seq), F32), jax.ShapeDtypeStruct((N_DEV, E_B, n_col), F32),
                   jax.ShapeDtypeStruct((1, D_MODEL), F32), jax.ShapeDtypeStruct((1, 1), F32)],
        compiler_params=_params(2),
    )(o, gb, x1, w_out, final_norm, target)


def _adamw_math(w, g, m, v):
    m = ADAM_B1 * m + (1.0 - ADAM_B1) * g
    v = ADAM_B2 * v + (1.0 - ADAM_B2) * jnp.square(g)
    m_hat = m / (1.0 - ADAM_B1 ** ADAM_STEP)
    v_hat = v / (1.0 - ADAM_B2 ** ADAM_STEP)
    delta = -ADAM_LR * (m_hat / (jnp.sqrt(v_hat) + ADAM_EPS) + ADAM_WD * w)
    return delta, m, v


def _adamw_reduce(own, parts, w, m, v):
    rows, cols = w.shape
    br = 256 if rows % 256 == 0 else 128
    n_parts = parts.shape[0]

    def body(*refs):
        p_ref, w_ref, m_ref, v_ref, g_ref, d_ref, nm_ref, nv_ref = refs[-8:]
        g = p_ref[0].astype(F32) if own is None else refs[0][...] + p_ref[0].astype(F32)
        for k in range(1, n_parts):
            g = g + p_ref[k].astype(F32)
        g_ref[...] = g
        d_ref[...], nm_ref[...], nv_ref[...] = _adamw_math(w_ref[...], g, m_ref[...], v_ref[...])

    blk = pl.BlockSpec((br, cols), lambda i: (i, 0))
    first = [] if own is None else [own]
    return pl.pallas_call(
        body, name="adamw_reduce", grid=(rows // br,),
        in_specs=[blk] * len(first) + [pl.BlockSpec((n_parts, br, cols), lambda i: (0, i, 0)), blk, blk, blk],
        out_specs=[blk] * 4, out_shape=[jax.ShapeDtypeStruct((rows, cols), F32)] * 4,
        compiler_params=_params(1),
    )(*first, parts, w, m, v)


def _sum_parts(parts):
    def body(p_ref, o_ref):
        g = p_ref[0]
        for k in range(1, N_DEV):
            g = g + p_ref[k]
        o_ref[...] = g

    return pl.pallas_call(body, name="small_grad_sum", out_shape=jax.ShapeDtypeStruct(parts.shape[1:], F32))(parts)


def _adamw_small(gs, ws, ms, vs):
    n = len(gs)

    def body(*refs):
        ins, outs = refs[:4 * n], refs[4 * n:]
        for j in range(n):
            g_ref, w_ref, m_ref, v_ref = ins[j], ins[n + j], ins[2 * n + j], ins[3 * n + j]
            outs[3 * j][...], outs[3 * j + 1][...], outs[3 * j + 2][...] = _adamw_math(
                w_ref[...], g_ref[...], m_ref[...], v_ref[...])

    shapes = [jax.ShapeDtypeStruct(g.shape, F32) for g in gs for _ in range(3)]
    flat = pl.pallas_call(body, name="adamw_small", out_shape=shapes)(*gs, *ws, *ms, *vs)
    return [tuple(flat[3 * j:3 * j + 3]) for j in range(n)]


SMALL_ROWS = 8
LOSS_LANE = D_MODEL - 1


def _pack_small(g_fn, g_kvn, g_bn, g_an, g_ckvn, g_qn, g_conv, loss_part):
    def body(fn_ref, kvn_ref, bn_ref, an_ref, ckvn_ref, qn_ref, conv_ref, loss_ref, o_ref):
        o_ref[0:1, :] = fn_ref[...]
        o_ref[1:2, :] = kvn_ref[...]
        o_ref[2:3, :] = bn_ref[...]
        o_ref[3:4, :] = an_ref[...]
        lane = lax.broadcasted_iota(jnp.int32, (1, D_MODEL), 1)
        o_ref[4:5, :] = jnp.where(lane == LOSS_LANE, loss_ref[...], 0.0)
        o_ref[4:5, 0:KV_RANK] = ckvn_ref[...]
        o_ref[4:5, KV_RANK:KV_RANK + Q_RANK] = qn_ref[...]
        o_ref[5:8, :] = conv_ref[0:3, :]

    return pl.pallas_call(body, name="pack_small", out_shape=jax.ShapeDtypeStruct((SMALL_ROWS, D_MODEL), F32))(
        g_fn, g_kvn, g_bn, g_an, g_ckvn, g_qn, g_conv, loss_part)


def _pad_cols(a, width):
    return jnp.pad(a, ((0, 0), (0, width - a.shape[1])))


def _dkv_to_padded(a):
    r = a.shape[0]
    z = jnp.zeros((r, ROPE_LO), a.dtype)
    z2 = jnp.zeros((r, HEAD_PAD - ROPE_LO - QK_ROPE), a.dtype)
    return jnp.concatenate([a[:, :KV_RANK], z, a[:, KV_RANK:], z2], axis=1)


def _dkv_from_padded(a):
    return jnp.concatenate([a[:, :KV_RANK], a[:, KV_RANK + ROPE_LO:KV_RANK + ROPE_LO + QK_ROPE]], axis=1)


def _unstack_cols(a):
    return jnp.transpose(a, (1, 0, 2)).reshape(a.shape[1], N_DEV * a.shape[2])


def kernel(x, positions, a_norm, a_w_in, a_conv, a_w_out, kv_norm, w_dkv, ckv_norm, w_ukv, b_norm, b_w_in, b_q_norm, b_w_uq, b_w_out, final_norm, loss_target, m_a_norm, m_a_w_in, m_a_conv, m_a_w_out, m_kv_norm, m_w_dkv, m_ckv_norm, m_w_ukv, m_b_norm, m_b_w_in, m_b_q_norm, m_b_w_uq, m_b_w_out, m_final_norm, v_a_norm, v_a_w_in, v_a_conv, v_a_w_out, v_kv_norm, v_w_dkv, v_ckv_norm, v_w_ukv, v_b_norm, v_b_w_in, v_b_q_norm, v_b_w_uq, v_b_w_out, v_final_norm):
    n_seq, seq, _ = x.shape
    t = n_seq * seq
    me = 4 * lax.axis_index("x") + 2 * lax.axis_index("y") + lax.axis_index("c")

    big = {
        "a_w_in": (a_w_in[0], m_a_w_in[0], v_a_w_in[0]),
        "a_w_out": (a_w_out[0], m_a_w_out[0], v_a_w_out[0]),
        "w_dkv": tuple(_dkv_to_padded(a) for a in (w_dkv, m_w_dkv, v_w_dkv)),
        "w_ukv": (w_ukv, m_w_ukv, v_w_ukv),
        "b_w_in": (b_w_in[0], m_b_w_in[0], v_b_w_in[0]),
        "b_w_uq": tuple(_pad_cols(a[0], HEAD_PAD) for a in (b_w_uq, m_b_w_uq, v_b_w_uq)),
        "b_w_out": (b_w_out[0], m_b_w_out[0], v_b_w_out[0]),
    }
    names = list(big)
    first_names, later_names = names[:2], names[2:]
    gathered = _all_gather([big[n][0].astype(BF16) for n in first_names] + [a_norm, a_conv[0]])
    a_norm_f = gathered[2].reshape(1, D_MODEL)
    a_conv_f = _unstack_cols(gathered[3])
    w_a_in = gathered[0]
    w_a_out = gathered[1].reshape(D_MODEL, D_MODEL)

    x2d = x.reshape(t, D_MODEL)
    tgt = loss_target.reshape(t, D_MODEL)
    pos = positions.astype(F32).reshape(t, 1)
    rc = _rope_consts()
    kvn, ckvn, bn, qn, fn = kv_norm.reshape(1, -1), ckv_norm.reshape(1, -1), b_norm, b_q_norm, final_norm.reshape(1, -1)

    x1, proj, conv, later = _conv_fwd(x2d, a_norm_f, w_a_in, a_conv_f, w_a_out, n_seq, seq,
                                      [big[n][0].astype(BF16) for n in later_names])
    full = dict(zip(later_names, later))
    w_dkv_f = full["w_dkv"].reshape(D_MODEL, CKR_PAD)
    w_ukv_f = _unstack_cols(full["w_ukv"])
    w_b_in = full["b_w_in"].reshape(D_MODEL, Q_RANK + E_B)
    w_uq_f = _unstack_cols(full["b_w_uq"])
    w_b_out = _unstack_cols(full["b_w_out"])
    layer_b = (kvn, w_dkv_f, ckvn, w_ukv_f, bn, w_b_in, qn, w_uq_f)
    ckr, cq, gb, q, k, v = _proj_fwd(x1, pos, rc, *layer_b)
    o, lse_row = _attn_fwd(q, k, v, n_seq, seq)
    dx2, do, dgb, delta_row, g_b_out, g_fn, loss_part = _head_tail(o, gb, x1, w_b_out, fn, tgt, n_seq, seq)
    dk, dv, dq = _attn_bwd(q, k, v, do, lse_row, delta_row, n_seq, seq)
    dx1, g_uq, g_b_in, g_ukv, g_dkv, g_qn, g_bn, g_ckvn, g_kvn = _proj_bwd(
        dq, dk, dv, dgb, cq, ckr, x1, dx2, pos, rc, *layer_b)
    stacks = {
        "w_dkv": g_dkv.reshape(N_DEV, D_MODEL // N_DEV, CKR_PAD),
        "w_ukv": g_ukv,
        "b_w_in": g_b_in.reshape(N_DEV, D_MODEL // N_DEV, Q_RANK + E_B),
        "b_w_uq": g_uq,
        "b_w_out": g_b_out,
    }
    (dx, h_a, dproj, g_a_out, g_an, g_conv), later_parts = _conv_bwd(
        dx1, x2d, proj, conv, a_norm_f, w_a_in, a_conv_f, w_a_out, n_seq, seq, [stacks[n] for n in later_names])
    parts = dict(zip(later_names, later_parts))
    small = _pack_small(g_fn, g_kvn, g_bn, g_an, g_ckvn, g_qn, g_conv, loss_part)
    a_in_own, a_in_got, (parts["a_w_out"],), small_parts = _w_in_grad_exchange(
        h_a, dproj, D_MODEL * 4 // N_DEV, me.reshape(1).astype(jnp.int32),
        [g_a_out.reshape(N_DEV, D_MODEL // N_DEV, D_MODEL)], small)

    outs = {"a_w_in": _adamw_reduce(a_in_own, a_in_got, *big["a_w_in"])}
    for n in names[1:]:
        outs[n] = _adamw_reduce(None, parts[n], *big[n])
    outs["w_dkv"] = tuple(_dkv_from_padded(a) for a in outs["w_dkv"])
    outs["b_w_uq"] = tuple(a[:, :QK_NOPE + QK_ROPE] for a in outs["b_w_uq"])
    for n in ("a_w_in", "a_w_out", "b_w_in", "b_w_uq", "b_w_out"):
        outs[n] = tuple(a[None] for a in outs[n])

    total = _sum_parts(small_parts)
    loss = total[4, LOSS_LANE]
    shard = D_MODEL // N_DEV
    g_small = {
        "final_norm": total[0], "kv_norm": total[1], "b_norm": total[2:3],
        "a_norm": lax.dynamic_slice_in_dim(total[3:4], me * shard, shard, axis=1),
        "ckv_norm": total[4, 0:KV_RANK], "b_q_norm": total[4:5, KV_RANK:KV_RANK + Q_RANK],
        "a_conv": lax.dynamic_slice_in_dim(total[5:8], me * shard, shard, axis=1)[None],
    }
    small_state = {
        "final_norm": (final_norm, m_final_norm, v_final_norm), "kv_norm": (kv_norm, m_kv_norm, v_kv_norm),
        "b_norm": (b_norm, m_b_norm, v_b_norm), "a_norm": (a_norm, m_a_norm, v_a_norm),
        "ckv_norm": (ckv_norm, m_ckv_norm, v_ckv_norm), "b_q_norm": (b_q_norm, m_b_q_norm, v_b_q_norm),
        "a_conv": (a_conv, m_a_conv, v_a_conv),
    }
    small_names = list(g_small)
    as2d = lambda a: a.reshape(-1, a.shape[-1])
    upd = _adamw_small([as2d(g_small[n]) for n in small_names],
                       *[[as2d(small_state[n][j]) for n in small_names] for j in range(3)])
    for n, u in zip(small_names, upd):
        outs[n] = (g_small[n],) + tuple(a.reshape(g_small[n].shape) for a in u)

    order = ["a_norm", "a_w_in", "a_conv", "a_w_out", "kv_norm", "w_dkv", "ckv_norm", "w_ukv", "b_norm", "b_w_in",
             "b_q_norm", "b_w_uq", "b_w_out", "final_norm"]
    result = [loss, dx.reshape(n_seq, seq, D_MODEL)]
    for j in range(4):
        result += [outs[n][j] for n in order]
    return tuple(result)
```

```python
import functools
import math

import numpy as np
import jax
import jax.numpy as jnp
from jax import lax
from jax.experimental import pallas as pl
from jax.experimental.pallas import tpu as pltpu

F32 = jnp.float32
BF16 = jnp.bfloat16

D_MODEL = 1024
N_HEADS = 8
QK_NOPE = 64
QK_ROPE = 32
V_HEAD = 64
KV_RANK = 256
Q_RANK = 384
E_B = N_HEADS * V_HEAD
HEAD_PAD = 128
CKR_PAD = KV_RANK + HEAD_PAD
ROPE_LO = QK_NOPE
ROPE_HALF = QK_ROPE // 2
ROPE_THETA = 10000.0
SOFTMAX_SCALE = 1.0 / math.sqrt(QK_NOPE + QK_ROPE)
LOG2_E = math.log2(math.e)
LN_2 = math.log(2.0)
EPS = 1e-6
N_DEV = 8

ADAM_LR = 0.001
ADAM_B1 = 0.9
ADAM_B2 = 0.999
ADAM_EPS = 1e-08
ADAM_WD = 0.01
ADAM_STEP = 10

ROW_TILE = 256
LATENT_ROW_TILE = 512
ATTN_FWD_TILES = (512, 512)
ATTN_BWD_TILES = (512, 512)
VMEM_LIMIT = 56 * 1024 * 1024

MESH = pl.DeviceIdType.MESH
NT = (((1,), (1,)), ((), ()))
TN = (((0,), (0,)), ((), ()))


def _dot(a, b):
    return jnp.dot(a.astype(BF16), b.astype(BF16), preferred_element_type=F32)


def _dot_nt(a, b):
    return lax.dot_general(a.astype(BF16), b.astype(BF16), NT, preferred_element_type=F32)


def _dot_tn(a, b):
    return lax.dot_general(a.astype(BF16), b.astype(BF16), TN, preferred_element_type=F32)


def _rstd(x):
    return lax.rsqrt(jnp.mean(x * x, axis=-1, keepdims=True) + EPS)


def _norm_bwd(a, xh, r):
    return r * (a - xh * jnp.mean(a * xh, axis=-1, keepdims=True))


def _silu_parts(g):
    sg = jax.nn.sigmoid(g)
    return g * sg, sg * (1.0 + g * (1.0 - sg))


def _rope_consts():
    inv = (ROPE_THETA ** (-np.arange(0, QK_ROPE, 2, dtype=np.float32) / QK_ROPE)).astype(np.float32)
    t = np.zeros((8, HEAD_PAD), np.float32)
    t[0, ROPE_LO:ROPE_LO + ROPE_HALF] = inv
    t[0, ROPE_LO + ROPE_HALF:ROPE_LO + QK_ROPE] = inv
    t[1, ROPE_LO:ROPE_LO + ROPE_HALF] = -1.0
    t[2, ROPE_LO + ROPE_HALF:ROPE_LO + QK_ROPE] = 1.0
    return jnp.asarray(t)


def _rope_tables(pos, rc):
    ang = pos * rc[0:1, :]
    cosv = jnp.cos(ang)
    sinv = jnp.sin(ang)
    return cosv, sinv * rc[1:2, :], sinv * rc[2:3, :]


def _rope(x, ct, s1, s2):
    up = pltpu.roll(x, HEAD_PAD - ROPE_HALF, 1)
    dn = pltpu.roll(x, ROPE_HALF, 1)
    return x * ct + up * s1 + dn * s2


def _const_spec(shape):
    nd = len(shape)
    return pl.BlockSpec(shape, lambda *_: (0,) * nd, pipeline_mode=pl.Buffered(1))


def _acc_spec(shape):
    nd = len(shape)
    return pl.BlockSpec(shape, lambda *_: (0,) * nd)


def _params(n_axes):
    return pltpu.CompilerParams(dimension_semantics=("arbitrary",) * n_axes, vmem_limit_bytes=VMEM_LIMIT)


def _place():
    x, y, c = lax.axis_index("x"), lax.axis_index("y"), lax.axis_index("c")
    return x, y, c, 4 * x + 2 * y + c


def _peer(x, y, c, mask):
    px = 1 - x if mask & 4 else x
    py = 1 - y if mask & 2 else y
    pc = 1 - c if mask & 1 else c
    return (px, py, pc), 4 * px + 2 * py + pc


ANY_SPEC = pl.BlockSpec(memory_space=pl.ANY)


def _comm_sems(n):
    return [pltpu.SemaphoreType.DMA((n, N_DEV - 1)), pltpu.SemaphoreType.DMA((n, N_DEV - 1)), pltpu.SemaphoreType.DMA((n,))]


def _gather_copies(ins, outs, sems):
    send_sems, recv_sems, local_sems = sems
    x, y, c, me = _place()
    starts, waits = [], []
    for w in range(len(ins)):
        mine = pltpu.make_async_copy(ins[w], outs[w].at[me], local_sems.at[w])
        starts.append(mine)
        waits.append(mine)
        for mask in range(1, N_DEV):
            peer, peer_idx = _peer(x, y, c, mask)
            starts.append(pltpu.make_async_remote_copy(
                src_ref=ins[w], dst_ref=outs[w].at[me], send_sem=send_sems.at[w, mask - 1],
                recv_sem=recv_sems.at[w, mask - 1], device_id=peer, device_id_type=MESH))
            waits.append(pltpu.make_async_remote_copy(
                src_ref=ins[w], dst_ref=outs[w].at[peer_idx], send_sem=send_sems.at[w, mask - 1],
                recv_sem=recv_sems.at[w, mask - 1], device_id=peer, device_id_type=MESH))
    return starts, waits


def _scatter_copies(ins, outs, sems):
    send_sems, recv_sems, local_sems = sems
    x, y, c, me = _place()
    copies = []
    for w in range(len(ins)):
        copies.append(pltpu.make_async_copy(ins[w].at[me], outs[w].at[0], local_sems.at[w]))
        for mask in range(1, N_DEV):
            peer, peer_idx = _peer(x, y, c, mask)
            copies.append(pltpu.make_async_remote_copy(
                src_ref=ins[w].at[peer_idx], dst_ref=outs[w].at[mask], send_sem=send_sems.at[w, mask - 1],
                recv_sem=recv_sems.at[w, mask - 1], device_id=peer, device_id_type=MESH))
    return copies, copies


def _stacked(arrays):
    return [jax.ShapeDtypeStruct((N_DEV,) + a.shape, a.dtype) for a in arrays]


def _all_gather(shards):
    n = len(shards)

    def body(*refs):
        ins, outs = refs[:n], refs[n:2 * n]
        send_sems, recv_sems, local_sems = refs[2 * n:]
        x, y, c, me = _place()
        sibling = (x, y, 1 - c)
        chips = [(1 - x, y), (x, 1 - y), (1 - x, 1 - y)]

        def copy(w, k, block, to, src=None):
            idx = 4 * block[0] + 2 * block[1] + block[2]
            return pltpu.make_async_remote_copy(
                src_ref=outs[w].at[idx] if src is None else src, dst_ref=outs[w].at[idx],
                send_sem=send_sems.at[w, k], recv_sem=recv_sems.at[w, k], device_id=to, device_id_type=MESH)

        local, sent = [], []
        for w in range(n):
            mine = pltpu.make_async_copy(ins[w], outs[w].at[me], local_sems.at[w])
            mine.start()
            local.append(mine)
            first = [copy(w, 0, (x, y, c), sibling, src=ins[w])]
            first += [copy(w, 1 + j, (x, y, c), (*chip, c), src=ins[w]) for j, chip in enumerate(chips)]
            for cp in first:
                cp.start()
            sent += first
        for w in range(n):
            for j, chip in enumerate(chips):
                copy(w, 1 + j, (*chip, c), (x, y, c)).wait_recv()
                onward = copy(w, 4 + j, (*chip, c), sibling)
                onward.start()
                sent.append(onward)
        for w in range(n):
            copy(w, 0, sibling, (x, y, c)).wait_recv()
            for j, chip in enumerate(chips):
                copy(w, 4 + j, (*chip, 1 - c), (x, y, c)).wait_recv()
        for cp in sent:
            cp.wait_send()
        for cp in local:
            cp.wait()

    return pl.pallas_call(
        body, name="weight_all_gather", out_shape=_stacked(shards),
        in_specs=[ANY_SPEC] * n, out_specs=[ANY_SPEC] * n, scratch_shapes=_comm_sems(n),
    )(*shards)


def _conv_fwd(x, a_norm, w_in, conv_w, w_out, n_seq, seq, later_shards):
    tm = ROW_TILE
    nt = seq // tm
    n_col = w_in.shape[2]
    n_later = len(later_shards)

    def body(x_ref, an_ref, win_ref, cw_ref, wout_ref, *rest):
        shard_refs, rest = rest[:n_later], rest[n_later:]
        x1_ref, proj_ref, conv_ref = rest[:3]
        stack_refs, rest = rest[3:3 + n_later], rest[3 + n_later:]
        prev_ref, sems = rest[0], rest[1:]
        step = pl.program_id(0) * nt + pl.program_id(1)

        @pl.when(step == 0)
        def _():
            for cp in _gather_copies(shard_refs, stack_refs, sems)[0]:
                cp.start()

        @pl.when(pl.program_id(1) == 0)
        def _():
            prev_ref[...] = jnp.zeros_like(prev_ref)

        xv = x_ref[...]
        h = (xv * _rstd(xv) * an_ref[...]).astype(BF16)
        for d in range(N_DEV):
            proj_ref[:, d * n_col:(d + 1) * n_col] = jnp.dot(h, win_ref[d], preferred_element_type=F32)
        b = proj_ref[:, 0:D_MODEL]
        v = proj_ref[:, D_MODEL:2 * D_MODEL] * proj_ref[:, 2 * D_MODEL:3 * D_MODEL]
        g = proj_ref[:, 3 * D_MODEL:4 * D_MODEL]
        w0, w1, w2 = cw_ref[0:1, :], cw_ref[1:2, :], cw_ref[2:3, :]
        conv_ref[...] = w0 * pltpu.roll(v, 2, 0) + w1 * pltpu.roll(v, 1, 0) + w2 * v
        rows = lax.broadcasted_iota(jnp.int32, (8, D_MODEL), 0)
        p8, v8 = prev_ref[...], v[0:8]
        back1 = jnp.where(rows < 1, pltpu.roll(p8, 1, 0), pltpu.roll(v8, 1, 0))
        back2 = jnp.where(rows < 2, pltpu.roll(p8, 2, 0), pltpu.roll(v8, 2, 0))
        conv_ref[0:8, :] = w0 * back2 + w1 * back1 + w2 * v8
        prev_ref[...] = v[tm - 8:tm]
        silu, _ = _silu_parts(g)
        yv = silu * b * conv_ref[...]
        x1_ref[...] = xv + _dot(yv, wout_ref[...])

        @pl.when(step == n_seq * nt - 1)
        def _():
            for cp in _gather_copies(shard_refs, stack_refs, sems)[1]:
                cp.wait()

    tok = lambda width: pl.BlockSpec((tm, width), lambda s, i: (s * nt + i, 0))
    t = n_seq * seq
    outs = pl.pallas_call(
        body, name="conv_mixer_fwd", grid=(n_seq, nt),
        in_specs=[tok(D_MODEL), _const_spec((1, D_MODEL)), _const_spec(w_in.shape), _const_spec((3, D_MODEL)),
                  _const_spec(w_out.shape)] + [ANY_SPEC] * n_later,
        out_specs=[tok(D_MODEL), tok(4 * D_MODEL), tok(D_MODEL)] + [ANY_SPEC] * n_later,
        out_shape=[jax.ShapeDtypeStruct((t, D_MODEL), F32), jax.ShapeDtypeStruct((t, 4 * D_MODEL), F32),
                   jax.ShapeDtypeStruct((t, D_MODEL), F32)] + _stacked(later_shards),
        scratch_shapes=[pltpu.VMEM((8, D_MODEL), F32)] + _comm_sems(n_later),
        compiler_params=_params(2),
    )(x, a_norm, w_in, conv_w, w_out, *later_shards)
    return outs[0], outs[1], outs[2], outs[3:]


def _conv_bwd(dx1, x, proj, conv, a_norm, w_in, conv_w, w_out, n_seq, seq, ready_stacks):
    tm = ROW_TILE
    nt = seq // tm
    n_col = w_in.shape[2]
    n_ready = len(ready_stacks)

    def body(dx1_ref, x_ref, proj_ref, conv_ref, an_ref, win_ref, cw_ref, wout_ref, *rest):
        ready_refs, rest = rest[:n_ready], rest[n_ready:]
        dx_ref, h_ref, dproj_ref, dwout_ref, dan_ref, dcw_ref = rest[:6]
        part_refs, rest = rest[6:6 + n_ready], rest[6 + n_ready:]
        next_ref, d1_ref, d2_ref = rest[:3]
        sems = rest[3:]
        step = pl.program_id(0) * nt + pl.program_id(1)
        first = step == 0

        @pl.when(first)
        def _():
            for cp in _scatter_copies(ready_refs, part_refs, sems)[0]:
                cp.start()
            dwout_ref[...] = jnp.zeros_like(dwout_ref)
            dan_ref[...] = jnp.zeros_like(dan_ref)
            dcw_ref[...] = jnp.zeros_like(dcw_ref)

        @pl.when(pl.program_id(1) == 0)
        def _():
            next_ref[...] = jnp.zeros_like(next_ref)

        dx1v = dx1_ref[...]
        dy = _dot_nt(dx1v, wout_ref[...])
        b = proj_ref[:, 0:D_MODEL]
        cc = proj_ref[:, D_MODEL:2 * D_MODEL]
        u = proj_ref[:, 2 * D_MODEL:3 * D_MODEL]
        g = proj_ref[:, 3 * D_MODEL:4 * D_MODEL]
        cv = conv_ref[...]
        silu, dsilu = _silu_parts(g)
        dwout_ref[...] += _dot_tn(silu * b * cv, dx1v)
        dproj_ref[:, 3 * D_MODEL:4 * D_MODEL] = (dy * b * cv * dsilu).astype(BF16)
        dproj_ref[:, 0:D_MODEL] = (dy * silu * cv).astype(BF16)
        dconv = dy * silu * b
        d1_ref[...] = pltpu.roll(dconv, tm - 1, 0)
        d2_ref[...] = pltpu.roll(dconv, tm - 2, 0)
        rows = lax.broadcasted_iota(jnp.int32, (8, D_MODEL), 0)
        n8, c8 = next_ref[...], dconv[tm - 8:tm]
        d1_ref[tm - 8:tm, :] = jnp.where(rows >= 7, pltpu.roll(n8, 7, 0), pltpu.roll(c8, 7, 0))
        d2_ref[tm - 8:tm, :] = jnp.where(rows >= 6, pltpu.roll(n8, 6, 0), pltpu.roll(c8, 6, 0))
        next_ref[...] = dconv[0:8]
        d1, d2 = d1_ref[...], d2_ref[...]
        v = cc * u
        dcw_ref[0:1, :] += jnp.sum(d2 * v, axis=0, keepdims=True)
        dcw_ref[1:2, :] += jnp.sum(d1 * v, axis=0, keepdims=True)
        dcw_ref[2:3, :] += jnp.sum(dconv * v, axis=0, keepdims=True)
        dv = cw_ref[0:1, :] * d2 + cw_ref[1:2, :] * d1 + cw_ref[2:3, :] * dconv
        dproj_ref[:, D_MODEL:2 * D_MODEL] = (dv * u).astype(BF16)
        dproj_ref[:, 2 * D_MODEL:3 * D_MODEL] = (dv * cc).astype(BF16)
        dh = jnp.zeros((tm, D_MODEL), F32)
        for d in range(N_DEV):
            dh += lax.dot_general(dproj_ref[:, d * n_col:(d + 1) * n_col], win_ref[d], NT, preferred_element_type=F32)
        xv = x_ref[...]
        r = _rstd(xv)
        xh = xv * r
        h_ref[...] = (xh * an_ref[...]).astype(BF16)
        dan_ref[...] += jnp.sum(dh * xh, axis=0, keepdims=True)
        dx_ref[...] = dx1v + _norm_bwd(dh * an_ref[...], xh, r)

        @pl.when(step == n_seq * nt - 1)
        def _():
            for cp in _scatter_copies(ready_refs, part_refs, sems)[1]:
                cp.wait()

    tok = lambda width: pl.BlockSpec((tm, width), lambda s, i: (s * nt + nt - 1 - i, 0))
    t = n_seq * seq
    outs = pl.pallas_call(
        body, name="conv_mixer_bwd", grid=(n_seq, nt),
        in_specs=[tok(D_MODEL), tok(D_MODEL), tok(4 * D_MODEL), tok(D_MODEL), _const_spec((1, D_MODEL)),
                  _const_spec(w_in.shape), _const_spec((3, D_MODEL)), _const_spec(w_out.shape)] + [ANY_SPEC] * n_ready,
        out_specs=[tok(D_MODEL), tok(D_MODEL), tok(4 * D_MODEL), _acc_spec((D_MODEL, D_MODEL)),
                   _acc_spec((1, D_MODEL)), _acc_spec((8, D_MODEL))] + [ANY_SPEC] * n_ready,
        out_shape=[jax.ShapeDtypeStruct((t, D_MODEL), F32), jax.ShapeDtypeStruct((t, D_MODEL), BF16),
                   jax.ShapeDtypeStruct((t, 4 * D_MODEL), BF16), jax.ShapeDtypeStruct((D_MODEL, D_MODEL), F32),
                   jax.ShapeDtypeStruct((1, D_MODEL), F32), jax.ShapeDtypeStruct((8, D_MODEL), F32)]
        + [jax.ShapeDtypeStruct(a.shape, a.dtype) for a in ready_stacks],
        scratch_shapes=[pltpu.VMEM((8, D_MODEL), F32), pltpu.VMEM((tm, D_MODEL), F32), pltpu.VMEM((tm, D_MODEL), F32)]
        + _comm_sems(n_ready),
        compiler_params=_params(2),
    )(dx1, x, proj, conv, a_norm, w_in, conv_w, w_out, *ready_stacks)
    return outs[:6], outs[6:]


def _w_in_grad_exchange(a, b, n_col, me, ready_stacks, small):
    t, r = a.shape
    bt = 1024
    nk = t // bt
    n_ready = len(ready_stacks)
    last_remote = N_DEV - 2

    def body(me_ref, a_ref, b_ref, *rest):
        ready_refs, small_ref, rest = rest[:n_ready], rest[n_ready], rest[n_ready + 1:]
        own_ref, got_ref, rest = rest[0], rest[1], rest[2:]
        part_refs, small_all_ref, rest = rest[:n_ready], rest[n_ready], rest[n_ready + 1:]
        acc_ref, stage_ref, shard_send, shard_recv = rest[:4]
        scatter_sems, gather_sems = rest[4:7], rest[7:10]
        s, k = pl.program_id(0), pl.program_id(1)
        x, y, c, me = _place()

        @pl.when(jnp.logical_and(s == 0, k == 0))
        def _():
            for cp in _scatter_copies(ready_refs, part_refs, scatter_sems)[0]:
                cp.start()
            for cp in _gather_copies([small_ref], [small_all_ref], gather_sems)[0]:
                cp.start()

        @pl.when(k == 0)
        def _():
            acc_ref[...] = jnp.zeros_like(acc_ref)

        acc_ref[...] += lax.dot_general(a_ref[...], b_ref[...], TN, preferred_element_type=F32)

        def shard_copy(step):
            owner = (me + 1 + step) % N_DEV
            slot = jnp.bitwise_xor(me, owner) - 1
            return pltpu.make_async_remote_copy(
                src_ref=stage_ref.at[step % 2], dst_ref=got_ref.at[slot], send_sem=shard_send.at[step],
                recv_sem=shard_recv.at[slot], device_id=(owner // 4, (owner // 2) % 2, owner % 2), device_id_type=MESH)

        for step in range(last_remote + 1):
            @pl.when(jnp.logical_and(s == step, k == nk - 1))
            def _(step=step):
                if step >= 2:
                    shard_copy(step - 2).wait_send()
                stage_ref[step % 2] = acc_ref[...].astype(BF16)
                shard_copy(step).start()

        @pl.when(jnp.logical_and(s == N_DEV - 1, k == nk - 1))
        def _():
            own_ref[...] = acc_ref[...]
            shard_copy(last_remote - 1).wait_send()
            shard_copy(last_remote).wait_send()
            for slot in range(N_DEV - 1):
                pltpu.make_async_remote_copy(
                    src_ref=stage_ref.at[0], dst_ref=got_ref.at[slot], send_sem=shard_send.at[0],
                    recv_sem=shard_recv.at[slot], device_id=(x, y, c), device_id_type=MESH).wait_recv()
            for cp in _scatter_copies(ready_refs, part_refs, scatter_sems)[1]:
                cp.wait()
            for cp in _gather_copies([small_ref], [small_all_ref], gather_sems)[1]:
                cp.wait()

    grid_spec = pltpu.PrefetchScalarGridSpec(
        num_scalar_prefetch=1, grid=(N_DEV, nk),
        in_specs=[pl.BlockSpec((bt, r), lambda s, k, me: (k, 0)),
                  pl.BlockSpec((bt, n_col), lambda s, k, me: (k, (me[0] + 1 + s) % N_DEV))] + [ANY_SPEC] * (n_ready + 1),
        out_specs=[pl.BlockSpec((r, n_col), lambda s, k, me: (0, 0)), ANY_SPEC] + [ANY_SPEC] * (n_ready + 1),
        scratch_shapes=[pltpu.VMEM((r, n_col), F32), pltpu.VMEM((2, r, n_col), BF16),
                        pltpu.SemaphoreType.DMA((N_DEV - 1,)), pltpu.SemaphoreType.DMA((N_DEV - 1,))]
        + _comm_sems(n_ready) + _comm_sems(1))
    outs = pl.pallas_call(
        body, name="w_in_grad_exchange", grid_spec=grid_spec,
        out_shape=[jax.ShapeDtypeStruct((r, n_col), F32), jax.ShapeDtypeStruct((N_DEV - 1, r, n_col), BF16)]
        + [jax.ShapeDtypeStruct(p.shape, p.dtype) for p in ready_stacks] + _stacked([small]),
        compiler_params=_params(2),
    )(me, a, b, *ready_stacks, small)
    return outs[0], outs[1], outs[2:2 + n_ready], outs[2 + n_ready]


def _proj_fwd(x1, pos, rc, kv_norm, w_dkv, ckv_norm, w_ukv, b_norm, b_w_in, q_norm, w_uq):
    t = x1.shape[0]
    tm = LATENT_ROW_TILE

    def body(x1_ref, pos_ref, rc_ref, kvn_ref, wdkv_ref, ckvn_ref, wukv_ref, bn_ref, bwin_ref, qn_ref, wuq_ref,
             ckr_ref, cq_ref, gb_ref, q_ref, k_ref, v_ref):
        ct, s1, s2 = _rope_tables(pos_ref[...], rc_ref[...])
        xv = x1_ref[...]
        xh = xv * _rstd(xv)
        ckr = _dot(xh * kvn_ref[...], wdkv_ref[...])
        ckr_ref[...] = ckr
        ckv = ckr[:, 0:KV_RANK]
        kv = _dot(ckv * _rstd(ckv) * ckvn_ref[...], wukv_ref[...])
        k_rope = _rope(ckr[:, KV_RANK:CKR_PAD], ct, s1, s2)
        lane = lax.broadcasted_iota(jnp.int32, (tm, HEAD_PAD), 1)
        low = lane < QK_NOPE
        for h in range(N_HEADS):
            kv_h = kv[:, h * HEAD_PAD:(h + 1) * HEAD_PAD]
            k_ref[:, h * HEAD_PAD:(h + 1) * HEAD_PAD] = jnp.where(low, kv_h, k_rope).astype(BF16)
            v_ref[:, h * HEAD_PAD:(h + 1) * HEAD_PAD] = jnp.where(low, pltpu.roll(kv_h, V_HEAD, 1), 1.0).astype(BF16)
        pb = _dot(xh * bn_ref[...], bwin_ref[...])
        cq = pb[:, 0:Q_RANK]
        cq_ref[...] = cq
        gb_ref[...] = pb[:, Q_RANK:Q_RANK + E_B]
        q = _dot(cq * _rstd(cq) * qn_ref[...], wuq_ref[...])
        for h in range(N_HEADS):
            q_ref[:, h * HEAD_PAD:(h + 1) * HEAD_PAD] = _rope(
                q[:, h * HEAD_PAD:(h + 1) * HEAD_PAD], ct, s1, s2).astype(BF16)

    tok = lambda width: pl.BlockSpec((tm, width), lambda i: (i, 0))
    weights = [kv_norm, w_dkv, ckv_norm, w_ukv, b_norm, b_w_in, q_norm, w_uq]
    wide = N_HEADS * HEAD_PAD
    return pl.pallas_call(
        body, name="latent_proj_fwd", grid=(t // tm,),
        in_specs=[tok(D_MODEL), tok(1), _const_spec(rc.shape)] + [_const_spec(w.shape) for w in weights],
        out_specs=[tok(CKR_PAD), tok(Q_RANK), tok(E_B), tok(wide), tok(wide), tok(wide)],
        out_shape=[jax.ShapeDtypeStruct((t, CKR_PAD), F32), jax.ShapeDtypeStruct((t, Q_RANK), F32),
                   jax.ShapeDtypeStruct((t, E_B), F32), jax.ShapeDtypeStruct((t, wide), BF16),
                   jax.ShapeDtypeStruct((t, wide), BF16), jax.ShapeDtypeStruct((t, wide), BF16)],
        compiler_params=_params(1),
    )(x1, pos, rc, *weights)


def _proj_bwd(dq, dk, dv, dgb, cq, ckr, x1, dx2, pos, rc, kv_norm, w_dkv, ckv_norm, w_ukv, b_norm, b_w_in, q_norm, w_uq):
    t = x1.shape[0]
    tm = LATENT_ROW_TILE
    wide = N_HEADS * HEAD_PAD

    def body(dq_ref, dk_ref, dv_ref, dgb_ref, cq_ref, ckr_ref, x1_ref, dx2_ref, pos_ref, rc_ref,
             kvn_ref, wdkv_ref, ckvn_ref, wukv_ref, bn_ref, bwin_ref, qn_ref, wuq_ref,
             dx1_ref, dwuq_ref, dwbin_ref, dwukv_ref, dwdkv_ref, dqn_ref, dbn_ref, dckvn_ref, dkvn_ref, dqu_ref, dkv_ref):
        @pl.when(pl.program_id(0) == 0)
        def _():
            for ref in (dwuq_ref, dwbin_ref, dwukv_ref, dwdkv_ref, dqn_ref, dbn_ref, dckvn_ref, dkvn_ref):
                ref[...] = jnp.zeros_like(ref)

        ct, s1, s2 = _rope_tables(pos_ref[...], rc_ref[...])
        lane = lax.broadcasted_iota(jnp.int32, (tm, HEAD_PAD), 1)
        low = lane < QK_NOPE
        for h in range(N_HEADS):
            dqu_ref[:, h * HEAD_PAD:(h + 1) * HEAD_PAD] = _rope(
                dq_ref[:, h * HEAD_PAD:(h + 1) * HEAD_PAD], ct, -s1, -s2).astype(BF16)
        cq = cq_ref[...]
        rq = _rstd(cq)
        cqh = cq * rq
        cqn = (cqh * qn_ref[...]).astype(BF16)
        dwuq = lax.dot_general(cqn, dqu_ref[...], TN, preferred_element_type=F32)
        for h in range(N_HEADS):
            dwuq_ref[h] += dwuq[:, h * HEAD_PAD:(h + 1) * HEAD_PAD]
        dcqn = lax.dot_general(dqu_ref[...], wuq_ref[...], NT, preferred_element_type=F32)
        dqn_ref[...] += jnp.sum(dcqn * cqh, axis=0, keepdims=True)
        dcq = _norm_bwd(dcqn * qn_ref[...], cqh, rq)
        dpb = jnp.concatenate([dcq, dgb_ref[...]], axis=1).astype(BF16)
        xv = x1_ref[...]
        r = _rstd(xv)
        xh = xv * r
        dwbin_ref[...] += _dot_tn(xh * bn_ref[...], dpb)
        dh3 = lax.dot_general(dpb, bwin_ref[...], NT, preferred_element_type=F32)
        dk_rope = jnp.zeros((tm, HEAD_PAD), F32)
        for h in range(N_HEADS):
            dk_h = dk_ref[:, h * HEAD_PAD:(h + 1) * HEAD_PAD]
            dv_h = dv_ref[:, h * HEAD_PAD:(h + 1) * HEAD_PAD]
            dkv_ref[:, h * HEAD_PAD:(h + 1) * HEAD_PAD] = jnp.where(low, dk_h, pltpu.roll(dv_h, V_HEAD, 1)).astype(BF16)
            dk_rope += dk_h
        rope_lanes = jnp.logical_and(lane >= ROPE_LO, lane < ROPE_LO + QK_ROPE)
        dk_rope = jnp.where(rope_lanes, _rope(dk_rope, ct, -s1, -s2), 0.0)
        ckv = ckr_ref[:, 0:KV_RANK]
        rk = _rstd(ckv)
        ckh = ckv * rk
        ckvn = (ckh * ckvn_ref[...]).astype(BF16)
        dwukv = lax.dot_general(ckvn, dkv_ref[...], TN, preferred_element_type=F32)
        for h in range(N_HEADS):
            dwukv_ref[h] += dwukv[:, h * HEAD_PAD:(h + 1) * HEAD_PAD]
        dckvn = lax.dot_general(dkv_ref[...], wukv_ref[...], NT, preferred_element_type=F32)
        dckvn_ref[...] += jnp.sum(dckvn * ckh, axis=0, keepdims=True)
        dckv = _norm_bwd(dckvn * ckvn_ref[...], ckh, rk)
        dckr = jnp.concatenate([dckv, dk_rope], axis=1).astype(BF16)
        dwdkv_ref[...] += _dot_tn(xh * kvn_ref[...], dckr)
        dh2 = lax.dot_general(dckr, wdkv_ref[...], NT, preferred_element_type=F32)
        dkvn_ref[...] += jnp.sum(dh2 * xh, axis=0, keepdims=True)
        dbn_ref[...] += jnp.sum(dh3 * xh, axis=0, keepdims=True)
        dx1_ref[...] = dx2_ref[...] + _norm_bwd(dh2 * kvn_ref[...] + dh3 * bn_ref[...], xh, r)

    tok = lambda width: pl.BlockSpec((tm, width), lambda i: (i, 0))
    weights = [kv_norm, w_dkv, ckv_norm, w_ukv, b_norm, b_w_in, q_norm, w_uq]
    acc_shapes = [(N_HEADS, Q_RANK, HEAD_PAD), (D_MODEL, Q_RANK + E_B), (N_HEADS, KV_RANK, HEAD_PAD), (D_MODEL, CKR_PAD),
                  (1, Q_RANK), (1, D_MODEL), (1, KV_RANK), (1, D_MODEL)]
    return pl.pallas_call(
        body, name="latent_proj_bwd", grid=(t // tm,),
        in_specs=[tok(wide), tok(wide), tok(wide), tok(E_B), tok(Q_RANK), tok(CKR_PAD), tok(D_MODEL), tok(D_MODEL), tok(1),
                  _const_spec(rc.shape)] + [_const_spec(w.shape) for w in weights],
        out_specs=[tok(D_MODEL)] + [_acc_spec(s) for s in acc_shapes],
        out_shape=[jax.ShapeDtypeStruct((t, D_MODEL), F32)] + [jax.ShapeDtypeStruct(s, F32) for s in acc_shapes],
        scratch_shapes=[pltpu.VMEM((tm, wide), BF16), pltpu.VMEM((tm, wide), BF16)],
        compiler_params=_params(1),
    )(dq, dk, dv, dgb, cq, ckr, x1, dx2, pos, rc, *weights)


def _attn_fwd(q, k, v, n_seq, seq):
    tq, tk = ATTN_FWD_TILES
    ratio = tq // tk
    nq = seq // tq
    pair = 2 * HEAD_PAD

    def body(q_ref, k_ref, v_ref, o_ref, lr_ref):
        i = pl.program_id(2)
        row = lax.broadcasted_iota(jnp.int32, (tq, tk), 0)
        col = lax.broadcasted_iota(jnp.int32, (tq, tk), 1)
        qs = [q_ref[:, hh * HEAD_PAD:(hh + 1) * HEAD_PAD] for hh in range(2)]

        def step(j, carry, shift):
            start = pl.multiple_of(j * tk, tk)
            out = []
            for hh in range(2):
                m, acc = carry[hh]
                s = lax.dot_general(qs[hh], k_ref[pl.ds(start, tk), hh * HEAD_PAD:(hh + 1) * HEAD_PAD], NT,
                                    preferred_element_type=F32) * (SOFTMAX_SCALE * LOG2_E)
                if shift is not None:
                    s = jnp.where(col + shift <= row, s, -jnp.inf)
                m_new = jnp.maximum(m, jnp.max(s, axis=-1, keepdims=True))
                p = jnp.exp2(s - m_new)
                acc = jnp.exp2(m - m_new) * acc + jnp.dot(
                    p.astype(BF16), v_ref[pl.ds(start, tk), hh * HEAD_PAD:(hh + 1) * HEAD_PAD], preferred_element_type=F32)
                out.append((m_new, acc))
            return tuple(out)

        one = (jnp.full((tq, 1), -jnp.inf, F32), jnp.zeros((tq, HEAD_PAD), F32))
        carry = lax.fori_loop(0, i * ratio, functools.partial(step, shift=None), (one, one))
        for d in range(ratio):
            carry = step(i * ratio + d, carry, d * tk)
        lane = lax.broadcasted_iota(jnp.int32, (tq, HEAD_PAD), 1)
        low = lane < V_HEAD
        halves = []
        for hh in range(2):
            m, acc = carry[hh]
            swapped = pltpu.roll(acc, V_HEAD, 1)
            halves.append(acc / swapped)
            lse = (m * LN_2) + jnp.log(jnp.where(low, swapped, acc))
            lr_ref[hh] = lse.T[0:8, :]
        o_ref[...] = jnp.where(low, halves[0], pltpu.roll(halves[1], V_HEAD, 1))

    t = n_seq * seq
    return pl.pallas_call(
        body, name="attention_fwd", grid=(n_seq, N_HEADS // 2, nq),
        in_specs=[pl.BlockSpec((tq, pair), lambda s, p, i: (s * nq + i, p)),
                  pl.BlockSpec((seq, pair), lambda s, p, i: (s, p)),
                  pl.BlockSpec((seq, pair), lambda s, p, i: (s, p))],
        out_specs=[pl.BlockSpec((tq, HEAD_PAD), lambda s, p, i: (s * nq + i, p)),
                   pl.BlockSpec((None, 2, 8, tq), lambda s, p, i: (s, p, 0, i))],
        out_shape=[jax.ShapeDtypeStruct((t, E_B), F32), jax.ShapeDtypeStruct((n_seq, N_HEADS, 8, seq), F32)],
        compiler_params=_params(3),
    )(q, k, v)


def _attn_bwd(q, k, v, do, lse_row, delta_row, n_seq, seq):
    tk, tq = ATTN_BWD_TILES
    ratio = tk // tq
    nk = seq // tk
    n_inner = seq // tq
    pair = 2 * HEAD_PAD

    def body(q_ref, k_ref, v_ref, do_ref, lr_ref, dr_ref, dk_ref, dv_ref, dq_ref):
        j = pl.program_id(2)

        @pl.when(j == 0)
        def _():
            dq_ref[...] = jnp.zeros_like(dq_ref)

        row = lax.broadcasted_iota(jnp.int32, (tk, tq), 0)
        col = lax.broadcasted_iota(jnp.int32, (tk, tq), 1)
        ks = [k_ref[:, hh * HEAD_PAD:(hh + 1) * HEAD_PAD] for hh in range(2)]
        vs = [v_ref[:, hh * HEAD_PAD:(hh + 1) * HEAD_PAD] for hh in range(2)]

        def step(i, carry, shift):
            start = pl.multiple_of(i * tq, tq)
            out = []
            for hh in range(2):
                dk_acc, dv_acc = carry[hh]
                qi = q_ref[pl.ds(start, tq), hh * HEAD_PAD:(hh + 1) * HEAD_PAD]
                doi = do_ref[pl.ds(start, tq), hh * HEAD_PAD:(hh + 1) * HEAD_PAD]
                st = lax.dot_general(ks[hh], qi, NT, preferred_element_type=F32) * SOFTMAX_SCALE
                pt = jnp.exp(st - lr_ref[hh, 0:1, pl.ds(start, tq)])
                if shift is not None:
                    pt = jnp.where(col + shift >= row, pt, 0.0)
                dv_acc = dv_acc + jnp.dot(pt.astype(BF16), doi, preferred_element_type=F32)
                dpt = lax.dot_general(vs[hh], doi, NT, preferred_element_type=F32)
                dst = (pt * (dpt - dr_ref[hh, 0:1, pl.ds(start, tq)]) * SOFTMAX_SCALE).astype(BF16)
                dk_acc = dk_acc + jnp.dot(dst, qi, preferred_element_type=F32)
                dq_ref[pl.ds(start, tq), hh * HEAD_PAD:(hh + 1) * HEAD_PAD] += lax.dot_general(
                    dst, ks[hh], TN, preferred_element_type=F32)
                out.append((dk_acc, dv_acc))
            return tuple(out)

        one = (jnp.zeros((tk, HEAD_PAD), F32), jnp.zeros((tk, HEAD_PAD), F32))
        carry = (one, one)
        for d in range(ratio):
            carry = step(j * ratio + d, carry, d * tq)
        carry = lax.fori_loop((j + 1) * ratio, n_inner, functools.partial(step, shift=None), carry)
        for hh in range(2):
            dk_ref[:, hh * HEAD_PAD:(hh + 1) * HEAD_PAD] = carry[hh][0]
            dv_ref[:, hh * HEAD_PAD:(hh + 1) * HEAD_PAD] = carry[hh][1]

    t = n_seq * seq
    wide = N_HEADS * HEAD_PAD
    return pl.pallas_call(
        body, name="attention_bwd", grid=(n_seq, N_HEADS // 2, nk),
        in_specs=[pl.BlockSpec((seq, pair), lambda s, p, j: (s, p)),
                  pl.BlockSpec((tk, pair), lambda s, p, j: (s * nk + j, p)),
                  pl.BlockSpec((tk, pair), lambda s, p, j: (s * nk + j, p)),
                  pl.BlockSpec((seq, pair), lambda s, p, j: (s, p)),
                  pl.BlockSpec((None, 2, 8, seq), lambda s, p, j: (s, p, 0, 0)),
                  pl.BlockSpec((None, 2, 8, seq), lambda s, p, j: (s, p, 0, 0))],
        out_specs=[pl.BlockSpec((tk, pair), lambda s, p, j: (s * nk + j, p)),
                   pl.BlockSpec((tk, pair), lambda s, p, j: (s * nk + j, p)),
                   pl.BlockSpec((seq, pair), lambda s, p, j: (s, p))],
        out_shape=[jax.ShapeDtypeStruct((t, wide), F32), jax.ShapeDtypeStruct((t, wide), F32),
                   jax.ShapeDtypeStruct((t, wide), F32)],
        compiler_params=_params(3),
    )(q, k, v, do, lse_row, delta_row)


def _head_tail(o, gb, x1, w_out, final_norm, target, n_seq, seq):
    tm = LATENT_ROW_TILE
    nt = seq // tm
    n_col = D_MODEL // N_DEV

    def body(o_ref, gb_ref, x1_ref, wout_ref, fn_ref, tgt_ref,
             dx2_ref, do_ref, dgb_ref, dr_ref, dwout_ref, dfn_ref, loss_ref):
        first = jnp.logical_and(pl.program_id(0) == 0, pl.program_id(1) == 0)

        @pl.when(first)
        def _():
            dwout_ref[...] = jnp.zeros_like(dwout_ref)
            dfn_ref[...] = jnp.zeros_like(dfn_ref)
            loss_ref[...] = jnp.zeros_like(loss_ref)

        ov, g = o_ref[...], gb_ref[...]
        silu, dsilu = _silu_parts(g)
        gated = (ov * silu).astype(BF16)
        x2 = x1_ref[...] + jnp.dot(gated, wout_ref[...], preferred_element_type=F32)
        r = _rstd(x2)
        xh = x2 * r
        err = xh * fn_ref[...] - tgt_ref[...]
        loss_ref[...] += 0.5 * jnp.sum(jnp.mean(err * err, axis=-1, keepdims=True), axis=0, keepdims=True)
        dy = err / D_MODEL
        dfn_ref[...] += jnp.sum(dy * xh, axis=0, keepdims=True)
        dx2 = _norm_bwd(dy * fn_ref[...], xh, r)
        dx2_ref[...] = dx2
        dx2b = dx2.astype(BF16)
        dw = lax.dot_general(gated, dx2b, TN, preferred_element_type=F32)
        for d in range(N_DEV):
            dwout_ref[d] += dw[:, d * n_col:(d + 1) * n_col]
        dgated = lax.dot_general(dx2b, wout_ref[...], NT, preferred_element_type=F32)
        do = dgated * silu
        dgb_ref[...] = dgated * ov * dsilu
        prod = do * ov
        lane = lax.broadcasted_iota(jnp.int32, (tm, HEAD_PAD), 1)
        low = lane < V_HEAD
        for p in range(N_HEADS // 2):
            blk = prod[:, p * HEAD_PAD:(p + 1) * HEAD_PAD]
            do_pair = do[:, p * HEAD_PAD:(p + 1) * HEAD_PAD]
            for hh in range(2):
                h = 2 * p + hh
                mine = do_pair if hh == 0 else pltpu.roll(do_pair, V_HEAD, 1)
                do_ref[:, h * HEAD_PAD:(h + 1) * HEAD_PAD] = jnp.where(low, mine, 0.0).astype(BF16)
                delta = jnp.sum(jnp.where(low if hh == 0 else ~low, blk, 0.0), axis=-1, keepdims=True)
                dr_ref[h] = jnp.broadcast_to(delta, (tm, HEAD_PAD)).T[0:8, :]

    tok = lambda width: pl.BlockSpec((tm, width), lambda s, i: (s * nt + i, 0))
    t = n_seq * seq
    return pl.pallas_call(
        body, name="head_tail", grid=(n_seq, nt),
        in_specs=[tok(E_B), tok(E_B), tok(D_MODEL), _const_spec(w_out.shape), _const_spec((1, D_MODEL)), tok(D_MODEL)],
        out_specs=[tok(D_MODEL), tok(N_HEADS * HEAD_PAD), tok(E_B),
                   pl.BlockSpec((None, N_HEADS, 8, tm), lambda s, i: (s, 0, 0, i)),
                   _acc_spec((N_DEV, E_B, n_col)), _acc_spec((1, D_MODEL)), _acc_spec((1, 1))],
        out_shape=[jax.ShapeDtypeStruct((t, D_MODEL), F32), jax.ShapeDtypeStruct((t, N_HEADS * HEAD_PAD), BF16),
                   jax.ShapeDtypeStruct((t, E_B), F32),
                   jax.ShapeDtypeStruct((n_seq, N_HEADS, 8, seq), F32), jax.ShapeDtypeStruct((N_DEV, E_B, n_col), F32),
                   jax.ShapeDtypeStruct((1, D_MODEL), F32), jax.ShapeDtypeStruct((1, 1), F32)],
        compiler_params=_params(2),
    )(o, gb, x1, w_out, final_norm, target)


def _adamw_math(w, g, m, v):
    m = ADAM_B1 * m + (1.0 - ADAM_B1) * g
    v = ADAM_B2 * v + (1.0 - ADAM_B2) * jnp.square(g)
    m_hat = m / (1.0 - ADAM_B1 ** ADAM_STEP)
    v_hat = v / (1.0 - ADAM_B2 ** ADAM_STEP)
    delta = -ADAM_LR * (m_hat / (jnp.sqrt(v_hat) + ADAM_EPS) + ADAM_WD * w)
    return delta, m, v


def _adamw_reduce(own, parts, w, m, v):
    rows, cols = w.shape
    br = 256 if rows % 256 == 0 else 128
    n_parts = parts.shape[0]

    def body(*refs):
        p_ref, w_ref, m_ref, v_ref, g_ref, d_ref, nm_ref, nv_ref = refs[-8:]
        g = p_ref[0].astype(F32) if own is None else refs[0][...] + p_ref[0].astype(F32)
        for k in range(1, n_parts):
            g = g + p_ref[k].astype(F32)
        g_ref[...] = g
        d_ref[...], nm_ref[...], nv_ref[...] = _adamw_math(w_ref[...], g, m_ref[...], v_ref[...])

    blk = pl.BlockSpec((br, cols), lambda i: (i, 0))
    first = [] if own is None else [own]
    return pl.pallas_call(
        body, name="adamw_reduce", grid=(rows // br,),
        in_specs=[blk] * len(first) + [pl.BlockSpec((n_parts, br, cols), lambda i: (0, i, 0)), blk, blk, blk],
        out_specs=[blk] * 4, out_shape=[jax.ShapeDtypeStruct((rows, cols), F32)] * 4,
        compiler_params=_params(1),
    )(*first, parts, w, m, v)


def _sum_parts(parts):
    def body(p_ref, o_ref):
        g = p_ref[0]
        for k in range(1, N_DEV):
            g = g + p_ref[k]
        o_ref[...] = g

    return pl.pallas_call(body, name="small_grad_sum", out_shape=jax.ShapeDtypeStruct(parts.shape[1:], F32))(parts)


def _adamw_small(gs, ws, ms, vs):
    n = len(gs)

    def body(*refs):
        ins, outs = refs[:4 * n], refs[4 * n:]
        for j in range(n):
            g_ref, w_ref, m_ref, v_ref = ins[j], ins[n + j], ins[2 * n + j], ins[3 * n + j]
            outs[3 * j][...], outs[3 * j + 1][...], outs[3 * j + 2][...] = _adamw_math(
                w_ref[...], g_ref[...], m_ref[...], v_ref[...])

    shapes = [jax.ShapeDtypeStruct(g.shape, F32) for g in gs for _ in range(3)]
    flat = pl.pallas_call(body, name="adamw_small", out_shape=shapes)(*gs, *ws, *ms, *vs)
    return [tuple(flat[3 * j:3 * j + 3]) for j in range(n)]


SMALL_ROWS = 8
LOSS_LANE = D_MODEL - 1


def _pack_small(g_fn, g_kvn, g_bn, g_an, g_ckvn, g_qn, g_conv, loss_part):
    def body(fn_ref, kvn_ref, bn_ref, an_ref, ckvn_ref, qn_ref, conv_ref, loss_ref, o_ref):
        o_ref[0:1, :] = fn_ref[...]
        o_ref[1:2, :] = kvn_ref[...]
        o_ref[2:3, :] = bn_ref[...]
        o_ref[3:4, :] = an_ref[...]
        lane = lax.broadcasted_iota(jnp.int32, (1, D_MODEL), 1)
        o_ref[4:5, :] = jnp.where(lane == LOSS_LANE, loss_ref[...], 0.0)
        o_ref[4:5, 0:KV_RANK] = ckvn_ref[...]
        o_ref[4:5, KV_RANK:KV_RANK + Q_RANK] = qn_ref[...]
        o_ref[5:8, :] = conv_ref[0:3, :]

    return pl.pallas_call(body, name="pack_small", out_shape=jax.ShapeDtypeStruct((SMALL_ROWS, D_MODEL), F32))(
        g_fn, g_kvn, g_bn, g_an, g_ckvn, g_qn, g_conv, loss_part)


def _pad_cols(a, width):
    return jnp.pad(a, ((0, 0), (0, width - a.shape[1])))


def _dkv_to_padded(a):
    r = a.shape[0]
    z = jnp.zeros((r, ROPE_LO), a.dtype)
    z2 = jnp.zeros((r, HEAD_PAD - ROPE_LO - QK_ROPE), a.dtype)
    return jnp.concatenate([a[:, :KV_RANK], z, a[:, KV_RANK:], z2], axis=1)


def _dkv_from_padded(a):
    return jnp.concatenate([a[:, :KV_RANK], a[:, KV_RANK + ROPE_LO:KV_RANK + ROPE_LO + QK_ROPE]], axis=1)


def _unstack_cols(a):
    return jnp.transpose(a, (1, 0, 2)).reshape(a.shape[1], N_DEV * a.shape[2])


def kernel(x, positions, a_norm, a_w_in, a_conv, a_w_out, kv_norm, w_dkv, ckv_norm, w_ukv, b_norm, b_w_in, b_q_norm, b_w_uq, b_w_out, final_norm, loss_target, m_a_norm, m_a_w_in, m_a_conv, m_a_w_out, m_kv_norm, m_w_dkv, m_ckv_norm, m_w_ukv, m_b_norm, m_b_w_in, m_b_q_norm, m_b_w_uq, m_b_w_out, m_final_norm, v_a_norm, v_a_w_in, v_a_conv, v_a_w_out, v_kv_norm, v_w_dkv, v_ckv_norm, v_w_ukv, v_b_norm, v_b_w_in, v_b_q_norm, v_b_w_uq, v_b_w_out, v_final_norm):
    n_seq, seq, _ = x.shape
    t = n_seq * seq
    me = 4 * lax.axis_index("x") + 2 * lax.axis_index("y") + lax.axis_index("c")

    big = {
        "a_w_in": (a_w_in[0], m_a_w_in[0], v_a_w_in[0]),
        "a_w_out": (a_w_out[0], m_a_w_out[0], v_a_w_out[0]),
        "w_dkv": tuple(_dkv_to_padded(a) for a in (w_dkv, m_w_dkv, v_w_dkv)),
        "w_ukv": (w_ukv, m_w_ukv, v_w_ukv),
        "b_w_in": (b_w_in[0], m_b_w_in[0], v_b_w_in[0]),
        "b_w_uq": tuple(_pad_cols(a[0], HEAD_PAD) for a in (b_w_uq, m_b_w_uq, v_b_w_uq)),
        "b_w_out": (b_w_out[0], m_b_w_out[0], v_b_w_out[0]),
    }
    names = list(big)
    first_names, later_names = names[:2], names[2:]
    gathered = _all_gather([big[n][0].astype(BF16) for n in first_names] + [a_norm, a_conv[0]])
    a_norm_f = gathered[2].reshape(1, D_MODEL)
    a_conv_f = _unstack_cols(gathered[3])
    w_a_in = gathered[0]
    w_a_out = gathered[1].reshape(D_MODEL, D_MODEL)

    x2d = x.reshape(t, D_MODEL)
    tgt = loss_target.reshape(t, D_MODEL)
    pos = positions.astype(F32).reshape(t, 1)
    rc = _rope_consts()
    kvn, ckvn, bn, qn, fn = kv_norm.reshape(1, -1), ckv_norm.reshape(1, -1), b_norm, b_q_norm, final_norm.reshape(1, -1)

    x1, proj, conv, later = _conv_fwd(x2d, a_norm_f, w_a_in, a_conv_f, w_a_out, n_seq, seq,
                                      [big[n][0].astype(BF16) for n in later_names])
    full = dict(zip(later_names, later))
    w_dkv_f = full["w_dkv"].reshape(D_MODEL, CKR_PAD)
    w_ukv_f = _unstack_cols(full["w_ukv"])
    w_b_in = full["b_w_in"].reshape(D_MODEL, Q_RANK + E_B)
    w_uq_f = _unstack_cols(full["b_w_uq"])
    w_b_out = _unstack_cols(full["b_w_out"])
    layer_b = (kvn, w_dkv_f, ckvn, w_ukv_f, bn, w_b_in, qn, w_uq_f)
    ckr, cq, gb, q, k, v = _proj_fwd(x1, pos, rc, *layer_b)
    o, lse_row = _attn_fwd(q, k, v, n_seq, seq)
    dx2, do, dgb, delta_row, g_b_out, g_fn, loss_part = _head_tail(o, gb, x1, w_b_out, fn, tgt, n_seq, seq)
    dk, dv, dq = _attn_bwd(q, k, v, do, lse_row, delta_row, n_seq, seq)
    dx1, g_uq, g_b_in, g_ukv, g_dkv, g_qn, g_bn, g_ckvn, g_kvn = _proj_bwd(
        dq, dk, dv, dgb, cq, ckr, x1, dx2, pos, rc, *layer_b)
    stacks = {
        "w_dkv": g_dkv.reshape(N_DEV, D_MODEL // N_DEV, CKR_PAD),
        "w_ukv": g_ukv,
        "b_w_in": g_b_in.reshape(N_DEV, D_MODEL // N_DEV, Q_RANK + E_B),
        "b_w_uq": g_uq,
        "b_w_out": g_b_out,
    }
    (dx, h_a, dproj, g_a_out, g_an, g_conv), later_parts = _conv_bwd(
        dx1, x2d, proj, conv, a_norm_f, w_a_in, a_conv_f, w_a_out, n_seq, seq, [stacks[n] for n in later_names])
    parts = dict(zip(later_names, later_parts))
    small = _pack_small(g_fn, g_kvn, g_bn, g_an, g_ckvn, g_qn, g_conv, loss_part)
    a_in_own, a_in_got, (parts["a_w_out"],), small_parts = _w_in_grad_exchange(
        h_a, dproj, D_MODEL * 4 // N_DEV, me.reshape(1).astype(jnp.int32),
        [g_a_out.reshape(N_DEV, D_MODEL // N_DEV, D_MODEL)], small)

    outs = {"a_w_in": _adamw_reduce(a_in_own, a_in_got, *big["a_w_in"])}
    for n in names[1:]:
        outs[n] = _adamw_reduce(None, parts[n], *big[n])
    outs["w_dkv"] = tuple(_dkv_from_padded(a) for a in outs["w_dkv"])
    outs["b_w_uq"] = tuple(a[:, :QK_NOPE + QK_ROPE] for a in outs["b_w_uq"])
    for n in ("a_w_in", "a_w_out", "b_w_in", "b_w_uq", "b_w_out"):
        outs[n] = tuple(a[None] for a in outs[n])

    total = _sum_parts(small_parts)
    loss = total[4, LOSS_LANE]
    shard = D_MODEL // N_DEV
    g_small = {
        "final_norm": total[0], "kv_norm": total[1], "b_norm": total[2:3],
        "a_norm": lax.dynamic_slice_in_dim(total[3:4], me * shard, shard, axis=1),
        "ckv_norm": total[4, 0:KV_RANK], "b_q_norm": total[4:5, KV_RANK:KV_RANK + Q_RANK],
        "a_conv": lax.dynamic_slice_in_dim(total[5:8], me * shard, shard, axis=1)[None],
    }
    small_state = {
        "final_norm": (final_norm, m_final_norm, v_final_norm), "kv_norm": (kv_norm, m_kv_norm, v_kv_norm),
        "b_norm": (b_norm, m_b_norm, v_b_norm), "a_norm": (a_norm, m_a_norm, v_a_norm),
        "ckv_norm": (ckv_norm, m_ckv_norm, v_ckv_norm), "b_q_norm": (b_q_norm, m_b_q_norm, v_b_q_norm),
        "a_conv": (a_conv, m_a_conv, v_a_conv),
    }
    small_names = list(g_small)
    as2d = lambda a: a.reshape(-1, a.shape[-1])
    upd = _adamw_small([as2d(g_small[n]) for n in small_names],
                       *[[as2d(small_state[n][j]) for n in small_names] for j in range(3)])
    for n, u in zip(small_names, upd):
        outs[n] = (g_small[n],) + tuple(a.reshape(g_small[n].shape) for a in u)

    order = ["a_norm", "a_w_in", "a_conv", "a_w_out", "kv_norm", "w_dkv", "ckv_norm", "w_ukv", "b_norm", "b_w_in",
             "b_q_norm", "b_w_uq", "b_w_out", "final_norm"]
    result = [loss, dx.reshape(n_seq, seq, D_MODEL)]
    for j in range(4):
        result += [outs[n][j] for n in order]
    return tuple(result)
```

```python
import functools
import math

import numpy as np
import jax
import jax.numpy as jnp
from jax import lax
from jax.experimental import pallas as pl
from jax.experimental.pallas import tpu as pltpu

F32 = jnp.float32
BF16 = jnp.bfloat16

D_MODEL = 1024
N_HEADS = 8
QK_NOPE = 64
QK_ROPE = 32
V_HEAD = 64
KV_RANK = 256
Q_RANK = 384
E_B = N_HEADS * V_HEAD
HEAD_PAD = 128
CKR_PAD = KV_RANK + HEAD_PAD
ROPE_LO = QK_NOPE
ROPE_HALF = QK_ROPE // 2
ROPE_THETA = 10000.0
SOFTMAX_SCALE = 1.0 / math.sqrt(QK_NOPE + QK_ROPE)
LOG2_E = math.log2(math.e)
LN_2 = math.log(2.0)
EPS = 1e-6
N_DEV = 8

ADAM_LR = 0.001
ADAM_B1 = 0.9
ADAM_B2 = 0.999
ADAM_EPS = 1e-08
ADAM_WD = 0.01
ADAM_STEP = 10

ROW_TILE = 256
LATENT_ROW_TILE = 512
ATTN_FWD_TILES = (512, 512)
ATTN_BWD_TILES = (512, 512)
VMEM_LIMIT = 56 * 1024 * 1024

MESH = pl.DeviceIdType.MESH
NT = (((1,), (1,)), ((), ()))
TN = (((0,), (0,)), ((), ()))


def _dot(a, b):
    return jnp.dot(a.astype(BF16), b.astype(BF16), preferred_element_type=F32)


def _dot_nt(a, b):
    return lax.dot_general(a.astype(BF16), b.astype(BF16), NT, preferred_element_type=F32)


def _dot_tn(a, b):
    return lax.dot_general(a.astype(BF16), b.astype(BF16), TN, preferred_element_type=F32)


def _rstd(x):
    return lax.rsqrt(jnp.mean(x * x, axis=-1, keepdims=True) + EPS)


def _norm_bwd(a, xh, r):
    return r * (a - xh * jnp.mean(a * xh, axis=-1, keepdims=True))


def _silu_parts(g):
    sg = jax.nn.sigmoid(g)
    return g * sg, sg * (1.0 + g * (1.0 - sg))


def _rope_consts():
    inv = (ROPE_THETA ** (-np.arange(0, QK_ROPE, 2, dtype=np.float32) / QK_ROPE)).astype(np.float32)
    t = np.zeros((8, HEAD_PAD), np.float32)
    t[0, ROPE_LO:ROPE_LO + ROPE_HALF] = inv
    t[0, ROPE_LO + ROPE_HALF:ROPE_LO + QK_ROPE] = inv
    t[1, ROPE_LO:ROPE_LO + ROPE_HALF] = -1.0
    t[2, ROPE_LO + ROPE_HALF:ROPE_LO + QK_ROPE] = 1.0
    return jnp.asarray(t)


def _rope_tables(pos, rc):
    ang = pos * rc[0:1, :]
    cosv = jnp.cos(ang)
    sinv = jnp.sin(ang)
    return cosv, sinv * rc[1:2, :], sinv * rc[2:3, :]


def _rope(x, ct, s1, s2):
    up = pltpu.roll(x, HEAD_PAD - ROPE_HALF, 1)
    dn = pltpu.roll(x, ROPE_HALF, 1)
    return x * ct + up * s1 + dn * s2


def _const_spec(shape):
    nd = len(shape)
    return pl.BlockSpec(shape, lambda *_: (0,) * nd, pipeline_mode=pl.Buffered(1))


def _acc_spec(shape):
    nd = len(shape)
    return pl.BlockSpec(shape, lambda *_: (0,) * nd)


def _params(n_axes):
    return pltpu.CompilerParams(dimension_semantics=("arbitrary",) * n_axes, vmem_limit_bytes=VMEM_LIMIT)


def _place():
    x, y, c = lax.axis_index("x"), lax.axis_index("y"), lax.axis_index("c")
    return x, y, c, 4 * x + 2 * y + c


def _peer(x, y, c, mask):
    px = 1 - x if mask & 4 else x
    py = 1 - y if mask & 2 else y
    pc = 1 - c if mask & 1 else c
    return (px, py, pc), 4 * px + 2 * py + pc


ANY_SPEC = pl.BlockSpec(memory_space=pl.ANY)


def _comm_sems(n):
    return [pltpu.SemaphoreType.DMA((n, N_DEV - 1)), pltpu.SemaphoreType.DMA((n, N_DEV - 1)), pltpu.SemaphoreType.DMA((n,))]


def _gather_copies(ins, outs, sems):
    send_sems, recv_sems, local_sems = sems
    x, y, c, me = _place()
    starts, waits = [], []
    for w in range(len(ins)):
        mine = pltpu.make_async_copy(ins[w], outs[w].at[me], local_sems.at[w])
        starts.append(mine)
        waits.append(mine)
        for mask in range(1, N_DEV):
            peer, peer_idx = _peer(x, y, c, mask)
            starts.append(pltpu.make_async_remote_copy(
                src_ref=ins[w], dst_ref=outs[w].at[me], send_sem=send_sems.at[w, mask - 1],
                recv_sem=recv_sems.at[w, mask - 1], device_id=peer, device_id_type=MESH))
            waits.append(pltpu.make_async_remote_copy(
                src_ref=ins[w], dst_ref=outs[w].at[peer_idx], send_sem=send_sems.at[w, mask - 1],
                recv_sem=recv_sems.at[w, mask - 1], device_id=peer, device_id_type=MESH))
    return starts, waits


def _scatter_copies(ins, outs, sems):
    send_sems, recv_sems, local_sems = sems
    x, y, c, me = _place()
    copies = []
    for w in range(len(ins)):
        copies.append(pltpu.make_async_copy(ins[w].at[me], outs[w].at[0], local_sems.at[w]))
        for mask in range(1, N_DEV):
            peer, peer_idx = _peer(x, y, c, mask)
            copies.append(pltpu.make_async_remote_copy(
                src_ref=ins[w].at[peer_idx], dst_ref=outs[w].at[mask], send_sem=send_sems.at[w, mask - 1],
                recv_sem=recv_sems.at[w, mask - 1], device_id=peer, device_id_type=MESH))
    return copies, copies


def _stacked(arrays):
    return [jax.ShapeDtypeStruct((N_DEV,) + a.shape, a.dtype) for a in arrays]


def _all_gather(shards):
    n = len(shards)

    def body(*refs):
        ins, outs = refs[:n], refs[n:2 * n]
        send_sems, recv_sems, local_sems = refs[2 * n:]
        x, y, c, me = _place()
        sibling = (x, y, 1 - c)
        chips = [(1 - x, y), (x, 1 - y), (1 - x, 1 - y)]

        def copy(w, k, block, to, src=None):
            idx = 4 * block[0] + 2 * block[1] + block[2]
            return pltpu.make_async_remote_copy(
                src_ref=outs[w].at[idx] if src is None else src, dst_ref=outs[w].at[idx],
                send_sem=send_sems.at[w, k], recv_sem=recv_sems.at[w, k], device_id=to, device_id_type=MESH)

        local, sent = [], []
        for w in range(n):
            mine = pltpu.make_async_copy(ins[w], outs[w].at[me], local_sems.at[w])
            mine.start()
            local.append(mine)
            first = [copy(w, 0, (x, y, c), sibling, src=ins[w])]
            first += [copy(w, 1 + j, (x, y, c), (*chip, c), src=ins[w]) for j, chip in enumerate(chips)]
            for cp in first:
                cp.start()
            sent += first
        for w in range(n):
            for j, chip in enumerate(chips):
                copy(w, 1 + j, (*chip, c), (x, y, c)).wait_recv()
                onward = copy(w, 4 + j, (*chip, c), sibling)
                onward.start()
                sent.append(onward)
        for w in range(n):
            copy(w, 0, sibling, (x, y, c)).wait_recv()
            for j, chip in enumerate(chips):
                copy(w, 4 + j, (*chip, 1 - c), (x, y, c)).wait_recv()
        for cp in sent:
            cp.wait_send()
        for cp in local:
            cp.wait()

    return pl.pallas_call(
        body, name="weight_all_gather", out_shape=_stacked(shards),
        in_specs=[ANY_SPEC] * n, out_specs=[ANY_SPEC] * n, scratch_shapes=_comm_sems(n),
    )(*shards)


def _conv_fwd(x, a_norm, w_in, conv_w, w_out, n_seq, seq, later_shards):
    tm = ROW_TILE
    nt = seq // tm
    n_col = w_in.shape[2]
    n_later = len(later_shards)

    def body(x_ref, an_ref, win_ref, cw_ref, wout_ref, *rest):
        shard_refs, rest = rest[:n_later], rest[n_later:]
        x1_ref, proj_ref, conv_ref = rest[:3]
        stack_refs, rest = rest[3:3 + n_later], rest[3 + n_later:]
        prev_ref, sems = rest[0], rest[1:]
        step = pl.program_id(0) * nt + pl.program_id(1)

        @pl.when(step == 0)
        def _():
            for cp in _gather_copies(shard_refs, stack_refs, sems)[0]:
                cp.start()

        @pl.when(pl.program_id(1) == 0)
        def _():
            prev_ref[...] = jnp.zeros_like(prev_ref)

        xv = x_ref[...]
        h = (xv * _rstd(xv) * an_ref[...]).astype(BF16)
        for d in range(N_DEV):
            proj_ref[:, d * n_col:(d + 1) * n_col] = jnp.dot(h, win_ref[d], preferred_element_type=F32)
        b = proj_ref[:, 0:D_MODEL]
        v = proj_ref[:, D_MODEL:2 * D_MODEL] * proj_ref[:, 2 * D_MODEL:3 * D_MODEL]
        g = proj_ref[:, 3 * D_MODEL:4 * D_MODEL]
        w0, w1, w2 = cw_ref[0:1, :], cw_ref[1:2, :], cw_ref[2:3, :]
        conv_ref[...] = w0 * pltpu.roll(v, 2, 0) + w1 * pltpu.roll(v, 1, 0) + w2 * v
        rows = lax.broadcasted_iota(jnp.int32, (8, D_MODEL), 0)
        p8, v8 = prev_ref[...], v[0:8]
        back1 = jnp.where(rows < 1, pltpu.roll(p8, 1, 0), pltpu.roll(v8, 1, 0))
        back2 = jnp.where(rows < 2, pltpu.roll(p8, 2, 0), pltpu.roll(v8, 2, 0))
        conv_ref[0:8, :] = w0 * back2 + w1 * back1 + w2 * v8
        prev_ref[...] = v[tm - 8:tm]
        silu, _ = _silu_parts(g)
        yv = silu * b * conv_ref[...]
        x1_ref[...] = xv + _dot(yv, wout_ref[...])

        @pl.when(step == n_seq * nt - 1)
        def _():
            for cp in _gather_copies(shard_refs, stack_refs, sems)[1]:
                cp.wait()

    tok = lambda width: pl.BlockSpec((tm, width), lambda s, i: (s * nt + i, 0))
    t = n_seq * seq
    outs = pl.pallas_call(
        body, name="conv_mixer_fwd", grid=(n_seq, nt),
        in_specs=[tok(D_MODEL), _const_spec((1, D_MODEL)), _const_spec(w_in.shape), _const_spec((3, D_MODEL)),
                  _const_spec(w_out.shape)] + [ANY_SPEC] * n_later,
        out_specs=[tok(D_MODEL), tok(4 * D_MODEL), tok(D_MODEL)] + [ANY_SPEC] * n_later,
        out_shape=[jax.ShapeDtypeStruct((t, D_MODEL), F32), jax.ShapeDtypeStruct((t, 4 * D_MODEL), F32),
                   jax.ShapeDtypeStruct((t, D_MODEL), F32)] + _stacked(later_shards),
        scratch_shapes=[pltpu.VMEM((8, D_MODEL), F32)] + _comm_sems(n_later),
        compiler_params=_params(2),
    )(x, a_norm, w_in, conv_w, w_out, *later_shards)
    return outs[0], outs[1], outs[2], outs[3:]


def _conv_bwd(dx1, x, proj, conv, a_norm, w_in, conv_w, w_out, n_seq, seq, ready_stacks):
    tm = ROW_TILE
    nt = seq // tm
    n_col = w_in.shape[2]
    n_ready = len(ready_stacks)

    def body(dx1_ref, x_ref, proj_ref, conv_ref, an_ref, win_ref, cw_ref, wout_ref, *rest):
        ready_refs, rest = rest[:n_ready], rest[n_ready:]
        dx_ref, h_ref, dproj_ref, dwout_ref, dan_ref, dcw_ref = rest[:6]
        part_refs, rest = rest[6:6 + n_ready], rest[6 + n_ready:]
        next_ref, d1_ref, d2_ref = rest[:3]
        sems = rest[3:]
        step = pl.program_id(0) * nt + pl.program_id(1)
        first = step == 0

        @pl.when(first)
        def _():
            for cp in _scatter_copies(ready_refs, part_refs, sems)[0]:
                cp.start()
            dwout_ref[...] = jnp.zeros_like(dwout_ref)
            dan_ref[...] = jnp.zeros_like(dan_ref)
            dcw_ref[...] = jnp.zeros_like(dcw_ref)

        @pl.when(pl.program_id(1) == 0)
        def _():
            next_ref[...] = jnp.zeros_like(next_ref)

        dx1v = dx1_ref[...]
        dy = _dot_nt(dx1v, wout_ref[...])
        b = proj_ref[:, 0:D_MODEL]
        cc = proj_ref[:, D_MODEL:2 * D_MODEL]
        u = proj_ref[:, 2 * D_MODEL:3 * D_MODEL]
        g = proj_ref[:, 3 * D_MODEL:4 * D_MODEL]
        cv = conv_ref[...]
        silu, dsilu = _silu_parts(g)
        dwout_ref[...] += _dot_tn(silu * b * cv, dx1v)
        dproj_ref[:, 3 * D_MODEL:4 * D_MODEL] = (dy * b * cv * dsilu).astype(BF16)
        dproj_ref[:, 0:D_MODEL] = (dy * silu * cv).astype(BF16)
        dconv = dy * silu * b
        d1_ref[...] = pltpu.roll(dconv, tm - 1, 0)
        d2_ref[...] = pltpu.roll(dconv, tm - 2, 0)
        rows = lax.broadcasted_iota(jnp.int32, (8, D_MODEL), 0)
        n8, c8 = next_ref[...], dconv[tm - 8:tm]
        d1_ref[tm - 8:tm, :] = jnp.where(rows >= 7, pltpu.roll(n8, 7, 0), pltpu.roll(c8, 7, 0))
        d2_ref[tm - 8:tm, :] = jnp.where(rows >= 6, pltpu.roll(n8, 6, 0), pltpu.roll(c8, 6, 0))
        next_ref[...] = dconv[0:8]
        d1, d2 = d1_ref[...], d2_ref[...]
        v = cc * u
        dcw_ref[0:1, :] += jnp.sum(d2 * v, axis=0, keepdims=True)
        dcw_ref[1:2, :] += jnp.sum(d1 * v, axis=0, keepdims=True)
        dcw_ref[2:3, :] += jnp.sum(dconv * v, axis=0, keepdims=True)
        dv = cw_ref[0:1, :] * d2 + cw_ref[1:2, :] * d1 + cw_ref[2:3, :] * dconv
        dproj_ref[:, D_MODEL:2 * D_MODEL] = (dv * u).astype(BF16)
        dproj_ref[:, 2 * D_MODEL:3 * D_MODEL] = (dv * cc).astype(BF16)
        dh = jnp.zeros((tm, D_MODEL), F32)
        for d in range(N_DEV):
            dh += lax.dot_general(dproj_ref[:, d * n_col:(d + 1) * n_col], win_ref[d], NT, preferred_element_type=F32)
        xv = x_ref[...]
        r = _rstd(xv)
        xh = xv * r
        h_ref[...] = (xh * an_ref[...]).T.astype(BF16)
        dan_ref[...] += jnp.sum(dh * xh, axis=0, keepdims=True)
        dx_ref[...] = dx1v + _norm_bwd(dh * an_ref[...], xh, r)

        @pl.when(step == n_seq * nt - 1)
        def _():
            for cp in _scatter_copies(ready_refs, part_refs, sems)[1]:
                cp.wait()

    tok = lambda width: pl.BlockSpec((tm, width), lambda s, i: (s * nt + nt - 1 - i, 0))
    t = n_seq * seq
    outs = pl.pallas_call(
        body, name="conv_mixer_bwd", grid=(n_seq, nt),
        in_specs=[tok(D_MODEL), tok(D_MODEL), tok(4 * D_MODEL), tok(D_MODEL), _const_spec((1, D_MODEL)),
                  _const_spec(w_in.shape), _const_spec((3, D_MODEL)), _const_spec(w_out.shape)] + [ANY_SPEC] * n_ready,
        out_specs=[tok(D_MODEL), pl.BlockSpec((D_MODEL, tm), lambda s, i: (0, s * nt + nt - 1 - i)),
                   tok(4 * D_MODEL), _acc_spec((D_MODEL, D_MODEL)),
                   _acc_spec((1, D_MODEL)), _acc_spec((8, D_MODEL))] + [ANY_SPEC] * n_ready,
        out_shape=[jax.ShapeDtypeStruct((t, D_MODEL), F32), jax.ShapeDtypeStruct((D_MODEL, t), BF16),
                   jax.ShapeDtypeStruct((t, 4 * D_MODEL), BF16), jax.ShapeDtypeStruct((D_MODEL, D_MODEL), F32),
                   jax.ShapeDtypeStruct((1, D_MODEL), F32), jax.ShapeDtypeStruct((8, D_MODEL), F32)]
        + [jax.ShapeDtypeStruct(a.shape, a.dtype) for a in ready_stacks],
        scratch_shapes=[pltpu.VMEM((8, D_MODEL), F32), pltpu.VMEM((tm, D_MODEL), F32), pltpu.VMEM((tm, D_MODEL), F32)]
        + _comm_sems(n_ready),
        compiler_params=_params(2),
    )(dx1, x, proj, conv, a_norm, w_in, conv_w, w_out, *ready_stacks)
    return outs[:6], outs[6:]


def _w_in_grad_exchange(a_t, b, n_col, chip, ready_stacks, small):
    r, t = a_t.shape
    bt = 1024
    nk = t // bt
    n_ready = len(ready_stacks)
    n_chip = N_DEV // 2
    n_remote = N_DEV - 2

    def body(chip_ref, a_ref, b_ref, *rest):
        ready_refs, small_ref, rest = rest[:n_ready], rest[n_ready], rest[n_ready + 1:]
        own_ref, got_ref, rest = rest[0], rest[1], rest[2:]
        part_refs, small_all_ref, rest = rest[:n_ready], rest[n_ready], rest[n_ready + 1:]
        acc_ref, land_ref, stage_ref, pair_send, pair_recv, chip_send, chip_recv = rest[:7]
        scatter_sems, gather_sems = rest[7:10], rest[10:13]
        s, k = pl.program_id(0), pl.program_id(1)
        x, y, c, _ = _place()
        chip = 2 * x + y

        @pl.when(jnp.logical_and(s == 0, k == 0))
        def _():
            for cp in _scatter_copies(ready_refs, part_refs, scatter_sems)[0]:
                cp.start()
            for cp in _gather_copies([small_ref], [small_all_ref], gather_sems)[0]:
                cp.start()

        for parity in range(2):
            @pl.when(s % 2 == parity)
            def _(parity=parity):
                @pl.when(k == 0)
                def _():
                    acc_ref[parity] = jnp.zeros((r, n_col), F32)

                acc_ref[parity] += jnp.dot(a_ref[...], b_ref[...], preferred_element_type=F32)

        def to_sibling(step):
            return pltpu.make_async_remote_copy(
                src_ref=acc_ref.at[step % 2], dst_ref=land_ref, send_sem=pair_send.at[step], recv_sem=pair_recv.at[step],
                device_id=(x, y, 1 - c), device_id_type=MESH)

        def to_owner(nth):
            owner_chip = (chip + 1 + nth) % n_chip
            slot = jnp.bitwise_xor(chip, owner_chip) - 1
            return pltpu.make_async_remote_copy(
                src_ref=stage_ref.at[nth % 2], dst_ref=got_ref.at[slot], send_sem=chip_send.at[nth],
                recv_sem=chip_recv.at[slot], device_id=(owner_chip // 2, owner_chip % 2, c), device_id_type=MESH)

        for step in range(N_DEV):
            @pl.when(jnp.logical_and(s == step, k == nk - 1))
            def _(step=step):
                owner_core = step % 2

                @pl.when(c != owner_core)
                def _():
                    to_sibling(step).start()

                @pl.when(c == owner_core)
                def _():
                    to_sibling(step).wait_recv()
                    total = acc_ref[step % 2] + land_ref[...]
                    if step < n_remote:
                        nth = step // 2
                        if nth >= 2:
                            to_owner(nth - 2).wait_send()
                        stage_ref[nth % 2] = total.astype(BF16)
                        to_owner(nth).start()
                    else:
                        own_ref[...] = total
                    if step >= 1:
                        to_sibling(step - 1).wait_send()

        @pl.when(jnp.logical_and(s == N_DEV - 1, k == nk - 1))
        def _():
            @pl.when(c != (N_DEV - 1) % 2)
            def _():
                to_sibling(N_DEV - 1).wait_send()

            to_owner(1).wait_send()
            to_owner(2).wait_send()
            for slot in range(n_chip - 1):
                pltpu.make_async_remote_copy(
                    src_ref=stage_ref.at[0], dst_ref=got_ref.at[slot], send_sem=chip_send.at[0],
                    recv_sem=chip_recv.at[slot], device_id=(x, y, c), device_id_type=MESH).wait_recv()
            for cp in _scatter_copies(ready_refs, part_refs, scatter_sems)[1]:
                cp.wait()
            for cp in _gather_copies([small_ref], [small_all_ref], gather_sems)[1]:
                cp.wait()

    def owner_block(s, k, chip):
        return (k, 2 * ((chip[0] + 1 + s // 2) % n_chip) + s % 2)

    grid_spec = pltpu.PrefetchScalarGridSpec(
        num_scalar_prefetch=1, grid=(N_DEV, nk),
        in_specs=[pl.BlockSpec((r, bt), lambda s, k, chip: (0, k)), pl.BlockSpec((bt, n_col), owner_block)]
        + [ANY_SPEC] * (n_ready + 1),
        out_specs=[pl.BlockSpec((r, n_col), lambda s, k, chip: (0, 0)), ANY_SPEC] + [ANY_SPEC] * (n_ready + 1),
        scratch_shapes=[pltpu.VMEM((2, r, n_col), F32), pltpu.VMEM((r, n_col), F32), pltpu.VMEM((2, r, n_col), BF16),
                        pltpu.SemaphoreType.DMA((N_DEV,)), pltpu.SemaphoreType.DMA((N_DEV,)),
                        pltpu.SemaphoreType.DMA((n_chip - 1,)), pltpu.SemaphoreType.DMA((n_chip - 1,))]
        + _comm_sems(n_ready) + _comm_sems(1))
    outs = pl.pallas_call(
        body, name="w_in_grad_exchange", grid_spec=grid_spec,
        out_shape=[jax.ShapeDtypeStruct((r, n_col), F32), jax.ShapeDtypeStruct((n_chip - 1, r, n_col), BF16)]
        + [jax.ShapeDtypeStruct(p.shape, p.dtype) for p in ready_stacks] + _stacked([small]),
        compiler_params=_params(2),
    )(chip, a_t, b, *ready_stacks, small)
    return outs[0], outs[1], outs[2:2 + n_ready], outs[2 + n_ready]


def _proj_fwd(x1, pos, rc, kv_norm, w_dkv, ckv_norm, w_ukv, b_norm, b_w_in, q_norm, w_uq):
    t = x1.shape[0]
    tm = LATENT_ROW_TILE

    def body(x1_ref, pos_ref, rc_ref, kvn_ref, wdkv_ref, ckvn_ref, wukv_ref, bn_ref, bwin_ref, qn_ref, wuq_ref,
             ckr_ref, cq_ref, gb_ref, q_ref, k_ref, v_ref):
        ct, s1, s2 = _rope_tables(pos_ref[...], rc_ref[...])
        xv = x1_ref[...]
        xh = xv * _rstd(xv)
        ckr = _dot(xh * kvn_ref[...], wdkv_ref[...])
        ckr_ref[...] = ckr
        ckv = ckr[:, 0:KV_RANK]
        kv = _dot(ckv * _rstd(ckv) * ckvn_ref[...], wukv_ref[...])
        k_rope = _rope(ckr[:, KV_RANK:CKR_PAD], ct, s1, s2)
        lane = lax.broadcasted_iota(jnp.int32, (tm, HEAD_PAD), 1)
        low = lane < QK_NOPE
        for h in range(N_HEADS):
            kv_h = kv[:, h * HEAD_PAD:(h + 1) * HEAD_PAD]
            k_ref[:, h * HEAD_PAD:(h + 1) * HEAD_PAD] = jnp.where(low, kv_h, k_rope).astype(BF16)
            v_ref[:, h * HEAD_PAD:(h + 1) * HEAD_PAD] = jnp.where(low, pltpu.roll(kv_h, V_HEAD, 1), 1.0).astype(BF16)
        pb = _dot(xh * bn_ref[...], bwin_ref[...])
        cq = pb[:, 0:Q_RANK]
        cq_ref[...] = cq
        gb_ref[...] = pb[:, Q_RANK:Q_RANK + E_B]
        q = _dot(cq * _rstd(cq) * qn_ref[...], wuq_ref[...])
        for h in range(N_HEADS):
            q_ref[:, h * HEAD_PAD:(h + 1) * HEAD_PAD] = _rope(
                q[:, h * HEAD_PAD:(h + 1) * HEAD_PAD], ct, s1, s2).astype(BF16)

    tok = lambda width: pl.BlockSpec((tm, width), lambda i: (i, 0))
    weights = [kv_norm, w_dkv, ckv_norm, w_ukv, b_norm, b_w_in, q_norm, w_uq]
    wide = N_HEADS * HEAD_PAD
    return pl.pallas_call(
        body, name="latent_proj_fwd", grid=(t // tm,),
        in_specs=[tok(D_MODEL), tok(1), _const_spec(rc.shape)] + [_const_spec(w.shape) for w in weights],
        out_specs=[tok(CKR_PAD), tok(Q_RANK), tok(E_B), tok(wide), tok(wide), tok(wide)],
        out_shape=[jax.ShapeDtypeStruct((t, CKR_PAD), F32), jax.ShapeDtypeStruct((t, Q_RANK), F32),
                   jax.ShapeDtypeStruct((t, E_B), F32), jax.ShapeDtypeStruct((t, wide), BF16),
                   jax.ShapeDtypeStruct((t, wide), BF16), jax.ShapeDtypeStruct((t, wide), BF16)],
        compiler_params=_params(1),
    )(x1, pos, rc, *weights)


def _proj_bwd(dq, dk, dv, dgb, cq, ckr, x1, dx2, pos, rc, kv_norm, w_dkv, ckv_norm, w_ukv, b_norm, b_w_in, q_norm, w_uq):
    t = x1.shape[0]
    tm = LATENT_ROW_TILE
    wide = N_HEADS * HEAD_PAD

    def body(dq_ref, dk_ref, dv_ref, dgb_ref, cq_ref, ckr_ref, x1_ref, dx2_ref, pos_ref, rc_ref,
             kvn_ref, wdkv_ref, ckvn_ref, wukv_ref, bn_ref, bwin_ref, qn_ref, wuq_ref,
             dx1_ref, dwuq_ref, dwbin_ref, dwukv_ref, dwdkv_ref, dqn_ref, dbn_ref, dckvn_ref, dkvn_ref, dqu_ref, dkv_ref):
        @pl.when(pl.program_id(0) == 0)
        def _():
            for ref in (dwuq_ref, dwbin_ref, dwukv_ref, dwdkv_ref, dqn_ref, dbn_ref, dckvn_ref, dkvn_ref):
                ref[...] = jnp.zeros_like(ref)

        ct, s1, s2 = _rope_tables(pos_ref[...], rc_ref[...])
        lane = lax.broadcasted_iota(jnp.int32, (tm, HEAD_PAD), 1)
        low = lane < QK_NOPE
        for h in range(N_HEADS):
            dqu_ref[:, h * HEAD_PAD:(h + 1) * HEAD_PAD] = _rope(
                dq_ref[:, h * HEAD_PAD:(h + 1) * HEAD_PAD], ct, -s1, -s2).astype(BF16)
        cq = cq_ref[...]
        rq = _rstd(cq)
        cqh = cq * rq
        cqn = (cqh * qn_ref[...]).astype(BF16)
        dwuq = lax.dot_general(cqn, dqu_ref[...], TN, preferred_element_type=F32)
        for h in range(N_HEADS):
            dwuq_ref[h] += dwuq[:, h * HEAD_PAD:(h + 1) * HEAD_PAD]
        dcqn = lax.dot_general(dqu_ref[...], wuq_ref[...], NT, preferred_element_type=F32)
        dqn_ref[...] += jnp.sum(dcqn * cqh, axis=0, keepdims=True)
        dcq = _norm_bwd(dcqn * qn_ref[...], cqh, rq)
        dpb = jnp.concatenate([dcq, dgb_ref[...]], axis=1).astype(BF16)
        xv = x1_ref[...]
        r = _rstd(xv)
        xh = xv * r
        dwbin_ref[...] += _dot_tn(xh * bn_ref[...], dpb)
        dh3 = lax.dot_general(dpb, bwin_ref[...], NT, preferred_element_type=F32)
        dk_rope = jnp.zeros((tm, HEAD_PAD), F32)
        for h in range(N_HEADS):
            dk_h = dk_ref[:, h * HEAD_PAD:(h + 1) * HEAD_PAD]
            dv_h = dv_ref[:, h * HEAD_PAD:(h + 1) * HEAD_PAD]
            dkv_ref[:, h * HEAD_PAD:(h + 1) * HEAD_PAD] = jnp.where(low, dk_h, pltpu.roll(dv_h, V_HEAD, 1)).astype(BF16)
            dk_rope += dk_h
        rope_lanes = jnp.logical_and(lane >= ROPE_LO, lane < ROPE_LO + QK_ROPE)
        dk_rope = jnp.where(rope_lanes, _rope(dk_rope, ct, -s1, -s2), 0.0)
        ckv = ckr_ref[:, 0:KV_RANK]
        rk = _rstd(ckv)
        ckh = ckv * rk
        ckvn = (ckh * ckvn_ref[...]).astype(BF16)
        dwukv = lax.dot_general(ckvn, dkv_ref[...], TN, preferred_element_type=F32)
        for h in range(N_HEADS):
            dwukv_ref[h] += dwukv[:, h * HEAD_PAD:(h + 1) * HEAD_PAD]
        dckvn = lax.dot_general(dkv_ref[...], wukv_ref[...], NT, preferred_element_type=F32)
        dckvn_ref[...] += jnp.sum(dckvn * ckh, axis=0, keepdims=True)
        dckv = _norm_bwd(dckvn * ckvn_ref[...], ckh, rk)
        dckr = jnp.concatenate([dckv, dk_rope], axis=1).astype(BF16)
        dwdkv_ref[...] += _dot_tn(xh * kvn_ref[...], dckr)
        dh2 = lax.dot_general(dckr, wdkv_ref[...], NT, preferred_element_type=F32)
        dkvn_ref[...] += jnp.sum(dh2 * xh, axis=0, keepdims=True)
        dbn_ref[...] += jnp.sum(dh3 * xh, axis=0, keepdims=True)
        dx1_ref[...] = dx2_ref[...] + _norm_bwd(dh2 * kvn_ref[...] + dh3 * bn_ref[...], xh, r)

    tok = lambda width: pl.BlockSpec((tm, width), lambda i: (i, 0))
    weights = [kv_norm, w_dkv, ckv_norm, w_ukv, b_norm, b_w_in, q_norm, w_uq]
    acc_shapes = [(N_HEADS, Q_RANK, HEAD_PAD), (D_MODEL, Q_RANK + E_B), (N_HEADS, KV_RANK, HEAD_PAD), (D_MODEL, CKR_PAD),
                  (1, Q_RANK), (1, D_MODEL), (1, KV_RANK), (1, D_MODEL)]
    return pl.pallas_call(
        body, name="latent_proj_bwd", grid=(t // tm,),
        in_specs=[tok(wide), tok(wide), tok(wide), tok(E_B), tok(Q_RANK), tok(CKR_PAD), tok(D_MODEL), tok(D_MODEL), tok(1),
                  _const_spec(rc.shape)] + [_const_spec(w.shape) for w in weights],
        out_specs=[tok(D_MODEL)] + [_acc_spec(s) for s in acc_shapes],
        out_shape=[jax.ShapeDtypeStruct((t, D_MODEL), F32)] + [jax.ShapeDtypeStruct(s, F32) for s in acc_shapes],
        scratch_shapes=[pltpu.VMEM((tm, wide), BF16), pltpu.VMEM((tm, wide), BF16)],
        compiler_params=_params(1),
    )(dq, dk, dv, dgb, cq, ckr, x1, dx2, pos, rc, *weights)


def _attn_fwd(q, k, v, n_seq, seq):
    tq, tk = ATTN_FWD_TILES
    ratio = tq // tk
    nq = seq // tq
    pair = 2 * HEAD_PAD

    def body(q_ref, k_ref, v_ref, o_ref, lr_ref):
        i = pl.program_id(2)
        row = lax.broadcasted_iota(jnp.int32, (tq, tk), 0)
        col = lax.broadcasted_iota(jnp.int32, (tq, tk), 1)
        qs = [q_ref[:, hh * HEAD_PAD:(hh + 1) * HEAD_PAD] for hh in range(2)]

        def step(j, carry, shift):
            start = pl.multiple_of(j * tk, tk)
            out = []
            for hh in range(2):
                m, acc = carry[hh]
                s = lax.dot_general(qs[hh], k_ref[pl.ds(start, tk), hh * HEAD_PAD:(hh + 1) * HEAD_PAD], NT,
                                    preferred_element_type=F32) * (SOFTMAX_SCALE * LOG2_E)
                if shift is not None:
                    s = jnp.where(col + shift <= row, s, -jnp.inf)
                m_new = jnp.maximum(m, jnp.max(s, axis=-1, keepdims=True))
                p = jnp.exp2(s - m_new)
                acc = jnp.exp2(m - m_new) * acc + jnp.dot(
                    p.astype(BF16), v_ref[pl.ds(start, tk), hh * HEAD_PAD:(hh + 1) * HEAD_PAD], preferred_element_type=F32)
                out.append((m_new, acc))
            return tuple(out)

        one = (jnp.full((tq, 1), -jnp.inf, F32), jnp.zeros((tq, HEAD_PAD), F32))
        carry = lax.fori_loop(0, i * ratio, functools.partial(step, shift=None), (one, one))
        for d in range(ratio):
            carry = step(i * ratio + d, carry, d * tk)
        lane = lax.broadcasted_iota(jnp.int32, (tq, HEAD_PAD), 1)
        low = lane < V_HEAD
        halves = []
        for hh in range(2):
            m, acc = carry[hh]
            swapped = pltpu.roll(acc, V_HEAD, 1)
            halves.append(acc / swapped)
            lse = (m * LN_2) + jnp.log(jnp.where(low, swapped, acc))
            lr_ref[hh] = lse.T[0:8, :]
        o_ref[...] = jnp.where(low, halves[0], pltpu.roll(halves[1], V_HEAD, 1))

    t = n_seq * seq
    return pl.pallas_call(
        body, name="attention_fwd", grid=(n_seq, N_HEADS // 2, nq),
        in_specs=[pl.BlockSpec((tq, pair), lambda s, p, i: (s * nq + i, p)),
                  pl.BlockSpec((seq, pair), lambda s, p, i: (s, p)),
                  pl.BlockSpec((seq, pair), lambda s, p, i: (s, p))],
        out_specs=[pl.BlockSpec((tq, HEAD_PAD), lambda s, p, i: (s * nq + i, p)),
                   pl.BlockSpec((None, 2, 8, tq), lambda s, p, i: (s, p, 0, i))],
        out_shape=[jax.ShapeDtypeStruct((t, E_B), F32), jax.ShapeDtypeStruct((n_seq, N_HEADS, 8, seq), F32)],
        compiler_params=_params(3),
    )(q, k, v)


def _attn_bwd(q, k, v, do, lse_row, delta_row, n_seq, seq):
    tk, tq = ATTN_BWD_TILES
    ratio = tk // tq
    nk = seq // tk
    n_inner = seq // tq
    pair = 2 * HEAD_PAD

    def body(q_ref, k_ref, v_ref, do_ref, lr_ref, dr_ref, dk_ref, dv_ref, dq_ref):
        j = pl.program_id(2)

        @pl.when(j == 0)
        def _():
            dq_ref[...] = jnp.zeros_like(dq_ref)

        row = lax.broadcasted_iota(jnp.int32, (tk, tq), 0)
        col = lax.broadcasted_iota(jnp.int32, (tk, tq), 1)
        ks = [k_ref[:, hh * HEAD_PAD:(hh + 1) * HEAD_PAD] for hh in range(2)]
        vs = [v_ref[:, hh * HEAD_PAD:(hh + 1) * HEAD_PAD] for hh in range(2)]

        def step(i, carry, shift):
            start = pl.multiple_of(i * tq, tq)
            out = []
            for hh in range(2):
                dk_acc, dv_acc = carry[hh]
                qi = q_ref[pl.ds(start, tq), hh * HEAD_PAD:(hh + 1) * HEAD_PAD]
                doi = do_ref[pl.ds(start, tq), hh * HEAD_PAD:(hh + 1) * HEAD_PAD]
                st = lax.dot_general(ks[hh], qi, NT, preferred_element_type=F32) * SOFTMAX_SCALE
                pt = jnp.exp(st - lr_ref[hh, 0:1, pl.ds(start, tq)])
                if shift is not None:
                    pt = jnp.where(col + shift >= row, pt, 0.0)
                dv_acc = dv_acc + jnp.dot(pt.astype(BF16), doi, preferred_element_type=F32)
                dpt = lax.dot_general(vs[hh], doi, NT, preferred_element_type=F32)
                dst = (pt * (dpt - dr_ref[hh, 0:1, pl.ds(start, tq)]) * SOFTMAX_SCALE).astype(BF16)
                dk_acc = dk_acc + jnp.dot(dst, qi, preferred_element_type=F32)
                dq_ref[pl.ds(start, tq), hh * HEAD_PAD:(hh + 1) * HEAD_PAD] += lax.dot_general(
                    dst, ks[hh], TN, preferred_element_type=F32)
                out.append((dk_acc, dv_acc))
            return tuple(out)

        one = (jnp.zeros((tk, HEAD_PAD), F32), jnp.zeros((tk, HEAD_PAD), F32))
        carry = (one, one)
        for d in range(ratio):
            carry = step(j * ratio + d, carry, d * tq)
        carry = lax.fori_loop((j + 1) * ratio, n_inner, functools.partial(step, shift=None), carry)
        for hh in range(2):
            dk_ref[:, hh * HEAD_PAD:(hh + 1) * HEAD_PAD] = carry[hh][0]
            dv_ref[:, hh * HEAD_PAD:(hh + 1) * HEAD_PAD] = carry[hh][1]

    t = n_seq * seq
    wide = N_HEADS * HEAD_PAD
    return pl.pallas_call(
        body, name="attention_bwd", grid=(n_seq, N_HEADS // 2, nk),
        in_specs=[pl.BlockSpec((seq, pair), lambda s, p, j: (s, p)),
                  pl.BlockSpec((tk, pair), lambda s, p, j: (s * nk + j, p)),
                  pl.BlockSpec((tk, pair), lambda s, p, j: (s * nk + j, p)),
                  pl.BlockSpec((seq, pair), lambda s, p, j: (s, p)),
                  pl.BlockSpec((None, 2, 8, seq), lambda s, p, j: (s, p, 0, 0)),
                  pl.BlockSpec((None, 2, 8, seq), lambda s, p, j: (s, p, 0, 0))],
        out_specs=[pl.BlockSpec((tk, pair), lambda s, p, j: (s * nk + j, p)),
                   pl.BlockSpec((tk, pair), lambda s, p, j: (s * nk + j, p)),
                   pl.BlockSpec((seq, pair), lambda s, p, j: (s, p))],
        out_shape=[jax.ShapeDtypeStruct((t, wide), F32), jax.ShapeDtypeStruct((t, wide), F32),
                   jax.ShapeDtypeStruct((t, wide), F32)],
        compiler_params=_params(3),
    )(q, k, v, do, lse_row, delta_row)


def _head_tail(o, gb, x1, w_out, final_norm, target, n_seq, seq):
    tm = LATENT_ROW_TILE
    nt = seq // tm
    n_col = D_MODEL // N_DEV

    def body(o_ref, gb_ref, x1_ref, wout_ref, fn_ref, tgt_ref,
             dx2_ref, do_ref, dgb_ref, dr_ref, dwout_ref, dfn_ref, loss_ref):
        first = jnp.logical_and(pl.program_id(0) == 0, pl.program_id(1) == 0)

        @pl.when(first)
        def _():
            dwout_ref[...] = jnp.zeros_like(dwout_ref)
            dfn_ref[...] = jnp.zeros_like(dfn_ref)
            loss_ref[...] = jnp.zeros_like(loss_ref)

        ov, g = o_ref[...], gb_ref[...]
        silu, dsilu = _silu_parts(g)
        gated = (ov * silu).astype(BF16)
        x2 = x1_ref[...] + jnp.dot(gated, wout_ref[...], preferred_element_type=F32)
        r = _rstd(x2)
        xh = x2 * r
        err = xh * fn_ref[...] - tgt_ref[...]
        loss_ref[...] += 0.5 * jnp.sum(jnp.mean(err * err, axis=-1, keepdims=True), axis=0, keepdims=True)
        dy = err / D_MODEL
        dfn_ref[...] += jnp.sum(dy * xh, axis=0, keepdims=True)
        dx2 = _norm_bwd(dy * fn_ref[...], xh, r)
        dx2_ref[...] = dx2
        dx2b = dx2.astype(BF16)
        dw = lax.dot_general(gated, dx2b, TN, preferred_element_type=F32)
        for d in range(N_DEV):
            dwout_ref[d] += dw[:, d * n_col:(d + 1) * n_col]
        dgated = lax.dot_general(dx2b, wout_ref[...], NT, preferred_element_type=F32)
        do = dgated * silu
        dgb_ref[...] = dgated * ov * dsilu
        prod = do * ov
        lane = lax.broadcasted_iota(jnp.int32, (tm, HEAD_PAD), 1)
        low = lane < V_HEAD
        for p in range(N_HEADS // 2):
            blk = prod[:, p * HEAD_PAD:(p + 1) * HEAD_PAD]
            do_pair = do[:, p * HEAD_PAD:(p + 1) * HEAD_PAD]
            for hh in range(2):
                h = 2 * p + hh
                mine = do_pair if hh == 0 else pltpu.roll(do_pair, V_HEAD, 1)
                do_ref[:, h * HEAD_PAD:(h + 1) * HEAD_PAD] = jnp.where(low, mine, 0.0).astype(BF16)
                delta = jnp.sum(jnp.where(low if hh == 0 else ~low, blk, 0.0), axis=-1, keepdims=True)
                dr_ref[h] = jnp.broadcast_to(delta, (tm, HEAD_PAD)).T[0:8, :]

    tok = lambda width: pl.BlockSpec((tm, width), lambda s, i: (s * nt + i, 0))
    t = n_seq * seq
    return pl.pallas_call(
        body, name="head_tail", grid=(n_seq, nt),
        in_specs=[tok(E_B), tok(E_B), tok(D_MODEL), _const_spec(w_out.shape), _const_spec((1, D_MODEL)), tok(D_MODEL)],
        out_specs=[tok(D_MODEL), tok(N_HEADS * HEAD_PAD), tok(E_B),
                   pl.BlockSpec((None, N_HEADS, 8, tm), lambda s, i: (s, 0, 0, i)),
                   _acc_spec((N_DEV, E_B, n_col)), _acc_spec((1, D_MODEL)), _acc_spec((1, 1))],
        out_shape=[jax.ShapeDtypeStruct((t, D_MODEL), F32), jax.ShapeDtypeStruct((t, N_HEADS * HEAD_PAD), BF16),
                   jax.ShapeDtypeStruct((t, E_B), F32),
                   jax.ShapeDtypeStruct((n_seq, N_HEADS, 8, seq), F32), jax.ShapeDtypeStruct((N_DEV, E_B, n_col), F32),
                   jax.ShapeDtypeStruct((1, D_MODEL), F32), jax.ShapeDtypeStruct((1, 1), F32)],
        compiler_params=_params(2),
    )(o, gb, x1, w_out, final_norm, target)


def _adamw_math(w, g, m, v):
    m = ADAM_B1 * m + (1.0 - ADAM_B1) * g
    v = ADAM_B2 * v + (1.0 - ADAM_B2) * jnp.square(g)
    m_hat = m / (1.0 - ADAM_B1 ** ADAM_STEP)
    v_hat = v / (1.0 - ADAM_B2 ** ADAM_STEP)
    delta = -ADAM_LR * (m_hat / (jnp.sqrt(v_hat) + ADAM_EPS) + ADAM_WD * w)
    return delta, m, v


def _adamw_reduce(own, parts, w, m, v):
    rows, cols = w.shape
    br = 256 if rows % 256 == 0 else 128
    n_parts = parts.shape[0]

    def body(*refs):
        p_ref, w_ref, m_ref, v_ref, g_ref, d_ref, nm_ref, nv_ref = refs[-8:]
        g = p_ref[0].astype(F32) if own is None else refs[0][...] + p_ref[0].astype(F32)
        for k in range(1, n_parts):
            g = g + p_ref[k].astype(F32)
        g_ref[...] = g
        d_ref[...], nm_ref[...], nv_ref[...] = _adamw_math(w_ref[...], g, m_ref[...], v_ref[...])

    blk = pl.BlockSpec((br, cols), lambda i: (i, 0))
    first = [] if own is None else [own]
    return pl.pallas_call(
        body, name="adamw_reduce", grid=(rows // br,),
        in_specs=[blk] * len(first) + [pl.BlockSpec((n_parts, br, cols), lambda i: (0, i, 0)), blk, blk, blk],
        out_specs=[blk] * 4, out_shape=[jax.ShapeDtypeStruct((rows, cols), F32)] * 4,
        compiler_params=_params(1),
    )(*first, parts, w, m, v)


def _sum_parts(parts):
    def body(p_ref, o_ref):
        g = p_ref[0]
        for k in range(1, N_DEV):
            g = g + p_ref[k]
        o_ref[...] = g

    return pl.pallas_call(body, name="small_grad_sum", out_shape=jax.ShapeDtypeStruct(parts.shape[1:], F32))(parts)


def _adamw_small(gs, ws, ms, vs):
    n = len(gs)

    def body(*refs):
        ins, outs = refs[:4 * n], refs[4 * n:]
        for j in range(n):
            g_ref, w_ref, m_ref, v_ref = ins[j], ins[n + j], ins[2 * n + j], ins[3 * n + j]
            outs[3 * j][...], outs[3 * j + 1][...], outs[3 * j + 2][...] = _adamw_math(
                w_ref[...], g_ref[...], m_ref[...], v_ref[...])

    shapes = [jax.ShapeDtypeStruct(g.shape, F32) for g in gs for _ in range(3)]
    flat = pl.pallas_call(body, name="adamw_small", out_shape=shapes)(*gs, *ws, *ms, *vs)
    return [tuple(flat[3 * j:3 * j + 3]) for j in range(n)]


SMALL_ROWS = 8
LOSS_LANE = D_MODEL - 1


def _pack_small(g_fn, g_kvn, g_bn, g_an, g_ckvn, g_qn, g_conv, loss_part):
    def body(fn_ref, kvn_ref, bn_ref, an_ref, ckvn_ref, qn_ref, conv_ref, loss_ref, o_ref):
        o_ref[0:1, :] = fn_ref[...]
        o_ref[1:2, :] = kvn_ref[...]
        o_ref[2:3, :] = bn_ref[...]
        o_ref[3:4, :] = an_ref[...]
        lane = lax.broadcasted_iota(jnp.int32, (1, D_MODEL), 1)
        o_ref[4:5, :] = jnp.where(lane == LOSS_LANE, loss_ref[...], 0.0)
        o_ref[4:5, 0:KV_RANK] = ckvn_ref[...]
        o_ref[4:5, KV_RANK:KV_RANK + Q_RANK] = qn_ref[...]
        o_ref[5:8, :] = conv_ref[0:3, :]

    return pl.pallas_call(body, name="pack_small", out_shape=jax.ShapeDtypeStruct((SMALL_ROWS, D_MODEL), F32))(
        g_fn, g_kvn, g_bn, g_an, g_ckvn, g_qn, g_conv, loss_part)


def _pad_cols(a, width):
    return jnp.pad(a, ((0, 0), (0, width - a.shape[1])))


def _dkv_to_padded(a):
    r = a.shape[0]
    z = jnp.zeros((r, ROPE_LO), a.dtype)
    z2 = jnp.zeros((r, HEAD_PAD - ROPE_LO - QK_ROPE), a.dtype)
    return jnp.concatenate([a[:, :KV_RANK], z, a[:, KV_RANK:], z2], axis=1)


def _dkv_from_padded(a):
    return jnp.concatenate([a[:, :KV_RANK], a[:, KV_RANK + ROPE_LO:KV_RANK + ROPE_LO + QK_ROPE]], axis=1)


def _unstack_cols(a):
    return jnp.transpose(a, (1, 0, 2)).reshape(a.shape[1], N_DEV * a.shape[2])


def kernel(x, positions, a_norm, a_w_in, a_conv, a_w_out, kv_norm, w_dkv, ckv_norm, w_ukv, b_norm, b_w_in, b_q_norm, b_w_uq, b_w_out, final_norm, loss_target, m_a_norm, m_a_w_in, m_a_conv, m_a_w_out, m_kv_norm, m_w_dkv, m_ckv_norm, m_w_ukv, m_b_norm, m_b_w_in, m_b_q_norm, m_b_w_uq, m_b_w_out, m_final_norm, v_a_norm, v_a_w_in, v_a_conv, v_a_w_out, v_kv_norm, v_w_dkv, v_ckv_norm, v_w_ukv, v_b_norm, v_b_w_in, v_b_q_norm, v_b_w_uq, v_b_w_out, v_final_norm):
    n_seq, seq, _ = x.shape
    t = n_seq * seq
    me = 4 * lax.axis_index("x") + 2 * lax.axis_index("y") + lax.axis_index("c")

    big = {
        "a_w_in": (a_w_in[0], m_a_w_in[0], v_a_w_in[0]),
        "a_w_out": (a_w_out[0], m_a_w_out[0], v_a_w_out[0]),
        "w_dkv": tuple(_dkv_to_padded(a) for a in (w_dkv, m_w_dkv, v_w_dkv)),
        "w_ukv": (w_ukv, m_w_ukv, v_w_ukv),
        "b_w_in": (b_w_in[0], m_b_w_in[0], v_b_w_in[0]),
        "b_w_uq": tuple(_pad_cols(a[0], HEAD_PAD) for a in (b_w_uq, m_b_w_uq, v_b_w_uq)),
        "b_w_out": (b_w_out[0], m_b_w_out[0], v_b_w_out[0]),
    }
    names = list(big)
    first_names, later_names = names[:2], names[2:]
    gathered = _all_gather([big[n][0].astype(BF16) for n in first_names] + [a_norm, a_conv[0]])
    a_norm_f = gathered[2].reshape(1, D_MODEL)
    a_conv_f = _unstack_cols(gathered[3])
    w_a_in = gathered[0]
    w_a_out = gathered[1].reshape(D_MODEL, D_MODEL)

    x2d = x.reshape(t, D_MODEL)
    tgt = loss_target.reshape(t, D_MODEL)
    pos = positions.astype(F32).reshape(t, 1)
    rc = _rope_consts()
    kvn, ckvn, bn, qn, fn = kv_norm.reshape(1, -1), ckv_norm.reshape(1, -1), b_norm, b_q_norm, final_norm.reshape(1, -1)

    x1, proj, conv, later = _conv_fwd(x2d, a_norm_f, w_a_in, a_conv_f, w_a_out, n_seq, seq,
                                      [big[n][0].astype(BF16) for n in later_names])
    full = dict(zip(later_names, later))
    w_dkv_f = full["w_dkv"].reshape(D_MODEL, CKR_PAD)
    w_ukv_f = _unstack_cols(full["w_ukv"])
    w_b_in = full["b_w_in"].reshape(D_MODEL, Q_RANK + E_B)
    w_uq_f = _unstack_cols(full["b_w_uq"])
    w_b_out = _unstack_cols(full["b_w_out"])
    layer_b = (kvn, w_dkv_f, ckvn, w_ukv_f, bn, w_b_in, qn, w_uq_f)
    ckr, cq, gb, q, k, v = _proj_fwd(x1, pos, rc, *layer_b)
    o, lse_row = _attn_fwd(q, k, v, n_seq, seq)
    dx2, do, dgb, delta_row, g_b_out, g_fn, loss_part = _head_tail(o, gb, x1, w_b_out, fn, tgt, n_seq, seq)
    dk, dv, dq = _attn_bwd(q, k, v, do, lse_row, delta_row, n_seq, seq)
    dx1, g_uq, g_b_in, g_ukv, g_dkv, g_qn, g_bn, g_ckvn, g_kvn = _proj_bwd(
        dq, dk, dv, dgb, cq, ckr, x1, dx2, pos, rc, *layer_b)
    stacks = {
        "w_dkv": g_dkv.reshape(N_DEV, D_MODEL // N_DEV, CKR_PAD),
        "w_ukv": g_ukv,
        "b_w_in": g_b_in.reshape(N_DEV, D_MODEL // N_DEV, Q_RANK + E_B),
        "b_w_uq": g_uq,
        "b_w_out": g_b_out,
    }
    (dx, h_a, dproj, g_a_out, g_an, g_conv), later_parts = _conv_bwd(
        dx1, x2d, proj, conv, a_norm_f, w_a_in, a_conv_f, w_a_out, n_seq, seq, [stacks[n] for n in later_names])
    parts = dict(zip(later_names, later_parts))
    small = _pack_small(g_fn, g_kvn, g_bn, g_an, g_ckvn, g_qn, g_conv, loss_part)
    a_in_own, a_in_got, (parts["a_w_out"],), small_parts = _w_in_grad_exchange(
        h_a, dproj, D_MODEL * 4 // N_DEV, (me // 2).reshape(1).astype(jnp.int32),
        [g_a_out.reshape(N_DEV, D_MODEL // N_DEV, D_MODEL)], small)

    outs = {"a_w_in": _adamw_reduce(a_in_own, a_in_got, *big["a_w_in"])}
    for n in names[1:]:
        outs[n] = _adamw_reduce(None, parts[n], *big[n])
    outs["w_dkv"] = tuple(_dkv_from_padded(a) for a in outs["w_dkv"])
    outs["b_w_uq"] = tuple(a[:, :QK_NOPE + QK_ROPE] for a in outs["b_w_uq"])
    for n in ("a_w_in", "a_w_out", "b_w_in", "b_w_uq", "b_w_out"):
        outs[n] = tuple(a[None] for a in outs[n])

    total = _sum_parts(small_parts)
    loss = total[4, LOSS_LANE]
    shard = D_MODEL // N_DEV
    g_small = {
        "final_norm": total[0], "kv_norm": total[1], "b_norm": total[2:3],
        "a_norm": lax.dynamic_slice_in_dim(total[3:4], me * shard, shard, axis=1),
        "ckv_norm": total[4, 0:KV_RANK], "b_q_norm": total[4:5, KV_RANK:KV_RANK + Q_RANK],
        "a_conv": lax.dynamic_slice_in_dim(total[5:8], me * shard, shard, axis=1)[None],
    }
    small_state = {
        "final_norm": (final_norm, m_final_norm, v_final_norm), "kv_norm": (kv_norm, m_kv_norm, v_kv_norm),
        "b_norm": (b_norm, m_b_norm, v_b_norm), "a_norm": (a_norm, m_a_norm, v_a_norm),
        "ckv_norm": (ckv_norm, m_ckv_norm, v_ckv_norm), "b_q_norm": (b_q_norm, m_b_q_norm, v_b_q_norm),
        "a_conv": (a_conv, m_a_conv, v_a_conv),
    }
    small_names = list(g_small)
    as2d = lambda a: a.reshape(-1, a.shape[-1])
    upd = _adamw_small([as2d(g_small[n]) for n in small_names],
                       *[[as2d(small_state[n][j]) for n in small_names] for j in range(3)])
    for n, u in zip(small_names, upd):
        outs[n] = (g_small[n],) + tuple(a.reshape(g_small[n].shape) for a in u)

    order = ["a_norm", "a_w_in", "a_conv", "a_w_out", "kv_norm", "w_dkv", "ckv_norm", "w_ukv", "b_norm", "b_w_in",
             "b_q_norm", "b_w_uq", "b_w_out", "final_norm"]
    result = [loss, dx.reshape(n_seq, seq, D_MODEL)]
    for j in range(4):
        result += [outs[n][j] for n in order]
    return tuple(result)
```

```python
import functools
import math

import numpy as np
import jax
import jax.numpy as jnp
from jax import lax
from jax.experimental import pallas as pl
from jax.experimental.pallas import tpu as pltpu

F32 = jnp.float32
BF16 = jnp.bfloat16

D_MODEL = 1024
N_HEADS = 8
QK_NOPE = 64
QK_ROPE = 32
V_HEAD = 64
KV_RANK = 256
Q_RANK = 384
E_B = N_HEADS * V_HEAD
HEAD_PAD = 128
CKR_PAD = KV_RANK + HEAD_PAD
ROPE_LO = QK_NOPE
ROPE_HALF = QK_ROPE // 2
ROPE_THETA = 10000.0
SOFTMAX_SCALE = 1.0 / math.sqrt(QK_NOPE + QK_ROPE)
LOG2_E = math.log2(math.e)
LN_2 = math.log(2.0)
EPS = 1e-6
N_DEV = 8

ADAM_LR = 0.001
ADAM_B1 = 0.9
ADAM_B2 = 0.999
ADAM_EPS = 1e-08
ADAM_WD = 0.01
ADAM_STEP = 10

ROW_TILE = 256
LATENT_ROW_TILE = 512
PICKUP_BLOCK = 3
ATTN_FWD_TILES = (512, 512)
ATTN_BWD_TILES = (512, 512)
VMEM_LIMIT = 56 * 1024 * 1024

MESH = pl.DeviceIdType.MESH
NT = (((1,), (1,)), ((), ()))
TN = (((0,), (0,)), ((), ()))


def _dot(a, b):
    return jnp.dot(a.astype(BF16), b.astype(BF16), preferred_element_type=F32)


def _dot_nt(a, b):
    return lax.dot_general(a.astype(BF16), b.astype(BF16), NT, preferred_element_type=F32)


def _dot_tn(a, b):
    return lax.dot_general(a.astype(BF16), b.astype(BF16), TN, preferred_element_type=F32)


def _rstd(x):
    return lax.rsqrt(jnp.mean(x * x, axis=-1, keepdims=True) + EPS)


def _norm_bwd(a, xh, r):
    return r * (a - xh * jnp.mean(a * xh, axis=-1, keepdims=True))


def _silu_parts(g):
    sg = jax.nn.sigmoid(g)
    return g * sg, sg * (1.0 + g * (1.0 - sg))


def _rope_consts():
    inv = (ROPE_THETA ** (-np.arange(0, QK_ROPE, 2, dtype=np.float32) / QK_ROPE)).astype(np.float32)
    t = np.zeros((8, HEAD_PAD), np.float32)
    t[0, ROPE_LO:ROPE_LO + ROPE_HALF] = inv
    t[0, ROPE_LO + ROPE_HALF:ROPE_LO + QK_ROPE] = inv
    t[1, ROPE_LO:ROPE_LO + ROPE_HALF] = -1.0
    t[2, ROPE_LO + ROPE_HALF:ROPE_LO + QK_ROPE] = 1.0
    return jnp.asarray(t)


def _rope_tables(pos, rc):
    ang = pos * rc[0:1, :]
    cosv = jnp.cos(ang)
    sinv = jnp.sin(ang)
    return cosv, sinv * rc[1:2, :], sinv * rc[2:3, :]


def _rope(x, ct, s1, s2):
    up = pltpu.roll(x, HEAD_PAD - ROPE_HALF, 1)
    dn = pltpu.roll(x, ROPE_HALF, 1)
    return x * ct + up * s1 + dn * s2


def _const_spec(shape):
    nd = len(shape)
    return pl.BlockSpec(shape, lambda *_: (0,) * nd, pipeline_mode=pl.Buffered(1))


def _acc_spec(shape):
    nd = len(shape)
    return pl.BlockSpec(shape, lambda *_: (0,) * nd)


def _params(n_axes):
    return pltpu.CompilerParams(dimension_semantics=("arbitrary",) * n_axes, vmem_limit_bytes=VMEM_LIMIT)


def _place():
    x, y, c = lax.axis_index("x"), lax.axis_index("y"), lax.axis_index("c")
    return x, y, c, 4 * x + 2 * y + c


def _peer(x, y, c, mask):
    px = 1 - x if mask & 4 else x
    py = 1 - y if mask & 2 else y
    pc = 1 - c if mask & 1 else c
    return (px, py, pc), 4 * px + 2 * py + pc


ANY_SPEC = pl.BlockSpec(memory_space=pl.ANY)


def _comm_sems(n):
    return [pltpu.SemaphoreType.DMA((n, N_DEV - 1)), pltpu.SemaphoreType.DMA((n, N_DEV - 1)), pltpu.SemaphoreType.DMA((n,))]


def _gather_copies(ins, outs, sems):
    send_sems, recv_sems, local_sems = sems
    x, y, c, me = _place()
    starts, waits = [], []
    for w in range(len(ins)):
        mine = pltpu.make_async_copy(ins[w], outs[w].at[me], local_sems.at[w])
        starts.append(mine)
        waits.append(mine)
        for mask in range(1, N_DEV):
            peer, peer_idx = _peer(x, y, c, mask)
            starts.append(pltpu.make_async_remote_copy(
                src_ref=ins[w], dst_ref=outs[w].at[me], send_sem=send_sems.at[w, mask - 1],
                recv_sem=recv_sems.at[w, mask - 1], device_id=peer, device_id_type=MESH))
            waits.append(pltpu.make_async_remote_copy(
                src_ref=ins[w], dst_ref=outs[w].at[peer_idx], send_sem=send_sems.at[w, mask - 1],
                recv_sem=recv_sems.at[w, mask - 1], device_id=peer, device_id_type=MESH))
    return starts, waits


def _scatter_copies(ins, outs, sems):
    send_sems, recv_sems, local_sems = sems
    x, y, c, me = _place()
    copies = []
    for w in range(len(ins)):
        copies.append(pltpu.make_async_copy(ins[w].at[me], outs[w].at[0], local_sems.at[w]))
        for mask in range(1, N_DEV):
            peer, peer_idx = _peer(x, y, c, mask)
            copies.append(pltpu.make_async_remote_copy(
                src_ref=ins[w].at[peer_idx], dst_ref=outs[w].at[mask], send_sem=send_sems.at[w, mask - 1],
                recv_sem=recv_sems.at[w, mask - 1], device_id=peer, device_id_type=MESH))
    return copies, copies


def _stacked(arrays):
    return [jax.ShapeDtypeStruct((N_DEV,) + a.shape, a.dtype) for a in arrays]


def _all_gather(shards):
    n = len(shards)

    def body(*refs):
        ins, outs = refs[:n], refs[n:2 * n]
        send_sems, recv_sems, local_sems = refs[2 * n:]
        x, y, c, me = _place()
        sibling = (x, y, 1 - c)
        chips = [(1 - x, y), (x, 1 - y), (1 - x, 1 - y)]

        def copy(w, k, block, to, src=None):
            idx = 4 * block[0] + 2 * block[1] + block[2]
            return pltpu.make_async_remote_copy(
                src_ref=outs[w].at[idx] if src is None else src, dst_ref=outs[w].at[idx],
                send_sem=send_sems.at[w, k], recv_sem=recv_sems.at[w, k], device_id=to, device_id_type=MESH)

        local, sent = [], []
        for w in range(n):
            mine = pltpu.make_async_copy(ins[w], outs[w].at[me], local_sems.at[w])
            mine.start()
            local.append(mine)
            first = [copy(w, 0, (x, y, c), sibling, src=ins[w])]
            first += [copy(w, 1 + j, (x, y, c), (*chip, c), src=ins[w]) for j, chip in enumerate(chips)]
            for cp in first:
                cp.start()
            sent += first
        for w in range(n):
            for j, chip in enumerate(chips):
                copy(w, 1 + j, (*chip, c), (x, y, c)).wait_recv()
                onward = copy(w, 4 + j, (*chip, c), sibling)
                onward.start()
                sent.append(onward)
        for w in range(n):
            copy(w, 0, sibling, (x, y, c)).wait_recv()
            for j, chip in enumerate(chips):
                copy(w, 4 + j, (*chip, 1 - c), (x, y, c)).wait_recv()
        for cp in sent:
            cp.wait_send()
        for cp in local:
            cp.wait()

    return pl.pallas_call(
        body, name="weight_all_gather", out_shape=_stacked(shards),
        in_specs=[ANY_SPEC] * n, out_specs=[ANY_SPEC] * n, scratch_shapes=_comm_sems(n),
    )(*shards)


def _conv_fwd(x, a_norm, w_in, conv_w, w_out, n_seq, seq, later_shards):
    tm = ROW_TILE
    nt = seq // tm
    n_col = w_in.shape[2]
    n_later = len(later_shards)

    def body(x_ref, an_ref, win_ref, cw_ref, wout_ref, *rest):
        shard_refs, rest = rest[:n_later], rest[n_later:]
        x1_ref, proj_ref, conv_ref = rest[:3]
        stack_refs, rest = rest[3:3 + n_later], rest[3 + n_later:]
        prev_ref, sems = rest[0], rest[1:]
        step = pl.program_id(0) * nt + pl.program_id(1)

        @pl.when(step == 0)
        def _():
            for cp in _gather_copies(shard_refs, stack_refs, sems)[0]:
                cp.start()

        @pl.when(pl.program_id(1) == 0)
        def _():
            prev_ref[...] = jnp.zeros_like(prev_ref)

        xv = x_ref[...]
        h = (xv * _rstd(xv) * an_ref[...]).astype(BF16)
        for d in range(N_DEV):
            proj_ref[:, d * n_col:(d + 1) * n_col] = jnp.dot(h, win_ref[d], preferred_element_type=F32)
        b = proj_ref[:, 0:D_MODEL]
        v = proj_ref[:, D_MODEL:2 * D_MODEL] * proj_ref[:, 2 * D_MODEL:3 * D_MODEL]
        g = proj_ref[:, 3 * D_MODEL:4 * D_MODEL]
        w0, w1, w2 = cw_ref[0:1, :], cw_ref[1:2, :], cw_ref[2:3, :]
        conv_ref[...] = w0 * pltpu.roll(v, 2, 0) + w1 * pltpu.roll(v, 1, 0) + w2 * v
        rows = lax.broadcasted_iota(jnp.int32, (8, D_MODEL), 0)
        p8, v8 = prev_ref[...], v[0:8]
        back1 = jnp.where(rows < 1, pltpu.roll(p8, 1, 0), pltpu.roll(v8, 1, 0))
        back2 = jnp.where(rows < 2, pltpu.roll(p8, 2, 0), pltpu.roll(v8, 2, 0))
        conv_ref[0:8, :] = w0 * back2 + w1 * back1 + w2 * v8
        prev_ref[...] = v[tm - 8:tm]
        silu, _ = _silu_parts(g)
        yv = silu * b * conv_ref[...]
        x1_ref[...] = xv + _dot(yv, wout_ref[...])

        @pl.when(step == n_seq * nt - 1)
        def _():
            for cp in _gather_copies(shard_refs, stack_refs, sems)[1]:
                cp.wait()

    tok = lambda width: pl.BlockSpec((tm, width), lambda s, i: (s * nt + i, 0))
    t = n_seq * seq
    outs = pl.pallas_call(
        body, name="conv_mixer_fwd", grid=(n_seq, nt),
        in_specs=[tok(D_MODEL), _const_spec((1, D_MODEL)), _const_spec(w_in.shape), _const_spec((3, D_MODEL)),
                  _const_spec(w_out.shape)] + [ANY_SPEC] * n_later,
        out_specs=[tok(D_MODEL), tok(4 * D_MODEL), tok(D_MODEL)] + [ANY_SPEC] * n_later,
        out_shape=[jax.ShapeDtypeStruct((t, D_MODEL), F32), jax.ShapeDtypeStruct((t, 4 * D_MODEL), F32),
                   jax.ShapeDtypeStruct((t, D_MODEL), F32)] + _stacked(later_shards),
        scratch_shapes=[pltpu.VMEM((8, D_MODEL), F32)] + _comm_sems(n_later),
        compiler_params=_params(2),
    )(x, a_norm, w_in, conv_w, w_out, *later_shards)
    return outs[0], outs[1], outs[2], outs[3:]


def _conv_bwd(dx1, x, proj, conv, a_norm, w_in, conv_w, w_out, n_seq, seq, ready_stacks):
    tm = ROW_TILE
    nt = seq // tm
    n_col = w_in.shape[2]
    n_ready = len(ready_stacks)

    def body(dx1_ref, x_ref, proj_ref, conv_ref, an_ref, win_ref, cw_ref, wout_ref, *rest):
        ready_refs, rest = rest[:n_ready], rest[n_ready:]
        dx_ref, h_ref, dproj_ref, dwout_ref, dan_ref, dcw_ref = rest[:6]
        part_refs, rest = rest[6:6 + n_ready], rest[6 + n_ready:]
        next_ref, d1_ref, d2_ref = rest[:3]
        sems = rest[3:]
        step = pl.program_id(0) * nt + pl.program_id(1)
        first = step == 0

        @pl.when(first)
        def _():
            for cp in _scatter_copies(ready_refs, part_refs, sems)[0]:
                cp.start()
            dwout_ref[...] = jnp.zeros_like(dwout_ref)
            dan_ref[...] = jnp.zeros_like(dan_ref)
            dcw_ref[...] = jnp.zeros_like(dcw_ref)

        @pl.when(pl.program_id(1) == 0)
        def _():
            next_ref[...] = jnp.zeros_like(next_ref)

        dx1v = dx1_ref[...]
        dy = _dot_nt(dx1v, wout_ref[...])
        b = proj_ref[:, 0:D_MODEL]
        cc = proj_ref[:, D_MODEL:2 * D_MODEL]
        u = proj_ref[:, 2 * D_MODEL:3 * D_MODEL]
        g = proj_ref[:, 3 * D_MODEL:4 * D_MODEL]
        cv = conv_ref[...]
        silu, dsilu = _silu_parts(g)
        dwout_ref[...] += _dot_tn(silu * b * cv, dx1v)
        dproj_ref[:, 3 * D_MODEL:4 * D_MODEL] = (dy * b * cv * dsilu).astype(BF16)
        dproj_ref[:, 0:D_MODEL] = (dy * silu * cv).astype(BF16)
        dconv = dy * silu * b
        d1_ref[...] = pltpu.roll(dconv, tm - 1, 0)
        d2_ref[...] = pltpu.roll(dconv, tm - 2, 0)
        rows = lax.broadcasted_iota(jnp.int32, (8, D_MODEL), 0)
        n8, c8 = next_ref[...], dconv[tm - 8:tm]
        d1_ref[tm - 8:tm, :] = jnp.where(rows >= 7, pltpu.roll(n8, 7, 0), pltpu.roll(c8, 7, 0))
        d2_ref[tm - 8:tm, :] = jnp.where(rows >= 6, pltpu.roll(n8, 6, 0), pltpu.roll(c8, 6, 0))
        next_ref[...] = dconv[0:8]
        d1, d2 = d1_ref[...], d2_ref[...]
        v = cc * u
        dcw_ref[0:1, :] += jnp.sum(d2 * v, axis=0, keepdims=True)
        dcw_ref[1:2, :] += jnp.sum(d1 * v, axis=0, keepdims=True)
        dcw_ref[2:3, :] += jnp.sum(dconv * v, axis=0, keepdims=True)
        dv = cw_ref[0:1, :] * d2 + cw_ref[1:2, :] * d1 + cw_ref[2:3, :] * dconv
        dproj_ref[:, D_MODEL:2 * D_MODEL] = (dv * u).astype(BF16)
        dproj_ref[:, 2 * D_MODEL:3 * D_MODEL] = (dv * cc).astype(BF16)
        dh = jnp.zeros((tm, D_MODEL), F32)
        for d in range(N_DEV):
            dh += lax.dot_general(dproj_ref[:, d * n_col:(d + 1) * n_col], win_ref[d], NT, preferred_element_type=F32)
        xv = x_ref[...]
        r = _rstd(xv)
        xh = xv * r
        h_ref[...] = (xh * an_ref[...]).T.astype(BF16)
        dan_ref[...] += jnp.sum(dh * xh, axis=0, keepdims=True)
        dx_ref[...] = dx1v + _norm_bwd(dh * an_ref[...], xh, r)

        @pl.when(step == n_seq * nt - 1)
        def _():
            for cp in _scatter_copies(ready_refs, part_refs, sems)[1]:
                cp.wait()

    tok = lambda width: pl.BlockSpec((tm, width), lambda s, i: (s * nt + nt - 1 - i, 0))
    t = n_seq * seq
    outs = pl.pallas_call(
        body, name="conv_mixer_bwd", grid=(n_seq, nt),
        in_specs=[tok(D_MODEL), tok(D_MODEL), tok(4 * D_MODEL), tok(D_MODEL), _const_spec((1, D_MODEL)),
                  _const_spec(w_in.shape), _const_spec((3, D_MODEL)), _const_spec(w_out.shape)] + [ANY_SPEC] * n_ready,
        out_specs=[tok(D_MODEL), pl.BlockSpec((D_MODEL, tm), lambda s, i: (0, s * nt + nt - 1 - i)),
                   tok(4 * D_MODEL), _acc_spec((D_MODEL, D_MODEL)),
                   _acc_spec((1, D_MODEL)), _acc_spec((8, D_MODEL))] + [ANY_SPEC] * n_ready,
        out_shape=[jax.ShapeDtypeStruct((t, D_MODEL), F32), jax.ShapeDtypeStruct((D_MODEL, t), BF16),
                   jax.ShapeDtypeStruct((t, 4 * D_MODEL), BF16), jax.ShapeDtypeStruct((D_MODEL, D_MODEL), F32),
                   jax.ShapeDtypeStruct((1, D_MODEL), F32), jax.ShapeDtypeStruct((8, D_MODEL), F32)]
        + [jax.ShapeDtypeStruct(a.shape, a.dtype) for a in ready_stacks],
        scratch_shapes=[pltpu.VMEM((8, D_MODEL), F32), pltpu.VMEM((tm, D_MODEL), F32), pltpu.VMEM((tm, D_MODEL), F32)]
        + _comm_sems(n_ready),
        compiler_params=_params(2),
    )(dx1, x, proj, conv, a_norm, w_in, conv_w, w_out, *ready_stacks)
    return outs[:6], outs[6:]


def _w_in_grad_exchange(a_t, b, n_col, chip, ready_stacks, small):
    r, t = a_t.shape
    bt = 1024
    nk = t // bt
    n_ready = len(ready_stacks)
    n_chip = N_DEV // 2
    n_remote = N_DEV - 2

    def body(chip_ref, a_ref, b_ref, *rest):
        ready_refs, small_ref, rest = rest[:n_ready], rest[n_ready], rest[n_ready + 1:]
        own_ref, got_ref, rest = rest[0], rest[1], rest[2:]
        part_refs, small_all_ref, rest = rest[:n_ready], rest[n_ready], rest[n_ready + 1:]
        acc_ref, land_ref, stage_ref, pair_send, pair_recv, chip_send, chip_recv = rest[:7]
        scatter_sems, gather_sems = rest[7:10], rest[10:13]
        s, k = pl.program_id(0), pl.program_id(1)
        x, y, c, _ = _place()
        chip = 2 * x + y

        @pl.when(jnp.logical_and(s == 0, k == 0))
        def _():
            for cp in _scatter_copies(ready_refs, part_refs, scatter_sems)[0]:
                cp.start()
            for cp in _gather_copies([small_ref], [small_all_ref], gather_sems)[0]:
                cp.start()

        for parity in range(2):
            @pl.when(s % 2 == parity)
            def _(parity=parity):
                @pl.when(k == 0)
                def _():
                    acc_ref[parity] = jnp.zeros((r, n_col), F32)

                acc_ref[parity] += jnp.dot(a_ref[...], b_ref[...], preferred_element_type=F32)

        def to_sibling(step):
            return pltpu.make_async_remote_copy(
                src_ref=acc_ref.at[step % 2], dst_ref=land_ref, send_sem=pair_send.at[step], recv_sem=pair_recv.at[step],
                device_id=(x, y, 1 - c), device_id_type=MESH)

        def to_owner(nth):
            owner_chip = (chip + 1 + nth) % n_chip
            slot = jnp.bitwise_xor(chip, owner_chip) - 1
            return pltpu.make_async_remote_copy(
                src_ref=stage_ref.at[nth % 2], dst_ref=got_ref.at[slot], send_sem=chip_send.at[nth],
                recv_sem=chip_recv.at[slot], device_id=(owner_chip // 2, owner_chip % 2, c), device_id_type=MESH)

        for step in range(N_DEV):
            owner_core = step % 2

            @pl.when(jnp.logical_and(s == step, k == nk - 1))
            def _(step=step, owner_core=owner_core):
                @pl.when(c != owner_core)
                def _():
                    to_sibling(step).start()

                if step >= 1:
                    @pl.when(c == owner_core)
                    def _():
                        to_sibling(step - 1).wait_send()

            if step < N_DEV - 1:
                pickup = jnp.logical_and(s == step + 1, k == min(PICKUP_BLOCK, nk - 1))
            else:
                pickup = jnp.logical_and(s == step, k == nk - 1)

            @pl.when(jnp.logical_and(pickup, c == owner_core))
            def _(step=step):
                to_sibling(step).wait_recv()
                total = acc_ref[step % 2] + land_ref[...]
                if step < n_remote:
                    nth = step // 2
                    if nth >= 2:
                        to_owner(nth - 2).wait_send()
                    stage_ref[nth % 2] = total.astype(BF16)
                    to_owner(nth).start()
                else:
                    own_ref[...] = total

        @pl.when(jnp.logical_and(s == N_DEV - 1, k == nk - 1))
        def _():
            @pl.when(c != (N_DEV - 1) % 2)
            def _():
                to_sibling(N_DEV - 1).wait_send()

            to_owner(1).wait_send()
            to_owner(2).wait_send()
            for slot in range(n_chip - 1):
                pltpu.make_async_remote_copy(
                    src_ref=stage_ref.at[0], dst_ref=got_ref.at[slot], send_sem=chip_send.at[0],
                    recv_sem=chip_recv.at[slot], device_id=(x, y, c), device_id_type=MESH).wait_recv()
            for cp in _scatter_copies(ready_refs, part_refs, scatter_sems)[1]:
                cp.wait()
            for cp in _gather_copies([small_ref], [small_all_ref], gather_sems)[1]:
                cp.wait()

    def owner_block(s, k, chip):
        return (k, 2 * ((chip[0] + 1 + s // 2) % n_chip) + s % 2)

    grid_spec = pltpu.PrefetchScalarGridSpec(
        num_scalar_prefetch=1, grid=(N_DEV, nk),
        in_specs=[pl.BlockSpec((r, bt), lambda s, k, chip: (0, k)), pl.BlockSpec((bt, n_col), owner_block)]
        + [ANY_SPEC] * (n_ready + 1),
        out_specs=[pl.BlockSpec((r, n_col), lambda s, k, chip: (0, 0)), ANY_SPEC] + [ANY_SPEC] * (n_ready + 1),
        scratch_shapes=[pltpu.VMEM((2, r, n_col), F32), pltpu.VMEM((r, n_col), F32), pltpu.VMEM((2, r, n_col), BF16),
                        pltpu.SemaphoreType.DMA((N_DEV,)), pltpu.SemaphoreType.DMA((N_DEV,)),
                        pltpu.SemaphoreType.DMA((n_chip - 1,)), pltpu.SemaphoreType.DMA((n_chip - 1,))]
        + _comm_sems(n_ready) + _comm_sems(1))
    outs = pl.pallas_call(
        body, name="w_in_grad_exchange", grid_spec=grid_spec,
        out_shape=[jax.ShapeDtypeStruct((r, n_col), F32), jax.ShapeDtypeStruct((n_chip - 1, r, n_col), BF16)]
        + [jax.ShapeDtypeStruct(p.shape, p.dtype) for p in ready_stacks] + _stacked([small]),
        compiler_params=_params(2),
    )(chip, a_t, b, *ready_stacks, small)
    return outs[0], outs[1], outs[2:2 + n_ready], outs[2 + n_ready]


def _proj_fwd(x1, pos, rc, kv_norm, w_dkv, ckv_norm, w_ukv, b_norm, b_w_in, q_norm, w_uq):
    t = x1.shape[0]
    tm = LATENT_ROW_TILE

    def body(x1_ref, pos_ref, rc_ref, kvn_ref, wdkv_ref, ckvn_ref, wukv_ref, bn_ref, bwin_ref, qn_ref, wuq_ref,
             ckr_ref, cq_ref, gb_ref, q_ref, k_ref, v_ref):
        ct, s1, s2 = _rope_tables(pos_ref[...], rc_ref[...])
        xv = x1_ref[...]
        xh = xv * _rstd(xv)
        ckr = _dot(xh * kvn_ref[...], wdkv_ref[...])
        ckr_ref[...] = ckr
        ckv = ckr[:, 0:KV_RANK]
        kv = _dot(ckv * _rstd(ckv) * ckvn_ref[...], wukv_ref[...])
        k_rope = _rope(ckr[:, KV_RANK:CKR_PAD], ct, s1, s2)
        lane = lax.broadcasted_iota(jnp.int32, (tm, HEAD_PAD), 1)
        low = lane < QK_NOPE
        for h in range(N_HEADS):
            kv_h = kv[:, h * HEAD_PAD:(h + 1) * HEAD_PAD]
            k_ref[:, h * HEAD_PAD:(h + 1) * HEAD_PAD] = jnp.where(low, kv_h, k_rope).astype(BF16)
            v_ref[:, h * HEAD_PAD:(h + 1) * HEAD_PAD] = jnp.where(low, pltpu.roll(kv_h, V_HEAD, 1), 1.0).astype(BF16)
        pb = _dot(xh * bn_ref[...], bwin_ref[...])
        cq = pb[:, 0:Q_RANK]
        cq_ref[...] = cq
        gb_ref[...] = pb[:, Q_RANK:Q_RANK + E_B]
        q = _dot(cq * _rstd(cq) * qn_ref[...], wuq_ref[...])
        for h in range(N_HEADS):
            q_ref[:, h * HEAD_PAD:(h + 1) * HEAD_PAD] = _rope(
                q[:, h * HEAD_PAD:(h + 1) * HEAD_PAD], ct, s1, s2).astype(BF16)

    tok = lambda width: pl.BlockSpec((tm, width), lambda i: (i, 0))
    weights = [kv_norm, w_dkv, ckv_norm, w_ukv, b_norm, b_w_in, q_norm, w_uq]
    wide = N_HEADS * HEAD_PAD
    return pl.pallas_call(
        body, name="latent_proj_fwd", grid=(t // tm,),
        in_specs=[tok(D_MODEL), tok(1), _const_spec(rc.shape)] + [_const_spec(w.shape) for w in weights],
        out_specs=[tok(CKR_PAD), tok(Q_RANK), tok(E_B), tok(wide), tok(wide), tok(wide)],
        out_shape=[jax.ShapeDtypeStruct((t, CKR_PAD), F32), jax.ShapeDtypeStruct((t, Q_RANK), F32),
                   jax.ShapeDtypeStruct((t, E_B), F32), jax.ShapeDtypeStruct((t, wide), BF16),
                   jax.ShapeDtypeStruct((t, wide), BF16), jax.ShapeDtypeStruct((t, wide), BF16)],
        compiler_params=_params(1),
    )(x1, pos, rc, *weights)


def _proj_bwd(dq, dk, dv, dgb, cq, ckr, x1, dx2, pos, rc, kv_norm, w_dkv, ckv_norm, w_ukv, b_norm, b_w_in, q_norm, w_uq):
    t = x1.shape[0]
    tm = LATENT_ROW_TILE
    wide = N_HEADS * HEAD_PAD

    def body(dq_ref, dk_ref, dv_ref, dgb_ref, cq_ref, ckr_ref, x1_ref, dx2_ref, pos_ref, rc_ref,
             kvn_ref, wdkv_ref, ckvn_ref, wukv_ref, bn_ref, bwin_ref, qn_ref, wuq_ref,
             dx1_ref, dwuq_ref, dwbin_ref, dwukv_ref, dwdkv_ref, dqn_ref, dbn_ref, dckvn_ref, dkvn_ref, dqu_ref, dkv_ref):
        @pl.when(pl.program_id(0) == 0)
        def _():
            for ref in (dwuq_ref, dwbin_ref, dwukv_ref, dwdkv_ref, dqn_ref, dbn_ref, dckvn_ref, dkvn_ref):
                ref[...] = jnp.zeros_like(ref)

        ct, s1, s2 = _rope_tables(pos_ref[...], rc_ref[...])
        lane = lax.broadcasted_iota(jnp.int32, (tm, HEAD_PAD), 1)
        low = lane < QK_NOPE
        for h in range(N_HEADS):
            dqu_ref[:, h * HEAD_PAD:(h + 1) * HEAD_PAD] = _rope(
                dq_ref[:, h * HEAD_PAD:(h + 1) * HEAD_PAD], ct, -s1, -s2).astype(BF16)
        cq = cq_ref[...]
        rq = _rstd(cq)
        cqh = cq * rq
        cqn = (cqh * qn_ref[...]).astype(BF16)
        dwuq = lax.dot_general(cqn, dqu_ref[...], TN, preferred_element_type=F32)
        for h in range(N_HEADS):
            dwuq_ref[h] += dwuq[:, h * HEAD_PAD:(h + 1) * HEAD_PAD]
        dcqn = lax.dot_general(dqu_ref[...], wuq_ref[...], NT, preferred_element_type=F32)
        dqn_ref[...] += jnp.sum(dcqn * cqh, axis=0, keepdims=True)
        dcq = _norm_bwd(dcqn * qn_ref[...], cqh, rq)
        dpb = jnp.concatenate([dcq, dgb_ref[...]], axis=1).astype(BF16)
        xv = x1_ref[...]
        r = _rstd(xv)
        xh = xv * r
        dwbin_ref[...] += _dot_tn(xh * bn_ref[...], dpb)
        dh3 = lax.dot_general(dpb, bwin_ref[...], NT, preferred_element_type=F32)
        dk_rope = jnp.zeros((tm, HEAD_PAD), F32)
        for h in range(N_HEADS):
            dk_h = dk_ref[:, h * HEAD_PAD:(h + 1) * HEAD_PAD]
            dv_h = dv_ref[:, h * HEAD_PAD:(h + 1) * HEAD_PAD]
            dkv_ref[:, h * HEAD_PAD:(h + 1) * HEAD_PAD] = jnp.where(low, dk_h, pltpu.roll(dv_h, V_HEAD, 1)).astype(BF16)
            dk_rope += dk_h
        rope_lanes = jnp.logical_and(lane >= ROPE_LO, lane < ROPE_LO + QK_ROPE)
        dk_rope = jnp.where(rope_lanes, _rope(dk_rope, ct, -s1, -s2), 0.0)
        ckv = ckr_ref[:, 0:KV_RANK]
        rk = _rstd(ckv)
        ckh = ckv * rk
        ckvn = (ckh * ckvn_ref[...]).astype(BF16)
        dwukv = lax.dot_general(ckvn, dkv_ref[...], TN, preferred_element_type=F32)
        for h in range(N_HEADS):
            dwukv_ref[h] += dwukv[:, h * HEAD_PAD:(h + 1) * HEAD_PAD]
        dckvn = lax.dot_general(dkv_ref[...], wukv_ref[...], NT, preferred_element_type=F32)
        dckvn_ref[...] += jnp.sum(dckvn * ckh, axis=0, keepdims=True)
        dckv = _norm_bwd(dckvn * ckvn_ref[...], ckh, rk)
        dckr = jnp.concatenate([dckv, dk_rope], axis=1).astype(BF16)
        dwdkv_ref[...] += _dot_tn(xh * kvn_ref[...], dckr)
        dh2 = lax.dot_general(dckr, wdkv_ref[...], NT, preferred_element_type=F32)
        dkvn_ref[...] += jnp.sum(dh2 * xh, axis=0, keepdims=True)
        dbn_ref[...] += jnp.sum(dh3 * xh, axis=0, keepdims=True)
        dx1_ref[...] = dx2_ref[...] + _norm_bwd(dh2 * kvn_ref[...] + dh3 * bn_ref[...], xh, r)

    tok = lambda width: pl.BlockSpec((tm, width), lambda i: (i, 0))
    weights = [kv_norm, w_dkv, ckv_norm, w_ukv, b_norm, b_w_in, q_norm, w_uq]
    acc_shapes = [(N_HEADS, Q_RANK, HEAD_PAD), (D_MODEL, Q_RANK + E_B), (N_HEADS, KV_RANK, HEAD_PAD), (D_MODEL, CKR_PAD),
                  (1, Q_RANK), (1, D_MODEL), (1, KV_RANK), (1, D_MODEL)]
    return pl.pallas_call(
        body, name="latent_proj_bwd", grid=(t // tm,),
        in_specs=[tok(wide), tok(wide), tok(wide), tok(E_B), tok(Q_RANK), tok(CKR_PAD), tok(D_MODEL), tok(D_MODEL), tok(1),
                  _const_spec(rc.shape)] + [_const_spec(w.shape) for w in weights],
        out_specs=[tok(D_MODEL)] + [_acc_spec(s) for s in acc_shapes],
        out_shape=[jax.ShapeDtypeStruct((t, D_MODEL), F32)] + [jax.ShapeDtypeStruct(s, F32) for s in acc_shapes],
        scratch_shapes=[pltpu.VMEM((tm, wide), BF16), pltpu.VMEM((tm, wide), BF16)],
        compiler_params=_params(1),
    )(dq, dk, dv, dgb, cq, ckr, x1, dx2, pos, rc, *weights)


def _attn_fwd(q, k, v, n_seq, seq):
    tq, tk = ATTN_FWD_TILES
    ratio = tq // tk
    nq = seq // tq
    pair = 2 * HEAD_PAD

    def body(q_ref, k_ref, v_ref, o_ref, lr_ref):
        i = pl.program_id(2)
        row = lax.broadcasted_iota(jnp.int32, (tq, tk), 0)
        col = lax.broadcasted_iota(jnp.int32, (tq, tk), 1)
        qs = [q_ref[:, hh * HEAD_PAD:(hh + 1) * HEAD_PAD] for hh in range(2)]

        def step(j, carry, shift):
            start = pl.multiple_of(j * tk, tk)
            out = []
            for hh in range(2):
                m, acc = carry[hh]
                s = lax.dot_general(qs[hh], k_ref[pl.ds(start, tk), hh * HEAD_PAD:(hh + 1) * HEAD_PAD], NT,
                                    preferred_element_type=F32) * (SOFTMAX_SCALE * LOG2_E)
                if shift is not None:
                    s = jnp.where(col + shift <= row, s, -jnp.inf)
                m_new = jnp.maximum(m, jnp.max(s, axis=-1, keepdims=True))
                p = jnp.exp2(s - m_new)
                acc = jnp.exp2(m - m_new) * acc + jnp.dot(
                    p.astype(BF16), v_ref[pl.ds(start, tk), hh * HEAD_PAD:(hh + 1) * HEAD_PAD], preferred_element_type=F32)
                out.append((m_new, acc))
            return tuple(out)

        one = (jnp.full((tq, 1), -jnp.inf, F32), jnp.zeros((tq, HEAD_PAD), F32))
        carry = lax.fori_loop(0, i * ratio, functools.partial(step, shift=None), (one, one))
        for d in range(ratio):
            carry = step(i * ratio + d, carry, d * tk)
        lane = lax.broadcasted_iota(jnp.int32, (tq, HEAD_PAD), 1)
        low = lane < V_HEAD
        halves = []
        for hh in range(2):
            m, acc = carry[hh]
            swapped = pltpu.roll(acc, V_HEAD, 1)
            halves.append(acc / swapped)
            lse = (m * LN_2) + jnp.log(jnp.where(low, swapped, acc))
            lr_ref[hh] = lse.T[0:8, :]
        o_ref[...] = jnp.where(low, halves[0], pltpu.roll(halves[1], V_HEAD, 1))

    t = n_seq * seq
    return pl.pallas_call(
        body, name="attention_fwd", grid=(n_seq, N_HEADS // 2, nq),
        in_specs=[pl.BlockSpec((tq, pair), lambda s, p, i: (s * nq + i, p)),
                  pl.BlockSpec((seq, pair), lambda s, p, i: (s, p)),
                  pl.BlockSpec((seq, pair), lambda s, p, i: (s, p))],
        out_specs=[pl.BlockSpec((tq, HEAD_PAD), lambda s, p, i: (s * nq + i, p)),
                   pl.BlockSpec((None, 2, 8, tq), lambda s, p, i: (s, p, 0, i))],
        out_shape=[jax.ShapeDtypeStruct((t, E_B), F32), jax.ShapeDtypeStruct((n_seq, N_HEADS, 8, seq), F32)],
        compiler_params=_params(3),
    )(q, k, v)


def _attn_bwd(q, k, v, do, lse_row, delta_row, n_seq, seq):
    tk, tq = ATTN_BWD_TILES
    ratio = tk // tq
    nk = seq // tk
    n_inner = seq // tq
    pair = 2 * HEAD_PAD

    def body(q_ref, k_ref, v_ref, do_ref, lr_ref, dr_ref, dk_ref, dv_ref, dq_ref):
        j = pl.program_id(2)

        @pl.when(j == 0)
        def _():
            dq_ref[...] = jnp.zeros_like(dq_ref)

        row = lax.broadcasted_iota(jnp.int32, (tk, tq), 0)
        col = lax.broadcasted_iota(jnp.int32, (tk, tq), 1)
        ks = [k_ref[:, hh * HEAD_PAD:(hh + 1) * HEAD_PAD] for hh in range(2)]
        vs = [v_ref[:, hh * HEAD_PAD:(hh + 1) * HEAD_PAD] for hh in range(2)]

        def step(i, carry, shift):
            start = pl.multiple_of(i * tq, tq)
            out = []
            for hh in range(2):
                dk_acc, dv_acc = carry[hh]
                qi = q_ref[pl.ds(start, tq), hh * HEAD_PAD:(hh + 1) * HEAD_PAD]
                doi = do_ref[pl.ds(start, tq), hh * HEAD_PAD:(hh + 1) * HEAD_PAD]
                st = lax.dot_general(ks[hh], qi, NT, preferred_element_type=F32) * SOFTMAX_SCALE
                pt = jnp.exp(st - lr_ref[hh, 0:1, pl.ds(start, tq)])
                if shift is not None:
                    pt = jnp.where(col + shift >= row, pt, 0.0)
                dv_acc = dv_acc + jnp.dot(pt.astype(BF16), doi, preferred_element_type=F32)
                dpt = lax.dot_general(vs[hh], doi, NT, preferred_element_type=F32)
                dst = (pt * (dpt - dr_ref[hh, 0:1, pl.ds(start, tq)]) * SOFTMAX_SCALE).astype(BF16)
                dk_acc = dk_acc + jnp.dot(dst, qi, preferred_element_type=F32)
                dq_ref[pl.ds(start, tq), hh * HEAD_PAD:(hh + 1) * HEAD_PAD] += lax.dot_general(
                    dst, ks[hh], TN, preferred_element_type=F32)
                out.append((dk_acc, dv_acc))
            return tuple(out)

        one = (jnp.zeros((tk, HEAD_PAD), F32), jnp.zeros((tk, HEAD_PAD), F32))
        carry = (one, one)
        for d in range(ratio):
            carry = step(j * ratio + d, carry, d * tq)
        carry = lax.fori_loop((j + 1) * ratio, n_inner, functools.partial(step, shift=None), carry)
        for hh in range(2):
            dk_ref[:, hh * HEAD_PAD:(hh + 1) * HEAD_PAD] = carry[hh][0]
            dv_ref[:, hh * HEAD_PAD:(hh + 1) * HEAD_PAD] = carry[hh][1]

    t = n_seq * seq
    wide = N_HEADS * HEAD_PAD
    return pl.pallas_call(
        body, name="attention_bwd", grid=(n_seq, N_HEADS // 2, nk),
        in_specs=[pl.BlockSpec((seq, pair), lambda s, p, j: (s, p)),
                  pl.BlockSpec((tk, pair), lambda s, p, j: (s * nk + j, p)),
                  pl.BlockSpec((tk, pair), lambda s, p, j: (s * nk + j, p)),
                  pl.BlockSpec((seq, pair), lambda s, p, j: (s, p)),
                  pl.BlockSpec((None, 2, 8, seq), lambda s, p, j: (s, p, 0, 0)),
                  pl.BlockSpec((None, 2, 8, seq), lambda s, p, j: (s, p, 0, 0))],
        out_specs=[pl.BlockSpec((tk, pair), lambda s, p, j: (s * nk + j, p)),
                   pl.BlockSpec((tk, pair), lambda s, p, j: (s * nk + j, p)),
                   pl.BlockSpec((seq, pair), lambda s, p, j: (s, p))],
        out_shape=[jax.ShapeDtypeStruct((t, wide), F32), jax.ShapeDtypeStruct((t, wide), F32),
                   jax.ShapeDtypeStruct((t, wide), F32)],
        compiler_params=_params(3),
    )(q, k, v, do, lse_row, delta_row)


def _head_tail(o, gb, x1, w_out, final_norm, target, n_seq, seq):
    tm = LATENT_ROW_TILE
    nt = seq // tm
    n_col = D_MODEL // N_DEV

    def body(o_ref, gb_ref, x1_ref, wout_ref, fn_ref, tgt_ref,
             dx2_ref, do_ref, dgb_ref, dr_ref, dwout_ref, dfn_ref, loss_ref):
        first = jnp.logical_and(pl.program_id(0) == 0, pl.program_id(1) == 0)

        @pl.when(first)
        def _():
            dwout_ref[...] = jnp.zeros_like(dwout_ref)
            dfn_ref[...] = jnp.zeros_like(dfn_ref)
            loss_ref[...] = jnp.zeros_like(loss_ref)

        ov, g = o_ref[...], gb_ref[...]
        silu, dsilu = _silu_parts(g)
        gated = (ov * silu).astype(BF16)
        x2 = x1_ref[...] + jnp.dot(gated, wout_ref[...], preferred_element_type=F32)
        r = _rstd(x2)
        xh = x2 * r
        err = xh * fn_ref[...] - tgt_ref[...]
        loss_ref[...] += 0.5 * jnp.sum(jnp.mean(err * err, axis=-1, keepdims=True), axis=0, keepdims=True)
        dy = err / D_MODEL
        dfn_ref[...] += jnp.sum(dy * xh, axis=0, keepdims=True)
        dx2 = _norm_bwd(dy * fn_ref[...], xh, r)
        dx2_ref[...] = dx2
        dx2b = dx2.astype(BF16)
        dw = lax.dot_general(gated, dx2b, TN, preferred_element_type=F32)
        for d in range(N_DEV):
            dwout_ref[d] += dw[:, d * n_col:(d + 1) * n_col]
        dgated = lax.dot_general(dx2b, wout_ref[...], NT, preferred_element_type=F32)
        do = dgated * silu
        dgb_ref[...] = dgated * ov * dsilu
        prod = do * ov
        lane = lax.broadcasted_iota(jnp.int32, (tm, HEAD_PAD), 1)
        low = lane < V_HEAD
        for p in range(N_HEADS // 2):
            blk = prod[:, p * HEAD_PAD:(p + 1) * HEAD_PAD]
            do_pair = do[:, p * HEAD_PAD:(p + 1) * HEAD_PAD]
            for hh in range(2):
                h = 2 * p + hh
                mine = do_pair if hh == 0 else pltpu.roll(do_pair, V_HEAD, 1)
                do_ref[:, h * HEAD_PAD:(h + 1) * HEAD_PAD] = jnp.where(low, mine, 0.0).astype(BF16)
                delta = jnp.sum(jnp.where(low if hh == 0 else ~low, blk, 0.0), axis=-1, keepdims=True)
                dr_ref[h] = jnp.broadcast_to(delta, (tm, HEAD_PAD)).T[0:8, :]

    tok = lambda width: pl.BlockSpec((tm, width), lambda s, i: (s * nt + i, 0))
    t = n_seq * seq
    return pl.pallas_call(
        body, name="head_tail", grid=(n_seq, nt),
        in_specs=[tok(E_B), tok(E_B), tok(D_MODEL), _const_spec(w_out.shape), _const_spec((1, D_MODEL)), tok(D_MODEL)],
        out_specs=[tok(D_MODEL), tok(N_HEADS * HEAD_PAD), tok(E_B),
                   pl.BlockSpec((None, N_HEADS, 8, tm), lambda s, i: (s, 0, 0, i)),
                   _acc_spec((N_DEV, E_B, n_col)), _acc_spec((1, D_MODEL)), _acc_spec((1, 1))],
        out_shape=[jax.ShapeDtypeStruct((t, D_MODEL), F32), jax.ShapeDtypeStruct((t, N_HEADS * HEAD_PAD), BF16),
                   jax.ShapeDtypeStruct((t, E_B), F32),
                   jax.ShapeDtypeStruct((n_seq, N_HEADS, 8, seq), F32), jax.ShapeDtypeStruct((N_DEV, E_B, n_col), F32),
                   jax.ShapeDtypeStruct((1, D_MODEL), F32), jax.ShapeDtypeStruct((1, 1), F32)],
        compiler_params=_params(2),
    )(o, gb, x1, w_out, final_norm, target)


def _adamw_math(w, g, m, v):
    m = ADAM_B1 * m + (1.0 - ADAM_B1) * g
    v = ADAM_B2 * v + (1.0 - ADAM_B2) * jnp.square(g)
    m_hat = m / (1.0 - ADAM_B1 ** ADAM_STEP)
    v_hat = v / (1.0 - ADAM_B2 ** ADAM_STEP)
    delta = -ADAM_LR * (m_hat / (jnp.sqrt(v_hat) + ADAM_EPS) + ADAM_WD * w)
    return delta, m, v


def _adamw_reduce(own, parts, w, m, v):
    rows, cols = w.shape
    br = 256 if rows % 256 == 0 else 128
    n_parts = parts.shape[0]

    def body(*refs):
        p_ref, w_ref, m_ref, v_ref, g_ref, d_ref, nm_ref, nv_ref = refs[-8:]
        g = p_ref[0].astype(F32) if own is None else refs[0][...] + p_ref[0].astype(F32)
        for k in range(1, n_parts):
            g = g + p_ref[k].astype(F32)
        g_ref[...] = g
        d_ref[...], nm_ref[...], nv_ref[...] = _adamw_math(w_ref[...], g, m_ref[...], v_ref[...])

    blk = pl.BlockSpec((br, cols), lambda i: (i, 0))
    first = [] if own is None else [own]
    return pl.pallas_call(
        body, name="adamw_reduce", grid=(rows // br,),
        in_specs=[blk] * len(first) + [pl.BlockSpec((n_parts, br, cols), lambda i: (0, i, 0)), blk, blk, blk],
        out_specs=[blk] * 4, out_shape=[jax.ShapeDtypeStruct((rows, cols), F32)] * 4,
        compiler_params=_params(1),
    )(*first, parts, w, m, v)


def _sum_parts(parts):
    def body(p_ref, o_ref):
        g = p_ref[0]
        for k in range(1, N_DEV):
            g = g + p_ref[k]
        o_ref[...] = g

    return pl.pallas_call(body, name="small_grad_sum", out_shape=jax.ShapeDtypeStruct(parts.shape[1:], F32))(parts)


def _adamw_small(gs, ws, ms, vs):
    n = len(gs)

    def body(*refs):
        ins, outs = refs[:4 * n], refs[4 * n:]
        for j in range(n):
            g_ref, w_ref, m_ref, v_ref = ins[j], ins[n + j], ins[2 * n + j], ins[3 * n + j]
            outs[3 * j][...], outs[3 * j + 1][...], outs[3 * j + 2][...] = _adamw_math(
                w_ref[...], g_ref[...], m_ref[...], v_ref[...])

    shapes = [jax.ShapeDtypeStruct(g.shape, F32) for g in gs for _ in range(3)]
    flat = pl.pallas_call(body, name="adamw_small", out_shape=shapes)(*gs, *ws, *ms, *vs)
    return [tuple(flat[3 * j:3 * j + 3]) for j in range(n)]


SMALL_ROWS = 8
LOSS_LANE = D_MODEL - 1


def _pack_small(g_fn, g_kvn, g_bn, g_an, g_ckvn, g_qn, g_conv, loss_part):
    def body(fn_ref, kvn_ref, bn_ref, an_ref, ckvn_ref, qn_ref, conv_ref, loss_ref, o_ref):
        o_ref[0:1, :] = fn_ref[...]
        o_ref[1:2, :] = kvn_ref[...]
        o_ref[2:3, :] = bn_ref[...]
        o_ref[3:4, :] = an_ref[...]
        lane = lax.broadcasted_iota(jnp.int32, (1, D_MODEL), 1)
        o_ref[4:5, :] = jnp.where(lane == LOSS_LANE, loss_ref[...], 0.0)
        o_ref[4:5, 0:KV_RANK] = ckvn_ref[...]
        o_ref[4:5, KV_RANK:KV_RANK + Q_RANK] = qn_ref[...]
        o_ref[5:8, :] = conv_ref[0:3, :]

    return pl.pallas_call(body, name="pack_small", out_shape=jax.ShapeDtypeStruct((SMALL_ROWS, D_MODEL), F32))(
        g_fn, g_kvn, g_bn, g_an, g_ckvn, g_qn, g_conv, loss_part)


def _pad_cols(a, width):
    return jnp.pad(a, ((0, 0), (0, width - a.shape[1])))


def _dkv_to_padded(a):
    r = a.shape[0]
    z = jnp.zeros((r, ROPE_LO), a.dtype)
    z2 = jnp.zeros((r, HEAD_PAD - ROPE_LO - QK_ROPE), a.dtype)
    return jnp.concatenate([a[:, :KV_RANK], z, a[:, KV_RANK:], z2], axis=1)


def _dkv_from_padded(a):
    return jnp.concatenate([a[:, :KV_RANK], a[:, KV_RANK + ROPE_LO:KV_RANK + ROPE_LO + QK_ROPE]], axis=1)


def _unstack_cols(a):
    return jnp.transpose(a, (1, 0, 2)).reshape(a.shape[1], N_DEV * a.shape[2])


def kernel(x, positions, a_norm, a_w_in, a_conv, a_w_out, kv_norm, w_dkv, ckv_norm, w_ukv, b_norm, b_w_in, b_q_norm, b_w_uq, b_w_out, final_norm, loss_target, m_a_norm, m_a_w_in, m_a_conv, m_a_w_out, m_kv_norm, m_w_dkv, m_ckv_norm, m_w_ukv, m_b_norm, m_b_w_in, m_b_q_norm, m_b_w_uq, m_b_w_out, m_final_norm, v_a_norm, v_a_w_in, v_a_conv, v_a_w_out, v_kv_norm, v_w_dkv, v_ckv_norm, v_w_ukv, v_b_norm, v_b_w_in, v_b_q_norm, v_b_w_uq, v_b_w_out, v_final_norm):
    n_seq, seq, _ = x.shape
    t = n_seq * seq
    me = 4 * lax.axis_index("x") + 2 * lax.axis_index("y") + lax.axis_index("c")

    big = {
        "a_w_in": (a_w_in[0], m_a_w_in[0], v_a_w_in[0]),
        "a_w_out": (a_w_out[0], m_a_w_out[0], v_a_w_out[0]),
        "w_dkv": tuple(_dkv_to_padded(a) for a in (w_dkv, m_w_dkv, v_w_dkv)),
        "w_ukv": (w_ukv, m_w_ukv, v_w_ukv),
        "b_w_in": (b_w_in[0], m_b_w_in[0], v_b_w_in[0]),
        "b_w_uq": tuple(_pad_cols(a[0], HEAD_PAD) for a in (b_w_uq, m_b_w_uq, v_b_w_uq)),
        "b_w_out": (b_w_out[0], m_b_w_out[0], v_b_w_out[0]),
    }
    names = list(big)
    first_names, later_names = names[:2], names[2:]
    gathered = _all_gather([big[n][0].astype(BF16) for n in first_names] + [a_norm, a_conv[0]])
    a_norm_f = gathered[2].reshape(1, D_MODEL)
    a_conv_f = _unstack_cols(gathered[3])
    w_a_in = gathered[0]
    w_a_out = gathered[1].reshape(D_MODEL, D_MODEL)

    x2d = x.reshape(t, D_MODEL)
    tgt = loss_target.reshape(t, D_MODEL)
    pos = positions.astype(F32).reshape(t, 1)
    rc = _rope_consts()
    kvn, ckvn, bn, qn, fn = kv_norm.reshape(1, -1), ckv_norm.reshape(1, -1), b_norm, b_q_norm, final_norm.reshape(1, -1)

    x1, proj, conv, later = _conv_fwd(x2d, a_norm_f, w_a_in, a_conv_f, w_a_out, n_seq, seq,
                                      [big[n][0].astype(BF16) for n in later_names])
    full = dict(zip(later_names, later))
    w_dkv_f = full["w_dkv"].reshape(D_MODEL, CKR_PAD)
    w_ukv_f = _unstack_cols(full["w_ukv"])
    w_b_in = full["b_w_in"].reshape(D_MODEL, Q_RANK + E_B)
    w_uq_f = _unstack_cols(full["b_w_uq"])
    w_b_out = _unstack_cols(full["b_w_out"])
    layer_b = (kvn, w_dkv_f, ckvn, w_ukv_f, bn, w_b_in, qn, w_uq_f)
    ckr, cq, gb, q, k, v = _proj_fwd(x1, pos, rc, *layer_b)
    o, lse_row = _attn_fwd(q, k, v, n_seq, seq)
    dx2, do, dgb, delta_row, g_b_out, g_fn, loss_part = _head_tail(o, gb, x1, w_b_out, fn, tgt, n_seq, seq)
    dk, dv, dq = _attn_bwd(q, k, v, do, lse_row, delta_row, n_seq, seq)
    dx1, g_uq, g_b_in, g_ukv, g_dkv, g_qn, g_bn, g_ckvn, g_kvn = _proj_bwd(
        dq, dk, dv, dgb, cq, ckr, x1, dx2, pos, rc, *layer_b)
    stacks = {
        "w_dkv": g_dkv.reshape(N_DEV, D_MODEL // N_DEV, CKR_PAD),
        "w_ukv": g_ukv,
        "b_w_in": g_b_in.reshape(N_DEV, D_MODEL // N_DEV, Q_RANK + E_B),
        "b_w_uq": g_uq,
        "b_w_out": g_b_out,
    }
    (dx, h_a, dproj, g_a_out, g_an, g_conv), later_parts = _conv_bwd(
        dx1, x2d, proj, conv, a_norm_f, w_a_in, a_conv_f, w_a_out, n_seq, seq, [stacks[n] for n in later_names])
    parts = dict(zip(later_names, later_parts))
    small = _pack_small(g_fn, g_kvn, g_bn, g_an, g_ckvn, g_qn, g_conv, loss_part)
    a_in_own, a_in_got, (parts["a_w_out"],), small_parts = _w_in_grad_exchange(
        h_a, dproj, D_MODEL * 4 // N_DEV, (me // 2).reshape(1).astype(jnp.int32),
        [g_a_out.reshape(N_DEV, D_MODEL // N_DEV, D_MODEL)], small)

    outs = {"a_w_in": _adamw_reduce(a_in_own, a_in_got, *big["a_w_in"])}
    for n in names[1:]:
        outs[n] = _adamw_reduce(None, parts[n], *big[n])
    outs["w_dkv"] = tuple(_dkv_from_padded(a) for a in outs["w_dkv"])
    outs["b_w_uq"] = tuple(a[:, :QK_NOPE + QK_ROPE] for a in outs["b_w_uq"])
    for n in ("a_w_in", "a_w_out", "b_w_in", "b_w_uq", "b_w_out"):
        outs[n] = tuple(a[None] for a in outs[n])

    total = _sum_parts(small_parts)
    loss = total[4, LOSS_LANE]
    shard = D_MODEL // N_DEV
    g_small = {
        "final_norm": total[0], "kv_norm": total[1], "b_norm": total[2:3],
        "a_norm": lax.dynamic_slice_in_dim(total[3:4], me * shard, shard, axis=1),
        "ckv_norm": total[4, 0:KV_RANK], "b_q_norm": total[4:5, KV_RANK:KV_RANK + Q_RANK],
        "a_conv": lax.dynamic_slice_in_dim(total[5:8], me * shard, shard, axis=1)[None],
    }
    small_state = {
        "final_norm": (final_norm, m_final_norm, v_final_norm), "kv_norm": (kv_norm, m_kv_norm, v_kv_norm),
        "b_norm": (b_norm, m_b_norm, v_b_norm), "a_norm": (a_norm, m_a_norm, v_a_norm),
        "ckv_norm": (ckv_norm, m_ckv_norm, v_ckv_norm), "b_q_norm": (b_q_norm, m_b_q_norm, v_b_q_norm),
        "a_conv": (a_conv, m_a_conv, v_a_conv),
    }
    small_names = list(g_small)
    as2d = lambda a: a.reshape(-1, a.shape[-1])
    upd = _adamw_small([as2d(g_small[n]) for n in small_names],
                       *[[as2d(small_state[n][j]) for n in small_names] for j in range(3)])
    for n, u in zip(small_names, upd):
        outs[n] = (g_small[n],) + tuple(a.reshape(g_small[n].shape) for a in u)

    order = ["a_norm", "a_w_in", "a_conv", "a_w_out", "kv_norm", "w_dkv", "ckv_norm", "w_ukv", "b_norm", "b_w_in",
             "b_q_norm", "b_w_uq", "b_w_out", "final_norm"]
    result = [loss, dx.reshape(n_seq, seq, D_MODEL)]
    for j in range(4):
        result += [outs[n][j] for n in order]
    return tuple(result)
```

```python
import functools
import math

import numpy as np
import jax
import jax.numpy as jnp
from jax import lax
from jax.experimental import pallas as pl
from jax.experimental.pallas import tpu as pltpu

F32 = jnp.float32
BF16 = jnp.bfloat16

D_MODEL = 1024
N_HEADS = 8
QK_NOPE = 64
QK_ROPE = 32
V_HEAD = 64
KV_RANK = 256
Q_RANK = 384
E_B = N_HEADS * V_HEAD
HEAD_PAD = 128
CKR_PAD = KV_RANK + HEAD_PAD
ROPE_LO = QK_NOPE
ROPE_HALF = QK_ROPE // 2
ROPE_THETA = 10000.0
SOFTMAX_SCALE = 1.0 / math.sqrt(QK_NOPE + QK_ROPE)
LOG2_E = math.log2(math.e)
LN_2 = math.log(2.0)
EPS = 1e-6
N_DEV = 8

ADAM_LR = 0.001
ADAM_B1 = 0.9
ADAM_B2 = 0.999
ADAM_EPS = 1e-08
ADAM_WD = 0.01
ADAM_STEP = 10

ROW_TILE = 256
LATENT_ROW_TILE = 512
PICKUP_BLOCK = 3
ATTN_FWD_TILES = (512, 512)
ATTN_BWD_TILES = (512, 512)
VMEM_LIMIT = 56 * 1024 * 1024

MESH = pl.DeviceIdType.MESH
NT = (((1,), (1,)), ((), ()))
TN = (((0,), (0,)), ((), ()))


def _dot(a, b):
    return jnp.dot(a.astype(BF16), b.astype(BF16), preferred_element_type=F32)


def _dot_nt(a, b):
    return lax.dot_general(a.astype(BF16), b.astype(BF16), NT, preferred_element_type=F32)


def _dot_tn(a, b):
    return lax.dot_general(a.astype(BF16), b.astype(BF16), TN, preferred_element_type=F32)


def _rstd(x):
    return lax.rsqrt(jnp.mean(x * x, axis=-1, keepdims=True) + EPS)


def _norm_bwd(a, xh, r):
    return r * (a - xh * jnp.mean(a * xh, axis=-1, keepdims=True))


def _silu_parts(g):
    sg = jax.nn.sigmoid(g)
    return g * sg, sg * (1.0 + g * (1.0 - sg))


def _rope_consts():
    inv = (ROPE_THETA ** (-np.arange(0, QK_ROPE, 2, dtype=np.float32) / QK_ROPE)).astype(np.float32)
    t = np.zeros((8, HEAD_PAD), np.float32)
    t[0, ROPE_LO:ROPE_LO + ROPE_HALF] = inv
    t[0, ROPE_LO + ROPE_HALF:ROPE_LO + QK_ROPE] = inv
    t[1, ROPE_LO:ROPE_LO + ROPE_HALF] = -1.0
    t[2, ROPE_LO + ROPE_HALF:ROPE_LO + QK_ROPE] = 1.0
    return jnp.asarray(t)


def _rope_angles(pos, rc):
    ang = pos * rc[0:1, :]
    return jnp.cos(ang), jnp.sin(ang)


def _rope_tables(cosv, sinv, rc):
    return cosv, sinv * rc[1:2, :], sinv * rc[2:3, :]


def _rope(x, ct, s1, s2):
    up = pltpu.roll(x, HEAD_PAD - ROPE_HALF, 1)
    dn = pltpu.roll(x, ROPE_HALF, 1)
    return x * ct + up * s1 + dn * s2


def _const_spec(shape):
    nd = len(shape)
    return pl.BlockSpec(shape, lambda *_: (0,) * nd, pipeline_mode=pl.Buffered(1))


def _acc_spec(shape):
    nd = len(shape)
    return pl.BlockSpec(shape, lambda *_: (0,) * nd)


def _params(n_axes):
    return pltpu.CompilerParams(dimension_semantics=("arbitrary",) * n_axes, vmem_limit_bytes=VMEM_LIMIT)


def _place():
    x, y, c = lax.axis_index("x"), lax.axis_index("y"), lax.axis_index("c")
    return x, y, c, 4 * x + 2 * y + c


def _peer(x, y, c, mask):
    px = 1 - x if mask & 4 else x
    py = 1 - y if mask & 2 else y
    pc = 1 - c if mask & 1 else c
    return (px, py, pc), 4 * px + 2 * py + pc


ANY_SPEC = pl.BlockSpec(memory_space=pl.ANY)


def _comm_sems(n):
    return [pltpu.SemaphoreType.DMA((n, N_DEV - 1)), pltpu.SemaphoreType.DMA((n, N_DEV - 1)), pltpu.SemaphoreType.DMA((n,))]


def _gather_copies(ins, outs, sems):
    send_sems, recv_sems, local_sems = sems
    x, y, c, me = _place()
    starts, waits = [], []
    for w in range(len(ins)):
        mine = pltpu.make_async_copy(ins[w], outs[w].at[me], local_sems.at[w])
        starts.append(mine)
        waits.append(mine)
        for mask in range(1, N_DEV):
            peer, peer_idx = _peer(x, y, c, mask)
            starts.append(pltpu.make_async_remote_copy(
                src_ref=ins[w], dst_ref=outs[w].at[me], send_sem=send_sems.at[w, mask - 1],
                recv_sem=recv_sems.at[w, mask - 1], device_id=peer, device_id_type=MESH))
            waits.append(pltpu.make_async_remote_copy(
                src_ref=ins[w], dst_ref=outs[w].at[peer_idx], send_sem=send_sems.at[w, mask - 1],
                recv_sem=recv_sems.at[w, mask - 1], device_id=peer, device_id_type=MESH))
    return starts, waits


def _scatter_copies(ins, outs, sems):
    send_sems, recv_sems, local_sems = sems
    x, y, c, me = _place()
    copies = []
    for w in range(len(ins)):
        copies.append(pltpu.make_async_copy(ins[w].at[me], outs[w].at[0], local_sems.at[w]))
        for mask in range(1, N_DEV):
            peer, peer_idx = _peer(x, y, c, mask)
            copies.append(pltpu.make_async_remote_copy(
                src_ref=ins[w].at[peer_idx], dst_ref=outs[w].at[mask], send_sem=send_sems.at[w, mask - 1],
                recv_sem=recv_sems.at[w, mask - 1], device_id=peer, device_id_type=MESH))
    return copies, copies


def _stacked(arrays):
    return [jax.ShapeDtypeStruct((N_DEV,) + a.shape, a.dtype) for a in arrays]


def _all_gather(shards):
    n = len(shards)

    def body(*refs):
        ins, outs = refs[:n], refs[n:2 * n]
        send_sems, recv_sems, local_sems = refs[2 * n:]
        x, y, c, me = _place()
        sibling = (x, y, 1 - c)
        chips = [(1 - x, y), (x, 1 - y), (1 - x, 1 - y)]

        def copy(w, k, block, to, src=None):
            idx = 4 * block[0] + 2 * block[1] + block[2]
            return pltpu.make_async_remote_copy(
                src_ref=outs[w].at[idx] if src is None else src, dst_ref=outs[w].at[idx],
                send_sem=send_sems.at[w, k], recv_sem=recv_sems.at[w, k], device_id=to, device_id_type=MESH)

        local, sent = [], []
        for w in range(n):
            mine = pltpu.make_async_copy(ins[w], outs[w].at[me], local_sems.at[w])
            mine.start()
            local.append(mine)
            first = [copy(w, 0, (x, y, c), sibling, src=ins[w])]
            first += [copy(w, 1 + j, (x, y, c), (*chip, c), src=ins[w]) for j, chip in enumerate(chips)]
            for cp in first:
                cp.start()
            sent += first
        for w in range(n):
            for j, chip in enumerate(chips):
                copy(w, 1 + j, (*chip, c), (x, y, c)).wait_recv()
                onward = copy(w, 4 + j, (*chip, c), sibling)
                onward.start()
                sent.append(onward)
        for w in range(n):
            copy(w, 0, sibling, (x, y, c)).wait_recv()
            for j, chip in enumerate(chips):
                copy(w, 4 + j, (*chip, 1 - c), (x, y, c)).wait_recv()
        for cp in sent:
            cp.wait_send()
        for cp in local:
            cp.wait()

    return pl.pallas_call(
        body, name="weight_all_gather", out_shape=_stacked(shards),
        in_specs=[ANY_SPEC] * n, out_specs=[ANY_SPEC] * n, scratch_shapes=_comm_sems(n),
    )(*shards)


def _conv_fwd(x, a_norm, w_in, conv_w, w_out, n_seq, seq, later_shards):
    tm = ROW_TILE
    nt = seq // tm
    n_col = w_in.shape[2]
    n_later = len(later_shards)

    def body(x_ref, an_ref, win_ref, cw_ref, wout_ref, *rest):
        shard_refs, rest = rest[:n_later], rest[n_later:]
        x1_ref, proj_ref, conv_ref = rest[:3]
        stack_refs, rest = rest[3:3 + n_later], rest[3 + n_later:]
        prev_ref, sems = rest[0], rest[1:]
        step = pl.program_id(0) * nt + pl.program_id(1)

        @pl.when(step == 0)
        def _():
            for cp in _gather_copies(shard_refs, stack_refs, sems)[0]:
                cp.start()

        @pl.when(pl.program_id(1) == 0)
        def _():
            prev_ref[...] = jnp.zeros_like(prev_ref)

        xv = x_ref[...]
        h = (xv * _rstd(xv) * an_ref[...]).astype(BF16)
        for d in range(N_DEV):
            proj_ref[:, d * n_col:(d + 1) * n_col] = jnp.dot(h, win_ref[d], preferred_element_type=F32)
        b = proj_ref[:, 0:D_MODEL]
        v = proj_ref[:, D_MODEL:2 * D_MODEL] * proj_ref[:, 2 * D_MODEL:3 * D_MODEL]
        g = proj_ref[:, 3 * D_MODEL:4 * D_MODEL]
        w0, w1, w2 = cw_ref[0:1, :], cw_ref[1:2, :], cw_ref[2:3, :]
        conv_ref[...] = w0 * pltpu.roll(v, 2, 0) + w1 * pltpu.roll(v, 1, 0) + w2 * v
        rows = lax.broadcasted_iota(jnp.int32, (8, D_MODEL), 0)
        p8, v8 = prev_ref[...], v[0:8]
        back1 = jnp.where(rows < 1, pltpu.roll(p8, 1, 0), pltpu.roll(v8, 1, 0))
        back2 = jnp.where(rows < 2, pltpu.roll(p8, 2, 0), pltpu.roll(v8, 2, 0))
        conv_ref[0:8, :] = w0 * back2 + w1 * back1 + w2 * v8
        prev_ref[...] = v[tm - 8:tm]
        silu, _ = _silu_parts(g)
        yv = silu * b * conv_ref[...]
        x1_ref[...] = xv + _dot(yv, wout_ref[...])

        @pl.when(step == n_seq * nt - 1)
        def _():
            for cp in _gather_copies(shard_refs, stack_refs, sems)[1]:
                cp.wait()

    tok = lambda width: pl.BlockSpec((tm, width), lambda s, i: (s * nt + i, 0))
    t = n_seq * seq
    outs = pl.pallas_call(
        body, name="conv_mixer_fwd", grid=(n_seq, nt),
        in_specs=[tok(D_MODEL), _const_spec((1, D_MODEL)), _const_spec(w_in.shape), _const_spec((3, D_MODEL)),
                  _const_spec(w_out.shape)] + [ANY_SPEC] * n_later,
        out_specs=[tok(D_MODEL), tok(4 * D_MODEL), tok(D_MODEL)] + [ANY_SPEC] * n_later,
        out_shape=[jax.ShapeDtypeStruct((t, D_MODEL), F32), jax.ShapeDtypeStruct((t, 4 * D_MODEL), F32),
                   jax.ShapeDtypeStruct((t, D_MODEL), F32)] + _stacked(later_shards),
        scratch_shapes=[pltpu.VMEM((8, D_MODEL), F32)] + _comm_sems(n_later),
        compiler_params=_params(2),
    )(x, a_norm, w_in, conv_w, w_out, *later_shards)
    return outs[0], outs[1], outs[2], outs[3:]


def _conv_bwd(dx1, x, proj, conv, a_norm, w_in, conv_w, w_out, n_seq, seq, ready_stacks):
    tm = ROW_TILE
    nt = seq // tm
    n_col = w_in.shape[2]
    n_ready = len(ready_stacks)

    def body(dx1_ref, x_ref, proj_ref, conv_ref, an_ref, win_ref, cw_ref, wout_ref, *rest):
        ready_refs, rest = rest[:n_ready], rest[n_ready:]
        dx_ref, h_ref, dproj_ref, dwout_ref, dan_ref, dcw_ref = rest[:6]
        part_refs, rest = rest[6:6 + n_ready], rest[6 + n_ready:]
        next_ref, d1_ref, d2_ref = rest[:3]
        sems = rest[3:]
        step = pl.program_id(0) * nt + pl.program_id(1)
        first = step == 0

        @pl.when(first)
        def _():
            for cp in _scatter_copies(ready_refs, part_refs, sems)[0]:
                cp.start()
            dwout_ref[...] = jnp.zeros_like(dwout_ref)
            dan_ref[...] = jnp.zeros_like(dan_ref)
            dcw_ref[...] = jnp.zeros_like(dcw_ref)

        @pl.when(pl.program_id(1) == 0)
        def _():
            next_ref[...] = jnp.zeros_like(next_ref)

        dx1v = dx1_ref[...]
        dy = _dot_nt(dx1v, wout_ref[...])
        b = proj_ref[:, 0:D_MODEL]
        cc = proj_ref[:, D_MODEL:2 * D_MODEL]
        u = proj_ref[:, 2 * D_MODEL:3 * D_MODEL]
        g = proj_ref[:, 3 * D_MODEL:4 * D_MODEL]
        cv = conv_ref[...]
        silu, dsilu = _silu_parts(g)
        per_part = D_MODEL // n_col

        def back_through_w_in(part, grad):
            grad = grad.astype(BF16)
            dproj_ref[:, part * D_MODEL:(part + 1) * D_MODEL] = grad
            term = None
            for j in range(per_part):
                d = part * per_part + j
                piece = lax.dot_general(grad[:, j * n_col:(j + 1) * n_col], win_ref[d], NT, preferred_element_type=F32)
                term = piece if term is None else term + piece
            return term

        dh = back_through_w_in(0, dy * silu * cv)
        dh += back_through_w_in(3, dy * b * cv * dsilu)
        dwout_ref[...] += _dot_tn(silu * b * cv, dx1v)
        dconv = dy * silu * b
        d1_ref[...] = pltpu.roll(dconv, tm - 1, 0)
        d2_ref[...] = pltpu.roll(dconv, tm - 2, 0)
        rows = lax.broadcasted_iota(jnp.int32, (8, D_MODEL), 0)
        n8, c8 = next_ref[...], dconv[tm - 8:tm]
        d1_ref[tm - 8:tm, :] = jnp.where(rows >= 7, pltpu.roll(n8, 7, 0), pltpu.roll(c8, 7, 0))
        d2_ref[tm - 8:tm, :] = jnp.where(rows >= 6, pltpu.roll(n8, 6, 0), pltpu.roll(c8, 6, 0))
        next_ref[...] = dconv[0:8]
        d1, d2 = d1_ref[...], d2_ref[...]
        v = cc * u
        dcw_ref[0:1, :] += jnp.sum(d2 * v, axis=0, keepdims=True)
        dcw_ref[1:2, :] += jnp.sum(d1 * v, axis=0, keepdims=True)
        dcw_ref[2:3, :] += jnp.sum(dconv * v, axis=0, keepdims=True)
        dv = cw_ref[0:1, :] * d2 + cw_ref[1:2, :] * d1 + cw_ref[2:3, :] * dconv
        dh += back_through_w_in(1, dv * u)
        dh += back_through_w_in(2, dv * cc)
        xv = x_ref[...]
        r = _rstd(xv)
        xh = xv * r
        h_ref[...] = (xh * an_ref[...]).T.astype(BF16)
        dan_ref[...] += jnp.sum(dh * xh, axis=0, keepdims=True)
        dx_ref[...] = dx1v + _norm_bwd(dh * an_ref[...], xh, r)

        @pl.when(step == n_seq * nt - 1)
        def _():
            for cp in _scatter_copies(ready_refs, part_refs, sems)[1]:
                cp.wait()

    tok = lambda width: pl.BlockSpec((tm, width), lambda s, i: (s * nt + nt - 1 - i, 0))
    t = n_seq * seq
    outs = pl.pallas_call(
        body, name="conv_mixer_bwd", grid=(n_seq, nt),
        in_specs=[tok(D_MODEL), tok(D_MODEL), tok(4 * D_MODEL), tok(D_MODEL), _const_spec((1, D_MODEL)),
                  _const_spec(w_in.shape), _const_spec((3, D_MODEL)), _const_spec(w_out.shape)] + [ANY_SPEC] * n_ready,
        out_specs=[tok(D_MODEL), pl.BlockSpec((D_MODEL, tm), lambda s, i: (0, s * nt + nt - 1 - i)),
                   tok(4 * D_MODEL), _acc_spec((D_MODEL, D_MODEL)),
                   _acc_spec((1, D_MODEL)), _acc_spec((8, D_MODEL))] + [ANY_SPEC] * n_ready,
        out_shape=[jax.ShapeDtypeStruct((t, D_MODEL), F32), jax.ShapeDtypeStruct((D_MODEL, t), BF16),
                   jax.ShapeDtypeStruct((t, 4 * D_MODEL), BF16), jax.ShapeDtypeStruct((D_MODEL, D_MODEL), F32),
                   jax.ShapeDtypeStruct((1, D_MODEL), F32), jax.ShapeDtypeStruct((8, D_MODEL), F32)]
        + [jax.ShapeDtypeStruct(a.shape, a.dtype) for a in ready_stacks],
        scratch_shapes=[pltpu.VMEM((8, D_MODEL), F32), pltpu.VMEM((tm, D_MODEL), F32), pltpu.VMEM((tm, D_MODEL), F32)]
        + _comm_sems(n_ready),
        compiler_params=_params(2),
    )(dx1, x, proj, conv, a_norm, w_in, conv_w, w_out, *ready_stacks)
    return outs[:6], outs[6:]


def _w_in_grad_exchange(a_t, b, n_col, chip, ready_stacks, small):
    r, t = a_t.shape
    bt = 1024
    nk = t // bt
    n_ready = len(ready_stacks)
    n_chip = N_DEV // 2
    n_remote = N_DEV - 2

    def body(chip_ref, a_ref, b_ref, *rest):
        ready_refs, small_ref, rest = rest[:n_ready], rest[n_ready], rest[n_ready + 1:]
        own_ref, got_ref, rest = rest[0], rest[1], rest[2:]
        part_refs, small_all_ref, rest = rest[:n_ready], rest[n_ready], rest[n_ready + 1:]
        acc_ref, land_ref, stage_ref, pair_send, pair_recv, chip_send, chip_recv = rest[:7]
        scatter_sems, gather_sems = rest[7:10], rest[10:13]
        s, k = pl.program_id(0), pl.program_id(1)
        x, y, c, _ = _place()
        chip = 2 * x + y

        @pl.when(jnp.logical_and(s == 0, k == 0))
        def _():
            for cp in _scatter_copies(ready_refs, part_refs, scatter_sems)[0]:
                cp.start()
            for cp in _gather_copies([small_ref], [small_all_ref], gather_sems)[0]:
                cp.start()

        for parity in range(2):
            @pl.when(s % 2 == parity)
            def _(parity=parity):
                @pl.when(k == 0)
                def _():
                    acc_ref[parity] = jnp.zeros((r, n_col), F32)

                acc_ref[parity] += jnp.dot(a_ref[...], b_ref[...], preferred_element_type=F32)

        def to_sibling(step):
            return pltpu.make_async_remote_copy(
                src_ref=acc_ref.at[step % 2], dst_ref=land_ref, send_sem=pair_send.at[step], recv_sem=pair_recv.at[step],
                device_id=(x, y, 1 - c), device_id_type=MESH)

        def to_owner(nth):
            owner_chip = (chip + 1 + nth) % n_chip
            slot = jnp.bitwise_xor(chip, owner_chip) - 1
            return pltpu.make_async_remote_copy(
                src_ref=stage_ref.at[nth % 2], dst_ref=got_ref.at[slot], send_sem=chip_send.at[nth],
                recv_sem=chip_recv.at[slot], device_id=(owner_chip // 2, owner_chip % 2, c), device_id_type=MESH)

        for step in range(N_DEV):
            owner_core = step % 2

            @pl.when(jnp.logical_and(s == step, k == nk - 1))
            def _(step=step, owner_core=owner_core):
                @pl.when(c != owner_core)
                def _():
                    to_sibling(step).start()

                if step >= 1:
                    @pl.when(c == owner_core)
                    def _():
                        to_sibling(step - 1).wait_send()

            if step < N_DEV - 1:
                pickup = jnp.logical_and(s == step + 1, k == min(PICKUP_BLOCK, nk - 1))
            else:
                pickup = jnp.logical_and(s == step, k == nk - 1)

            @pl.when(jnp.logical_and(pickup, c == owner_core))
            def _(step=step):
                to_sibling(step).wait_recv()
                total = acc_ref[step % 2] + land_ref[...]
                if step < n_remote:
                    nth = step // 2
                    if nth >= 2:
                        to_owner(nth - 2).wait_send()
                    stage_ref[nth % 2] = total.astype(BF16)
                    to_owner(nth).start()
                else:
                    own_ref[...] = total

        @pl.when(jnp.logical_and(s == N_DEV - 1, k == nk - 1))
        def _():
            @pl.when(c != (N_DEV - 1) % 2)
            def _():
                to_sibling(N_DEV - 1).wait_send()

            to_owner(1).wait_send()
            to_owner(2).wait_send()
            for slot in range(n_chip - 1):
                pltpu.make_async_remote_copy(
                    src_ref=stage_ref.at[0], dst_ref=got_ref.at[slot], send_sem=chip_send.at[0],
                    recv_sem=chip_recv.at[slot], device_id=(x, y, c), device_id_type=MESH).wait_recv()
            for cp in _scatter_copies(ready_refs, part_refs, scatter_sems)[1]:
                cp.wait()
            for cp in _gather_copies([small_ref], [small_all_ref], gather_sems)[1]:
                cp.wait()

    def owner_block(s, k, chip):
        return (k, 2 * ((chip[0] + 1 + s // 2) % n_chip) + s % 2)

    grid_spec = pltpu.PrefetchScalarGridSpec(
        num_scalar_prefetch=1, grid=(N_DEV, nk),
        in_specs=[pl.BlockSpec((r, bt), lambda s, k, chip: (0, k)), pl.BlockSpec((bt, n_col), owner_block)]
        + [ANY_SPEC] * (n_ready + 1),
        out_specs=[pl.BlockSpec((r, n_col), lambda s, k, chip: (0, 0)), ANY_SPEC] + [ANY_SPEC] * (n_ready + 1),
        scratch_shapes=[pltpu.VMEM((2, r, n_col), F32), pltpu.VMEM((r, n_col), F32), pltpu.VMEM((2, r, n_col), BF16),
                        pltpu.SemaphoreType.DMA((N_DEV,)), pltpu.SemaphoreType.DMA((N_DEV,)),
                        pltpu.SemaphoreType.DMA((n_chip - 1,)), pltpu.SemaphoreType.DMA((n_chip - 1,))]
        + _comm_sems(n_ready) + _comm_sems(1))
    outs = pl.pallas_call(
        body, name="w_in_grad_exchange", grid_spec=grid_spec,
        out_shape=[jax.ShapeDtypeStruct((r, n_col), F32), jax.ShapeDtypeStruct((n_chip - 1, r, n_col), BF16)]
        + [jax.ShapeDtypeStruct(p.shape, p.dtype) for p in ready_stacks] + _stacked([small]),
        compiler_params=_params(2),
    )(chip, a_t, b, *ready_stacks, small)
    return outs[0], outs[1], outs[2:2 + n_ready], outs[2 + n_ready]


def _proj_fwd(x1, pos, rc, kv_norm, w_dkv, ckv_norm, w_ukv, b_norm, b_w_in, q_norm, w_uq):
    t = x1.shape[0]
    tm = LATENT_ROW_TILE

    def body(x1_ref, pos_ref, rc_ref, kvn_ref, wdkv_ref, ckvn_ref, wukv_ref, bn_ref, bwin_ref, qn_ref, wuq_ref,
             ckr_ref, cq_ref, gb_ref, q_ref, k_ref, v_ref, cos_ref, sin_ref):
        cosv, sinv = _rope_angles(pos_ref[...], rc_ref[...])
        cos_ref[...] = cosv
        sin_ref[...] = sinv
        ct, s1, s2 = _rope_tables(cosv, sinv, rc_ref[...])
        xv = x1_ref[...]
        xh = xv * _rstd(xv)
        ckr = _dot(xh * kvn_ref[...], wdkv_ref[...])
        ckr_ref[...] = ckr
        ckv = ckr[:, 0:KV_RANK]
        kv = _dot(ckv * _rstd(ckv) * ckvn_ref[...], wukv_ref[...])
        k_rope = _rope(ckr[:, KV_RANK:CKR_PAD], ct, s1, s2)
        lane = lax.broadcasted_iota(jnp.int32, (tm, HEAD_PAD), 1)
        low = lane < QK_NOPE
        for h in range(N_HEADS):
            kv_h = kv[:, h * HEAD_PAD:(h + 1) * HEAD_PAD]
            k_ref[:, h * HEAD_PAD:(h + 1) * HEAD_PAD] = jnp.where(low, kv_h, k_rope).astype(BF16)
            v_ref[:, h * HEAD_PAD:(h + 1) * HEAD_PAD] = jnp.where(low, pltpu.roll(kv_h, V_HEAD, 1), 1.0).astype(BF16)
        pb = _dot(xh * bn_ref[...], bwin_ref[...])
        cq = pb[:, 0:Q_RANK]
        cq_ref[...] = cq
        gb_ref[...] = pb[:, Q_RANK:Q_RANK + E_B]
        q = _dot(cq * _rstd(cq) * qn_ref[...], wuq_ref[...])
        for h in range(N_HEADS):
            q_ref[:, h * HEAD_PAD:(h + 1) * HEAD_PAD] = _rope(
                q[:, h * HEAD_PAD:(h + 1) * HEAD_PAD], ct, s1, s2).astype(BF16)

    tok = lambda width: pl.BlockSpec((tm, width), lambda i: (i, 0))
    weights = [kv_norm, w_dkv, ckv_norm, w_ukv, b_norm, b_w_in, q_norm, w_uq]
    wide = N_HEADS * HEAD_PAD
    return pl.pallas_call(
        body, name="latent_proj_fwd", grid=(t // tm,),
        in_specs=[tok(D_MODEL), tok(1), _const_spec(rc.shape)] + [_const_spec(w.shape) for w in weights],
        out_specs=[tok(CKR_PAD), tok(Q_RANK), tok(E_B), tok(wide), tok(wide), tok(wide), tok(HEAD_PAD), tok(HEAD_PAD)],
        out_shape=[jax.ShapeDtypeStruct((t, CKR_PAD), F32), jax.ShapeDtypeStruct((t, Q_RANK), F32),
                   jax.ShapeDtypeStruct((t, E_B), F32), jax.ShapeDtypeStruct((t, wide), BF16),
                   jax.ShapeDtypeStruct((t, wide), BF16), jax.ShapeDtypeStruct((t, wide), BF16),
                   jax.ShapeDtypeStruct((t, HEAD_PAD), F32), jax.ShapeDtypeStruct((t, HEAD_PAD), F32)],
        compiler_params=_params(1),
    )(x1, pos, rc, *weights)


def _proj_bwd(dq, dk, dv, dgb, cq, ckr, x1, dx2, cosv, sinv, rc, kv_norm, w_dkv, ckv_norm, w_ukv, b_norm, b_w_in, q_norm, w_uq):
    t = x1.shape[0]
    tm = LATENT_ROW_TILE
    wide = N_HEADS * HEAD_PAD

    def body(dq_ref, dk_ref, dv_ref, dgb_ref, cq_ref, ckr_ref, x1_ref, dx2_ref, cos_ref, sin_ref, rc_ref,
             kvn_ref, wdkv_ref, ckvn_ref, wukv_ref, bn_ref, bwin_ref, qn_ref, wuq_ref,
             dx1_ref, dwuq_ref, dwbin_ref, dwukv_ref, dwdkv_ref, dqn_ref, dbn_ref, dckvn_ref, dkvn_ref, dqu_ref, dkv_ref):
        @pl.when(pl.program_id(0) == 0)
        def _():
            for ref in (dwuq_ref, dwbin_ref, dwukv_ref, dwdkv_ref, dqn_ref, dbn_ref, dckvn_ref, dkvn_ref):
                ref[...] = jnp.zeros_like(ref)

        ct, s1, s2 = _rope_tables(cos_ref[...], sin_ref[...], rc_ref[...])
        lane = lax.broadcasted_iota(jnp.int32, (tm, HEAD_PAD), 1)
        low = lane < QK_NOPE
        for h in range(N_HEADS):
            dqu_ref[:, h * HEAD_PAD:(h + 1) * HEAD_PAD] = _rope(
                dq_ref[:, h * HEAD_PAD:(h + 1) * HEAD_PAD], ct, -s1, -s2).astype(BF16)
        cq = cq_ref[...]
        rq = _rstd(cq)
        cqh = cq * rq
        cqn = (cqh * qn_ref[...]).astype(BF16)
        dwuq = lax.dot_general(cqn, dqu_ref[...], TN, preferred_element_type=F32)
        for h in range(N_HEADS):
            dwuq_ref[h] += dwuq[:, h * HEAD_PAD:(h + 1) * HEAD_PAD]
        dcqn = lax.dot_general(dqu_ref[...], wuq_ref[...], NT, preferred_element_type=F32)
        dqn_ref[...] += jnp.sum(dcqn * cqh, axis=0, keepdims=True)
        dcq = _norm_bwd(dcqn * qn_ref[...], cqh, rq)
        dpb = jnp.concatenate([dcq, dgb_ref[...]], axis=1).astype(BF16)
        xv = x1_ref[...]
        r = _rstd(xv)
        xh = xv * r
        dwbin_ref[...] += _dot_tn(xh * bn_ref[...], dpb)
        dh3 = lax.dot_general(dpb, bwin_ref[...], NT, preferred_element_type=F32)
        dk_rope = jnp.zeros((tm, HEAD_PAD), F32)
        for h in range(N_HEADS):
            dk_h = dk_ref[:, h * HEAD_PAD:(h + 1) * HEAD_PAD]
            dv_h = dv_ref[:, h * HEAD_PAD:(h + 1) * HEAD_PAD]
            dkv_ref[:, h * HEAD_PAD:(h + 1) * HEAD_PAD] = jnp.where(low, dk_h, pltpu.roll(dv_h, V_HEAD, 1)).astype(BF16)
            dk_rope += dk_h
        rope_lanes = jnp.logical_and(lane >= ROPE_LO, lane < ROPE_LO + QK_ROPE)
        dk_rope = jnp.where(rope_lanes, _rope(dk_rope, ct, -s1, -s2), 0.0)
        ckv = ckr_ref[:, 0:KV_RANK]
        rk = _rstd(ckv)
        ckh = ckv * rk
        ckvn = (ckh * ckvn_ref[...]).astype(BF16)
        dwukv = lax.dot_general(ckvn, dkv_ref[...], TN, preferred_element_type=F32)
        for h in range(N_HEADS):
            dwukv_ref[h] += dwukv[:, h * HEAD_PAD:(h + 1) * HEAD_PAD]
        dckvn = lax.dot_general(dkv_ref[...], wukv_ref[...], NT, preferred_element_type=F32)
        dckvn_ref[...] += jnp.sum(dckvn * ckh, axis=0, keepdims=True)
        dckv = _norm_bwd(dckvn * ckvn_ref[...], ckh, rk)
        dckr = jnp.concatenate([dckv, dk_rope], axis=1).astype(BF16)
        dwdkv_ref[...] += _dot_tn(xh * kvn_ref[...], dckr)
        dh2 = lax.dot_general(dckr, wdkv_ref[...], NT, preferred_element_type=F32)
        dkvn_ref[...] += jnp.sum(dh2 * xh, axis=0, keepdims=True)
        dbn_ref[...] += jnp.sum(dh3 * xh, axis=0, keepdims=True)
        dx1_ref[...] = dx2_ref[...] + _norm_bwd(dh2 * kvn_ref[...] + dh3 * bn_ref[...], xh, r)

    tok = lambda width: pl.BlockSpec((tm, width), lambda i: (i, 0))
    weights = [kv_norm, w_dkv, ckv_norm, w_ukv, b_norm, b_w_in, q_norm, w_uq]
    acc_shapes = [(N_HEADS, Q_RANK, HEAD_PAD), (D_MODEL, Q_RANK + E_B), (N_HEADS, KV_RANK, HEAD_PAD), (D_MODEL, CKR_PAD),
                  (1, Q_RANK), (1, D_MODEL), (1, KV_RANK), (1, D_MODEL)]
    return pl.pallas_call(
        body, name="latent_proj_bwd", grid=(t // tm,),
        in_specs=[tok(wide), tok(wide), tok(wide), tok(E_B), tok(Q_RANK), tok(CKR_PAD), tok(D_MODEL), tok(D_MODEL),
                  tok(HEAD_PAD), tok(HEAD_PAD), _const_spec(rc.shape)] + [_const_spec(w.shape) for w in weights],
        out_specs=[tok(D_MODEL)] + [_acc_spec(s) for s in acc_shapes],
        out_shape=[jax.ShapeDtypeStruct((t, D_MODEL), F32)] + [jax.ShapeDtypeStruct(s, F32) for s in acc_shapes],
        scratch_shapes=[pltpu.VMEM((tm, wide), BF16), pltpu.VMEM((tm, wide), BF16)],
        compiler_params=_params(1),
    )(dq, dk, dv, dgb, cq, ckr, x1, dx2, cosv, sinv, rc, *weights)


def _attn_fwd(q, k, v, n_seq, seq):
    tq, tk = ATTN_FWD_TILES
    ratio = tq // tk
    nq = seq // tq
    pair = 2 * HEAD_PAD

    def body(q_ref, k_ref, v_ref, o_ref, lr_ref):
        i = pl.program_id(2)
        row = lax.broadcasted_iota(jnp.int32, (tq, tk), 0)
        col = lax.broadcasted_iota(jnp.int32, (tq, tk), 1)
        qs = [q_ref[:, hh * HEAD_PAD:(hh + 1) * HEAD_PAD] for hh in range(2)]

        def step(j, carry, shift):
            start = pl.multiple_of(j * tk, tk)
            out = []
            for hh in range(2):
                m, acc = carry[hh]
                s = lax.dot_general(qs[hh], k_ref[pl.ds(start, tk), hh * HEAD_PAD:(hh + 1) * HEAD_PAD], NT,
                                    preferred_element_type=F32) * (SOFTMAX_SCALE * LOG2_E)
                if shift is not None:
                    s = jnp.where(col + shift <= row, s, -jnp.inf)
                m_new = jnp.maximum(m, jnp.max(s, axis=-1, keepdims=True))
                p = jnp.exp2(s - m_new)
                acc = jnp.exp2(m - m_new) * acc + jnp.dot(
                    p.astype(BF16), v_ref[pl.ds(start, tk), hh * HEAD_PAD:(hh + 1) * HEAD_PAD], preferred_element_type=F32)
                out.append((m_new, acc))
            return tuple(out)

        one = (jnp.full((tq, 1), -jnp.inf, F32), jnp.zeros((tq, HEAD_PAD), F32))
        carry = lax.fori_loop(0, i * ratio, functools.partial(step, shift=None), (one, one))
        for d in range(ratio):
            carry = step(i * ratio + d, carry, d * tk)
        lane = lax.broadcasted_iota(jnp.int32, (tq, HEAD_PAD), 1)
        low = lane < V_HEAD
        halves = []
        for hh in range(2):
            m, acc = carry[hh]
            swapped = pltpu.roll(acc, V_HEAD, 1)
            halves.append(acc / swapped)
            lse = (m * LN_2) + jnp.log(jnp.where(low, swapped, acc))
            lr_ref[hh] = lse.T[0:8, :]
        o_ref[...] = jnp.where(low, halves[0], pltpu.roll(halves[1], V_HEAD, 1))

    t = n_seq * seq
    return pl.pallas_call(
        body, name="attention_fwd", grid=(n_seq, N_HEADS // 2, nq),
        in_specs=[pl.BlockSpec((tq, pair), lambda s, p, i: (s * nq + i, p)),
                  pl.BlockSpec((seq, pair), lambda s, p, i: (s, p)),
                  pl.BlockSpec((seq, pair), lambda s, p, i: (s, p))],
        out_specs=[pl.BlockSpec((tq, HEAD_PAD), lambda s, p, i: (s * nq + i, p)),
                   pl.BlockSpec((None, 2, 8, tq), lambda s, p, i: (s, p, 0, i))],
        out_shape=[jax.ShapeDtypeStruct((t, E_B), F32), jax.ShapeDtypeStruct((n_seq, N_HEADS, 8, seq), F32)],
        compiler_params=_params(3),
    )(q, k, v)


def _attn_bwd(q, k, v, do, lse_row, delta_row, n_seq, seq):
    tk, tq = ATTN_BWD_TILES
    ratio = tk // tq
    nk = seq // tk
    n_inner = seq // tq
    pair = 2 * HEAD_PAD

    def body(q_ref, k_ref, v_ref, do_ref, lr_ref, dr_ref, dk_ref, dv_ref, dq_ref):
        j = pl.program_id(2)

        @pl.when(j == 0)
        def _():
            dq_ref[...] = jnp.zeros_like(dq_ref)

        row = lax.broadcasted_iota(jnp.int32, (tk, tq), 0)
        col = lax.broadcasted_iota(jnp.int32, (tk, tq), 1)
        ks = [k_ref[:, hh * HEAD_PAD:(hh + 1) * HEAD_PAD] for hh in range(2)]
        vs = [v_ref[:, hh * HEAD_PAD:(hh + 1) * HEAD_PAD] for hh in range(2)]

        def step(i, carry, shift):
            start = pl.multiple_of(i * tq, tq)
            out = []
            for hh in range(2):
                dk_acc, dv_acc = carry[hh]
                qi = q_ref[pl.ds(start, tq), hh * HEAD_PAD:(hh + 1) * HEAD_PAD]
                doi = do_ref[pl.ds(start, tq), hh * HEAD_PAD:(hh + 1) * HEAD_PAD]
                st = lax.dot_general(ks[hh], qi, NT, preferred_element_type=F32) * SOFTMAX_SCALE
                pt = jnp.exp(st - lr_ref[hh, 0:1, pl.ds(start, tq)])
                if shift is not None:
                    pt = jnp.where(col + shift >= row, pt, 0.0)
                dv_acc = dv_acc + jnp.dot(pt.astype(BF16), doi, preferred_element_type=F32)
                dpt = lax.dot_general(vs[hh], doi, NT, preferred_element_type=F32)
                dst = (pt * (dpt - dr_ref[hh, 0:1, pl.ds(start, tq)]) * SOFTMAX_SCALE).astype(BF16)
                dk_acc = dk_acc + jnp.dot(dst, qi, preferred_element_type=F32)
                dq_ref[pl.ds(start, tq), hh * HEAD_PAD:(hh + 1) * HEAD_PAD] += lax.dot_general(
                    dst, ks[hh], TN, preferred_element_type=F32)
                out.append((dk_acc, dv_acc))
            return tuple(out)

        one = (jnp.zeros((tk, HEAD_PAD), F32), jnp.zeros((tk, HEAD_PAD), F32))
        carry = (one, one)
        for d in range(ratio):
            carry = step(j * ratio + d, carry, d * tq)
        carry = lax.fori_loop((j + 1) * ratio, n_inner, functools.partial(step, shift=None), carry)
        for hh in range(2):
            dk_ref[:, hh * HEAD_PAD:(hh + 1) * HEAD_PAD] = carry[hh][0]
            dv_ref[:, hh * HEAD_PAD:(hh + 1) * HEAD_PAD] = carry[hh][1]

    t = n_seq * seq
    wide = N_HEADS * HEAD_PAD
    return pl.pallas_call(
        body, name="attention_bwd", grid=(n_seq, N_HEADS // 2, nk),
        in_specs=[pl.BlockSpec((seq, pair), lambda s, p, j: (s, p)),
                  pl.BlockSpec((tk, pair), lambda s, p, j: (s * nk + j, p)),
                  pl.BlockSpec((tk, pair), lambda s, p, j: (s * nk + j, p)),
                  pl.BlockSpec((seq, pair), lambda s, p, j: (s, p)),
                  pl.BlockSpec((None, 2, 8, seq), lambda s, p, j: (s, p, 0, 0)),
                  pl.BlockSpec((None, 2, 8, seq), lambda s, p, j: (s, p, 0, 0))],
        out_specs=[pl.BlockSpec((tk, pair), lambda s, p, j: (s * nk + j, p)),
                   pl.BlockSpec((tk, pair), lambda s, p, j: (s * nk + j, p)),
                   pl.BlockSpec((seq, pair), lambda s, p, j: (s, p))],
        out_shape=[jax.ShapeDtypeStruct((t, wide), F32), jax.ShapeDtypeStruct((t, wide), F32),
                   jax.ShapeDtypeStruct((t, wide), F32)],
        compiler_params=_params(3),
    )(q, k, v, do, lse_row, delta_row)


def _head_tail(o, gb, x1, w_out, final_norm, target, n_seq, seq):
    tm = LATENT_ROW_TILE
    nt = seq // tm
    n_col = D_MODEL // N_DEV

    def body(o_ref, gb_ref, x1_ref, wout_ref, fn_ref, tgt_ref,
             dx2_ref, do_ref, dgb_ref, dr_ref, dwout_ref, dfn_ref, loss_ref):
        first = jnp.logical_and(pl.program_id(0) == 0, pl.program_id(1) == 0)

        @pl.when(first)
        def _():
            dwout_ref[...] = jnp.zeros_like(dwout_ref)
            dfn_ref[...] = jnp.zeros_like(dfn_ref)
            loss_ref[...] = jnp.zeros_like(loss_ref)

        ov, g = o_ref[...], gb_ref[...]
        silu, dsilu = _silu_parts(g)
        gated = (ov * silu).astype(BF16)
        x2 = x1_ref[...] + jnp.dot(gated, wout_ref[...], preferred_element_type=F32)
        r = _rstd(x2)
        xh = x2 * r
        err = xh * fn_ref[...] - tgt_ref[...]
        loss_ref[...] += 0.5 * jnp.sum(jnp.mean(err * err, axis=-1, keepdims=True), axis=0, keepdims=True)
        dy = err / D_MODEL
        dfn_ref[...] += jnp.sum(dy * xh, axis=0, keepdims=True)
        dx2 = _norm_bwd(dy * fn_ref[...], xh, r)
        dx2_ref[...] = dx2
        dx2b = dx2.astype(BF16)
        dw = lax.dot_general(gated, dx2b, TN, preferred_element_type=F32)
        for d in range(N_DEV):
            dwout_ref[d] += dw[:, d * n_col:(d + 1) * n_col]
        dgated = lax.dot_general(dx2b, wout_ref[...], NT, preferred_element_type=F32)
        do = dgated * silu
        dgb_ref[...] = dgated * ov * dsilu
        prod = do * ov
        lane = lax.broadcasted_iota(jnp.int32, (tm, HEAD_PAD), 1)
        low = lane < V_HEAD
        for p in range(N_HEADS // 2):
            blk = prod[:, p * HEAD_PAD:(p + 1) * HEAD_PAD]
            do_pair = do[:, p * HEAD_PAD:(p + 1) * HEAD_PAD]
            for hh in range(2):
                h = 2 * p + hh
                mine = do_pair if hh == 0 else pltpu.roll(do_pair, V_HEAD, 1)
                do_ref[:, h * HEAD_PAD:(h + 1) * HEAD_PAD] = jnp.where(low, mine, 0.0).astype(BF16)
                delta = jnp.sum(jnp.where(low if hh == 0 else ~low, blk, 0.0), axis=-1, keepdims=True)
                dr_ref[h] = jnp.broadcast_to(delta, (tm, HEAD_PAD)).T[0:8, :]

    tok = lambda width: pl.BlockSpec((tm, width), lambda s, i: (s * nt + i, 0))
    t = n_seq * seq
    return pl.pallas_call(
        body, name="head_tail", grid=(n_seq, nt),
        in_specs=[tok(E_B), tok(E_B), tok(D_MODEL), _const_spec(w_out.shape), _const_spec((1, D_MODEL)), tok(D_MODEL)],
        out_specs=[tok(D_MODEL), tok(N_HEADS * HEAD_PAD), tok(E_B),
                   pl.BlockSpec((None, N_HEADS, 8, tm), lambda s, i: (s, 0, 0, i)),
                   _acc_spec((N_DEV, E_B, n_col)), _acc_spec((1, D_MODEL)), _acc_spec((1, 1))],
        out_shape=[jax.ShapeDtypeStruct((t, D_MODEL), F32), jax.ShapeDtypeStruct((t, N_HEADS * HEAD_PAD), BF16),
                   jax.ShapeDtypeStruct((t, E_B), F32),
                   jax.ShapeDtypeStruct((n_seq, N_HEADS, 8, seq), F32), jax.ShapeDtypeStruct((N_DEV, E_B, n_col), F32),
                   jax.ShapeDtypeStruct((1, D_MODEL), F32), jax.ShapeDtypeStruct((1, 1), F32)],
        compiler_params=_params(2),
    )(o, gb, x1, w_out, final_norm, target)


def _adamw_math(w, g, m, v):
    m = ADAM_B1 * m + (1.0 - ADAM_B1) * g
    v = ADAM_B2 * v + (1.0 - ADAM_B2) * jnp.square(g)
    m_hat = m / (1.0 - ADAM_B1 ** ADAM_STEP)
    v_hat = v / (1.0 - ADAM_B2 ** ADAM_STEP)
    delta = -ADAM_LR * (m_hat / (jnp.sqrt(v_hat) + ADAM_EPS) + ADAM_WD * w)
    return delta, m, v


def _adamw_reduce(own, parts, w, m, v):
    rows, cols = w.shape
    br = 256 if rows % 256 == 0 else 128
    n_parts = parts.shape[0]

    def body(*refs):
        p_ref, w_ref, m_ref, v_ref, g_ref, d_ref, nm_ref, nv_ref = refs[-8:]
        g = p_ref[0].astype(F32) if own is None else refs[0][...] + p_ref[0].astype(F32)
        for k in range(1, n_parts):
            g = g + p_ref[k].astype(F32)
        g_ref[...] = g
        d_ref[...], nm_ref[...], nv_ref[...] = _adamw_math(w_ref[...], g, m_ref[...], v_ref[...])

    blk = pl.BlockSpec((br, cols), lambda i: (i, 0))
    first = [] if own is None else [own]
    return pl.pallas_call(
        body, name="adamw_reduce", grid=(rows // br,),
        in_specs=[blk] * len(first) + [pl.BlockSpec((n_parts, br, cols), lambda i: (0, i, 0)), blk, blk, blk],
        out_specs=[blk] * 4, out_shape=[jax.ShapeDtypeStruct((rows, cols), F32)] * 4,
        compiler_params=_params(1),
    )(*first, parts, w, m, v)


def _sum_parts(parts):
    def body(p_ref, o_ref):
        g = p_ref[0]
        for k in range(1, N_DEV):
            g = g + p_ref[k]
        o_ref[...] = g

    return pl.pallas_call(body, name="small_grad_sum", out_shape=jax.ShapeDtypeStruct(parts.shape[1:], F32))(parts)


def _adamw_small(gs, ws, ms, vs):
    n = len(gs)

    def body(*refs):
        ins, outs = refs[:4 * n], refs[4 * n:]
        for j in range(n):
            g_ref, w_ref, m_ref, v_ref = ins[j], ins[n + j], ins[2 * n + j], ins[3 * n + j]
            outs[3 * j][...], outs[3 * j + 1][...], outs[3 * j + 2][...] = _adamw_math(
                w_ref[...], g_ref[...], m_ref[...], v_ref[...])

    shapes = [jax.ShapeDtypeStruct(g.shape, F32) for g in gs for _ in range(3)]
    flat = pl.pallas_call(body, name="adamw_small", out_shape=shapes)(*gs, *ws, *ms, *vs)
    return [tuple(flat[3 * j:3 * j + 3]) for j in range(n)]


SMALL_ROWS = 8
LOSS_LANE = D_MODEL - 1


def _pack_small(g_fn, g_kvn, g_bn, g_an, g_ckvn, g_qn, g_conv, loss_part):
    def body(fn_ref, kvn_ref, bn_ref, an_ref, ckvn_ref, qn_ref, conv_ref, loss_ref, o_ref):
        o_ref[0:1, :] = fn_ref[...]
        o_ref[1:2, :] = kvn_ref[...]
        o_ref[2:3, :] = bn_ref[...]
        o_ref[3:4, :] = an_ref[...]
        lane = lax.broadcasted_iota(jnp.int32, (1, D_MODEL), 1)
        o_ref[4:5, :] = jnp.where(lane == LOSS_LANE, loss_ref[...], 0.0)
        o_ref[4:5, 0:KV_RANK] = ckvn_ref[...]
        o_ref[4:5, KV_RANK:KV_RANK + Q_RANK] = qn_ref[...]
        o_ref[5:8, :] = conv_ref[0:3, :]

    return pl.pallas_call(body, name="pack_small", out_shape=jax.ShapeDtypeStruct((SMALL_ROWS, D_MODEL), F32))(
        g_fn, g_kvn, g_bn, g_an, g_ckvn, g_qn, g_conv, loss_part)


def _pad_cols(a, width):
    return jnp.pad(a, ((0, 0), (0, width - a.shape[1])))


def _dkv_to_padded(a):
    r = a.shape[0]
    z = jnp.zeros((r, ROPE_LO), a.dtype)
    z2 = jnp.zeros((r, HEAD_PAD - ROPE_LO - QK_ROPE), a.dtype)
    return jnp.concatenate([a[:, :KV_RANK], z, a[:, KV_RANK:], z2], axis=1)


def _dkv_from_padded(a):
    return jnp.concatenate([a[:, :KV_RANK], a[:, KV_RANK + ROPE_LO:KV_RANK + ROPE_LO + QK_ROPE]], axis=1)


def _unstack_cols(a):
    return jnp.transpose(a, (1, 0, 2)).reshape(a.shape[1], N_DEV * a.shape[2])


def kernel(x, positions, a_norm, a_w_in, a_conv, a_w_out, kv_norm, w_dkv, ckv_norm, w_ukv, b_norm, b_w_in, b_q_norm, b_w_uq, b_w_out, final_norm, loss_target, m_a_norm, m_a_w_in, m_a_conv, m_a_w_out, m_kv_norm, m_w_dkv, m_ckv_norm, m_w_ukv, m_b_norm, m_b_w_in, m_b_q_norm, m_b_w_uq, m_b_w_out, m_final_norm, v_a_norm, v_a_w_in, v_a_conv, v_a_w_out, v_kv_norm, v_w_dkv, v_ckv_norm, v_w_ukv, v_b_norm, v_b_w_in, v_b_q_norm, v_b_w_uq, v_b_w_out, v_final_norm):
    n_seq, seq, _ = x.shape
    t = n_seq * seq
    me = 4 * lax.axis_index("x") + 2 * lax.axis_index("y") + lax.axis_index("c")

    big = {
        "a_w_in": (a_w_in[0], m_a_w_in[0], v_a_w_in[0]),
        "a_w_out": (a_w_out[0], m_a_w_out[0], v_a_w_out[0]),
        "w_dkv": tuple(_dkv_to_padded(a) for a in (w_dkv, m_w_dkv, v_w_dkv)),
        "w_ukv": (w_ukv, m_w_ukv, v_w_ukv),
        "b_w_in": (b_w_in[0], m_b_w_in[0], v_b_w_in[0]),
        "b_w_uq": tuple(_pad_cols(a[0], HEAD_PAD) for a in (b_w_uq, m_b_w_uq, v_b_w_uq)),
        "b_w_out": (b_w_out[0], m_b_w_out[0], v_b_w_out[0]),
    }
    names = list(big)
    first_names, later_names = names[:2], names[2:]
    gathered = _all_gather([big[n][0].astype(BF16) for n in first_names] + [a_norm, a_conv[0]])
    a_norm_f = gathered[2].reshape(1, D_MODEL)
    a_conv_f = _unstack_cols(gathered[3])
    w_a_in = gathered[0]
    w_a_out = gathered[1].reshape(D_MODEL, D_MODEL)

    x2d = x.reshape(t, D_MODEL)
    tgt = loss_target.reshape(t, D_MODEL)
    pos = positions.astype(F32).reshape(t, 1)
    rc = _rope_consts()
    kvn, ckvn, bn, qn, fn = kv_norm.reshape(1, -1), ckv_norm.reshape(1, -1), b_norm, b_q_norm, final_norm.reshape(1, -1)

    x1, proj, conv, later = _conv_fwd(x2d, a_norm_f, w_a_in, a_conv_f, w_a_out, n_seq, seq,
                                      [big[n][0].astype(BF16) for n in later_names])
    full = dict(zip(later_names, later))
    w_dkv_f = full["w_dkv"].reshape(D_MODEL, CKR_PAD)
    w_ukv_f = _unstack_cols(full["w_ukv"])
    w_b_in = full["b_w_in"].reshape(D_MODEL, Q_RANK + E_B)
    w_uq_f = _unstack_cols(full["b_w_uq"])
    w_b_out = _unstack_cols(full["b_w_out"])
    layer_b = (kvn, w_dkv_f, ckvn, w_ukv_f, bn, w_b_in, qn, w_uq_f)
    ckr, cq, gb, q, k, v, cosv, sinv = _proj_fwd(x1, pos, rc, *layer_b)
    o, lse_row = _attn_fwd(q, k, v, n_seq, seq)
    dx2, do, dgb, delta_row, g_b_out, g_fn, loss_part = _head_tail(o, gb, x1, w_b_out, fn, tgt, n_seq, seq)
    dk, dv, dq = _attn_bwd(q, k, v, do, lse_row, delta_row, n_seq, seq)
    dx1, g_uq, g_b_in, g_ukv, g_dkv, g_qn, g_bn, g_ckvn, g_kvn = _proj_bwd(
        dq, dk, dv, dgb, cq, ckr, x1, dx2, cosv, sinv, rc, *layer_b)
    stacks = {
        "w_dkv": g_dkv.reshape(N_DEV, D_MODEL // N_DEV, CKR_PAD),
        "w_ukv": g_ukv,
        "b_w_in": g_b_in.reshape(N_DEV, D_MODEL // N_DEV, Q_RANK + E_B),
        "b_w_uq": g_uq,
        "b_w_out": g_b_out,
    }
    (dx, h_a, dproj, g_a_out, g_an, g_conv), later_parts = _conv_bwd(
        dx1, x2d, proj, conv, a_norm_f, w_a_in, a_conv_f, w_a_out, n_seq, seq, [stacks[n] for n in later_names])
    parts = dict(zip(later_names, later_parts))
    small = _pack_small(g_fn, g_kvn, g_bn, g_an, g_ckvn, g_qn, g_conv, loss_part)
    a_in_own, a_in_got, (parts["a_w_out"],), small_parts = _w_in_grad_exchange(
        h_a, dproj, D_MODEL * 4 // N_DEV, (me // 2).reshape(1).astype(jnp.int32),
        [g_a_out.reshape(N_DEV, D_MODEL // N_DEV, D_MODEL)], small)

    outs = {"a_w_in": _adamw_reduce(a_in_own, a_in_got, *big["a_w_in"])}
    for n in names[1:]:
        outs[n] = _adamw_reduce(None, parts[n], *big[n])
    outs["w_dkv"] = tuple(_dkv_from_padded(a) for a in outs["w_dkv"])
    outs["b_w_uq"] = tuple(a[:, :QK_NOPE + QK_ROPE] for a in outs["b_w_uq"])
    for n in ("a_w_in", "a_w_out", "b_w_in", "b_w_uq", "b_w_out"):
        outs[n] = tuple(a[None] for a in outs[n])

    total = _sum_parts(small_parts)
    loss = total[4, LOSS_LANE]
    shard = D_MODEL // N_DEV
    g_small = {
        "final_norm": total[0], "kv_norm": total[1], "b_norm": total[2:3],
        "a_norm": lax.dynamic_slice_in_dim(total[3:4], me * shard, shard, axis=1),
        "ckv_norm": total[4, 0:KV_RANK], "b_q_norm": total[4:5, KV_RANK:KV_RANK + Q_RANK],
        "a_conv": lax.dynamic_slice_in_dim(total[5:8], me * shard, shard, axis=1)[None],
    }
    small_state = {
        "final_norm": (final_norm, m_final_norm, v_final_norm), "kv_norm": (kv_norm, m_kv_norm, v_kv_norm),
        "b_norm": (b_norm, m_b_norm, v_b_norm), "a_norm": (a_norm, m_a_norm, v_a_norm),
        "ckv_norm": (ckv_norm, m_ckv_norm, v_ckv_norm), "b_q_norm": (b_q_norm, m_b_q_norm, v_b_q_norm),
        "a_conv": (a_conv, m_a_conv, v_a_conv),
    }
    small_names = list(g_small)
    as2d = lambda a: a.reshape(-1, a.shape[-1])
    upd = _adamw_small([as2d(g_small[n]) for n in small_names],
                       *[[as2d(small_state[n][j]) for n in small_names] for j in range(3)])
    for n, u in zip(small_names, upd):
        outs[n] = (g_small[n],) + tuple(a.reshape(g_small[n].shape) for a in u)

    order = ["a_norm", "a_w_in", "a_conv", "a_w_out", "kv_norm", "w_dkv", "ckv_norm", "w_ukv", "b_norm", "b_w_in",
             "b_q_norm", "b_w_uq", "b_w_out", "final_norm"]
    result = [loss, dx.reshape(n_seq, seq, D_MODEL)]
    for j in range(4):
        result += [outs[n][j] for n in order]
    return tuple(result)
```

```python
import functools
import math

import numpy as np
import jax
import jax.numpy as jnp
from jax import lax
from jax.experimental import pallas as pl
from jax.experimental.pallas import tpu as pltpu

F32 = jnp.float32
BF16 = jnp.bfloat16

D_MODEL = 1024
N_HEADS = 8
QK_NOPE = 64
QK_ROPE = 32
V_HEAD = 64
KV_RANK = 256
Q_RANK = 384
E_B = N_HEADS * V_HEAD
HEAD_PAD = 128
CKR_PAD = KV_RANK + HEAD_PAD
ROPE_LO = QK_NOPE
ROPE_HALF = QK_ROPE // 2
ROPE_THETA = 10000.0
SOFTMAX_SCALE = 1.0 / math.sqrt(QK_NOPE + QK_ROPE)
LOG2_E = math.log2(math.e)
LN_2 = math.log(2.0)
EPS = 1e-6
N_DEV = 8

ADAM_LR = 0.001
ADAM_B1 = 0.9
ADAM_B2 = 0.999
ADAM_EPS = 1e-08
ADAM_WD = 0.01
ADAM_STEP = 10

ROW_TILE = 256
LATENT_ROW_TILE = 512
PICKUP_BLOCK = 3
ATTN_TILE = 512
ATTN_GROUP = 4
ATTN_BWD_GROUP = 4
VMEM_LIMIT = 56 * 1024 * 1024

MESH = pl.DeviceIdType.MESH
NT = (((1,), (1,)), ((), ()))
TN = (((0,), (0,)), ((), ()))


def _dot(a, b):
    return jnp.dot(a.astype(BF16), b.astype(BF16), preferred_element_type=F32)


def _dot_nt(a, b):
    return lax.dot_general(a.astype(BF16), b.astype(BF16), NT, preferred_element_type=F32)


def _dot_tn(a, b):
    return lax.dot_general(a.astype(BF16), b.astype(BF16), TN, preferred_element_type=F32)


def _rstd(x):
    return lax.rsqrt(jnp.mean(x * x, axis=-1, keepdims=True) + EPS)


def _norm_bwd(a, xh, r):
    return r * (a - xh * jnp.mean(a * xh, axis=-1, keepdims=True))


def _silu_parts(g):
    sg = jax.nn.sigmoid(g)
    return g * sg, sg * (1.0 + g * (1.0 - sg))


def _rope_consts():
    inv = (ROPE_THETA ** (-np.arange(0, QK_ROPE, 2, dtype=np.float32) / QK_ROPE)).astype(np.float32)
    t = np.zeros((8, HEAD_PAD), np.float32)
    t[0, ROPE_LO:ROPE_LO + ROPE_HALF] = inv
    t[0, ROPE_LO + ROPE_HALF:ROPE_LO + QK_ROPE] = inv
    t[1, ROPE_LO:ROPE_LO + ROPE_HALF] = -1.0
    t[2, ROPE_LO + ROPE_HALF:ROPE_LO + QK_ROPE] = 1.0
    return jnp.asarray(t)


def _rope_angles(pos, rc):
    ang = pos * rc[0:1, :]
    return jnp.cos(ang), jnp.sin(ang)


def _rope_tables(cosv, sinv, rc):
    return cosv, sinv * rc[1:2, :], sinv * rc[2:3, :]


def _rope(x, ct, s1, s2):
    up = pltpu.roll(x, HEAD_PAD - ROPE_HALF, 1)
    dn = pltpu.roll(x, ROPE_HALF, 1)
    return x * ct + up * s1 + dn * s2


def _const_spec(shape):
    nd = len(shape)
    return pl.BlockSpec(shape, lambda *_: (0,) * nd, pipeline_mode=pl.Buffered(1))


def _acc_spec(shape):
    nd = len(shape)
    return pl.BlockSpec(shape, lambda *_: (0,) * nd)


def _params(n_axes):
    return pltpu.CompilerParams(dimension_semantics=("arbitrary",) * n_axes, vmem_limit_bytes=VMEM_LIMIT)


def _place():
    x, y, c = lax.axis_index("x"), lax.axis_index("y"), lax.axis_index("c")
    return x, y, c, 4 * x + 2 * y + c


def _peer(x, y, c, mask):
    px = 1 - x if mask & 4 else x
    py = 1 - y if mask & 2 else y
    pc = 1 - c if mask & 1 else c
    return (px, py, pc), 4 * px + 2 * py + pc


ANY_SPEC = pl.BlockSpec(memory_space=pl.ANY)


def _comm_sems(n):
    return [pltpu.SemaphoreType.DMA((n, N_DEV - 1)), pltpu.SemaphoreType.DMA((n, N_DEV - 1)), pltpu.SemaphoreType.DMA((n,))]


def _gather_copies(ins, outs, sems):
    send_sems, recv_sems, local_sems = sems
    x, y, c, me = _place()
    starts, waits = [], []
    for w in range(len(ins)):
        mine = pltpu.make_async_copy(ins[w], outs[w].at[me], local_sems.at[w])
        starts.append(mine)
        waits.append(mine)
        for mask in range(1, N_DEV):
            peer, peer_idx = _peer(x, y, c, mask)
            starts.append(pltpu.make_async_remote_copy(
                src_ref=ins[w], dst_ref=outs[w].at[me], send_sem=send_sems.at[w, mask - 1],
                recv_sem=recv_sems.at[w, mask - 1], device_id=peer, device_id_type=MESH))
            waits.append(pltpu.make_async_remote_copy(
                src_ref=ins[w], dst_ref=outs[w].at[peer_idx], send_sem=send_sems.at[w, mask - 1],
                recv_sem=recv_sems.at[w, mask - 1], device_id=peer, device_id_type=MESH))
    return starts, waits


def _scatter_copies(ins, outs, sems):
    send_sems, recv_sems, local_sems = sems
    x, y, c, me = _place()
    copies = []
    for w in range(len(ins)):
        copies.append(pltpu.make_async_copy(ins[w].at[me], outs[w].at[0], local_sems.at[w]))
        for mask in range(1, N_DEV):
            peer, peer_idx = _peer(x, y, c, mask)
            copies.append(pltpu.make_async_remote_copy(
                src_ref=ins[w].at[peer_idx], dst_ref=outs[w].at[mask], send_sem=send_sems.at[w, mask - 1],
                recv_sem=recv_sems.at[w, mask - 1], device_id=peer, device_id_type=MESH))
    return copies, copies


def _stacked(arrays):
    return [jax.ShapeDtypeStruct((N_DEV,) + a.shape, a.dtype) for a in arrays]


def _all_gather(shards):
    n = len(shards)

    def body(*refs):
        ins, outs = refs[:n], refs[n:2 * n]
        send_sems, recv_sems, local_sems = refs[2 * n:]
        x, y, c, me = _place()
        sibling = (x, y, 1 - c)
        chips = [(1 - x, y), (x, 1 - y), (1 - x, 1 - y)]

        def copy(w, k, block, to, src=None):
            idx = 4 * block[0] + 2 * block[1] + block[2]
            return pltpu.make_async_remote_copy(
                src_ref=outs[w].at[idx] if src is None else src, dst_ref=outs[w].at[idx],
                send_sem=send_sems.at[w, k], recv_sem=recv_sems.at[w, k], device_id=to, device_id_type=MESH)

        local, sent = [], []
        for w in range(n):
            mine = pltpu.make_async_copy(ins[w], outs[w].at[me], local_sems.at[w])
            mine.start()
            local.append(mine)
            first = [copy(w, 0, (x, y, c), sibling, src=ins[w])]
            first += [copy(w, 1 + j, (x, y, c), (*chip, c), src=ins[w]) for j, chip in enumerate(chips)]
            for cp in first:
                cp.start()
            sent += first
        for w in range(n):
            for j, chip in enumerate(chips):
                copy(w, 1 + j, (*chip, c), (x, y, c)).wait_recv()
                onward = copy(w, 4 + j, (*chip, c), sibling)
                onward.start()
                sent.append(onward)
        for w in range(n):
            copy(w, 0, sibling, (x, y, c)).wait_recv()
            for j, chip in enumerate(chips):
                copy(w, 4 + j, (*chip, 1 - c), (x, y, c)).wait_recv()
        for cp in sent:
            cp.wait_send()
        for cp in local:
            cp.wait()

    return pl.pallas_call(
        body, name="weight_all_gather", out_shape=_stacked(shards),
        in_specs=[ANY_SPEC] * n, out_specs=[ANY_SPEC] * n, scratch_shapes=_comm_sems(n),
    )(*shards)


def _conv_fwd(x, a_norm, w_in, conv_w, w_out, n_seq, seq, later_shards):
    tm = ROW_TILE
    nt = seq // tm
    n_col = w_in.shape[2]
    n_later = len(later_shards)

    def body(x_ref, an_ref, win_ref, cw_ref, wout_ref, *rest):
        shard_refs, rest = rest[:n_later], rest[n_later:]
        x1_ref, proj_ref, conv_ref = rest[:3]
        stack_refs, rest = rest[3:3 + n_later], rest[3 + n_later:]
        prev_ref, sems = rest[0], rest[1:]
        step = pl.program_id(0) * nt + pl.program_id(1)

        @pl.when(step == 0)
        def _():
            for cp in _gather_copies(shard_refs, stack_refs, sems)[0]:
                cp.start()

        @pl.when(pl.program_id(1) == 0)
        def _():
            prev_ref[...] = jnp.zeros_like(prev_ref)

        xv = x_ref[...]
        h = (xv * _rstd(xv) * an_ref[...]).astype(BF16)
        for d in range(N_DEV):
            proj_ref[:, d * n_col:(d + 1) * n_col] = jnp.dot(h, win_ref[d], preferred_element_type=F32)
        b = proj_ref[:, 0:D_MODEL]
        v = proj_ref[:, D_MODEL:2 * D_MODEL] * proj_ref[:, 2 * D_MODEL:3 * D_MODEL]
        g = proj_ref[:, 3 * D_MODEL:4 * D_MODEL]
        w0, w1, w2 = cw_ref[0:1, :], cw_ref[1:2, :], cw_ref[2:3, :]
        conv_ref[...] = w0 * pltpu.roll(v, 2, 0) + w1 * pltpu.roll(v, 1, 0) + w2 * v
        rows = lax.broadcasted_iota(jnp.int32, (8, D_MODEL), 0)
        p8, v8 = prev_ref[...], v[0:8]
        back1 = jnp.where(rows < 1, pltpu.roll(p8, 1, 0), pltpu.roll(v8, 1, 0))
        back2 = jnp.where(rows < 2, pltpu.roll(p8, 2, 0), pltpu.roll(v8, 2, 0))
        conv_ref[0:8, :] = w0 * back2 + w1 * back1 + w2 * v8
        prev_ref[...] = v[tm - 8:tm]
        silu, _ = _silu_parts(g)
        yv = silu * b * conv_ref[...]
        x1_ref[...] = xv + _dot(yv, wout_ref[...])

        @pl.when(step == n_seq * nt - 1)
        def _():
            for cp in _gather_copies(shard_refs, stack_refs, sems)[1]:
                cp.wait()

    tok = lambda width: pl.BlockSpec((tm, width), lambda s, i: (s * nt + i, 0))
    t = n_seq * seq
    outs = pl.pallas_call(
        body, name="conv_mixer_fwd", grid=(n_seq, nt),
        in_specs=[tok(D_MODEL), _const_spec((1, D_MODEL)), _const_spec(w_in.shape), _const_spec((3, D_MODEL)),
                  _const_spec(w_out.shape)] + [ANY_SPEC] * n_later,
        out_specs=[tok(D_MODEL), tok(4 * D_MODEL), tok(D_MODEL)] + [ANY_SPEC] * n_later,
        out_shape=[jax.ShapeDtypeStruct((t, D_MODEL), F32), jax.ShapeDtypeStruct((t, 4 * D_MODEL), F32),
                   jax.ShapeDtypeStruct((t, D_MODEL), F32)] + _stacked(later_shards),
        scratch_shapes=[pltpu.VMEM((8, D_MODEL), F32)] + _comm_sems(n_later),
        compiler_params=_params(2),
    )(x, a_norm, w_in, conv_w, w_out, *later_shards)
    return outs[0], outs[1], outs[2], outs[3:]


def _conv_bwd(dx1, x, proj, conv, a_norm, w_in, conv_w, w_out, n_seq, seq, ready_stacks):
    tm = ROW_TILE
    nt = seq // tm
    n_col = w_in.shape[2]
    n_ready = len(ready_stacks)

    def body(dx1_ref, x_ref, proj_ref, conv_ref, an_ref, win_ref, cw_ref, wout_ref, *rest):
        ready_refs, rest = rest[:n_ready], rest[n_ready:]
        dx_ref, h_ref, dproj_ref, dwout_ref, dan_ref, dcw_ref = rest[:6]
        part_refs, rest = rest[6:6 + n_ready], rest[6 + n_ready:]
        next_ref, d1_ref, d2_ref = rest[:3]
        sems = rest[3:]
        step = pl.program_id(0) * nt + pl.program_id(1)
        first = step == 0

        @pl.when(first)
        def _():
            for cp in _scatter_copies(ready_refs, part_refs, sems)[0]:
                cp.start()
            dwout_ref[...] = jnp.zeros_like(dwout_ref)
            dan_ref[...] = jnp.zeros_like(dan_ref)
            dcw_ref[...] = jnp.zeros_like(dcw_ref)

        @pl.when(pl.program_id(1) == 0)
        def _():
            next_ref[...] = jnp.zeros_like(next_ref)

        dx1v = dx1_ref[...]
        dy = _dot_nt(dx1v, wout_ref[...])
        b = proj_ref[:, 0:D_MODEL]
        cc = proj_ref[:, D_MODEL:2 * D_MODEL]
        u = proj_ref[:, 2 * D_MODEL:3 * D_MODEL]
        g = proj_ref[:, 3 * D_MODEL:4 * D_MODEL]
        cv = conv_ref[...]
        silu, dsilu = _silu_parts(g)
        per_part = D_MODEL // n_col

        def back_through_w_in(part, grad):
            grad = grad.astype(BF16)
            dproj_ref[:, part * D_MODEL:(part + 1) * D_MODEL] = grad
            term = None
            for j in range(per_part):
                d = part * per_part + j
                piece = lax.dot_general(grad[:, j * n_col:(j + 1) * n_col], win_ref[d], NT, preferred_element_type=F32)
                term = piece if term is None else term + piece
            return term

        dh = back_through_w_in(0, dy * silu * cv)
        dh += back_through_w_in(3, dy * b * cv * dsilu)
        dwout_ref[...] += _dot_tn(silu * b * cv, dx1v)
        dconv = dy * silu * b
        d1_ref[...] = pltpu.roll(dconv, tm - 1, 0)
        d2_ref[...] = pltpu.roll(dconv, tm - 2, 0)
        rows = lax.broadcasted_iota(jnp.int32, (8, D_MODEL), 0)
        n8, c8 = next_ref[...], dconv[tm - 8:tm]
        d1_ref[tm - 8:tm, :] = jnp.where(rows >= 7, pltpu.roll(n8, 7, 0), pltpu.roll(c8, 7, 0))
        d2_ref[tm - 8:tm, :] = jnp.where(rows >= 6, pltpu.roll(n8, 6, 0), pltpu.roll(c8, 6, 0))
        next_ref[...] = dconv[0:8]
        d1, d2 = d1_ref[...], d2_ref[...]
        v = cc * u
        dcw_ref[0:1, :] += jnp.sum(d2 * v, axis=0, keepdims=True)
        dcw_ref[1:2, :] += jnp.sum(d1 * v, axis=0, keepdims=True)
        dcw_ref[2:3, :] += jnp.sum(dconv * v, axis=0, keepdims=True)
        dv = cw_ref[0:1, :] * d2 + cw_ref[1:2, :] * d1 + cw_ref[2:3, :] * dconv
        dh += back_through_w_in(1, dv * u)
        dh += back_through_w_in(2, dv * cc)
        xv = x_ref[...]
        r = _rstd(xv)
        xh = xv * r
        h_ref[...] = (xh * an_ref[...]).T.astype(BF16)
        dan_ref[...] += jnp.sum(dh * xh, axis=0, keepdims=True)
        dx_ref[...] = dx1v + _norm_bwd(dh * an_ref[...], xh, r)

        @pl.when(step == n_seq * nt - 1)
        def _():
            for cp in _scatter_copies(ready_refs, part_refs, sems)[1]:
                cp.wait()

    tok = lambda width: pl.BlockSpec((tm, width), lambda s, i: (s * nt + nt - 1 - i, 0))
    t = n_seq * seq
    outs = pl.pallas_call(
        body, name="conv_mixer_bwd", grid=(n_seq, nt),
        in_specs=[tok(D_MODEL), tok(D_MODEL), tok(4 * D_MODEL), tok(D_MODEL), _const_spec((1, D_MODEL)),
                  _const_spec(w_in.shape), _const_spec((3, D_MODEL)), _const_spec(w_out.shape)] + [ANY_SPEC] * n_ready,
        out_specs=[tok(D_MODEL), pl.BlockSpec((D_MODEL, tm), lambda s, i: (0, s * nt + nt - 1 - i)),
                   tok(4 * D_MODEL), _acc_spec((D_MODEL, D_MODEL)),
                   _acc_spec((1, D_MODEL)), _acc_spec((8, D_MODEL))] + [ANY_SPEC] * n_ready,
        out_shape=[jax.ShapeDtypeStruct((t, D_MODEL), F32), jax.ShapeDtypeStruct((D_MODEL, t), BF16),
                   jax.ShapeDtypeStruct((t, 4 * D_MODEL), BF16), jax.ShapeDtypeStruct((D_MODEL, D_MODEL), F32),
                   jax.ShapeDtypeStruct((1, D_MODEL), F32), jax.ShapeDtypeStruct((8, D_MODEL), F32)]
        + [jax.ShapeDtypeStruct(a.shape, a.dtype) for a in ready_stacks],
        scratch_shapes=[pltpu.VMEM((8, D_MODEL), F32), pltpu.VMEM((tm, D_MODEL), F32), pltpu.VMEM((tm, D_MODEL), F32)]
        + _comm_sems(n_ready),
        compiler_params=_params(2),
    )(dx1, x, proj, conv, a_norm, w_in, conv_w, w_out, *ready_stacks)
    return outs[:6], outs[6:]


def _w_in_grad_exchange(a_t, b, n_col, chip, ready_stacks, small):
    r, t = a_t.shape
    bt = 1024
    nk = t // bt
    n_ready = len(ready_stacks)
    n_chip = N_DEV // 2
    n_remote = N_DEV - 2

    def body(chip_ref, a_ref, b_ref, *rest):
        ready_refs, small_ref, rest = rest[:n_ready], rest[n_ready], rest[n_ready + 1:]
        own_ref, got_ref, rest = rest[0], rest[1], rest[2:]
        part_refs, small_all_ref, rest = rest[:n_ready], rest[n_ready], rest[n_ready + 1:]
        acc_ref, land_ref, stage_ref, pair_send, pair_recv, chip_send, chip_recv = rest[:7]
        scatter_sems, gather_sems = rest[7:10], rest[10:13]
        s, k = pl.program_id(0), pl.program_id(1)
        x, y, c, _ = _place()
        chip = 2 * x + y

        @pl.when(jnp.logical_and(s == 0, k == 0))
        def _():
            for cp in _scatter_copies(ready_refs, part_refs, scatter_sems)[0]:
                cp.start()
            for cp in _gather_copies([small_ref], [small_all_ref], gather_sems)[0]:
                cp.start()

        for parity in range(2):
            @pl.when(s % 2 == parity)
            def _(parity=parity):
                @pl.when(k == 0)
                def _():
                    acc_ref[parity] = jnp.zeros((r, n_col), F32)

                acc_ref[parity] += jnp.dot(a_ref[...], b_ref[...], preferred_element_type=F32)

        def to_sibling(step):
            return pltpu.make_async_remote_copy(
                src_ref=acc_ref.at[step % 2], dst_ref=land_ref, send_sem=pair_send.at[step], recv_sem=pair_recv.at[step],
                device_id=(x, y, 1 - c), device_id_type=MESH)

        def to_owner(nth):
            owner_chip = (chip + 1 + nth) % n_chip
            slot = jnp.bitwise_xor(chip, owner_chip) - 1
            return pltpu.make_async_remote_copy(
                src_ref=stage_ref.at[nth % 2], dst_ref=got_ref.at[slot], send_sem=chip_send.at[nth],
                recv_sem=chip_recv.at[slot], device_id=(owner_chip // 2, owner_chip % 2, c), device_id_type=MESH)

        for step in range(N_DEV):
            owner_core = step % 2

            @pl.when(jnp.logical_and(s == step, k == nk - 1))
            def _(step=step, owner_core=owner_core):
                @pl.when(c != owner_core)
                def _():
                    to_sibling(step).start()

                if step >= 1:
                    @pl.when(c == owner_core)
                    def _():
                        to_sibling(step - 1).wait_send()

            if step < N_DEV - 1:
                pickup = jnp.logical_and(s == step + 1, k == min(PICKUP_BLOCK, nk - 1))
            else:
                pickup = jnp.logical_and(s == step, k == nk - 1)

            @pl.when(jnp.logical_and(pickup, c == owner_core))
            def _(step=step):
                to_sibling(step).wait_recv()
                total = acc_ref[step % 2] + land_ref[...]
                if step < n_remote:
                    nth = step // 2
                    if nth >= 2:
                        to_owner(nth - 2).wait_send()
                    stage_ref[nth % 2] = total.astype(BF16)
                    to_owner(nth).start()
                else:
                    own_ref[...] = total

        @pl.when(jnp.logical_and(s == N_DEV - 1, k == nk - 1))
        def _():
            @pl.when(c != (N_DEV - 1) % 2)
            def _():
                to_sibling(N_DEV - 1).wait_send()

            to_owner(1).wait_send()
            to_owner(2).wait_send()
            for slot in range(n_chip - 1):
                pltpu.make_async_remote_copy(
                    src_ref=stage_ref.at[0], dst_ref=got_ref.at[slot], send_sem=chip_send.at[0],
                    recv_sem=chip_recv.at[slot], device_id=(x, y, c), device_id_type=MESH).wait_recv()
            for cp in _scatter_copies(ready_refs, part_refs, scatter_sems)[1]:
                cp.wait()
            for cp in _gather_copies([small_ref], [small_all_ref], gather_sems)[1]:
                cp.wait()

    def owner_block(s, k, chip):
        return (k, 2 * ((chip[0] + 1 + s // 2) % n_chip) + s % 2)

    grid_spec = pltpu.PrefetchScalarGridSpec(
        num_scalar_prefetch=1, grid=(N_DEV, nk),
        in_specs=[pl.BlockSpec((r, bt), lambda s, k, chip: (0, k)), pl.BlockSpec((bt, n_col), owner_block)]
        + [ANY_SPEC] * (n_ready + 1),
        out_specs=[pl.BlockSpec((r, n_col), lambda s, k, chip: (0, 0)), ANY_SPEC] + [ANY_SPEC] * (n_ready + 1),
        scratch_shapes=[pltpu.VMEM((2, r, n_col), F32), pltpu.VMEM((r, n_col), F32), pltpu.VMEM((2, r, n_col), BF16),
                        pltpu.SemaphoreType.DMA((N_DEV,)), pltpu.SemaphoreType.DMA((N_DEV,)),
                        pltpu.SemaphoreType.DMA((n_chip - 1,)), pltpu.SemaphoreType.DMA((n_chip - 1,))]
        + _comm_sems(n_ready) + _comm_sems(1))
    outs = pl.pallas_call(
        body, name="w_in_grad_exchange", grid_spec=grid_spec,
        out_shape=[jax.ShapeDtypeStruct((r, n_col), F32), jax.ShapeDtypeStruct((n_chip - 1, r, n_col), BF16)]
        + [jax.ShapeDtypeStruct(p.shape, p.dtype) for p in ready_stacks] + _stacked([small]),
        compiler_params=_params(2),
    )(chip, a_t, b, *ready_stacks, small)
    return outs[0], outs[1], outs[2:2 + n_ready], outs[2 + n_ready]


def _proj_fwd(x1, pos, rc, kv_norm, w_dkv, ckv_norm, w_ukv, b_norm, b_w_in, q_norm, w_uq):
    t = x1.shape[0]
    tm = LATENT_ROW_TILE

    def body(x1_ref, pos_ref, rc_ref, kvn_ref, wdkv_ref, ckvn_ref, wukv_ref, bn_ref, bwin_ref, qn_ref, wuq_ref,
             ckr_ref, cq_ref, gb_ref, q_ref, k_ref, v_ref, cos_ref, sin_ref):
        cosv, sinv = _rope_angles(pos_ref[...], rc_ref[...])
        cos_ref[...] = cosv
        sin_ref[...] = sinv
        ct, s1, s2 = _rope_tables(cosv, sinv, rc_ref[...])
        xv = x1_ref[...]
        xh = xv * _rstd(xv)
        ckr = _dot(xh * kvn_ref[...], wdkv_ref[...])
        ckr_ref[...] = ckr
        ckv = ckr[:, 0:KV_RANK]
        kv = _dot(ckv * _rstd(ckv) * ckvn_ref[...], wukv_ref[...])
        k_rope = _rope(ckr[:, KV_RANK:CKR_PAD], ct, s1, s2)
        lane = lax.broadcasted_iota(jnp.int32, (tm, HEAD_PAD), 1)
        low = lane < QK_NOPE
        for h in range(N_HEADS):
            kv_h = kv[:, h * HEAD_PAD:(h + 1) * HEAD_PAD]
            k_ref[:, h * HEAD_PAD:(h + 1) * HEAD_PAD] = jnp.where(low, kv_h, k_rope).astype(BF16)
            v_ref[:, h * HEAD_PAD:(h + 1) * HEAD_PAD] = jnp.where(low, pltpu.roll(kv_h, V_HEAD, 1), 1.0).astype(BF16)
        pb = _dot(xh * bn_ref[...], bwin_ref[...])
        cq = pb[:, 0:Q_RANK]
        cq_ref[...] = cq
        gb_ref[...] = pb[:, Q_RANK:Q_RANK + E_B]
        q = _dot(cq * _rstd(cq) * qn_ref[...], wuq_ref[...])
        for h in range(N_HEADS):
            q_ref[:, h * HEAD_PAD:(h + 1) * HEAD_PAD] = _rope(
                q[:, h * HEAD_PAD:(h + 1) * HEAD_PAD], ct, s1, s2).astype(BF16)

    tok = lambda width: pl.BlockSpec((tm, width), lambda i: (i, 0))
    weights = [kv_norm, w_dkv, ckv_norm, w_ukv, b_norm, b_w_in, q_norm, w_uq]
    wide = N_HEADS * HEAD_PAD
    return pl.pallas_call(
        body, name="latent_proj_fwd", grid=(t // tm,),
        in_specs=[tok(D_MODEL), tok(1), _const_spec(rc.shape)] + [_const_spec(w.shape) for w in weights],
        out_specs=[tok(CKR_PAD), tok(Q_RANK), tok(E_B), tok(wide), tok(wide), tok(wide), tok(HEAD_PAD), tok(HEAD_PAD)],
        out_shape=[jax.ShapeDtypeStruct((t, CKR_PAD), F32), jax.ShapeDtypeStruct((t, Q_RANK), F32),
                   jax.ShapeDtypeStruct((t, E_B), F32), jax.ShapeDtypeStruct((t, wide), BF16),
                   jax.ShapeDtypeStruct((t, wide), BF16), jax.ShapeDtypeStruct((t, wide), BF16),
                   jax.ShapeDtypeStruct((t, HEAD_PAD), F32), jax.ShapeDtypeStruct((t, HEAD_PAD), F32)],
        compiler_params=_params(1),
    )(x1, pos, rc, *weights)


def _proj_bwd(dq, dk, dv, dgb, cq, ckr, x1, dx2, cosv, sinv, rc, kv_norm, w_dkv, ckv_norm, w_ukv, b_norm, b_w_in, q_norm, w_uq):
    t = x1.shape[0]
    tm = LATENT_ROW_TILE
    wide = N_HEADS * HEAD_PAD

    def body(dq_ref, dk_ref, dv_ref, dgb_ref, cq_ref, ckr_ref, x1_ref, dx2_ref, cos_ref, sin_ref, rc_ref,
             kvn_ref, wdkv_ref, ckvn_ref, wukv_ref, bn_ref, bwin_ref, qn_ref, wuq_ref,
             dx1_ref, dwuq_ref, dwbin_ref, dwukv_ref, dwdkv_ref, dqn_ref, dbn_ref, dckvn_ref, dkvn_ref, dqu_ref, dkv_ref):
        @pl.when(pl.program_id(0) == 0)
        def _():
            for ref in (dwuq_ref, dwbin_ref, dwukv_ref, dwdkv_ref, dqn_ref, dbn_ref, dckvn_ref, dkvn_ref):
                ref[...] = jnp.zeros_like(ref)

        ct, s1, s2 = _rope_tables(cos_ref[...], sin_ref[...], rc_ref[...])
        lane = lax.broadcasted_iota(jnp.int32, (tm, HEAD_PAD), 1)
        low = lane < QK_NOPE
        for h in range(N_HEADS):
            dqu_ref[:, h * HEAD_PAD:(h + 1) * HEAD_PAD] = _rope(
                dq_ref[:, h * HEAD_PAD:(h + 1) * HEAD_PAD], ct, -s1, -s2).astype(BF16)
        cq = cq_ref[...]
        rq = _rstd(cq)
        cqh = cq * rq
        cqn = (cqh * qn_ref[...]).astype(BF16)
        dwuq = lax.dot_general(cqn, dqu_ref[...], TN, preferred_element_type=F32)
        for h in range(N_HEADS):
            dwuq_ref[h] += dwuq[:, h * HEAD_PAD:(h + 1) * HEAD_PAD]
        dcqn = lax.dot_general(dqu_ref[...], wuq_ref[...], NT, preferred_element_type=F32)
        dqn_ref[...] += jnp.sum(dcqn * cqh, axis=0, keepdims=True)
        dcq = _norm_bwd(dcqn * qn_ref[...], cqh, rq)
        dpb = jnp.concatenate([dcq, dgb_ref[...]], axis=1).astype(BF16)
        xv = x1_ref[...]
        r = _rstd(xv)
        xh = xv * r
        dwbin_ref[...] += _dot_tn(xh * bn_ref[...], dpb)
        dh3 = lax.dot_general(dpb, bwin_ref[...], NT, preferred_element_type=F32)
        dk_rope = jnp.zeros((tm, HEAD_PAD), F32)
        for h in range(N_HEADS):
            dk_h = dk_ref[:, h * HEAD_PAD:(h + 1) * HEAD_PAD]
            dv_h = dv_ref[:, h * HEAD_PAD:(h + 1) * HEAD_PAD]
            dkv_ref[:, h * HEAD_PAD:(h + 1) * HEAD_PAD] = jnp.where(low, dk_h, pltpu.roll(dv_h, V_HEAD, 1)).astype(BF16)
            dk_rope += dk_h
        rope_lanes = jnp.logical_and(lane >= ROPE_LO, lane < ROPE_LO + QK_ROPE)
        dk_rope = jnp.where(rope_lanes, _rope(dk_rope, ct, -s1, -s2), 0.0)
        ckv = ckr_ref[:, 0:KV_RANK]
        rk = _rstd(ckv)
        ckh = ckv * rk
        ckvn = (ckh * ckvn_ref[...]).astype(BF16)
        dwukv = lax.dot_general(ckvn, dkv_ref[...], TN, preferred_element_type=F32)
        for h in range(N_HEADS):
            dwukv_ref[h] += dwukv[:, h * HEAD_PAD:(h + 1) * HEAD_PAD]
        dckvn = lax.dot_general(dkv_ref[...], wukv_ref[...], NT, preferred_element_type=F32)
        dckvn_ref[...] += jnp.sum(dckvn * ckh, axis=0, keepdims=True)
        dckv = _norm_bwd(dckvn * ckvn_ref[...], ckh, rk)
        dckr = jnp.concatenate([dckv, dk_rope], axis=1).astype(BF16)
        dwdkv_ref[...] += _dot_tn(xh * kvn_ref[...], dckr)
        dh2 = lax.dot_general(dckr, wdkv_ref[...], NT, preferred_element_type=F32)
        dkvn_ref[...] += jnp.sum(dh2 * xh, axis=0, keepdims=True)
        dbn_ref[...] += jnp.sum(dh3 * xh, axis=0, keepdims=True)
        dx1_ref[...] = dx2_ref[...] + _norm_bwd(dh2 * kvn_ref[...] + dh3 * bn_ref[...], xh, r)

    tok = lambda width: pl.BlockSpec((tm, width), lambda i: (i, 0))
    weights = [kv_norm, w_dkv, ckv_norm, w_ukv, b_norm, b_w_in, q_norm, w_uq]
    acc_shapes = [(N_HEADS, Q_RANK, HEAD_PAD), (D_MODEL, Q_RANK + E_B), (N_HEADS, KV_RANK, HEAD_PAD), (D_MODEL, CKR_PAD),
                  (1, Q_RANK), (1, D_MODEL), (1, KV_RANK), (1, D_MODEL)]
    return pl.pallas_call(
        body, name="latent_proj_bwd", grid=(t // tm,),
        in_specs=[tok(wide), tok(wide), tok(wide), tok(E_B), tok(Q_RANK), tok(CKR_PAD), tok(D_MODEL), tok(D_MODEL),
                  tok(HEAD_PAD), tok(HEAD_PAD), _const_spec(rc.shape)] + [_const_spec(w.shape) for w in weights],
        out_specs=[tok(D_MODEL)] + [_acc_spec(s) for s in acc_shapes],
        out_shape=[jax.ShapeDtypeStruct((t, D_MODEL), F32)] + [jax.ShapeDtypeStruct(s, F32) for s in acc_shapes],
        scratch_shapes=[pltpu.VMEM((tm, wide), BF16), pltpu.VMEM((tm, wide), BF16)],
        compiler_params=_params(1),
    )(dq, dk, dv, dgb, cq, ckr, x1, dx2, cosv, sinv, rc, *weights)


def _attn_fwd(q, k, v, n_seq, seq):
    tb, group = ATTN_TILE, ATTN_GROUP
    rows = tb * group
    nq = seq // rows
    pair = 2 * HEAD_PAD

    def body(q_ref, k_ref, v_ref, o_ref, lr_ref):
        i = pl.program_id(2)
        row = lax.broadcasted_iota(jnp.int32, (tb, tb), 0)
        col = lax.broadcasted_iota(jnp.int32, (tb, tb), 1)
        causal = col <= row

        def step(j, carry, kinds):
            start = pl.multiple_of(j * tb, tb)
            out = []
            for g in range(group):
                for hh in range(2):
                    m, acc = carry[2 * g + hh]
                    if kinds[g] != "none":
                        heads = slice(hh * HEAD_PAD, (hh + 1) * HEAD_PAD)
                        s = lax.dot_general(q_ref[g * tb:(g + 1) * tb, heads], k_ref[pl.ds(start, tb), heads], NT,
                                            preferred_element_type=F32) * (SOFTMAX_SCALE * LOG2_E)
                        if kinds[g] == "diagonal":
                            s = jnp.where(causal, s, -jnp.inf)
                        m_new = jnp.maximum(m, jnp.max(s, axis=-1, keepdims=True))
                        p = jnp.exp2(s - m_new)
                        acc = jnp.exp2(m - m_new) * acc + jnp.dot(
                            p.astype(BF16), v_ref[pl.ds(start, tb), heads], preferred_element_type=F32)
                        m = m_new
                    out.append((m, acc))
            return tuple(out)

        one = (jnp.full((tb, 1), -jnp.inf, F32), jnp.zeros((tb, HEAD_PAD), F32))
        carry = lax.fori_loop(0, i * group, functools.partial(step, kinds=("full",) * group), (one,) * (2 * group))
        for d in range(group):
            kinds = tuple("none" if g < d else ("diagonal" if g == d else "full") for g in range(group))
            carry = step(i * group + d, carry, kinds)
        lane = lax.broadcasted_iota(jnp.int32, (tb, HEAD_PAD), 1)
        low = lane < V_HEAD
        for g in range(group):
            halves = []
            for hh in range(2):
                m, acc = carry[2 * g + hh]
                swapped = pltpu.roll(acc, V_HEAD, 1)
                halves.append(acc / swapped)
                lse = (m * LN_2) + jnp.log(jnp.where(low, swapped, acc))
                lr_ref[hh, :, g * tb:(g + 1) * tb] = lse.T[0:8, :]
            o_ref[g * tb:(g + 1) * tb, :] = jnp.where(low, halves[0], pltpu.roll(halves[1], V_HEAD, 1))

    t = n_seq * seq
    return pl.pallas_call(
        body, name="attention_fwd", grid=(n_seq, N_HEADS // 2, nq),
        in_specs=[pl.BlockSpec((rows, pair), lambda s, p, i: (s * nq + i, p)),
                  pl.BlockSpec((seq, pair), lambda s, p, i: (s, p)),
                  pl.BlockSpec((seq, pair), lambda s, p, i: (s, p))],
        out_specs=[pl.BlockSpec((rows, HEAD_PAD), lambda s, p, i: (s * nq + i, p)),
                   pl.BlockSpec((None, 2, 8, rows), lambda s, p, i: (s, p, 0, i))],
        out_shape=[jax.ShapeDtypeStruct((t, E_B), F32), jax.ShapeDtypeStruct((n_seq, N_HEADS, 8, seq), F32)],
        compiler_params=_params(3),
    )(q, k, v)


def _attn_bwd(q, k, v, do, lse_row, delta_row, n_seq, seq):
    tb, group = ATTN_TILE, ATTN_BWD_GROUP
    rows = tb * group
    nk = seq // rows
    n_inner = seq // tb
    pair = 2 * HEAD_PAD

    def body(q_ref, k_ref, v_ref, do_ref, lr_ref, dr_ref, dk_ref, dv_ref, dq_ref):
        j = pl.program_id(2)

        @pl.when(j == 0)
        def _():
            dq_ref[...] = jnp.zeros_like(dq_ref)

        row = lax.broadcasted_iota(jnp.int32, (tb, tb), 0)
        col = lax.broadcasted_iota(jnp.int32, (tb, tb), 1)
        causal = col >= row

        def step(i, carry, kinds):
            start = pl.multiple_of(i * tb, tb)
            out = []
            for hh in range(2):
                heads = slice(hh * HEAD_PAD, (hh + 1) * HEAD_PAD)
                qi = q_ref[pl.ds(start, tb), heads]
                doi = do_ref[pl.ds(start, tb), heads]
                dq = None
                for g in range(group):
                    dk_acc, dv_acc = carry[hh * group + g]
                    if kinds[g] != "none":
                        kg = k_ref[g * tb:(g + 1) * tb, heads]
                        st = lax.dot_general(kg, qi, NT, preferred_element_type=F32) * SOFTMAX_SCALE
                        pt = jnp.exp(st - lr_ref[hh, 0:1, pl.ds(start, tb)])
                        if kinds[g] == "diagonal":
                            pt = jnp.where(causal, pt, 0.0)
                        dv_acc = dv_acc + jnp.dot(pt.astype(BF16), doi, preferred_element_type=F32)
                        dpt = lax.dot_general(v_ref[g * tb:(g + 1) * tb, heads], doi, NT, preferred_element_type=F32)
                        dst = (pt * (dpt - dr_ref[hh, 0:1, pl.ds(start, tb)]) * SOFTMAX_SCALE).astype(BF16)
                        dk_acc = dk_acc + jnp.dot(dst, qi, preferred_element_type=F32)
                        term = lax.dot_general(dst, kg, TN, preferred_element_type=F32)
                        dq = term if dq is None else dq + term
                    out.append((dk_acc, dv_acc))
                dq_ref[pl.ds(start, tb), heads] += dq
            return tuple(out)

        one = (jnp.zeros((tb, HEAD_PAD), F32), jnp.zeros((tb, HEAD_PAD), F32))
        carry = (one,) * (2 * group)
        for d in range(group):
            kinds = tuple("full" if g < d else ("diagonal" if g == d else "none") for g in range(group))
            carry = step(j * group + d, carry, kinds)
        carry = lax.fori_loop((j + 1) * group, n_inner, functools.partial(step, kinds=("full",) * group), carry)
        for hh in range(2):
            for g in range(group):
                dk_ref[g * tb:(g + 1) * tb, hh * HEAD_PAD:(hh + 1) * HEAD_PAD] = carry[hh * group + g][0]
                dv_ref[g * tb:(g + 1) * tb, hh * HEAD_PAD:(hh + 1) * HEAD_PAD] = carry[hh * group + g][1]

    t = n_seq * seq
    wide = N_HEADS * HEAD_PAD
    return pl.pallas_call(
        body, name="attention_bwd", grid=(n_seq, N_HEADS // 2, nk),
        in_specs=[pl.BlockSpec((seq, pair), lambda s, p, j: (s, p)),
                  pl.BlockSpec((rows, pair), lambda s, p, j: (s * nk + j, p)),
                  pl.BlockSpec((rows, pair), lambda s, p, j: (s * nk + j, p)),
                  pl.BlockSpec((seq, pair), lambda s, p, j: (s, p)),
                  pl.BlockSpec((None, 2, 8, seq), lambda s, p, j: (s, p, 0, 0)),
                  pl.BlockSpec((None, 2, 8, seq), lambda s, p, j: (s, p, 0, 0))],
        out_specs=[pl.BlockSpec((rows, pair), lambda s, p, j: (s * nk + j, p)),
                   pl.BlockSpec((rows, pair), lambda s, p, j: (s * nk + j, p)),
                   pl.BlockSpec((seq, pair), lambda s, p, j: (s, p))],
        out_shape=[jax.ShapeDtypeStruct((t, wide), F32), jax.ShapeDtypeStruct((t, wide), F32),
                   jax.ShapeDtypeStruct((t, wide), F32)],
        compiler_params=_params(3),
    )(q, k, v, do, lse_row, delta_row)


def _head_tail(o, gb, x1, w_out, final_norm, target, n_seq, seq):
    tm = LATENT_ROW_TILE
    nt = seq // tm
    n_col = D_MODEL // N_DEV

    def body(o_ref, gb_ref, x1_ref, wout_ref, fn_ref, tgt_ref,
             dx2_ref, do_ref, dgb_ref, dr_ref, dwout_ref, dfn_ref, loss_ref):
        first = jnp.logical_and(pl.program_id(0) == 0, pl.program_id(1) == 0)

        @pl.when(first)
        def _():
            dwout_ref[...] = jnp.zeros_like(dwout_ref)
            dfn_ref[...] = jnp.zeros_like(dfn_ref)
            loss_ref[...] = jnp.zeros_like(loss_ref)

        ov, g = o_ref[...], gb_ref[...]
        silu, dsilu = _silu_parts(g)
        gated = (ov * silu).astype(BF16)
        x2 = x1_ref[...] + jnp.dot(gated, wout_ref[...], preferred_element_type=F32)
        r = _rstd(x2)
        xh = x2 * r
        err = xh * fn_ref[...] - tgt_ref[...]
        loss_ref[...] += 0.5 * jnp.sum(jnp.mean(err * err, axis=-1, keepdims=True), axis=0, keepdims=True)
        dy = err / D_MODEL
        dfn_ref[...] += jnp.sum(dy * xh, axis=0, keepdims=True)
        dx2 = _norm_bwd(dy * fn_ref[...], xh, r)
        dx2_ref[...] = dx2
        dx2b = dx2.astype(BF16)
        dw = lax.dot_general(gated, dx2b, TN, preferred_element_type=F32)
        for d in range(N_DEV):
            dwout_ref[d] += dw[:, d * n_col:(d + 1) * n_col]
        dgated = lax.dot_general(dx2b, wout_ref[...], NT, preferred_element_type=F32)
        do = dgated * silu
        dgb_ref[...] = dgated * ov * dsilu
        prod = do * ov
        lane = lax.broadcasted_iota(jnp.int32, (tm, HEAD_PAD), 1)
        low = lane < V_HEAD
        for p in range(N_HEADS // 2):
            blk = prod[:, p * HEAD_PAD:(p + 1) * HEAD_PAD]
            do_pair = do[:, p * HEAD_PAD:(p + 1) * HEAD_PAD]
            for hh in range(2):
                h = 2 * p + hh
                mine = do_pair if hh == 0 else pltpu.roll(do_pair, V_HEAD, 1)
                do_ref[:, h * HEAD_PAD:(h + 1) * HEAD_PAD] = jnp.where(low, mine, 0.0).astype(BF16)
                delta = jnp.sum(jnp.where(low if hh == 0 else ~low, blk, 0.0), axis=-1, keepdims=True)
                dr_ref[h] = jnp.broadcast_to(delta, (tm, HEAD_PAD)).T[0:8, :]

    tok = lambda width: pl.BlockSpec((tm, width), lambda s, i: (s * nt + i, 0))
    t = n_seq * seq
    return pl.pallas_call(
        body, name="head_tail", grid=(n_seq, nt),
        in_specs=[tok(E_B), tok(E_B), tok(D_MODEL), _const_spec(w_out.shape), _const_spec((1, D_MODEL)), tok(D_MODEL)],
        out_specs=[tok(D_MODEL), tok(N_HEADS * HEAD_PAD), tok(E_B),
                   pl.BlockSpec((None, N_HEADS, 8, tm), lambda s, i: (s, 0, 0, i)),
                   _acc_spec((N_DEV, E_B, n_col)), _acc_spec((1, D_MODEL)), _acc_spec((1, 1))],
        out_shape=[jax.ShapeDtypeStruct((t, D_MODEL), F32), jax.ShapeDtypeStruct((t, N_HEADS * HEAD_PAD), BF16),
                   jax.ShapeDtypeStruct((t, E_B), F32),
                   jax.ShapeDtypeStruct((n_seq, N_HEADS, 8, seq), F32), jax.ShapeDtypeStruct((N_DEV, E_B, n_col), F32),
                   jax.ShapeDtypeStruct((1, D_MODEL), F32), jax.ShapeDtypeStruct((1, 1), F32)],
        compiler_params=_params(2),
    )(o, gb, x1, w_out, final_norm, target)


def _adamw_math(w, g, m, v):
    m = ADAM_B1 * m + (1.0 - ADAM_B1) * g
    v = ADAM_B2 * v + (1.0 - ADAM_B2) * jnp.square(g)
    m_hat = m / (1.0 - ADAM_B1 ** ADAM_STEP)
    v_hat = v / (1.0 - ADAM_B2 ** ADAM_STEP)
    delta = -ADAM_LR * (m_hat / (jnp.sqrt(v_hat) + ADAM_EPS) + ADAM_WD * w)
    return delta, m, v


def _adamw_reduce(own, parts, w, m, v):
    rows, cols = w.shape
    br = 256 if rows % 256 == 0 else 128
    n_parts = parts.shape[0]

    def body(*refs):
        p_ref, w_ref, m_ref, v_ref, g_ref, d_ref, nm_ref, nv_ref = refs[-8:]
        g = p_ref[0].astype(F32) if own is None else refs[0][...] + p_ref[0].astype(F32)
        for k in range(1, n_parts):
            g = g + p_ref[k].astype(F32)
        g_ref[...] = g
        d_ref[...], nm_ref[...], nv_ref[...] = _adamw_math(w_ref[...], g, m_ref[...], v_ref[...])

    blk = pl.BlockSpec((br, cols), lambda i: (i, 0))
    first = [] if own is None else [own]
    return pl.pallas_call(
        body, name="adamw_reduce", grid=(rows // br,),
        in_specs=[blk] * len(first) + [pl.BlockSpec((n_parts, br, cols), lambda i: (0, i, 0)), blk, blk, blk],
        out_specs=[blk] * 4, out_shape=[jax.ShapeDtypeStruct((rows, cols), F32)] * 4,
        compiler_params=_params(1),
    )(*first, parts, w, m, v)


def _sum_parts(parts):
    def body(p_ref, o_ref):
        g = p_ref[0]
        for k in range(1, N_DEV):
            g = g + p_ref[k]
        o_ref[...] = g

    return pl.pallas_call(body, name="small_grad_sum", out_shape=jax.ShapeDtypeStruct(parts.shape[1:], F32))(parts)


def _adamw_small(gs, ws, ms, vs):
    n = len(gs)

    def body(*refs):
        ins, outs = refs[:4 * n], refs[4 * n:]
        for j in range(n):
            g_ref, w_ref, m_ref, v_ref = ins[j], ins[n + j], ins[2 * n + j], ins[3 * n + j]
            outs[3 * j][...], outs[3 * j + 1][...], outs[3 * j + 2][...] = _adamw_math(
                w_ref[...], g_ref[...], m_ref[...], v_ref[...])

    shapes = [jax.ShapeDtypeStruct(g.shape, F32) for g in gs for _ in range(3)]
    flat = pl.pallas_call(body, name="adamw_small", out_shape=shapes)(*gs, *ws, *ms, *vs)
    return [tuple(flat[3 * j:3 * j + 3]) for j in range(n)]


SMALL_ROWS = 8
LOSS_LANE = D_MODEL - 1


def _pack_small(g_fn, g_kvn, g_bn, g_an, g_ckvn, g_qn, g_conv, loss_part):
    def body(fn_ref, kvn_ref, bn_ref, an_ref, ckvn_ref, qn_ref, conv_ref, loss_ref, o_ref):
        o_ref[0:1, :] = fn_ref[...]
        o_ref[1:2, :] = kvn_ref[...]
        o_ref[2:3, :] = bn_ref[...]
        o_ref[3:4, :] = an_ref[...]
        lane = lax.broadcasted_iota(jnp.int32, (1, D_MODEL), 1)
        o_ref[4:5, :] = jnp.where(lane == LOSS_LANE, loss_ref[...], 0.0)
        o_ref[4:5, 0:KV_RANK] = ckvn_ref[...]
        o_ref[4:5, KV_RANK:KV_RANK + Q_RANK] = qn_ref[...]
        o_ref[5:8, :] = conv_ref[0:3, :]

    return pl.pallas_call(body, name="pack_small", out_shape=jax.ShapeDtypeStruct((SMALL_ROWS, D_MODEL), F32))(
        g_fn, g_kvn, g_bn, g_an, g_ckvn, g_qn, g_conv, loss_part)


def _pad_cols(a, width):
    return jnp.pad(a, ((0, 0), (0, width - a.shape[1])))


def _dkv_to_padded(a):
    r = a.shape[0]
    z = jnp.zeros((r, ROPE_LO), a.dtype)
    z2 = jnp.zeros((r, HEAD_PAD - ROPE_LO - QK_ROPE), a.dtype)
    return jnp.concatenate([a[:, :KV_RANK], z, a[:, KV_RANK:], z2], axis=1)


def _dkv_from_padded(a):
    return jnp.concatenate([a[:, :KV_RANK], a[:, KV_RANK + ROPE_LO:KV_RANK + ROPE_LO + QK_ROPE]], axis=1)


def _unstack_cols(a):
    return jnp.transpose(a, (1, 0, 2)).reshape(a.shape[1], N_DEV * a.shape[2])


def kernel(x, positions, a_norm, a_w_in, a_conv, a_w_out, kv_norm, w_dkv, ckv_norm, w_ukv, b_norm, b_w_in, b_q_norm, b_w_uq, b_w_out, final_norm, loss_target, m_a_norm, m_a_w_in, m_a_conv, m_a_w_out, m_kv_norm, m_w_dkv, m_ckv_norm, m_w_ukv, m_b_norm, m_b_w_in, m_b_q_norm, m_b_w_uq, m_b_w_out, m_final_norm, v_a_norm, v_a_w_in, v_a_conv, v_a_w_out, v_kv_norm, v_w_dkv, v_ckv_norm, v_w_ukv, v_b_norm, v_b_w_in, v_b_q_norm, v_b_w_uq, v_b_w_out, v_final_norm):
    n_seq, seq, _ = x.shape
    t = n_seq * seq
    me = 4 * lax.axis_index("x") + 2 * lax.axis_index("y") + lax.axis_index("c")

    big = {
        "a_w_in": (a_w_in[0], m_a_w_in[0], v_a_w_in[0]),
        "a_w_out": (a_w_out[0], m_a_w_out[0], v_a_w_out[0]),
        "w_dkv": tuple(_dkv_to_padded(a) for a in (w_dkv, m_w_dkv, v_w_dkv)),
        "w_ukv": (w_ukv, m_w_ukv, v_w_ukv),
        "b_w_in": (b_w_in[0], m_b_w_in[0], v_b_w_in[0]),
        "b_w_uq": tuple(_pad_cols(a[0], HEAD_PAD) for a in (b_w_uq, m_b_w_uq, v_b_w_uq)),
        "b_w_out": (b_w_out[0], m_b_w_out[0], v_b_w_out[0]),
    }
    names = list(big)
    first_names, later_names = names[:2], names[2:]
    gathered = _all_gather([big[n][0].astype(BF16) for n in first_names] + [a_norm, a_conv[0]])
    a_norm_f = gathered[2].reshape(1, D_MODEL)
    a_conv_f = _unstack_cols(gathered[3])
    w_a_in = gathered[0]
    w_a_out = gathered[1].reshape(D_MODEL, D_MODEL)

    x2d = x.reshape(t, D_MODEL)
    tgt = loss_target.reshape(t, D_MODEL)
    pos = positions.astype(F32).reshape(t, 1)
    rc = _rope_consts()
    kvn, ckvn, bn, qn, fn = kv_norm.reshape(1, -1), ckv_norm.reshape(1, -1), b_norm, b_q_norm, final_norm.reshape(1, -1)

    x1, proj, conv, later = _conv_fwd(x2d, a_norm_f, w_a_in, a_conv_f, w_a_out, n_seq, seq,
                                      [big[n][0].astype(BF16) for n in later_names])
    full = dict(zip(later_names, later))
    w_dkv_f = full["w_dkv"].reshape(D_MODEL, CKR_PAD)
    w_ukv_f = _unstack_cols(full["w_ukv"])
    w_b_in = full["b_w_in"].reshape(D_MODEL, Q_RANK + E_B)
    w_uq_f = _unstack_cols(full["b_w_uq"])
    w_b_out = _unstack_cols(full["b_w_out"])
    layer_b = (kvn, w_dkv_f, ckvn, w_ukv_f, bn, w_b_in, qn, w_uq_f)
    ckr, cq, gb, q, k, v, cosv, sinv = _proj_fwd(x1, pos, rc, *layer_b)
    o, lse_row = _attn_fwd(q, k, v, n_seq, seq)
    dx2, do, dgb, delta_row, g_b_out, g_fn, loss_part = _head_tail(o, gb, x1, w_b_out, fn, tgt, n_seq, seq)
    dk, dv, dq = _attn_bwd(q, k, v, do, lse_row, delta_row, n_seq, seq)
    dx1, g_uq, g_b_in, g_ukv, g_dkv, g_qn, g_bn, g_ckvn, g_kvn = _proj_bwd(
        dq, dk, dv, dgb, cq, ckr, x1, dx2, cosv, sinv, rc, *layer_b)
    stacks = {
        "w_dkv": g_dkv.reshape(N_DEV, D_MODEL // N_DEV, CKR_PAD),
        "w_ukv": g_ukv,
        "b_w_in": g_b_in.reshape(N_DEV, D_MODEL // N_DEV, Q_RANK + E_B),
        "b_w_uq": g_uq,
        "b_w_out": g_b_out,
    }
    (dx, h_a, dproj, g_a_out, g_an, g_conv), later_parts = _conv_bwd(
        dx1, x2d, proj, conv, a_norm_f, w_a_in, a_conv_f, w_a_out, n_seq, seq, [stacks[n] for n in later_names])
    parts = dict(zip(later_names, later_parts))
    small = _pack_small(g_fn, g_kvn, g_bn, g_an, g_ckvn, g_qn, g_conv, loss_part)
    a_in_own, a_in_got, (parts["a_w_out"],), small_parts = _w_in_grad_exchange(
        h_a, dproj, D_MODEL * 4 // N_DEV, (me // 2).reshape(1).astype(jnp.int32),
        [g_a_out.reshape(N_DEV, D_MODEL // N_DEV, D_MODEL)], small)

    outs = {"a_w_in": _adamw_reduce(a_in_own, a_in_got, *big["a_w_in"])}
    for n in names[1:]:
        outs[n] = _adamw_reduce(None, parts[n], *big[n])
    outs["w_dkv"] = tuple(_dkv_from_padded(a) for a in outs["w_dkv"])
    outs["b_w_uq"] = tuple(a[:, :QK_NOPE + QK_ROPE] for a in outs["b_w_uq"])
    for n in ("a_w_in", "a_w_out", "b_w_in", "b_w_uq", "b_w_out"):
        outs[n] = tuple(a[None] for a in outs[n])

    total = _sum_parts(small_parts)
    loss = total[4, LOSS_LANE]
    shard = D_MODEL // N_DEV
    g_small = {
        "final_norm": total[0], "kv_norm": total[1], "b_norm": total[2:3],
        "a_norm": lax.dynamic_slice_in_dim(total[3:4], me * shard, shard, axis=1),
        "ckv_norm": total[4, 0:KV_RANK], "b_q_norm": total[4:5, KV_RANK:KV_RANK + Q_RANK],
        "a_conv": lax.dynamic_slice_in_dim(total[5:8], me * shard, shard, axis=1)[None],
    }
    small_state = {
        "final_norm": (final_norm, m_final_norm, v_final_norm), "kv_norm": (kv_norm, m_kv_norm, v_kv_norm),
        "b_norm": (b_norm, m_b_norm, v_b_norm), "a_norm": (a_norm, m_a_norm, v_a_norm),
        "ckv_norm": (ckv_norm, m_ckv_norm, v_ckv_norm), "b_q_norm": (b_q_norm, m_b_q_norm, v_b_q_norm),
        "a_conv": (a_conv, m_a_conv, v_a_conv),
    }
    small_names = list(g_small)
    as2d = lambda a: a.reshape(-1, a.shape[-1])
    upd = _adamw_small([as2d(g_small[n]) for n in small_names],
                       *[[as2d(small_state[n][j]) for n in small_names] for j in range(3)])
    for n, u in zip(small_names, upd):
        outs[n] = (g_small[n],) + tuple(a.reshape(g_small[n].shape) for a in u)

    order = ["a_norm", "a_w_in", "a_conv", "a_w_out", "kv_norm", "w_dkv", "ckv_norm", "w_ukv", "b_norm", "b_w_in",
             "b_q_norm", "b_w_uq", "b_w_out", "final_norm"]
    result = [loss, dx.reshape(n_seq, seq, D_MODEL)]
    for j in range(4):
        result += [outs[n][j] for n in order]
    return tuple(result)
```

```python
import functools
import math

import numpy as np
import jax
import jax.numpy as jnp
from jax import lax
from jax.experimental import pallas as pl
from jax.experimental.pallas import tpu as pltpu

F32 = jnp.float32
BF16 = jnp.bfloat16

D_MODEL = 1024
N_HEADS = 8
QK_NOPE = 64
QK_ROPE = 32
V_HEAD = 64
KV_RANK = 256
Q_RANK = 384
E_B = N_HEADS * V_HEAD
HEAD_PAD = 128
CKR_PAD = KV_RANK + HEAD_PAD
ROPE_LO = QK_NOPE
ROPE_HALF = QK_ROPE // 2
ROPE_THETA = 10000.0
SOFTMAX_SCALE = 1.0 / math.sqrt(QK_NOPE + QK_ROPE)
LOG2_E = math.log2(math.e)
LN_2 = math.log(2.0)
EPS = 1e-6
N_DEV = 8

ADAM_LR = 0.001
ADAM_B1 = 0.9
ADAM_B2 = 0.999
ADAM_EPS = 1e-08
ADAM_WD = 0.01
ADAM_STEP = 10

ROW_TILE = 256
LATENT_ROW_TILE = 512
PICKUP_BLOCK = 3
ATTN_TILE = 512
ATTN_GROUP = 4
ATTN_BWD_GROUP = 4
VMEM_LIMIT = 56 * 1024 * 1024

MESH = pl.DeviceIdType.MESH
NT = (((1,), (1,)), ((), ()))
TN = (((0,), (0,)), ((), ()))


def _dot(a, b):
    return jnp.dot(a.astype(BF16), b.astype(BF16), preferred_element_type=F32)


def _dot_nt(a, b):
    return lax.dot_general(a.astype(BF16), b.astype(BF16), NT, preferred_element_type=F32)


def _dot_tn(a, b):
    return lax.dot_general(a.astype(BF16), b.astype(BF16), TN, preferred_element_type=F32)


def _rstd(x):
    return lax.rsqrt(jnp.mean(x * x, axis=-1, keepdims=True) + EPS)


def _norm_bwd(a, xh, r):
    return r * (a - xh * jnp.mean(a * xh, axis=-1, keepdims=True))


def _silu_parts(g):
    sg = jax.nn.sigmoid(g)
    return g * sg, sg * (1.0 + g * (1.0 - sg))


def _rope_consts():
    inv = (ROPE_THETA ** (-np.arange(0, QK_ROPE, 2, dtype=np.float32) / QK_ROPE)).astype(np.float32)
    t = np.zeros((8, HEAD_PAD), np.float32)
    t[0, ROPE_LO:ROPE_LO + ROPE_HALF] = inv
    t[0, ROPE_LO + ROPE_HALF:ROPE_LO + QK_ROPE] = inv
    t[1, ROPE_LO:ROPE_LO + ROPE_HALF] = -1.0
    t[2, ROPE_LO + ROPE_HALF:ROPE_LO + QK_ROPE] = 1.0
    return jnp.asarray(t)


def _rope_angles(pos, rc):
    ang = pos * rc[0:1, :]
    return jnp.cos(ang), jnp.sin(ang)


def _rope_tables(cosv, sinv, rc):
    return cosv, sinv * rc[1:2, :], sinv * rc[2:3, :]


def _rope(x, ct, s1, s2):
    up = pltpu.roll(x, HEAD_PAD - ROPE_HALF, 1)
    dn = pltpu.roll(x, ROPE_HALF, 1)
    return x * ct + up * s1 + dn * s2


def _const_spec(shape):
    nd = len(shape)
    return pl.BlockSpec(shape, lambda *_: (0,) * nd, pipeline_mode=pl.Buffered(1))


def _acc_spec(shape):
    nd = len(shape)
    return pl.BlockSpec(shape, lambda *_: (0,) * nd)


def _params(n_axes):
    return pltpu.CompilerParams(dimension_semantics=("arbitrary",) * n_axes, vmem_limit_bytes=VMEM_LIMIT)


def _place():
    x, y, c = lax.axis_index("x"), lax.axis_index("y"), lax.axis_index("c")
    return x, y, c, 4 * x + 2 * y + c


def _peer(x, y, c, mask):
    px = 1 - x if mask & 4 else x
    py = 1 - y if mask & 2 else y
    pc = 1 - c if mask & 1 else c
    return (px, py, pc), 4 * px + 2 * py + pc


ANY_SPEC = pl.BlockSpec(memory_space=pl.ANY)


def _comm_sems(n):
    return [pltpu.SemaphoreType.DMA((n, N_DEV - 1)), pltpu.SemaphoreType.DMA((n, N_DEV - 1)), pltpu.SemaphoreType.DMA((n,))]


def _gather_copies(ins, outs, sems):
    send_sems, recv_sems, local_sems = sems
    x, y, c, me = _place()
    starts, waits = [], []
    for w in range(len(ins)):
        mine = pltpu.make_async_copy(ins[w], outs[w].at[me], local_sems.at[w])
        starts.append(mine)
        waits.append(mine)
        for mask in range(1, N_DEV):
            peer, peer_idx = _peer(x, y, c, mask)
            starts.append(pltpu.make_async_remote_copy(
                src_ref=ins[w], dst_ref=outs[w].at[me], send_sem=send_sems.at[w, mask - 1],
                recv_sem=recv_sems.at[w, mask - 1], device_id=peer, device_id_type=MESH))
            waits.append(pltpu.make_async_remote_copy(
                src_ref=ins[w], dst_ref=outs[w].at[peer_idx], send_sem=send_sems.at[w, mask - 1],
                recv_sem=recv_sems.at[w, mask - 1], device_id=peer, device_id_type=MESH))
    return starts, waits


def _scatter_copies(ins, outs, sems):
    send_sems, recv_sems, local_sems = sems
    x, y, c, me = _place()
    copies = []
    for w in range(len(ins)):
        copies.append(pltpu.make_async_copy(ins[w].at[me], outs[w].at[0], local_sems.at[w]))
        for mask in range(1, N_DEV):
            peer, peer_idx = _peer(x, y, c, mask)
            copies.append(pltpu.make_async_remote_copy(
                src_ref=ins[w].at[peer_idx], dst_ref=outs[w].at[mask], send_sem=send_sems.at[w, mask - 1],
                recv_sem=recv_sems.at[w, mask - 1], device_id=peer, device_id_type=MESH))
    return copies, copies


def _stacked(arrays):
    return [jax.ShapeDtypeStruct((N_DEV,) + a.shape, a.dtype) for a in arrays]


def _all_gather(shards):
    n = len(shards)

    def body(*refs):
        ins, outs = refs[:n], refs[n:2 * n]
        send_sems, recv_sems, local_sems = refs[2 * n:]
        x, y, c, me = _place()
        sibling = (x, y, 1 - c)
        chips = [(1 - x, y), (x, 1 - y), (1 - x, 1 - y)]

        def copy(w, k, block, to, src=None):
            idx = 4 * block[0] + 2 * block[1] + block[2]
            return pltpu.make_async_remote_copy(
                src_ref=outs[w].at[idx] if src is None else src, dst_ref=outs[w].at[idx],
                send_sem=send_sems.at[w, k], recv_sem=recv_sems.at[w, k], device_id=to, device_id_type=MESH)

        local, sent = [], []
        for w in range(n):
            mine = pltpu.make_async_copy(ins[w], outs[w].at[me], local_sems.at[w])
            mine.start()
            local.append(mine)
            first = [copy(w, 0, (x, y, c), sibling, src=ins[w])]
            first += [copy(w, 1 + j, (x, y, c), (*chip, c), src=ins[w]) for j, chip in enumerate(chips)]
            for cp in first:
                cp.start()
            sent += first
        for w in range(n):
            for j, chip in enumerate(chips):
                copy(w, 1 + j, (*chip, c), (x, y, c)).wait_recv()
                onward = copy(w, 4 + j, (*chip, c), sibling)
                onward.start()
                sent.append(onward)
        for w in range(n):
            copy(w, 0, sibling, (x, y, c)).wait_recv()
            for j, chip in enumerate(chips):
                copy(w, 4 + j, (*chip, 1 - c), (x, y, c)).wait_recv()
        for cp in sent:
            cp.wait_send()
        for cp in local:
            cp.wait()

    return pl.pallas_call(
        body, name="weight_all_gather", out_shape=_stacked(shards),
        in_specs=[ANY_SPEC] * n, out_specs=[ANY_SPEC] * n, scratch_shapes=_comm_sems(n),
    )(*shards)


def _conv_fwd(x, a_norm, w_in, conv_w, w_out, n_seq, seq, later_shards):
    tm = ROW_TILE
    nt = seq // tm
    n_col = w_in.shape[2]
    n_later = len(later_shards)

    def body(x_ref, an_ref, win_ref, cw_ref, wout_ref, *rest):
        shard_refs, rest = rest[:n_later], rest[n_later:]
        x1_ref, proj_ref, conv_ref = rest[:3]
        stack_refs, rest = rest[3:3 + n_later], rest[3 + n_later:]
        prev_ref, sems = rest[0], rest[1:]
        step = pl.program_id(0) * nt + pl.program_id(1)

        @pl.when(step == 0)
        def _():
            for cp in _gather_copies(shard_refs, stack_refs, sems)[0]:
                cp.start()

        @pl.when(pl.program_id(1) == 0)
        def _():
            prev_ref[...] = jnp.zeros_like(prev_ref)

        xv = x_ref[...]
        h = (xv * _rstd(xv) * an_ref[...]).astype(BF16)
        for d in range(N_DEV):
            proj_ref[:, d * n_col:(d + 1) * n_col] = jnp.dot(h, win_ref[d], preferred_element_type=F32)
        b = proj_ref[:, 0:D_MODEL]
        v = proj_ref[:, D_MODEL:2 * D_MODEL] * proj_ref[:, 2 * D_MODEL:3 * D_MODEL]
        g = proj_ref[:, 3 * D_MODEL:4 * D_MODEL]
        w0, w1, w2 = cw_ref[0:1, :], cw_ref[1:2, :], cw_ref[2:3, :]
        conv_ref[...] = w0 * pltpu.roll(v, 2, 0) + w1 * pltpu.roll(v, 1, 0) + w2 * v
        rows = lax.broadcasted_iota(jnp.int32, (8, D_MODEL), 0)
        p8, v8 = prev_ref[...], v[0:8]
        back1 = jnp.where(rows < 1, pltpu.roll(p8, 1, 0), pltpu.roll(v8, 1, 0))
        back2 = jnp.where(rows < 2, pltpu.roll(p8, 2, 0), pltpu.roll(v8, 2, 0))
        conv_ref[0:8, :] = w0 * back2 + w1 * back1 + w2 * v8
        prev_ref[...] = v[tm - 8:tm]
        silu, _ = _silu_parts(g)
        yv = silu * b * conv_ref[...]
        x1_ref[...] = xv + _dot(yv, wout_ref[...])

        @pl.when(step == n_seq * nt - 1)
        def _():
            for cp in _gather_copies(shard_refs, stack_refs, sems)[1]:
                cp.wait()

    tok = lambda width: pl.BlockSpec((tm, width), lambda s, i: (s * nt + i, 0))
    t = n_seq * seq
    outs = pl.pallas_call(
        body, name="conv_mixer_fwd", grid=(n_seq, nt),
        in_specs=[tok(D_MODEL), _const_spec((1, D_MODEL)), _const_spec(w_in.shape), _const_spec((3, D_MODEL)),
                  _const_spec(w_out.shape)] + [ANY_SPEC] * n_later,
        out_specs=[tok(D_MODEL), tok(4 * D_MODEL), tok(D_MODEL)] + [ANY_SPEC] * n_later,
        out_shape=[jax.ShapeDtypeStruct((t, D_MODEL), F32), jax.ShapeDtypeStruct((t, 4 * D_MODEL), F32),
                   jax.ShapeDtypeStruct((t, D_MODEL), F32)] + _stacked(later_shards),
        scratch_shapes=[pltpu.VMEM((8, D_MODEL), F32)] + _comm_sems(n_later),
        compiler_params=_params(2),
    )(x, a_norm, w_in, conv_w, w_out, *later_shards)
    return outs[0], outs[1], outs[2], outs[3:]


def _conv_bwd(dx1, x, proj, conv, a_norm, w_in, conv_w, w_out, n_seq, seq, ready_stacks):
    tm = ROW_TILE
    nt = seq // tm
    n_col = w_in.shape[2]
    n_ready = len(ready_stacks)

    def body(dx1_ref, x_ref, proj_ref, conv_ref, an_ref, win_ref, cw_ref, wout_ref, *rest):
        ready_refs, rest = rest[:n_ready], rest[n_ready:]
        dx_ref, h_ref, dproj_ref, dwout_ref, dan_ref, dcw_ref = rest[:6]
        part_refs, rest = rest[6:6 + n_ready], rest[6 + n_ready:]
        next_ref, d1_ref, d2_ref = rest[:3]
        sems = rest[3:]
        step = pl.program_id(0) * nt + pl.program_id(1)
        first = step == 0

        @pl.when(first)
        def _():
            for cp in _scatter_copies(ready_refs, part_refs, sems)[0]:
                cp.start()
            dwout_ref[...] = jnp.zeros_like(dwout_ref)
            dan_ref[...] = jnp.zeros_like(dan_ref)
            dcw_ref[...] = jnp.zeros_like(dcw_ref)

        @pl.when(pl.program_id(1) == 0)
        def _():
            next_ref[...] = jnp.zeros_like(next_ref)

        dx1v = dx1_ref[...]
        dy = _dot_nt(dx1v, wout_ref[...])
        b = proj_ref[:, 0:D_MODEL]
        cc = proj_ref[:, D_MODEL:2 * D_MODEL]
        u = proj_ref[:, 2 * D_MODEL:3 * D_MODEL]
        g = proj_ref[:, 3 * D_MODEL:4 * D_MODEL]
        cv = conv_ref[...]
        silu, dsilu = _silu_parts(g)
        per_part = D_MODEL // n_col

        def back_through_w_in(part, grad):
            grad = grad.astype(BF16)
            dproj_ref[:, part * D_MODEL:(part + 1) * D_MODEL] = grad
            term = None
            for j in range(per_part):
                d = part * per_part + j
                piece = lax.dot_general(grad[:, j * n_col:(j + 1) * n_col], win_ref[d], NT, preferred_element_type=F32)
                term = piece if term is None else term + piece
            return term

        dh = back_through_w_in(0, dy * silu * cv)
        dh += back_through_w_in(3, dy * b * cv * dsilu)
        dwout_ref[...] += _dot_tn(silu * b * cv, dx1v)
        dconv = dy * silu * b
        d1_ref[...] = pltpu.roll(dconv, tm - 1, 0)
        d2_ref[...] = pltpu.roll(dconv, tm - 2, 0)
        rows = lax.broadcasted_iota(jnp.int32, (8, D_MODEL), 0)
        n8, c8 = next_ref[...], dconv[tm - 8:tm]
        d1_ref[tm - 8:tm, :] = jnp.where(rows >= 7, pltpu.roll(n8, 7, 0), pltpu.roll(c8, 7, 0))
        d2_ref[tm - 8:tm, :] = jnp.where(rows >= 6, pltpu.roll(n8, 6, 0), pltpu.roll(c8, 6, 0))
        next_ref[...] = dconv[0:8]
        d1, d2 = d1_ref[...], d2_ref[...]
        v = cc * u
        dcw_ref[0:1, :] += jnp.sum(d2 * v, axis=0, keepdims=True)
        dcw_ref[1:2, :] += jnp.sum(d1 * v, axis=0, keepdims=True)
        dcw_ref[2:3, :] += jnp.sum(dconv * v, axis=0, keepdims=True)
        dv = cw_ref[0:1, :] * d2 + cw_ref[1:2, :] * d1 + cw_ref[2:3, :] * dconv
        dh += back_through_w_in(1, dv * u)
        dh += back_through_w_in(2, dv * cc)
        xv = x_ref[...]
        r = _rstd(xv)
        xh = xv * r
        h_ref[...] = (xh * an_ref[...]).T.astype(BF16)
        dan_ref[...] += jnp.sum(dh * xh, axis=0, keepdims=True)
        dx_ref[...] = dx1v + _norm_bwd(dh * an_ref[...], xh, r)

        @pl.when(step == n_seq * nt - 1)
        def _():
            for cp in _scatter_copies(ready_refs, part_refs, sems)[1]:
                cp.wait()

    tok = lambda width: pl.BlockSpec((tm, width), lambda s, i: (s * nt + nt - 1 - i, 0))
    t = n_seq * seq
    outs = pl.pallas_call(
        body, name="conv_mixer_bwd", grid=(n_seq, nt),
        in_specs=[tok(D_MODEL), tok(D_MODEL), tok(4 * D_MODEL), tok(D_MODEL), _const_spec((1, D_MODEL)),
                  _const_spec(w_in.shape), _const_spec((3, D_MODEL)), _const_spec(w_out.shape)] + [ANY_SPEC] * n_ready,
        out_specs=[tok(D_MODEL), pl.BlockSpec((D_MODEL, tm), lambda s, i: (0, s * nt + nt - 1 - i)),
                   tok(4 * D_MODEL), _acc_spec((D_MODEL, D_MODEL)),
                   _acc_spec((1, D_MODEL)), _acc_spec((8, D_MODEL))] + [ANY_SPEC] * n_ready,
        out_shape=[jax.ShapeDtypeStruct((t, D_MODEL), F32), jax.ShapeDtypeStruct((D_MODEL, t), BF16),
                   jax.ShapeDtypeStruct((t, 4 * D_MODEL), BF16), jax.ShapeDtypeStruct((D_MODEL, D_MODEL), F32),
                   jax.ShapeDtypeStruct((1, D_MODEL), F32), jax.ShapeDtypeStruct((8, D_MODEL), F32)]
        + [jax.ShapeDtypeStruct(a.shape, a.dtype) for a in ready_stacks],
        scratch_shapes=[pltpu.VMEM((8, D_MODEL), F32), pltpu.VMEM((tm, D_MODEL), F32), pltpu.VMEM((tm, D_MODEL), F32)]
        + _comm_sems(n_ready),
        compiler_params=_params(2),
    )(dx1, x, proj, conv, a_norm, w_in, conv_w, w_out, *ready_stacks)
    return outs[:6], outs[6:]


def _w_in_grad_exchange(a_t, b, n_col, chip, ready_stacks, small):
    r, t = a_t.shape
    bt = 1024
    nk = t // bt
    n_ready = len(ready_stacks)
    n_chip = N_DEV // 2
    n_remote = N_DEV - 2

    def body(chip_ref, a_ref, b_ref, *rest):
        ready_refs, small_ref, rest = rest[:n_ready], rest[n_ready], rest[n_ready + 1:]
        own_ref, got_ref, rest = rest[0], rest[1], rest[2:]
        part_refs, small_all_ref, rest = rest[:n_ready], rest[n_ready], rest[n_ready + 1:]
        acc_ref, land_ref, stage_ref, pair_send, pair_recv, chip_send, chip_recv = rest[:7]
        scatter_sems, gather_sems = rest[7:10], rest[10:13]
        s, k = pl.program_id(0), pl.program_id(1)
        x, y, c, _ = _place()
        chip = 2 * x + y

        @pl.when(jnp.logical_and(s == 0, k == 0))
        def _():
            for cp in _scatter_copies(ready_refs, part_refs, scatter_sems)[0]:
                cp.start()
            for cp in _gather_copies([small_ref], [small_all_ref], gather_sems)[0]:
                cp.start()

        for parity in range(2):
            @pl.when(s % 2 == parity)
            def _(parity=parity):
                @pl.when(k == 0)
                def _():
                    acc_ref[parity] = jnp.zeros((r, n_col), F32)

                acc_ref[parity] += jnp.dot(a_ref[...], b_ref[...], preferred_element_type=F32)

        def to_sibling(step):
            return pltpu.make_async_remote_copy(
                src_ref=acc_ref.at[step % 2], dst_ref=land_ref, send_sem=pair_send.at[step], recv_sem=pair_recv.at[step],
                device_id=(x, y, 1 - c), device_id_type=MESH)

        def to_owner(nth):
            owner_chip = (chip + 1 + nth) % n_chip
            slot = jnp.bitwise_xor(chip, owner_chip) - 1
            return pltpu.make_async_remote_copy(
                src_ref=stage_ref.at[nth % 2], dst_ref=got_ref.at[slot], send_sem=chip_send.at[nth],
                recv_sem=chip_recv.at[slot], device_id=(owner_chip // 2, owner_chip % 2, c), device_id_type=MESH)

        for step in range(N_DEV):
            owner_core = step % 2

            @pl.when(jnp.logical_and(s == step, k == nk - 1))
            def _(step=step, owner_core=owner_core):
                @pl.when(c != owner_core)
                def _():
                    to_sibling(step).start()

                if step >= 1:
                    @pl.when(c == owner_core)
                    def _():
                        to_sibling(step - 1).wait_send()

            if step < N_DEV - 1:
                pickup = jnp.logical_and(s == step + 1, k == min(PICKUP_BLOCK, nk - 1))
            else:
                pickup = jnp.logical_and(s == step, k == nk - 1)

            @pl.when(jnp.logical_and(pickup, c == owner_core))
            def _(step=step):
                to_sibling(step).wait_recv()
                total = acc_ref[step % 2] + land_ref[...]
                if step < n_remote:
                    nth = step // 2
                    if nth >= 2:
                        to_owner(nth - 2).wait_send()
                    stage_ref[nth % 2] = total.astype(BF16)
                    to_owner(nth).start()
                else:
                    own_ref[...] = total

        @pl.when(jnp.logical_and(s == N_DEV - 1, k == nk - 1))
        def _():
            @pl.when(c != (N_DEV - 1) % 2)
            def _():
                to_sibling(N_DEV - 1).wait_send()

            to_owner(1).wait_send()
            to_owner(2).wait_send()
            for slot in range(n_chip - 1):
                pltpu.make_async_remote_copy(
                    src_ref=stage_ref.at[0], dst_ref=got_ref.at[slot], send_sem=chip_send.at[0],
                    recv_sem=chip_recv.at[slot], device_id=(x, y, c), device_id_type=MESH).wait_recv()
            for cp in _scatter_copies(ready_refs, part_refs, scatter_sems)[1]:
                cp.wait()
            for cp in _gather_copies([small_ref], [small_all_ref], gather_sems)[1]:
                cp.wait()

    def owner_block(s, k, chip):
        return (k, 2 * ((chip[0] + 1 + s // 2) % n_chip) + s % 2)

    grid_spec = pltpu.PrefetchScalarGridSpec(
        num_scalar_prefetch=1, grid=(N_DEV, nk),
        in_specs=[pl.BlockSpec((r, bt), lambda s, k, chip: (0, k)), pl.BlockSpec((bt, n_col), owner_block)]
        + [ANY_SPEC] * (n_ready + 1),
        out_specs=[pl.BlockSpec((r, n_col), lambda s, k, chip: (0, 0)), ANY_SPEC] + [ANY_SPEC] * (n_ready + 1),
        scratch_shapes=[pltpu.VMEM((2, r, n_col), F32), pltpu.VMEM((r, n_col), F32), pltpu.VMEM((2, r, n_col), BF16),
                        pltpu.SemaphoreType.DMA((N_DEV,)), pltpu.SemaphoreType.DMA((N_DEV,)),
                        pltpu.SemaphoreType.DMA((n_chip - 1,)), pltpu.SemaphoreType.DMA((n_chip - 1,))]
        + _comm_sems(n_ready) + _comm_sems(1))
    outs = pl.pallas_call(
        body, name="w_in_grad_exchange", grid_spec=grid_spec,
        out_shape=[jax.ShapeDtypeStruct((r, n_col), F32), jax.ShapeDtypeStruct((n_chip - 1, r, n_col), BF16)]
        + [jax.ShapeDtypeStruct(p.shape, p.dtype) for p in ready_stacks] + _stacked([small]),
        compiler_params=_params(2),
    )(chip, a_t, b, *ready_stacks, small)
    return outs[0], outs[1], outs[2:2 + n_ready], outs[2 + n_ready]


def _proj_fwd(x1, pos, rc, kv_norm, w_dkv, ckv_norm, w_ukv, b_norm, b_w_in, q_norm, w_uq):
    t = x1.shape[0]
    tm = LATENT_ROW_TILE

    def body(x1_ref, pos_ref, rc_ref, kvn_ref, wdkv_ref, ckvn_ref, wukv_ref, bn_ref, bwin_ref, qn_ref, wuq_ref,
             ckr_ref, cq_ref, gb_ref, q_ref, k_ref, v_ref, cos_ref, sin_ref):
        cosv, sinv = _rope_angles(pos_ref[...], rc_ref[...])
        cos_ref[...] = cosv
        sin_ref[...] = sinv
        ct, s1, s2 = _rope_tables(cosv, sinv, rc_ref[...])
        xv = x1_ref[...]
        xh = xv * _rstd(xv)
        ckr = _dot(xh * kvn_ref[...], wdkv_ref[...])
        ckr_ref[...] = ckr
        ckv = ckr[:, 0:KV_RANK]
        kv = _dot(ckv * _rstd(ckv) * ckvn_ref[...], wukv_ref[...])
        k_rope = _rope(ckr[:, KV_RANK:CKR_PAD], ct, s1, s2)
        lane = lax.broadcasted_iota(jnp.int32, (tm, HEAD_PAD), 1)
        low = lane < QK_NOPE
        for h in range(N_HEADS):
            kv_h = kv[:, h * HEAD_PAD:(h + 1) * HEAD_PAD]
            k_ref[:, h * HEAD_PAD:(h + 1) * HEAD_PAD] = jnp.where(low, kv_h, k_rope).astype(BF16)
            v_ref[:, h * HEAD_PAD:(h + 1) * HEAD_PAD] = jnp.where(low, pltpu.roll(kv_h, V_HEAD, 1), 1.0).astype(BF16)
        pb = _dot(xh * bn_ref[...], bwin_ref[...])
        cq = pb[:, 0:Q_RANK]
        cq_ref[...] = cq
        gb_ref[...] = pb[:, Q_RANK:Q_RANK + E_B]
        q = _dot(cq * _rstd(cq) * qn_ref[...], wuq_ref[...])
        for h in range(N_HEADS):
            q_ref[:, h * HEAD_PAD:(h + 1) * HEAD_PAD] = _rope(
                q[:, h * HEAD_PAD:(h + 1) * HEAD_PAD], ct, s1, s2).astype(BF16)

    tok = lambda width: pl.BlockSpec((tm, width), lambda i: (i, 0))
    weights = [kv_norm, w_dkv, ckv_norm, w_ukv, b_norm, b_w_in, q_norm, w_uq]
    wide = N_HEADS * HEAD_PAD
    return pl.pallas_call(
        body, name="latent_proj_fwd", grid=(t // tm,),
        in_specs=[tok(D_MODEL), tok(1), _const_spec(rc.shape)] + [_const_spec(w.shape) for w in weights],
        out_specs=[tok(CKR_PAD), tok(Q_RANK), tok(E_B), tok(wide), tok(wide), tok(wide), tok(HEAD_PAD), tok(HEAD_PAD)],
        out_shape=[jax.ShapeDtypeStruct((t, CKR_PAD), F32), jax.ShapeDtypeStruct((t, Q_RANK), F32),
                   jax.ShapeDtypeStruct((t, E_B), F32), jax.ShapeDtypeStruct((t, wide), BF16),
                   jax.ShapeDtypeStruct((t, wide), BF16), jax.ShapeDtypeStruct((t, wide), BF16),
                   jax.ShapeDtypeStruct((t, HEAD_PAD), F32), jax.ShapeDtypeStruct((t, HEAD_PAD), F32)],
        compiler_params=_params(1),
    )(x1, pos, rc, *weights)


def _proj_bwd(dq, dk, dv, dgb, cq, ckr, x1, dx2, cosv, sinv, rc, kv_norm, w_dkv, ckv_norm, w_ukv, b_norm, b_w_in, q_norm, w_uq):
    t = x1.shape[0]
    tm = LATENT_ROW_TILE
    wide = N_HEADS * HEAD_PAD

    def body(dq_ref, dk_ref, dv_ref, dgb_ref, cq_ref, ckr_ref, x1_ref, dx2_ref, cos_ref, sin_ref, rc_ref,
             kvn_ref, wdkv_ref, ckvn_ref, wukv_ref, bn_ref, bwin_ref, qn_ref, wuq_ref,
             dx1_ref, dwuq_ref, dwbin_ref, dwukv_ref, dwdkv_ref, dqn_ref, dbn_ref, dckvn_ref, dkvn_ref, dqu_ref, dkv_ref):
        @pl.when(pl.program_id(0) == 0)
        def _():
            for ref in (dwuq_ref, dwbin_ref, dwukv_ref, dwdkv_ref, dqn_ref, dbn_ref, dckvn_ref, dkvn_ref):
                ref[...] = jnp.zeros_like(ref)

        ct, s1, s2 = _rope_tables(cos_ref[...], sin_ref[...], rc_ref[...])
        lane = lax.broadcasted_iota(jnp.int32, (tm, HEAD_PAD), 1)
        low = lane < QK_NOPE
        for h in range(N_HEADS):
            dqu_ref[:, h * HEAD_PAD:(h + 1) * HEAD_PAD] = _rope(
                dq_ref[:, h * HEAD_PAD:(h + 1) * HEAD_PAD], ct, -s1, -s2).astype(BF16)
        cq = cq_ref[...]
        rq = _rstd(cq)
        cqh = cq * rq
        cqn = (cqh * qn_ref[...]).astype(BF16)
        dwuq = lax.dot_general(cqn, dqu_ref[...], TN, preferred_element_type=F32)
        for h in range(N_HEADS):
            dwuq_ref[h] += dwuq[:, h * HEAD_PAD:(h + 1) * HEAD_PAD]
        dcqn = lax.dot_general(dqu_ref[...], wuq_ref[...], NT, preferred_element_type=F32)
        dqn_ref[...] += jnp.sum(dcqn * cqh, axis=0, keepdims=True)
        dcq = _norm_bwd(dcqn * qn_ref[...], cqh, rq)
        dpb = jnp.concatenate([dcq, dgb_ref[...]], axis=1).astype(BF16)
        xv = x1_ref[...]
        r = _rstd(xv)
        xh = xv * r
        dwbin_ref[...] += _dot_tn(xh * bn_ref[...], dpb)
        dh3 = lax.dot_general(dpb, bwin_ref[...], NT, preferred_element_type=F32)
        dk_rope = jnp.zeros((tm, HEAD_PAD), F32)
        for h in range(N_HEADS):
            dk_h = dk_ref[:, h * HEAD_PAD:(h + 1) * HEAD_PAD]
            dv_h = dv_ref[:, h * HEAD_PAD:(h + 1) * HEAD_PAD]
            dkv_ref[:, h * HEAD_PAD:(h + 1) * HEAD_PAD] = jnp.where(low, dk_h, pltpu.roll(dv_h, V_HEAD, 1)).astype(BF16)
            dk_rope += dk_h
        rope_lanes = jnp.logical_and(lane >= ROPE_LO, lane < ROPE_LO + QK_ROPE)
        dk_rope = jnp.where(rope_lanes, _rope(dk_rope, ct, -s1, -s2), 0.0)
        ckv = ckr_ref[:, 0:KV_RANK]
        rk = _rstd(ckv)
        ckh = ckv * rk
        ckvn = (ckh * ckvn_ref[...]).astype(BF16)
        dwukv = lax.dot_general(ckvn, dkv_ref[...], TN, preferred_element_type=F32)
        for h in range(N_HEADS):
            dwukv_ref[h] += dwukv[:, h * HEAD_PAD:(h + 1) * HEAD_PAD]
        dckvn = lax.dot_general(dkv_ref[...], wukv_ref[...], NT, preferred_element_type=F32)
        dckvn_ref[...] += jnp.sum(dckvn * ckh, axis=0, keepdims=True)
        dckv = _norm_bwd(dckvn * ckvn_ref[...], ckh, rk)
        dckr = jnp.concatenate([dckv, dk_rope], axis=1).astype(BF16)
        dwdkv_ref[...] += _dot_tn(xh * kvn_ref[...], dckr)
        dh2 = lax.dot_general(dckr, wdkv_ref[...], NT, preferred_element_type=F32)
        dkvn_ref[...] += jnp.sum(dh2 * xh, axis=0, keepdims=True)
        dbn_ref[...] += jnp.sum(dh3 * xh, axis=0, keepdims=True)
        dx1_ref[...] = dx2_ref[...] + _norm_bwd(dh2 * kvn_ref[...] + dh3 * bn_ref[...], xh, r)

    tok = lambda width: pl.BlockSpec((tm, width), lambda i: (i, 0))
    weights = [kv_norm, w_dkv, ckv_norm, w_ukv, b_norm, b_w_in, q_norm, w_uq]
    acc_shapes = [(N_HEADS, Q_RANK, HEAD_PAD), (D_MODEL, Q_RANK + E_B), (N_HEADS, KV_RANK, HEAD_PAD), (D_MODEL, CKR_PAD),
                  (1, Q_RANK), (1, D_MODEL), (1, KV_RANK), (1, D_MODEL)]
    return pl.pallas_call(
        body, name="latent_proj_bwd", grid=(t // tm,),
        in_specs=[tok(wide), tok(wide), tok(wide), tok(E_B), tok(Q_RANK), tok(CKR_PAD), tok(D_MODEL), tok(D_MODEL),
                  tok(HEAD_PAD), tok(HEAD_PAD), _const_spec(rc.shape)] + [_const_spec(w.shape) for w in weights],
        out_specs=[tok(D_MODEL)] + [_acc_spec(s) for s in acc_shapes],
        out_shape=[jax.ShapeDtypeStruct((t, D_MODEL), F32)] + [jax.ShapeDtypeStruct(s, F32) for s in acc_shapes],
        scratch_shapes=[pltpu.VMEM((tm, wide), BF16), pltpu.VMEM((tm, wide), BF16)],
        compiler_params=_params(1),
    )(dq, dk, dv, dgb, cq, ckr, x1, dx2, cosv, sinv, rc, *weights)


def _attn_fwd(q, k, v, n_seq, seq):
    tb, group = ATTN_TILE, ATTN_GROUP
    rows = tb * group
    nq = seq // rows
    pair = 2 * HEAD_PAD

    def body(q_ref, k_ref, v_ref, o_ref, lr_ref):
        i = pl.program_id(2)
        row = lax.broadcasted_iota(jnp.int32, (tb, tb), 0)
        col = lax.broadcasted_iota(jnp.int32, (tb, tb), 1)
        causal = col <= row

        def step(j, carry, kinds):
            start = pl.multiple_of(j * tb, tb)
            out = []
            for g in range(group):
                for hh in range(2):
                    m, acc = carry[2 * g + hh]
                    if kinds[g] != "none":
                        heads = slice(hh * HEAD_PAD, (hh + 1) * HEAD_PAD)
                        s = lax.dot_general(q_ref[g * tb:(g + 1) * tb, heads], k_ref[pl.ds(start, tb), heads], NT,
                                            preferred_element_type=F32) * (SOFTMAX_SCALE * LOG2_E)
                        if kinds[g] == "diagonal":
                            s = jnp.where(causal, s, -jnp.inf)
                        m_new = jnp.maximum(m, jnp.max(s, axis=-1, keepdims=True))
                        p = jnp.exp2(s - m_new)
                        acc = jnp.exp2(m - m_new) * acc + jnp.dot(
                            p.astype(BF16), v_ref[pl.ds(start, tb), heads], preferred_element_type=F32)
                        m = m_new
                    out.append((m, acc))
            return tuple(out)

        one = (jnp.full((tb, 1), -jnp.inf, F32), jnp.zeros((tb, HEAD_PAD), F32))
        carry = lax.fori_loop(0, i * group, functools.partial(step, kinds=("full",) * group), (one,) * (2 * group))
        for d in range(group):
            kinds = tuple("none" if g < d else ("diagonal" if g == d else "full") for g in range(group))
            carry = step(i * group + d, carry, kinds)
        lane = lax.broadcasted_iota(jnp.int32, (tb, HEAD_PAD), 1)
        low = lane < V_HEAD
        for g in range(group):
            halves = []
            for hh in range(2):
                m, acc = carry[2 * g + hh]
                swapped = pltpu.roll(acc, V_HEAD, 1)
                halves.append(acc / swapped)
                lse = (m * LN_2) + jnp.log(jnp.where(low, swapped, acc))
                lr_ref[hh, :, g * tb:(g + 1) * tb] = lse.T[0:8, :]
            o_ref[g * tb:(g + 1) * tb, :] = jnp.where(low, halves[0], pltpu.roll(halves[1], V_HEAD, 1))

    t = n_seq * seq
    return pl.pallas_call(
        body, name="attention_fwd", grid=(n_seq, N_HEADS // 2, nq),
        in_specs=[pl.BlockSpec((rows, pair), lambda s, p, i: (s * nq + i, p)),
                  pl.BlockSpec((seq, pair), lambda s, p, i: (s, p)),
                  pl.BlockSpec((seq, pair), lambda s, p, i: (s, p))],
        out_specs=[pl.BlockSpec((rows, HEAD_PAD), lambda s, p, i: (s * nq + i, p)),
                   pl.BlockSpec((None, 2, 8, rows), lambda s, p, i: (s, p, 0, i))],
        out_shape=[jax.ShapeDtypeStruct((t, E_B), F32), jax.ShapeDtypeStruct((n_seq, N_HEADS, 8, seq), F32)],
        compiler_params=_params(3),
    )(q, k, v)


def _attn_bwd(q, k, v, do, lse_row, delta_row, n_seq, seq):
    tb, group = ATTN_TILE, ATTN_BWD_GROUP
    rows = tb * group
    nk = seq // rows
    n_inner = seq // tb
    pair = 2 * HEAD_PAD

    def body(q_ref, k_ref, v_ref, do_ref, lr_ref, dr_ref, dk_ref, dv_ref, dq_ref):
        j = pl.program_id(2)

        @pl.when(j == 0)
        def _():
            dq_ref[...] = jnp.zeros_like(dq_ref)

        row = lax.broadcasted_iota(jnp.int32, (tb, tb), 0)
        col = lax.broadcasted_iota(jnp.int32, (tb, tb), 1)
        causal = col >= row

        def step(i, carry, kinds):
            start = pl.multiple_of(i * tb, tb)
            out = []
            for hh in range(2):
                heads = slice(hh * HEAD_PAD, (hh + 1) * HEAD_PAD)
                qi = q_ref[pl.ds(start, tb), heads]
                doi = do_ref[pl.ds(start, tb), heads]
                dq = None
                for g in range(group):
                    dk_acc, dv_acc = carry[hh * group + g]
                    if kinds[g] != "none":
                        kg = k_ref[g * tb:(g + 1) * tb, heads]
                        st = lax.dot_general(kg, qi, NT, preferred_element_type=F32) * SOFTMAX_SCALE
                        pt = jnp.exp(st - lr_ref[hh, 0:1, pl.ds(start, tb)])
                        if kinds[g] == "diagonal":
                            pt = jnp.where(causal, pt, 0.0)
                        dv_acc = dv_acc + jnp.dot(pt.astype(BF16), doi, preferred_element_type=F32)
                        dpt = lax.dot_general(v_ref[g * tb:(g + 1) * tb, heads], doi, NT, preferred_element_type=F32)
                        dst = (pt * (dpt - dr_ref[hh, 0:1, pl.ds(start, tb)]) * SOFTMAX_SCALE).astype(BF16)
                        dk_acc = dk_acc + jnp.dot(dst, qi, preferred_element_type=F32)
                        term = lax.dot_general(dst, kg, TN, preferred_element_type=F32)
                        dq = term if dq is None else dq + term
                    out.append((dk_acc, dv_acc))
                dq_ref[pl.ds(start, tb), heads] += dq
            return tuple(out)

        one = (jnp.zeros((tb, HEAD_PAD), F32), jnp.zeros((tb, HEAD_PAD), F32))
        carry = (one,) * (2 * group)
        for d in range(group):
            kinds = tuple("full" if g < d else ("diagonal" if g == d else "none") for g in range(group))
            carry = step(j * group + d, carry, kinds)
        carry = lax.fori_loop((j + 1) * group, n_inner, functools.partial(step, kinds=("full",) * group), carry)
        for hh in range(2):
            for g in range(group):
                dk_ref[g * tb:(g + 1) * tb, hh * HEAD_PAD:(hh + 1) * HEAD_PAD] = carry[hh * group + g][0]
                dv_ref[g * tb:(g + 1) * tb, hh * HEAD_PAD:(hh + 1) * HEAD_PAD] = carry[hh * group + g][1]

    t = n_seq * seq
    wide = N_HEADS * HEAD_PAD
    return pl.pallas_call(
        body, name="attention_bwd", grid=(n_seq, N_HEADS // 2, nk),
        in_specs=[pl.BlockSpec((seq, pair), lambda s, p, j: (s, p)),
                  pl.BlockSpec((rows, pair), lambda s, p, j: (s * nk + j, p)),
                  pl.BlockSpec((rows, pair), lambda s, p, j: (s * nk + j, p)),
                  pl.BlockSpec((seq, pair), lambda s, p, j: (s, p)),
                  pl.BlockSpec((None, 2, 8, seq), lambda s, p, j: (s, p, 0, 0)),
                  pl.BlockSpec((None, 2, 8, seq), lambda s, p, j: (s, p, 0, 0))],
        out_specs=[pl.BlockSpec((rows, pair), lambda s, p, j: (s * nk + j, p)),
                   pl.BlockSpec((rows, pair), lambda s, p, j: (s * nk + j, p)),
                   pl.BlockSpec((seq, pair), lambda s, p, j: (s, p))],
        out_shape=[jax.ShapeDtypeStruct((t, wide), F32), jax.ShapeDtypeStruct((t, wide), F32),
                   jax.ShapeDtypeStruct((t, wide), F32)],
        compiler_params=_params(3),
    )(q, k, v, do, lse_row, delta_row)


def _head_tail(o, gb, x1, w_out, final_norm, target, n_seq, seq):
    tm = LATENT_ROW_TILE
    nt = seq // tm
    n_col = D_MODEL // N_DEV

    def body(o_ref, gb_ref, x1_ref, wout_ref, fn_ref, tgt_ref,
             dx2_ref, do_ref, dgb_ref, dr_ref, dwout_ref, dfn_ref, loss_ref):
        first = jnp.logical_and(pl.program_id(0) == 0, pl.program_id(1) == 0)

        @pl.when(first)
        def _():
            dwout_ref[...] = jnp.zeros_like(dwout_ref)
            dfn_ref[...] = jnp.zeros_like(dfn_ref)
            loss_ref[...] = jnp.zeros_like(loss_ref)

        ov, g = o_ref[...], gb_ref[...]
        silu, dsilu = _silu_parts(g)
        gated = (ov * silu).astype(BF16)
        x2 = x1_ref[...] + jnp.dot(gated, wout_ref[...], preferred_element_type=F32)
        r = _rstd(x2)
        xh = x2 * r
        err = xh * fn_ref[...] - tgt_ref[...]
        loss_ref[...] += 0.5 * jnp.sum(jnp.mean(err * err, axis=-1, keepdims=True), axis=0, keepdims=True)
        dy = err / D_MODEL
        dfn_ref[...] += jnp.sum(dy * xh, axis=0, keepdims=True)
        dx2 = _norm_bwd(dy * fn_ref[...], xh, r)
        dx2_ref[...] = dx2
        dx2b = dx2.astype(BF16)
        dw = lax.dot_general(gated, dx2b, TN, preferred_element_type=F32)
        for d in range(N_DEV):
            dwout_ref[d] += dw[:, d * n_col:(d + 1) * n_col]
        dgated = lax.dot_general(dx2b, wout_ref[...], NT, preferred_element_type=F32)
        do = dgated * silu
        dgb_ref[...] = dgated * ov * dsilu
        prod = do * ov
        lane = lax.broadcasted_iota(jnp.int32, (tm, HEAD_PAD), 1)
        low = lane < V_HEAD
        for p in range(N_HEADS // 2):
            blk = prod[:, p * HEAD_PAD:(p + 1) * HEAD_PAD]
            do_pair = do[:, p * HEAD_PAD:(p + 1) * HEAD_PAD]
            for hh in range(2):
                h = 2 * p + hh
                mine = do_pair if hh == 0 else pltpu.roll(do_pair, V_HEAD, 1)
                do_ref[:, h * HEAD_PAD:(h + 1) * HEAD_PAD] = jnp.where(low, mine, 0.0).astype(BF16)
                delta = jnp.sum(jnp.where(low if hh == 0 else ~low, blk, 0.0), axis=-1, keepdims=True)
                dr_ref[h] = jnp.broadcast_to(delta, (tm, HEAD_PAD)).T[0:8, :]

    tok = lambda width: pl.BlockSpec((tm, width), lambda s, i: (s * nt + i, 0))
    t = n_seq * seq
    return pl.pallas_call(
        body, name="head_tail", grid=(n_seq, nt),
        in_specs=[tok(E_B), tok(E_B), tok(D_MODEL), _const_spec(w_out.shape), _const_spec((1, D_MODEL)), tok(D_MODEL)],
        out_specs=[tok(D_MODEL), tok(N_HEADS * HEAD_PAD), tok(E_B),
                   pl.BlockSpec((None, N_HEADS, 8, tm), lambda s, i: (s, 0, 0, i)),
                   _acc_spec((N_DEV, E_B, n_col)), _acc_spec((1, D_MODEL)), _acc_spec((1, 1))],
        out_shape=[jax.ShapeDtypeStruct((t, D_MODEL), F32), jax.ShapeDtypeStruct((t, N_HEADS * HEAD_PAD), BF16),
                   jax.ShapeDtypeStruct((t, E_B), F32),
                   jax.ShapeDtypeStruct((n_seq, N_HEADS, 8, seq), F32), jax.ShapeDtypeStruct((N_DEV, E_B, n_col), F32),
                   jax.ShapeDtypeStruct((1, D_MODEL), F32), jax.ShapeDtypeStruct((1, 1), F32)],
        compiler_params=_params(2),
    )(o, gb, x1, w_out, final_norm, target)


def _adamw_math(w, g, m, v):
    m = ADAM_B1 * m + (1.0 - ADAM_B1) * g
    v = ADAM_B2 * v + (1.0 - ADAM_B2) * jnp.square(g)
    m_hat = m / (1.0 - ADAM_B1 ** ADAM_STEP)
    v_hat = v / (1.0 - ADAM_B2 ** ADAM_STEP)
    delta = -ADAM_LR * (m_hat / (jnp.sqrt(v_hat) + ADAM_EPS) + ADAM_WD * w)
    return delta, m, v


def _adamw_reduce(own, parts, w, m, v):
    rows, cols = w.shape
    br = 256
    n_parts = parts.shape[0]

    def body(own_ref, p_ref, w_ref, m_ref, v_ref, g_ref, d_ref, nm_ref, nv_ref):
        g = own_ref[...]
        for k in range(n_parts):
            g = g + p_ref[k].astype(F32)
        g_ref[...] = g
        d_ref[...], nm_ref[...], nv_ref[...] = _adamw_math(w_ref[...], g, m_ref[...], v_ref[...])

    blk = pl.BlockSpec((br, cols), lambda i: (i, 0))
    return pl.pallas_call(
        body, name="adamw_reduce", grid=(rows // br,),
        in_specs=[blk, pl.BlockSpec((n_parts, br, cols), lambda i: (0, i, 0)), blk, blk, blk],
        out_specs=[blk] * 4, out_shape=[jax.ShapeDtypeStruct((rows, cols), F32)] * 4,
        compiler_params=_params(1),
    )(own, parts, w, m, v)


def _adamw_reduce_many(parts_list, states):
    n = len(parts_list)

    def body(*refs):
        ins, outs = refs[:4 * n], refs[4 * n:]
        for j in range(n):
            p_ref, w_ref, m_ref, v_ref = ins[4 * j:4 * j + 4]
            g = p_ref[0]
            for k in range(1, p_ref.shape[0]):
                g = g + p_ref[k]
            outs[4 * j][...] = g
            outs[4 * j + 1][...], outs[4 * j + 2][...], outs[4 * j + 3][...] = _adamw_math(
                w_ref[...], g, m_ref[...], v_ref[...])

    flat_in = [a for p, st in zip(parts_list, states) for a in (p,) + tuple(st)]
    shapes = [jax.ShapeDtypeStruct(st[0].shape, F32) for st in states for _ in range(4)]
    flat = pl.pallas_call(body, name="adamw_reduce_many", out_shape=shapes,
                          compiler_params=pltpu.CompilerParams(vmem_limit_bytes=VMEM_LIMIT))(*flat_in)
    return [tuple(flat[4 * j:4 * j + 4]) for j in range(n)]


def _sum_parts(parts):
    def body(p_ref, o_ref):
        g = p_ref[0]
        for k in range(1, N_DEV):
            g = g + p_ref[k]
        o_ref[...] = g

    return pl.pallas_call(body, name="small_grad_sum", out_shape=jax.ShapeDtypeStruct(parts.shape[1:], F32))(parts)


def _adamw_small(gs, ws, ms, vs):
    n = len(gs)

    def body(*refs):
        ins, outs = refs[:4 * n], refs[4 * n:]
        for j in range(n):
            g_ref, w_ref, m_ref, v_ref = ins[j], ins[n + j], ins[2 * n + j], ins[3 * n + j]
            outs[3 * j][...], outs[3 * j + 1][...], outs[3 * j + 2][...] = _adamw_math(
                w_ref[...], g_ref[...], m_ref[...], v_ref[...])

    shapes = [jax.ShapeDtypeStruct(g.shape, F32) for g in gs for _ in range(3)]
    flat = pl.pallas_call(body, name="adamw_small", out_shape=shapes)(*gs, *ws, *ms, *vs)
    return [tuple(flat[3 * j:3 * j + 3]) for j in range(n)]


SMALL_ROWS = 8
LOSS_LANE = D_MODEL - 1


def _pack_small(g_fn, g_kvn, g_bn, g_an, g_ckvn, g_qn, g_conv, loss_part):
    def body(fn_ref, kvn_ref, bn_ref, an_ref, ckvn_ref, qn_ref, conv_ref, loss_ref, o_ref):
        o_ref[0:1, :] = fn_ref[...]
        o_ref[1:2, :] = kvn_ref[...]
        o_ref[2:3, :] = bn_ref[...]
        o_ref[3:4, :] = an_ref[...]
        lane = lax.broadcasted_iota(jnp.int32, (1, D_MODEL), 1)
        o_ref[4:5, :] = jnp.where(lane == LOSS_LANE, loss_ref[...], 0.0)
        o_ref[4:5, 0:KV_RANK] = ckvn_ref[...]
        o_ref[4:5, KV_RANK:KV_RANK + Q_RANK] = qn_ref[...]
        o_ref[5:8, :] = conv_ref[0:3, :]

    return pl.pallas_call(body, name="pack_small", out_shape=jax.ShapeDtypeStruct((SMALL_ROWS, D_MODEL), F32))(
        g_fn, g_kvn, g_bn, g_an, g_ckvn, g_qn, g_conv, loss_part)


def _pad_cols(a, width):
    return jnp.pad(a, ((0, 0), (0, width - a.shape[1])))


def _dkv_to_padded(a):
    r = a.shape[0]
    z = jnp.zeros((r, ROPE_LO), a.dtype)
    z2 = jnp.zeros((r, HEAD_PAD - ROPE_LO - QK_ROPE), a.dtype)
    return jnp.concatenate([a[:, :KV_RANK], z, a[:, KV_RANK:], z2], axis=1)


def _dkv_from_padded(a):
    return jnp.concatenate([a[:, :KV_RANK], a[:, KV_RANK + ROPE_LO:KV_RANK + ROPE_LO + QK_ROPE]], axis=1)


def _unstack_cols(a):
    return jnp.transpose(a, (1, 0, 2)).reshape(a.shape[1], N_DEV * a.shape[2])


def kernel(x, positions, a_norm, a_w_in, a_conv, a_w_out, kv_norm, w_dkv, ckv_norm, w_ukv, b_norm, b_w_in, b_q_norm, b_w_uq, b_w_out, final_norm, loss_target, m_a_norm, m_a_w_in, m_a_conv, m_a_w_out, m_kv_norm, m_w_dkv, m_ckv_norm, m_w_ukv, m_b_norm, m_b_w_in, m_b_q_norm, m_b_w_uq, m_b_w_out, m_final_norm, v_a_norm, v_a_w_in, v_a_conv, v_a_w_out, v_kv_norm, v_w_dkv, v_ckv_norm, v_w_ukv, v_b_norm, v_b_w_in, v_b_q_norm, v_b_w_uq, v_b_w_out, v_final_norm):
    n_seq, seq, _ = x.shape
    t = n_seq * seq
    me = 4 * lax.axis_index("x") + 2 * lax.axis_index("y") + lax.axis_index("c")

    big = {
        "a_w_in": (a_w_in[0], m_a_w_in[0], v_a_w_in[0]),
        "a_w_out": (a_w_out[0], m_a_w_out[0], v_a_w_out[0]),
        "w_dkv": tuple(_dkv_to_padded(a) for a in (w_dkv, m_w_dkv, v_w_dkv)),
        "w_ukv": (w_ukv, m_w_ukv, v_w_ukv),
        "b_w_in": (b_w_in[0], m_b_w_in[0], v_b_w_in[0]),
        "b_w_uq": tuple(_pad_cols(a[0], HEAD_PAD) for a in (b_w_uq, m_b_w_uq, v_b_w_uq)),
        "b_w_out": (b_w_out[0], m_b_w_out[0], v_b_w_out[0]),
    }
    names = list(big)
    first_names, later_names = names[:2], names[2:]
    gathered = _all_gather([big[n][0].astype(BF16) for n in first_names] + [a_norm, a_conv[0]])
    a_norm_f = gathered[2].reshape(1, D_MODEL)
    a_conv_f = _unstack_cols(gathered[3])
    w_a_in = gathered[0]
    w_a_out = gathered[1].reshape(D_MODEL, D_MODEL)

    x2d = x.reshape(t, D_MODEL)
    tgt = loss_target.reshape(t, D_MODEL)
    pos = positions.astype(F32).reshape(t, 1)
    rc = _rope_consts()
    kvn, ckvn, bn, qn, fn = kv_norm.reshape(1, -1), ckv_norm.reshape(1, -1), b_norm, b_q_norm, final_norm.reshape(1, -1)

    x1, proj, conv, later = _conv_fwd(x2d, a_norm_f, w_a_in, a_conv_f, w_a_out, n_seq, seq,
                                      [big[n][0].astype(BF16) for n in later_names])
    full = dict(zip(later_names, later))
    w_dkv_f = full["w_dkv"].reshape(D_MODEL, CKR_PAD)
    w_ukv_f = _unstack_cols(full["w_ukv"])
    w_b_in = full["b_w_in"].reshape(D_MODEL, Q_RANK + E_B)
    w_uq_f = _unstack_cols(full["b_w_uq"])
    w_b_out = _unstack_cols(full["b_w_out"])
    layer_b = (kvn, w_dkv_f, ckvn, w_ukv_f, bn, w_b_in, qn, w_uq_f)
    ckr, cq, gb, q, k, v, cosv, sinv = _proj_fwd(x1, pos, rc, *layer_b)
    o, lse_row = _attn_fwd(q, k, v, n_seq, seq)
    dx2, do, dgb, delta_row, g_b_out, g_fn, loss_part = _head_tail(o, gb, x1, w_b_out, fn, tgt, n_seq, seq)
    dk, dv, dq = _attn_bwd(q, k, v, do, lse_row, delta_row, n_seq, seq)
    dx1, g_uq, g_b_in, g_ukv, g_dkv, g_qn, g_bn, g_ckvn, g_kvn = _proj_bwd(
        dq, dk, dv, dgb, cq, ckr, x1, dx2, cosv, sinv, rc, *layer_b)
    stacks = {
        "w_dkv": g_dkv.reshape(N_DEV, D_MODEL // N_DEV, CKR_PAD),
        "w_ukv": g_ukv,
        "b_w_in": g_b_in.reshape(N_DEV, D_MODEL // N_DEV, Q_RANK + E_B),
        "b_w_uq": g_uq,
        "b_w_out": g_b_out,
    }
    (dx, h_a, dproj, g_a_out, g_an, g_conv), later_parts = _conv_bwd(
        dx1, x2d, proj, conv, a_norm_f, w_a_in, a_conv_f, w_a_out, n_seq, seq, [stacks[n] for n in later_names])
    parts = dict(zip(later_names, later_parts))
    small = _pack_small(g_fn, g_kvn, g_bn, g_an, g_ckvn, g_qn, g_conv, loss_part)
    a_in_own, a_in_got, (parts["a_w_out"],), small_parts = _w_in_grad_exchange(
        h_a, dproj, D_MODEL * 4 // N_DEV, (me // 2).reshape(1).astype(jnp.int32),
        [g_a_out.reshape(N_DEV, D_MODEL // N_DEV, D_MODEL)], small)

    outs = {"a_w_in": _adamw_reduce(a_in_own, a_in_got, *big["a_w_in"])}
    outs.update(zip(names[1:], _adamw_reduce_many([parts[n] for n in names[1:]], [big[n] for n in names[1:]])))
    outs["w_dkv"] = tuple(_dkv_from_padded(a) for a in outs["w_dkv"])
    outs["b_w_uq"] = tuple(a[:, :QK_NOPE + QK_ROPE] for a in outs["b_w_uq"])
    for n in ("a_w_in", "a_w_out", "b_w_in", "b_w_uq", "b_w_out"):
        outs[n] = tuple(a[None] for a in outs[n])

    total = _sum_parts(small_parts)
    loss = total[4, LOSS_LANE]
    shard = D_MODEL // N_DEV
    g_small = {
        "final_norm": total[0], "kv_norm": total[1], "b_norm": total[2:3],
        "a_norm": lax.dynamic_slice_in_dim(total[3:4], me * shard, shard, axis=1),
        "ckv_norm": total[4, 0:KV_RANK], "b_q_norm": total[4:5, KV_RANK:KV_RANK + Q_RANK],
        "a_conv": lax.dynamic_slice_in_dim(total[5:8], me * shard, shard, axis=1)[None],
    }
    small_state = {
        "final_norm": (final_norm, m_final_norm, v_final_norm), "kv_norm": (kv_norm, m_kv_norm, v_kv_norm),
        "b_norm": (b_norm, m_b_norm, v_b_norm), "a_norm": (a_norm, m_a_norm, v_a_norm),
        "ckv_norm": (ckv_norm, m_ckv_norm, v_ckv_norm), "b_q_norm": (b_q_norm, m_b_q_norm, v_b_q_norm),
        "a_conv": (a_conv, m_a_conv, v_a_conv),
    }
    small_names = list(g_small)
    as2d = lambda a: a.reshape(-1, a.shape[-1])
    upd = _adamw_small([as2d(g_small[n]) for n in small_names],
                       *[[as2d(small_state[n][j]) for n in small_names] for j in range(3)])
    for n, u in zip(small_names, upd):
        outs[n] = (g_small[n],) + tuple(a.reshape(g_small[n].shape) for a in u)

    order = ["a_norm", "a_w_in", "a_conv", "a_w_out", "kv_norm", "w_dkv", "ckv_norm", "w_ukv", "b_norm", "b_w_in",
             "b_q_norm", "b_w_uq", "b_w_out", "final_norm"]
    result = [loss, dx.reshape(n_seq, seq, D_MODEL)]
    for j in range(4):
        result += [outs[n][j] for n in order]
    return tuple(result)
```

```python
import functools
import math

import numpy as np
import jax
import jax.numpy as jnp
from jax import lax
from jax.experimental import pallas as pl
from jax.experimental.pallas import tpu as pltpu

F32 = jnp.float32
BF16 = jnp.bfloat16

D_MODEL = 1024
N_HEADS = 8
QK_NOPE = 64
QK_ROPE = 32
V_HEAD = 64
KV_RANK = 256
Q_RANK = 384
E_B = N_HEADS * V_HEAD
HEAD_PAD = 128
CKR_PAD = KV_RANK + HEAD_PAD
ROPE_LO = QK_NOPE
ROPE_HALF = QK_ROPE // 2
ROPE_THETA = 10000.0
SOFTMAX_SCALE = 1.0 / math.sqrt(QK_NOPE + QK_ROPE)
LOG2_E = math.log2(math.e)
LN_2 = math.log(2.0)
EPS = 1e-6
N_DEV = 8

ADAM_LR = 0.001
ADAM_B1 = 0.9
ADAM_B2 = 0.999
ADAM_EPS = 1e-08
ADAM_WD = 0.01
ADAM_STEP = 10

ROW_TILE = 256
LATENT_ROW_TILE = 512
PICKUP_BLOCK = 3
ATTN_TILE = 512
ATTN_GROUP = 4
ATTN_BWD_GROUP = 4
VMEM_LIMIT = 56 * 1024 * 1024

MESH = pl.DeviceIdType.MESH
NT = (((1,), (1,)), ((), ()))
TN = (((0,), (0,)), ((), ()))


def _dot(a, b):
    return jnp.dot(a.astype(BF16), b.astype(BF16), preferred_element_type=F32)


def _dot_nt(a, b):
    return lax.dot_general(a.astype(BF16), b.astype(BF16), NT, preferred_element_type=F32)


def _dot_tn(a, b):
    return lax.dot_general(a.astype(BF16), b.astype(BF16), TN, preferred_element_type=F32)


def _rstd(x):
    return lax.rsqrt(jnp.mean(x * x, axis=-1, keepdims=True) + EPS)


def _norm_bwd(a, xh, r):
    return r * (a - xh * jnp.mean(a * xh, axis=-1, keepdims=True))


def _silu_parts(g):
    sg = jax.nn.sigmoid(g)
    return g * sg, sg * (1.0 + g * (1.0 - sg))


def _rope_consts():
    inv = (ROPE_THETA ** (-np.arange(0, QK_ROPE, 2, dtype=np.float32) / QK_ROPE)).astype(np.float32)
    t = np.zeros((8, HEAD_PAD), np.float32)
    t[0, ROPE_LO:ROPE_LO + ROPE_HALF] = inv
    t[0, ROPE_LO + ROPE_HALF:ROPE_LO + QK_ROPE] = inv
    t[1, ROPE_LO:ROPE_LO + ROPE_HALF] = -1.0
    t[2, ROPE_LO + ROPE_HALF:ROPE_LO + QK_ROPE] = 1.0
    return jnp.asarray(t)


def _rope_angles(pos, rc):
    ang = pos * rc[0:1, :]
    return jnp.cos(ang), jnp.sin(ang)


def _rope_tables(cosv, sinv, rc):
    return cosv, sinv * rc[1:2, :], sinv * rc[2:3, :]


def _rope(x, ct, s1, s2):
    up = pltpu.roll(x, HEAD_PAD - ROPE_HALF, 1)
    dn = pltpu.roll(x, ROPE_HALF, 1)
    return x * ct + up * s1 + dn * s2


def _const_spec(shape):
    nd = len(shape)
    return pl.BlockSpec(shape, lambda *_: (0,) * nd, pipeline_mode=pl.Buffered(1))


def _acc_spec(shape):
    nd = len(shape)
    return pl.BlockSpec(shape, lambda *_: (0,) * nd)


def _params(n_axes):
    return pltpu.CompilerParams(dimension_semantics=("arbitrary",) * n_axes, vmem_limit_bytes=VMEM_LIMIT)


def _place():
    x, y, c = lax.axis_index("x"), lax.axis_index("y"), lax.axis_index("c")
    return x, y, c, 4 * x + 2 * y + c


def _peer(x, y, c, mask):
    px = 1 - x if mask & 4 else x
    py = 1 - y if mask & 2 else y
    pc = 1 - c if mask & 1 else c
    return (px, py, pc), 4 * px + 2 * py + pc


ANY_SPEC = pl.BlockSpec(memory_space=pl.ANY)


def _comm_sems(n):
    return [pltpu.SemaphoreType.DMA((n, N_DEV - 1)), pltpu.SemaphoreType.DMA((n, N_DEV - 1)), pltpu.SemaphoreType.DMA((n,))]


def _gather_copies(ins, outs, sems):
    send_sems, recv_sems, local_sems = sems
    x, y, c, me = _place()
    starts, waits = [], []
    for w in range(len(ins)):
        mine = pltpu.make_async_copy(ins[w], outs[w].at[me], local_sems.at[w])
        starts.append(mine)
        waits.append(mine)
        for mask in range(1, N_DEV):
            peer, peer_idx = _peer(x, y, c, mask)
            starts.append(pltpu.make_async_remote_copy(
                src_ref=ins[w], dst_ref=outs[w].at[me], send_sem=send_sems.at[w, mask - 1],
                recv_sem=recv_sems.at[w, mask - 1], device_id=peer, device_id_type=MESH))
            waits.append(pltpu.make_async_remote_copy(
                src_ref=ins[w], dst_ref=outs[w].at[peer_idx], send_sem=send_sems.at[w, mask - 1],
                recv_sem=recv_sems.at[w, mask - 1], device_id=peer, device_id_type=MESH))
    return starts, waits


def _scatter_copies(ins, outs, sems):
    send_sems, recv_sems, local_sems = sems
    x, y, c, me = _place()
    copies = []
    for w in range(len(ins)):
        copies.append(pltpu.make_async_copy(ins[w].at[me], outs[w].at[0], local_sems.at[w]))
        for mask in range(1, N_DEV):
            peer, peer_idx = _peer(x, y, c, mask)
            copies.append(pltpu.make_async_remote_copy(
                src_ref=ins[w].at[peer_idx], dst_ref=outs[w].at[mask], send_sem=send_sems.at[w, mask - 1],
                recv_sem=recv_sems.at[w, mask - 1], device_id=peer, device_id_type=MESH))
    return copies, copies


def _stacked(arrays):
    return [jax.ShapeDtypeStruct((N_DEV,) + a.shape, a.dtype) for a in arrays]


def _all_gather(shards):
    n = len(shards)

    def body(*refs):
        ins, outs = refs[:n], refs[n:2 * n]
        send_sems, recv_sems, local_sems = refs[2 * n:]
        x, y, c, me = _place()
        sibling = (x, y, 1 - c)
        chips = [(1 - x, y), (x, 1 - y), (1 - x, 1 - y)]

        def copy(w, k, block, to, src=None):
            idx = 4 * block[0] + 2 * block[1] + block[2]
            return pltpu.make_async_remote_copy(
                src_ref=outs[w].at[idx] if src is None else src, dst_ref=outs[w].at[idx],
                send_sem=send_sems.at[w, k], recv_sem=recv_sems.at[w, k], device_id=to, device_id_type=MESH)

        local, sent = [], []
        for w in range(n):
            mine = pltpu.make_async_copy(ins[w], outs[w].at[me], local_sems.at[w])
            mine.start()
            local.append(mine)
            first = [copy(w, 0, (x, y, c), sibling, src=ins[w])]
            first += [copy(w, 1 + j, (x, y, c), (*chip, c), src=ins[w]) for j, chip in enumerate(chips)]
            for cp in first:
                cp.start()
            sent += first
        for w in range(n):
            for j, chip in enumerate(chips):
                copy(w, 1 + j, (*chip, c), (x, y, c)).wait_recv()
                onward = copy(w, 4 + j, (*chip, c), sibling)
                onward.start()
                sent.append(onward)
        for w in range(n):
            copy(w, 0, sibling, (x, y, c)).wait_recv()
            for j, chip in enumerate(chips):
                copy(w, 4 + j, (*chip, 1 - c), (x, y, c)).wait_recv()
        for cp in sent:
            cp.wait_send()
        for cp in local:
            cp.wait()

    return pl.pallas_call(
        body, name="weight_all_gather", out_shape=_stacked(shards),
        in_specs=[ANY_SPEC] * n, out_specs=[ANY_SPEC] * n, scratch_shapes=_comm_sems(n),
    )(*shards)


def _conv_fwd(x, a_norm, w_in, conv_w, w_out, n_seq, seq, later_shards):
    tm = ROW_TILE
    nt = seq // tm
    n_col = w_in.shape[2]
    n_later = len(later_shards)

    def body(x_ref, an_ref, win_ref, cw_ref, wout_ref, *rest):
        shard_refs, rest = rest[:n_later], rest[n_later:]
        x1_ref, proj_ref, conv_ref = rest[:3]
        stack_refs, rest = rest[3:3 + n_later], rest[3 + n_later:]
        prev_ref, sems = rest[0], rest[1:]
        step = pl.program_id(0) * nt + pl.program_id(1)

        @pl.when(step == 0)
        def _():
            for cp in _gather_copies(shard_refs, stack_refs, sems)[0]:
                cp.start()

        @pl.when(pl.program_id(1) == 0)
        def _():
            prev_ref[...] = jnp.zeros_like(prev_ref)

        xv = x_ref[...]
        h = (xv * _rstd(xv) * an_ref[...]).astype(BF16)
        for d in range(N_DEV):
            proj_ref[:, d * n_col:(d + 1) * n_col] = jnp.dot(h, win_ref[d], preferred_element_type=F32)
        b = proj_ref[:, 0:D_MODEL]
        v = proj_ref[:, D_MODEL:2 * D_MODEL] * proj_ref[:, 2 * D_MODEL:3 * D_MODEL]
        g = proj_ref[:, 3 * D_MODEL:4 * D_MODEL]
        w0, w1, w2 = cw_ref[0:1, :], cw_ref[1:2, :], cw_ref[2:3, :]
        conv_ref[...] = w0 * pltpu.roll(v, 2, 0) + w1 * pltpu.roll(v, 1, 0) + w2 * v
        rows = lax.broadcasted_iota(jnp.int32, (8, D_MODEL), 0)
        p8, v8 = prev_ref[...], v[0:8]
        back1 = jnp.where(rows < 1, pltpu.roll(p8, 1, 0), pltpu.roll(v8, 1, 0))
        back2 = jnp.where(rows < 2, pltpu.roll(p8, 2, 0), pltpu.roll(v8, 2, 0))
        conv_ref[0:8, :] = w0 * back2 + w1 * back1 + w2 * v8
        prev_ref[...] = v[tm - 8:tm]
        silu, _ = _silu_parts(g)
        yv = silu * b * conv_ref[...]
        x1_ref[...] = xv + _dot(yv, wout_ref[...])

        @pl.when(step == n_seq * nt - 1)
        def _():
            for cp in _gather_copies(shard_refs, stack_refs, sems)[1]:
                cp.wait()

    tok = lambda width: pl.BlockSpec((tm, width), lambda s, i: (s * nt + i, 0))
    t = n_seq * seq
    outs = pl.pallas_call(
        body, name="conv_mixer_fwd", grid=(n_seq, nt),
        in_specs=[tok(D_MODEL), _const_spec((1, D_MODEL)), _const_spec(w_in.shape), _const_spec((3, D_MODEL)),
                  _const_spec(w_out.shape)] + [ANY_SPEC] * n_later,
        out_specs=[tok(D_MODEL), tok(4 * D_MODEL), tok(D_MODEL)] + [ANY_SPEC] * n_later,
        out_shape=[jax.ShapeDtypeStruct((t, D_MODEL), F32), jax.ShapeDtypeStruct((t, 4 * D_MODEL), F32),
                   jax.ShapeDtypeStruct((t, D_MODEL), F32)] + _stacked(later_shards),
        scratch_shapes=[pltpu.VMEM((8, D_MODEL), F32)] + _comm_sems(n_later),
        compiler_params=_params(2),
    )(x, a_norm, w_in, conv_w, w_out, *later_shards)
    return outs[0], outs[1], outs[2], outs[3:]


def _conv_bwd(dx1, x, proj, conv, a_norm, w_in, conv_w, w_out, n_seq, seq, ready_stacks):
    tm = ROW_TILE
    nt = seq // tm
    n_col = w_in.shape[2]
    n_ready = len(ready_stacks)

    def body(dx1_ref, x_ref, proj_ref, conv_ref, an_ref, win_ref, cw_ref, wout_ref, *rest):
        ready_refs, rest = rest[:n_ready], rest[n_ready:]
        dx_ref, h_ref, dproj_ref, dwout_ref, dan_ref, dcw_ref = rest[:6]
        part_refs, rest = rest[6:6 + n_ready], rest[6 + n_ready:]
        next_ref, d1_ref, d2_ref = rest[:3]
        sems = rest[3:]
        step = pl.program_id(0) * nt + pl.program_id(1)
        first = step == 0

        @pl.when(first)
        def _():
            for cp in _scatter_copies(ready_refs, part_refs, sems)[0]:
                cp.start()
            dwout_ref[...] = jnp.zeros_like(dwout_ref)
            dan_ref[...] = jnp.zeros_like(dan_ref)
            dcw_ref[...] = jnp.zeros_like(dcw_ref)

        @pl.when(pl.program_id(1) == 0)
        def _():
            next_ref[...] = jnp.zeros_like(next_ref)

        dx1v = dx1_ref[...]
        dy = _dot_nt(dx1v, wout_ref[...])
        b = proj_ref[:, 0:D_MODEL]
        cc = proj_ref[:, D_MODEL:2 * D_MODEL]
        u = proj_ref[:, 2 * D_MODEL:3 * D_MODEL]
        g = proj_ref[:, 3 * D_MODEL:4 * D_MODEL]
        cv = conv_ref[...]
        silu, dsilu = _silu_parts(g)
        per_part = D_MODEL // n_col

        def back_through_w_in(part, grad):
            grad = grad.astype(BF16)
            dproj_ref[:, part * D_MODEL:(part + 1) * D_MODEL] = grad
            term = None
            for j in range(per_part):
                d = part * per_part + j
                piece = lax.dot_general(grad[:, j * n_col:(j + 1) * n_col], win_ref[d], NT, preferred_element_type=F32)
                term = piece if term is None else term + piece
            return term

        dh = back_through_w_in(0, dy * silu * cv)
        dh += back_through_w_in(3, dy * b * cv * dsilu)
        dwout_ref[...] += _dot_tn(silu * b * cv, dx1v)
        dconv = dy * silu * b
        d1_ref[...] = pltpu.roll(dconv, tm - 1, 0)
        d2_ref[...] = pltpu.roll(dconv, tm - 2, 0)
        rows = lax.broadcasted_iota(jnp.int32, (8, D_MODEL), 0)
        n8, c8 = next_ref[...], dconv[tm - 8:tm]
        d1_ref[tm - 8:tm, :] = jnp.where(rows >= 7, pltpu.roll(n8, 7, 0), pltpu.roll(c8, 7, 0))
        d2_ref[tm - 8:tm, :] = jnp.where(rows >= 6, pltpu.roll(n8, 6, 0), pltpu.roll(c8, 6, 0))
        next_ref[...] = dconv[0:8]
        d1, d2 = d1_ref[...], d2_ref[...]
        v = cc * u
        dcw_ref[0:1, :] += jnp.sum(d2 * v, axis=0, keepdims=True)
        dcw_ref[1:2, :] += jnp.sum(d1 * v, axis=0, keepdims=True)
        dcw_ref[2:3, :] += jnp.sum(dconv * v, axis=0, keepdims=True)
        dv = cw_ref[0:1, :] * d2 + cw_ref[1:2, :] * d1 + cw_ref[2:3, :] * dconv
        dh += back_through_w_in(1, dv * u)
        dh += back_through_w_in(2, dv * cc)
        xv = x_ref[...]
        r = _rstd(xv)
        xh = xv * r
        h_ref[...] = (xh * an_ref[...]).T.astype(BF16)
        dan_ref[...] += jnp.sum(dh * xh, axis=0, keepdims=True)
        dx_ref[...] = dx1v + _norm_bwd(dh * an_ref[...], xh, r)

        @pl.when(step == n_seq * nt - 1)
        def _():
            for cp in _scatter_copies(ready_refs, part_refs, sems)[1]:
                cp.wait()

    tok = lambda width: pl.BlockSpec((tm, width), lambda s, i: (s * nt + nt - 1 - i, 0))
    t = n_seq * seq
    outs = pl.pallas_call(
        body, name="conv_mixer_bwd", grid=(n_seq, nt),
        in_specs=[tok(D_MODEL), tok(D_MODEL), tok(4 * D_MODEL), tok(D_MODEL), _const_spec((1, D_MODEL)),
                  _const_spec(w_in.shape), _const_spec((3, D_MODEL)), _const_spec(w_out.shape)] + [ANY_SPEC] * n_ready,
        out_specs=[tok(D_MODEL), pl.BlockSpec((D_MODEL, tm), lambda s, i: (0, s * nt + nt - 1 - i)),
                   tok(4 * D_MODEL), _acc_spec((D_MODEL, D_MODEL)),
                   _acc_spec((1, D_MODEL)), _acc_spec((8, D_MODEL))] + [ANY_SPEC] * n_ready,
        out_shape=[jax.ShapeDtypeStruct((t, D_MODEL), F32), jax.ShapeDtypeStruct((D_MODEL, t), BF16),
                   jax.ShapeDtypeStruct((t, 4 * D_MODEL), BF16), jax.ShapeDtypeStruct((D_MODEL, D_MODEL), F32),
                   jax.ShapeDtypeStruct((1, D_MODEL), F32), jax.ShapeDtypeStruct((8, D_MODEL), F32)]
        + [jax.ShapeDtypeStruct(a.shape, a.dtype) for a in ready_stacks],
        scratch_shapes=[pltpu.VMEM((8, D_MODEL), F32), pltpu.VMEM((tm, D_MODEL), F32), pltpu.VMEM((tm, D_MODEL), F32)]
        + _comm_sems(n_ready),
        compiler_params=_params(2),
    )(dx1, x, proj, conv, a_norm, w_in, conv_w, w_out, *ready_stacks)
    return outs[:6], outs[6:]


def _w_in_grad_exchange(a_t, b, n_col, chip, ready_stacks, small):
    r, t = a_t.shape
    bt = 1024
    nk = t // bt
    n_ready = len(ready_stacks)
    n_chip = N_DEV // 2
    n_remote = N_DEV - 2

    def body(chip_ref, a_ref, b_ref, *rest):
        ready_refs, small_ref, rest = rest[:n_ready], rest[n_ready], rest[n_ready + 1:]
        own_ref, got_ref, rest = rest[0], rest[1], rest[2:]
        part_refs, small_all_ref, rest = rest[:n_ready], rest[n_ready], rest[n_ready + 1:]
        acc_ref, land_ref, stage_ref, pair_send, pair_recv, chip_send, chip_recv = rest[:7]
        scatter_sems, gather_sems = rest[7:10], rest[10:13]
        s, k = pl.program_id(0), pl.program_id(1)
        x, y, c, _ = _place()
        chip = 2 * x + y

        @pl.when(jnp.logical_and(s == 0, k == 0))
        def _():
            for cp in _scatter_copies(ready_refs, part_refs, scatter_sems)[0]:
                cp.start()
            for cp in _gather_copies([small_ref], [small_all_ref], gather_sems)[0]:
                cp.start()

        for parity in range(2):
            @pl.when(s % 2 == parity)
            def _(parity=parity):
                @pl.when(k == 0)
                def _():
                    acc_ref[parity] = jnp.zeros((r, n_col), F32)

                acc_ref[parity] += jnp.dot(a_ref[...], b_ref[...], preferred_element_type=F32)

        def to_sibling(step):
            return pltpu.make_async_remote_copy(
                src_ref=acc_ref.at[step % 2], dst_ref=land_ref, send_sem=pair_send.at[step], recv_sem=pair_recv.at[step],
                device_id=(x, y, 1 - c), device_id_type=MESH)

        def to_owner(nth):
            owner_chip = (chip + 1 + nth) % n_chip
            slot = jnp.bitwise_xor(chip, owner_chip) - 1
            return pltpu.make_async_remote_copy(
                src_ref=stage_ref.at[nth % 2], dst_ref=got_ref.at[slot], send_sem=chip_send.at[nth],
                recv_sem=chip_recv.at[slot], device_id=(owner_chip // 2, owner_chip % 2, c), device_id_type=MESH)

        for step in range(N_DEV):
            owner_core = step % 2

            @pl.when(jnp.logical_and(s == step, k == nk - 1))
            def _(step=step, owner_core=owner_core):
                @pl.when(c != owner_core)
                def _():
                    to_sibling(step).start()

                if step >= 1:
                    @pl.when(c == owner_core)
                    def _():
                        to_sibling(step - 1).wait_send()

            if step < N_DEV - 1:
                pickup = jnp.logical_and(s == step + 1, k == min(PICKUP_BLOCK, nk - 1))
            else:
                pickup = jnp.logical_and(s == step, k == nk - 1)

            @pl.when(jnp.logical_and(pickup, c == owner_core))
            def _(step=step):
                to_sibling(step).wait_recv()
                total = acc_ref[step % 2] + land_ref[...]
                if step < n_remote:
                    nth = step // 2
                    if nth >= 2:
                        to_owner(nth - 2).wait_send()
                    stage_ref[nth % 2] = total.astype(BF16)
                    to_owner(nth).start()
                else:
                    own_ref[...] = total

        @pl.when(jnp.logical_and(s == N_DEV - 1, k == nk - 1))
        def _():
            @pl.when(c != (N_DEV - 1) % 2)
            def _():
                to_sibling(N_DEV - 1).wait_send()

            to_owner(1).wait_send()
            to_owner(2).wait_send()
            for slot in range(n_chip - 1):
                pltpu.make_async_remote_copy(
                    src_ref=stage_ref.at[0], dst_ref=got_ref.at[slot], send_sem=chip_send.at[0],
                    recv_sem=chip_recv.at[slot], device_id=(x, y, c), device_id_type=MESH).wait_recv()
            for cp in _scatter_copies(ready_refs, part_refs, scatter_sems)[1]:
                cp.wait()
            for cp in _gather_copies([small_ref], [small_all_ref], gather_sems)[1]:
                cp.wait()

    def owner_block(s, k, chip):
        return (k, 2 * ((chip[0] + 1 + s // 2) % n_chip) + s % 2)

    grid_spec = pltpu.PrefetchScalarGridSpec(
        num_scalar_prefetch=1, grid=(N_DEV, nk),
        in_specs=[pl.BlockSpec((r, bt), lambda s, k, chip: (0, k)), pl.BlockSpec((bt, n_col), owner_block)]
        + [ANY_SPEC] * (n_ready + 1),
        out_specs=[pl.BlockSpec((r, n_col), lambda s, k, chip: (0, 0)), ANY_SPEC] + [ANY_SPEC] * (n_ready + 1),
        scratch_shapes=[pltpu.VMEM((2, r, n_col), F32), pltpu.VMEM((r, n_col), F32), pltpu.VMEM((2, r, n_col), BF16),
                        pltpu.SemaphoreType.DMA((N_DEV,)), pltpu.SemaphoreType.DMA((N_DEV,)),
                        pltpu.SemaphoreType.DMA((n_chip - 1,)), pltpu.SemaphoreType.DMA((n_chip - 1,))]
        + _comm_sems(n_ready) + _comm_sems(1))
    outs = pl.pallas_call(
        body, name="w_in_grad_exchange", grid_spec=grid_spec,
        out_shape=[jax.ShapeDtypeStruct((r, n_col), F32), jax.ShapeDtypeStruct((n_chip - 1, r, n_col), BF16)]
        + [jax.ShapeDtypeStruct(p.shape, p.dtype) for p in ready_stacks] + _stacked([small]),
        compiler_params=_params(2),
    )(chip, a_t, b, *ready_stacks, small)
    return outs[0], outs[1], outs[2:2 + n_ready], outs[2 + n_ready]


def _proj_fwd(x1, pos, rc, kv_norm, w_dkv, ckv_norm, w_ukv, b_norm, b_w_in, q_norm, w_uq):
    t = x1.shape[0]
    tm = LATENT_ROW_TILE

    def body(x1_ref, pos_ref, rc_ref, kvn_ref, wdkv_ref, ckvn_ref, wukv_ref, bn_ref, bwin_ref, qn_ref, wuq_ref,
             ckr_ref, cq_ref, gb_ref, q_ref, k_ref, v_ref, cos_ref, sin_ref):
        xv = x1_ref[...]
        xh = xv * _rstd(xv)
        ckr = _dot(xh * kvn_ref[...], wdkv_ref[...])
        pb = _dot(xh * bn_ref[...], bwin_ref[...])
        ckr_ref[...] = ckr
        ckv = ckr[:, 0:KV_RANK]
        kv = _dot(ckv * _rstd(ckv) * ckvn_ref[...], wukv_ref[...])
        cosv, sinv = _rope_angles(pos_ref[...], rc_ref[...])
        cos_ref[...] = cosv
        sin_ref[...] = sinv
        ct, s1, s2 = _rope_tables(cosv, sinv, rc_ref[...])
        k_rope = _rope(ckr[:, KV_RANK:CKR_PAD], ct, s1, s2)
        lane = lax.broadcasted_iota(jnp.int32, (tm, HEAD_PAD), 1)
        low = lane < QK_NOPE
        for h in range(N_HEADS):
            kv_h = kv[:, h * HEAD_PAD:(h + 1) * HEAD_PAD]
            k_ref[:, h * HEAD_PAD:(h + 1) * HEAD_PAD] = jnp.where(low, kv_h, k_rope).astype(BF16)
            v_ref[:, h * HEAD_PAD:(h + 1) * HEAD_PAD] = jnp.where(low, pltpu.roll(kv_h, V_HEAD, 1), 1.0).astype(BF16)
        cq = pb[:, 0:Q_RANK]
        cq_ref[...] = cq
        gb_ref[...] = pb[:, Q_RANK:Q_RANK + E_B]
        q = _dot(cq * _rstd(cq) * qn_ref[...], wuq_ref[...])
        for h in range(N_HEADS):
            q_ref[:, h * HEAD_PAD:(h + 1) * HEAD_PAD] = _rope(
                q[:, h * HEAD_PAD:(h + 1) * HEAD_PAD], ct, s1, s2).astype(BF16)

    tok = lambda width: pl.BlockSpec((tm, width), lambda i: (i, 0))
    weights = [kv_norm, w_dkv, ckv_norm, w_ukv, b_norm, b_w_in, q_norm, w_uq]
    wide = N_HEADS * HEAD_PAD
    return pl.pallas_call(
        body, name="latent_proj_fwd", grid=(t // tm,),
        in_specs=[tok(D_MODEL), tok(1), _const_spec(rc.shape)] + [_const_spec(w.shape) for w in weights],
        out_specs=[tok(CKR_PAD), tok(Q_RANK), tok(E_B), tok(wide), tok(wide), tok(wide), tok(HEAD_PAD), tok(HEAD_PAD)],
        out_shape=[jax.ShapeDtypeStruct((t, CKR_PAD), F32), jax.ShapeDtypeStruct((t, Q_RANK), F32),
                   jax.ShapeDtypeStruct((t, E_B), F32), jax.ShapeDtypeStruct((t, wide), BF16),
                   jax.ShapeDtypeStruct((t, wide), BF16), jax.ShapeDtypeStruct((t, wide), BF16),
                   jax.ShapeDtypeStruct((t, HEAD_PAD), F32), jax.ShapeDtypeStruct((t, HEAD_PAD), F32)],
        compiler_params=_params(1),
    )(x1, pos, rc, *weights)


def _proj_bwd(dq, dk, dv, dgb, cq, ckr, x1, dx2, cosv, sinv, rc, kv_norm, w_dkv, ckv_norm, w_ukv, b_norm, b_w_in, q_norm, w_uq):
    t = x1.shape[0]
    tm = LATENT_ROW_TILE
    wide = N_HEADS * HEAD_PAD

    def body(dq_ref, dk_ref, dv_ref, dgb_ref, cq_ref, ckr_ref, x1_ref, dx2_ref, cos_ref, sin_ref, rc_ref,
             kvn_ref, wdkv_ref, ckvn_ref, wukv_ref, bn_ref, bwin_ref, qn_ref, wuq_ref,
             dx1_ref, dwuq_ref, dwbin_ref, dwukv_ref, dwdkv_ref, dqn_ref, dbn_ref, dckvn_ref, dkvn_ref, dqu_ref, dkv_ref):
        @pl.when(pl.program_id(0) == 0)
        def _():
            for ref in (dwuq_ref, dwbin_ref, dwukv_ref, dwdkv_ref, dqn_ref, dbn_ref, dckvn_ref, dkvn_ref):
                ref[...] = jnp.zeros_like(ref)

        ct, s1, s2 = _rope_tables(cos_ref[...], sin_ref[...], rc_ref[...])
        lane = lax.broadcasted_iota(jnp.int32, (tm, HEAD_PAD), 1)
        low = lane < QK_NOPE
        xv = x1_ref[...]
        r = _rstd(xv)
        xh = xv * r
        dk_rope = jnp.zeros((tm, HEAD_PAD), F32)
        for h in range(N_HEADS):
            dk_h = dk_ref[:, h * HEAD_PAD:(h + 1) * HEAD_PAD]
            dv_h = dv_ref[:, h * HEAD_PAD:(h + 1) * HEAD_PAD]
            dkv_ref[:, h * HEAD_PAD:(h + 1) * HEAD_PAD] = jnp.where(low, dk_h, pltpu.roll(dv_h, V_HEAD, 1)).astype(BF16)
            dk_rope += dk_h
        rope_lanes = jnp.logical_and(lane >= ROPE_LO, lane < ROPE_LO + QK_ROPE)
        dk_rope = jnp.where(rope_lanes, _rope(dk_rope, ct, -s1, -s2), 0.0)
        ckv = ckr_ref[:, 0:KV_RANK]
        rk = _rstd(ckv)
        ckh = ckv * rk
        ckvn = (ckh * ckvn_ref[...]).astype(BF16)
        dwukv = lax.dot_general(ckvn, dkv_ref[...], TN, preferred_element_type=F32)
        for h in range(N_HEADS):
            dwukv_ref[h] += dwukv[:, h * HEAD_PAD:(h + 1) * HEAD_PAD]
        dckvn = lax.dot_general(dkv_ref[...], wukv_ref[...], NT, preferred_element_type=F32)
        dckvn_ref[...] += jnp.sum(dckvn * ckh, axis=0, keepdims=True)
        dckv = _norm_bwd(dckvn * ckvn_ref[...], ckh, rk)
        dckr = jnp.concatenate([dckv, dk_rope], axis=1).astype(BF16)
        dwdkv_ref[...] += _dot_tn(xh * kvn_ref[...], dckr)
        dh2 = lax.dot_general(dckr, wdkv_ref[...], NT, preferred_element_type=F32)
        for h in range(N_HEADS):
            dqu_ref[:, h * HEAD_PAD:(h + 1) * HEAD_PAD] = _rope(
                dq_ref[:, h * HEAD_PAD:(h + 1) * HEAD_PAD], ct, -s1, -s2).astype(BF16)
        cq = cq_ref[...]
        rq = _rstd(cq)
        cqh = cq * rq
        cqn = (cqh * qn_ref[...]).astype(BF16)
        dwuq = lax.dot_general(cqn, dqu_ref[...], TN, preferred_element_type=F32)
        for h in range(N_HEADS):
            dwuq_ref[h] += dwuq[:, h * HEAD_PAD:(h + 1) * HEAD_PAD]
        dcqn = lax.dot_general(dqu_ref[...], wuq_ref[...], NT, preferred_element_type=F32)
        dqn_ref[...] += jnp.sum(dcqn * cqh, axis=0, keepdims=True)
        dcq = _norm_bwd(dcqn * qn_ref[...], cqh, rq)
        dpb = jnp.concatenate([dcq, dgb_ref[...]], axis=1).astype(BF16)
        dwbin_ref[...] += _dot_tn(xh * bn_ref[...], dpb)
        dh3 = lax.dot_general(dpb, bwin_ref[...], NT, preferred_element_type=F32)
        dkvn_ref[...] += jnp.sum(dh2 * xh, axis=0, keepdims=True)
        dbn_ref[...] += jnp.sum(dh3 * xh, axis=0, keepdims=True)
        dx1_ref[...] = dx2_ref[...] + _norm_bwd(dh2 * kvn_ref[...] + dh3 * bn_ref[...], xh, r)

    tok = lambda width: pl.BlockSpec((tm, width), lambda i: (i, 0))
    weights = [kv_norm, w_dkv, ckv_norm, w_ukv, b_norm, b_w_in, q_norm, w_uq]
    acc_shapes = [(N_HEADS, Q_RANK, HEAD_PAD), (D_MODEL, Q_RANK + E_B), (N_HEADS, KV_RANK, HEAD_PAD), (D_MODEL, CKR_PAD),
                  (1, Q_RANK), (1, D_MODEL), (1, KV_RANK), (1, D_MODEL)]
    return pl.pallas_call(
        body, name="latent_proj_bwd", grid=(t // tm,),
        in_specs=[tok(wide), tok(wide), tok(wide), tok(E_B), tok(Q_RANK), tok(CKR_PAD), tok(D_MODEL), tok(D_MODEL),
                  tok(HEAD_PAD), tok(HEAD_PAD), _const_spec(rc.shape)] + [_const_spec(w.shape) for w in weights],
        out_specs=[tok(D_MODEL)] + [_acc_spec(s) for s in acc_shapes],
        out_shape=[jax.ShapeDtypeStruct((t, D_MODEL), F32)] + [jax.ShapeDtypeStruct(s, F32) for s in acc_shapes],
        scratch_shapes=[pltpu.VMEM((tm, wide), BF16), pltpu.VMEM((tm, wide), BF16)],
        compiler_params=_params(1),
    )(dq, dk, dv, dgb, cq, ckr, x1, dx2, cosv, sinv, rc, *weights)


def _attn_fwd(q, k, v, n_seq, seq):
    tb, group = ATTN_TILE, ATTN_GROUP
    rows = tb * group
    nq = seq // rows
    pair = 2 * HEAD_PAD

    def body(q_ref, k_ref, v_ref, o_ref, lr_ref):
        i = pl.program_id(2)
        row = lax.broadcasted_iota(jnp.int32, (tb, tb), 0)
        col = lax.broadcasted_iota(jnp.int32, (tb, tb), 1)
        causal = col <= row

        def step(j, carry, kinds):
            start = pl.multiple_of(j * tb, tb)
            out = []
            for g in range(group):
                for hh in range(2):
                    m, acc = carry[2 * g + hh]
                    if kinds[g] != "none":
                        heads = slice(hh * HEAD_PAD, (hh + 1) * HEAD_PAD)
                        s = lax.dot_general(q_ref[g * tb:(g + 1) * tb, heads], k_ref[pl.ds(start, tb), heads], NT,
                                            preferred_element_type=F32) * (SOFTMAX_SCALE * LOG2_E)
                        if kinds[g] == "diagonal":
                            s = jnp.where(causal, s, -jnp.inf)
                        m_new = jnp.maximum(m, jnp.max(s, axis=-1, keepdims=True))
                        p = jnp.exp2(s - m_new)
                        acc = jnp.exp2(m - m_new) * acc + jnp.dot(
                            p.astype(BF16), v_ref[pl.ds(start, tb), heads], preferred_element_type=F32)
                        m = m_new
                    out.append((m, acc))
            return tuple(out)

        one = (jnp.full((tb, 1), -jnp.inf, F32), jnp.zeros((tb, HEAD_PAD), F32))
        carry = lax.fori_loop(0, i * group, functools.partial(step, kinds=("full",) * group), (one,) * (2 * group))
        for d in range(group):
            kinds = tuple("none" if g < d else ("diagonal" if g == d else "full") for g in range(group))
            carry = step(i * group + d, carry, kinds)
        lane = lax.broadcasted_iota(jnp.int32, (tb, HEAD_PAD), 1)
        low = lane < V_HEAD
        for g in range(group):
            halves = []
            for hh in range(2):
                m, acc = carry[2 * g + hh]
                swapped = pltpu.roll(acc, V_HEAD, 1)
                halves.append(acc / swapped)
                lse = (m * LN_2) + jnp.log(jnp.where(low, swapped, acc))
                lr_ref[hh, :, g * tb:(g + 1) * tb] = lse.T[0:8, :]
            o_ref[g * tb:(g + 1) * tb, :] = jnp.where(low, halves[0], pltpu.roll(halves[1], V_HEAD, 1))

    t = n_seq * seq
    return pl.pallas_call(
        body, name="attention_fwd", grid=(n_seq, N_HEADS // 2, nq),
        in_specs=[pl.BlockSpec((rows, pair), lambda s, p, i: (s * nq + i, p)),
                  pl.BlockSpec((seq, pair), lambda s, p, i: (s, p)),
                  pl.BlockSpec((seq, pair), lambda s, p, i: (s, p))],
        out_specs=[pl.BlockSpec((rows, HEAD_PAD), lambda s, p, i: (s * nq + i, p)),
                   pl.BlockSpec((None, 2, 8, rows), lambda s, p, i: (s, p, 0, i))],
        out_shape=[jax.ShapeDtypeStruct((t, E_B), F32), jax.ShapeDtypeStruct((n_seq, N_HEADS, 8, seq), F32)],
        compiler_params=_params(3),
    )(q, k, v)


def _attn_bwd(q, k, v, do, lse_row, delta_row, n_seq, seq):
    tb, group = ATTN_TILE, ATTN_BWD_GROUP
    rows = tb * group
    nk = seq // rows
    n_inner = seq // tb
    pair = 2 * HEAD_PAD

    def body(q_ref, k_ref, v_ref, do_ref, lr_ref, dr_ref, dk_ref, dv_ref, dq_ref):
        j = pl.program_id(2)

        @pl.when(j == 0)
        def _():
            dq_ref[...] = jnp.zeros_like(dq_ref)

        row = lax.broadcasted_iota(jnp.int32, (tb, tb), 0)
        col = lax.broadcasted_iota(jnp.int32, (tb, tb), 1)
        causal = col >= row

        def step(i, carry, kinds):
            start = pl.multiple_of(i * tb, tb)
            out = []
            for hh in range(2):
                heads = slice(hh * HEAD_PAD, (hh + 1) * HEAD_PAD)
                qi = q_ref[pl.ds(start, tb), heads]
                doi = do_ref[pl.ds(start, tb), heads]
                dq = None
                for g in range(group):
                    dk_acc, dv_acc = carry[hh * group + g]
                    if kinds[g] != "none":
                        kg = k_ref[g * tb:(g + 1) * tb, heads]
                        st = lax.dot_general(kg, qi, NT, preferred_element_type=F32) * SOFTMAX_SCALE
                        pt = jnp.exp(st - lr_ref[hh, 0:1, pl.ds(start, tb)])
                        if kinds[g] == "diagonal":
                            pt = jnp.where(causal, pt, 0.0)
                        dv_acc = dv_acc + jnp.dot(pt.astype(BF16), doi, preferred_element_type=F32)
                        dpt = lax.dot_general(v_ref[g * tb:(g + 1) * tb, heads], doi, NT, preferred_element_type=F32)
                        dst = (pt * (dpt - dr_ref[hh, 0:1, pl.ds(start, tb)]) * SOFTMAX_SCALE).astype(BF16)
                        dk_acc = dk_acc + jnp.dot(dst, qi, preferred_element_type=F32)
                        term = lax.dot_general(dst, kg, TN, preferred_element_type=F32)
                        dq = term if dq is None else dq + term
                    out.append((dk_acc, dv_acc))
                dq_ref[pl.ds(start, tb), heads] += dq
            return tuple(out)

        one = (jnp.zeros((tb, HEAD_PAD), F32), jnp.zeros((tb, HEAD_PAD), F32))
        carry = (one,) * (2 * group)
        for d in range(group):
            kinds = tuple("full" if g < d else ("diagonal" if g == d else "none") for g in range(group))
            carry = step(j * group + d, carry, kinds)
        carry = lax.fori_loop((j + 1) * group, n_inner, functools.partial(step, kinds=("full",) * group), carry)
        for hh in range(2):
            for g in range(group):
                dk_ref[g * tb:(g + 1) * tb, hh * HEAD_PAD:(hh + 1) * HEAD_PAD] = carry[hh * group + g][0]
                dv_ref[g * tb:(g + 1) * tb, hh * HEAD_PAD:(hh + 1) * HEAD_PAD] = carry[hh * group + g][1]

    t = n_seq * seq
    wide = N_HEADS * HEAD_PAD
    return pl.pallas_call(
        body, name="attention_bwd", grid=(n_seq, N_HEADS // 2, nk),
        in_specs=[pl.BlockSpec((seq, pair), lambda s, p, j: (s, p)),
                  pl.BlockSpec((rows, pair), lambda s, p, j: (s * nk + j, p)),
                  pl.BlockSpec((rows, pair), lambda s, p, j: (s * nk + j, p)),
                  pl.BlockSpec((seq, pair), lambda s, p, j: (s, p)),
                  pl.BlockSpec((None, 2, 8, seq), lambda s, p, j: (s, p, 0, 0)),
                  pl.BlockSpec((None, 2, 8, seq), lambda s, p, j: (s, p, 0, 0))],
        out_specs=[pl.BlockSpec((rows, pair), lambda s, p, j: (s * nk + j, p)),
                   pl.BlockSpec((rows, pair), lambda s, p, j: (s * nk + j, p)),
                   pl.BlockSpec((seq, pair), lambda s, p, j: (s, p))],
        out_shape=[jax.ShapeDtypeStruct((t, wide), F32), jax.ShapeDtypeStruct((t, wide), F32),
                   jax.ShapeDtypeStruct((t, wide), F32)],
        compiler_params=_params(3),
    )(q, k, v, do, lse_row, delta_row)


def _head_tail(o, gb, x1, w_out, final_norm, target, n_seq, seq):
    tm = LATENT_ROW_TILE
    nt = seq // tm
    n_col = D_MODEL // N_DEV

    def body(o_ref, gb_ref, x1_ref, wout_ref, fn_ref, tgt_ref,
             dx2_ref, do_ref, dgb_ref, dr_ref, dwout_ref, dfn_ref, loss_ref):
        first = jnp.logical_and(pl.program_id(0) == 0, pl.program_id(1) == 0)

        @pl.when(first)
        def _():
            dwout_ref[...] = jnp.zeros_like(dwout_ref)
            dfn_ref[...] = jnp.zeros_like(dfn_ref)
            loss_ref[...] = jnp.zeros_like(loss_ref)

        ov, g = o_ref[...], gb_ref[...]
        silu, dsilu = _silu_parts(g)
        gated = (ov * silu).astype(BF16)
        x2 = x1_ref[...] + jnp.dot(gated, wout_ref[...], preferred_element_type=F32)
        r = _rstd(x2)
        xh = x2 * r
        err = xh * fn_ref[...] - tgt_ref[...]
        loss_ref[...] += 0.5 * jnp.sum(jnp.mean(err * err, axis=-1, keepdims=True), axis=0, keepdims=True)
        dy = err / D_MODEL
        dfn_ref[...] += jnp.sum(dy * xh, axis=0, keepdims=True)
        dx2 = _norm_bwd(dy * fn_ref[...], xh, r)
        dx2_ref[...] = dx2
        dx2b = dx2.astype(BF16)
        dw = lax.dot_general(gated, dx2b, TN, preferred_element_type=F32)
        for d in range(N_DEV):
            dwout_ref[d] += dw[:, d * n_col:(d + 1) * n_col]
        dgated = lax.dot_general(dx2b, wout_ref[...], NT, preferred_element_type=F32)
        do = dgated * silu
        dgb_ref[...] = dgated * ov * dsilu
        prod = do * ov
        lane = lax.broadcasted_iota(jnp.int32, (tm, HEAD_PAD), 1)
        low = lane < V_HEAD
        for p in range(N_HEADS // 2):
            blk = prod[:, p * HEAD_PAD:(p + 1) * HEAD_PAD]
            do_pair = do[:, p * HEAD_PAD:(p + 1) * HEAD_PAD]
            for hh in range(2):
                h = 2 * p + hh
                mine = do_pair if hh == 0 else pltpu.roll(do_pair, V_HEAD, 1)
                do_ref[:, h * HEAD_PAD:(h + 1) * HEAD_PAD] = jnp.where(low, mine, 0.0).astype(BF16)
                delta = jnp.sum(jnp.where(low if hh == 0 else ~low, blk, 0.0), axis=-1, keepdims=True)
                dr_ref[h] = jnp.broadcast_to(delta, (tm, HEAD_PAD)).T[0:8, :]

    tok = lambda width: pl.BlockSpec((tm, width), lambda s, i: (s * nt + i, 0))
    t = n_seq * seq
    return pl.pallas_call(
        body, name="head_tail", grid=(n_seq, nt),
        in_specs=[tok(E_B), tok(E_B), tok(D_MODEL), _const_spec(w_out.shape), _const_spec((1, D_MODEL)), tok(D_MODEL)],
        out_specs=[tok(D_MODEL), tok(N_HEADS * HEAD_PAD), tok(E_B),
                   pl.BlockSpec((None, N_HEADS, 8, tm), lambda s, i: (s, 0, 0, i)),
                   _acc_spec((N_DEV, E_B, n_col)), _acc_spec((1, D_MODEL)), _acc_spec((1, 1))],
        out_shape=[jax.ShapeDtypeStruct((t, D_MODEL), F32), jax.ShapeDtypeStruct((t, N_HEADS * HEAD_PAD), BF16),
                   jax.ShapeDtypeStruct((t, E_B), F32),
                   jax.ShapeDtypeStruct((n_seq, N_HEADS, 8, seq), F32), jax.ShapeDtypeStruct((N_DEV, E_B, n_col), F32),
                   jax.ShapeDtypeStruct((1, D_MODEL), F32), jax.ShapeDtypeStruct((1, 1), F32)],
        compiler_params=_params(2),
    )(o, gb, x1, w_out, final_norm, target)


def _adamw_math(w, g, m, v):
    m = ADAM_B1 * m + (1.0 - ADAM_B1) * g
    v = ADAM_B2 * v + (1.0 - ADAM_B2) * jnp.square(g)
    m_hat = m / (1.0 - ADAM_B1 ** ADAM_STEP)
    v_hat = v / (1.0 - ADAM_B2 ** ADAM_STEP)
    delta = -ADAM_LR * (m_hat / (jnp.sqrt(v_hat) + ADAM_EPS) + ADAM_WD * w)
    return delta, m, v


def _adamw_reduce(own, parts, w, m, v):
    rows, cols = w.shape
    br = 256
    n_parts = parts.shape[0]

    def body(own_ref, p_ref, w_ref, m_ref, v_ref, g_ref, d_ref, nm_ref, nv_ref):
        g = own_ref[...]
        for k in range(n_parts):
            g = g + p_ref[k].astype(F32)
        g_ref[...] = g
        d_ref[...], nm_ref[...], nv_ref[...] = _adamw_math(w_ref[...], g, m_ref[...], v_ref[...])

    blk = pl.BlockSpec((br, cols), lambda i: (i, 0))
    return pl.pallas_call(
        body, name="adamw_reduce", grid=(rows // br,),
        in_specs=[blk, pl.BlockSpec((n_parts, br, cols), lambda i: (0, i, 0)), blk, blk, blk],
        out_specs=[blk] * 4, out_shape=[jax.ShapeDtypeStruct((rows, cols), F32)] * 4,
        compiler_params=_params(1),
    )(own, parts, w, m, v)


def _adamw_reduce_many(parts_list, states):
    n = len(parts_list)

    def body(*refs):
        ins, outs = refs[:4 * n], refs[4 * n:]
        for j in range(n):
            p_ref, w_ref, m_ref, v_ref = ins[4 * j:4 * j + 4]
            g = p_ref[0]
            for k in range(1, p_ref.shape[0]):
                g = g + p_ref[k]
            outs[4 * j][...] = g
            outs[4 * j + 1][...], outs[4 * j + 2][...], outs[4 * j + 3][...] = _adamw_math(
                w_ref[...], g, m_ref[...], v_ref[...])

    flat_in = [a for p, st in zip(parts_list, states) for a in (p,) + tuple(st)]
    shapes = [jax.ShapeDtypeStruct(st[0].shape, F32) for st in states for _ in range(4)]
    flat = pl.pallas_call(body, name="adamw_reduce_many", out_shape=shapes,
                          compiler_params=pltpu.CompilerParams(vmem_limit_bytes=VMEM_LIMIT))(*flat_in)
    return [tuple(flat[4 * j:4 * j + 4]) for j in range(n)]


def _sum_parts(parts):
    def body(p_ref, o_ref):
        g = p_ref[0]
        for k in range(1, N_DEV):
            g = g + p_ref[k]
        o_ref[...] = g

    return pl.pallas_call(body, name="small_grad_sum", out_shape=jax.ShapeDtypeStruct(parts.shape[1:], F32))(parts)


def _adamw_small(gs, ws, ms, vs):
    n = len(gs)

    def body(*refs):
        ins, outs = refs[:4 * n], refs[4 * n:]
        for j in range(n):
            g_ref, w_ref, m_ref, v_ref = ins[j], ins[n + j], ins[2 * n + j], ins[3 * n + j]
            outs[3 * j][...], outs[3 * j + 1][...], outs[3 * j + 2][...] = _adamw_math(
                w_ref[...], g_ref[...], m_ref[...], v_ref[...])

    shapes = [jax.ShapeDtypeStruct(g.shape, F32) for g in gs for _ in range(3)]
    flat = pl.pallas_call(body, name="adamw_small", out_shape=shapes)(*gs, *ws, *ms, *vs)
    return [tuple(flat[3 * j:3 * j + 3]) for j in range(n)]


SMALL_ROWS = 8
LOSS_LANE = D_MODEL - 1


def _pack_small(g_fn, g_kvn, g_bn, g_an, g_ckvn, g_qn, g_conv, loss_part):
    def body(fn_ref, kvn_ref, bn_ref, an_ref, ckvn_ref, qn_ref, conv_ref, loss_ref, o_ref):
        o_ref[0:1, :] = fn_ref[...]
        o_ref[1:2, :] = kvn_ref[...]
        o_ref[2:3, :] = bn_ref[...]
        o_ref[3:4, :] = an_ref[...]
        lane = lax.broadcasted_iota(jnp.int32, (1, D_MODEL), 1)
        o_ref[4:5, :] = jnp.where(lane == LOSS_LANE, loss_ref[...], 0.0)
        o_ref[4:5, 0:KV_RANK] = ckvn_ref[...]
        o_ref[4:5, KV_RANK:KV_RANK + Q_RANK] = qn_ref[...]
        o_ref[5:8, :] = conv_ref[0:3, :]

    return pl.pallas_call(body, name="pack_small", out_shape=jax.ShapeDtypeStruct((SMALL_ROWS, D_MODEL), F32))(
        g_fn, g_kvn, g_bn, g_an, g_ckvn, g_qn, g_conv, loss_part)


def _pad_cols(a, width):
    return jnp.pad(a, ((0, 0), (0, width - a.shape[1])))


def _dkv_to_padded(a):
    r = a.shape[0]
    z = jnp.zeros((r, ROPE_LO), a.dtype)
    z2 = jnp.zeros((r, HEAD_PAD - ROPE_LO - QK_ROPE), a.dtype)
    return jnp.concatenate([a[:, :KV_RANK], z, a[:, KV_RANK:], z2], axis=1)


def _dkv_from_padded(a):
    return jnp.concatenate([a[:, :KV_RANK], a[:, KV_RANK + ROPE_LO:KV_RANK + ROPE_LO + QK_ROPE]], axis=1)


def _unstack_cols(a):
    return jnp.transpose(a, (1, 0, 2)).reshape(a.shape[1], N_DEV * a.shape[2])


def kernel(x, positions, a_norm, a_w_in, a_conv, a_w_out, kv_norm, w_dkv, ckv_norm, w_ukv, b_norm, b_w_in, b_q_norm, b_w_uq, b_w_out, final_norm, loss_target, m_a_norm, m_a_w_in, m_a_conv, m_a_w_out, m_kv_norm, m_w_dkv, m_ckv_norm, m_w_ukv, m_b_norm, m_b_w_in, m_b_q_norm, m_b_w_uq, m_b_w_out, m_final_norm, v_a_norm, v_a_w_in, v_a_conv, v_a_w_out, v_kv_norm, v_w_dkv, v_ckv_norm, v_w_ukv, v_b_norm, v_b_w_in, v_b_q_norm, v_b_w_uq, v_b_w_out, v_final_norm):
    n_seq, seq, _ = x.shape
    t = n_seq * seq
    me = 4 * lax.axis_index("x") + 2 * lax.axis_index("y") + lax.axis_index("c")

    big = {
        "a_w_in": (a_w_in[0], m_a_w_in[0], v_a_w_in[0]),
        "a_w_out": (a_w_out[0], m_a_w_out[0], v_a_w_out[0]),
        "w_dkv": tuple(_dkv_to_padded(a) for a in (w_dkv, m_w_dkv, v_w_dkv)),
        "w_ukv": (w_ukv, m_w_ukv, v_w_ukv),
        "b_w_in": (b_w_in[0], m_b_w_in[0], v_b_w_in[0]),
        "b_w_uq": tuple(_pad_cols(a[0], HEAD_PAD) for a in (b_w_uq, m_b_w_uq, v_b_w_uq)),
        "b_w_out": (b_w_out[0], m_b_w_out[0], v_b_w_out[0]),
    }
    names = list(big)
    first_names, later_names = names[:2], names[2:]
    gathered = _all_gather([big[n][0].astype(BF16) for n in first_names] + [a_norm, a_conv[0]])
    a_norm_f = gathered[2].reshape(1, D_MODEL)
    a_conv_f = _unstack_cols(gathered[3])
    w_a_in = gathered[0]
    w_a_out = gathered[1].reshape(D_MODEL, D_MODEL)

    x2d = x.reshape(t, D_MODEL)
    tgt = loss_target.reshape(t, D_MODEL)
    pos = positions.astype(F32).reshape(t, 1)
    rc = _rope_consts()
    kvn, ckvn, bn, qn, fn = kv_norm.reshape(1, -1), ckv_norm.reshape(1, -1), b_norm, b_q_norm, final_norm.reshape(1, -1)

    x1, proj, conv, later = _conv_fwd(x2d, a_norm_f, w_a_in, a_conv_f, w_a_out, n_seq, seq,
                                      [big[n][0].astype(BF16) for n in later_names])
    full = dict(zip(later_names, later))
    w_dkv_f = full["w_dkv"].reshape(D_MODEL, CKR_PAD)
    w_ukv_f = _unstack_cols(full["w_ukv"])
    w_b_in = full["b_w_in"].reshape(D_MODEL, Q_RANK + E_B)
    w_uq_f = _unstack_cols(full["b_w_uq"])
    w_b_out = _unstack_cols(full["b_w_out"])
    layer_b = (kvn, w_dkv_f, ckvn, w_ukv_f, bn, w_b_in, qn, w_uq_f)
    ckr, cq, gb, q, k, v, cosv, sinv = _proj_fwd(x1, pos, rc, *layer_b)
    o, lse_row = _attn_fwd(q, k, v, n_seq, seq)
    dx2, do, dgb, delta_row, g_b_out, g_fn, loss_part = _head_tail(o, gb, x1, w_b_out, fn, tgt, n_seq, seq)
    dk, dv, dq = _attn_bwd(q, k, v, do, lse_row, delta_row, n_seq, seq)
    dx1, g_uq, g_b_in, g_ukv, g_dkv, g_qn, g_bn, g_ckvn, g_kvn = _proj_bwd(
        dq, dk, dv, dgb, cq, ckr, x1, dx2, cosv, sinv, rc, *layer_b)
    stacks = {
        "w_dkv": g_dkv.reshape(N_DEV, D_MODEL // N_DEV, CKR_PAD),
        "w_ukv": g_ukv,
        "b_w_in": g_b_in.reshape(N_DEV, D_MODEL // N_DEV, Q_RANK + E_B),
        "b_w_uq": g_uq,
        "b_w_out": g_b_out,
    }
    (dx, h_a, dproj, g_a_out, g_an, g_conv), later_parts = _conv_bwd(
        dx1, x2d, proj, conv, a_norm_f, w_a_in, a_conv_f, w_a_out, n_seq, seq, [stacks[n] for n in later_names])
    parts = dict(zip(later_names, later_parts))
    small = _pack_small(g_fn, g_kvn, g_bn, g_an, g_ckvn, g_qn, g_conv, loss_part)
    a_in_own, a_in_got, (parts["a_w_out"],), small_parts = _w_in_grad_exchange(
        h_a, dproj, D_MODEL * 4 // N_DEV, (me // 2).reshape(1).astype(jnp.int32),
        [g_a_out.reshape(N_DEV, D_MODEL // N_DEV, D_MODEL)], small)

    outs = {"a_w_in": _adamw_reduce(a_in_own, a_in_got, *big["a_w_in"])}
    outs.update(zip(names[1:], _adamw_reduce_many([parts[n] for n in names[1:]], [big[n] for n in names[1:]])))
    outs["w_dkv"] = tuple(_dkv_from_padded(a) for a in outs["w_dkv"])
    outs["b_w_uq"] = tuple(a[:, :QK_NOPE + QK_ROPE] for a in outs["b_w_uq"])
    for n in ("a_w_in", "a_w_out", "b_w_in", "b_w_uq", "b_w_out"):
        outs[n] = tuple(a[None] for a in outs[n])

    total = _sum_parts(small_parts)
    loss = total[4, LOSS_LANE]
    shard = D_MODEL // N_DEV
    g_small = {
        "final_norm": total[0], "kv_norm": total[1], "b_norm": total[2:3],
        "a_norm": lax.dynamic_slice_in_dim(total[3:4], me * shard, shard, axis=1),
        "ckv_norm": total[4, 0:KV_RANK], "b_q_norm": total[4:5, KV_RANK:KV_RANK + Q_RANK],
        "a_conv": lax.dynamic_slice_in_dim(total[5:8], me * shard, shard, axis=1)[None],
    }
    small_state = {
        "final_norm": (final_norm, m_final_norm, v_final_norm), "kv_norm": (kv_norm, m_kv_norm, v_kv_norm),
        "b_norm": (b_norm, m_b_norm, v_b_norm), "a_norm": (a_norm, m_a_norm, v_a_norm),
        "ckv_norm": (ckv_norm, m_ckv_norm, v_ckv_norm), "b_q_norm": (b_q_norm, m_b_q_norm, v_b_q_norm),
        "a_conv": (a_conv, m_a_conv, v_a_conv),
    }
    small_names = list(g_small)
    as2d = lambda a: a.reshape(-1, a.shape[-1])
    upd = _adamw_small([as2d(g_small[n]) for n in small_names],
                       *[[as2d(small_state[n][j]) for n in small_names] for j in range(3)])
    for n, u in zip(small_names, upd):
        outs[n] = (g_small[n],) + tuple(a.reshape(g_small[n].shape) for a in u)

    order = ["a_norm", "a_w_in", "a_conv", "a_w_out", "kv_norm", "w_dkv", "ckv_norm", "w_ukv", "b_norm", "b_w_in",
             "b_q_norm", "b_w_uq", "b_w_out", "final_norm"]
    result = [loss, dx.reshape(n_seq, seq, D_MODEL)]
    for j in range(4):
        result += [outs[n][j] for n in order]
    return tuple(result)
```

```python
import functools
import math

import numpy as np
import jax
import jax.numpy as jnp
from jax import lax
from jax.experimental import pallas as pl
from jax.experimental.pallas import tpu as pltpu

F32 = jnp.float32
BF16 = jnp.bfloat16
SAVED = BF16

D_MODEL = 1024
N_HEADS = 8
QK_NOPE = 64
QK_ROPE = 32
V_HEAD = 64
KV_RANK = 256
Q_RANK = 384
E_B = N_HEADS * V_HEAD
HEAD_PAD = 128
CKR_PAD = KV_RANK + HEAD_PAD
ROPE_LO = QK_NOPE
ROPE_HALF = QK_ROPE // 2
ROPE_THETA = 10000.0
SOFTMAX_SCALE = 1.0 / math.sqrt(QK_NOPE + QK_ROPE)
LOG2_E = math.log2(math.e)
LN_2 = math.log(2.0)
EPS = 1e-6
N_DEV = 8

ADAM_LR = 0.001
ADAM_B1 = 0.9
ADAM_B2 = 0.999
ADAM_EPS = 1e-08
ADAM_WD = 0.01
ADAM_STEP = 10

ROW_TILE = 256
LATENT_ROW_TILE = 512
PICKUP_BLOCK = 3
ATTN_TILE = 512
ATTN_GROUP = 4
ATTN_BWD_GROUP = 4
VMEM_LIMIT = 56 * 1024 * 1024

MESH = pl.DeviceIdType.MESH
NT = (((1,), (1,)), ((), ()))
TN = (((0,), (0,)), ((), ()))


def _dot(a, b):
    return jnp.dot(a.astype(BF16), b.astype(BF16), preferred_element_type=F32)


def _dot_nt(a, b):
    return lax.dot_general(a.astype(BF16), b.astype(BF16), NT, preferred_element_type=F32)


def _dot_tn(a, b):
    return lax.dot_general(a.astype(BF16), b.astype(BF16), TN, preferred_element_type=F32)


def _rstd(x):
    return lax.rsqrt(jnp.mean(x * x, axis=-1, keepdims=True) + EPS)


def _norm_bwd(a, xh, r):
    return r * (a - xh * jnp.mean(a * xh, axis=-1, keepdims=True))


def _silu_parts(g):
    sg = jax.nn.sigmoid(g)
    return g * sg, sg * (1.0 + g * (1.0 - sg))


def _rope_consts():
    inv = (ROPE_THETA ** (-np.arange(0, QK_ROPE, 2, dtype=np.float32) / QK_ROPE)).astype(np.float32)
    t = np.zeros((8, HEAD_PAD), np.float32)
    t[0, ROPE_LO:ROPE_LO + ROPE_HALF] = inv
    t[0, ROPE_LO + ROPE_HALF:ROPE_LO + QK_ROPE] = inv
    t[1, ROPE_LO:ROPE_LO + ROPE_HALF] = -1.0
    t[2, ROPE_LO + ROPE_HALF:ROPE_LO + QK_ROPE] = 1.0
    return jnp.asarray(t)


def _rope_angles(pos, rc):
    ang = pos * rc[0:1, :]
    return jnp.cos(ang), jnp.sin(ang)


def _rope_tables(cosv, sinv, rc):
    return cosv, sinv * rc[1:2, :], sinv * rc[2:3, :]


def _rope(x, ct, s1, s2):
    up = pltpu.roll(x, HEAD_PAD - ROPE_HALF, 1)
    dn = pltpu.roll(x, ROPE_HALF, 1)
    return x * ct + up * s1 + dn * s2


def _const_spec(shape):
    nd = len(shape)
    return pl.BlockSpec(shape, lambda *_: (0,) * nd, pipeline_mode=pl.Buffered(1))


def _acc_spec(shape):
    nd = len(shape)
    return pl.BlockSpec(shape, lambda *_: (0,) * nd)


def _params(n_axes):
    return pltpu.CompilerParams(dimension_semantics=("arbitrary",) * n_axes, vmem_limit_bytes=VMEM_LIMIT)


def _place():
    x, y, c = lax.axis_index("x"), lax.axis_index("y"), lax.axis_index("c")
    return x, y, c, 4 * x + 2 * y + c


def _peer(x, y, c, mask):
    px = 1 - x if mask & 4 else x
    py = 1 - y if mask & 2 else y
    pc = 1 - c if mask & 1 else c
    return (px, py, pc), 4 * px + 2 * py + pc


ANY_SPEC = pl.BlockSpec(memory_space=pl.ANY)


def _comm_sems(n):
    return [pltpu.SemaphoreType.DMA((n, N_DEV - 1)), pltpu.SemaphoreType.DMA((n, N_DEV - 1)), pltpu.SemaphoreType.DMA((n,))]


def _gather_copies(ins, outs, sems):
    send_sems, recv_sems, local_sems = sems
    x, y, c, me = _place()
    starts, waits = [], []
    for w in range(len(ins)):
        mine = pltpu.make_async_copy(ins[w], outs[w].at[me], local_sems.at[w])
        starts.append(mine)
        waits.append(mine)
        for mask in range(1, N_DEV):
            peer, peer_idx = _peer(x, y, c, mask)
            starts.append(pltpu.make_async_remote_copy(
                src_ref=ins[w], dst_ref=outs[w].at[me], send_sem=send_sems.at[w, mask - 1],
                recv_sem=recv_sems.at[w, mask - 1], device_id=peer, device_id_type=MESH))
            waits.append(pltpu.make_async_remote_copy(
                src_ref=ins[w], dst_ref=outs[w].at[peer_idx], send_sem=send_sems.at[w, mask - 1],
                recv_sem=recv_sems.at[w, mask - 1], device_id=peer, device_id_type=MESH))
    return starts, waits


def _scatter_copies(ins, outs, sems):
    send_sems, recv_sems, local_sems = sems
    x, y, c, me = _place()
    copies = []
    for w in range(len(ins)):
        copies.append(pltpu.make_async_copy(ins[w].at[me], outs[w].at[0], local_sems.at[w]))
        for mask in range(1, N_DEV):
            peer, peer_idx = _peer(x, y, c, mask)
            copies.append(pltpu.make_async_remote_copy(
                src_ref=ins[w].at[peer_idx], dst_ref=outs[w].at[mask], send_sem=send_sems.at[w, mask - 1],
                recv_sem=recv_sems.at[w, mask - 1], device_id=peer, device_id_type=MESH))
    return copies, copies


def _stacked(arrays):
    return [jax.ShapeDtypeStruct((N_DEV,) + a.shape, a.dtype) for a in arrays]


def _all_gather(shards):
    n = len(shards)

    def body(*refs):
        ins, outs = refs[:n], refs[n:2 * n]
        send_sems, recv_sems, local_sems = refs[2 * n:]
        x, y, c, me = _place()
        sibling = (x, y, 1 - c)
        chips = [(1 - x, y), (x, 1 - y), (1 - x, 1 - y)]

        def copy(w, k, block, to, src=None):
            idx = 4 * block[0] + 2 * block[1] + block[2]
            return pltpu.make_async_remote_copy(
                src_ref=outs[w].at[idx] if src is None else src, dst_ref=outs[w].at[idx],
                send_sem=send_sems.at[w, k], recv_sem=recv_sems.at[w, k], device_id=to, device_id_type=MESH)

        local, sent = [], []
        for w in range(n):
            mine = pltpu.make_async_copy(ins[w], outs[w].at[me], local_sems.at[w])
            mine.start()
            local.append(mine)
            first = [copy(w, 0, (x, y, c), sibling, src=ins[w])]
            first += [copy(w, 1 + j, (x, y, c), (*chip, c), src=ins[w]) for j, chip in enumerate(chips)]
            for cp in first:
                cp.start()
            sent += first
        for w in range(n):
            for j, chip in enumerate(chips):
                copy(w, 1 + j, (*chip, c), (x, y, c)).wait_recv()
                onward = copy(w, 4 + j, (*chip, c), sibling)
                onward.start()
                sent.append(onward)
        for w in range(n):
            copy(w, 0, sibling, (x, y, c)).wait_recv()
            for j, chip in enumerate(chips):
                copy(w, 4 + j, (*chip, 1 - c), (x, y, c)).wait_recv()
        for cp in sent:
            cp.wait_send()
        for cp in local:
            cp.wait()

    return pl.pallas_call(
        body, name="weight_all_gather", out_shape=_stacked(shards),
        in_specs=[ANY_SPEC] * n, out_specs=[ANY_SPEC] * n, scratch_shapes=_comm_sems(n),
    )(*shards)


def _conv_fwd(x, a_norm, w_in, conv_w, w_out, n_seq, seq, later_shards):
    tm = ROW_TILE
    nt = seq // tm
    n_col = w_in.shape[2]
    n_later = len(later_shards)

    def body(x_ref, an_ref, win_ref, cw_ref, wout_ref, *rest):
        shard_refs, rest = rest[:n_later], rest[n_later:]
        x1_ref, proj_ref, conv_ref = rest[:3]
        stack_refs, rest = rest[3:3 + n_later], rest[3 + n_later:]
        prev_ref, wide_ref, sems = rest[0], rest[1], rest[2:]
        step = pl.program_id(0) * nt + pl.program_id(1)

        @pl.when(step == 0)
        def _():
            for cp in _gather_copies(shard_refs, stack_refs, sems)[0]:
                cp.start()

        @pl.when(pl.program_id(1) == 0)
        def _():
            prev_ref[...] = jnp.zeros_like(prev_ref)

        xv = x_ref[...]
        h = (xv * _rstd(xv) * an_ref[...]).astype(BF16)
        for d in range(N_DEV):
            part = jnp.dot(h, win_ref[d], preferred_element_type=F32)
            wide_ref[:, d * n_col:(d + 1) * n_col] = part
            proj_ref[:, d * n_col:(d + 1) * n_col] = part.astype(SAVED)
        b = wide_ref[:, 0:D_MODEL]
        v = wide_ref[:, D_MODEL:2 * D_MODEL] * wide_ref[:, 2 * D_MODEL:3 * D_MODEL]
        g = wide_ref[:, 3 * D_MODEL:4 * D_MODEL]
        w0, w1, w2 = cw_ref[0:1, :], cw_ref[1:2, :], cw_ref[2:3, :]
        conv_ref[...] = w0 * pltpu.roll(v, 2, 0) + w1 * pltpu.roll(v, 1, 0) + w2 * v
        rows = lax.broadcasted_iota(jnp.int32, (8, D_MODEL), 0)
        p8, v8 = prev_ref[...], v[0:8]
        back1 = jnp.where(rows < 1, pltpu.roll(p8, 1, 0), pltpu.roll(v8, 1, 0))
        back2 = jnp.where(rows < 2, pltpu.roll(p8, 2, 0), pltpu.roll(v8, 2, 0))
        conv_ref[0:8, :] = w0 * back2 + w1 * back1 + w2 * v8
        prev_ref[...] = v[tm - 8:tm]
        silu, _ = _silu_parts(g)
        yv = silu * b * conv_ref[...]
        x1_ref[...] = xv + _dot(yv, wout_ref[...])

        @pl.when(step == n_seq * nt - 1)
        def _():
            for cp in _gather_copies(shard_refs, stack_refs, sems)[1]:
                cp.wait()

    tok = lambda width: pl.BlockSpec((tm, width), lambda s, i: (s * nt + i, 0))
    t = n_seq * seq
    outs = pl.pallas_call(
        body, name="conv_mixer_fwd", grid=(n_seq, nt),
        in_specs=[tok(D_MODEL), _const_spec((1, D_MODEL)), _const_spec(w_in.shape), _const_spec((3, D_MODEL)),
                  _const_spec(w_out.shape)] + [ANY_SPEC] * n_later,
        out_specs=[tok(D_MODEL), tok(4 * D_MODEL), tok(D_MODEL)] + [ANY_SPEC] * n_later,
        out_shape=[jax.ShapeDtypeStruct((t, D_MODEL), F32), jax.ShapeDtypeStruct((t, 4 * D_MODEL), SAVED),
                   jax.ShapeDtypeStruct((t, D_MODEL), F32)] + _stacked(later_shards),
        scratch_shapes=[pltpu.VMEM((8, D_MODEL), F32), pltpu.VMEM((tm, 4 * D_MODEL), F32)] + _comm_sems(n_later),
        compiler_params=_params(2),
    )(x, a_norm, w_in, conv_w, w_out, *later_shards)
    return outs[0], outs[1], outs[2], outs[3:]


def _conv_bwd(dx1, x, proj, conv, a_norm, w_in, conv_w, w_out, n_seq, seq, ready_stacks):
    tm = ROW_TILE
    nt = seq // tm
    n_col = w_in.shape[2]
    n_ready = len(ready_stacks)

    def body(dx1_ref, x_ref, proj_ref, conv_ref, an_ref, win_ref, cw_ref, wout_ref, *rest):
        ready_refs, rest = rest[:n_ready], rest[n_ready:]
        dx_ref, h_ref, dproj_ref, dwout_ref, dan_ref, dcw_ref = rest[:6]
        part_refs, rest = rest[6:6 + n_ready], rest[6 + n_ready:]
        next_ref, d1_ref, d2_ref = rest[:3]
        sems = rest[3:]
        step = pl.program_id(0) * nt + pl.program_id(1)
        first = step == 0

        @pl.when(first)
        def _():
            for cp in _scatter_copies(ready_refs, part_refs, sems)[0]:
                cp.start()
            dwout_ref[...] = jnp.zeros_like(dwout_ref)
            dan_ref[...] = jnp.zeros_like(dan_ref)
            dcw_ref[...] = jnp.zeros_like(dcw_ref)

        @pl.when(pl.program_id(1) == 0)
        def _():
            next_ref[...] = jnp.zeros_like(next_ref)

        dx1v = dx1_ref[...]
        dy = _dot_nt(dx1v, wout_ref[...])
        b = proj_ref[:, 0:D_MODEL].astype(F32)
        cc = proj_ref[:, D_MODEL:2 * D_MODEL].astype(F32)
        u = proj_ref[:, 2 * D_MODEL:3 * D_MODEL].astype(F32)
        g = proj_ref[:, 3 * D_MODEL:4 * D_MODEL].astype(F32)
        cv = conv_ref[...]
        silu, dsilu = _silu_parts(g)
        per_part = D_MODEL // n_col

        def back_through_w_in(part, grad):
            grad = grad.astype(BF16)
            dproj_ref[:, part * D_MODEL:(part + 1) * D_MODEL] = grad
            term = None
            for j in range(per_part):
                d = part * per_part + j
                piece = lax.dot_general(grad[:, j * n_col:(j + 1) * n_col], win_ref[d], NT, preferred_element_type=F32)
                term = piece if term is None else term + piece
            return term

        dh = back_through_w_in(0, dy * silu * cv)
        dh += back_through_w_in(3, dy * b * cv * dsilu)
        dwout_ref[...] += _dot_tn(silu * b * cv, dx1v)
        dconv = dy * silu * b
        d1_ref[...] = pltpu.roll(dconv, tm - 1, 0)
        d2_ref[...] = pltpu.roll(dconv, tm - 2, 0)
        rows = lax.broadcasted_iota(jnp.int32, (8, D_MODEL), 0)
        n8, c8 = next_ref[...], dconv[tm - 8:tm]
        d1_ref[tm - 8:tm, :] = jnp.where(rows >= 7, pltpu.roll(n8, 7, 0), pltpu.roll(c8, 7, 0))
        d2_ref[tm - 8:tm, :] = jnp.where(rows >= 6, pltpu.roll(n8, 6, 0), pltpu.roll(c8, 6, 0))
        next_ref[...] = dconv[0:8]
        d1, d2 = d1_ref[...], d2_ref[...]
        v = cc * u
        dcw_ref[0:1, :] += jnp.sum(d2 * v, axis=0, keepdims=True)
        dcw_ref[1:2, :] += jnp.sum(d1 * v, axis=0, keepdims=True)
        dcw_ref[2:3, :] += jnp.sum(dconv * v, axis=0, keepdims=True)
        dv = cw_ref[0:1, :] * d2 + cw_ref[1:2, :] * d1 + cw_ref[2:3, :] * dconv
        dh += back_through_w_in(1, dv * u)
        dh += back_through_w_in(2, dv * cc)
        xv = x_ref[...]
        r = _rstd(xv)
        xh = xv * r
        h_ref[...] = (xh * an_ref[...]).T.astype(BF16)
        dan_ref[...] += jnp.sum(dh * xh, axis=0, keepdims=True)
        dx_ref[...] = dx1v + _norm_bwd(dh * an_ref[...], xh, r)

        @pl.when(step == n_seq * nt - 1)
        def _():
            for cp in _scatter_copies(ready_refs, part_refs, sems)[1]:
                cp.wait()

    tok = lambda width: pl.BlockSpec((tm, width), lambda s, i: (s * nt + nt - 1 - i, 0))
    t = n_seq * seq
    outs = pl.pallas_call(
        body, name="conv_mixer_bwd", grid=(n_seq, nt),
        in_specs=[tok(D_MODEL), tok(D_MODEL), tok(4 * D_MODEL), tok(D_MODEL), _const_spec((1, D_MODEL)),
                  _const_spec(w_in.shape), _const_spec((3, D_MODEL)), _const_spec(w_out.shape)] + [ANY_SPEC] * n_ready,
        out_specs=[tok(D_MODEL), pl.BlockSpec((D_MODEL, tm), lambda s, i: (0, s * nt + nt - 1 - i)),
                   tok(4 * D_MODEL), _acc_spec((D_MODEL, D_MODEL)),
                   _acc_spec((1, D_MODEL)), _acc_spec((8, D_MODEL))] + [ANY_SPEC] * n_ready,
        out_shape=[jax.ShapeDtypeStruct((t, D_MODEL), F32), jax.ShapeDtypeStruct((D_MODEL, t), BF16),
                   jax.ShapeDtypeStruct((t, 4 * D_MODEL), BF16), jax.ShapeDtypeStruct((D_MODEL, D_MODEL), F32),
                   jax.ShapeDtypeStruct((1, D_MODEL), F32), jax.ShapeDtypeStruct((8, D_MODEL), F32)]
        + [jax.ShapeDtypeStruct(a.shape, a.dtype) for a in ready_stacks],
        scratch_shapes=[pltpu.VMEM((8, D_MODEL), F32), pltpu.VMEM((tm, D_MODEL), F32), pltpu.VMEM((tm, D_MODEL), F32)]
        + _comm_sems(n_ready),
        compiler_params=_params(2),
    )(dx1, x, proj, conv, a_norm, w_in, conv_w, w_out, *ready_stacks)
    return outs[:6], outs[6:]


def _w_in_grad_exchange(a_t, b, n_col, chip, ready_stacks, small):
    r, t = a_t.shape
    bt = 1024
    nk = t // bt
    n_ready = len(ready_stacks)
    n_chip = N_DEV // 2
    n_remote = N_DEV - 2

    def body(chip_ref, a_ref, b_ref, *rest):
        ready_refs, small_ref, rest = rest[:n_ready], rest[n_ready], rest[n_ready + 1:]
        own_ref, got_ref, rest = rest[0], rest[1], rest[2:]
        part_refs, small_all_ref, rest = rest[:n_ready], rest[n_ready], rest[n_ready + 1:]
        acc_ref, land_ref, stage_ref, pair_send, pair_recv, chip_send, chip_recv = rest[:7]
        scatter_sems, gather_sems = rest[7:10], rest[10:13]
        s, k = pl.program_id(0), pl.program_id(1)
        x, y, c, _ = _place()
        chip = 2 * x + y

        @pl.when(jnp.logical_and(s == 0, k == 0))
        def _():
            for cp in _scatter_copies(ready_refs, part_refs, scatter_sems)[0]:
                cp.start()
            for cp in _gather_copies([small_ref], [small_all_ref], gather_sems)[0]:
                cp.start()

        for parity in range(2):
            @pl.when(s % 2 == parity)
            def _(parity=parity):
                @pl.when(k == 0)
                def _():
                    acc_ref[parity] = jnp.zeros((r, n_col), F32)

                acc_ref[parity] += jnp.dot(a_ref[...], b_ref[...], preferred_element_type=F32)

        def to_sibling(step):
            return pltpu.make_async_remote_copy(
                src_ref=acc_ref.at[step % 2], dst_ref=land_ref, send_sem=pair_send.at[step], recv_sem=pair_recv.at[step],
                device_id=(x, y, 1 - c), device_id_type=MESH)

        def to_owner(nth):
            owner_chip = (chip + 1 + nth) % n_chip
            slot = jnp.bitwise_xor(chip, owner_chip) - 1
            return pltpu.make_async_remote_copy(
                src_ref=stage_ref.at[nth % 2], dst_ref=got_ref.at[slot], send_sem=chip_send.at[nth],
                recv_sem=chip_recv.at[slot], device_id=(owner_chip // 2, owner_chip % 2, c), device_id_type=MESH)

        for step in range(N_DEV):
            owner_core = step % 2

            @pl.when(jnp.logical_and(s == step, k == nk - 1))
            def _(step=step, owner_core=owner_core):
                @pl.when(c != owner_core)
                def _():
                    to_sibling(step).start()

                if step >= 1:
                    @pl.when(c == owner_core)
                    def _():
                        to_sibling(step - 1).wait_send()

            if step < N_DEV - 1:
                pickup = jnp.logical_and(s == step + 1, k == min(PICKUP_BLOCK, nk - 1))
            else:
                pickup = jnp.logical_and(s == step, k == nk - 1)

            @pl.when(jnp.logical_and(pickup, c == owner_core))
            def _(step=step):
                to_sibling(step).wait_recv()
                total = acc_ref[step % 2] + land_ref[...]
                if step < n_remote:
                    nth = step // 2
                    if nth >= 2:
                        to_owner(nth - 2).wait_send()
                    stage_ref[nth % 2] = total.astype(BF16)
                    to_owner(nth).start()
                else:
                    own_ref[...] = total

        @pl.when(jnp.logical_and(s == N_DEV - 1, k == nk - 1))
        def _():
            @pl.when(c != (N_DEV - 1) % 2)
            def _():
                to_sibling(N_DEV - 1).wait_send()

            to_owner(1).wait_send()
            to_owner(2).wait_send()
            for slot in range(n_chip - 1):
                pltpu.make_async_remote_copy(
                    src_ref=stage_ref.at[0], dst_ref=got_ref.at[slot], send_sem=chip_send.at[0],
                    recv_sem=chip_recv.at[slot], device_id=(x, y, c), device_id_type=MESH).wait_recv()
            for cp in _scatter_copies(ready_refs, part_refs, scatter_sems)[1]:
                cp.wait()
            for cp in _gather_copies([small_ref], [small_all_ref], gather_sems)[1]:
                cp.wait()

    def owner_block(s, k, chip):
        return (k, 2 * ((chip[0] + 1 + s // 2) % n_chip) + s % 2)

    grid_spec = pltpu.PrefetchScalarGridSpec(
        num_scalar_prefetch=1, grid=(N_DEV, nk),
        in_specs=[pl.BlockSpec((r, bt), lambda s, k, chip: (0, k)), pl.BlockSpec((bt, n_col), owner_block)]
        + [ANY_SPEC] * (n_ready + 1),
        out_specs=[pl.BlockSpec((r, n_col), lambda s, k, chip: (0, 0)), ANY_SPEC] + [ANY_SPEC] * (n_ready + 1),
        scratch_shapes=[pltpu.VMEM((2, r, n_col), F32), pltpu.VMEM((r, n_col), F32), pltpu.VMEM((2, r, n_col), BF16),
                        pltpu.SemaphoreType.DMA((N_DEV,)), pltpu.SemaphoreType.DMA((N_DEV,)),
                        pltpu.SemaphoreType.DMA((n_chip - 1,)), pltpu.SemaphoreType.DMA((n_chip - 1,))]
        + _comm_sems(n_ready) + _comm_sems(1))
    outs = pl.pallas_call(
        body, name="w_in_grad_exchange", grid_spec=grid_spec,
        out_shape=[jax.ShapeDtypeStruct((r, n_col), F32), jax.ShapeDtypeStruct((n_chip - 1, r, n_col), BF16)]
        + [jax.ShapeDtypeStruct(p.shape, p.dtype) for p in ready_stacks] + _stacked([small]),
        compiler_params=_params(2),
    )(chip, a_t, b, *ready_stacks, small)
    return outs[0], outs[1], outs[2:2 + n_ready], outs[2 + n_ready]


def _proj_fwd(x1, pos, rc, kv_norm, w_dkv, ckv_norm, w_ukv, b_norm, b_w_in, q_norm, w_uq):
    t = x1.shape[0]
    tm = LATENT_ROW_TILE

    def body(x1_ref, pos_ref, rc_ref, kvn_ref, wdkv_ref, ckvn_ref, wukv_ref, bn_ref, bwin_ref, qn_ref, wuq_ref,
             ckr_ref, cq_ref, gb_ref, q_ref, k_ref, v_ref, cos_ref, sin_ref):
        xv = x1_ref[...]
        xh = xv * _rstd(xv)
        ckr = _dot(xh * kvn_ref[...], wdkv_ref[...])
        pb = _dot(xh * bn_ref[...], bwin_ref[...])
        ckr_ref[...] = ckr
        ckv = ckr[:, 0:KV_RANK]
        kv = _dot(ckv * _rstd(ckv) * ckvn_ref[...], wukv_ref[...])
        cosv, sinv = _rope_angles(pos_ref[...], rc_ref[...])
        cos_ref[...] = cosv
        sin_ref[...] = sinv
        ct, s1, s2 = _rope_tables(cosv, sinv, rc_ref[...])
        k_rope = _rope(ckr[:, KV_RANK:CKR_PAD], ct, s1, s2)
        lane = lax.broadcasted_iota(jnp.int32, (tm, HEAD_PAD), 1)
        low = lane < QK_NOPE
        for h in range(N_HEADS):
            kv_h = kv[:, h * HEAD_PAD:(h + 1) * HEAD_PAD]
            k_ref[:, h * HEAD_PAD:(h + 1) * HEAD_PAD] = jnp.where(low, kv_h, k_rope).astype(BF16)
            v_ref[:, h * HEAD_PAD:(h + 1) * HEAD_PAD] = jnp.where(low, pltpu.roll(kv_h, V_HEAD, 1), 1.0).astype(BF16)
        cq = pb[:, 0:Q_RANK]
        cq_ref[...] = cq
        gb_ref[...] = pb[:, Q_RANK:Q_RANK + E_B]
        q = _dot(cq * _rstd(cq) * qn_ref[...], wuq_ref[...])
        for h in range(N_HEADS):
            q_ref[:, h * HEAD_PAD:(h + 1) * HEAD_PAD] = _rope(
                q[:, h * HEAD_PAD:(h + 1) * HEAD_PAD], ct, s1, s2).astype(BF16)

    tok = lambda width: pl.BlockSpec((tm, width), lambda i: (i, 0))
    weights = [kv_norm, w_dkv, ckv_norm, w_ukv, b_norm, b_w_in, q_norm, w_uq]
    wide = N_HEADS * HEAD_PAD
    return pl.pallas_call(
        body, name="latent_proj_fwd", grid=(t // tm,),
        in_specs=[tok(D_MODEL), tok(1), _const_spec(rc.shape)] + [_const_spec(w.shape) for w in weights],
        out_specs=[tok(CKR_PAD), tok(Q_RANK), tok(E_B), tok(wide), tok(wide), tok(wide), tok(HEAD_PAD), tok(HEAD_PAD)],
        out_shape=[jax.ShapeDtypeStruct((t, CKR_PAD), F32), jax.ShapeDtypeStruct((t, Q_RANK), F32),
                   jax.ShapeDtypeStruct((t, E_B), F32), jax.ShapeDtypeStruct((t, wide), BF16),
                   jax.ShapeDtypeStruct((t, wide), BF16), jax.ShapeDtypeStruct((t, wide), BF16),
                   jax.ShapeDtypeStruct((t, HEAD_PAD), F32), jax.ShapeDtypeStruct((t, HEAD_PAD), F32)],
        compiler_params=_params(1),
    )(x1, pos, rc, *weights)


def _proj_bwd(dq, dk, dv, dgb, cq, ckr, x1, dx2, cosv, sinv, rc, kv_norm, w_dkv, ckv_norm, w_ukv, b_norm, b_w_in, q_norm, w_uq):
    t = x1.shape[0]
    tm = LATENT_ROW_TILE
    wide = N_HEADS * HEAD_PAD

    def body(dq_ref, dk_ref, dv_ref, dgb_ref, cq_ref, ckr_ref, x1_ref, dx2_ref, cos_ref, sin_ref, rc_ref,
             kvn_ref, wdkv_ref, ckvn_ref, wukv_ref, bn_ref, bwin_ref, qn_ref, wuq_ref,
             dx1_ref, dwuq_ref, dwbin_ref, dwukv_ref, dwdkv_ref, dqn_ref, dbn_ref, dckvn_ref, dkvn_ref, dqu_ref, dkv_ref):
        @pl.when(pl.program_id(0) == 0)
        def _():
            for ref in (dwuq_ref, dwbin_ref, dwukv_ref, dwdkv_ref, dqn_ref, dbn_ref, dckvn_ref, dkvn_ref):
                ref[...] = jnp.zeros_like(ref)

        ct, s1, s2 = _rope_tables(cos_ref[...], sin_ref[...], rc_ref[...])
        lane = lax.broadcasted_iota(jnp.int32, (tm, HEAD_PAD), 1)
        low = lane < QK_NOPE
        xv = x1_ref[...]
        r = _rstd(xv)
        xh = xv * r
        dk_rope = jnp.zeros((tm, HEAD_PAD), F32)
        for h in range(N_HEADS):
            dk_h = dk_ref[:, h * HEAD_PAD:(h + 1) * HEAD_PAD].astype(F32)
            dv_h = dv_ref[:, h * HEAD_PAD:(h + 1) * HEAD_PAD].astype(F32)
            dkv_ref[:, h * HEAD_PAD:(h + 1) * HEAD_PAD] = jnp.where(low, dk_h, pltpu.roll(dv_h, V_HEAD, 1)).astype(BF16)
            dk_rope += dk_h
        rope_lanes = jnp.logical_and(lane >= ROPE_LO, lane < ROPE_LO + QK_ROPE)
        dk_rope = jnp.where(rope_lanes, _rope(dk_rope, ct, -s1, -s2), 0.0)
        ckv = ckr_ref[:, 0:KV_RANK]
        rk = _rstd(ckv)
        ckh = ckv * rk
        ckvn = (ckh * ckvn_ref[...]).astype(BF16)
        dwukv = lax.dot_general(ckvn, dkv_ref[...], TN, preferred_element_type=F32)
        for h in range(N_HEADS):
            dwukv_ref[h] += dwukv[:, h * HEAD_PAD:(h + 1) * HEAD_PAD]
        dckvn = lax.dot_general(dkv_ref[...], wukv_ref[...], NT, preferred_element_type=F32)
        dckvn_ref[...] += jnp.sum(dckvn * ckh, axis=0, keepdims=True)
        dckv = _norm_bwd(dckvn * ckvn_ref[...], ckh, rk)
        dckr = jnp.concatenate([dckv, dk_rope], axis=1).astype(BF16)
        dwdkv_ref[...] += _dot_tn(xh * kvn_ref[...], dckr)
        dh2 = lax.dot_general(dckr, wdkv_ref[...], NT, preferred_element_type=F32)
        for h in range(N_HEADS):
            dqu_ref[:, h * HEAD_PAD:(h + 1) * HEAD_PAD] = _rope(
                dq_ref[:, h * HEAD_PAD:(h + 1) * HEAD_PAD].astype(F32), ct, -s1, -s2).astype(BF16)
        cq = cq_ref[...]
        rq = _rstd(cq)
        cqh = cq * rq
        cqn = (cqh * qn_ref[...]).astype(BF16)
        dwuq = lax.dot_general(cqn, dqu_ref[...], TN, preferred_element_type=F32)
        for h in range(N_HEADS):
            dwuq_ref[h] += dwuq[:, h * HEAD_PAD:(h + 1) * HEAD_PAD]
        dcqn = lax.dot_general(dqu_ref[...], wuq_ref[...], NT, preferred_element_type=F32)
        dqn_ref[...] += jnp.sum(dcqn * cqh, axis=0, keepdims=True)
        dcq = _norm_bwd(dcqn * qn_ref[...], cqh, rq)
        dpb = jnp.concatenate([dcq, dgb_ref[...]], axis=1).astype(BF16)
        dwbin_ref[...] += _dot_tn(xh * bn_ref[...], dpb)
        dh3 = lax.dot_general(dpb, bwin_ref[...], NT, preferred_element_type=F32)
        dkvn_ref[...] += jnp.sum(dh2 * xh, axis=0, keepdims=True)
        dbn_ref[...] += jnp.sum(dh3 * xh, axis=0, keepdims=True)
        dx1_ref[...] = dx2_ref[...] + _norm_bwd(dh2 * kvn_ref[...] + dh3 * bn_ref[...], xh, r)

    tok = lambda width: pl.BlockSpec((tm, width), lambda i: (i, 0))
    weights = [kv_norm, w_dkv, ckv_norm, w_ukv, b_norm, b_w_in, q_norm, w_uq]
    acc_shapes = [(N_HEADS, Q_RANK, HEAD_PAD), (D_MODEL, Q_RANK + E_B), (N_HEADS, KV_RANK, HEAD_PAD), (D_MODEL, CKR_PAD),
                  (1, Q_RANK), (1, D_MODEL), (1, KV_RANK), (1, D_MODEL)]
    return pl.pallas_call(
        body, name="latent_proj_bwd", grid=(t // tm,),
        in_specs=[tok(wide), tok(wide), tok(wide), tok(E_B), tok(Q_RANK), tok(CKR_PAD), tok(D_MODEL), tok(D_MODEL),
                  tok(HEAD_PAD), tok(HEAD_PAD), _const_spec(rc.shape)] + [_const_spec(w.shape) for w in weights],
        out_specs=[tok(D_MODEL)] + [_acc_spec(s) for s in acc_shapes],
        out_shape=[jax.ShapeDtypeStruct((t, D_MODEL), F32)] + [jax.ShapeDtypeStruct(s, F32) for s in acc_shapes],
        scratch_shapes=[pltpu.VMEM((tm, wide), BF16), pltpu.VMEM((tm, wide), BF16)],
        compiler_params=_params(1),
    )(dq, dk, dv, dgb, cq, ckr, x1, dx2, cosv, sinv, rc, *weights)


def _attn_fwd(q, k, v, n_seq, seq):
    tb, group = ATTN_TILE, ATTN_GROUP
    rows = tb * group
    nq = seq // rows
    pair = 2 * HEAD_PAD

    def body(q_ref, k_ref, v_ref, o_ref, lr_ref):
        i = pl.program_id(2)
        row = lax.broadcasted_iota(jnp.int32, (tb, tb), 0)
        col = lax.broadcasted_iota(jnp.int32, (tb, tb), 1)
        causal = col <= row

        def step(j, carry, kinds):
            start = pl.multiple_of(j * tb, tb)
            out = []
            for g in range(group):
                for hh in range(2):
                    m, acc = carry[2 * g + hh]
                    if kinds[g] != "none":
                        heads = slice(hh * HEAD_PAD, (hh + 1) * HEAD_PAD)
                        s = lax.dot_general(q_ref[g * tb:(g + 1) * tb, heads], k_ref[pl.ds(start, tb), heads], NT,
                                            preferred_element_type=F32) * (SOFTMAX_SCALE * LOG2_E)
                        if kinds[g] == "diagonal":
                            s = jnp.where(causal, s, -jnp.inf)
                        m_new = jnp.maximum(m, jnp.max(s, axis=-1, keepdims=True))
                        p = jnp.exp2(s - m_new)
                        acc = jnp.exp2(m - m_new) * acc + jnp.dot(
                            p.astype(BF16), v_ref[pl.ds(start, tb), heads], preferred_element_type=F32)
                        m = m_new
                    out.append((m, acc))
            return tuple(out)

        one = (jnp.full((tb, 1), -jnp.inf, F32), jnp.zeros((tb, HEAD_PAD), F32))
        carry = lax.fori_loop(0, i * group, functools.partial(step, kinds=("full",) * group), (one,) * (2 * group))
        for d in range(group):
            kinds = tuple("none" if g < d else ("diagonal" if g == d else "full") for g in range(group))
            carry = step(i * group + d, carry, kinds)
        lane = lax.broadcasted_iota(jnp.int32, (tb, HEAD_PAD), 1)
        low = lane < V_HEAD
        for g in range(group):
            halves = []
            for hh in range(2):
                m, acc = carry[2 * g + hh]
                swapped = pltpu.roll(acc, V_HEAD, 1)
                halves.append(acc / swapped)
                lse = (m * LN_2) + jnp.log(jnp.where(low, swapped, acc))
                lr_ref[hh, :, g * tb:(g + 1) * tb] = lse.T[0:8, :]
            o_ref[g * tb:(g + 1) * tb, :] = jnp.where(low, halves[0], pltpu.roll(halves[1], V_HEAD, 1))

    t = n_seq * seq
    return pl.pallas_call(
        body, name="attention_fwd", grid=(n_seq, N_HEADS // 2, nq),
        in_specs=[pl.BlockSpec((rows, pair), lambda s, p, i: (s * nq + i, p)),
                  pl.BlockSpec((seq, pair), lambda s, p, i: (s, p)),
                  pl.BlockSpec((seq, pair), lambda s, p, i: (s, p))],
        out_specs=[pl.BlockSpec((rows, HEAD_PAD), lambda s, p, i: (s * nq + i, p)),
                   pl.BlockSpec((None, 2, 8, rows), lambda s, p, i: (s, p, 0, i))],
        out_shape=[jax.ShapeDtypeStruct((t, E_B), F32), jax.ShapeDtypeStruct((n_seq, N_HEADS, 8, seq), F32)],
        compiler_params=_params(3),
    )(q, k, v)


def _attn_bwd(q, k, v, do, lse_row, delta_row, n_seq, seq):
    tb, group = ATTN_TILE, ATTN_BWD_GROUP
    rows = tb * group
    nk = seq // rows
    n_inner = seq // tb
    pair = 2 * HEAD_PAD

    def body(q_ref, k_ref, v_ref, do_ref, lr_ref, dr_ref, dk_ref, dv_ref, dq_out_ref, dq_ref):
        j = pl.program_id(2)

        @pl.when(j == 0)
        def _():
            dq_ref[...] = jnp.zeros_like(dq_ref)

        row = lax.broadcasted_iota(jnp.int32, (tb, tb), 0)
        col = lax.broadcasted_iota(jnp.int32, (tb, tb), 1)
        causal = col >= row

        def step(i, carry, kinds):
            start = pl.multiple_of(i * tb, tb)
            out = []
            for hh in range(2):
                heads = slice(hh * HEAD_PAD, (hh + 1) * HEAD_PAD)
                qi = q_ref[pl.ds(start, tb), heads]
                doi = do_ref[pl.ds(start, tb), heads]
                dq = None
                for g in range(group):
                    dk_acc, dv_acc = carry[hh * group + g]
                    if kinds[g] != "none":
                        kg = k_ref[g * tb:(g + 1) * tb, heads]
                        st = lax.dot_general(kg, qi, NT, preferred_element_type=F32) * SOFTMAX_SCALE
                        pt = jnp.exp(st - lr_ref[hh, 0:1, pl.ds(start, tb)])
                        if kinds[g] == "diagonal":
                            pt = jnp.where(causal, pt, 0.0)
                        dv_acc = dv_acc + jnp.dot(pt.astype(BF16), doi, preferred_element_type=F32)
                        dpt = lax.dot_general(v_ref[g * tb:(g + 1) * tb, heads], doi, NT, preferred_element_type=F32)
                        dst = (pt * (dpt - dr_ref[hh, 0:1, pl.ds(start, tb)]) * SOFTMAX_SCALE).astype(BF16)
                        dk_acc = dk_acc + jnp.dot(dst, qi, preferred_element_type=F32)
                        term = lax.dot_general(dst, kg, TN, preferred_element_type=F32)
                        dq = term if dq is None else dq + term
                    out.append((dk_acc, dv_acc))
                dq_ref[pl.ds(start, tb), heads] += dq
            return tuple(out)

        one = (jnp.zeros((tb, HEAD_PAD), F32), jnp.zeros((tb, HEAD_PAD), F32))
        carry = (one,) * (2 * group)
        for d in range(group):
            kinds = tuple("full" if g < d else ("diagonal" if g == d else "none") for g in range(group))
            carry = step(j * group + d, carry, kinds)
        carry = lax.fori_loop((j + 1) * group, n_inner, functools.partial(step, kinds=("full",) * group), carry)
        for hh in range(2):
            for g in range(group):
                dk_ref[g * tb:(g + 1) * tb, hh * HEAD_PAD:(hh + 1) * HEAD_PAD] = carry[hh * group + g][0].astype(BF16)
                dv_ref[g * tb:(g + 1) * tb, hh * HEAD_PAD:(hh + 1) * HEAD_PAD] = carry[hh * group + g][1].astype(BF16)

        @pl.when(j == nk - 1)
        def _():
            dq_out_ref[...] = dq_ref[...].astype(BF16)

    t = n_seq * seq
    wide = N_HEADS * HEAD_PAD
    return pl.pallas_call(
        body, name="attention_bwd", grid=(n_seq, N_HEADS // 2, nk),
        in_specs=[pl.BlockSpec((seq, pair), lambda s, p, j: (s, p)),
                  pl.BlockSpec((rows, pair), lambda s, p, j: (s * nk + j, p)),
                  pl.BlockSpec((rows, pair), lambda s, p, j: (s * nk + j, p)),
                  pl.BlockSpec((seq, pair), lambda s, p, j: (s, p)),
                  pl.BlockSpec((None, 2, 8, seq), lambda s, p, j: (s, p, 0, 0)),
                  pl.BlockSpec((None, 2, 8, seq), lambda s, p, j: (s, p, 0, 0))],
        out_specs=[pl.BlockSpec((rows, pair), lambda s, p, j: (s * nk + j, p)),
                   pl.BlockSpec((rows, pair), lambda s, p, j: (s * nk + j, p)),
                   pl.BlockSpec((seq, pair), lambda s, p, j: (s, p))],
        out_shape=[jax.ShapeDtypeStruct((t, wide), BF16), jax.ShapeDtypeStruct((t, wide), BF16),
                   jax.ShapeDtypeStruct((t, wide), BF16)],
        scratch_shapes=[pltpu.VMEM((seq, pair), F32)],
        compiler_params=_params(3),
    )(q, k, v, do, lse_row, delta_row)


def _head_tail(o, gb, x1, w_out, final_norm, target, n_seq, seq):
    tm = LATENT_ROW_TILE
    nt = seq // tm
    n_col = D_MODEL // N_DEV

    def body(o_ref, gb_ref, x1_ref, wout_ref, fn_ref, tgt_ref,
             dx2_ref, do_ref, dgb_ref, dr_ref, dwout_ref, dfn_ref, loss_ref):
        first = jnp.logical_and(pl.program_id(0) == 0, pl.program_id(1) == 0)

        @pl.when(first)
        def _():
            dwout_ref[...] = jnp.zeros_like(dwout_ref)
            dfn_ref[...] = jnp.zeros_like(dfn_ref)
            loss_ref[...] = jnp.zeros_like(loss_ref)

        ov, g = o_ref[...], gb_ref[...]
        silu, dsilu = _silu_parts(g)
        gated = (ov * silu).astype(BF16)
        x2 = x1_ref[...] + jnp.dot(gated, wout_ref[...], preferred_element_type=F32)
        r = _rstd(x2)
        xh = x2 * r
        err = xh * fn_ref[...] - tgt_ref[...]
        loss_ref[...] += 0.5 * jnp.sum(jnp.mean(err * err, axis=-1, keepdims=True), axis=0, keepdims=True)
        dy = err / D_MODEL
        dfn_ref[...] += jnp.sum(dy * xh, axis=0, keepdims=True)
        dx2 = _norm_bwd(dy * fn_ref[...], xh, r)
        dx2_ref[...] = dx2
        dx2b = dx2.astype(BF16)
        dw = lax.dot_general(gated, dx2b, TN, preferred_element_type=F32)
        for d in range(N_DEV):
            dwout_ref[d] += dw[:, d * n_col:(d + 1) * n_col]
        dgated = lax.dot_general(dx2b, wout_ref[...], NT, preferred_element_type=F32)
        do = dgated * silu
        dgb_ref[...] = dgated * ov * dsilu
        prod = do * ov
        lane = lax.broadcasted_iota(jnp.int32, (tm, HEAD_PAD), 1)
        low = lane < V_HEAD
        for p in range(N_HEADS // 2):
            blk = prod[:, p * HEAD_PAD:(p + 1) * HEAD_PAD]
            do_pair = do[:, p * HEAD_PAD:(p + 1) * HEAD_PAD]
            for hh in range(2):
                h = 2 * p + hh
                mine = do_pair if hh == 0 else pltpu.roll(do_pair, V_HEAD, 1)
                do_ref[:, h * HEAD_PAD:(h + 1) * HEAD_PAD] = jnp.where(low, mine, 0.0).astype(BF16)
                delta = jnp.sum(jnp.where(low if hh == 0 else ~low, blk, 0.0), axis=-1, keepdims=True)
                dr_ref[h] = jnp.broadcast_to(delta, (tm, HEAD_PAD)).T[0:8, :]

    tok = lambda width: pl.BlockSpec((tm, width), lambda s, i: (s * nt + i, 0))
    t = n_seq * seq
    return pl.pallas_call(
        body, name="head_tail", grid=(n_seq, nt),
        in_specs=[tok(E_B), tok(E_B), tok(D_MODEL), _const_spec(w_out.shape), _const_spec((1, D_MODEL)), tok(D_MODEL)],
        out_specs=[tok(D_MODEL), tok(N_HEADS * HEAD_PAD), tok(E_B),
                   pl.BlockSpec((None, N_HEADS, 8, tm), lambda s, i: (s, 0, 0, i)),
                   _acc_spec((N_DEV, E_B, n_col)), _acc_spec((1, D_MODEL)), _acc_spec((1, 1))],
        out_shape=[jax.ShapeDtypeStruct((t, D_MODEL), F32), jax.ShapeDtypeStruct((t, N_HEADS * HEAD_PAD), BF16),
                   jax.ShapeDtypeStruct((t, E_B), F32),
                   jax.ShapeDtypeStruct((n_seq, N_HEADS, 8, seq), F32), jax.ShapeDtypeStruct((N_DEV, E_B, n_col), F32),
                   jax.ShapeDtypeStruct((1, D_MODEL), F32), jax.ShapeDtypeStruct((1, 1), F32)],
        compiler_params=_params(2),
    )(o, gb, x1, w_out, final_norm, target)


def _adamw_math(w, g, m, v):
    m = ADAM_B1 * m + (1.0 - ADAM_B1) * g
    v = ADAM_B2 * v + (1.0 - ADAM_B2) * jnp.square(g)
    m_hat = m / (1.0 - ADAM_B1 ** ADAM_STEP)
    v_hat = v / (1.0 - ADAM_B2 ** ADAM_STEP)
    delta = -ADAM_LR * (m_hat / (jnp.sqrt(v_hat) + ADAM_EPS) + ADAM_WD * w)
    return delta, m, v


def _adamw_reduce(own, parts, w, m, v):
    rows, cols = w.shape
    br = 256
    n_parts = parts.shape[0]

    def body(own_ref, p_ref, w_ref, m_ref, v_ref, g_ref, d_ref, nm_ref, nv_ref):
        g = own_ref[...]
        for k in range(n_parts):
            g = g + p_ref[k].astype(F32)
        g_ref[...] = g
        d_ref[...], nm_ref[...], nv_ref[...] = _adamw_math(w_ref[...], g, m_ref[...], v_ref[...])

    blk = pl.BlockSpec((br, cols), lambda i: (i, 0))
    return pl.pallas_call(
        body, name="adamw_reduce", grid=(rows // br,),
        in_specs=[blk, pl.BlockSpec((n_parts, br, cols), lambda i: (0, i, 0)), blk, blk, blk],
        out_specs=[blk] * 4, out_shape=[jax.ShapeDtypeStruct((rows, cols), F32)] * 4,
        compiler_params=_params(1),
    )(own, parts, w, m, v)


def _adamw_reduce_many(parts_list, states):
    n = len(parts_list)

    def body(*refs):
        ins, outs = refs[:4 * n], refs[4 * n:]
        for j in range(n):
            p_ref, w_ref, m_ref, v_ref = ins[4 * j:4 * j + 4]
            g = p_ref[0]
            for k in range(1, p_ref.shape[0]):
                g = g + p_ref[k]
            outs[4 * j][...] = g
            outs[4 * j + 1][...], outs[4 * j + 2][...], outs[4 * j + 3][...] = _adamw_math(
                w_ref[...], g, m_ref[...], v_ref[...])

    flat_in = [a for p, st in zip(parts_list, states) for a in (p,) + tuple(st)]
    shapes = [jax.ShapeDtypeStruct(st[0].shape, F32) for st in states for _ in range(4)]
    flat = pl.pallas_call(body, name="adamw_reduce_many", out_shape=shapes,
                          compiler_params=pltpu.CompilerParams(vmem_limit_bytes=VMEM_LIMIT))(*flat_in)
    return [tuple(flat[4 * j:4 * j + 4]) for j in range(n)]


def _sum_parts(parts):
    def body(p_ref, o_ref):
        g = p_ref[0]
        for k in range(1, N_DEV):
            g = g + p_ref[k]
        o_ref[...] = g

    return pl.pallas_call(body, name="small_grad_sum", out_shape=jax.ShapeDtypeStruct(parts.shape[1:], F32))(parts)


def _adamw_small(gs, ws, ms, vs):
    n = len(gs)

    def body(*refs):
        ins, outs = refs[:4 * n], refs[4 * n:]
        for j in range(n):
            g_ref, w_ref, m_ref, v_ref = ins[j], ins[n + j], ins[2 * n + j], ins[3 * n + j]
            outs[3 * j][...], outs[3 * j + 1][...], outs[3 * j + 2][...] = _adamw_math(
                w_ref[...], g_ref[...], m_ref[...], v_ref[...])

    shapes = [jax.ShapeDtypeStruct(g.shape, F32) for g in gs for _ in range(3)]
    flat = pl.pallas_call(body, name="adamw_small", out_shape=shapes)(*gs, *ws, *ms, *vs)
    return [tuple(flat[3 * j:3 * j + 3]) for j in range(n)]


SMALL_ROWS = 8
LOSS_LANE = D_MODEL - 1


def _pack_small(g_fn, g_kvn, g_bn, g_an, g_ckvn, g_qn, g_conv, loss_part):
    def body(fn_ref, kvn_ref, bn_ref, an_ref, ckvn_ref, qn_ref, conv_ref, loss_ref, o_ref):
        o_ref[0:1, :] = fn_ref[...]
        o_ref[1:2, :] = kvn_ref[...]
        o_ref[2:3, :] = bn_ref[...]
        o_ref[3:4, :] = an_ref[...]
        lane = lax.broadcasted_iota(jnp.int32, (1, D_MODEL), 1)
        o_ref[4:5, :] = jnp.where(lane == LOSS_LANE, loss_ref[...], 0.0)
        o_ref[4:5, 0:KV_RANK] = ckvn_ref[...]
        o_ref[4:5, KV_RANK:KV_RANK + Q_RANK] = qn_ref[...]
        o_ref[5:8, :] = conv_ref[0:3, :]

    return pl.pallas_call(body, name="pack_small", out_shape=jax.ShapeDtypeStruct((SMALL_ROWS, D_MODEL), F32))(
        g_fn, g_kvn, g_bn, g_an, g_ckvn, g_qn, g_conv, loss_part)


def _pad_cols(a, width):
    return jnp.pad(a, ((0, 0), (0, width - a.shape[1])))


def _dkv_to_padded(a):
    r = a.shape[0]
    z = jnp.zeros((r, ROPE_LO), a.dtype)
    z2 = jnp.zeros((r, HEAD_PAD - ROPE_LO - QK_ROPE), a.dtype)
    return jnp.concatenate([a[:, :KV_RANK], z, a[:, KV_RANK:], z2], axis=1)


def _dkv_from_padded(a):
    return jnp.concatenate([a[:, :KV_RANK], a[:, KV_RANK + ROPE_LO:KV_RANK + ROPE_LO + QK_ROPE]], axis=1)


def _unstack_cols(a):
    return jnp.transpose(a, (1, 0, 2)).reshape(a.shape[1], N_DEV * a.shape[2])


def kernel(x, positions, a_norm, a_w_in, a_conv, a_w_out, kv_norm, w_dkv, ckv_norm, w_ukv, b_norm, b_w_in, b_q_norm, b_w_uq, b_w_out, final_norm, loss_target, m_a_norm, m_a_w_in, m_a_conv, m_a_w_out, m_kv_norm, m_w_dkv, m_ckv_norm, m_w_ukv, m_b_norm, m_b_w_in, m_b_q_norm, m_b_w_uq, m_b_w_out, m_final_norm, v_a_norm, v_a_w_in, v_a_conv, v_a_w_out, v_kv_norm, v_w_dkv, v_ckv_norm, v_w_ukv, v_b_norm, v_b_w_in, v_b_q_norm, v_b_w_uq, v_b_w_out, v_final_norm):
    n_seq, seq, _ = x.shape
    t = n_seq * seq
    me = 4 * lax.axis_index("x") + 2 * lax.axis_index("y") + lax.axis_index("c")

    big = {
        "a_w_in": (a_w_in[0], m_a_w_in[0], v_a_w_in[0]),
        "a_w_out": (a_w_out[0], m_a_w_out[0], v_a_w_out[0]),
        "w_dkv": tuple(_dkv_to_padded(a) for a in (w_dkv, m_w_dkv, v_w_dkv)),
        "w_ukv": (w_ukv, m_w_ukv, v_w_ukv),
        "b_w_in": (b_w_in[0], m_b_w_in[0], v_b_w_in[0]),
        "b_w_uq": tuple(_pad_cols(a[0], HEAD_PAD) for a in (b_w_uq, m_b_w_uq, v_b_w_uq)),
        "b_w_out": (b_w_out[0], m_b_w_out[0], v_b_w_out[0]),
    }
    names = list(big)
    first_names, later_names = names[:2], names[2:]
    gathered = _all_gather([big[n][0].astype(BF16) for n in first_names] + [a_norm, a_conv[0]])
    a_norm_f = gathered[2].reshape(1, D_MODEL)
    a_conv_f = _unstack_cols(gathered[3])
    w_a_in = gathered[0]
    w_a_out = gathered[1].reshape(D_MODEL, D_MODEL)

    x2d = x.reshape(t, D_MODEL)
    tgt = loss_target.reshape(t, D_MODEL)
    pos = positions.astype(F32).reshape(t, 1)
    rc = _rope_consts()
    kvn, ckvn, bn, qn, fn = kv_norm.reshape(1, -1), ckv_norm.reshape(1, -1), b_norm, b_q_norm, final_norm.reshape(1, -1)

    x1, proj, conv, later = _conv_fwd(x2d, a_norm_f, w_a_in, a_conv_f, w_a_out, n_seq, seq,
                                      [big[n][0].astype(BF16) for n in later_names])
    full = dict(zip(later_names, later))
    w_dkv_f = full["w_dkv"].reshape(D_MODEL, CKR_PAD)
    w_ukv_f = _unstack_cols(full["w_ukv"])
    w_b_in = full["b_w_in"].reshape(D_MODEL, Q_RANK + E_B)
    w_uq_f = _unstack_cols(full["b_w_uq"])
    w_b_out = _unstack_cols(full["b_w_out"])
    layer_b = (kvn, w_dkv_f, ckvn, w_ukv_f, bn, w_b_in, qn, w_uq_f)
    ckr, cq, gb, q, k, v, cosv, sinv = _proj_fwd(x1, pos, rc, *layer_b)
    o, lse_row = _attn_fwd(q, k, v, n_seq, seq)
    dx2, do, dgb, delta_row, g_b_out, g_fn, loss_part = _head_tail(o, gb, x1, w_b_out, fn, tgt, n_seq, seq)
    dk, dv, dq = _attn_bwd(q, k, v, do, lse_row, delta_row, n_seq, seq)
    dx1, g_uq, g_b_in, g_ukv, g_dkv, g_qn, g_bn, g_ckvn, g_kvn = _proj_bwd(
        dq, dk, dv, dgb, cq, ckr, x1, dx2, cosv, sinv, rc, *layer_b)
    stacks = {
        "w_dkv": g_dkv.reshape(N_DEV, D_MODEL // N_DEV, CKR_PAD),
        "w_ukv": g_ukv,
        "b_w_in": g_b_in.reshape(N_DEV, D_MODEL // N_DEV, Q_RANK + E_B),
        "b_w_uq": g_uq,
        "b_w_out": g_b_out,
    }
    (dx, h_a, dproj, g_a_out, g_an, g_conv), later_parts = _conv_bwd(
        dx1, x2d, proj, conv, a_norm_f, w_a_in, a_conv_f, w_a_out, n_seq, seq, [stacks[n] for n in later_names])
    parts = dict(zip(later_names, later_parts))
    small = _pack_small(g_fn, g_kvn, g_bn, g_an, g_ckvn, g_qn, g_conv, loss_part)
    a_in_own, a_in_got, (parts["a_w_out"],), small_parts = _w_in_grad_exchange(
        h_a, dproj, D_MODEL * 4 // N_DEV, (me // 2).reshape(1).astype(jnp.int32),
        [g_a_out.reshape(N_DEV, D_MODEL // N_DEV, D_MODEL)], small)

    outs = {"a_w_in": _adamw_reduce(a_in_own, a_in_got, *big["a_w_in"])}
    outs.update(zip(names[1:], _adamw_reduce_many([parts[n] for n in names[1:]], [big[n] for n in names[1:]])))
    outs["w_dkv"] = tuple(_dkv_from_padded(a) for a in outs["w_dkv"])
    outs["b_w_uq"] = tuple(a[:, :QK_NOPE + QK_ROPE] for a in outs["b_w_uq"])
    for n in ("a_w_in", "a_w_out", "b_w_in", "b_w_uq", "b_w_out"):
        outs[n] = tuple(a[None] for a in outs[n])

    total = _sum_parts(small_parts)
    loss = total[4, LOSS_LANE]
    shard = D_MODEL // N_DEV
    g_small = {
        "final_norm": total[0], "kv_norm": total[1], "b_norm": total[2:3],
        "a_norm": lax.dynamic_slice_in_dim(total[3:4], me * shard, shard, axis=1),
        "ckv_norm": total[4, 0:KV_RANK], "b_q_norm": total[4:5, KV_RANK:KV_RANK + Q_RANK],
        "a_conv": lax.dynamic_slice_in_dim(total[5:8], me * shard, shard, axis=1)[None],
    }
    small_state = {
        "final_norm": (final_norm, m_final_norm, v_final_norm), "kv_norm": (kv_norm, m_kv_norm, v_kv_norm),
        "b_norm": (b_norm, m_b_norm, v_b_norm), "a_norm": (a_norm, m_a_norm, v_a_norm),
        "ckv_norm": (ckv_norm, m_ckv_norm, v_ckv_norm), "b_q_norm": (b_q_norm, m_b_q_norm, v_b_q_norm),
        "a_conv": (a_conv, m_a_conv, v_a_conv),
    }
    small_names = list(g_small)
    as2d = lambda a: a.reshape(-1, a.shape[-1])
    upd = _adamw_small([as2d(g_small[n]) for n in small_names],
                       *[[as2d(small_state[n][j]) for n in small_names] for j in range(3)])
    for n, u in zip(small_names, upd):
        outs[n] = (g_small[n],) + tuple(a.reshape(g_small[n].shape) for a in u)

    order = ["a_norm", "a_w_in", "a_conv", "a_w_out", "kv_norm", "w_dkv", "ckv_norm", "w_ukv", "b_norm", "b_w_in",
             "b_q_norm", "b_w_uq", "b_w_out", "final_norm"]
    result = [loss, dx.reshape(n_seq, seq, D_MODEL)]
    for j in range(4):
        result += [outs[n][j] for n in order]
    return tuple(result)
```

```python
import functools
import math

import numpy as np
import jax
import jax.numpy as jnp
from jax import lax
from jax.experimental import pallas as pl
from jax.experimental.pallas import tpu as pltpu

F32 = jnp.float32
BF16 = jnp.bfloat16
SAVED = BF16

D_MODEL = 1024
N_HEADS = 8
QK_NOPE = 64
QK_ROPE = 32
V_HEAD = 64
KV_RANK = 256
Q_RANK = 384
E_B = N_HEADS * V_HEAD
HEAD_PAD = 128
CKR_PAD = KV_RANK + HEAD_PAD
ROPE_LO = QK_NOPE
ROPE_HALF = QK_ROPE // 2
ROPE_THETA = 10000.0
SOFTMAX_SCALE = 1.0 / math.sqrt(QK_NOPE + QK_ROPE)
LOG2_E = math.log2(math.e)
LN_2 = math.log(2.0)
EPS = 1e-6
N_DEV = 8

ADAM_LR = 0.001
ADAM_B1 = 0.9
ADAM_B2 = 0.999
ADAM_EPS = 1e-08
ADAM_WD = 0.01
ADAM_STEP = 10

ROW_TILE = 256
LATENT_ROW_TILE = 512
PICKUP_BLOCK = 3
ATTN_TILE = 512
ATTN_GROUP = 4
ATTN_BWD_GROUP = 4
VMEM_LIMIT = 56 * 1024 * 1024

MESH = pl.DeviceIdType.MESH
NT = (((1,), (1,)), ((), ()))
TN = (((0,), (0,)), ((), ()))


def _dot(a, b):
    return jnp.dot(a.astype(BF16), b.astype(BF16), preferred_element_type=F32)


def _dot_nt(a, b):
    return lax.dot_general(a.astype(BF16), b.astype(BF16), NT, preferred_element_type=F32)


def _dot_tn(a, b):
    return lax.dot_general(a.astype(BF16), b.astype(BF16), TN, preferred_element_type=F32)


def _rstd(x):
    return lax.rsqrt(jnp.mean(x * x, axis=-1, keepdims=True) + EPS)


def _norm_bwd(a, xh, r):
    return r * (a - xh * jnp.mean(a * xh, axis=-1, keepdims=True))


def _silu_parts(g):
    sg = jax.nn.sigmoid(g)
    return g * sg, sg * (1.0 + g * (1.0 - sg))


def _rope_consts():
    inv = (ROPE_THETA ** (-np.arange(0, QK_ROPE, 2, dtype=np.float32) / QK_ROPE)).astype(np.float32)
    t = np.zeros((8, HEAD_PAD), np.float32)
    t[0, ROPE_LO:ROPE_LO + ROPE_HALF] = inv
    t[0, ROPE_LO + ROPE_HALF:ROPE_LO + QK_ROPE] = inv
    t[1, ROPE_LO:ROPE_LO + ROPE_HALF] = -1.0
    t[2, ROPE_LO + ROPE_HALF:ROPE_LO + QK_ROPE] = 1.0
    return jnp.asarray(t)


def _rope_angles(pos, rc):
    ang = pos * rc[0:1, :]
    return jnp.cos(ang), jnp.sin(ang)


def _rope_tables(cosv, sinv, rc):
    return cosv, sinv * rc[1:2, :], sinv * rc[2:3, :]


def _rope(x, ct, s1, s2):
    up = pltpu.roll(x, HEAD_PAD - ROPE_HALF, 1)
    dn = pltpu.roll(x, ROPE_HALF, 1)
    return x * ct + up * s1 + dn * s2


def _const_spec(shape):
    nd = len(shape)
    return pl.BlockSpec(shape, lambda *_: (0,) * nd, pipeline_mode=pl.Buffered(1))


def _acc_spec(shape):
    nd = len(shape)
    return pl.BlockSpec(shape, lambda *_: (0,) * nd)


def _params(n_axes):
    return pltpu.CompilerParams(dimension_semantics=("arbitrary",) * n_axes, vmem_limit_bytes=VMEM_LIMIT)


def _place():
    x, y, c = lax.axis_index("x"), lax.axis_index("y"), lax.axis_index("c")
    return x, y, c, 4 * x + 2 * y + c


def _peer(x, y, c, mask):
    px = 1 - x if mask & 4 else x
    py = 1 - y if mask & 2 else y
    pc = 1 - c if mask & 1 else c
    return (px, py, pc), 4 * px + 2 * py + pc


ANY_SPEC = pl.BlockSpec(memory_space=pl.ANY)


def _comm_sems(n):
    return [pltpu.SemaphoreType.DMA((n, N_DEV - 1)), pltpu.SemaphoreType.DMA((n, N_DEV - 1)), pltpu.SemaphoreType.DMA((n,))]


def _gather_copies(ins, outs, sems):
    send_sems, recv_sems, local_sems = sems
    x, y, c, me = _place()
    starts, waits = [], []
    for w in range(len(ins)):
        mine = pltpu.make_async_copy(ins[w], outs[w].at[me], local_sems.at[w])
        starts.append(mine)
        waits.append(mine)
        for mask in range(1, N_DEV):
            peer, peer_idx = _peer(x, y, c, mask)
            starts.append(pltpu.make_async_remote_copy(
                src_ref=ins[w], dst_ref=outs[w].at[me], send_sem=send_sems.at[w, mask - 1],
                recv_sem=recv_sems.at[w, mask - 1], device_id=peer, device_id_type=MESH))
            waits.append(pltpu.make_async_remote_copy(
                src_ref=ins[w], dst_ref=outs[w].at[peer_idx], send_sem=send_sems.at[w, mask - 1],
                recv_sem=recv_sems.at[w, mask - 1], device_id=peer, device_id_type=MESH))
    return starts, waits


def _scatter_copies(ins, outs, sems):
    send_sems, recv_sems, local_sems = sems
    x, y, c, me = _place()
    copies = []
    for w in range(len(ins)):
        copies.append(pltpu.make_async_copy(ins[w].at[me], outs[w].at[0], local_sems.at[w]))
        for mask in range(1, N_DEV):
            peer, peer_idx = _peer(x, y, c, mask)
            copies.append(pltpu.make_async_remote_copy(
                src_ref=ins[w].at[peer_idx], dst_ref=outs[w].at[mask], send_sem=send_sems.at[w, mask - 1],
                recv_sem=recv_sems.at[w, mask - 1], device_id=peer, device_id_type=MESH))
    return copies, copies


def _stacked(arrays):
    return [jax.ShapeDtypeStruct((N_DEV,) + a.shape, a.dtype) for a in arrays]


def _all_gather(shards):
    n = len(shards)

    def body(*refs):
        ins, outs = refs[:n], refs[n:2 * n]
        send_sems, recv_sems, local_sems = refs[2 * n:]
        x, y, c, me = _place()
        sibling = (x, y, 1 - c)
        chips = [(1 - x, y), (x, 1 - y), (1 - x, 1 - y)]

        def copy(w, k, block, to, src=None):
            idx = 4 * block[0] + 2 * block[1] + block[2]
            return pltpu.make_async_remote_copy(
                src_ref=outs[w].at[idx] if src is None else src, dst_ref=outs[w].at[idx],
                send_sem=send_sems.at[w, k], recv_sem=recv_sems.at[w, k], device_id=to, device_id_type=MESH)

        local, sent = [], []
        for w in range(n):
            mine = pltpu.make_async_copy(ins[w], outs[w].at[me], local_sems.at[w])
            mine.start()
            local.append(mine)
            first = [copy(w, 0, (x, y, c), sibling, src=ins[w])]
            first += [copy(w, 1 + j, (x, y, c), (*chip, c), src=ins[w]) for j, chip in enumerate(chips)]
            for cp in first:
                cp.start()
            sent += first
        for w in range(n):
            for j, chip in enumerate(chips):
                copy(w, 1 + j, (*chip, c), (x, y, c)).wait_recv()
                onward = copy(w, 4 + j, (*chip, c), sibling)
                onward.start()
                sent.append(onward)
        for w in range(n):
            copy(w, 0, sibling, (x, y, c)).wait_recv()
            for j, chip in enumerate(chips):
                copy(w, 4 + j, (*chip, 1 - c), (x, y, c)).wait_recv()
        for cp in sent:
            cp.wait_send()
        for cp in local:
            cp.wait()

    return pl.pallas_call(
        body, name="weight_all_gather", out_shape=_stacked(shards),
        in_specs=[ANY_SPEC] * n, out_specs=[ANY_SPEC] * n, scratch_shapes=_comm_sems(n),
    )(*shards)


def _conv_fwd(x, a_norm, w_in, conv_w, w_out, n_seq, seq, later_shards):
    tm = ROW_TILE
    nt = seq // tm
    n_col = w_in.shape[2]
    n_later = len(later_shards)

    def body(x_ref, an_ref, win_ref, cw_ref, wout_ref, *rest):
        shard_refs, rest = rest[:n_later], rest[n_later:]
        x1_ref, proj_ref, conv_ref = rest[:3]
        stack_refs, rest = rest[3:3 + n_later], rest[3 + n_later:]
        prev_ref, wide_ref, sems = rest[0], rest[1], rest[2:]
        step = pl.program_id(0) * nt + pl.program_id(1)

        @pl.when(step == 0)
        def _():
            for cp in _gather_copies(shard_refs, stack_refs, sems)[0]:
                cp.start()

        @pl.when(pl.program_id(1) == 0)
        def _():
            prev_ref[...] = jnp.zeros_like(prev_ref)

        xv = x_ref[...]
        h = (xv * _rstd(xv) * an_ref[...]).astype(BF16)
        for d in range(N_DEV):
            part = jnp.dot(h, win_ref[d], preferred_element_type=F32)
            wide_ref[:, d * n_col:(d + 1) * n_col] = part
            proj_ref[:, d * n_col:(d + 1) * n_col] = part.astype(SAVED)
        b = wide_ref[:, 0:D_MODEL]
        v = wide_ref[:, D_MODEL:2 * D_MODEL] * wide_ref[:, 2 * D_MODEL:3 * D_MODEL]
        g = wide_ref[:, 3 * D_MODEL:4 * D_MODEL]
        w0, w1, w2 = cw_ref[0:1, :], cw_ref[1:2, :], cw_ref[2:3, :]
        conv_ref[...] = w0 * pltpu.roll(v, 2, 0) + w1 * pltpu.roll(v, 1, 0) + w2 * v
        rows = lax.broadcasted_iota(jnp.int32, (8, D_MODEL), 0)
        p8, v8 = prev_ref[...], v[0:8]
        back1 = jnp.where(rows < 1, pltpu.roll(p8, 1, 0), pltpu.roll(v8, 1, 0))
        back2 = jnp.where(rows < 2, pltpu.roll(p8, 2, 0), pltpu.roll(v8, 2, 0))
        conv_ref[0:8, :] = w0 * back2 + w1 * back1 + w2 * v8
        prev_ref[...] = v[tm - 8:tm]
        silu, _ = _silu_parts(g)
        yv = silu * b * conv_ref[...]
        x1_ref[...] = xv + _dot(yv, wout_ref[...])

        @pl.when(step == n_seq * nt - 1)
        def _():
            for cp in _gather_copies(shard_refs, stack_refs, sems)[1]:
                cp.wait()

    tok = lambda width: pl.BlockSpec((tm, width), lambda s, i: (s * nt + i, 0))
    t = n_seq * seq
    outs = pl.pallas_call(
        body, name="conv_mixer_fwd", grid=(n_seq, nt),
        in_specs=[tok(D_MODEL), _const_spec((1, D_MODEL)), _const_spec(w_in.shape), _const_spec((3, D_MODEL)),
                  _const_spec(w_out.shape)] + [ANY_SPEC] * n_later,
        out_specs=[tok(D_MODEL), tok(4 * D_MODEL), tok(D_MODEL)] + [ANY_SPEC] * n_later,
        out_shape=[jax.ShapeDtypeStruct((t, D_MODEL), F32), jax.ShapeDtypeStruct((t, 4 * D_MODEL), SAVED),
                   jax.ShapeDtypeStruct((t, D_MODEL), F32)] + _stacked(later_shards),
        scratch_shapes=[pltpu.VMEM((8, D_MODEL), F32), pltpu.VMEM((tm, 4 * D_MODEL), F32)] + _comm_sems(n_later),
        compiler_params=_params(2),
    )(x, a_norm, w_in, conv_w, w_out, *later_shards)
    return outs[0], outs[1], outs[2], outs[3:]


def _conv_bwd(dx1, x, proj, conv, a_norm, w_in, conv_w, w_out, n_seq, seq, ready_stacks):
    tm = ROW_TILE
    nt = seq // tm
    n_col = w_in.shape[2]
    n_ready = len(ready_stacks)

    def body(dx1_ref, x_ref, proj_ref, conv_ref, an_ref, win_ref, cw_ref, wout_ref, *rest):
        ready_refs, rest = rest[:n_ready], rest[n_ready:]
        dx_ref, h_ref, dproj_ref, dwout_ref, dan_ref, dcw_ref = rest[:6]
        part_refs, rest = rest[6:6 + n_ready], rest[6 + n_ready:]
        next_ref, d1_ref, d2_ref = rest[:3]
        sems = rest[3:]
        step = pl.program_id(0) * nt + pl.program_id(1)
        first = step == 0

        @pl.when(first)
        def _():
            for cp in _scatter_copies(ready_refs, part_refs, sems)[0]:
                cp.start()
            dwout_ref[...] = jnp.zeros_like(dwout_ref)
            dan_ref[...] = jnp.zeros_like(dan_ref)
            dcw_ref[...] = jnp.zeros_like(dcw_ref)

        @pl.when(pl.program_id(1) == 0)
        def _():
            next_ref[...] = jnp.zeros_like(next_ref)

        dx1v = dx1_ref[...]
        dy = _dot_nt(dx1v, wout_ref[...])
        b = proj_ref[:, 0:D_MODEL].astype(F32)
        cc = proj_ref[:, D_MODEL:2 * D_MODEL].astype(F32)
        u = proj_ref[:, 2 * D_MODEL:3 * D_MODEL].astype(F32)
        g = proj_ref[:, 3 * D_MODEL:4 * D_MODEL].astype(F32)
        cv = conv_ref[...]
        silu, dsilu = _silu_parts(g)
        per_part = D_MODEL // n_col

        def back_through_w_in(part, grad):
            grad = grad.astype(BF16)
            dproj_ref[:, part * D_MODEL:(part + 1) * D_MODEL] = grad
            term = None
            for j in range(per_part):
                d = part * per_part + j
                piece = lax.dot_general(grad[:, j * n_col:(j + 1) * n_col], win_ref[d], NT, preferred_element_type=F32)
                term = piece if term is None else term + piece
            return term

        dh = back_through_w_in(0, dy * silu * cv)
        dh += back_through_w_in(3, dy * b * cv * dsilu)
        dwout_ref[...] += _dot_tn(silu * b * cv, dx1v)
        dconv = dy * silu * b
        d1_ref[...] = pltpu.roll(dconv, tm - 1, 0)
        d2_ref[...] = pltpu.roll(dconv, tm - 2, 0)
        rows = lax.broadcasted_iota(jnp.int32, (8, D_MODEL), 0)
        n8, c8 = next_ref[...], dconv[tm - 8:tm]
        d1_ref[tm - 8:tm, :] = jnp.where(rows >= 7, pltpu.roll(n8, 7, 0), pltpu.roll(c8, 7, 0))
        d2_ref[tm - 8:tm, :] = jnp.where(rows >= 6, pltpu.roll(n8, 6, 0), pltpu.roll(c8, 6, 0))
        next_ref[...] = dconv[0:8]
        d1, d2 = d1_ref[...], d2_ref[...]
        v = cc * u
        dcw_ref[0:1, :] += jnp.sum(d2 * v, axis=0, keepdims=True)
        dcw_ref[1:2, :] += jnp.sum(d1 * v, axis=0, keepdims=True)
        dcw_ref[2:3, :] += jnp.sum(dconv * v, axis=0, keepdims=True)
        dv = cw_ref[0:1, :] * d2 + cw_ref[1:2, :] * d1 + cw_ref[2:3, :] * dconv
        dh += back_through_w_in(1, dv * u)
        dh += back_through_w_in(2, dv * cc)
        xv = x_ref[...]
        r = _rstd(xv)
        xh = xv * r
        h_ref[...] = (xh * an_ref[...]).T.astype(BF16)
        dan_ref[...] += jnp.sum(dh * xh, axis=0, keepdims=True)
        dx_ref[...] = dx1v + _norm_bwd(dh * an_ref[...], xh, r)

        @pl.when(step == n_seq * nt - 1)
        def _():
            for cp in _scatter_copies(ready_refs, part_refs, sems)[1]:
                cp.wait()

    tok = lambda width: pl.BlockSpec((tm, width), lambda s, i: (s * nt + nt - 1 - i, 0))
    t = n_seq * seq
    outs = pl.pallas_call(
        body, name="conv_mixer_bwd", grid=(n_seq, nt),
        in_specs=[tok(D_MODEL), tok(D_MODEL), tok(4 * D_MODEL), tok(D_MODEL), _const_spec((1, D_MODEL)),
                  _const_spec(w_in.shape), _const_spec((3, D_MODEL)), _const_spec(w_out.shape)] + [ANY_SPEC] * n_ready,
        out_specs=[tok(D_MODEL), pl.BlockSpec((D_MODEL, tm), lambda s, i: (0, s * nt + nt - 1 - i)),
                   tok(4 * D_MODEL), _acc_spec((D_MODEL, D_MODEL)),
                   _acc_spec((1, D_MODEL)), _acc_spec((8, D_MODEL))] + [ANY_SPEC] * n_ready,
        out_shape=[jax.ShapeDtypeStruct((t, D_MODEL), F32), jax.ShapeDtypeStruct((D_MODEL, t), BF16),
                   jax.ShapeDtypeStruct((t, 4 * D_MODEL), BF16), jax.ShapeDtypeStruct((D_MODEL, D_MODEL), F32),
                   jax.ShapeDtypeStruct((1, D_MODEL), F32), jax.ShapeDtypeStruct((8, D_MODEL), F32)]
        + [jax.ShapeDtypeStruct(a.shape, a.dtype) for a in ready_stacks],
        scratch_shapes=[pltpu.VMEM((8, D_MODEL), F32), pltpu.VMEM((tm, D_MODEL), F32), pltpu.VMEM((tm, D_MODEL), F32)]
        + _comm_sems(n_ready),
        compiler_params=_params(2),
    )(dx1, x, proj, conv, a_norm, w_in, conv_w, w_out, *ready_stacks)
    return outs[:6], outs[6:]


def _w_in_grad_exchange(a_t, b, n_col, chip, ready_stacks, small):
    r, t = a_t.shape
    bt = 1024
    nk = t // bt
    n_ready = len(ready_stacks)
    n_chip = N_DEV // 2
    n_remote = N_DEV - 2

    def body(chip_ref, a_ref, b_ref, *rest):
        ready_refs, small_ref, rest = rest[:n_ready], rest[n_ready], rest[n_ready + 1:]
        own_ref, got_ref, rest = rest[0], rest[1], rest[2:]
        part_refs, small_all_ref, rest = rest[:n_ready], rest[n_ready], rest[n_ready + 1:]
        acc_ref, land_ref, stage_ref, pair_send, pair_recv, chip_send, chip_recv = rest[:7]
        scatter_sems, gather_sems = rest[7:10], rest[10:13]
        s, k = pl.program_id(0), pl.program_id(1)
        x, y, c, _ = _place()
        chip = 2 * x + y

        @pl.when(jnp.logical_and(s == 0, k == 0))
        def _():
            for cp in _scatter_copies(ready_refs, part_refs, scatter_sems)[0]:
                cp.start()
            for cp in _gather_copies([small_ref], [small_all_ref], gather_sems)[0]:
                cp.start()

        for parity in range(2):
            @pl.when(s % 2 == parity)
            def _(parity=parity):
                @pl.when(k == 0)
                def _():
                    acc_ref[parity] = jnp.zeros((r, n_col), F32)

                acc_ref[parity] += jnp.dot(a_ref[...], b_ref[...], preferred_element_type=F32)

        def to_sibling(step):
            return pltpu.make_async_remote_copy(
                src_ref=acc_ref.at[step % 2], dst_ref=land_ref, send_sem=pair_send.at[step], recv_sem=pair_recv.at[step],
                device_id=(x, y, 1 - c), device_id_type=MESH)

        def to_owner(nth):
            owner_chip = (chip + 1 + nth) % n_chip
            slot = jnp.bitwise_xor(chip, owner_chip) - 1
            return pltpu.make_async_remote_copy(
                src_ref=stage_ref.at[nth % 2], dst_ref=got_ref.at[slot], send_sem=chip_send.at[nth],
                recv_sem=chip_recv.at[slot], device_id=(owner_chip // 2, owner_chip % 2, c), device_id_type=MESH)

        for step in range(N_DEV):
            owner_core = step % 2

            @pl.when(jnp.logical_and(s == step, k == nk - 1))
            def _(step=step, owner_core=owner_core):
                @pl.when(c != owner_core)
                def _():
                    to_sibling(step).start()

                if step >= 1:
                    @pl.when(c == owner_core)
                    def _():
                        to_sibling(step - 1).wait_send()

            if step < N_DEV - 1:
                pickup = jnp.logical_and(s == step + 1, k == min(PICKUP_BLOCK, nk - 1))
            else:
                pickup = jnp.logical_and(s == step, k == nk - 1)

            @pl.when(jnp.logical_and(pickup, c == owner_core))
            def _(step=step):
                to_sibling(step).wait_recv()
                total = acc_ref[step % 2] + land_ref[...]
                if step < n_remote:
                    nth = step // 2
                    if nth >= 2:
                        to_owner(nth - 2).wait_send()
                    stage_ref[nth % 2] = total.astype(BF16)
                    to_owner(nth).start()
                else:
                    own_ref[...] = total

        @pl.when(jnp.logical_and(s == N_DEV - 1, k == nk - 1))
        def _():
            @pl.when(c != (N_DEV - 1) % 2)
            def _():
                to_sibling(N_DEV - 1).wait_send()

            to_owner(1).wait_send()
            to_owner(2).wait_send()
            for slot in range(n_chip - 1):
                pltpu.make_async_remote_copy(
                    src_ref=stage_ref.at[0], dst_ref=got_ref.at[slot], send_sem=chip_send.at[0],
                    recv_sem=chip_recv.at[slot], device_id=(x, y, c), device_id_type=MESH).wait_recv()
            for cp in _scatter_copies(ready_refs, part_refs, scatter_sems)[1]:
                cp.wait()
            for cp in _gather_copies([small_ref], [small_all_ref], gather_sems)[1]:
                cp.wait()

    def owner_block(s, k, chip):
        return (k, 2 * ((chip[0] + 1 + s // 2) % n_chip) + s % 2)

    grid_spec = pltpu.PrefetchScalarGridSpec(
        num_scalar_prefetch=1, grid=(N_DEV, nk),
        in_specs=[pl.BlockSpec((r, bt), lambda s, k, chip: (0, k)), pl.BlockSpec((bt, n_col), owner_block)]
        + [ANY_SPEC] * (n_ready + 1),
        out_specs=[pl.BlockSpec((r, n_col), lambda s, k, chip: (0, 0)), ANY_SPEC] + [ANY_SPEC] * (n_ready + 1),
        scratch_shapes=[pltpu.VMEM((2, r, n_col), F32), pltpu.VMEM((r, n_col), F32), pltpu.VMEM((2, r, n_col), BF16),
                        pltpu.SemaphoreType.DMA((N_DEV,)), pltpu.SemaphoreType.DMA((N_DEV,)),
                        pltpu.SemaphoreType.DMA((n_chip - 1,)), pltpu.SemaphoreType.DMA((n_chip - 1,))]
        + _comm_sems(n_ready) + _comm_sems(1))
    outs = pl.pallas_call(
        body, name="w_in_grad_exchange", grid_spec=grid_spec,
        out_shape=[jax.ShapeDtypeStruct((r, n_col), F32), jax.ShapeDtypeStruct((n_chip - 1, r, n_col), BF16)]
        + [jax.ShapeDtypeStruct(p.shape, p.dtype) for p in ready_stacks] + _stacked([small]),
        compiler_params=_params(2),
    )(chip, a_t, b, *ready_stacks, small)
    return outs[0], outs[1], outs[2:2 + n_ready], outs[2 + n_ready]


def _proj_fwd(x1, pos, rc, kv_norm, w_dkv, ckv_norm, w_ukv, b_norm, b_w_in, q_norm, w_uq):
    t = x1.shape[0]
    tm = LATENT_ROW_TILE

    def body(x1_ref, pos_ref, rc_ref, kvn_ref, wdkv_ref, ckvn_ref, wukv_ref, bn_ref, bwin_ref, qn_ref, wuq_ref,
             ckr_ref, cq_ref, gb_ref, q_ref, k_ref, v_ref, cos_ref, sin_ref):
        xv = x1_ref[...]
        xh = xv * _rstd(xv)
        ckr = _dot(xh * kvn_ref[...], wdkv_ref[...])
        pb = _dot(xh * bn_ref[...], bwin_ref[...])
        ckr_ref[...] = ckr
        ckv = ckr[:, 0:KV_RANK]
        kv = _dot(ckv * _rstd(ckv) * ckvn_ref[...], wukv_ref[...])
        cosv, sinv = _rope_angles(pos_ref[...], rc_ref[...])
        cos_ref[...] = cosv
        sin_ref[...] = sinv
        ct, s1, s2 = _rope_tables(cosv, sinv, rc_ref[...])
        k_rope = _rope(ckr[:, KV_RANK:CKR_PAD], ct, s1, s2)
        lane = lax.broadcasted_iota(jnp.int32, (tm, HEAD_PAD), 1)
        low = lane < QK_NOPE
        for h in range(N_HEADS):
            kv_h = kv[:, h * HEAD_PAD:(h + 1) * HEAD_PAD]
            k_ref[:, h * HEAD_PAD:(h + 1) * HEAD_PAD] = jnp.where(low, kv_h, k_rope).astype(BF16)
            v_ref[:, h * HEAD_PAD:(h + 1) * HEAD_PAD] = jnp.where(low, pltpu.roll(kv_h, V_HEAD, 1), 1.0).astype(BF16)
        cq = pb[:, 0:Q_RANK]
        cq_ref[...] = cq
        gb_ref[...] = pb[:, Q_RANK:Q_RANK + E_B]
        q = _dot(cq * _rstd(cq) * qn_ref[...], wuq_ref[...])
        for h in range(N_HEADS):
            q_ref[:, h * HEAD_PAD:(h + 1) * HEAD_PAD] = _rope(
                q[:, h * HEAD_PAD:(h + 1) * HEAD_PAD], ct, s1, s2).astype(BF16)

    tok = lambda width: pl.BlockSpec((tm, width), lambda i: (i, 0))
    weights = [kv_norm, w_dkv, ckv_norm, w_ukv, b_norm, b_w_in, q_norm, w_uq]
    wide = N_HEADS * HEAD_PAD
    return pl.pallas_call(
        body, name="latent_proj_fwd", grid=(t // tm,),
        in_specs=[tok(D_MODEL), tok(1), _const_spec(rc.shape)] + [_const_spec(w.shape) for w in weights],
        out_specs=[tok(CKR_PAD), tok(Q_RANK), tok(E_B), tok(wide), tok(wide), tok(wide), tok(HEAD_PAD), tok(HEAD_PAD)],
        out_shape=[jax.ShapeDtypeStruct((t, CKR_PAD), F32), jax.ShapeDtypeStruct((t, Q_RANK), F32),
                   jax.ShapeDtypeStruct((t, E_B), F32), jax.ShapeDtypeStruct((t, wide), BF16),
                   jax.ShapeDtypeStruct((t, wide), BF16), jax.ShapeDtypeStruct((t, wide), BF16),
                   jax.ShapeDtypeStruct((t, HEAD_PAD), F32), jax.ShapeDtypeStruct((t, HEAD_PAD), F32)],
        compiler_params=_params(1),
    )(x1, pos, rc, *weights)


def _proj_bwd(dq, dk, dv, dgb, cq, ckr, x1, dx2, cosv, sinv, rc, kv_norm, w_dkv, ckv_norm, w_ukv, b_norm, b_w_in, q_norm, w_uq):
    t = x1.shape[0]
    tm = LATENT_ROW_TILE
    wide = N_HEADS * HEAD_PAD

    def body(dq_ref, dk_ref, dv_ref, dgb_ref, cq_ref, ckr_ref, x1_ref, dx2_ref, cos_ref, sin_ref, rc_ref,
             kvn_ref, wdkv_ref, ckvn_ref, wukv_ref, bn_ref, bwin_ref, qn_ref, wuq_ref,
             dx1_ref, dwuq_ref, dwbin_ref, dwukv_ref, dwdkv_ref, dqn_ref, dbn_ref, dckvn_ref, dkvn_ref, dqu_ref, dkv_ref):
        @pl.when(pl.program_id(0) == 0)
        def _():
            for ref in (dwuq_ref, dwbin_ref, dwukv_ref, dwdkv_ref, dqn_ref, dbn_ref, dckvn_ref, dkvn_ref):
                ref[...] = jnp.zeros_like(ref)

        ct, s1, s2 = _rope_tables(cos_ref[...], sin_ref[...], rc_ref[...])
        lane = lax.broadcasted_iota(jnp.int32, (tm, HEAD_PAD), 1)
        low = lane < QK_NOPE
        xv = x1_ref[...]
        r = _rstd(xv)
        xh = xv * r
        dk_rope = jnp.zeros((tm, HEAD_PAD), F32)
        for h in range(N_HEADS):
            dk_h = dk_ref[:, h * HEAD_PAD:(h + 1) * HEAD_PAD].astype(F32)
            dv_h = dv_ref[:, h * HEAD_PAD:(h + 1) * HEAD_PAD].astype(F32)
            dkv_ref[:, h * HEAD_PAD:(h + 1) * HEAD_PAD] = jnp.where(low, dk_h, pltpu.roll(dv_h, V_HEAD, 1)).astype(BF16)
            dk_rope += dk_h
        rope_lanes = jnp.logical_and(lane >= ROPE_LO, lane < ROPE_LO + QK_ROPE)
        dk_rope = jnp.where(rope_lanes, _rope(dk_rope, ct, -s1, -s2), 0.0)
        ckv = ckr_ref[:, 0:KV_RANK]
        rk = _rstd(ckv)
        ckh = ckv * rk
        ckvn = (ckh * ckvn_ref[...]).astype(BF16)
        dwukv = lax.dot_general(ckvn, dkv_ref[...], TN, preferred_element_type=F32)
        for h in range(N_HEADS):
            dwukv_ref[h] += dwukv[:, h * HEAD_PAD:(h + 1) * HEAD_PAD]
        dckvn = lax.dot_general(dkv_ref[...], wukv_ref[...], NT, preferred_element_type=F32)
        dckvn_ref[...] += jnp.sum(dckvn * ckh, axis=0, keepdims=True)
        dckv = _norm_bwd(dckvn * ckvn_ref[...], ckh, rk)
        dckr = jnp.concatenate([dckv, dk_rope], axis=1).astype(BF16)
        dwdkv_ref[...] += _dot_tn(xh * kvn_ref[...], dckr)
        dh2 = lax.dot_general(dckr, wdkv_ref[...], NT, preferred_element_type=F32)
        for h in range(N_HEADS):
            dqu_ref[:, h * HEAD_PAD:(h + 1) * HEAD_PAD] = _rope(
                dq_ref[:, h * HEAD_PAD:(h + 1) * HEAD_PAD].astype(F32), ct, -s1, -s2).astype(BF16)
        cq = cq_ref[...]
        rq = _rstd(cq)
        cqh = cq * rq
        cqn = (cqh * qn_ref[...]).astype(BF16)
        dwuq = lax.dot_general(cqn, dqu_ref[...], TN, preferred_element_type=F32)
        for h in range(N_HEADS):
            dwuq_ref[h] += dwuq[:, h * HEAD_PAD:(h + 1) * HEAD_PAD]
        dcqn = lax.dot_general(dqu_ref[...], wuq_ref[...], NT, preferred_element_type=F32)
        dqn_ref[...] += jnp.sum(dcqn * cqh, axis=0, keepdims=True)
        dcq = _norm_bwd(dcqn * qn_ref[...], cqh, rq)
        dpb = jnp.concatenate([dcq.astype(BF16), dgb_ref[...]], axis=1)
        dwbin_ref[...] += _dot_tn(xh * bn_ref[...], dpb)
        dh3 = lax.dot_general(dpb, bwin_ref[...], NT, preferred_element_type=F32)
        dkvn_ref[...] += jnp.sum(dh2 * xh, axis=0, keepdims=True)
        dbn_ref[...] += jnp.sum(dh3 * xh, axis=0, keepdims=True)
        dx1_ref[...] = dx2_ref[...] + _norm_bwd(dh2 * kvn_ref[...] + dh3 * bn_ref[...], xh, r)

    tok = lambda width: pl.BlockSpec((tm, width), lambda i: (i, 0))
    weights = [kv_norm, w_dkv, ckv_norm, w_ukv, b_norm, b_w_in, q_norm, w_uq]
    acc_shapes = [(N_HEADS, Q_RANK, HEAD_PAD), (D_MODEL, Q_RANK + E_B), (N_HEADS, KV_RANK, HEAD_PAD), (D_MODEL, CKR_PAD),
                  (1, Q_RANK), (1, D_MODEL), (1, KV_RANK), (1, D_MODEL)]
    return pl.pallas_call(
        body, name="latent_proj_bwd", grid=(t // tm,),
        in_specs=[tok(wide), tok(wide), tok(wide), tok(E_B), tok(Q_RANK), tok(CKR_PAD), tok(D_MODEL), tok(D_MODEL),
                  tok(HEAD_PAD), tok(HEAD_PAD), _const_spec(rc.shape)] + [_const_spec(w.shape) for w in weights],
        out_specs=[tok(D_MODEL)] + [_acc_spec(s) for s in acc_shapes],
        out_shape=[jax.ShapeDtypeStruct((t, D_MODEL), F32)] + [jax.ShapeDtypeStruct(s, F32) for s in acc_shapes],
        scratch_shapes=[pltpu.VMEM((tm, wide), BF16), pltpu.VMEM((tm, wide), BF16)],
        compiler_params=_params(1),
    )(dq, dk, dv, dgb, cq, ckr, x1, dx2, cosv, sinv, rc, *weights)


def _attn_fwd(q, k, v, n_seq, seq):
    tb, group = ATTN_TILE, ATTN_GROUP
    rows = tb * group
    nq = seq // rows
    pair = 2 * HEAD_PAD

    def body(q_ref, k_ref, v_ref, o_ref, lr_ref):
        i = pl.program_id(2)
        row = lax.broadcasted_iota(jnp.int32, (tb, tb), 0)
        col = lax.broadcasted_iota(jnp.int32, (tb, tb), 1)
        causal = col <= row

        def step(j, carry, kinds):
            start = pl.multiple_of(j * tb, tb)
            out = []
            for g in range(group):
                for hh in range(2):
                    m, acc = carry[2 * g + hh]
                    if kinds[g] != "none":
                        heads = slice(hh * HEAD_PAD, (hh + 1) * HEAD_PAD)
                        s = lax.dot_general(q_ref[g * tb:(g + 1) * tb, heads], k_ref[pl.ds(start, tb), heads], NT,
                                            preferred_element_type=F32) * (SOFTMAX_SCALE * LOG2_E)
                        if kinds[g] == "diagonal":
                            s = jnp.where(causal, s, -jnp.inf)
                        m_new = jnp.maximum(m, jnp.max(s, axis=-1, keepdims=True))
                        p = jnp.exp2(s - m_new)
                        acc = jnp.exp2(m - m_new) * acc + jnp.dot(
                            p.astype(BF16), v_ref[pl.ds(start, tb), heads], preferred_element_type=F32)
                        m = m_new
                    out.append((m, acc))
            return tuple(out)

        one = (jnp.full((tb, 1), -jnp.inf, F32), jnp.zeros((tb, HEAD_PAD), F32))
        carry = lax.fori_loop(0, i * group, functools.partial(step, kinds=("full",) * group), (one,) * (2 * group))
        for d in range(group):
            kinds = tuple("none" if g < d else ("diagonal" if g == d else "full") for g in range(group))
            carry = step(i * group + d, carry, kinds)
        lane = lax.broadcasted_iota(jnp.int32, (tb, HEAD_PAD), 1)
        low = lane < V_HEAD
        for g in range(group):
            halves = []
            for hh in range(2):
                m, acc = carry[2 * g + hh]
                swapped = pltpu.roll(acc, V_HEAD, 1)
                halves.append(acc / swapped)
                lse = (m * LN_2) + jnp.log(jnp.where(low, swapped, acc))
                lr_ref[hh, :, g * tb:(g + 1) * tb] = lse.T[0:8, :]
            o_ref[g * tb:(g + 1) * tb, :] = jnp.where(low, halves[0], pltpu.roll(halves[1], V_HEAD, 1))

    t = n_seq * seq
    return pl.pallas_call(
        body, name="attention_fwd", grid=(n_seq, N_HEADS // 2, nq),
        in_specs=[pl.BlockSpec((rows, pair), lambda s, p, i: (s * nq + i, p)),
                  pl.BlockSpec((seq, pair), lambda s, p, i: (s, p)),
                  pl.BlockSpec((seq, pair), lambda s, p, i: (s, p))],
        out_specs=[pl.BlockSpec((rows, HEAD_PAD), lambda s, p, i: (s * nq + i, p)),
                   pl.BlockSpec((None, 2, 8, rows), lambda s, p, i: (s, p, 0, i))],
        out_shape=[jax.ShapeDtypeStruct((t, E_B), F32), jax.ShapeDtypeStruct((n_seq, N_HEADS, 8, seq), F32)],
        compiler_params=_params(3),
    )(q, k, v)


def _attn_bwd(q, k, v, do, lse_row, delta_row, n_seq, seq):
    tb, group = ATTN_TILE, ATTN_BWD_GROUP
    rows = tb * group
    nk = seq // rows
    n_inner = seq // tb
    pair = 2 * HEAD_PAD

    def body(q_ref, k_ref, v_ref, do_ref, lr_ref, dr_ref, dk_ref, dv_ref, dq_out_ref, dq_ref):
        j = pl.program_id(2)

        @pl.when(j == 0)
        def _():
            dq_ref[...] = jnp.zeros_like(dq_ref)

        row = lax.broadcasted_iota(jnp.int32, (tb, tb), 0)
        col = lax.broadcasted_iota(jnp.int32, (tb, tb), 1)
        causal = col >= row

        def step(i, carry, kinds):
            start = pl.multiple_of(i * tb, tb)
            out = []
            for hh in range(2):
                heads = slice(hh * HEAD_PAD, (hh + 1) * HEAD_PAD)
                qi = q_ref[pl.ds(start, tb), heads]
                doi = do_ref[pl.ds(start, tb), heads]
                dq = None
                for g in range(group):
                    dk_acc, dv_acc = carry[hh * group + g]
                    if kinds[g] != "none":
                        kg = k_ref[g * tb:(g + 1) * tb, heads]
                        st = lax.dot_general(kg, qi, NT, preferred_element_type=F32) * SOFTMAX_SCALE
                        pt = jnp.exp(st - lr_ref[hh, 0:1, pl.ds(start, tb)])
                        if kinds[g] == "diagonal":
                            pt = jnp.where(causal, pt, 0.0)
                        dv_acc = dv_acc + jnp.dot(pt.astype(BF16), doi, preferred_element_type=F32)
                        dpt = lax.dot_general(v_ref[g * tb:(g + 1) * tb, heads], doi, NT, preferred_element_type=F32)
                        dst = (pt * (dpt - dr_ref[hh, 0:1, pl.ds(start, tb)]) * SOFTMAX_SCALE).astype(BF16)
                        dk_acc = dk_acc + jnp.dot(dst, qi, preferred_element_type=F32)
                        term = lax.dot_general(dst, kg, TN, preferred_element_type=F32)
                        dq = term if dq is None else dq + term
                    out.append((dk_acc, dv_acc))
                dq_ref[pl.ds(start, tb), heads] += dq
            return tuple(out)

        one = (jnp.zeros((tb, HEAD_PAD), F32), jnp.zeros((tb, HEAD_PAD), F32))
        carry = (one,) * (2 * group)
        for d in range(group):
            kinds = tuple("full" if g < d else ("diagonal" if g == d else "none") for g in range(group))
            carry = step(j * group + d, carry, kinds)
        carry = lax.fori_loop((j + 1) * group, n_inner, functools.partial(step, kinds=("full",) * group), carry)
        for hh in range(2):
            for g in range(group):
                dk_ref[g * tb:(g + 1) * tb, hh * HEAD_PAD:(hh + 1) * HEAD_PAD] = carry[hh * group + g][0].astype(BF16)
                dv_ref[g * tb:(g + 1) * tb, hh * HEAD_PAD:(hh + 1) * HEAD_PAD] = carry[hh * group + g][1].astype(BF16)

        @pl.when(j == nk - 1)
        def _():
            dq_out_ref[...] = dq_ref[...].astype(BF16)

    t = n_seq * seq
    wide = N_HEADS * HEAD_PAD
    return pl.pallas_call(
        body, name="attention_bwd", grid=(n_seq, N_HEADS // 2, nk),
        in_specs=[pl.BlockSpec((seq, pair), lambda s, p, j: (s, p)),
                  pl.BlockSpec((rows, pair), lambda s, p, j: (s * nk + j, p)),
                  pl.BlockSpec((rows, pair), lambda s, p, j: (s * nk + j, p)),
                  pl.BlockSpec((seq, pair), lambda s, p, j: (s, p)),
                  pl.BlockSpec((None, 2, 8, seq), lambda s, p, j: (s, p, 0, 0)),
                  pl.BlockSpec((None, 2, 8, seq), lambda s, p, j: (s, p, 0, 0))],
        out_specs=[pl.BlockSpec((rows, pair), lambda s, p, j: (s * nk + j, p)),
                   pl.BlockSpec((rows, pair), lambda s, p, j: (s * nk + j, p)),
                   pl.BlockSpec((seq, pair), lambda s, p, j: (s, p))],
        out_shape=[jax.ShapeDtypeStruct((t, wide), BF16), jax.ShapeDtypeStruct((t, wide), BF16),
                   jax.ShapeDtypeStruct((t, wide), BF16)],
        scratch_shapes=[pltpu.VMEM((seq, pair), F32)],
        compiler_params=_params(3),
    )(q, k, v, do, lse_row, delta_row)


def _head_tail(o, gb, x1, w_out, final_norm, target, n_seq, seq):
    tm = LATENT_ROW_TILE
    nt = seq // tm
    n_col = D_MODEL // N_DEV

    head_of_column = np.arange(E_B) // V_HEAD
    selector = jnp.asarray((head_of_column[None, :] == np.arange(N_HEADS)[:, None]).astype(np.float32))

    def body(o_ref, gb_ref, x1_ref, wout_ref, fn_ref, tgt_ref, sel_ref,
             dx2_ref, do_ref, dgb_ref, dr_ref, dwout_ref, dfn_ref, loss_ref):
        first = jnp.logical_and(pl.program_id(0) == 0, pl.program_id(1) == 0)

        @pl.when(first)
        def _():
            dwout_ref[...] = jnp.zeros_like(dwout_ref)
            dfn_ref[...] = jnp.zeros_like(dfn_ref)
            loss_ref[...] = jnp.zeros_like(loss_ref)

        ov, g = o_ref[...], gb_ref[...]
        silu, dsilu = _silu_parts(g)
        gated = (ov * silu).astype(BF16)
        x2 = x1_ref[...] + jnp.dot(gated, wout_ref[...], preferred_element_type=F32)
        r = _rstd(x2)
        xh = x2 * r
        err = xh * fn_ref[...] - tgt_ref[...]
        loss_ref[...] += 0.5 * jnp.sum(jnp.mean(err * err, axis=-1, keepdims=True), axis=0, keepdims=True)
        dy = err / D_MODEL
        dfn_ref[...] += jnp.sum(dy * xh, axis=0, keepdims=True)
        dx2 = _norm_bwd(dy * fn_ref[...], xh, r)
        dx2_ref[...] = dx2
        dx2b = dx2.astype(BF16)
        dw = lax.dot_general(gated, dx2b, TN, preferred_element_type=F32)
        for d in range(N_DEV):
            dwout_ref[d] += dw[:, d * n_col:(d + 1) * n_col]
        dgated = lax.dot_general(dx2b, wout_ref[...], NT, preferred_element_type=F32)
        do = dgated * silu
        dgb_ref[...] = (dgated * ov * dsilu).astype(BF16)
        delta = lax.dot_general(sel_ref[...], do * ov, NT, precision=lax.Precision.HIGHEST, preferred_element_type=F32)
        lane = lax.broadcasted_iota(jnp.int32, (tm, HEAD_PAD), 1)
        low = lane < V_HEAD
        for p in range(N_HEADS // 2):
            do_pair = do[:, p * HEAD_PAD:(p + 1) * HEAD_PAD]
            for hh in range(2):
                h = 2 * p + hh
                mine = do_pair if hh == 0 else pltpu.roll(do_pair, V_HEAD, 1)
                do_ref[:, h * HEAD_PAD:(h + 1) * HEAD_PAD] = jnp.where(low, mine, 0.0).astype(BF16)
                dr_ref[h] = jnp.broadcast_to(delta[h:h + 1, :], (8, tm))

    tok = lambda width: pl.BlockSpec((tm, width), lambda s, i: (s * nt + i, 0))
    t = n_seq * seq
    return pl.pallas_call(
        body, name="head_tail", grid=(n_seq, nt),
        in_specs=[tok(E_B), tok(E_B), tok(D_MODEL), _const_spec(w_out.shape), _const_spec((1, D_MODEL)), tok(D_MODEL),
                  _const_spec((N_HEADS, E_B))],
        out_specs=[tok(D_MODEL), tok(N_HEADS * HEAD_PAD), tok(E_B),
                   pl.BlockSpec((None, N_HEADS, 8, tm), lambda s, i: (s, 0, 0, i)),
                   _acc_spec((N_DEV, E_B, n_col)), _acc_spec((1, D_MODEL)), _acc_spec((1, 1))],
        out_shape=[jax.ShapeDtypeStruct((t, D_MODEL), F32), jax.ShapeDtypeStruct((t, N_HEADS * HEAD_PAD), BF16),
                   jax.ShapeDtypeStruct((t, E_B), BF16),
                   jax.ShapeDtypeStruct((n_seq, N_HEADS, 8, seq), F32), jax.ShapeDtypeStruct((N_DEV, E_B, n_col), F32),
                   jax.ShapeDtypeStruct((1, D_MODEL), F32), jax.ShapeDtypeStruct((1, 1), F32)],
        compiler_params=_params(2),
    )(o, gb, x1, w_out, final_norm, target, selector)


def _adamw_math(w, g, m, v):
    m = ADAM_B1 * m + (1.0 - ADAM_B1) * g
    v = ADAM_B2 * v + (1.0 - ADAM_B2) * jnp.square(g)
    m_hat = m / (1.0 - ADAM_B1 ** ADAM_STEP)
    v_hat = v / (1.0 - ADAM_B2 ** ADAM_STEP)
    delta = -ADAM_LR * (m_hat / (jnp.sqrt(v_hat) + ADAM_EPS) + ADAM_WD * w)
    return delta, m, v


def _adamw_reduce(own, parts, w, m, v):
    rows, cols = w.shape
    br = 256
    n_parts = parts.shape[0]

    def body(own_ref, p_ref, w_ref, m_ref, v_ref, g_ref, d_ref, nm_ref, nv_ref):
        g = own_ref[...]
        for k in range(n_parts):
            g = g + p_ref[k].astype(F32)
        g_ref[...] = g
        d_ref[...], nm_ref[...], nv_ref[...] = _adamw_math(w_ref[...], g, m_ref[...], v_ref[...])

    blk = pl.BlockSpec((br, cols), lambda i: (i, 0))
    return pl.pallas_call(
        body, name="adamw_reduce", grid=(rows // br,),
        in_specs=[blk, pl.BlockSpec((n_parts, br, cols), lambda i: (0, i, 0)), blk, blk, blk],
        out_specs=[blk] * 4, out_shape=[jax.ShapeDtypeStruct((rows, cols), F32)] * 4,
        compiler_params=_params(1),
    )(own, parts, w, m, v)


def _adamw_reduce_many(parts_list, states):
    n = len(parts_list)

    def body(*refs):
        ins, outs = refs[:4 * n], refs[4 * n:]
        for j in range(n):
            p_ref, w_ref, m_ref, v_ref = ins[4 * j:4 * j + 4]
            g = p_ref[0]
            for k in range(1, p_ref.shape[0]):
                g = g + p_ref[k]
            outs[4 * j][...] = g
            outs[4 * j + 1][...], outs[4 * j + 2][...], outs[4 * j + 3][...] = _adamw_math(
                w_ref[...], g, m_ref[...], v_ref[...])

    flat_in = [a for p, st in zip(parts_list, states) for a in (p,) + tuple(st)]
    shapes = [jax.ShapeDtypeStruct(st[0].shape, F32) for st in states for _ in range(4)]
    flat = pl.pallas_call(body, name="adamw_reduce_many", out_shape=shapes,
                          compiler_params=pltpu.CompilerParams(vmem_limit_bytes=VMEM_LIMIT))(*flat_in)
    return [tuple(flat[4 * j:4 * j + 4]) for j in range(n)]


def _sum_parts(parts):
    def body(p_ref, o_ref):
        g = p_ref[0]
        for k in range(1, N_DEV):
            g = g + p_ref[k]
        o_ref[...] = g

    return pl.pallas_call(body, name="small_grad_sum", out_shape=jax.ShapeDtypeStruct(parts.shape[1:], F32))(parts)


def _adamw_small(gs, ws, ms, vs):
    n = len(gs)

    def body(*refs):
        ins, outs = refs[:4 * n], refs[4 * n:]
        for j in range(n):
            g_ref, w_ref, m_ref, v_ref = ins[j], ins[n + j], ins[2 * n + j], ins[3 * n + j]
            outs[3 * j][...], outs[3 * j + 1][...], outs[3 * j + 2][...] = _adamw_math(
                w_ref[...], g_ref[...], m_ref[...], v_ref[...])

    shapes = [jax.ShapeDtypeStruct(g.shape, F32) for g in gs for _ in range(3)]
    flat = pl.pallas_call(body, name="adamw_small", out_shape=shapes)(*gs, *ws, *ms, *vs)
    return [tuple(flat[3 * j:3 * j + 3]) for j in range(n)]


SMALL_ROWS = 8
LOSS_LANE = D_MODEL - 1


def _pack_small(g_fn, g_kvn, g_bn, g_an, g_ckvn, g_qn, g_conv, loss_part):
    def body(fn_ref, kvn_ref, bn_ref, an_ref, ckvn_ref, qn_ref, conv_ref, loss_ref, o_ref):
        o_ref[0:1, :] = fn_ref[...]
        o_ref[1:2, :] = kvn_ref[...]
        o_ref[2:3, :] = bn_ref[...]
        o_ref[3:4, :] = an_ref[...]
        lane = lax.broadcasted_iota(jnp.int32, (1, D_MODEL), 1)
        o_ref[4:5, :] = jnp.where(lane == LOSS_LANE, loss_ref[...], 0.0)
        o_ref[4:5, 0:KV_RANK] = ckvn_ref[...]
        o_ref[4:5, KV_RANK:KV_RANK + Q_RANK] = qn_ref[...]
        o_ref[5:8, :] = conv_ref[0:3, :]

    return pl.pallas_call(body, name="pack_small", out_shape=jax.ShapeDtypeStruct((SMALL_ROWS, D_MODEL), F32))(
        g_fn, g_kvn, g_bn, g_an, g_ckvn, g_qn, g_conv, loss_part)


def _pad_cols(a, width):
    return jnp.pad(a, ((0, 0), (0, width - a.shape[1])))


def _dkv_to_padded(a):
    r = a.shape[0]
    z = jnp.zeros((r, ROPE_LO), a.dtype)
    z2 = jnp.zeros((r, HEAD_PAD - ROPE_LO - QK_ROPE), a.dtype)
    return jnp.concatenate([a[:, :KV_RANK], z, a[:, KV_RANK:], z2], axis=1)


def _dkv_from_padded(a):
    return jnp.concatenate([a[:, :KV_RANK], a[:, KV_RANK + ROPE_LO:KV_RANK + ROPE_LO + QK_ROPE]], axis=1)


def _unstack_cols(a):
    return jnp.transpose(a, (1, 0, 2)).reshape(a.shape[1], N_DEV * a.shape[2])


def kernel(x, positions, a_norm, a_w_in, a_conv, a_w_out, kv_norm, w_dkv, ckv_norm, w_ukv, b_norm, b_w_in, b_q_norm, b_w_uq, b_w_out, final_norm, loss_target, m_a_norm, m_a_w_in, m_a_conv, m_a_w_out, m_kv_norm, m_w_dkv, m_ckv_norm, m_w_ukv, m_b_norm, m_b_w_in, m_b_q_norm, m_b_w_uq, m_b_w_out, m_final_norm, v_a_norm, v_a_w_in, v_a_conv, v_a_w_out, v_kv_norm, v_w_dkv, v_ckv_norm, v_w_ukv, v_b_norm, v_b_w_in, v_b_q_norm, v_b_w_uq, v_b_w_out, v_final_norm):
    n_seq, seq, _ = x.shape
    t = n_seq * seq
    me = 4 * lax.axis_index("x") + 2 * lax.axis_index("y") + lax.axis_index("c")

    big = {
        "a_w_in": (a_w_in[0], m_a_w_in[0], v_a_w_in[0]),
        "a_w_out": (a_w_out[0], m_a_w_out[0], v_a_w_out[0]),
        "w_dkv": tuple(_dkv_to_padded(a) for a in (w_dkv, m_w_dkv, v_w_dkv)),
        "w_ukv": (w_ukv, m_w_ukv, v_w_ukv),
        "b_w_in": (b_w_in[0], m_b_w_in[0], v_b_w_in[0]),
        "b_w_uq": tuple(_pad_cols(a[0], HEAD_PAD) for a in (b_w_uq, m_b_w_uq, v_b_w_uq)),
        "b_w_out": (b_w_out[0], m_b_w_out[0], v_b_w_out[0]),
    }
    names = list(big)
    first_names, later_names = names[:2], names[2:]
    gathered = _all_gather([big[n][0].astype(BF16) for n in first_names] + [a_norm, a_conv[0]])
    a_norm_f = gathered[2].reshape(1, D_MODEL)
    a_conv_f = _unstack_cols(gathered[3])
    w_a_in = gathered[0]
    w_a_out = gathered[1].reshape(D_MODEL, D_MODEL)

    x2d = x.reshape(t, D_MODEL)
    tgt = loss_target.reshape(t, D_MODEL)
    pos = positions.astype(F32).reshape(t, 1)
    rc = _rope_consts()
    kvn, ckvn, bn, qn, fn = kv_norm.reshape(1, -1), ckv_norm.reshape(1, -1), b_norm, b_q_norm, final_norm.reshape(1, -1)

    x1, proj, conv, later = _conv_fwd(x2d, a_norm_f, w_a_in, a_conv_f, w_a_out, n_seq, seq,
                                      [big[n][0].astype(BF16) for n in later_names])
    full = dict(zip(later_names, later))
    w_dkv_f = full["w_dkv"].reshape(D_MODEL, CKR_PAD)
    w_ukv_f = _unstack_cols(full["w_ukv"])
    w_b_in = full["b_w_in"].reshape(D_MODEL, Q_RANK + E_B)
    w_uq_f = _unstack_cols(full["b_w_uq"])
    w_b_out = _unstack_cols(full["b_w_out"])
    layer_b = (kvn, w_dkv_f, ckvn, w_ukv_f, bn, w_b_in, qn, w_uq_f)
    ckr, cq, gb, q, k, v, cosv, sinv = _proj_fwd(x1, pos, rc, *layer_b)
    o, lse_row = _attn_fwd(q, k, v, n_seq, seq)
    dx2, do, dgb, delta_row, g_b_out, g_fn, loss_part = _head_tail(o, gb, x1, w_b_out, fn, tgt, n_seq, seq)
    dk, dv, dq = _attn_bwd(q, k, v, do, lse_row, delta_row, n_seq, seq)
    dx1, g_uq, g_b_in, g_ukv, g_dkv, g_qn, g_bn, g_ckvn, g_kvn = _proj_bwd(
        dq, dk, dv, dgb, cq, ckr, x1, dx2, cosv, sinv, rc, *layer_b)
    stacks = {
        "w_dkv": g_dkv.reshape(N_DEV, D_MODEL // N_DEV, CKR_PAD),
        "w_ukv": g_ukv,
        "b_w_in": g_b_in.reshape(N_DEV, D_MODEL // N_DEV, Q_RANK + E_B),
        "b_w_uq": g_uq,
        "b_w_out": g_b_out,
    }
    (dx, h_a, dproj, g_a_out, g_an, g_conv), later_parts = _conv_bwd(
        dx1, x2d, proj, conv, a_norm_f, w_a_in, a_conv_f, w_a_out, n_seq, seq, [stacks[n] for n in later_names])
    parts = dict(zip(later_names, later_parts))
    small = _pack_small(g_fn, g_kvn, g_bn, g_an, g_ckvn, g_qn, g_conv, loss_part)
    a_in_own, a_in_got, (parts["a_w_out"],), small_parts = _w_in_grad_exchange(
        h_a, dproj, D_MODEL * 4 // N_DEV, (me // 2).reshape(1).astype(jnp.int32),
        [g_a_out.reshape(N_DEV, D_MODEL // N_DEV, D_MODEL)], small)

    outs = {"a_w_in": _adamw_reduce(a_in_own, a_in_got, *big["a_w_in"])}
    outs.update(zip(names[1:], _adamw_reduce_many([parts[n] for n in names[1:]], [big[n] for n in names[1:]])))
    outs["w_dkv"] = tuple(_dkv_from_padded(a) for a in outs["w_dkv"])
    outs["b_w_uq"] = tuple(a[:, :QK_NOPE + QK_ROPE] for a in outs["b_w_uq"])
    for n in ("a_w_in", "a_w_out", "b_w_in", "b_w_uq", "b_w_out"):
        outs[n] = tuple(a[None] for a in outs[n])

    total = _sum_parts(small_parts)
    loss = total[4, LOSS_LANE]
    shard = D_MODEL // N_DEV
    g_small = {
        "final_norm": total[0], "kv_norm": total[1], "b_norm": total[2:3],
        "a_norm": lax.dynamic_slice_in_dim(total[3:4], me * shard, shard, axis=1),
        "ckv_norm": total[4, 0:KV_RANK], "b_q_norm": total[4:5, KV_RANK:KV_RANK + Q_RANK],
        "a_conv": lax.dynamic_slice_in_dim(total[5:8], me * shard, shard, axis=1)[None],
    }
    small_state = {
        "final_norm": (final_norm, m_final_norm, v_final_norm), "kv_norm": (kv_norm, m_kv_norm, v_kv_norm),
        "b_norm": (b_norm, m_b_norm, v_b_norm), "a_norm": (a_norm, m_a_norm, v_a_norm),
        "ckv_norm": (ckv_norm, m_ckv_norm, v_ckv_norm), "b_q_norm": (b_q_norm, m_b_q_norm, v_b_q_norm),
        "a_conv": (a_conv, m_a_conv, v_a_conv),
    }
    small_names = list(g_small)
    as2d = lambda a: a.reshape(-1, a.shape[-1])
    upd = _adamw_small([as2d(g_small[n]) for n in small_names],
                       *[[as2d(small_state[n][j]) for n in small_names] for j in range(3)])
    for n, u in zip(small_names, upd):
        outs[n] = (g_small[n],) + tuple(a.reshape(g_small[n].shape) for a in u)

    order = ["a_norm", "a_w_in", "a_conv", "a_w_out", "kv_norm", "w_dkv", "ckv_norm", "w_ukv", "b_norm", "b_w_in",
             "b_q_norm", "b_w_uq", "b_w_out", "final_norm"]
    result = [loss, dx.reshape(n_seq, seq, D_MODEL)]
    for j in range(4):
        result += [outs[n][j] for n in order]
    return tuple(result)
```

```python
import functools
import math

import numpy as np
import jax
import jax.numpy as jnp
from jax import lax
from jax.experimental import pallas as pl
from jax.experimental.pallas import tpu as pltpu

F32 = jnp.float32
BF16 = jnp.bfloat16
SAVED = BF16

D_MODEL = 1024
N_HEADS = 8
QK_NOPE = 64
QK_ROPE = 32
V_HEAD = 64
KV_RANK = 256
Q_RANK = 384
E_B = N_HEADS * V_HEAD
HEAD_PAD = 128
CKR_PAD = KV_RANK + HEAD_PAD
ROPE_LO = QK_NOPE
ROPE_HALF = QK_ROPE // 2
ROPE_THETA = 10000.0
SOFTMAX_SCALE = 1.0 / math.sqrt(QK_NOPE + QK_ROPE)
LOG2_E = math.log2(math.e)
LN_2 = math.log(2.0)
EPS = 1e-6
N_DEV = 8

ADAM_LR = 0.001
ADAM_B1 = 0.9
ADAM_B2 = 0.999
ADAM_EPS = 1e-08
ADAM_WD = 0.01
ADAM_STEP = 10

ROW_TILE = 256
LATENT_ROW_TILE = 512
PICKUP_BLOCK = 3
ATTN_TILE = 512
ATTN_GROUP = 4
ATTN_BWD_GROUP = 4
VMEM_LIMIT = 56 * 1024 * 1024

MESH = pl.DeviceIdType.MESH
NT = (((1,), (1,)), ((), ()))
TN = (((0,), (0,)), ((), ()))


def _dot(a, b):
    return jnp.dot(a.astype(BF16), b.astype(BF16), preferred_element_type=F32)


def _dot_nt(a, b):
    return lax.dot_general(a.astype(BF16), b.astype(BF16), NT, preferred_element_type=F32)


def _dot_tn(a, b):
    return lax.dot_general(a.astype(BF16), b.astype(BF16), TN, preferred_element_type=F32)


def _rstd(x):
    return lax.rsqrt(jnp.mean(x * x, axis=-1, keepdims=True) + EPS)


def _norm_bwd(a, xh, r):
    return r * (a - xh * jnp.mean(a * xh, axis=-1, keepdims=True))


def _silu_parts(g):
    sg = jax.nn.sigmoid(g)
    return g * sg, sg * (1.0 + g * (1.0 - sg))


def _rope_consts():
    inv = (ROPE_THETA ** (-np.arange(0, QK_ROPE, 2, dtype=np.float32) / QK_ROPE)).astype(np.float32)
    t = np.zeros((8, HEAD_PAD), np.float32)
    t[0, ROPE_LO:ROPE_LO + ROPE_HALF] = inv
    t[0, ROPE_LO + ROPE_HALF:ROPE_LO + QK_ROPE] = inv
    t[1, ROPE_LO:ROPE_LO + ROPE_HALF] = -1.0
    t[2, ROPE_LO + ROPE_HALF:ROPE_LO + QK_ROPE] = 1.0
    return jnp.asarray(t)


def _rope_angles(pos, rc):
    ang = pos * rc[0:1, :]
    return jnp.cos(ang), jnp.sin(ang)


def _rope_tables(cosv, sinv, rc):
    return cosv, sinv * rc[1:2, :], sinv * rc[2:3, :]


def _rope(x, ct, s1, s2):
    up = pltpu.roll(x, HEAD_PAD - ROPE_HALF, 1)
    dn = pltpu.roll(x, ROPE_HALF, 1)
    return x * ct + up * s1 + dn * s2


def _const_spec(shape):
    nd = len(shape)
    return pl.BlockSpec(shape, lambda *_: (0,) * nd, pipeline_mode=pl.Buffered(1))


def _acc_spec(shape):
    nd = len(shape)
    return pl.BlockSpec(shape, lambda *_: (0,) * nd)


def _params(n_axes):
    return pltpu.CompilerParams(dimension_semantics=("arbitrary",) * n_axes, vmem_limit_bytes=VMEM_LIMIT)


def _place():
    x, y, c = lax.axis_index("x"), lax.axis_index("y"), lax.axis_index("c")
    return x, y, c, 4 * x + 2 * y + c


def _peer(x, y, c, mask):
    px = 1 - x if mask & 4 else x
    py = 1 - y if mask & 2 else y
    pc = 1 - c if mask & 1 else c
    return (px, py, pc), 4 * px + 2 * py + pc


ANY_SPEC = pl.BlockSpec(memory_space=pl.ANY)


def _comm_sems(n):
    return [pltpu.SemaphoreType.DMA((n, N_DEV - 1)), pltpu.SemaphoreType.DMA((n, N_DEV - 1)), pltpu.SemaphoreType.DMA((n,))]


def _gather_copies(ins, outs, sems):
    send_sems, recv_sems, local_sems = sems
    x, y, c, me = _place()
    starts, waits = [], []
    for w in range(len(ins)):
        mine = pltpu.make_async_copy(ins[w], outs[w].at[me], local_sems.at[w])
        starts.append(mine)
        waits.append(mine)
        for mask in range(1, N_DEV):
            peer, peer_idx = _peer(x, y, c, mask)
            starts.append(pltpu.make_async_remote_copy(
                src_ref=ins[w], dst_ref=outs[w].at[me], send_sem=send_sems.at[w, mask - 1],
                recv_sem=recv_sems.at[w, mask - 1], device_id=peer, device_id_type=MESH))
            waits.append(pltpu.make_async_remote_copy(
                src_ref=ins[w], dst_ref=outs[w].at[peer_idx], send_sem=send_sems.at[w, mask - 1],
                recv_sem=recv_sems.at[w, mask - 1], device_id=peer, device_id_type=MESH))
    return starts, waits


def _scatter_copies(ins, outs, sems):
    send_sems, recv_sems, local_sems = sems
    x, y, c, me = _place()
    copies = []
    for w in range(len(ins)):
        copies.append(pltpu.make_async_copy(ins[w].at[me], outs[w].at[0], local_sems.at[w]))
        for mask in range(1, N_DEV):
            peer, peer_idx = _peer(x, y, c, mask)
            copies.append(pltpu.make_async_remote_copy(
                src_ref=ins[w].at[peer_idx], dst_ref=outs[w].at[mask], send_sem=send_sems.at[w, mask - 1],
                recv_sem=recv_sems.at[w, mask - 1], device_id=peer, device_id_type=MESH))
    return copies, copies


def _stacked(arrays):
    return [jax.ShapeDtypeStruct((N_DEV,) + a.shape, a.dtype) for a in arrays]


def _all_gather(shards):
    n = len(shards)

    def body(*refs):
        ins, outs = refs[:n], refs[n:2 * n]
        send_sems, recv_sems, local_sems = refs[2 * n:]
        x, y, c, me = _place()
        sibling = (x, y, 1 - c)
        chips = [(1 - x, y), (x, 1 - y), (1 - x, 1 - y)]

        def copy(w, k, block, to, src=None):
            idx = 4 * block[0] + 2 * block[1] + block[2]
            return pltpu.make_async_remote_copy(
                src_ref=outs[w].at[idx] if src is None else src, dst_ref=outs[w].at[idx],
                send_sem=send_sems.at[w, k], recv_sem=recv_sems.at[w, k], device_id=to, device_id_type=MESH)

        local, sent = [], []
        for w in range(n):
            mine = pltpu.make_async_copy(ins[w], outs[w].at[me], local_sems.at[w])
            mine.start()
            local.append(mine)
            first = [copy(w, 0, (x, y, c), sibling, src=ins[w])]
            first += [copy(w, 1 + j, (x, y, c), (*chip, c), src=ins[w]) for j, chip in enumerate(chips)]
            for cp in first:
                cp.start()
            sent += first
        for w in range(n):
            for j, chip in enumerate(chips):
                copy(w, 1 + j, (*chip, c), (x, y, c)).wait_recv()
                onward = copy(w, 4 + j, (*chip, c), sibling)
                onward.start()
                sent.append(onward)
        for w in range(n):
            copy(w, 0, sibling, (x, y, c)).wait_recv()
            for j, chip in enumerate(chips):
                copy(w, 4 + j, (*chip, 1 - c), (x, y, c)).wait_recv()
        for cp in sent:
            cp.wait_send()
        for cp in local:
            cp.wait()

    return pl.pallas_call(
        body, name="weight_all_gather", out_shape=_stacked(shards),
        in_specs=[ANY_SPEC] * n, out_specs=[ANY_SPEC] * n, scratch_shapes=_comm_sems(n),
    )(*shards)


def _conv_fwd(x, a_norm, w_in, conv_w, w_out, n_seq, seq, later_shards):
    tm = ROW_TILE
    nt = seq // tm
    n_col = w_in.shape[2]
    n_later = len(later_shards)

    def body(x_ref, an_ref, win_ref, cw_ref, wout_ref, *rest):
        shard_refs, rest = rest[:n_later], rest[n_later:]
        x1_ref, proj_ref, conv_ref = rest[:3]
        stack_refs, rest = rest[3:3 + n_later], rest[3 + n_later:]
        prev_ref, wide_ref, sems = rest[0], rest[1], rest[2:]
        step = pl.program_id(0) * nt + pl.program_id(1)

        @pl.when(step == 0)
        def _():
            for cp in _gather_copies(shard_refs, stack_refs, sems)[0]:
                cp.start()

        @pl.when(pl.program_id(1) == 0)
        def _():
            prev_ref[...] = jnp.zeros_like(prev_ref)

        xv = x_ref[...]
        h = (xv * _rstd(xv) * an_ref[...]).astype(BF16)
        for d in range(N_DEV):
            part = jnp.dot(h, win_ref[d], preferred_element_type=F32)
            wide_ref[:, d * n_col:(d + 1) * n_col] = part
            proj_ref[:, d * n_col:(d + 1) * n_col] = part.astype(SAVED)
        b = wide_ref[:, 0:D_MODEL]
        v = wide_ref[:, D_MODEL:2 * D_MODEL] * wide_ref[:, 2 * D_MODEL:3 * D_MODEL]
        g = wide_ref[:, 3 * D_MODEL:4 * D_MODEL]
        w0, w1, w2 = cw_ref[0:1, :], cw_ref[1:2, :], cw_ref[2:3, :]
        conv_ref[...] = w0 * pltpu.roll(v, 2, 0) + w1 * pltpu.roll(v, 1, 0) + w2 * v
        rows = lax.broadcasted_iota(jnp.int32, (8, D_MODEL), 0)
        p8, v8 = prev_ref[...], v[0:8]
        back1 = jnp.where(rows < 1, pltpu.roll(p8, 1, 0), pltpu.roll(v8, 1, 0))
        back2 = jnp.where(rows < 2, pltpu.roll(p8, 2, 0), pltpu.roll(v8, 2, 0))
        conv_ref[0:8, :] = w0 * back2 + w1 * back1 + w2 * v8
        prev_ref[...] = v[tm - 8:tm]
        silu, _ = _silu_parts(g)
        yv = silu * b * conv_ref[...]
        x1_ref[...] = xv + _dot(yv, wout_ref[...])

        @pl.when(step == n_seq * nt - 1)
        def _():
            for cp in _gather_copies(shard_refs, stack_refs, sems)[1]:
                cp.wait()

    tok = lambda width: pl.BlockSpec((tm, width), lambda s, i: (s * nt + i, 0))
    t = n_seq * seq
    outs = pl.pallas_call(
        body, name="conv_mixer_fwd", grid=(n_seq, nt),
        in_specs=[tok(D_MODEL), _const_spec((1, D_MODEL)), _const_spec(w_in.shape), _const_spec((3, D_MODEL)),
                  _const_spec(w_out.shape)] + [ANY_SPEC] * n_later,
        out_specs=[tok(D_MODEL), tok(4 * D_MODEL), tok(D_MODEL)] + [ANY_SPEC] * n_later,
        out_shape=[jax.ShapeDtypeStruct((t, D_MODEL), F32), jax.ShapeDtypeStruct((t, 4 * D_MODEL), SAVED),
                   jax.ShapeDtypeStruct((t, D_MODEL), F32)] + _stacked(later_shards),
        scratch_shapes=[pltpu.VMEM((8, D_MODEL), F32), pltpu.VMEM((tm, 4 * D_MODEL), F32)] + _comm_sems(n_later),
        compiler_params=_params(2),
    )(x, a_norm, w_in, conv_w, w_out, *later_shards)
    return outs[0], outs[1], outs[2], outs[3:]


def _conv_bwd(dx1, x, proj, conv, a_norm, w_in, conv_w, w_out, n_seq, seq, ready_stacks):
    tm = ROW_TILE
    nt = seq // tm
    n_col = w_in.shape[2]
    n_ready = len(ready_stacks)

    def body(dx1_ref, x_ref, proj_ref, conv_ref, an_ref, win_ref, cw_ref, wout_ref, *rest):
        ready_refs, rest = rest[:n_ready], rest[n_ready:]
        dx_ref, h_ref, dproj_ref, dwout_ref, dan_ref, dcw_ref = rest[:6]
        part_refs, rest = rest[6:6 + n_ready], rest[6 + n_ready:]
        next_ref, d1_ref, d2_ref = rest[:3]
        sems = rest[3:]
        step = pl.program_id(0) * nt + pl.program_id(1)
        first = step == 0

        @pl.when(first)
        def _():
            for cp in _scatter_copies(ready_refs, part_refs, sems)[0]:
                cp.start()
            dwout_ref[...] = jnp.zeros_like(dwout_ref)
            dan_ref[...] = jnp.zeros_like(dan_ref)
            dcw_ref[...] = jnp.zeros_like(dcw_ref)

        @pl.when(pl.program_id(1) == 0)
        def _():
            next_ref[...] = jnp.zeros_like(next_ref)

        dx1v = dx1_ref[...]
        dy = _dot_nt(dx1v, wout_ref[...])
        b = proj_ref[:, 0:D_MODEL].astype(F32)
        cc = proj_ref[:, D_MODEL:2 * D_MODEL].astype(F32)
        u = proj_ref[:, 2 * D_MODEL:3 * D_MODEL].astype(F32)
        g = proj_ref[:, 3 * D_MODEL:4 * D_MODEL].astype(F32)
        cv = conv_ref[...]
        silu, dsilu = _silu_parts(g)
        per_part = D_MODEL // n_col

        def back_through_w_in(part, grad):
            grad = grad.astype(BF16)
            dproj_ref[:, part * D_MODEL:(part + 1) * D_MODEL] = grad
            term = None
            for j in range(per_part):
                d = part * per_part + j
                piece = lax.dot_general(grad[:, j * n_col:(j + 1) * n_col], win_ref[d], NT, preferred_element_type=F32)
                term = piece if term is None else term + piece
            return term

        dh = back_through_w_in(0, dy * silu * cv)
        dh += back_through_w_in(3, dy * b * cv * dsilu)
        dwout_ref[...] += _dot_tn(silu * b * cv, dx1v)
        dconv = dy * silu * b
        d1_ref[...] = pltpu.roll(dconv, tm - 1, 0)
        d2_ref[...] = pltpu.roll(dconv, tm - 2, 0)
        rows = lax.broadcasted_iota(jnp.int32, (8, D_MODEL), 0)
        n8, c8 = next_ref[...], dconv[tm - 8:tm]
        d1_ref[tm - 8:tm, :] = jnp.where(rows >= 7, pltpu.roll(n8, 7, 0), pltpu.roll(c8, 7, 0))
        d2_ref[tm - 8:tm, :] = jnp.where(rows >= 6, pltpu.roll(n8, 6, 0), pltpu.roll(c8, 6, 0))
        next_ref[...] = dconv[0:8]
        d1, d2 = d1_ref[...], d2_ref[...]
        v = cc * u
        dcw_ref[0:1, :] += jnp.sum(d2 * v, axis=0, keepdims=True)
        dcw_ref[1:2, :] += jnp.sum(d1 * v, axis=0, keepdims=True)
        dcw_ref[2:3, :] += jnp.sum(dconv * v, axis=0, keepdims=True)
        dv = cw_ref[0:1, :] * d2 + cw_ref[1:2, :] * d1 + cw_ref[2:3, :] * dconv
        dh += back_through_w_in(1, dv * u)
        dh += back_through_w_in(2, dv * cc)
        xv = x_ref[...]
        r = _rstd(xv)
        xh = xv * r
        h_ref[...] = (xh * an_ref[...]).T.astype(BF16)
        dan_ref[...] += jnp.sum(dh * xh, axis=0, keepdims=True)
        dx_ref[...] = dx1v + _norm_bwd(dh * an_ref[...], xh, r)

        @pl.when(step == n_seq * nt - 1)
        def _():
            for cp in _scatter_copies(ready_refs, part_refs, sems)[1]:
                cp.wait()

    tok = lambda width: pl.BlockSpec((tm, width), lambda s, i: (s * nt + nt - 1 - i, 0))
    t = n_seq * seq
    outs = pl.pallas_call(
        body, name="conv_mixer_bwd", grid=(n_seq, nt),
        in_specs=[tok(D_MODEL), tok(D_MODEL), tok(4 * D_MODEL), tok(D_MODEL), _const_spec((1, D_MODEL)),
                  _const_spec(w_in.shape), _const_spec((3, D_MODEL)), _const_spec(w_out.shape)] + [ANY_SPEC] * n_ready,
        out_specs=[tok(D_MODEL), pl.BlockSpec((D_MODEL, tm), lambda s, i: (0, s * nt + nt - 1 - i)),
                   tok(4 * D_MODEL), _acc_spec((D_MODEL, D_MODEL)),
                   _acc_spec((1, D_MODEL)), _acc_spec((8, D_MODEL))] + [ANY_SPEC] * n_ready,
        out_shape=[jax.ShapeDtypeStruct((t, D_MODEL), F32), jax.ShapeDtypeStruct((D_MODEL, t), BF16),
                   jax.ShapeDtypeStruct((t, 4 * D_MODEL), BF16), jax.ShapeDtypeStruct((D_MODEL, D_MODEL), F32),
                   jax.ShapeDtypeStruct((1, D_MODEL), F32), jax.ShapeDtypeStruct((8, D_MODEL), F32)]
        + [jax.ShapeDtypeStruct(a.shape, a.dtype) for a in ready_stacks],
        scratch_shapes=[pltpu.VMEM((8, D_MODEL), F32), pltpu.VMEM((tm, D_MODEL), F32), pltpu.VMEM((tm, D_MODEL), F32)]
        + _comm_sems(n_ready),
        compiler_params=_params(2),
    )(dx1, x, proj, conv, a_norm, w_in, conv_w, w_out, *ready_stacks)
    return outs[:6], outs[6:]


def _w_in_grad_exchange(a_t, b, n_col, chip, ready_stacks, small):
    r, t = a_t.shape
    bt = 2048
    nk = t // bt
    n_ready = len(ready_stacks)
    n_chip = N_DEV // 2
    n_remote = N_DEV - 2

    def body(chip_ref, a_ref, b_ref, *rest):
        ready_refs, small_ref, rest = rest[:n_ready], rest[n_ready], rest[n_ready + 1:]
        own_ref, got_ref, rest = rest[0], rest[1], rest[2:]
        part_refs, small_all_ref, rest = rest[:n_ready], rest[n_ready], rest[n_ready + 1:]
        acc_ref, land_ref, stage_ref, pair_send, pair_recv, chip_send, chip_recv = rest[:7]
        scatter_sems, gather_sems = rest[7:10], rest[10:13]
        s, k = pl.program_id(0), pl.program_id(1)
        x, y, c, _ = _place()
        chip = 2 * x + y

        @pl.when(jnp.logical_and(s == 0, k == 0))
        def _():
            for cp in _scatter_copies(ready_refs, part_refs, scatter_sems)[0]:
                cp.start()
            for cp in _gather_copies([small_ref], [small_all_ref], gather_sems)[0]:
                cp.start()

        for parity in range(2):
            @pl.when(s % 2 == parity)
            def _(parity=parity):
                @pl.when(k == 0)
                def _():
                    acc_ref[parity] = jnp.zeros((r, n_col), F32)

                acc_ref[parity] += jnp.dot(a_ref[...], b_ref[...], preferred_element_type=F32)

        def to_sibling(step):
            return pltpu.make_async_remote_copy(
                src_ref=acc_ref.at[step % 2], dst_ref=land_ref, send_sem=pair_send.at[step], recv_sem=pair_recv.at[step],
                device_id=(x, y, 1 - c), device_id_type=MESH)

        def to_owner(nth):
            owner_chip = (chip + 1 + nth) % n_chip
            slot = jnp.bitwise_xor(chip, owner_chip) - 1
            return pltpu.make_async_remote_copy(
                src_ref=stage_ref.at[nth % 2], dst_ref=got_ref.at[slot], send_sem=chip_send.at[nth],
                recv_sem=chip_recv.at[slot], device_id=(owner_chip // 2, owner_chip % 2, c), device_id_type=MESH)

        for step in range(N_DEV):
            owner_core = step % 2

            @pl.when(jnp.logical_and(s == step, k == nk - 1))
            def _(step=step, owner_core=owner_core):
                @pl.when(c != owner_core)
                def _():
                    to_sibling(step).start()

                if step >= 1:
                    @pl.when(c == owner_core)
                    def _():
                        to_sibling(step - 1).wait_send()

            if step < N_DEV - 1:
                pickup = jnp.logical_and(s == step + 1, k == min(PICKUP_BLOCK, nk - 1))
            else:
                pickup = jnp.logical_and(s == step, k == nk - 1)

            @pl.when(jnp.logical_and(pickup, c == owner_core))
            def _(step=step):
                to_sibling(step).wait_recv()
                total = acc_ref[step % 2] + land_ref[...]
                if step < n_remote:
                    nth = step // 2
                    if nth >= 2:
                        to_owner(nth - 2).wait_send()
                    stage_ref[nth % 2] = total.astype(BF16)
                    to_owner(nth).start()
                else:
                    own_ref[...] = total

        @pl.when(jnp.logical_and(s == N_DEV - 1, k == nk - 1))
        def _():
            @pl.when(c != (N_DEV - 1) % 2)
            def _():
                to_sibling(N_DEV - 1).wait_send()

            to_owner(1).wait_send()
            to_owner(2).wait_send()
            for slot in range(n_chip - 1):
                pltpu.make_async_remote_copy(
                    src_ref=stage_ref.at[0], dst_ref=got_ref.at[slot], send_sem=chip_send.at[0],
                    recv_sem=chip_recv.at[slot], device_id=(x, y, c), device_id_type=MESH).wait_recv()
            for cp in _scatter_copies(ready_refs, part_refs, scatter_sems)[1]:
                cp.wait()
            for cp in _gather_copies([small_ref], [small_all_ref], gather_sems)[1]:
                cp.wait()

    def owner_block(s, k, chip):
        return (k, 2 * ((chip[0] + 1 + s // 2) % n_chip) + s % 2)

    grid_spec = pltpu.PrefetchScalarGridSpec(
        num_scalar_prefetch=1, grid=(N_DEV, nk),
        in_specs=[pl.BlockSpec((r, bt), lambda s, k, chip: (0, k)), pl.BlockSpec((bt, n_col), owner_block)]
        + [ANY_SPEC] * (n_ready + 1),
        out_specs=[pl.BlockSpec((r, n_col), lambda s, k, chip: (0, 0)), ANY_SPEC] + [ANY_SPEC] * (n_ready + 1),
        scratch_shapes=[pltpu.VMEM((2, r, n_col), F32), pltpu.VMEM((r, n_col), F32), pltpu.VMEM((2, r, n_col), BF16),
                        pltpu.SemaphoreType.DMA((N_DEV,)), pltpu.SemaphoreType.DMA((N_DEV,)),
                        pltpu.SemaphoreType.DMA((n_chip - 1,)), pltpu.SemaphoreType.DMA((n_chip - 1,))]
        + _comm_sems(n_ready) + _comm_sems(1))
    outs = pl.pallas_call(
        body, name="w_in_grad_exchange", grid_spec=grid_spec,
        out_shape=[jax.ShapeDtypeStruct((r, n_col), F32), jax.ShapeDtypeStruct((n_chip - 1, r, n_col), BF16)]
        + [jax.ShapeDtypeStruct(p.shape, p.dtype) for p in ready_stacks] + _stacked([small]),
        compiler_params=_params(2),
    )(chip, a_t, b, *ready_stacks, small)
    return outs[0], outs[1], outs[2:2 + n_ready], outs[2 + n_ready]


def _proj_fwd(x1, pos, rc, kv_norm, w_dkv, ckv_norm, w_ukv, b_norm, b_w_in, q_norm, w_uq):
    t = x1.shape[0]
    tm = LATENT_ROW_TILE

    def body(x1_ref, pos_ref, rc_ref, kvn_ref, wdkv_ref, ckvn_ref, wukv_ref, bn_ref, bwin_ref, qn_ref, wuq_ref,
             ckr_ref, cq_ref, gb_ref, q_ref, k_ref, v_ref, cos_ref, sin_ref):
        xv = x1_ref[...]
        xh = xv * _rstd(xv)
        ckr = _dot(xh * kvn_ref[...], wdkv_ref[...])
        pb = _dot(xh * bn_ref[...], bwin_ref[...])
        ckr_ref[...] = ckr
        ckv = ckr[:, 0:KV_RANK]
        kv = _dot(ckv * _rstd(ckv) * ckvn_ref[...], wukv_ref[...])
        cosv, sinv = _rope_angles(pos_ref[...], rc_ref[...])
        cos_ref[...] = cosv
        sin_ref[...] = sinv
        ct, s1, s2 = _rope_tables(cosv, sinv, rc_ref[...])
        k_rope = _rope(ckr[:, KV_RANK:CKR_PAD], ct, s1, s2)
        lane = lax.broadcasted_iota(jnp.int32, (tm, HEAD_PAD), 1)
        low = lane < QK_NOPE
        for h in range(N_HEADS):
            kv_h = kv[:, h * HEAD_PAD:(h + 1) * HEAD_PAD]
            k_ref[:, h * HEAD_PAD:(h + 1) * HEAD_PAD] = jnp.where(low, kv_h, k_rope).astype(BF16)
            v_ref[:, h * HEAD_PAD:(h + 1) * HEAD_PAD] = jnp.where(low, pltpu.roll(kv_h, V_HEAD, 1), 1.0).astype(BF16)
        cq = pb[:, 0:Q_RANK]
        cq_ref[...] = cq
        gb_ref[...] = pb[:, Q_RANK:Q_RANK + E_B]
        q = _dot(cq * _rstd(cq) * qn_ref[...], wuq_ref[...])
        for h in range(N_HEADS):
            q_ref[:, h * HEAD_PAD:(h + 1) * HEAD_PAD] = _rope(
                q[:, h * HEAD_PAD:(h + 1) * HEAD_PAD], ct, s1, s2).astype(BF16)

    tok = lambda width: pl.BlockSpec((tm, width), lambda i: (i, 0))
    weights = [kv_norm, w_dkv, ckv_norm, w_ukv, b_norm, b_w_in, q_norm, w_uq]
    wide = N_HEADS * HEAD_PAD
    return pl.pallas_call(
        body, name="latent_proj_fwd", grid=(t // tm,),
        in_specs=[tok(D_MODEL), tok(1), _const_spec(rc.shape)] + [_const_spec(w.shape) for w in weights],
        out_specs=[tok(CKR_PAD), tok(Q_RANK), tok(E_B), tok(wide), tok(wide), tok(wide), tok(HEAD_PAD), tok(HEAD_PAD)],
        out_shape=[jax.ShapeDtypeStruct((t, CKR_PAD), F32), jax.ShapeDtypeStruct((t, Q_RANK), F32),
                   jax.ShapeDtypeStruct((t, E_B), F32), jax.ShapeDtypeStruct((t, wide), BF16),
                   jax.ShapeDtypeStruct((t, wide), BF16), jax.ShapeDtypeStruct((t, wide), BF16),
                   jax.ShapeDtypeStruct((t, HEAD_PAD), F32), jax.ShapeDtypeStruct((t, HEAD_PAD), F32)],
        compiler_params=_params(1),
    )(x1, pos, rc, *weights)


def _proj_bwd(dq, dk, dv, dgb, cq, ckr, x1, dx2, cosv, sinv, rc, kv_norm, w_dkv, ckv_norm, w_ukv, b_norm, b_w_in, q_norm, w_uq):
    t = x1.shape[0]
    tm = LATENT_ROW_TILE
    wide = N_HEADS * HEAD_PAD

    def body(dq_ref, dk_ref, dv_ref, dgb_ref, cq_ref, ckr_ref, x1_ref, dx2_ref, cos_ref, sin_ref, rc_ref,
             kvn_ref, wdkv_ref, ckvn_ref, wukv_ref, bn_ref, bwin_ref, qn_ref, wuq_ref,
             dx1_ref, dwuq_ref, dwbin_ref, dwukv_ref, dwdkv_ref, dqn_ref, dbn_ref, dckvn_ref, dkvn_ref, dqu_ref, dkv_ref):
        @pl.when(pl.program_id(0) == 0)
        def _():
            for ref in (dwuq_ref, dwbin_ref, dwukv_ref, dwdkv_ref, dqn_ref, dbn_ref, dckvn_ref, dkvn_ref):
                ref[...] = jnp.zeros_like(ref)

        ct, s1, s2 = _rope_tables(cos_ref[...], sin_ref[...], rc_ref[...])
        lane = lax.broadcasted_iota(jnp.int32, (tm, HEAD_PAD), 1)
        low = lane < QK_NOPE
        xv = x1_ref[...]
        r = _rstd(xv)
        xh = xv * r
        dk_rope = jnp.zeros((tm, HEAD_PAD), F32)
        for h in range(N_HEADS):
            dk_h = dk_ref[:, h * HEAD_PAD:(h + 1) * HEAD_PAD].astype(F32)
            dv_h = dv_ref[:, h * HEAD_PAD:(h + 1) * HEAD_PAD].astype(F32)
            dkv_ref[:, h * HEAD_PAD:(h + 1) * HEAD_PAD] = jnp.where(low, dk_h, pltpu.roll(dv_h, V_HEAD, 1)).astype(BF16)
            dk_rope += dk_h
        rope_lanes = jnp.logical_and(lane >= ROPE_LO, lane < ROPE_LO + QK_ROPE)
        dk_rope = jnp.where(rope_lanes, _rope(dk_rope, ct, -s1, -s2), 0.0)
        ckv = ckr_ref[:, 0:KV_RANK]
        rk = _rstd(ckv)
        ckh = ckv * rk
        ckvn = (ckh * ckvn_ref[...]).astype(BF16)
        dwukv = lax.dot_general(ckvn, dkv_ref[...], TN, preferred_element_type=F32)
        for h in range(N_HEADS):
            dwukv_ref[h] += dwukv[:, h * HEAD_PAD:(h + 1) * HEAD_PAD]
        dckvn = lax.dot_general(dkv_ref[...], wukv_ref[...], NT, preferred_element_type=F32)
        dckvn_ref[...] += jnp.sum(dckvn * ckh, axis=0, keepdims=True)
        dckv = _norm_bwd(dckvn * ckvn_ref[...], ckh, rk)
        dckr = jnp.concatenate([dckv, dk_rope], axis=1).astype(BF16)
        dwdkv_ref[...] += _dot_tn(xh * kvn_ref[...], dckr)
        dh2 = lax.dot_general(dckr, wdkv_ref[...], NT, preferred_element_type=F32)
        for h in range(N_HEADS):
            dqu_ref[:, h * HEAD_PAD:(h + 1) * HEAD_PAD] = _rope(
                dq_ref[:, h * HEAD_PAD:(h + 1) * HEAD_PAD].astype(F32), ct, -s1, -s2).astype(BF16)
        cq = cq_ref[...]
        rq = _rstd(cq)
        cqh = cq * rq
        cqn = (cqh * qn_ref[...]).astype(BF16)
        dwuq = lax.dot_general(cqn, dqu_ref[...], TN, preferred_element_type=F32)
        for h in range(N_HEADS):
            dwuq_ref[h] += dwuq[:, h * HEAD_PAD:(h + 1) * HEAD_PAD]
        dcqn = lax.dot_general(dqu_ref[...], wuq_ref[...], NT, preferred_element_type=F32)
        dqn_ref[...] += jnp.sum(dcqn * cqh, axis=0, keepdims=True)
        dcq = _norm_bwd(dcqn * qn_ref[...], cqh, rq)
        dpb = jnp.concatenate([dcq.astype(BF16), dgb_ref[...]], axis=1)
        dwbin_ref[...] += _dot_tn(xh * bn_ref[...], dpb)
        dh3 = lax.dot_general(dpb, bwin_ref[...], NT, preferred_element_type=F32)
        dkvn_ref[...] += jnp.sum(dh2 * xh, axis=0, keepdims=True)
        dbn_ref[...] += jnp.sum(dh3 * xh, axis=0, keepdims=True)
        dx1_ref[...] = dx2_ref[...] + _norm_bwd(dh2 * kvn_ref[...] + dh3 * bn_ref[...], xh, r)

    tok = lambda width: pl.BlockSpec((tm, width), lambda i: (i, 0))
    weights = [kv_norm, w_dkv, ckv_norm, w_ukv, b_norm, b_w_in, q_norm, w_uq]
    acc_shapes = [(N_HEADS, Q_RANK, HEAD_PAD), (D_MODEL, Q_RANK + E_B), (N_HEADS, KV_RANK, HEAD_PAD), (D_MODEL, CKR_PAD),
                  (1, Q_RANK), (1, D_MODEL), (1, KV_RANK), (1, D_MODEL)]
    return pl.pallas_call(
        body, name="latent_proj_bwd", grid=(t // tm,),
        in_specs=[tok(wide), tok(wide), tok(wide), tok(E_B), tok(Q_RANK), tok(CKR_PAD), tok(D_MODEL), tok(D_MODEL),
                  tok(HEAD_PAD), tok(HEAD_PAD), _const_spec(rc.shape)] + [_const_spec(w.shape) for w in weights],
        out_specs=[tok(D_MODEL)] + [_acc_spec(s) for s in acc_shapes],
        out_shape=[jax.ShapeDtypeStruct((t, D_MODEL), F32)] + [jax.ShapeDtypeStruct(s, F32) for s in acc_shapes],
        scratch_shapes=[pltpu.VMEM((tm, wide), BF16), pltpu.VMEM((tm, wide), BF16)],
        compiler_params=_params(1),
    )(dq, dk, dv, dgb, cq, ckr, x1, dx2, cosv, sinv, rc, *weights)


def _attn_fwd(q, k, v, n_seq, seq):
    tb, group = ATTN_TILE, ATTN_GROUP
    rows = tb * group
    nq = seq // rows
    pair = 2 * HEAD_PAD

    def body(q_ref, k_ref, v_ref, o_ref, lr_ref):
        i = pl.program_id(2)
        row = lax.broadcasted_iota(jnp.int32, (tb, tb), 0)
        col = lax.broadcasted_iota(jnp.int32, (tb, tb), 1)
        causal = col <= row

        def step(j, carry, kinds):
            start = pl.multiple_of(j * tb, tb)
            out = []
            for g in range(group):
                for hh in range(2):
                    m, acc = carry[2 * g + hh]
                    if kinds[g] != "none":
                        heads = slice(hh * HEAD_PAD, (hh + 1) * HEAD_PAD)
                        s = lax.dot_general(q_ref[g * tb:(g + 1) * tb, heads], k_ref[pl.ds(start, tb), heads], NT,
                                            preferred_element_type=F32) * (SOFTMAX_SCALE * LOG2_E)
                        if kinds[g] == "diagonal":
                            s = jnp.where(causal, s, -jnp.inf)
                        m_new = jnp.maximum(m, jnp.max(s, axis=-1, keepdims=True))
                        p = jnp.exp2(s - m_new)
                        acc = jnp.exp2(m - m_new) * acc + jnp.dot(
                            p.astype(BF16), v_ref[pl.ds(start, tb), heads], preferred_element_type=F32)
                        m = m_new
                    out.append((m, acc))
            return tuple(out)

        one = (jnp.full((tb, 1), -jnp.inf, F32), jnp.zeros((tb, HEAD_PAD), F32))
        carry = lax.fori_loop(0, i * group, functools.partial(step, kinds=("full",) * group), (one,) * (2 * group))
        for d in range(group):
            kinds = tuple("none" if g < d else ("diagonal" if g == d else "full") for g in range(group))
            carry = step(i * group + d, carry, kinds)
        lane = lax.broadcasted_iota(jnp.int32, (tb, HEAD_PAD), 1)
        low = lane < V_HEAD
        for g in range(group):
            halves = []
            for hh in range(2):
                m, acc = carry[2 * g + hh]
                swapped = pltpu.roll(acc, V_HEAD, 1)
                halves.append(acc / swapped)
                lse = (m * LN_2) + jnp.log(jnp.where(low, swapped, acc))
                lr_ref[hh, :, g * tb:(g + 1) * tb] = lse.T[0:8, :]
            o_ref[g * tb:(g + 1) * tb, :] = jnp.where(low, halves[0], pltpu.roll(halves[1], V_HEAD, 1))

    t = n_seq * seq
    return pl.pallas_call(
        body, name="attention_fwd", grid=(n_seq, N_HEADS // 2, nq),
        in_specs=[pl.BlockSpec((rows, pair), lambda s, p, i: (s * nq + i, p)),
                  pl.BlockSpec((seq, pair), lambda s, p, i: (s, p)),
                  pl.BlockSpec((seq, pair), lambda s, p, i: (s, p))],
        out_specs=[pl.BlockSpec((rows, HEAD_PAD), lambda s, p, i: (s * nq + i, p)),
                   pl.BlockSpec((None, 2, 8, rows), lambda s, p, i: (s, p, 0, i))],
        out_shape=[jax.ShapeDtypeStruct((t, E_B), F32), jax.ShapeDtypeStruct((n_seq, N_HEADS, 8, seq), F32)],
        compiler_params=_params(3),
    )(q, k, v)


def _attn_bwd(q, k, v, do, lse_row, delta_row, n_seq, seq):
    tb, group = ATTN_TILE, ATTN_BWD_GROUP
    rows = tb * group
    nk = seq // rows
    n_inner = seq // tb
    pair = 2 * HEAD_PAD

    def body(q_ref, k_ref, v_ref, do_ref, lr_ref, dr_ref, dk_ref, dv_ref, dq_out_ref, dq_ref):
        j = pl.program_id(2)

        @pl.when(j == 0)
        def _():
            dq_ref[...] = jnp.zeros_like(dq_ref)

        row = lax.broadcasted_iota(jnp.int32, (tb, tb), 0)
        col = lax.broadcasted_iota(jnp.int32, (tb, tb), 1)
        causal = col >= row

        def step(i, carry, kinds):
            start = pl.multiple_of(i * tb, tb)
            out = []
            for hh in range(2):
                heads = slice(hh * HEAD_PAD, (hh + 1) * HEAD_PAD)
                qi = q_ref[pl.ds(start, tb), heads]
                doi = do_ref[pl.ds(start, tb), heads]
                dq = None
                for g in range(group):
                    dk_acc, dv_acc = carry[hh * group + g]
                    if kinds[g] != "none":
                        kg = k_ref[g * tb:(g + 1) * tb, heads]
                        st = lax.dot_general(kg, qi, NT, preferred_element_type=F32) * SOFTMAX_SCALE
                        pt = jnp.exp(st - lr_ref[hh, 0:1, pl.ds(start, tb)])
                        if kinds[g] == "diagonal":
                            pt = jnp.where(causal, pt, 0.0)
                        dv_acc = dv_acc + jnp.dot(pt.astype(BF16), doi, preferred_element_type=F32)
                        dpt = lax.dot_general(v_ref[g * tb:(g + 1) * tb, heads], doi, NT, preferred_element_type=F32)
                        dst = (pt * (dpt - dr_ref[hh, 0:1, pl.ds(start, tb)]) * SOFTMAX_SCALE).astype(BF16)
                        dk_acc = dk_acc + jnp.dot(dst, qi, preferred_element_type=F32)
                        term = lax.dot_general(dst, kg, TN, preferred_element_type=F32)
                        dq = term if dq is None else dq + term
                    out.append((dk_acc, dv_acc))
                dq_ref[pl.ds(start, tb), heads] += dq
            return tuple(out)

        one = (jnp.zeros((tb, HEAD_PAD), F32), jnp.zeros((tb, HEAD_PAD), F32))
        carry = (one,) * (2 * group)
        for d in range(group):
            kinds = tuple("full" if g < d else ("diagonal" if g == d else "none") for g in range(group))
            carry = step(j * group + d, carry, kinds)
        carry = lax.fori_loop((j + 1) * group, n_inner, functools.partial(step, kinds=("full",) * group), carry)
        for hh in range(2):
            for g in range(group):
                dk_ref[g * tb:(g + 1) * tb, hh * HEAD_PAD:(hh + 1) * HEAD_PAD] = carry[hh * group + g][0].astype(BF16)
                dv_ref[g * tb:(g + 1) * tb, hh * HEAD_PAD:(hh + 1) * HEAD_PAD] = carry[hh * group + g][1].astype(BF16)

        @pl.when(j == nk - 1)
        def _():
            dq_out_ref[...] = dq_ref[...].astype(BF16)

    t = n_seq * seq
    wide = N_HEADS * HEAD_PAD
    return pl.pallas_call(
        body, name="attention_bwd", grid=(n_seq, N_HEADS // 2, nk),
        in_specs=[pl.BlockSpec((seq, pair), lambda s, p, j: (s, p)),
                  pl.BlockSpec((rows, pair), lambda s, p, j: (s * nk + j, p)),
                  pl.BlockSpec((rows, pair), lambda s, p, j: (s * nk + j, p)),
                  pl.BlockSpec((seq, pair), lambda s, p, j: (s, p)),
                  pl.BlockSpec((None, 2, 8, seq), lambda s, p, j: (s, p, 0, 0)),
                  pl.BlockSpec((None, 2, 8, seq), lambda s, p, j: (s, p, 0, 0))],
        out_specs=[pl.BlockSpec((rows, pair), lambda s, p, j: (s * nk + j, p)),
                   pl.BlockSpec((rows, pair), lambda s, p, j: (s * nk + j, p)),
                   pl.BlockSpec((seq, pair), lambda s, p, j: (s, p))],
        out_shape=[jax.ShapeDtypeStruct((t, wide), BF16), jax.ShapeDtypeStruct((t, wide), BF16),
                   jax.ShapeDtypeStruct((t, wide), BF16)],
        scratch_shapes=[pltpu.VMEM((seq, pair), F32)],
        compiler_params=_params(3),
    )(q, k, v, do, lse_row, delta_row)


def _head_tail(o, gb, x1, w_out, final_norm, target, n_seq, seq):
    tm = LATENT_ROW_TILE
    nt = seq // tm
    n_col = D_MODEL // N_DEV

    head_of_column = np.arange(E_B) // V_HEAD
    selector = jnp.asarray((head_of_column[None, :] == np.arange(N_HEADS)[:, None]).astype(np.float32))

    def body(o_ref, gb_ref, x1_ref, wout_ref, fn_ref, tgt_ref, sel_ref,
             dx2_ref, do_ref, dgb_ref, dr_ref, dwout_ref, dfn_ref, loss_ref):
        first = jnp.logical_and(pl.program_id(0) == 0, pl.program_id(1) == 0)

        @pl.when(first)
        def _():
            dwout_ref[...] = jnp.zeros_like(dwout_ref)
            dfn_ref[...] = jnp.zeros_like(dfn_ref)
            loss_ref[...] = jnp.zeros_like(loss_ref)

        ov, g = o_ref[...], gb_ref[...]
        silu, dsilu = _silu_parts(g)
        gated = (ov * silu).astype(BF16)
        x2 = x1_ref[...] + jnp.dot(gated, wout_ref[...], preferred_element_type=F32)
        r = _rstd(x2)
        xh = x2 * r
        err = xh * fn_ref[...] - tgt_ref[...]
        loss_ref[...] += 0.5 * jnp.sum(jnp.mean(err * err, axis=-1, keepdims=True), axis=0, keepdims=True)
        dy = err / D_MODEL
        dfn_ref[...] += jnp.sum(dy * xh, axis=0, keepdims=True)
        dx2 = _norm_bwd(dy * fn_ref[...], xh, r)
        dx2_ref[...] = dx2
        dx2b = dx2.astype(BF16)
        dw = lax.dot_general(gated, dx2b, TN, preferred_element_type=F32)
        for d in range(N_DEV):
            dwout_ref[d] += dw[:, d * n_col:(d + 1) * n_col]
        dgated = lax.dot_general(dx2b, wout_ref[...], NT, preferred_element_type=F32)
        do = dgated * silu
        dgb_ref[...] = (dgated * ov * dsilu).astype(BF16)
        delta = lax.dot_general(sel_ref[...], do * ov, NT, precision=lax.Precision.HIGHEST, preferred_element_type=F32)
        lane = lax.broadcasted_iota(jnp.int32, (tm, HEAD_PAD), 1)
        low = lane < V_HEAD
        for p in range(N_HEADS // 2):
            do_pair = do[:, p * HEAD_PAD:(p + 1) * HEAD_PAD]
            for hh in range(2):
                h = 2 * p + hh
                mine = do_pair if hh == 0 else pltpu.roll(do_pair, V_HEAD, 1)
                do_ref[:, h * HEAD_PAD:(h + 1) * HEAD_PAD] = jnp.where(low, mine, 0.0).astype(BF16)
                dr_ref[h] = jnp.broadcast_to(delta[h:h + 1, :], (8, tm))

    tok = lambda width: pl.BlockSpec((tm, width), lambda s, i: (s * nt + i, 0))
    t = n_seq * seq
    return pl.pallas_call(
        body, name="head_tail", grid=(n_seq, nt),
        in_specs=[tok(E_B), tok(E_B), tok(D_MODEL), _const_spec(w_out.shape), _const_spec((1, D_MODEL)), tok(D_MODEL),
                  _const_spec((N_HEADS, E_B))],
        out_specs=[tok(D_MODEL), tok(N_HEADS * HEAD_PAD), tok(E_B),
                   pl.BlockSpec((None, N_HEADS, 8, tm), lambda s, i: (s, 0, 0, i)),
                   _acc_spec((N_DEV, E_B, n_col)), _acc_spec((1, D_MODEL)), _acc_spec((1, 1))],
        out_shape=[jax.ShapeDtypeStruct((t, D_MODEL), F32), jax.ShapeDtypeStruct((t, N_HEADS * HEAD_PAD), BF16),
                   jax.ShapeDtypeStruct((t, E_B), BF16),
                   jax.ShapeDtypeStruct((n_seq, N_HEADS, 8, seq), F32), jax.ShapeDtypeStruct((N_DEV, E_B, n_col), F32),
                   jax.ShapeDtypeStruct((1, D_MODEL), F32), jax.ShapeDtypeStruct((1, 1), F32)],
        compiler_params=_params(2),
    )(o, gb, x1, w_out, final_norm, target, selector)


def _adamw_math(w, g, m, v):
    m = ADAM_B1 * m + (1.0 - ADAM_B1) * g
    v = ADAM_B2 * v + (1.0 - ADAM_B2) * jnp.square(g)
    m_hat = m / (1.0 - ADAM_B1 ** ADAM_STEP)
    v_hat = v / (1.0 - ADAM_B2 ** ADAM_STEP)
    delta = -ADAM_LR * (m_hat / (jnp.sqrt(v_hat) + ADAM_EPS) + ADAM_WD * w)
    return delta, m, v


def _adamw_reduce(own, parts, w, m, v):
    rows, cols = w.shape
    br = 256
    n_parts = parts.shape[0]

    def body(own_ref, p_ref, w_ref, m_ref, v_ref, g_ref, d_ref, nm_ref, nv_ref):
        g = own_ref[...]
        for k in range(n_parts):
            g = g + p_ref[k].astype(F32)
        g_ref[...] = g
        d_ref[...], nm_ref[...], nv_ref[...] = _adamw_math(w_ref[...], g, m_ref[...], v_ref[...])

    blk = pl.BlockSpec((br, cols), lambda i: (i, 0))
    return pl.pallas_call(
        body, name="adamw_reduce", grid=(rows // br,),
        in_specs=[blk, pl.BlockSpec((n_parts, br, cols), lambda i: (0, i, 0)), blk, blk, blk],
        out_specs=[blk] * 4, out_shape=[jax.ShapeDtypeStruct((rows, cols), F32)] * 4,
        compiler_params=_params(1),
    )(own, parts, w, m, v)


def _adamw_reduce_many(parts_list, states):
    n = len(parts_list)

    def body(*refs):
        ins, outs = refs[:4 * n], refs[4 * n:]
        for j in range(n):
            p_ref, w_ref, m_ref, v_ref = ins[4 * j:4 * j + 4]
            g = p_ref[0].astype(F32)
            for k in range(1, p_ref.shape[0]):
                g = g + p_ref[k].astype(F32)
            outs[4 * j][...] = g
            outs[4 * j + 1][...], outs[4 * j + 2][...], outs[4 * j + 3][...] = _adamw_math(
                w_ref[...], g, m_ref[...], v_ref[...])

    flat_in = [a for p, st in zip(parts_list, states) for a in (p,) + tuple(st)]
    shapes = [jax.ShapeDtypeStruct(st[0].shape, F32) for st in states for _ in range(4)]
    flat = pl.pallas_call(body, name="adamw_reduce_many", out_shape=shapes,
                          compiler_params=pltpu.CompilerParams(vmem_limit_bytes=VMEM_LIMIT))(*flat_in)
    return [tuple(flat[4 * j:4 * j + 4]) for j in range(n)]


def _sum_parts(parts):
    def body(p_ref, o_ref):
        g = p_ref[0]
        for k in range(1, N_DEV):
            g = g + p_ref[k]
        o_ref[...] = g

    return pl.pallas_call(body, name="small_grad_sum", out_shape=jax.ShapeDtypeStruct(parts.shape[1:], F32))(parts)


def _adamw_small(gs, ws, ms, vs):
    n = len(gs)

    def body(*refs):
        ins, outs = refs[:4 * n], refs[4 * n:]
        for j in range(n):
            g_ref, w_ref, m_ref, v_ref = ins[j], ins[n + j], ins[2 * n + j], ins[3 * n + j]
            outs[3 * j][...], outs[3 * j + 1][...], outs[3 * j + 2][...] = _adamw_math(
                w_ref[...], g_ref[...], m_ref[...], v_ref[...])

    shapes = [jax.ShapeDtypeStruct(g.shape, F32) for g in gs for _ in range(3)]
    flat = pl.pallas_call(body, name="adamw_small", out_shape=shapes)(*gs, *ws, *ms, *vs)
    return [tuple(flat[3 * j:3 * j + 3]) for j in range(n)]


SMALL_ROWS = 8
LOSS_LANE = D_MODEL - 1


def _pack_small(g_fn, g_kvn, g_bn, g_an, g_ckvn, g_qn, g_conv, loss_part):
    def body(fn_ref, kvn_ref, bn_ref, an_ref, ckvn_ref, qn_ref, conv_ref, loss_ref, o_ref):
        o_ref[0:1, :] = fn_ref[...]
        o_ref[1:2, :] = kvn_ref[...]
        o_ref[2:3, :] = bn_ref[...]
        o_ref[3:4, :] = an_ref[...]
        lane = lax.broadcasted_iota(jnp.int32, (1, D_MODEL), 1)
        o_ref[4:5, :] = jnp.where(lane == LOSS_LANE, loss_ref[...], 0.0)
        o_ref[4:5, 0:KV_RANK] = ckvn_ref[...]
        o_ref[4:5, KV_RANK:KV_RANK + Q_RANK] = qn_ref[...]
        o_ref[5:8, :] = conv_ref[0:3, :]

    return pl.pallas_call(body, name="pack_small", out_shape=jax.ShapeDtypeStruct((SMALL_ROWS, D_MODEL), F32))(
        g_fn, g_kvn, g_bn, g_an, g_ckvn, g_qn, g_conv, loss_part)


def _pad_cols(a, width):
    return jnp.pad(a, ((0, 0), (0, width - a.shape[1])))


def _dkv_to_padded(a):
    r = a.shape[0]
    z = jnp.zeros((r, ROPE_LO), a.dtype)
    z2 = jnp.zeros((r, HEAD_PAD - ROPE_LO - QK_ROPE), a.dtype)
    return jnp.concatenate([a[:, :KV_RANK], z, a[:, KV_RANK:], z2], axis=1)


def _dkv_from_padded(a):
    return jnp.concatenate([a[:, :KV_RANK], a[:, KV_RANK + ROPE_LO:KV_RANK + ROPE_LO + QK_ROPE]], axis=1)


def _unstack_cols(a):
    return jnp.transpose(a, (1, 0, 2)).reshape(a.shape[1], N_DEV * a.shape[2])


def kernel(x, positions, a_norm, a_w_in, a_conv, a_w_out, kv_norm, w_dkv, ckv_norm, w_ukv, b_norm, b_w_in, b_q_norm, b_w_uq, b_w_out, final_norm, loss_target, m_a_norm, m_a_w_in, m_a_conv, m_a_w_out, m_kv_norm, m_w_dkv, m_ckv_norm, m_w_ukv, m_b_norm, m_b_w_in, m_b_q_norm, m_b_w_uq, m_b_w_out, m_final_norm, v_a_norm, v_a_w_in, v_a_conv, v_a_w_out, v_kv_norm, v_w_dkv, v_ckv_norm, v_w_ukv, v_b_norm, v_b_w_in, v_b_q_norm, v_b_w_uq, v_b_w_out, v_final_norm):
    n_seq, seq, _ = x.shape
    t = n_seq * seq
    me = 4 * lax.axis_index("x") + 2 * lax.axis_index("y") + lax.axis_index("c")

    big = {
        "a_w_in": (a_w_in[0], m_a_w_in[0], v_a_w_in[0]),
        "a_w_out": (a_w_out[0], m_a_w_out[0], v_a_w_out[0]),
        "w_dkv": tuple(_dkv_to_padded(a) for a in (w_dkv, m_w_dkv, v_w_dkv)),
        "w_ukv": (w_ukv, m_w_ukv, v_w_ukv),
        "b_w_in": (b_w_in[0], m_b_w_in[0], v_b_w_in[0]),
        "b_w_uq": tuple(_pad_cols(a[0], HEAD_PAD) for a in (b_w_uq, m_b_w_uq, v_b_w_uq)),
        "b_w_out": (b_w_out[0], m_b_w_out[0], v_b_w_out[0]),
    }
    names = list(big)
    first_names, later_names = names[:2], names[2:]
    gathered = _all_gather([big[n][0].astype(BF16) for n in first_names] + [a_norm, a_conv[0]])
    a_norm_f = gathered[2].reshape(1, D_MODEL)
    a_conv_f = _unstack_cols(gathered[3])
    w_a_in = gathered[0]
    w_a_out = gathered[1].reshape(D_MODEL, D_MODEL)

    x2d = x.reshape(t, D_MODEL)
    tgt = loss_target.reshape(t, D_MODEL)
    pos = positions.astype(F32).reshape(t, 1)
    rc = _rope_consts()
    kvn, ckvn, bn, qn, fn = kv_norm.reshape(1, -1), ckv_norm.reshape(1, -1), b_norm, b_q_norm, final_norm.reshape(1, -1)

    x1, proj, conv, later = _conv_fwd(x2d, a_norm_f, w_a_in, a_conv_f, w_a_out, n_seq, seq,
                                      [big[n][0].astype(BF16) for n in later_names])
    full = dict(zip(later_names, later))
    w_dkv_f = full["w_dkv"].reshape(D_MODEL, CKR_PAD)
    w_ukv_f = _unstack_cols(full["w_ukv"])
    w_b_in = full["b_w_in"].reshape(D_MODEL, Q_RANK + E_B)
    w_uq_f = _unstack_cols(full["b_w_uq"])
    w_b_out = _unstack_cols(full["b_w_out"])
    layer_b = (kvn, w_dkv_f, ckvn, w_ukv_f, bn, w_b_in, qn, w_uq_f)
    ckr, cq, gb, q, k, v, cosv, sinv = _proj_fwd(x1, pos, rc, *layer_b)
    o, lse_row = _attn_fwd(q, k, v, n_seq, seq)
    dx2, do, dgb, delta_row, g_b_out, g_fn, loss_part = _head_tail(o, gb, x1, w_b_out, fn, tgt, n_seq, seq)
    dk, dv, dq = _attn_bwd(q, k, v, do, lse_row, delta_row, n_seq, seq)
    dx1, g_uq, g_b_in, g_ukv, g_dkv, g_qn, g_bn, g_ckvn, g_kvn = _proj_bwd(
        dq, dk, dv, dgb, cq, ckr, x1, dx2, cosv, sinv, rc, *layer_b)
    stacks = {
        "w_dkv": g_dkv.reshape(N_DEV, D_MODEL // N_DEV, CKR_PAD),
        "w_ukv": g_ukv,
        "b_w_in": g_b_in.reshape(N_DEV, D_MODEL // N_DEV, Q_RANK + E_B),
        "b_w_uq": g_uq,
        "b_w_out": g_b_out,
    }
    (dx, h_a, dproj, g_a_out, g_an, g_conv), later_parts = _conv_bwd(
        dx1, x2d, proj, conv, a_norm_f, w_a_in, a_conv_f, w_a_out, n_seq, seq, [stacks[n] for n in later_names])
    parts = dict(zip(later_names, later_parts))
    small = _pack_small(g_fn, g_kvn, g_bn, g_an, g_ckvn, g_qn, g_conv, loss_part)
    a_in_own, a_in_got, (parts["a_w_out"],), small_parts = _w_in_grad_exchange(
        h_a, dproj, D_MODEL * 4 // N_DEV, (me // 2).reshape(1).astype(jnp.int32),
        [g_a_out.astype(BF16).reshape(N_DEV, D_MODEL // N_DEV, D_MODEL)], small)

    outs = {"a_w_in": _adamw_reduce(a_in_own, a_in_got, *big["a_w_in"])}
    outs.update(zip(names[1:], _adamw_reduce_many([parts[n] for n in names[1:]], [big[n] for n in names[1:]])))
    outs["w_dkv"] = tuple(_dkv_from_padded(a) for a in outs["w_dkv"])
    outs["b_w_uq"] = tuple(a[:, :QK_NOPE + QK_ROPE] for a in outs["b_w_uq"])
    for n in ("a_w_in", "a_w_out", "b_w_in", "b_w_uq", "b_w_out"):
        outs[n] = tuple(a[None] for a in outs[n])

    total = _sum_parts(small_parts)
    loss = total[4, LOSS_LANE]
    shard = D_MODEL // N_DEV
    g_small = {
        "final_norm": total[0], "kv_norm": total[1], "b_norm": total[2:3],
        "a_norm": lax.dynamic_slice_in_dim(total[3:4], me * shard, shard, axis=1),
        "ckv_norm": total[4, 0:KV_RANK], "b_q_norm": total[4:5, KV_RANK:KV_RANK + Q_RANK],
        "a_conv": lax.dynamic_slice_in_dim(total[5:8], me * shard, shard, axis=1)[None],
    }
    small_state = {
        "final_norm": (final_norm, m_final_norm, v_final_norm), "kv_norm": (kv_norm, m_kv_norm, v_kv_norm),
        "b_norm": (b_norm, m_b_norm, v_b_norm), "a_norm": (a_norm, m_a_norm, v_a_norm),
        "ckv_norm": (ckv_norm, m_ckv_norm, v_ckv_norm), "b_q_norm": (b_q_norm, m_b_q_norm, v_b_q_norm),
        "a_conv": (a_conv, m_a_conv, v_a_conv),
    }
    small_names = list(g_small)
    as2d = lambda a: a.reshape(-1, a.shape[-1])
    upd = _adamw_small([as2d(g_small[n]) for n in small_names],
                       *[[as2d(small_state[n][j]) for n in small_names] for j in range(3)])
    for n, u in zip(small_names, upd):
        outs[n] = (g_small[n],) + tuple(a.reshape(g_small[n].shape) for a in u)

    order = ["a_norm", "a_w_in", "a_conv", "a_w_out", "kv_norm", "w_dkv", "ckv_norm", "w_ukv", "b_norm", "b_w_in",
             "b_q_norm", "b_w_uq", "b_w_out", "final_norm"]
    result = [loss, dx.reshape(n_seq, seq, D_MODEL)]
    for j in range(4):
        result += [outs[n][j] for n in order]
    return tuple(result)
```

```python
import functools
import math

import numpy as np
import jax
import jax.numpy as jnp
from jax import lax
from jax.experimental import pallas as pl
from jax.experimental.pallas import tpu as pltpu

F32 = jnp.float32
BF16 = jnp.bfloat16
SAVED = BF16

D_MODEL = 1024
N_HEADS = 8
QK_NOPE = 64
QK_ROPE = 32
V_HEAD = 64
KV_RANK = 256
Q_RANK = 384
E_B = N_HEADS * V_HEAD
HEAD_PAD = 128
CKR_PAD = KV_RANK + HEAD_PAD
ROPE_LO = QK_NOPE
ROPE_HALF = QK_ROPE // 2
ROPE_THETA = 10000.0
SOFTMAX_SCALE = 1.0 / math.sqrt(QK_NOPE + QK_ROPE)
LOG2_E = math.log2(math.e)
LN_2 = math.log(2.0)
EPS = 1e-6
N_DEV = 8

ADAM_LR = 0.001
ADAM_B1 = 0.9
ADAM_B2 = 0.999
ADAM_EPS = 1e-08
ADAM_WD = 0.01
ADAM_STEP = 10

ROW_TILE = 256
LATENT_ROW_TILE = 512
PICKUP_BLOCK = 3
ATTN_TILE = 512
ATTN_GROUP = 4
ATTN_BWD_GROUP = 4
VMEM_LIMIT = 56 * 1024 * 1024

MESH = pl.DeviceIdType.MESH
NT = (((1,), (1,)), ((), ()))
TN = (((0,), (0,)), ((), ()))


def _dot(a, b):
    return jnp.dot(a.astype(BF16), b.astype(BF16), preferred_element_type=F32)


def _dot_nt(a, b):
    return lax.dot_general(a.astype(BF16), b.astype(BF16), NT, preferred_element_type=F32)


def _dot_tn(a, b):
    return lax.dot_general(a.astype(BF16), b.astype(BF16), TN, preferred_element_type=F32)


def _rstd(x):
    return lax.rsqrt(jnp.mean(x * x, axis=-1, keepdims=True) + EPS)


def _norm_bwd(a, xh, r):
    return r * (a - xh * jnp.mean(a * xh, axis=-1, keepdims=True))


def _silu_parts(g):
    sg = jax.nn.sigmoid(g)
    return g * sg, sg * (1.0 + g * (1.0 - sg))


def _rope_consts():
    inv = (ROPE_THETA ** (-np.arange(0, QK_ROPE, 2, dtype=np.float32) / QK_ROPE)).astype(np.float32)
    t = np.zeros((8, HEAD_PAD), np.float32)
    t[0, ROPE_LO:ROPE_LO + ROPE_HALF] = inv
    t[0, ROPE_LO + ROPE_HALF:ROPE_LO + QK_ROPE] = inv
    t[1, ROPE_LO:ROPE_LO + ROPE_HALF] = -1.0
    t[2, ROPE_LO + ROPE_HALF:ROPE_LO + QK_ROPE] = 1.0
    return jnp.asarray(t)


def _rope_angles(pos, rc):
    ang = pos * rc[0:1, :]
    return jnp.cos(ang), jnp.sin(ang)


def _rope_tables(cosv, sinv, rc):
    return cosv, sinv * rc[1:2, :], sinv * rc[2:3, :]


def _rope(x, ct, s1, s2):
    up = pltpu.roll(x, HEAD_PAD - ROPE_HALF, 1)
    dn = pltpu.roll(x, ROPE_HALF, 1)
    return x * ct + up * s1 + dn * s2


def _const_spec(shape):
    nd = len(shape)
    return pl.BlockSpec(shape, lambda *_: (0,) * nd, pipeline_mode=pl.Buffered(1))


def _acc_spec(shape):
    nd = len(shape)
    return pl.BlockSpec(shape, lambda *_: (0,) * nd)


def _params(n_axes):
    return pltpu.CompilerParams(dimension_semantics=("arbitrary",) * n_axes, vmem_limit_bytes=VMEM_LIMIT)


def _place():
    x, y, c = lax.axis_index("x"), lax.axis_index("y"), lax.axis_index("c")
    return x, y, c, 4 * x + 2 * y + c


def _peer(x, y, c, mask):
    px = 1 - x if mask & 4 else x
    py = 1 - y if mask & 2 else y
    pc = 1 - c if mask & 1 else c
    return (px, py, pc), 4 * px + 2 * py + pc


ANY_SPEC = pl.BlockSpec(memory_space=pl.ANY)


def _comm_sems(n):
    return [pltpu.SemaphoreType.DMA((n, N_DEV - 1)), pltpu.SemaphoreType.DMA((n, N_DEV - 1)), pltpu.SemaphoreType.DMA((n,))]


def _gather_copies(ins, outs, sems):
    send_sems, recv_sems, local_sems = sems
    x, y, c, me = _place()
    starts, waits = [], []
    for w in range(len(ins)):
        mine = pltpu.make_async_copy(ins[w], outs[w].at[me], local_sems.at[w])
        starts.append(mine)
        waits.append(mine)
        for mask in range(1, N_DEV):
            peer, peer_idx = _peer(x, y, c, mask)
            starts.append(pltpu.make_async_remote_copy(
                src_ref=ins[w], dst_ref=outs[w].at[me], send_sem=send_sems.at[w, mask - 1],
                recv_sem=recv_sems.at[w, mask - 1], device_id=peer, device_id_type=MESH))
            waits.append(pltpu.make_async_remote_copy(
                src_ref=ins[w], dst_ref=outs[w].at[peer_idx], send_sem=send_sems.at[w, mask - 1],
                recv_sem=recv_sems.at[w, mask - 1], device_id=peer, device_id_type=MESH))
    return starts, waits


def _scatter_copies(ins, outs, sems):
    send_sems, recv_sems, local_sems = sems
    x, y, c, me = _place()
    copies = []
    for w in range(len(ins)):
        copies.append(pltpu.make_async_copy(ins[w].at[me], outs[w].at[0], local_sems.at[w]))
        for mask in range(1, N_DEV):
            peer, peer_idx = _peer(x, y, c, mask)
            copies.append(pltpu.make_async_remote_copy(
                src_ref=ins[w].at[peer_idx], dst_ref=outs[w].at[mask], send_sem=send_sems.at[w, mask - 1],
                recv_sem=recv_sems.at[w, mask - 1], device_id=peer, device_id_type=MESH))
    return copies, copies


def _stacked(arrays):
    return [jax.ShapeDtypeStruct((N_DEV,) + a.shape, a.dtype) for a in arrays]


def _all_gather(shards):
    n = len(shards)

    def body(*refs):
        ins, outs = refs[:n], refs[n:2 * n]
        send_sems, recv_sems, local_sems = refs[2 * n:]
        x, y, c, me = _place()
        sibling = (x, y, 1 - c)
        chips = [(1 - x, y), (x, 1 - y), (1 - x, 1 - y)]

        def copy(w, k, block, to, src=None):
            idx = 4 * block[0] + 2 * block[1] + block[2]
            return pltpu.make_async_remote_copy(
                src_ref=outs[w].at[idx] if src is None else src, dst_ref=outs[w].at[idx],
                send_sem=send_sems.at[w, k], recv_sem=recv_sems.at[w, k], device_id=to, device_id_type=MESH)

        local, sent = [], []
        for w in range(n):
            mine = pltpu.make_async_copy(ins[w], outs[w].at[me], local_sems.at[w])
            mine.start()
            local.append(mine)
            first = [copy(w, 0, (x, y, c), sibling, src=ins[w])]
            first += [copy(w, 1 + j, (x, y, c), (*chip, c), src=ins[w]) for j, chip in enumerate(chips)]
            for cp in first:
                cp.start()
            sent += first
        for w in range(n):
            for j, chip in enumerate(chips):
                copy(w, 1 + j, (*chip, c), (x, y, c)).wait_recv()
                onward = copy(w, 4 + j, (*chip, c), sibling)
                onward.start()
                sent.append(onward)
        for w in range(n):
            copy(w, 0, sibling, (x, y, c)).wait_recv()
            for j, chip in enumerate(chips):
                copy(w, 4 + j, (*chip, 1 - c), (x, y, c)).wait_recv()
        for cp in sent:
            cp.wait_send()
        for cp in local:
            cp.wait()

    return pl.pallas_call(
        body, name="weight_all_gather", out_shape=_stacked(shards),
        in_specs=[ANY_SPEC] * n, out_specs=[ANY_SPEC] * n, scratch_shapes=_comm_sems(n),
    )(*shards)


def _conv_fwd(x, a_norm, w_in, conv_w, w_out, n_seq, seq, later_shards):
    tm = LATENT_ROW_TILE
    nt = seq // tm
    n_col = w_in.shape[2]
    n_later = len(later_shards)

    def body(x_ref, an_ref, win_ref, cw_ref, wout_ref, *rest):
        shard_refs, rest = rest[:n_later], rest[n_later:]
        x1_ref, proj_ref, conv_ref = rest[:3]
        stack_refs, rest = rest[3:3 + n_later], rest[3 + n_later:]
        prev_ref, wide_ref, sems = rest[0], rest[1], rest[2:]
        step = pl.program_id(0) * nt + pl.program_id(1)

        @pl.when(step == 0)
        def _():
            for cp in _gather_copies(shard_refs, stack_refs, sems)[0]:
                cp.start()

        @pl.when(pl.program_id(1) == 0)
        def _():
            prev_ref[...] = jnp.zeros_like(prev_ref)

        xv = x_ref[...]
        h = (xv * _rstd(xv) * an_ref[...]).astype(BF16)
        for d in range(N_DEV):
            part = jnp.dot(h, win_ref[d], preferred_element_type=F32)
            wide_ref[:, d * n_col:(d + 1) * n_col] = part
            proj_ref[:, d * n_col:(d + 1) * n_col] = part.astype(SAVED)
        b = wide_ref[:, 0:D_MODEL]
        v = wide_ref[:, D_MODEL:2 * D_MODEL] * wide_ref[:, 2 * D_MODEL:3 * D_MODEL]
        g = wide_ref[:, 3 * D_MODEL:4 * D_MODEL]
        w0, w1, w2 = cw_ref[0:1, :], cw_ref[1:2, :], cw_ref[2:3, :]
        conv_ref[...] = w0 * pltpu.roll(v, 2, 0) + w1 * pltpu.roll(v, 1, 0) + w2 * v
        rows = lax.broadcasted_iota(jnp.int32, (8, D_MODEL), 0)
        p8, v8 = prev_ref[...], v[0:8]
        back1 = jnp.where(rows < 1, pltpu.roll(p8, 1, 0), pltpu.roll(v8, 1, 0))
        back2 = jnp.where(rows < 2, pltpu.roll(p8, 2, 0), pltpu.roll(v8, 2, 0))
        conv_ref[0:8, :] = w0 * back2 + w1 * back1 + w2 * v8
        prev_ref[...] = v[tm - 8:tm]
        silu, _ = _silu_parts(g)
        yv = silu * b * conv_ref[...]
        x1_ref[...] = xv + _dot(yv, wout_ref[...])

        @pl.when(step == n_seq * nt - 1)
        def _():
            for cp in _gather_copies(shard_refs, stack_refs, sems)[1]:
                cp.wait()

    tok = lambda width: pl.BlockSpec((tm, width), lambda s, i: (s * nt + i, 0))
    t = n_seq * seq
    outs = pl.pallas_call(
        body, name="conv_mixer_fwd", grid=(n_seq, nt),
        in_specs=[tok(D_MODEL), _const_spec((1, D_MODEL)), _const_spec(w_in.shape), _const_spec((3, D_MODEL)),
                  _const_spec(w_out.shape)] + [ANY_SPEC] * n_later,
        out_specs=[tok(D_MODEL), tok(4 * D_MODEL), tok(D_MODEL)] + [ANY_SPEC] * n_later,
        out_shape=[jax.ShapeDtypeStruct((t, D_MODEL), F32), jax.ShapeDtypeStruct((t, 4 * D_MODEL), SAVED),
                   jax.ShapeDtypeStruct((t, D_MODEL), F32)] + _stacked(later_shards),
        scratch_shapes=[pltpu.VMEM((8, D_MODEL), F32), pltpu.VMEM((tm, 4 * D_MODEL), F32)] + _comm_sems(n_later),
        compiler_params=_params(2),
    )(x, a_norm, w_in, conv_w, w_out, *later_shards)
    return outs[0], outs[1], outs[2], outs[3:]


def _conv_bwd(dx1, x, proj, conv, a_norm, w_in, conv_w, w_out, n_seq, seq, ready_stacks):
    tm = ROW_TILE
    nt = seq // tm
    n_col = w_in.shape[2]
    n_ready = len(ready_stacks)

    def body(dx1_ref, x_ref, proj_ref, conv_ref, an_ref, win_ref, cw_ref, wout_ref, *rest):
        ready_refs, rest = rest[:n_ready], rest[n_ready:]
        dx_ref, h_ref, dproj_ref, dwout_ref, dan_ref, dcw_ref = rest[:6]
        part_refs, rest = rest[6:6 + n_ready], rest[6 + n_ready:]
        next_ref, d1_ref, d2_ref = rest[:3]
        sems = rest[3:]
        step = pl.program_id(0) * nt + pl.program_id(1)
        first = step == 0

        @pl.when(first)
        def _():
            for cp in _scatter_copies(ready_refs, part_refs, sems)[0]:
                cp.start()
            dwout_ref[...] = jnp.zeros_like(dwout_ref)
            dan_ref[...] = jnp.zeros_like(dan_ref)
            dcw_ref[...] = jnp.zeros_like(dcw_ref)

        @pl.when(pl.program_id(1) == 0)
        def _():
            next_ref[...] = jnp.zeros_like(next_ref)

        dx1v = dx1_ref[...]
        dy = _dot_nt(dx1v, wout_ref[...])
        b = proj_ref[:, 0:D_MODEL].astype(F32)
        cc = proj_ref[:, D_MODEL:2 * D_MODEL].astype(F32)
        u = proj_ref[:, 2 * D_MODEL:3 * D_MODEL].astype(F32)
        g = proj_ref[:, 3 * D_MODEL:4 * D_MODEL].astype(F32)
        cv = conv_ref[...]
        silu, dsilu = _silu_parts(g)
        per_part = D_MODEL // n_col

        def back_through_w_in(part, grad):
            grad = grad.astype(BF16)
            dproj_ref[:, part * D_MODEL:(part + 1) * D_MODEL] = grad
            term = None
            for j in range(per_part):
                d = part * per_part + j
                piece = lax.dot_general(grad[:, j * n_col:(j + 1) * n_col], win_ref[d], NT, preferred_element_type=F32)
                term = piece if term is None else term + piece
            return term

        dh = back_through_w_in(0, dy * silu * cv)
        dh += back_through_w_in(3, dy * b * cv * dsilu)
        dwout_ref[...] += _dot_tn(silu * b * cv, dx1v)
        dconv = dy * silu * b
        d1_ref[...] = pltpu.roll(dconv, tm - 1, 0)
        d2_ref[...] = pltpu.roll(dconv, tm - 2, 0)
        rows = lax.broadcasted_iota(jnp.int32, (8, D_MODEL), 0)
        n8, c8 = next_ref[...], dconv[tm - 8:tm]
        d1_ref[tm - 8:tm, :] = jnp.where(rows >= 7, pltpu.roll(n8, 7, 0), pltpu.roll(c8, 7, 0))
        d2_ref[tm - 8:tm, :] = jnp.where(rows >= 6, pltpu.roll(n8, 6, 0), pltpu.roll(c8, 6, 0))
        next_ref[...] = dconv[0:8]
        d1, d2 = d1_ref[...], d2_ref[...]
        v = cc * u
        dcw_ref[0:1, :] += jnp.sum(d2 * v, axis=0, keepdims=True)
        dcw_ref[1:2, :] += jnp.sum(d1 * v, axis=0, keepdims=True)
        dcw_ref[2:3, :] += jnp.sum(dconv * v, axis=0, keepdims=True)
        dv = cw_ref[0:1, :] * d2 + cw_ref[1:2, :] * d1 + cw_ref[2:3, :] * dconv
        dh += back_through_w_in(1, dv * u)
        dh += back_through_w_in(2, dv * cc)
        xv = x_ref[...]
        r = _rstd(xv)
        xh = xv * r
        h_ref[...] = (xh * an_ref[...]).T.astype(BF16)
        dan_ref[...] += jnp.sum(dh * xh, axis=0, keepdims=True)
        dx_ref[...] = dx1v + _norm_bwd(dh * an_ref[...], xh, r)

        @pl.when(step == n_seq * nt - 1)
        def _():
            for cp in _scatter_copies(ready_refs, part_refs, sems)[1]:
                cp.wait()

    tok = lambda width: pl.BlockSpec((tm, width), lambda s, i: (s * nt + nt - 1 - i, 0))
    t = n_seq * seq
    outs = pl.pallas_call(
        body, name="conv_mixer_bwd", grid=(n_seq, nt),
        in_specs=[tok(D_MODEL), tok(D_MODEL), tok(4 * D_MODEL), tok(D_MODEL), _const_spec((1, D_MODEL)),
                  _const_spec(w_in.shape), _const_spec((3, D_MODEL)), _const_spec(w_out.shape)] + [ANY_SPEC] * n_ready,
        out_specs=[tok(D_MODEL), pl.BlockSpec((D_MODEL, tm), lambda s, i: (0, s * nt + nt - 1 - i)),
                   tok(4 * D_MODEL), _acc_spec((D_MODEL, D_MODEL)),
                   _acc_spec((1, D_MODEL)), _acc_spec((8, D_MODEL))] + [ANY_SPEC] * n_ready,
        out_shape=[jax.ShapeDtypeStruct((t, D_MODEL), F32), jax.ShapeDtypeStruct((D_MODEL, t), BF16),
                   jax.ShapeDtypeStruct((t, 4 * D_MODEL), BF16), jax.ShapeDtypeStruct((D_MODEL, D_MODEL), F32),
                   jax.ShapeDtypeStruct((1, D_MODEL), F32), jax.ShapeDtypeStruct((8, D_MODEL), F32)]
        + [jax.ShapeDtypeStruct(a.shape, a.dtype) for a in ready_stacks],
        scratch_shapes=[pltpu.VMEM((8, D_MODEL), F32), pltpu.VMEM((tm, D_MODEL), F32), pltpu.VMEM((tm, D_MODEL), F32)]
        + _comm_sems(n_ready),
        compiler_params=_params(2),
    )(dx1, x, proj, conv, a_norm, w_in, conv_w, w_out, *ready_stacks)
    return outs[:6], outs[6:]


def _w_in_grad_exchange(a_t, b, n_col, chip, ready_stacks, small):
    r, t = a_t.shape
    bt = 2048
    nk = t // bt
    n_ready = len(ready_stacks)
    n_chip = N_DEV // 2
    n_remote = N_DEV - 2

    def body(chip_ref, a_ref, b_ref, *rest):
        ready_refs, small_ref, rest = rest[:n_ready], rest[n_ready], rest[n_ready + 1:]
        own_ref, got_ref, rest = rest[0], rest[1], rest[2:]
        part_refs, small_all_ref, rest = rest[:n_ready], rest[n_ready], rest[n_ready + 1:]
        acc_ref, land_ref, stage_ref, pair_send, pair_recv, chip_send, chip_recv = rest[:7]
        scatter_sems, gather_sems = rest[7:10], rest[10:13]
        s, k = pl.program_id(0), pl.program_id(1)
        x, y, c, _ = _place()
        chip = 2 * x + y

        @pl.when(jnp.logical_and(s == 0, k == 0))
        def _():
            for cp in _scatter_copies(ready_refs, part_refs, scatter_sems)[0]:
                cp.start()
            for cp in _gather_copies([small_ref], [small_all_ref], gather_sems)[0]:
                cp.start()

        for parity in range(2):
            @pl.when(s % 2 == parity)
            def _(parity=parity):
                @pl.when(k == 0)
                def _():
                    acc_ref[parity] = jnp.zeros((r, n_col), F32)

                acc_ref[parity] += jnp.dot(a_ref[...], b_ref[...], preferred_element_type=F32)

        def to_sibling(step):
            return pltpu.make_async_remote_copy(
                src_ref=acc_ref.at[step % 2], dst_ref=land_ref, send_sem=pair_send.at[step], recv_sem=pair_recv.at[step],
                device_id=(x, y, 1 - c), device_id_type=MESH)

        def to_owner(nth):
            owner_chip = (chip + 1 + nth) % n_chip
            slot = jnp.bitwise_xor(chip, owner_chip) - 1
            return pltpu.make_async_remote_copy(
                src_ref=stage_ref.at[nth % 2], dst_ref=got_ref.at[slot], send_sem=chip_send.at[nth],
                recv_sem=chip_recv.at[slot], device_id=(owner_chip // 2, owner_chip % 2, c), device_id_type=MESH)

        for step in range(N_DEV):
            owner_core = step % 2

            @pl.when(jnp.logical_and(s == step, k == nk - 1))
            def _(step=step, owner_core=owner_core):
                @pl.when(c != owner_core)
                def _():
                    to_sibling(step).start()

                if step >= 1:
                    @pl.when(c == owner_core)
                    def _():
                        to_sibling(step - 1).wait_send()

            if step < N_DEV - 1:
                pickup = jnp.logical_and(s == step + 1, k == min(PICKUP_BLOCK, nk - 1))
            else:
                pickup = jnp.logical_and(s == step, k == nk - 1)

            @pl.when(jnp.logical_and(pickup, c == owner_core))
            def _(step=step):
                to_sibling(step).wait_recv()
                total = acc_ref[step % 2] + land_ref[...]
                if step < n_remote:
                    nth = step // 2
                    if nth >= 2:
                        to_owner(nth - 2).wait_send()
                    stage_ref[nth % 2] = total.astype(BF16)
                    to_owner(nth).start()
                else:
                    own_ref[...] = total

        @pl.when(jnp.logical_and(s == N_DEV - 1, k == nk - 1))
        def _():
            @pl.when(c != (N_DEV - 1) % 2)
            def _():
                to_sibling(N_DEV - 1).wait_send()

            to_owner(1).wait_send()
            to_owner(2).wait_send()
            for slot in range(n_chip - 1):
                pltpu.make_async_remote_copy(
                    src_ref=stage_ref.at[0], dst_ref=got_ref.at[slot], send_sem=chip_send.at[0],
                    recv_sem=chip_recv.at[slot], device_id=(x, y, c), device_id_type=MESH).wait_recv()
            for cp in _scatter_copies(ready_refs, part_refs, scatter_sems)[1]:
                cp.wait()
            for cp in _gather_copies([small_ref], [small_all_ref], gather_sems)[1]:
                cp.wait()

    def owner_block(s, k, chip):
        return (k, 2 * ((chip[0] + 1 + s // 2) % n_chip) + s % 2)

    grid_spec = pltpu.PrefetchScalarGridSpec(
        num_scalar_prefetch=1, grid=(N_DEV, nk),
        in_specs=[pl.BlockSpec((r, bt), lambda s, k, chip: (0, k)), pl.BlockSpec((bt, n_col), owner_block)]
        + [ANY_SPEC] * (n_ready + 1),
        out_specs=[pl.BlockSpec((r, n_col), lambda s, k, chip: (0, 0)), ANY_SPEC] + [ANY_SPEC] * (n_ready + 1),
        scratch_shapes=[pltpu.VMEM((2, r, n_col), F32), pltpu.VMEM((r, n_col), F32), pltpu.VMEM((2, r, n_col), BF16),
                        pltpu.SemaphoreType.DMA((N_DEV,)), pltpu.SemaphoreType.DMA((N_DEV,)),
                        pltpu.SemaphoreType.DMA((n_chip - 1,)), pltpu.SemaphoreType.DMA((n_chip - 1,))]
        + _comm_sems(n_ready) + _comm_sems(1))
    outs = pl.pallas_call(
        body, name="w_in_grad_exchange", grid_spec=grid_spec,
        out_shape=[jax.ShapeDtypeStruct((r, n_col), F32), jax.ShapeDtypeStruct((n_chip - 1, r, n_col), BF16)]
        + [jax.ShapeDtypeStruct(p.shape, p.dtype) for p in ready_stacks] + _stacked([small]),
        compiler_params=_params(2),
    )(chip, a_t, b, *ready_stacks, small)
    return outs[0], outs[1], outs[2:2 + n_ready], outs[2 + n_ready]


def _proj_fwd(x1, pos, rc, kv_norm, w_dkv, ckv_norm, w_ukv, b_norm, b_w_in, q_norm, w_uq):
    t = x1.shape[0]
    tm = LATENT_ROW_TILE

    def body(x1_ref, pos_ref, rc_ref, kvn_ref, wdkv_ref, ckvn_ref, wukv_ref, bn_ref, bwin_ref, qn_ref, wuq_ref,
             ckr_ref, cq_ref, gb_ref, q_ref, k_ref, v_ref, cos_ref, sin_ref):
        xv = x1_ref[...]
        xh = xv * _rstd(xv)
        ckr = _dot(xh * kvn_ref[...], wdkv_ref[...])
        pb = _dot(xh * bn_ref[...], bwin_ref[...])
        ckr_ref[...] = ckr
        ckv = ckr[:, 0:KV_RANK]
        kv = _dot(ckv * _rstd(ckv) * ckvn_ref[...], wukv_ref[...])
        cosv, sinv = _rope_angles(pos_ref[...], rc_ref[...])
        cos_ref[...] = cosv
        sin_ref[...] = sinv
        ct, s1, s2 = _rope_tables(cosv, sinv, rc_ref[...])
        k_rope = _rope(ckr[:, KV_RANK:CKR_PAD], ct, s1, s2)
        lane = lax.broadcasted_iota(jnp.int32, (tm, HEAD_PAD), 1)
        low = lane < QK_NOPE
        for h in range(N_HEADS):
            kv_h = kv[:, h * HEAD_PAD:(h + 1) * HEAD_PAD]
            k_ref[:, h * HEAD_PAD:(h + 1) * HEAD_PAD] = jnp.where(low, kv_h, k_rope).astype(BF16)
            v_ref[:, h * HEAD_PAD:(h + 1) * HEAD_PAD] = jnp.where(low, pltpu.roll(kv_h, V_HEAD, 1), 1.0).astype(BF16)
        cq = pb[:, 0:Q_RANK]
        cq_ref[...] = cq
        gb_ref[...] = pb[:, Q_RANK:Q_RANK + E_B]
        q = _dot(cq * _rstd(cq) * qn_ref[...], wuq_ref[...])
        for h in range(N_HEADS):
            q_ref[:, h * HEAD_PAD:(h + 1) * HEAD_PAD] = _rope(
                q[:, h * HEAD_PAD:(h + 1) * HEAD_PAD], ct, s1, s2).astype(BF16)

    tok = lambda width: pl.BlockSpec((tm, width), lambda i: (i, 0))
    weights = [kv_norm, w_dkv, ckv_norm, w_ukv, b_norm, b_w_in, q_norm, w_uq]
    wide = N_HEADS * HEAD_PAD
    return pl.pallas_call(
        body, name="latent_proj_fwd", grid=(t // tm,),
        in_specs=[tok(D_MODEL), tok(1), _const_spec(rc.shape)] + [_const_spec(w.shape) for w in weights],
        out_specs=[tok(CKR_PAD), tok(Q_RANK), tok(E_B), tok(wide), tok(wide), tok(wide), tok(HEAD_PAD), tok(HEAD_PAD)],
        out_shape=[jax.ShapeDtypeStruct((t, CKR_PAD), F32), jax.ShapeDtypeStruct((t, Q_RANK), F32),
                   jax.ShapeDtypeStruct((t, E_B), F32), jax.ShapeDtypeStruct((t, wide), BF16),
                   jax.ShapeDtypeStruct((t, wide), BF16), jax.ShapeDtypeStruct((t, wide), BF16),
                   jax.ShapeDtypeStruct((t, HEAD_PAD), F32), jax.ShapeDtypeStruct((t, HEAD_PAD), F32)],
        compiler_params=_params(1),
    )(x1, pos, rc, *weights)


def _proj_bwd(dq, dk, dv, dgb, cq, ckr, x1, dx2, cosv, sinv, rc, kv_norm, w_dkv, ckv_norm, w_ukv, b_norm, b_w_in, q_norm, w_uq):
    t = x1.shape[0]
    tm = LATENT_ROW_TILE
    wide = N_HEADS * HEAD_PAD

    def body(dq_ref, dk_ref, dv_ref, dgb_ref, cq_ref, ckr_ref, x1_ref, dx2_ref, cos_ref, sin_ref, rc_ref,
             kvn_ref, wdkv_ref, ckvn_ref, wukv_ref, bn_ref, bwin_ref, qn_ref, wuq_ref,
             dx1_ref, dwuq_ref, dwbin_ref, dwukv_ref, dwdkv_ref, dqn_ref, dbn_ref, dckvn_ref, dkvn_ref, dqu_ref, dkv_ref):
        @pl.when(pl.program_id(0) == 0)
        def _():
            for ref in (dwuq_ref, dwbin_ref, dwukv_ref, dwdkv_ref, dqn_ref, dbn_ref, dckvn_ref, dkvn_ref):
                ref[...] = jnp.zeros_like(ref)

        ct, s1, s2 = _rope_tables(cos_ref[...], sin_ref[...], rc_ref[...])
        lane = lax.broadcasted_iota(jnp.int32, (tm, HEAD_PAD), 1)
        low = lane < QK_NOPE
        xv = x1_ref[...]
        r = _rstd(xv)
        xh = xv * r
        dk_rope = jnp.zeros((tm, HEAD_PAD), F32)
        for h in range(N_HEADS):
            dk_h = dk_ref[:, h * HEAD_PAD:(h + 1) * HEAD_PAD].astype(F32)
            dv_h = dv_ref[:, h * HEAD_PAD:(h + 1) * HEAD_PAD].astype(F32)
            dkv_ref[:, h * HEAD_PAD:(h + 1) * HEAD_PAD] = jnp.where(low, dk_h, pltpu.roll(dv_h, V_HEAD, 1)).astype(BF16)
            dk_rope += dk_h
        rope_lanes = jnp.logical_and(lane >= ROPE_LO, lane < ROPE_LO + QK_ROPE)
        dk_rope = jnp.where(rope_lanes, _rope(dk_rope, ct, -s1, -s2), 0.0)
        ckv = ckr_ref[:, 0:KV_RANK]
        rk = _rstd(ckv)
        ckh = ckv * rk
        ckvn = (ckh * ckvn_ref[...]).astype(BF16)
        dwukv = lax.dot_general(ckvn, dkv_ref[...], TN, preferred_element_type=F32)
        for h in range(N_HEADS):
            dwukv_ref[h] += dwukv[:, h * HEAD_PAD:(h + 1) * HEAD_PAD]
        dckvn = lax.dot_general(dkv_ref[...], wukv_ref[...], NT, preferred_element_type=F32)
        dckvn_ref[...] += jnp.sum(dckvn * ckh, axis=0, keepdims=True)
        dckv = _norm_bwd(dckvn * ckvn_ref[...], ckh, rk)
        dckr = jnp.concatenate([dckv, dk_rope], axis=1).astype(BF16)
        dwdkv_ref[...] += _dot_tn(xh * kvn_ref[...], dckr)
        dh2 = lax.dot_general(dckr, wdkv_ref[...], NT, preferred_element_type=F32)
        for h in range(N_HEADS):
            dqu_ref[:, h * HEAD_PAD:(h + 1) * HEAD_PAD] = _rope(
                dq_ref[:, h * HEAD_PAD:(h + 1) * HEAD_PAD].astype(F32), ct, -s1, -s2).astype(BF16)
        cq = cq_ref[...]
        rq = _rstd(cq)
        cqh = cq * rq
        cqn = (cqh * qn_ref[...]).astype(BF16)
        dwuq = lax.dot_general(cqn, dqu_ref[...], TN, preferred_element_type=F32)
        for h in range(N_HEADS):
            dwuq_ref[h] += dwuq[:, h * HEAD_PAD:(h + 1) * HEAD_PAD]
        dcqn = lax.dot_general(dqu_ref[...], wuq_ref[...], NT, preferred_element_type=F32)
        dqn_ref[...] += jnp.sum(dcqn * cqh, axis=0, keepdims=True)
        dcq = _norm_bwd(dcqn * qn_ref[...], cqh, rq)
        dpb = jnp.concatenate([dcq.astype(BF16), dgb_ref[...]], axis=1)
        dwbin_ref[...] += _dot_tn(xh * bn_ref[...], dpb)
        dh3 = lax.dot_general(dpb, bwin_ref[...], NT, preferred_element_type=F32)
        dkvn_ref[...] += jnp.sum(dh2 * xh, axis=0, keepdims=True)
        dbn_ref[...] += jnp.sum(dh3 * xh, axis=0, keepdims=True)
        dx1_ref[...] = dx2_ref[...] + _norm_bwd(dh2 * kvn_ref[...] + dh3 * bn_ref[...], xh, r)

    tok = lambda width: pl.BlockSpec((tm, width), lambda i: (i, 0))
    weights = [kv_norm, w_dkv, ckv_norm, w_ukv, b_norm, b_w_in, q_norm, w_uq]
    acc_shapes = [(N_HEADS, Q_RANK, HEAD_PAD), (D_MODEL, Q_RANK + E_B), (N_HEADS, KV_RANK, HEAD_PAD), (D_MODEL, CKR_PAD),
                  (1, Q_RANK), (1, D_MODEL), (1, KV_RANK), (1, D_MODEL)]
    return pl.pallas_call(
        body, name="latent_proj_bwd", grid=(t // tm,),
        in_specs=[tok(wide), tok(wide), tok(wide), tok(E_B), tok(Q_RANK), tok(CKR_PAD), tok(D_MODEL), tok(D_MODEL),
                  tok(HEAD_PAD), tok(HEAD_PAD), _const_spec(rc.shape)] + [_const_spec(w.shape) for w in weights],
        out_specs=[tok(D_MODEL)] + [_acc_spec(s) for s in acc_shapes],
        out_shape=[jax.ShapeDtypeStruct((t, D_MODEL), F32)] + [jax.ShapeDtypeStruct(s, F32) for s in acc_shapes],
        scratch_shapes=[pltpu.VMEM((tm, wide), BF16), pltpu.VMEM((tm, wide), BF16)],
        compiler_params=_params(1),
    )(dq, dk, dv, dgb, cq, ckr, x1, dx2, cosv, sinv, rc, *weights)


def _attn_fwd(q, k, v, n_seq, seq):
    tb, group = ATTN_TILE, ATTN_GROUP
    rows = tb * group
    nq = seq // rows
    pair = 2 * HEAD_PAD

    def body(q_ref, k_ref, v_ref, o_ref, lr_ref):
        i = pl.program_id(2)
        row = lax.broadcasted_iota(jnp.int32, (tb, tb), 0)
        col = lax.broadcasted_iota(jnp.int32, (tb, tb), 1)
        causal = col <= row

        def step(j, carry, kinds):
            start = pl.multiple_of(j * tb, tb)
            out = []
            for g in range(group):
                for hh in range(2):
                    m, acc = carry[2 * g + hh]
                    if kinds[g] != "none":
                        heads = slice(hh * HEAD_PAD, (hh + 1) * HEAD_PAD)
                        s = lax.dot_general(q_ref[g * tb:(g + 1) * tb, heads], k_ref[pl.ds(start, tb), heads], NT,
                                            preferred_element_type=F32) * (SOFTMAX_SCALE * LOG2_E)
                        if kinds[g] == "diagonal":
                            s = jnp.where(causal, s, -jnp.inf)
                        m_new = jnp.maximum(m, jnp.max(s, axis=-1, keepdims=True))
                        p = jnp.exp2(s - m_new)
                        acc = jnp.exp2(m - m_new) * acc + jnp.dot(
                            p.astype(BF16), v_ref[pl.ds(start, tb), heads], preferred_element_type=F32)
                        m = m_new
                    out.append((m, acc))
            return tuple(out)

        one = (jnp.full((tb, 1), -jnp.inf, F32), jnp.zeros((tb, HEAD_PAD), F32))
        carry = lax.fori_loop(0, i * group, functools.partial(step, kinds=("full",) * group), (one,) * (2 * group))
        for d in range(group):
            kinds = tuple("none" if g < d else ("diagonal" if g == d else "full") for g in range(group))
            carry = step(i * group + d, carry, kinds)
        lane = lax.broadcasted_iota(jnp.int32, (tb, HEAD_PAD), 1)
        low = lane < V_HEAD
        for g in range(group):
            halves = []
            for hh in range(2):
                m, acc = carry[2 * g + hh]
                swapped = pltpu.roll(acc, V_HEAD, 1)
                halves.append(acc / swapped)
                lse = (m * LN_2) + jnp.log(jnp.where(low, swapped, acc))
                lr_ref[hh, :, g * tb:(g + 1) * tb] = lse.T[0:8, :]
            o_ref[g * tb:(g + 1) * tb, :] = jnp.where(low, halves[0], pltpu.roll(halves[1], V_HEAD, 1))

    t = n_seq * seq
    return pl.pallas_call(
        body, name="attention_fwd", grid=(n_seq, N_HEADS // 2, nq),
        in_specs=[pl.BlockSpec((rows, pair), lambda s, p, i: (s * nq + i, p)),
                  pl.BlockSpec((seq, pair), lambda s, p, i: (s, p)),
                  pl.BlockSpec((seq, pair), lambda s, p, i: (s, p))],
        out_specs=[pl.BlockSpec((rows, HEAD_PAD), lambda s, p, i: (s * nq + i, p)),
                   pl.BlockSpec((None, 2, 8, rows), lambda s, p, i: (s, p, 0, i))],
        out_shape=[jax.ShapeDtypeStruct((t, E_B), F32), jax.ShapeDtypeStruct((n_seq, N_HEADS, 8, seq), F32)],
        compiler_params=_params(3),
    )(q, k, v)


def _attn_bwd(q, k, v, do, lse_row, delta_row, n_seq, seq):
    tb, group = ATTN_TILE, ATTN_BWD_GROUP
    rows = tb * group
    nk = seq // rows
    n_inner = seq // tb
    pair = 2 * HEAD_PAD

    def body(q_ref, k_ref, v_ref, do_ref, lr_ref, dr_ref, dk_ref, dv_ref, dq_out_ref, dq_ref):
        j = pl.program_id(2)

        @pl.when(j == 0)
        def _():
            dq_ref[...] = jnp.zeros_like(dq_ref)

        row = lax.broadcasted_iota(jnp.int32, (tb, tb), 0)
        col = lax.broadcasted_iota(jnp.int32, (tb, tb), 1)
        causal = col >= row

        def step(i, carry, kinds):
            start = pl.multiple_of(i * tb, tb)
            out = []
            for hh in range(2):
                heads = slice(hh * HEAD_PAD, (hh + 1) * HEAD_PAD)
                qi = q_ref[pl.ds(start, tb), heads]
                doi = do_ref[pl.ds(start, tb), heads]
                dq = None
                for g in range(group):
                    dk_acc, dv_acc = carry[hh * group + g]
                    if kinds[g] != "none":
                        kg = k_ref[g * tb:(g + 1) * tb, heads]
                        st = lax.dot_general(kg, qi, NT, preferred_element_type=F32) * SOFTMAX_SCALE
                        pt = jnp.exp(st - lr_ref[hh, 0:1, pl.ds(start, tb)])
                        if kinds[g] == "diagonal":
                            pt = jnp.where(causal, pt, 0.0)
                        dv_acc = dv_acc + jnp.dot(pt.astype(BF16), doi, preferred_element_type=F32)
                        dpt = lax.dot_general(v_ref[g * tb:(g + 1) * tb, heads], doi, NT, preferred_element_type=F32)
                        dst = (pt * (dpt - dr_ref[hh, 0:1, pl.ds(start, tb)]) * SOFTMAX_SCALE).astype(BF16)
                        dk_acc = dk_acc + jnp.dot(dst, qi, preferred_element_type=F32)
                        term = lax.dot_general(dst, kg, TN, preferred_element_type=F32)
                        dq = term if dq is None else dq + term
                    out.append((dk_acc, dv_acc))
                dq_ref[pl.ds(start, tb), heads] += dq
            return tuple(out)

        one = (jnp.zeros((tb, HEAD_PAD), F32), jnp.zeros((tb, HEAD_PAD), F32))
        carry = (one,) * (2 * group)
        for d in range(group):
            kinds = tuple("full" if g < d else ("diagonal" if g == d else "none") for g in range(group))
            carry = step(j * group + d, carry, kinds)
        carry = lax.fori_loop((j + 1) * group, n_inner, functools.partial(step, kinds=("full",) * group), carry)
        for hh in range(2):
            for g in range(group):
                dk_ref[g * tb:(g + 1) * tb, hh * HEAD_PAD:(hh + 1) * HEAD_PAD] = carry[hh * group + g][0].astype(BF16)
                dv_ref[g * tb:(g + 1) * tb, hh * HEAD_PAD:(hh + 1) * HEAD_PAD] = carry[hh * group + g][1].astype(BF16)

        @pl.when(j == nk - 1)
        def _():
            dq_out_ref[...] = dq_ref[...].astype(BF16)

    t = n_seq * seq
    wide = N_HEADS * HEAD_PAD
    return pl.pallas_call(
        body, name="attention_bwd", grid=(n_seq, N_HEADS // 2, nk),
        in_specs=[pl.BlockSpec((seq, pair), lambda s, p, j: (s, p)),
                  pl.BlockSpec((rows, pair), lambda s, p, j: (s * nk + j, p)),
                  pl.BlockSpec((rows, pair), lambda s, p, j: (s * nk + j, p)),
                  pl.BlockSpec((seq, pair), lambda s, p, j: (s, p)),
                  pl.BlockSpec((None, 2, 8, seq), lambda s, p, j: (s, p, 0, 0)),
                  pl.BlockSpec((None, 2, 8, seq), lambda s, p, j: (s, p, 0, 0))],
        out_specs=[pl.BlockSpec((rows, pair), lambda s, p, j: (s * nk + j, p)),
                   pl.BlockSpec((rows, pair), lambda s, p, j: (s * nk + j, p)),
                   pl.BlockSpec((seq, pair), lambda s, p, j: (s, p))],
        out_shape=[jax.ShapeDtypeStruct((t, wide), BF16), jax.ShapeDtypeStruct((t, wide), BF16),
                   jax.ShapeDtypeStruct((t, wide), BF16)],
        scratch_shapes=[pltpu.VMEM((seq, pair), F32)],
        compiler_params=_params(3),
    )(q, k, v, do, lse_row, delta_row)


def _head_tail(o, gb, x1, w_out, final_norm, target, n_seq, seq):
    tm = LATENT_ROW_TILE
    nt = seq // tm
    n_col = D_MODEL // N_DEV

    head_of_column = np.arange(E_B) // V_HEAD
    selector = jnp.asarray((head_of_column[None, :] == np.arange(N_HEADS)[:, None]).astype(np.float32))

    def body(o_ref, gb_ref, x1_ref, wout_ref, fn_ref, tgt_ref, sel_ref,
             dx2_ref, do_ref, dgb_ref, dr_ref, dwout_ref, dfn_ref, loss_ref):
        first = jnp.logical_and(pl.program_id(0) == 0, pl.program_id(1) == 0)

        @pl.when(first)
        def _():
            dwout_ref[...] = jnp.zeros_like(dwout_ref)
            dfn_ref[...] = jnp.zeros_like(dfn_ref)
            loss_ref[...] = jnp.zeros_like(loss_ref)

        ov, g = o_ref[...], gb_ref[...]
        silu, dsilu = _silu_parts(g)
        gated = (ov * silu).astype(BF16)
        x2 = x1_ref[...] + jnp.dot(gated, wout_ref[...], preferred_element_type=F32)
        r = _rstd(x2)
        xh = x2 * r
        err = xh * fn_ref[...] - tgt_ref[...]
        loss_ref[...] += 0.5 * jnp.sum(jnp.mean(err * err, axis=-1, keepdims=True), axis=0, keepdims=True)
        dy = err / D_MODEL
        dfn_ref[...] += jnp.sum(dy * xh, axis=0, keepdims=True)
        dx2 = _norm_bwd(dy * fn_ref[...], xh, r)
        dx2_ref[...] = dx2
        dx2b = dx2.astype(BF16)
        dw = lax.dot_general(gated, dx2b, TN, preferred_element_type=F32)
        for d in range(N_DEV):
            dwout_ref[d] += dw[:, d * n_col:(d + 1) * n_col]
        dgated = lax.dot_general(dx2b, wout_ref[...], NT, preferred_element_type=F32)
        do = dgated * silu
        dgb_ref[...] = (dgated * ov * dsilu).astype(BF16)
        delta = lax.dot_general(sel_ref[...], do * ov, NT, precision=lax.Precision.HIGHEST, preferred_element_type=F32)
        lane = lax.broadcasted_iota(jnp.int32, (tm, HEAD_PAD), 1)
        low = lane < V_HEAD
        for p in range(N_HEADS // 2):
            do_pair = do[:, p * HEAD_PAD:(p + 1) * HEAD_PAD]
            for hh in range(2):
                h = 2 * p + hh
                mine = do_pair if hh == 0 else pltpu.roll(do_pair, V_HEAD, 1)
                do_ref[:, h * HEAD_PAD:(h + 1) * HEAD_PAD] = jnp.where(low, mine, 0.0).astype(BF16)
                dr_ref[h] = jnp.broadcast_to(delta[h:h + 1, :], (8, tm))

    tok = lambda width: pl.BlockSpec((tm, width), lambda s, i: (s * nt + i, 0))
    t = n_seq * seq
    return pl.pallas_call(
        body, name="head_tail", grid=(n_seq, nt),
        in_specs=[tok(E_B), tok(E_B), tok(D_MODEL), _const_spec(w_out.shape), _const_spec((1, D_MODEL)), tok(D_MODEL),
                  _const_spec((N_HEADS, E_B))],
        out_specs=[tok(D_MODEL), tok(N_HEADS * HEAD_PAD), tok(E_B),
                   pl.BlockSpec((None, N_HEADS, 8, tm), lambda s, i: (s, 0, 0, i)),
                   _acc_spec((N_DEV, E_B, n_col)), _acc_spec((1, D_MODEL)), _acc_spec((1, 1))],
        out_shape=[jax.ShapeDtypeStruct((t, D_MODEL), F32), jax.ShapeDtypeStruct((t, N_HEADS * HEAD_PAD), BF16),
                   jax.ShapeDtypeStruct((t, E_B), BF16),
                   jax.ShapeDtypeStruct((n_seq, N_HEADS, 8, seq), F32), jax.ShapeDtypeStruct((N_DEV, E_B, n_col), F32),
                   jax.ShapeDtypeStruct((1, D_MODEL), F32), jax.ShapeDtypeStruct((1, 1), F32)],
        compiler_params=_params(2),
    )(o, gb, x1, w_out, final_norm, target, selector)


def _adamw_math(w, g, m, v):
    m = ADAM_B1 * m + (1.0 - ADAM_B1) * g
    v = ADAM_B2 * v + (1.0 - ADAM_B2) * jnp.square(g)
    m_hat = m / (1.0 - ADAM_B1 ** ADAM_STEP)
    v_hat = v / (1.0 - ADAM_B2 ** ADAM_STEP)
    delta = -ADAM_LR * (m_hat / (jnp.sqrt(v_hat) + ADAM_EPS) + ADAM_WD * w)
    return delta, m, v


def _adamw_reduce(own, parts, w, m, v):
    rows, cols = w.shape
    br = 256
    n_parts = parts.shape[0]

    def body(own_ref, p_ref, w_ref, m_ref, v_ref, g_ref, d_ref, nm_ref, nv_ref):
        g = own_ref[...]
        for k in range(n_parts):
            g = g + p_ref[k].astype(F32)
        g_ref[...] = g
        d_ref[...], nm_ref[...], nv_ref[...] = _adamw_math(w_ref[...], g, m_ref[...], v_ref[...])

    blk = pl.BlockSpec((br, cols), lambda i: (i, 0))
    return pl.pallas_call(
        body, name="adamw_reduce", grid=(rows // br,),
        in_specs=[blk, pl.BlockSpec((n_parts, br, cols), lambda i: (0, i, 0)), blk, blk, blk],
        out_specs=[blk] * 4, out_shape=[jax.ShapeDtypeStruct((rows, cols), F32)] * 4,
        compiler_params=_params(1),
    )(own, parts, w, m, v)


def _adamw_reduce_many(parts_list, states):
    n = len(parts_list)

    def body(*refs):
        ins, outs = refs[:4 * n], refs[4 * n:]
        for j in range(n):
            p_ref, w_ref, m_ref, v_ref = ins[4 * j:4 * j + 4]
            g = p_ref[0].astype(F32)
            for k in range(1, p_ref.shape[0]):
                g = g + p_ref[k].astype(F32)
            outs[4 * j][...] = g
            outs[4 * j + 1][...], outs[4 * j + 2][...], outs[4 * j + 3][...] = _adamw_math(
                w_ref[...], g, m_ref[...], v_ref[...])

    flat_in = [a for p, st in zip(parts_list, states) for a in (p,) + tuple(st)]
    shapes = [jax.ShapeDtypeStruct(st[0].shape, F32) for st in states for _ in range(4)]
    flat = pl.pallas_call(body, name="adamw_reduce_many", out_shape=shapes,
                          compiler_params=pltpu.CompilerParams(vmem_limit_bytes=VMEM_LIMIT))(*flat_in)
    return [tuple(flat[4 * j:4 * j + 4]) for j in range(n)]


def _sum_parts(parts):
    def body(p_ref, o_ref):
        g = p_ref[0]
        for k in range(1, N_DEV):
            g = g + p_ref[k]
        o_ref[...] = g

    return pl.pallas_call(body, name="small_grad_sum", out_shape=jax.ShapeDtypeStruct(parts.shape[1:], F32))(parts)


def _adamw_small(gs, ws, ms, vs):
    n = len(gs)

    def body(*refs):
        ins, outs = refs[:4 * n], refs[4 * n:]
        for j in range(n):
            g_ref, w_ref, m_ref, v_ref = ins[j], ins[n + j], ins[2 * n + j], ins[3 * n + j]
            outs[3 * j][...], outs[3 * j + 1][...], outs[3 * j + 2][...] = _adamw_math(
                w_ref[...], g_ref[...], m_ref[...], v_ref[...])

    shapes = [jax.ShapeDtypeStruct(g.shape, F32) for g in gs for _ in range(3)]
    flat = pl.pallas_call(body, name="adamw_small", out_shape=shapes)(*gs, *ws, *ms, *vs)
    return [tuple(flat[3 * j:3 * j + 3]) for j in range(n)]


SMALL_ROWS = 8
LOSS_LANE = D_MODEL - 1


def _pack_small(g_fn, g_kvn, g_bn, g_an, g_ckvn, g_qn, g_conv, loss_part):
    def body(fn_ref, kvn_ref, bn_ref, an_ref, ckvn_ref, qn_ref, conv_ref, loss_ref, o_ref):
        o_ref[0:1, :] = fn_ref[...]
        o_ref[1:2, :] = kvn_ref[...]
        o_ref[2:3, :] = bn_ref[...]
        o_ref[3:4, :] = an_ref[...]
        lane = lax.broadcasted_iota(jnp.int32, (1, D_MODEL), 1)
        o_ref[4:5, :] = jnp.where(lane == LOSS_LANE, loss_ref[...], 0.0)
        o_ref[4:5, 0:KV_RANK] = ckvn_ref[...]
        o_ref[4:5, KV_RANK:KV_RANK + Q_RANK] = qn_ref[...]
        o_ref[5:8, :] = conv_ref[0:3, :]

    return pl.pallas_call(body, name="pack_small", out_shape=jax.ShapeDtypeStruct((SMALL_ROWS, D_MODEL), F32))(
        g_fn, g_kvn, g_bn, g_an, g_ckvn, g_qn, g_conv, loss_part)


def _pad_cols(a, width):
    return jnp.pad(a, ((0, 0), (0, width - a.shape[1])))


def _dkv_to_padded(a):
    r = a.shape[0]
    z = jnp.zeros((r, ROPE_LO), a.dtype)
    z2 = jnp.zeros((r, HEAD_PAD - ROPE_LO - QK_ROPE), a.dtype)
    return jnp.concatenate([a[:, :KV_RANK], z, a[:, KV_RANK:], z2], axis=1)


def _dkv_from_padded(a):
    return jnp.concatenate([a[:, :KV_RANK], a[:, KV_RANK + ROPE_LO:KV_RANK + ROPE_LO + QK_ROPE]], axis=1)


def _unstack_cols(a):
    return jnp.transpose(a, (1, 0, 2)).reshape(a.shape[1], N_DEV * a.shape[2])


def kernel(x, positions, a_norm, a_w_in, a_conv, a_w_out, kv_norm, w_dkv, ckv_norm, w_ukv, b_norm, b_w_in, b_q_norm, b_w_uq, b_w_out, final_norm, loss_target, m_a_norm, m_a_w_in, m_a_conv, m_a_w_out, m_kv_norm, m_w_dkv, m_ckv_norm, m_w_ukv, m_b_norm, m_b_w_in, m_b_q_norm, m_b_w_uq, m_b_w_out, m_final_norm, v_a_norm, v_a_w_in, v_a_conv, v_a_w_out, v_kv_norm, v_w_dkv, v_ckv_norm, v_w_ukv, v_b_norm, v_b_w_in, v_b_q_norm, v_b_w_uq, v_b_w_out, v_final_norm):
    n_seq, seq, _ = x.shape
    t = n_seq * seq
    me = 4 * lax.axis_index("x") + 2 * lax.axis_index("y") + lax.axis_index("c")

    big = {
        "a_w_in": (a_w_in[0], m_a_w_in[0], v_a_w_in[0]),
        "a_w_out": (a_w_out[0], m_a_w_out[0], v_a_w_out[0]),
        "w_dkv": tuple(_dkv_to_padded(a) for a in (w_dkv, m_w_dkv, v_w_dkv)),
        "w_ukv": (w_ukv, m_w_ukv, v_w_ukv),
        "b_w_in": (b_w_in[0], m_b_w_in[0], v_b_w_in[0]),
        "b_w_uq": tuple(_pad_cols(a[0], HEAD_PAD) for a in (b_w_uq, m_b_w_uq, v_b_w_uq)),
        "b_w_out": (b_w_out[0], m_b_w_out[0], v_b_w_out[0]),
    }
    names = list(big)
    first_names, later_names = names[:2], names[2:]
    gathered = _all_gather([big[n][0].astype(BF16) for n in first_names] + [a_norm, a_conv[0]])
    a_norm_f = gathered[2].reshape(1, D_MODEL)
    a_conv_f = _unstack_cols(gathered[3])
    w_a_in = gathered[0]
    w_a_out = gathered[1].reshape(D_MODEL, D_MODEL)

    x2d = x.reshape(t, D_MODEL)
    tgt = loss_target.reshape(t, D_MODEL)
    pos = positions.astype(F32).reshape(t, 1)
    rc = _rope_consts()
    kvn, ckvn, bn, qn, fn = kv_norm.reshape(1, -1), ckv_norm.reshape(1, -1), b_norm, b_q_norm, final_norm.reshape(1, -1)

    x1, proj, conv, later = _conv_fwd(x2d, a_norm_f, w_a_in, a_conv_f, w_a_out, n_seq, seq,
                                      [big[n][0].astype(BF16) for n in later_names])
    full = dict(zip(later_names, later))
    w_dkv_f = full["w_dkv"].reshape(D_MODEL, CKR_PAD)
    w_ukv_f = _unstack_cols(full["w_ukv"])
    w_b_in = full["b_w_in"].reshape(D_MODEL, Q_RANK + E_B)
    w_uq_f = _unstack_cols(full["b_w_uq"])
    w_b_out = _unstack_cols(full["b_w_out"])
    layer_b = (kvn, w_dkv_f, ckvn, w_ukv_f, bn, w_b_in, qn, w_uq_f)
    ckr, cq, gb, q, k, v, cosv, sinv = _proj_fwd(x1, pos, rc, *layer_b)
    o, lse_row = _attn_fwd(q, k, v, n_seq, seq)
    dx2, do, dgb, delta_row, g_b_out, g_fn, loss_part = _head_tail(o, gb, x1, w_b_out, fn, tgt, n_seq, seq)
    dk, dv, dq = _attn_bwd(q, k, v, do, lse_row, delta_row, n_seq, seq)
    dx1, g_uq, g_b_in, g_ukv, g_dkv, g_qn, g_bn, g_ckvn, g_kvn = _proj_bwd(
        dq, dk, dv, dgb, cq, ckr, x1, dx2, cosv, sinv, rc, *layer_b)
    stacks = {
        "w_dkv": g_dkv.reshape(N_DEV, D_MODEL // N_DEV, CKR_PAD),
        "w_ukv": g_ukv,
        "b_w_in": g_b_in.reshape(N_DEV, D_MODEL // N_DEV, Q_RANK + E_B),
        "b_w_uq": g_uq,
        "b_w_out": g_b_out,
    }
    (dx, h_a, dproj, g_a_out, g_an, g_conv), later_parts = _conv_bwd(
        dx1, x2d, proj, conv, a_norm_f, w_a_in, a_conv_f, w_a_out, n_seq, seq, [stacks[n] for n in later_names])
    parts = dict(zip(later_names, later_parts))
    small = _pack_small(g_fn, g_kvn, g_bn, g_an, g_ckvn, g_qn, g_conv, loss_part)
    a_in_own, a_in_got, (parts["a_w_out"],), small_parts = _w_in_grad_exchange(
        h_a, dproj, D_MODEL * 4 // N_DEV, (me // 2).reshape(1).astype(jnp.int32),
        [g_a_out.astype(BF16).reshape(N_DEV, D_MODEL // N_DEV, D_MODEL)], small)

    outs = {"a_w_in": _adamw_reduce(a_in_own, a_in_got, *big["a_w_in"])}
    outs.update(zip(names[1:], _adamw_reduce_many([parts[n] for n in names[1:]], [big[n] for n in names[1:]])))
    outs["w_dkv"] = tuple(_dkv_from_padded(a) for a in outs["w_dkv"])
    outs["b_w_uq"] = tuple(a[:, :QK_NOPE + QK_ROPE] for a in outs["b_w_uq"])
    for n in ("a_w_in", "a_w_out", "b_w_in", "b_w_uq", "b_w_out"):
        outs[n] = tuple(a[None] for a in outs[n])

    total = _sum_parts(small_parts)
    loss = total[4, LOSS_LANE]
    shard = D_MODEL // N_DEV
    g_small = {
        "final_norm": total[0], "kv_norm": total[1], "b_norm": total[2:3],
        "a_norm": lax.dynamic_slice_in_dim(total[3:4], me * shard, shard, axis=1),
        "ckv_norm": total[4, 0:KV_RANK], "b_q_norm": total[4:5, KV_RANK:KV_RANK + Q_RANK],
        "a_conv": lax.dynamic_slice_in_dim(total[5:8], me * shard, shard, axis=1)[None],
    }
    small_state = {
        "final_norm": (final_norm, m_final_norm, v_final_norm), "kv_norm": (kv_norm, m_kv_norm, v_kv_norm),
        "b_norm": (b_norm, m_b_norm, v_b_norm), "a_norm": (a_norm, m_a_norm, v_a_norm),
        "ckv_norm": (ckv_norm, m_ckv_norm, v_ckv_norm), "b_q_norm": (b_q_norm, m_b_q_norm, v_b_q_norm),
        "a_conv": (a_conv, m_a_conv, v_a_conv),
    }
    small_names = list(g_small)
    as2d = lambda a: a.reshape(-1, a.shape[-1])
    upd = _adamw_small([as2d(g_small[n]) for n in small_names],
                       *[[as2d(small_state[n][j]) for n in small_names] for j in range(3)])
    for n, u in zip(small_names, upd):
        outs[n] = (g_small[n],) + tuple(a.reshape(g_small[n].shape) for a in u)

    order = ["a_norm", "a_w_in", "a_conv", "a_w_out", "kv_norm", "w_dkv", "ckv_norm", "w_ukv", "b_norm", "b_w_in",
             "b_q_norm", "b_w_uq", "b_w_out", "final_norm"]
    result = [loss, dx.reshape(n_seq, seq, D_MODEL)]
    for j in range(4):
        result += [outs[n][j] for n in order]
    return tuple(result)
```

```python
import functools
import math

import numpy as np
import jax
import jax.numpy as jnp
from jax import lax
from jax.experimental import pallas as pl
from jax.experimental.pallas import tpu as pltpu

F32 = jnp.float32
BF16 = jnp.bfloat16
SAVED = BF16

D_MODEL = 1024
N_HEADS = 8
QK_NOPE = 64
QK_ROPE = 32
V_HEAD = 64
KV_RANK = 256
Q_RANK = 384
E_B = N_HEADS * V_HEAD
HEAD_PAD = 128
CKR_PAD = KV_RANK + HEAD_PAD
ROPE_LO = QK_NOPE
ROPE_HALF = QK_ROPE // 2
ROPE_THETA = 10000.0
SOFTMAX_SCALE = 1.0 / math.sqrt(QK_NOPE + QK_ROPE)
LOG2_E = math.log2(math.e)
LN_2 = math.log(2.0)
EPS = 1e-6
N_DEV = 8

ADAM_LR = 0.001
ADAM_B1 = 0.9
ADAM_B2 = 0.999
ADAM_EPS = 1e-08
ADAM_WD = 0.01
ADAM_STEP = 10

CONV_BWD_TILE = 256
CONV_FWD_TILE = 512
LATENT_BWD_TILE = 512
LATENT_FWD_TILE = 1024
HEAD_TAIL_TILE = 512
PICKUP_BLOCK = 1
ATTN_TILE = 512
ATTN_GROUP = 4
ATTN_BWD_GROUP = 4
VMEM_LIMIT = 56 * 1024 * 1024

MESH = pl.DeviceIdType.MESH
NT = (((1,), (1,)), ((), ()))
TN = (((0,), (0,)), ((), ()))


def _dot(a, b):
    return jnp.dot(a.astype(BF16), b.astype(BF16), preferred_element_type=F32)


def _dot_nt(a, b):
    return lax.dot_general(a.astype(BF16), b.astype(BF16), NT, preferred_element_type=F32)


def _dot_tn(a, b):
    return lax.dot_general(a.astype(BF16), b.astype(BF16), TN, preferred_element_type=F32)


def _rstd(x):
    return lax.rsqrt(jnp.mean(x * x, axis=-1, keepdims=True) + EPS)


def _norm_bwd(a, xh, r):
    return r * (a - xh * jnp.mean(a * xh, axis=-1, keepdims=True))


def _silu_parts(g):
    sg = jax.nn.sigmoid(g)
    return g * sg, sg * (1.0 + g * (1.0 - sg))


def _rope_consts():
    inv = (ROPE_THETA ** (-np.arange(0, QK_ROPE, 2, dtype=np.float32) / QK_ROPE)).astype(np.float32)
    t = np.zeros((8, HEAD_PAD), np.float32)
    t[0, ROPE_LO:ROPE_LO + ROPE_HALF] = inv
    t[0, ROPE_LO + ROPE_HALF:ROPE_LO + QK_ROPE] = inv
    t[1, ROPE_LO:ROPE_LO + ROPE_HALF] = -1.0
    t[2, ROPE_LO + ROPE_HALF:ROPE_LO + QK_ROPE] = 1.0
    return jnp.asarray(t)


def _rope_angles(pos, rc):
    ang = pos * rc[0:1, :]
    return jnp.cos(ang), jnp.sin(ang)


def _rope_tables(cosv, sinv, rc):
    return cosv, sinv * rc[1:2, :], sinv * rc[2:3, :]


def _rope(x, ct, s1, s2):
    up = pltpu.roll(x, HEAD_PAD - ROPE_HALF, 1)
    dn = pltpu.roll(x, ROPE_HALF, 1)
    return x * ct + up * s1 + dn * s2


def _const_spec(shape):
    nd = len(shape)
    return pl.BlockSpec(shape, lambda *_: (0,) * nd, pipeline_mode=pl.Buffered(1))


def _acc_spec(shape):
    nd = len(shape)
    return pl.BlockSpec(shape, lambda *_: (0,) * nd)


def _params(n_axes):
    return pltpu.CompilerParams(dimension_semantics=("arbitrary",) * n_axes, vmem_limit_bytes=VMEM_LIMIT)


def _place():
    x, y, c = lax.axis_index("x"), lax.axis_index("y"), lax.axis_index("c")
    return x, y, c, 4 * x + 2 * y + c


def _peer(x, y, c, mask):
    px = 1 - x if mask & 4 else x
    py = 1 - y if mask & 2 else y
    pc = 1 - c if mask & 1 else c
    return (px, py, pc), 4 * px + 2 * py + pc


ANY_SPEC = pl.BlockSpec(memory_space=pl.ANY)


def _comm_sems(n):
    return [pltpu.SemaphoreType.DMA((n, N_DEV - 1)), pltpu.SemaphoreType.DMA((n, N_DEV - 1)), pltpu.SemaphoreType.DMA((n,))]


def _gather_copies(ins, outs, sems):
    send_sems, recv_sems, local_sems = sems
    x, y, c, me = _place()
    starts, waits = [], []
    for w in range(len(ins)):
        mine = pltpu.make_async_copy(ins[w], outs[w].at[me], local_sems.at[w])
        starts.append(mine)
        waits.append(mine)
        for mask in range(1, N_DEV):
            peer, peer_idx = _peer(x, y, c, mask)
            starts.append(pltpu.make_async_remote_copy(
                src_ref=ins[w], dst_ref=outs[w].at[me], send_sem=send_sems.at[w, mask - 1],
                recv_sem=recv_sems.at[w, mask - 1], device_id=peer, device_id_type=MESH))
            waits.append(pltpu.make_async_remote_copy(
                src_ref=ins[w], dst_ref=outs[w].at[peer_idx], send_sem=send_sems.at[w, mask - 1],
                recv_sem=recv_sems.at[w, mask - 1], device_id=peer, device_id_type=MESH))
    return starts, waits


def _scatter_copies(ins, outs, sems):
    send_sems, recv_sems, local_sems = sems
    x, y, c, me = _place()
    copies = []
    for w in range(len(ins)):
        copies.append(pltpu.make_async_copy(ins[w].at[me], outs[w].at[0], local_sems.at[w]))
        for mask in range(1, N_DEV):
            peer, peer_idx = _peer(x, y, c, mask)
            copies.append(pltpu.make_async_remote_copy(
                src_ref=ins[w].at[peer_idx], dst_ref=outs[w].at[mask], send_sem=send_sems.at[w, mask - 1],
                recv_sem=recv_sems.at[w, mask - 1], device_id=peer, device_id_type=MESH))
    return copies, copies


def _stacked(arrays):
    return [jax.ShapeDtypeStruct((N_DEV,) + a.shape, a.dtype) for a in arrays]


def _all_gather(shards):
    n = len(shards)

    def body(*refs):
        ins, outs = refs[:n], refs[n:2 * n]
        send_sems, recv_sems, local_sems = refs[2 * n:]
        x, y, c, me = _place()
        sibling = (x, y, 1 - c)
        chips = [(1 - x, y), (x, 1 - y), (1 - x, 1 - y)]

        def copy(w, k, block, to, src=None):
            idx = 4 * block[0] + 2 * block[1] + block[2]
            return pltpu.make_async_remote_copy(
                src_ref=outs[w].at[idx] if src is None else src, dst_ref=outs[w].at[idx],
                send_sem=send_sems.at[w, k], recv_sem=recv_sems.at[w, k], device_id=to, device_id_type=MESH)

        local, sent = [], []
        for w in range(n):
            mine = pltpu.make_async_copy(ins[w], outs[w].at[me], local_sems.at[w])
            mine.start()
            local.append(mine)
            first = [copy(w, 0, (x, y, c), sibling, src=ins[w])]
            first += [copy(w, 1 + j, (x, y, c), (*chip, c), src=ins[w]) for j, chip in enumerate(chips)]
            for cp in first:
                cp.start()
            sent += first
        for w in range(n):
            for j, chip in enumerate(chips):
                copy(w, 1 + j, (*chip, c), (x, y, c)).wait_recv()
                onward = copy(w, 4 + j, (*chip, c), sibling)
                onward.start()
                sent.append(onward)
        for w in range(n):
            copy(w, 0, sibling, (x, y, c)).wait_recv()
            for j, chip in enumerate(chips):
                copy(w, 4 + j, (*chip, 1 - c), (x, y, c)).wait_recv()
        for cp in sent:
            cp.wait_send()
        for cp in local:
            cp.wait()

    return pl.pallas_call(
        body, name="weight_all_gather", out_shape=_stacked(shards),
        in_specs=[ANY_SPEC] * n, out_specs=[ANY_SPEC] * n, scratch_shapes=_comm_sems(n),
    )(*shards)


def _conv_fwd(x, a_norm, w_in, conv_w, w_out, n_seq, seq, later_shards):
    tm = CONV_FWD_TILE
    nt = seq // tm
    n_col = w_in.shape[2]
    n_later = len(later_shards)

    def body(x_ref, an_ref, win_ref, cw_ref, wout_ref, *rest):
        shard_refs, rest = rest[:n_later], rest[n_later:]
        x1_ref, proj_ref, conv_ref = rest[:3]
        stack_refs, rest = rest[3:3 + n_later], rest[3 + n_later:]
        prev_ref, wide_ref, sems = rest[0], rest[1], rest[2:]
        step = pl.program_id(0) * nt + pl.program_id(1)

        @pl.when(step == 0)
        def _():
            for cp in _gather_copies(shard_refs, stack_refs, sems)[0]:
                cp.start()

        @pl.when(pl.program_id(1) == 0)
        def _():
            prev_ref[...] = jnp.zeros_like(prev_ref)

        xv = x_ref[...]
        h = (xv * _rstd(xv) * an_ref[...]).astype(BF16)
        for d in range(N_DEV):
            part = jnp.dot(h, win_ref[d], preferred_element_type=F32)
            wide_ref[:, d * n_col:(d + 1) * n_col] = part
            proj_ref[:, d * n_col:(d + 1) * n_col] = part.astype(SAVED)
        b = wide_ref[:, 0:D_MODEL]
        v = wide_ref[:, D_MODEL:2 * D_MODEL] * wide_ref[:, 2 * D_MODEL:3 * D_MODEL]
        g = wide_ref[:, 3 * D_MODEL:4 * D_MODEL]
        w0, w1, w2 = cw_ref[0:1, :], cw_ref[1:2, :], cw_ref[2:3, :]
        conv_ref[...] = w0 * pltpu.roll(v, 2, 0) + w1 * pltpu.roll(v, 1, 0) + w2 * v
        rows = lax.broadcasted_iota(jnp.int32, (8, D_MODEL), 0)
        p8, v8 = prev_ref[...], v[0:8]
        back1 = jnp.where(rows < 1, pltpu.roll(p8, 1, 0), pltpu.roll(v8, 1, 0))
        back2 = jnp.where(rows < 2, pltpu.roll(p8, 2, 0), pltpu.roll(v8, 2, 0))
        conv_ref[0:8, :] = w0 * back2 + w1 * back1 + w2 * v8
        prev_ref[...] = v[tm - 8:tm]
        silu, _ = _silu_parts(g)
        yv = silu * b * conv_ref[...]
        x1_ref[...] = xv + _dot(yv, wout_ref[...])

        @pl.when(step == n_seq * nt - 1)
        def _():
            for cp in _gather_copies(shard_refs, stack_refs, sems)[1]:
                cp.wait()

    tok = lambda width: pl.BlockSpec((tm, width), lambda s, i: (s * nt + i, 0))
    t = n_seq * seq
    outs = pl.pallas_call(
        body, name="conv_mixer_fwd", grid=(n_seq, nt),
        in_specs=[tok(D_MODEL), _const_spec((1, D_MODEL)), _const_spec(w_in.shape), _const_spec((3, D_MODEL)),
                  _const_spec(w_out.shape)] + [ANY_SPEC] * n_later,
        out_specs=[tok(D_MODEL), tok(4 * D_MODEL), tok(D_MODEL)] + [ANY_SPEC] * n_later,
        out_shape=[jax.ShapeDtypeStruct((t, D_MODEL), F32), jax.ShapeDtypeStruct((t, 4 * D_MODEL), SAVED),
                   jax.ShapeDtypeStruct((t, D_MODEL), F32)] + _stacked(later_shards),
        scratch_shapes=[pltpu.VMEM((8, D_MODEL), F32), pltpu.VMEM((tm, 4 * D_MODEL), F32)] + _comm_sems(n_later),
        compiler_params=_params(2),
    )(x, a_norm, w_in, conv_w, w_out, *later_shards)
    return outs[0], outs[1], outs[2], outs[3:]


def _conv_bwd(dx1, x, proj, conv, a_norm, w_in, conv_w, w_out, n_seq, seq, ready_stacks):
    tm = CONV_BWD_TILE
    nt = seq // tm
    n_col = w_in.shape[2]
    n_ready = len(ready_stacks)

    def body(dx1_ref, x_ref, proj_ref, conv_ref, an_ref, win_ref, cw_ref, wout_ref, *rest):
        ready_refs, rest = rest[:n_ready], rest[n_ready:]
        dx_ref, h_ref, dproj_ref, dwout_ref, dan_ref, dcw_ref = rest[:6]
        part_refs, rest = rest[6:6 + n_ready], rest[6 + n_ready:]
        next_ref, d1_ref, d2_ref = rest[:3]
        sems = rest[3:]
        step = pl.program_id(0) * nt + pl.program_id(1)
        first = step == 0

        @pl.when(first)
        def _():
            for cp in _scatter_copies(ready_refs, part_refs, sems)[0]:
                cp.start()
            dwout_ref[...] = jnp.zeros_like(dwout_ref)
            dan_ref[...] = jnp.zeros_like(dan_ref)
            dcw_ref[...] = jnp.zeros_like(dcw_ref)

        @pl.when(pl.program_id(1) == 0)
        def _():
            next_ref[...] = jnp.zeros_like(next_ref)

        dx1v = dx1_ref[...]
        dy = _dot_nt(dx1v, wout_ref[...])
        b = proj_ref[:, 0:D_MODEL].astype(F32)
        cc = proj_ref[:, D_MODEL:2 * D_MODEL].astype(F32)
        u = proj_ref[:, 2 * D_MODEL:3 * D_MODEL].astype(F32)
        g = proj_ref[:, 3 * D_MODEL:4 * D_MODEL].astype(F32)
        cv = conv_ref[...]
        silu, dsilu = _silu_parts(g)
        per_part = D_MODEL // n_col

        def back_through_w_in(part, grad):
            grad = grad.astype(BF16)
            dproj_ref[:, part * D_MODEL:(part + 1) * D_MODEL] = grad
            term = None
            for j in range(per_part):
                d = part * per_part + j
                piece = lax.dot_general(grad[:, j * n_col:(j + 1) * n_col], win_ref[d], NT, preferred_element_type=F32)
                term = piece if term is None else term + piece
            return term

        dh = back_through_w_in(0, dy * silu * cv)
        dh += back_through_w_in(3, dy * b * cv * dsilu)
        dwout_ref[...] += _dot_tn(silu * b * cv, dx1v)
        dconv = dy * silu * b
        d1_ref[...] = pltpu.roll(dconv, tm - 1, 0)
        d2_ref[...] = pltpu.roll(dconv, tm - 2, 0)
        rows = lax.broadcasted_iota(jnp.int32, (8, D_MODEL), 0)
        n8, c8 = next_ref[...], dconv[tm - 8:tm]
        d1_ref[tm - 8:tm, :] = jnp.where(rows >= 7, pltpu.roll(n8, 7, 0), pltpu.roll(c8, 7, 0))
        d2_ref[tm - 8:tm, :] = jnp.where(rows >= 6, pltpu.roll(n8, 6, 0), pltpu.roll(c8, 6, 0))
        next_ref[...] = dconv[0:8]
        d1, d2 = d1_ref[...], d2_ref[...]
        v = cc * u
        dcw_ref[0:1, :] += jnp.sum(d2 * v, axis=0, keepdims=True)
        dcw_ref[1:2, :] += jnp.sum(d1 * v, axis=0, keepdims=True)
        dcw_ref[2:3, :] += jnp.sum(dconv * v, axis=0, keepdims=True)
        dv = cw_ref[0:1, :] * d2 + cw_ref[1:2, :] * d1 + cw_ref[2:3, :] * dconv
        dh += back_through_w_in(1, dv * u)
        dh += back_through_w_in(2, dv * cc)
        xv = x_ref[...]
        r = _rstd(xv)
        xh = xv * r
        h_ref[...] = (xh * an_ref[...]).T.astype(BF16)
        dan_ref[...] += jnp.sum(dh * xh, axis=0, keepdims=True)
        dx_ref[...] = dx1v + _norm_bwd(dh * an_ref[...], xh, r)

        @pl.when(step == n_seq * nt - 1)
        def _():
            for cp in _scatter_copies(ready_refs, part_refs, sems)[1]:
                cp.wait()

    tok = lambda width: pl.BlockSpec((tm, width), lambda s, i: (s * nt + nt - 1 - i, 0))
    t = n_seq * seq
    outs = pl.pallas_call(
        body, name="conv_mixer_bwd", grid=(n_seq, nt),
        in_specs=[tok(D_MODEL), tok(D_MODEL), tok(4 * D_MODEL), tok(D_MODEL), _const_spec((1, D_MODEL)),
                  _const_spec(w_in.shape), _const_spec((3, D_MODEL)), _const_spec(w_out.shape)] + [ANY_SPEC] * n_ready,
        out_specs=[tok(D_MODEL), pl.BlockSpec((D_MODEL, tm), lambda s, i: (0, s * nt + nt - 1 - i)),
                   tok(4 * D_MODEL), _acc_spec((D_MODEL, D_MODEL)),
                   _acc_spec((1, D_MODEL)), _acc_spec((8, D_MODEL))] + [ANY_SPEC] * n_ready,
        out_shape=[jax.ShapeDtypeStruct((t, D_MODEL), F32), jax.ShapeDtypeStruct((D_MODEL, t), BF16),
                   jax.ShapeDtypeStruct((t, 4 * D_MODEL), BF16), jax.ShapeDtypeStruct((D_MODEL, D_MODEL), F32),
                   jax.ShapeDtypeStruct((1, D_MODEL), F32), jax.ShapeDtypeStruct((8, D_MODEL), F32)]
        + [jax.ShapeDtypeStruct(a.shape, a.dtype) for a in ready_stacks],
        scratch_shapes=[pltpu.VMEM((8, D_MODEL), F32), pltpu.VMEM((tm, D_MODEL), F32), pltpu.VMEM((tm, D_MODEL), F32)]
        + _comm_sems(n_ready),
        compiler_params=_params(2),
    )(dx1, x, proj, conv, a_norm, w_in, conv_w, w_out, *ready_stacks)
    return outs[:6], outs[6:]


def _w_in_grad_exchange(a_t, b, n_col, chip, ready_stacks, small):
    r, t = a_t.shape
    bt = 2048
    nk = t // bt
    n_ready = len(ready_stacks)
    n_chip = N_DEV // 2
    n_remote = N_DEV - 2

    def body(chip_ref, a_ref, b_ref, *rest):
        ready_refs, small_ref, rest = rest[:n_ready], rest[n_ready], rest[n_ready + 1:]
        own_ref, got_ref, rest = rest[0], rest[1], rest[2:]
        part_refs, small_all_ref, rest = rest[:n_ready], rest[n_ready], rest[n_ready + 1:]
        acc_ref, land_ref, stage_ref, pair_send, pair_recv, chip_send, chip_recv = rest[:7]
        scatter_sems, gather_sems = rest[7:10], rest[10:13]
        s, k = pl.program_id(0), pl.program_id(1)
        x, y, c, _ = _place()
        chip = 2 * x + y

        @pl.when(jnp.logical_and(s == 0, k == 0))
        def _():
            for cp in _scatter_copies(ready_refs, part_refs, scatter_sems)[0]:
                cp.start()
            for cp in _gather_copies([small_ref], [small_all_ref], gather_sems)[0]:
                cp.start()

        for parity in range(2):
            @pl.when(s % 2 == parity)
            def _(parity=parity):
                @pl.when(k == 0)
                def _():
                    acc_ref[parity] = jnp.zeros((r, n_col), F32)

                acc_ref[parity] += jnp.dot(a_ref[...], b_ref[...], preferred_element_type=F32)

        def to_sibling(step):
            return pltpu.make_async_remote_copy(
                src_ref=acc_ref.at[step % 2], dst_ref=land_ref, send_sem=pair_send.at[step], recv_sem=pair_recv.at[step],
                device_id=(x, y, 1 - c), device_id_type=MESH)

        def to_owner(nth):
            owner_chip = (chip + 1 + nth) % n_chip
            slot = jnp.bitwise_xor(chip, owner_chip) - 1
            return pltpu.make_async_remote_copy(
                src_ref=stage_ref.at[nth % 2], dst_ref=got_ref.at[slot], send_sem=chip_send.at[nth],
                recv_sem=chip_recv.at[slot], device_id=(owner_chip // 2, owner_chip % 2, c), device_id_type=MESH)

        for step in range(N_DEV):
            owner_core = step % 2

            @pl.when(jnp.logical_and(s == step, k == nk - 1))
            def _(step=step, owner_core=owner_core):
                @pl.when(c != owner_core)
                def _():
                    to_sibling(step).start()

                if step >= 1:
                    @pl.when(c == owner_core)
                    def _():
                        to_sibling(step - 1).wait_send()

            if step < N_DEV - 1:
                pickup = jnp.logical_and(s == step + 1, k == min(PICKUP_BLOCK, nk - 1))
            else:
                pickup = jnp.logical_and(s == step, k == nk - 1)

            @pl.when(jnp.logical_and(pickup, c == owner_core))
            def _(step=step):
                to_sibling(step).wait_recv()
                total = acc_ref[step % 2] + land_ref[...]
                if step < n_remote:
                    nth = step // 2
                    if nth >= 2:
                        to_owner(nth - 2).wait_send()
                    stage_ref[nth % 2] = total.astype(BF16)
                    to_owner(nth).start()
                else:
                    own_ref[...] = total

        @pl.when(jnp.logical_and(s == N_DEV - 1, k == nk - 1))
        def _():
            @pl.when(c != (N_DEV - 1) % 2)
            def _():
                to_sibling(N_DEV - 1).wait_send()

            to_owner(1).wait_send()
            to_owner(2).wait_send()
            for slot in range(n_chip - 1):
                pltpu.make_async_remote_copy(
                    src_ref=stage_ref.at[0], dst_ref=got_ref.at[slot], send_sem=chip_send.at[0],
                    recv_sem=chip_recv.at[slot], device_id=(x, y, c), device_id_type=MESH).wait_recv()
            for cp in _scatter_copies(ready_refs, part_refs, scatter_sems)[1]:
                cp.wait()
            for cp in _gather_copies([small_ref], [small_all_ref], gather_sems)[1]:
                cp.wait()

    def owner_block(s, k, chip):
        return (k, 2 * ((chip[0] + 1 + s // 2) % n_chip) + s % 2)

    grid_spec = pltpu.PrefetchScalarGridSpec(
        num_scalar_prefetch=1, grid=(N_DEV, nk),
        in_specs=[pl.BlockSpec((r, bt), lambda s, k, chip: (0, k)), pl.BlockSpec((bt, n_col), owner_block)]
        + [ANY_SPEC] * (n_ready + 1),
        out_specs=[pl.BlockSpec((r, n_col), lambda s, k, chip: (0, 0)), ANY_SPEC] + [ANY_SPEC] * (n_ready + 1),
        scratch_shapes=[pltpu.VMEM((2, r, n_col), F32), pltpu.VMEM((r, n_col), F32), pltpu.VMEM((2, r, n_col), BF16),
                        pltpu.SemaphoreType.DMA((N_DEV,)), pltpu.SemaphoreType.DMA((N_DEV,)),
                        pltpu.SemaphoreType.DMA((n_chip - 1,)), pltpu.SemaphoreType.DMA((n_chip - 1,))]
        + _comm_sems(n_ready) + _comm_sems(1))
    outs = pl.pallas_call(
        body, name="w_in_grad_exchange", grid_spec=grid_spec,
        out_shape=[jax.ShapeDtypeStruct((r, n_col), F32), jax.ShapeDtypeStruct((n_chip - 1, r, n_col), BF16)]
        + [jax.ShapeDtypeStruct(p.shape, p.dtype) for p in ready_stacks] + _stacked([small]),
        compiler_params=_params(2),
    )(chip, a_t, b, *ready_stacks, small)
    return outs[0], outs[1], outs[2:2 + n_ready], outs[2 + n_ready]


def _proj_fwd(x1, pos, rc, kv_norm, w_dkv, ckv_norm, w_ukv, b_norm, b_w_in, q_norm, w_uq):
    t = x1.shape[0]
    tm = LATENT_FWD_TILE

    def body(x1_ref, pos_ref, rc_ref, kvn_ref, wdkv_ref, ckvn_ref, wukv_ref, bn_ref, bwin_ref, qn_ref, wuq_ref,
             ckr_ref, cq_ref, gb_ref, q_ref, k_ref, v_ref, cos_ref, sin_ref):
        xv = x1_ref[...]
        xh = xv * _rstd(xv)
        ckr = _dot(xh * kvn_ref[...], wdkv_ref[...])
        pb = _dot(xh * bn_ref[...], bwin_ref[...])
        ckr_ref[...] = ckr
        ckv = ckr[:, 0:KV_RANK]
        kv = _dot(ckv * _rstd(ckv) * ckvn_ref[...], wukv_ref[...])
        cosv, sinv = _rope_angles(pos_ref[...], rc_ref[...])
        cos_ref[...] = cosv
        sin_ref[...] = sinv
        ct, s1, s2 = _rope_tables(cosv, sinv, rc_ref[...])
        k_rope = _rope(ckr[:, KV_RANK:CKR_PAD], ct, s1, s2)
        lane = lax.broadcasted_iota(jnp.int32, (tm, HEAD_PAD), 1)
        low = lane < QK_NOPE
        for h in range(N_HEADS):
            kv_h = kv[:, h * HEAD_PAD:(h + 1) * HEAD_PAD]
            k_ref[:, h * HEAD_PAD:(h + 1) * HEAD_PAD] = jnp.where(low, kv_h, k_rope).astype(BF16)
            v_ref[:, h * HEAD_PAD:(h + 1) * HEAD_PAD] = jnp.where(low, pltpu.roll(kv_h, V_HEAD, 1), 1.0).astype(BF16)
        cq = pb[:, 0:Q_RANK]
        cq_ref[...] = cq
        gb_ref[...] = pb[:, Q_RANK:Q_RANK + E_B]
        q = _dot(cq * _rstd(cq) * qn_ref[...], wuq_ref[...])
        for h in range(N_HEADS):
            q_ref[:, h * HEAD_PAD:(h + 1) * HEAD_PAD] = _rope(
                q[:, h * HEAD_PAD:(h + 1) * HEAD_PAD], ct, s1, s2).astype(BF16)

    tok = lambda width: pl.BlockSpec((tm, width), lambda i: (i, 0))
    weights = [kv_norm, w_dkv, ckv_norm, w_ukv, b_norm, b_w_in, q_norm, w_uq]
    wide = N_HEADS * HEAD_PAD
    return pl.pallas_call(
        body, name="latent_proj_fwd", grid=(t // tm,),
        in_specs=[tok(D_MODEL), tok(1), _const_spec(rc.shape)] + [_const_spec(w.shape) for w in weights],
        out_specs=[tok(CKR_PAD), tok(Q_RANK), tok(E_B), tok(wide), tok(wide), tok(wide), tok(HEAD_PAD), tok(HEAD_PAD)],
        out_shape=[jax.ShapeDtypeStruct((t, CKR_PAD), F32), jax.ShapeDtypeStruct((t, Q_RANK), F32),
                   jax.ShapeDtypeStruct((t, E_B), F32), jax.ShapeDtypeStruct((t, wide), BF16),
                   jax.ShapeDtypeStruct((t, wide), BF16), jax.ShapeDtypeStruct((t, wide), BF16),
                   jax.ShapeDtypeStruct((t, HEAD_PAD), F32), jax.ShapeDtypeStruct((t, HEAD_PAD), F32)],
        compiler_params=_params(1),
    )(x1, pos, rc, *weights)


def _proj_bwd(dq, dk, dv, dgb, cq, ckr, x1, dx2, cosv, sinv, rc, kv_norm, w_dkv, ckv_norm, w_ukv, b_norm, b_w_in, q_norm, w_uq):
    t = x1.shape[0]
    tm = LATENT_BWD_TILE
    wide = N_HEADS * HEAD_PAD

    def body(dq_ref, dk_ref, dv_ref, dgb_ref, cq_ref, ckr_ref, x1_ref, dx2_ref, cos_ref, sin_ref, rc_ref,
             kvn_ref, wdkv_ref, ckvn_ref, wukv_ref, bn_ref, bwin_ref, qn_ref, wuq_ref,
             dx1_ref, dwuq_ref, dwbin_ref, dwukv_ref, dwdkv_ref, dqn_ref, dbn_ref, dckvn_ref, dkvn_ref, dqu_ref, dkv_ref):
        @pl.when(pl.program_id(0) == 0)
        def _():
            for ref in (dwuq_ref, dwbin_ref, dwukv_ref, dwdkv_ref, dqn_ref, dbn_ref, dckvn_ref, dkvn_ref):
                ref[...] = jnp.zeros_like(ref)

        ct, s1, s2 = _rope_tables(cos_ref[...], sin_ref[...], rc_ref[...])
        lane = lax.broadcasted_iota(jnp.int32, (tm, HEAD_PAD), 1)
        low = lane < QK_NOPE
        xv = x1_ref[...]
        r = _rstd(xv)
        xh = xv * r
        dk_rope = jnp.zeros((tm, HEAD_PAD), F32)
        for h in range(N_HEADS):
            dk_h = dk_ref[:, h * HEAD_PAD:(h + 1) * HEAD_PAD].astype(F32)
            dv_h = dv_ref[:, h * HEAD_PAD:(h + 1) * HEAD_PAD].astype(F32)
            dkv_ref[:, h * HEAD_PAD:(h + 1) * HEAD_PAD] = jnp.where(low, dk_h, pltpu.roll(dv_h, V_HEAD, 1)).astype(BF16)
            dk_rope += dk_h
        rope_lanes = jnp.logical_and(lane >= ROPE_LO, lane < ROPE_LO + QK_ROPE)
        dk_rope = jnp.where(rope_lanes, _rope(dk_rope, ct, -s1, -s2), 0.0)
        ckv = ckr_ref[:, 0:KV_RANK]
        rk = _rstd(ckv)
        ckh = ckv * rk
        ckvn = (ckh * ckvn_ref[...]).astype(BF16)
        dwukv = lax.dot_general(ckvn, dkv_ref[...], TN, preferred_element_type=F32)
        for h in range(N_HEADS):
            dwukv_ref[h] += dwukv[:, h * HEAD_PAD:(h + 1) * HEAD_PAD]
        dckvn = lax.dot_general(dkv_ref[...], wukv_ref[...], NT, preferred_element_type=F32)
        dckvn_ref[...] += jnp.sum(dckvn * ckh, axis=0, keepdims=True)
        dckv = _norm_bwd(dckvn * ckvn_ref[...], ckh, rk)
        dckr = jnp.concatenate([dckv, dk_rope], axis=1).astype(BF16)
        dwdkv_ref[...] += _dot_tn(xh * kvn_ref[...], dckr)
        dh2 = lax.dot_general(dckr, wdkv_ref[...], NT, preferred_element_type=F32)
        for h in range(N_HEADS):
            dqu_ref[:, h * HEAD_PAD:(h + 1) * HEAD_PAD] = _rope(
                dq_ref[:, h * HEAD_PAD:(h + 1) * HEAD_PAD].astype(F32), ct, -s1, -s2).astype(BF16)
        cq = cq_ref[...]
        rq = _rstd(cq)
        cqh = cq * rq
        cqn = (cqh * qn_ref[...]).astype(BF16)
        dwuq = lax.dot_general(cqn, dqu_ref[...], TN, preferred_element_type=F32)
        for h in range(N_HEADS):
            dwuq_ref[h] += dwuq[:, h * HEAD_PAD:(h + 1) * HEAD_PAD]
        dcqn = lax.dot_general(dqu_ref[...], wuq_ref[...], NT, preferred_element_type=F32)
        dqn_ref[...] += jnp.sum(dcqn * cqh, axis=0, keepdims=True)
        dcq = _norm_bwd(dcqn * qn_ref[...], cqh, rq)
        dpb = jnp.concatenate([dcq.astype(BF16), dgb_ref[...]], axis=1)
        dwbin_ref[...] += _dot_tn(xh * bn_ref[...], dpb)
        dh3 = lax.dot_general(dpb, bwin_ref[...], NT, preferred_element_type=F32)
        dkvn_ref[...] += jnp.sum(dh2 * xh, axis=0, keepdims=True)
        dbn_ref[...] += jnp.sum(dh3 * xh, axis=0, keepdims=True)
        dx1_ref[...] = dx2_ref[...] + _norm_bwd(dh2 * kvn_ref[...] + dh3 * bn_ref[...], xh, r)

    tok = lambda width: pl.BlockSpec((tm, width), lambda i: (i, 0))
    weights = [kv_norm, w_dkv, ckv_norm, w_ukv, b_norm, b_w_in, q_norm, w_uq]
    acc_shapes = [(N_HEADS, Q_RANK, HEAD_PAD), (D_MODEL, Q_RANK + E_B), (N_HEADS, KV_RANK, HEAD_PAD), (D_MODEL, CKR_PAD),
                  (1, Q_RANK), (1, D_MODEL), (1, KV_RANK), (1, D_MODEL)]
    return pl.pallas_call(
        body, name="latent_proj_bwd", grid=(t // tm,),
        in_specs=[tok(wide), tok(wide), tok(wide), tok(E_B), tok(Q_RANK), tok(CKR_PAD), tok(D_MODEL), tok(D_MODEL),
                  tok(HEAD_PAD), tok(HEAD_PAD), _const_spec(rc.shape)] + [_const_spec(w.shape) for w in weights],
        out_specs=[tok(D_MODEL)] + [_acc_spec(s) for s in acc_shapes],
        out_shape=[jax.ShapeDtypeStruct((t, D_MODEL), F32)] + [jax.ShapeDtypeStruct(s, F32) for s in acc_shapes],
        scratch_shapes=[pltpu.VMEM((tm, wide), BF16), pltpu.VMEM((tm, wide), BF16)],
        compiler_params=_params(1),
    )(dq, dk, dv, dgb, cq, ckr, x1, dx2, cosv, sinv, rc, *weights)


def _attn_fwd(q, k, v, n_seq, seq):
    tb, group = ATTN_TILE, ATTN_GROUP
    rows = tb * group
    nq = seq // rows
    pair = 2 * HEAD_PAD

    def body(q_ref, k_ref, v_ref, o_ref, lr_ref):
        i = pl.program_id(2)
        row = lax.broadcasted_iota(jnp.int32, (tb, tb), 0)
        col = lax.broadcasted_iota(jnp.int32, (tb, tb), 1)
        causal = col <= row

        def step(j, carry, kinds):
            start = pl.multiple_of(j * tb, tb)
            out = []
            for g in range(group):
                for hh in range(2):
                    m, acc = carry[2 * g + hh]
                    if kinds[g] != "none":
                        heads = slice(hh * HEAD_PAD, (hh + 1) * HEAD_PAD)
                        s = lax.dot_general(q_ref[g * tb:(g + 1) * tb, heads], k_ref[pl.ds(start, tb), heads], NT,
                                            preferred_element_type=F32) * (SOFTMAX_SCALE * LOG2_E)
                        if kinds[g] == "diagonal":
                            s = jnp.where(causal, s, -jnp.inf)
                        m_new = jnp.maximum(m, jnp.max(s, axis=-1, keepdims=True))
                        p = jnp.exp2(s - m_new)
                        acc = jnp.exp2(m - m_new) * acc + jnp.dot(
                            p.astype(BF16), v_ref[pl.ds(start, tb), heads], preferred_element_type=F32)
                        m = m_new
                    out.append((m, acc))
            return tuple(out)

        one = (jnp.full((tb, 1), -jnp.inf, F32), jnp.zeros((tb, HEAD_PAD), F32))
        carry = lax.fori_loop(0, i * group, functools.partial(step, kinds=("full",) * group), (one,) * (2 * group))
        for d in range(group):
            kinds = tuple("none" if g < d else ("diagonal" if g == d else "full") for g in range(group))
            carry = step(i * group + d, carry, kinds)
        lane = lax.broadcasted_iota(jnp.int32, (tb, HEAD_PAD), 1)
        low = lane < V_HEAD
        for g in range(group):
            halves = []
            for hh in range(2):
                m, acc = carry[2 * g + hh]
                swapped = pltpu.roll(acc, V_HEAD, 1)
                halves.append(acc / swapped)
                lse = (m * LN_2) + jnp.log(jnp.where(low, swapped, acc))
                lr_ref[hh, :, g * tb:(g + 1) * tb] = lse.T[0:8, :]
            o_ref[g * tb:(g + 1) * tb, :] = jnp.where(low, halves[0], pltpu.roll(halves[1], V_HEAD, 1))

    t = n_seq * seq
    return pl.pallas_call(
        body, name="attention_fwd", grid=(n_seq, N_HEADS // 2, nq),
        in_specs=[pl.BlockSpec((rows, pair), lambda s, p, i: (s * nq + i, p)),
                  pl.BlockSpec((seq, pair), lambda s, p, i: (s, p)),
                  pl.BlockSpec((seq, pair), lambda s, p, i: (s, p))],
        out_specs=[pl.BlockSpec((rows, HEAD_PAD), lambda s, p, i: (s * nq + i, p)),
                   pl.BlockSpec((None, 2, 8, rows), lambda s, p, i: (s, p, 0, i))],
        out_shape=[jax.ShapeDtypeStruct((t, E_B), F32), jax.ShapeDtypeStruct((n_seq, N_HEADS, 8, seq), F32)],
        compiler_params=_params(3),
    )(q, k, v)


def _attn_bwd(q, k, v, do, lse_row, delta_row, n_seq, seq):
    tb, group = ATTN_TILE, ATTN_BWD_GROUP
    rows = tb * group
    nk = seq // rows
    n_inner = seq // tb
    pair = 2 * HEAD_PAD

    def body(q_ref, k_ref, v_ref, do_ref, lr_ref, dr_ref, dk_ref, dv_ref, dq_out_ref, dq_ref):
        j = pl.program_id(2)

        @pl.when(j == 0)
        def _():
            dq_ref[...] = jnp.zeros_like(dq_ref)

        row = lax.broadcasted_iota(jnp.int32, (tb, tb), 0)
        col = lax.broadcasted_iota(jnp.int32, (tb, tb), 1)
        causal = col >= row

        def step(i, carry, kinds):
            start = pl.multiple_of(i * tb, tb)
            out = []
            for hh in range(2):
                heads = slice(hh * HEAD_PAD, (hh + 1) * HEAD_PAD)
                qi = q_ref[pl.ds(start, tb), heads]
                doi = do_ref[pl.ds(start, tb), heads]
                dq = None
                for g in range(group):
                    dk_acc, dv_acc = carry[hh * group + g]
                    if kinds[g] != "none":
                        kg = k_ref[g * tb:(g + 1) * tb, heads]
                        st = lax.dot_general(kg, qi, NT, preferred_element_type=F32) * SOFTMAX_SCALE
                        pt = jnp.exp(st - lr_ref[hh, 0:1, pl.ds(start, tb)])
                        if kinds[g] == "diagonal":
                            pt = jnp.where(causal, pt, 0.0)
                        dv_acc = dv_acc + jnp.dot(pt.astype(BF16), doi, preferred_element_type=F32)
                        dpt = lax.dot_general(v_ref[g * tb:(g + 1) * tb, heads], doi, NT, preferred_element_type=F32)
                        dst = (pt * (dpt - dr_ref[hh, 0:1, pl.ds(start, tb)]) * SOFTMAX_SCALE).astype(BF16)
                        dk_acc = dk_acc + jnp.dot(dst, qi, preferred_element_type=F32)
                        term = lax.dot_general(dst, kg, TN, preferred_element_type=F32)
                        dq = term if dq is None else dq + term
                    out.append((dk_acc, dv_acc))
                dq_ref[pl.ds(start, tb), heads] += dq
            return tuple(out)

        one = (jnp.zeros((tb, HEAD_PAD), F32), jnp.zeros((tb, HEAD_PAD), F32))
        carry = (one,) * (2 * group)
        for d in range(group):
            kinds = tuple("full" if g < d else ("diagonal" if g == d else "none") for g in range(group))
            carry = step(j * group + d, carry, kinds)
        carry = lax.fori_loop((j + 1) * group, n_inner, functools.partial(step, kinds=("full",) * group), carry)
        for hh in range(2):
            for g in range(group):
                dk_ref[g * tb:(g + 1) * tb, hh * HEAD_PAD:(hh + 1) * HEAD_PAD] = carry[hh * group + g][0].astype(BF16)
                dv_ref[g * tb:(g + 1) * tb, hh * HEAD_PAD:(hh + 1) * HEAD_PAD] = carry[hh * group + g][1].astype(BF16)

        @pl.when(j == nk - 1)
        def _():
            dq_out_ref[...] = dq_ref[...].astype(BF16)

    t = n_seq * seq
    wide = N_HEADS * HEAD_PAD
    return pl.pallas_call(
        body, name="attention_bwd", grid=(n_seq, N_HEADS // 2, nk),
        in_specs=[pl.BlockSpec((seq, pair), lambda s, p, j: (s, p)),
                  pl.BlockSpec((rows, pair), lambda s, p, j: (s * nk + j, p)),
                  pl.BlockSpec((rows, pair), lambda s, p, j: (s * nk + j, p)),
                  pl.BlockSpec((seq, pair), lambda s, p, j: (s, p)),
                  pl.BlockSpec((None, 2, 8, seq), lambda s, p, j: (s, p, 0, 0)),
                  pl.BlockSpec((None, 2, 8, seq), lambda s, p, j: (s, p, 0, 0))],
        out_specs=[pl.BlockSpec((rows, pair), lambda s, p, j: (s * nk + j, p)),
                   pl.BlockSpec((rows, pair), lambda s, p, j: (s * nk + j, p)),
                   pl.BlockSpec((seq, pair), lambda s, p, j: (s, p))],
        out_shape=[jax.ShapeDtypeStruct((t, wide), BF16), jax.ShapeDtypeStruct((t, wide), BF16),
                   jax.ShapeDtypeStruct((t, wide), BF16)],
        scratch_shapes=[pltpu.VMEM((seq, pair), F32)],
        compiler_params=_params(3),
    )(q, k, v, do, lse_row, delta_row)


def _head_tail(o, gb, x1, w_out, final_norm, target, n_seq, seq):
    tm = HEAD_TAIL_TILE
    nt = seq // tm
    n_col = D_MODEL // N_DEV

    head_of_column = np.arange(E_B) // V_HEAD
    selector = jnp.asarray((head_of_column[None, :] == np.arange(N_HEADS)[:, None]).astype(np.float32))

    def body(o_ref, gb_ref, x1_ref, wout_ref, fn_ref, tgt_ref, sel_ref,
             dx2_ref, do_ref, dgb_ref, dr_ref, dwout_ref, dfn_ref, loss_ref):
        first = jnp.logical_and(pl.program_id(0) == 0, pl.program_id(1) == 0)

        @pl.when(first)
        def _():
            dwout_ref[...] = jnp.zeros_like(dwout_ref)
            dfn_ref[...] = jnp.zeros_like(dfn_ref)
            loss_ref[...] = jnp.zeros_like(loss_ref)

        ov, g = o_ref[...], gb_ref[...]
        silu, dsilu = _silu_parts(g)
        gated = (ov * silu).astype(BF16)
        x2 = x1_ref[...] + jnp.dot(gated, wout_ref[...], preferred_element_type=F32)
        r = _rstd(x2)
        xh = x2 * r
        err = xh * fn_ref[...] - tgt_ref[...]
        loss_ref[...] += 0.5 * jnp.sum(jnp.mean(err * err, axis=-1, keepdims=True), axis=0, keepdims=True)
        dy = err / D_MODEL
        dfn_ref[...] += jnp.sum(dy * xh, axis=0, keepdims=True)
        dx2 = _norm_bwd(dy * fn_ref[...], xh, r)
        dx2_ref[...] = dx2
        dx2b = dx2.astype(BF16)
        dw = lax.dot_general(gated, dx2b, TN, preferred_element_type=F32)
        for d in range(N_DEV):
            dwout_ref[d] += dw[:, d * n_col:(d + 1) * n_col]
        dgated = lax.dot_general(dx2b, wout_ref[...], NT, preferred_element_type=F32)
        do = dgated * silu
        dgb_ref[...] = (dgated * ov * dsilu).astype(BF16)
        delta = lax.dot_general(sel_ref[...], do * ov, NT, precision=lax.Precision.HIGHEST, preferred_element_type=F32)
        lane = lax.broadcasted_iota(jnp.int32, (tm, HEAD_PAD), 1)
        low = lane < V_HEAD
        for p in range(N_HEADS // 2):
            do_pair = do[:, p * HEAD_PAD:(p + 1) * HEAD_PAD]
            for hh in range(2):
                h = 2 * p + hh
                mine = do_pair if hh == 0 else pltpu.roll(do_pair, V_HEAD, 1)
                do_ref[:, h * HEAD_PAD:(h + 1) * HEAD_PAD] = jnp.where(low, mine, 0.0).astype(BF16)
                dr_ref[h] = jnp.broadcast_to(delta[h:h + 1, :], (8, tm))

    tok = lambda width: pl.BlockSpec((tm, width), lambda s, i: (s * nt + i, 0))
    t = n_seq * seq
    return pl.pallas_call(
        body, name="head_tail", grid=(n_seq, nt),
        in_specs=[tok(E_B), tok(E_B), tok(D_MODEL), _const_spec(w_out.shape), _const_spec((1, D_MODEL)), tok(D_MODEL),
                  _const_spec((N_HEADS, E_B))],
        out_specs=[tok(D_MODEL), tok(N_HEADS * HEAD_PAD), tok(E_B),
                   pl.BlockSpec((None, N_HEADS, 8, tm), lambda s, i: (s, 0, 0, i)),
                   _acc_spec((N_DEV, E_B, n_col)), _acc_spec((1, D_MODEL)), _acc_spec((1, 1))],
        out_shape=[jax.ShapeDtypeStruct((t, D_MODEL), F32), jax.ShapeDtypeStruct((t, N_HEADS * HEAD_PAD), BF16),
                   jax.ShapeDtypeStruct((t, E_B), BF16),
                   jax.ShapeDtypeStruct((n_seq, N_HEADS, 8, seq), F32), jax.ShapeDtypeStruct((N_DEV, E_B, n_col), F32),
                   jax.ShapeDtypeStruct((1, D_MODEL), F32), jax.ShapeDtypeStruct((1, 1), F32)],
        compiler_params=_params(2),
    )(o, gb, x1, w_out, final_norm, target, selector)


def _adamw_math(w, g, m, v):
    m = ADAM_B1 * m + (1.0 - ADAM_B1) * g
    v = ADAM_B2 * v + (1.0 - ADAM_B2) * jnp.square(g)
    m_hat = m / (1.0 - ADAM_B1 ** ADAM_STEP)
    v_hat = v / (1.0 - ADAM_B2 ** ADAM_STEP)
    delta = -ADAM_LR * (m_hat / (jnp.sqrt(v_hat) + ADAM_EPS) + ADAM_WD * w)
    return delta, m, v


def _adamw_reduce(own, parts, w, m, v):
    rows, cols = w.shape
    br = 256
    n_parts = parts.shape[0]

    def body(own_ref, p_ref, w_ref, m_ref, v_ref, g_ref, d_ref, nm_ref, nv_ref):
        g = own_ref[...]
        for k in range(n_parts):
            g = g + p_ref[k].astype(F32)
        g_ref[...] = g
        d_ref[...], nm_ref[...], nv_ref[...] = _adamw_math(w_ref[...], g, m_ref[...], v_ref[...])

    blk = pl.BlockSpec((br, cols), lambda i: (i, 0))
    return pl.pallas_call(
        body, name="adamw_reduce", grid=(rows // br,),
        in_specs=[blk, pl.BlockSpec((n_parts, br, cols), lambda i: (0, i, 0)), blk, blk, blk],
        out_specs=[blk] * 4, out_shape=[jax.ShapeDtypeStruct((rows, cols), F32)] * 4,
        compiler_params=_params(1),
    )(own, parts, w, m, v)


def _adamw_reduce_many(parts_list, states):
    n = len(parts_list)

    def body(*refs):
        ins, outs = refs[:4 * n], refs[4 * n:]
        for j in range(n):
            p_ref, w_ref, m_ref, v_ref = ins[4 * j:4 * j + 4]
            g = p_ref[0].astype(F32)
            for k in range(1, p_ref.shape[0]):
                g = g + p_ref[k].astype(F32)
            outs[4 * j][...] = g
            outs[4 * j + 1][...], outs[4 * j + 2][...], outs[4 * j + 3][...] = _adamw_math(
                w_ref[...], g, m_ref[...], v_ref[...])

    flat_in = [a for p, st in zip(parts_list, states) for a in (p,) + tuple(st)]
    shapes = [jax.ShapeDtypeStruct(st[0].shape, F32) for st in states for _ in range(4)]
    flat = pl.pallas_call(body, name="adamw_reduce_many", out_shape=shapes,
                          compiler_params=pltpu.CompilerParams(vmem_limit_bytes=VMEM_LIMIT))(*flat_in)
    return [tuple(flat[4 * j:4 * j + 4]) for j in range(n)]


def _sum_parts(parts):
    def body(p_ref, o_ref):
        g = p_ref[0]
        for k in range(1, N_DEV):
            g = g + p_ref[k]
        o_ref[...] = g

    return pl.pallas_call(body, name="small_grad_sum", out_shape=jax.ShapeDtypeStruct(parts.shape[1:], F32))(parts)


def _adamw_small(gs, ws, ms, vs):
    n = len(gs)

    def body(*refs):
        ins, outs = refs[:4 * n], refs[4 * n:]
        for j in range(n):
            g_ref, w_ref, m_ref, v_ref = ins[j], ins[n + j], ins[2 * n + j], ins[3 * n + j]
            outs[3 * j][...], outs[3 * j + 1][...], outs[3 * j + 2][...] = _adamw_math(
                w_ref[...], g_ref[...], m_ref[...], v_ref[...])

    shapes = [jax.ShapeDtypeStruct(g.shape, F32) for g in gs for _ in range(3)]
    flat = pl.pallas_call(body, name="adamw_small", out_shape=shapes)(*gs, *ws, *ms, *vs)
    return [tuple(flat[3 * j:3 * j + 3]) for j in range(n)]


SMALL_ROWS = 8
LOSS_LANE = D_MODEL - 1


def _pack_small(g_fn, g_kvn, g_bn, g_an, g_ckvn, g_qn, g_conv, loss_part):
    def body(fn_ref, kvn_ref, bn_ref, an_ref, ckvn_ref, qn_ref, conv_ref, loss_ref, o_ref):
        o_ref[0:1, :] = fn_ref[...]
        o_ref[1:2, :] = kvn_ref[...]
        o_ref[2:3, :] = bn_ref[...]
        o_ref[3:4, :] = an_ref[...]
        lane = lax.broadcasted_iota(jnp.int32, (1, D_MODEL), 1)
        o_ref[4:5, :] = jnp.where(lane == LOSS_LANE, loss_ref[...], 0.0)
        o_ref[4:5, 0:KV_RANK] = ckvn_ref[...]
        o_ref[4:5, KV_RANK:KV_RANK + Q_RANK] = qn_ref[...]
        o_ref[5:8, :] = conv_ref[0:3, :]

    return pl.pallas_call(body, name="pack_small", out_shape=jax.ShapeDtypeStruct((SMALL_ROWS, D_MODEL), F32))(
        g_fn, g_kvn, g_bn, g_an, g_ckvn, g_qn, g_conv, loss_part)


def _pad_cols(a, width):
    return jnp.pad(a, ((0, 0), (0, width - a.shape[1])))


def _dkv_to_padded(a):
    r = a.shape[0]
    z = jnp.zeros((r, ROPE_LO), a.dtype)
    z2 = jnp.zeros((r, HEAD_PAD - ROPE_LO - QK_ROPE), a.dtype)
    return jnp.concatenate([a[:, :KV_RANK], z, a[:, KV_RANK:], z2], axis=1)


def _dkv_from_padded(a):
    return jnp.concatenate([a[:, :KV_RANK], a[:, KV_RANK + ROPE_LO:KV_RANK + ROPE_LO + QK_ROPE]], axis=1)


def _unstack_cols(a):
    return jnp.transpose(a, (1, 0, 2)).reshape(a.shape[1], N_DEV * a.shape[2])


def kernel(x, positions, a_norm, a_w_in, a_conv, a_w_out, kv_norm, w_dkv, ckv_norm, w_ukv, b_norm, b_w_in, b_q_norm, b_w_uq, b_w_out, final_norm, loss_target, m_a_norm, m_a_w_in, m_a_conv, m_a_w_out, m_kv_norm, m_w_dkv, m_ckv_norm, m_w_ukv, m_b_norm, m_b_w_in, m_b_q_norm, m_b_w_uq, m_b_w_out, m_final_norm, v_a_norm, v_a_w_in, v_a_conv, v_a_w_out, v_kv_norm, v_w_dkv, v_ckv_norm, v_w_ukv, v_b_norm, v_b_w_in, v_b_q_norm, v_b_w_uq, v_b_w_out, v_final_norm):
    n_seq, seq, _ = x.shape
    t = n_seq * seq
    me = 4 * lax.axis_index("x") + 2 * lax.axis_index("y") + lax.axis_index("c")

    big = {
        "a_w_in": (a_w_in[0], m_a_w_in[0], v_a_w_in[0]),
        "a_w_out": (a_w_out[0], m_a_w_out[0], v_a_w_out[0]),
        "w_dkv": tuple(_dkv_to_padded(a) for a in (w_dkv, m_w_dkv, v_w_dkv)),
        "w_ukv": (w_ukv, m_w_ukv, v_w_ukv),
        "b_w_in": (b_w_in[0], m_b_w_in[0], v_b_w_in[0]),
        "b_w_uq": tuple(_pad_cols(a[0], HEAD_PAD) for a in (b_w_uq, m_b_w_uq, v_b_w_uq)),
        "b_w_out": (b_w_out[0], m_b_w_out[0], v_b_w_out[0]),
    }
    names = list(big)
    first_names, later_names = names[:2], names[2:]
    gathered = _all_gather([big[n][0].astype(BF16) for n in first_names] + [a_norm, a_conv[0]])
    a_norm_f = gathered[2].reshape(1, D_MODEL)
    a_conv_f = _unstack_cols(gathered[3])
    w_a_in = gathered[0]
    w_a_out = gathered[1].reshape(D_MODEL, D_MODEL)

    x2d = x.reshape(t, D_MODEL)
    tgt = loss_target.reshape(t, D_MODEL)
    pos = positions.astype(F32).reshape(t, 1)
    rc = _rope_consts()
    kvn, ckvn, bn, qn, fn = kv_norm.reshape(1, -1), ckv_norm.reshape(1, -1), b_norm, b_q_norm, final_norm.reshape(1, -1)

    x1, proj, conv, later = _conv_fwd(x2d, a_norm_f, w_a_in, a_conv_f, w_a_out, n_seq, seq,
                                      [big[n][0].astype(BF16) for n in later_names])
    full = dict(zip(later_names, later))
    w_dkv_f = full["w_dkv"].reshape(D_MODEL, CKR_PAD)
    w_ukv_f = _unstack_cols(full["w_ukv"])
    w_b_in = full["b_w_in"].reshape(D_MODEL, Q_RANK + E_B)
    w_uq_f = _unstack_cols(full["b_w_uq"])
    w_b_out = _unstack_cols(full["b_w_out"])
    layer_b = (kvn, w_dkv_f, ckvn, w_ukv_f, bn, w_b_in, qn, w_uq_f)
    ckr, cq, gb, q, k, v, cosv, sinv = _proj_fwd(x1, pos, rc, *layer_b)
    o, lse_row = _attn_fwd(q, k, v, n_seq, seq)
    dx2, do, dgb, delta_row, g_b_out, g_fn, loss_part = _head_tail(o, gb, x1, w_b_out, fn, tgt, n_seq, seq)
    dk, dv, dq = _attn_bwd(q, k, v, do, lse_row, delta_row, n_seq, seq)
    dx1, g_uq, g_b_in, g_ukv, g_dkv, g_qn, g_bn, g_ckvn, g_kvn = _proj_bwd(
        dq, dk, dv, dgb, cq, ckr, x1, dx2, cosv, sinv, rc, *layer_b)
    stacks = {
        "w_dkv": g_dkv.reshape(N_DEV, D_MODEL // N_DEV, CKR_PAD),
        "w_ukv": g_ukv,
        "b_w_in": g_b_in.reshape(N_DEV, D_MODEL // N_DEV, Q_RANK + E_B),
        "b_w_uq": g_uq,
        "b_w_out": g_b_out,
    }
    (dx, h_a, dproj, g_a_out, g_an, g_conv), later_parts = _conv_bwd(
        dx1, x2d, proj, conv, a_norm_f, w_a_in, a_conv_f, w_a_out, n_seq, seq, [stacks[n] for n in later_names])
    parts = dict(zip(later_names, later_parts))
    small = _pack_small(g_fn, g_kvn, g_bn, g_an, g_ckvn, g_qn, g_conv, loss_part)
    a_in_own, a_in_got, (parts["a_w_out"],), small_parts = _w_in_grad_exchange(
        h_a, dproj, D_MODEL * 4 // N_DEV, (me // 2).reshape(1).astype(jnp.int32),
        [g_a_out.astype(BF16).reshape(N_DEV, D_MODEL // N_DEV, D_MODEL)], small)

    outs = {"a_w_in": _adamw_reduce(a_in_own, a_in_got, *big["a_w_in"])}
    outs.update(zip(names[1:], _adamw_reduce_many([parts[n] for n in names[1:]], [big[n] for n in names[1:]])))
    outs["w_dkv"] = tuple(_dkv_from_padded(a) for a in outs["w_dkv"])
    outs["b_w_uq"] = tuple(a[:, :QK_NOPE + QK_ROPE] for a in outs["b_w_uq"])
    for n in ("a_w_in", "a_w_out", "b_w_in", "b_w_uq", "b_w_out"):
        outs[n] = tuple(a[None] for a in outs[n])

    total = _sum_parts(small_parts)
    loss = total[4, LOSS_LANE]
    shard = D_MODEL // N_DEV
    g_small = {
        "final_norm": total[0], "kv_norm": total[1], "b_norm": total[2:3],
        "a_norm": lax.dynamic_slice_in_dim(total[3:4], me * shard, shard, axis=1),
        "ckv_norm": total[4, 0:KV_RANK], "b_q_norm": total[4:5, KV_RANK:KV_RANK + Q_RANK],
        "a_conv": lax.dynamic_slice_in_dim(total[5:8], me * shard, shard, axis=1)[None],
    }
    small_state = {
        "final_norm": (final_norm, m_final_norm, v_final_norm), "kv_norm": (kv_norm, m_kv_norm, v_kv_norm),
        "b_norm": (b_norm, m_b_norm, v_b_norm), "a_norm": (a_norm, m_a_norm, v_a_norm),
        "ckv_norm": (ckv_norm, m_ckv_norm, v_ckv_norm), "b_q_norm": (b_q_norm, m_b_q_norm, v_b_q_norm),
        "a_conv": (a_conv, m_a_conv, v_a_conv),
    }
    small_names = list(g_small)
    as2d = lambda a: a.reshape(-1, a.shape[-1])
    upd = _adamw_small([as2d(g_small[n]) for n in small_names],
                       *[[as2d(small_state[n][j]) for n in small_names] for j in range(3)])
    for n, u in zip(small_names, upd):
        outs[n] = (g_small[n],) + tuple(a.reshape(g_small[n].shape) for a in u)

    order = ["a_norm", "a_w_in", "a_conv", "a_w_out", "kv_norm", "w_dkv", "ckv_norm", "w_ukv", "b_norm", "b_w_in",
             "b_q_norm", "b_w_uq", "b_w_out", "final_norm"]
    result = [loss, dx.reshape(n_seq, seq, D_MODEL)]
    for j in range(4):
        result += [outs[n][j] for n in order]
    return tuple(result)
```

```python
import functools
import math

import numpy as np
import jax
import jax.numpy as jnp
from jax import lax
from jax.experimental import pallas as pl
from jax.experimental.pallas import tpu as pltpu

F32 = jnp.float32
BF16 = jnp.bfloat16
SAVED = BF16

D_MODEL = 1024
N_HEADS = 8
QK_NOPE = 64
QK_ROPE = 32
V_HEAD = 64
KV_RANK = 256
Q_RANK = 384
E_B = N_HEADS * V_HEAD
HEAD_PAD = 128
CKR_PAD = KV_RANK + HEAD_PAD
ROPE_LO = QK_NOPE
ROPE_HALF = QK_ROPE // 2
ROPE_THETA = 10000.0
SOFTMAX_SCALE = 1.0 / math.sqrt(QK_NOPE + QK_ROPE)
LOG2_E = math.log2(math.e)
LN_2 = math.log(2.0)
EPS = 1e-6
N_DEV = 8

ADAM_LR = 0.001
ADAM_B1 = 0.9
ADAM_B2 = 0.999
ADAM_EPS = 1e-08
ADAM_WD = 0.01
ADAM_STEP = 10

CONV_BWD_TILE = 256
CONV_FWD_TILE = 512
LATENT_BWD_TILE = 512
LATENT_FWD_TILE = 1024
HEAD_TAIL_TILE = 512
PICKUP_BLOCK = 1
ATTN_TILE = 512
ATTN_GROUP = 4
ATTN_BWD_GROUP = 4
VMEM_LIMIT = 56 * 1024 * 1024

MESH = pl.DeviceIdType.MESH
NT = (((1,), (1,)), ((), ()))
TN = (((0,), (0,)), ((), ()))


def _dot(a, b):
    return jnp.dot(a.astype(BF16), b.astype(BF16), preferred_element_type=F32)


def _dot_nt(a, b):
    return lax.dot_general(a.astype(BF16), b.astype(BF16), NT, preferred_element_type=F32)


def _dot_tn(a, b):
    return lax.dot_general(a.astype(BF16), b.astype(BF16), TN, preferred_element_type=F32)


def _rstd(x):
    return lax.rsqrt(jnp.mean(x * x, axis=-1, keepdims=True) + EPS)


def _norm_bwd(a, xh, r):
    return r * (a - xh * jnp.mean(a * xh, axis=-1, keepdims=True))


def _silu_parts(g):
    sg = jax.nn.sigmoid(g)
    return g * sg, sg * (1.0 + g * (1.0 - sg))


def _rope_consts():
    inv = (ROPE_THETA ** (-np.arange(0, QK_ROPE, 2, dtype=np.float32) / QK_ROPE)).astype(np.float32)
    t = np.zeros((8, HEAD_PAD), np.float32)
    t[0, ROPE_LO:ROPE_LO + ROPE_HALF] = inv
    t[0, ROPE_LO + ROPE_HALF:ROPE_LO + QK_ROPE] = inv
    t[1, ROPE_LO:ROPE_LO + ROPE_HALF] = -1.0
    t[2, ROPE_LO + ROPE_HALF:ROPE_LO + QK_ROPE] = 1.0
    return jnp.asarray(t)


def _rope_angles(pos, rc):
    ang = pos * rc[0:1, :]
    return jnp.cos(ang), jnp.sin(ang)


def _rope_tables(cosv, sinv, rc):
    return cosv, sinv * rc[1:2, :], sinv * rc[2:3, :]


def _rope(x, ct, s1, s2):
    up = pltpu.roll(x, HEAD_PAD - ROPE_HALF, 1)
    dn = pltpu.roll(x, ROPE_HALF, 1)
    return x * ct + up * s1 + dn * s2


def _const_spec(shape):
    nd = len(shape)
    return pl.BlockSpec(shape, lambda *_: (0,) * nd, pipeline_mode=pl.Buffered(1))


def _acc_spec(shape):
    nd = len(shape)
    return pl.BlockSpec(shape, lambda *_: (0,) * nd)


def _params(n_axes):
    return pltpu.CompilerParams(dimension_semantics=("arbitrary",) * n_axes, vmem_limit_bytes=VMEM_LIMIT)


def _place():
    x, y, c = lax.axis_index("x"), lax.axis_index("y"), lax.axis_index("c")
    return x, y, c, 4 * x + 2 * y + c


def _peer(x, y, c, mask):
    px = 1 - x if mask & 4 else x
    py = 1 - y if mask & 2 else y
    pc = 1 - c if mask & 1 else c
    return (px, py, pc), 4 * px + 2 * py + pc


ANY_SPEC = pl.BlockSpec(memory_space=pl.ANY)


def _comm_sems(n):
    return [pltpu.SemaphoreType.DMA((n, N_DEV - 1)), pltpu.SemaphoreType.DMA((n, N_DEV - 1)), pltpu.SemaphoreType.DMA((n,))]


def _gather_copies(ins, outs, sems):
    send_sems, recv_sems, local_sems = sems
    x, y, c, me = _place()
    starts, waits = [], []
    for w in range(len(ins)):
        mine = pltpu.make_async_copy(ins[w], outs[w].at[me], local_sems.at[w])
        starts.append(mine)
        waits.append(mine)
        for mask in range(1, N_DEV):
            peer, peer_idx = _peer(x, y, c, mask)
            starts.append(pltpu.make_async_remote_copy(
                src_ref=ins[w], dst_ref=outs[w].at[me], send_sem=send_sems.at[w, mask - 1],
                recv_sem=recv_sems.at[w, mask - 1], device_id=peer, device_id_type=MESH))
            waits.append(pltpu.make_async_remote_copy(
                src_ref=ins[w], dst_ref=outs[w].at[peer_idx], send_sem=send_sems.at[w, mask - 1],
                recv_sem=recv_sems.at[w, mask - 1], device_id=peer, device_id_type=MESH))
    return starts, waits


def _scatter_copies(ins, outs, sems):
    send_sems, recv_sems, local_sems = sems
    x, y, c, me = _place()
    copies = []
    for w in range(len(ins)):
        copies.append(pltpu.make_async_copy(ins[w].at[me], outs[w].at[0], local_sems.at[w]))
        for mask in range(1, N_DEV):
            peer, peer_idx = _peer(x, y, c, mask)
            copies.append(pltpu.make_async_remote_copy(
                src_ref=ins[w].at[peer_idx], dst_ref=outs[w].at[mask], send_sem=send_sems.at[w, mask - 1],
                recv_sem=recv_sems.at[w, mask - 1], device_id=peer, device_id_type=MESH))
    return copies, copies


def _stacked(arrays):
    return [jax.ShapeDtypeStruct((N_DEV,) + a.shape, a.dtype) for a in arrays]


def _all_gather(shards):
    n = len(shards)

    def body(*refs):
        ins, outs = refs[:n], refs[n:2 * n]
        send_sems, recv_sems, local_sems = refs[2 * n:]
        x, y, c, me = _place()
        sibling = (x, y, 1 - c)
        chips = [(1 - x, y), (x, 1 - y), (1 - x, 1 - y)]

        def copy(w, k, block, to, src=None):
            idx = 4 * block[0] + 2 * block[1] + block[2]
            return pltpu.make_async_remote_copy(
                src_ref=outs[w].at[idx] if src is None else src, dst_ref=outs[w].at[idx],
                send_sem=send_sems.at[w, k], recv_sem=recv_sems.at[w, k], device_id=to, device_id_type=MESH)

        local, sent = [], []
        for w in range(n):
            mine = pltpu.make_async_copy(ins[w], outs[w].at[me], local_sems.at[w])
            mine.start()
            local.append(mine)
            first = [copy(w, 0, (x, y, c), sibling, src=ins[w])]
            first += [copy(w, 1 + j, (x, y, c), (*chip, c), src=ins[w]) for j, chip in enumerate(chips)]
            for cp in first:
                cp.start()
            sent += first
        for w in range(n):
            for j, chip in enumerate(chips):
                copy(w, 1 + j, (*chip, c), (x, y, c)).wait_recv()
                onward = copy(w, 4 + j, (*chip, c), sibling)
                onward.start()
                sent.append(onward)
        for w in range(n):
            copy(w, 0, sibling, (x, y, c)).wait_recv()
            for j, chip in enumerate(chips):
                copy(w, 4 + j, (*chip, 1 - c), (x, y, c)).wait_recv()
        for cp in sent:
            cp.wait_send()
        for cp in local:
            cp.wait()

    return pl.pallas_call(
        body, name="weight_all_gather", out_shape=_stacked(shards),
        in_specs=[ANY_SPEC] * n, out_specs=[ANY_SPEC] * n, scratch_shapes=_comm_sems(n),
    )(*shards)


def _conv_fwd(x, a_norm, w_in, conv_w, w_out, n_seq, seq, later_shards):
    tm = CONV_FWD_TILE
    nt = seq // tm
    n_col = w_in.shape[2]
    n_later = len(later_shards)

    def body(x_ref, an_ref, win_ref, cw_ref, wout_ref, *rest):
        shard_refs, rest = rest[:n_later], rest[n_later:]
        x1_ref, proj_ref, conv_ref = rest[:3]
        stack_refs, rest = rest[3:3 + n_later], rest[3 + n_later:]
        prev_ref, wide_ref, sems = rest[0], rest[1], rest[2:]
        step = pl.program_id(0) * nt + pl.program_id(1)

        @pl.when(step == 0)
        def _():
            for cp in _gather_copies(shard_refs, stack_refs, sems)[0]:
                cp.start()

        @pl.when(pl.program_id(1) == 0)
        def _():
            prev_ref[...] = jnp.zeros_like(prev_ref)

        xv = x_ref[...]
        h = (xv * _rstd(xv) * an_ref[...]).astype(BF16)
        for d in range(N_DEV):
            part = jnp.dot(h, win_ref[d], preferred_element_type=F32)
            wide_ref[:, d * n_col:(d + 1) * n_col] = part
            proj_ref[:, d * n_col:(d + 1) * n_col] = part.astype(SAVED)
        b = wide_ref[:, 0:D_MODEL]
        v = wide_ref[:, D_MODEL:2 * D_MODEL] * wide_ref[:, 2 * D_MODEL:3 * D_MODEL]
        g = wide_ref[:, 3 * D_MODEL:4 * D_MODEL]
        w0, w1, w2 = cw_ref[0:1, :], cw_ref[1:2, :], cw_ref[2:3, :]
        conv_ref[...] = w0 * pltpu.roll(v, 2, 0) + w1 * pltpu.roll(v, 1, 0) + w2 * v
        rows = lax.broadcasted_iota(jnp.int32, (8, D_MODEL), 0)
        p8, v8 = prev_ref[...], v[0:8]
        back1 = jnp.where(rows < 1, pltpu.roll(p8, 1, 0), pltpu.roll(v8, 1, 0))
        back2 = jnp.where(rows < 2, pltpu.roll(p8, 2, 0), pltpu.roll(v8, 2, 0))
        conv_ref[0:8, :] = w0 * back2 + w1 * back1 + w2 * v8
        prev_ref[...] = v[tm - 8:tm]
        silu, _ = _silu_parts(g)
        yv = silu * b * conv_ref[...]
        x1_ref[...] = xv + _dot(yv, wout_ref[...])

        @pl.when(step == n_seq * nt - 1)
        def _():
            for cp in _gather_copies(shard_refs, stack_refs, sems)[1]:
                cp.wait()

    tok = lambda width: pl.BlockSpec((tm, width), lambda s, i: (s * nt + i, 0))
    t = n_seq * seq
    outs = pl.pallas_call(
        body, name="conv_mixer_fwd", grid=(n_seq, nt),
        in_specs=[tok(D_MODEL), _const_spec((1, D_MODEL)), _const_spec(w_in.shape), _const_spec((3, D_MODEL)),
                  _const_spec(w_out.shape)] + [ANY_SPEC] * n_later,
        out_specs=[tok(D_MODEL), tok(4 * D_MODEL), tok(D_MODEL)] + [ANY_SPEC] * n_later,
        out_shape=[jax.ShapeDtypeStruct((t, D_MODEL), F32), jax.ShapeDtypeStruct((t, 4 * D_MODEL), SAVED),
                   jax.ShapeDtypeStruct((t, D_MODEL), F32)] + _stacked(later_shards),
        scratch_shapes=[pltpu.VMEM((8, D_MODEL), F32), pltpu.VMEM((tm, 4 * D_MODEL), F32)] + _comm_sems(n_later),
        compiler_params=_params(2),
    )(x, a_norm, w_in, conv_w, w_out, *later_shards)
    return outs[0], outs[1], outs[2], outs[3:]


def _conv_bwd(dx1, x, proj, conv, a_norm, w_in, conv_w, w_out, n_seq, seq, ready_stacks):
    tm = CONV_BWD_TILE
    nt = seq // tm
    n_col = w_in.shape[2]
    n_ready = len(ready_stacks)

    def body(dx1_ref, x_ref, proj_ref, conv_ref, an_ref, win_ref, cw_ref, wout_ref, *rest):
        ready_refs, rest = rest[:n_ready], rest[n_ready:]
        dx_ref, h_ref, dproj_ref, dwout_ref, dan_ref, dcw_ref = rest[:6]
        part_refs, rest = rest[6:6 + n_ready], rest[6 + n_ready:]
        next_ref, d1_ref, d2_ref = rest[:3]
        sems = rest[3:]
        step = pl.program_id(0) * nt + pl.program_id(1)
        first = step == 0

        @pl.when(first)
        def _():
            for cp in _scatter_copies(ready_refs, part_refs, sems)[0]:
                cp.start()
            dwout_ref[...] = jnp.zeros_like(dwout_ref)
            dan_ref[...] = jnp.zeros_like(dan_ref)
            dcw_ref[...] = jnp.zeros_like(dcw_ref)

        @pl.when(pl.program_id(1) == 0)
        def _():
            next_ref[...] = jnp.zeros_like(next_ref)

        dx1v = dx1_ref[...]
        dy = _dot_nt(dx1v, wout_ref[...])
        b = proj_ref[:, 0:D_MODEL].astype(F32)
        cc = proj_ref[:, D_MODEL:2 * D_MODEL].astype(F32)
        u = proj_ref[:, 2 * D_MODEL:3 * D_MODEL].astype(F32)
        g = proj_ref[:, 3 * D_MODEL:4 * D_MODEL].astype(F32)
        cv = conv_ref[...]
        silu, dsilu = _silu_parts(g)
        per_part = D_MODEL // n_col

        def back_through_w_in(part, grad):
            grad = grad.astype(BF16)
            dproj_ref[:, part * D_MODEL:(part + 1) * D_MODEL] = grad
            term = None
            for j in range(per_part):
                d = part * per_part + j
                piece = lax.dot_general(grad[:, j * n_col:(j + 1) * n_col], win_ref[d], NT, preferred_element_type=F32)
                term = piece if term is None else term + piece
            return term

        dh = back_through_w_in(0, dy * silu * cv)
        dh += back_through_w_in(3, dy * b * cv * dsilu)
        dconv = dy * silu * b
        d1_ref[...] = pltpu.roll(dconv, tm - 1, 0)
        d2_ref[...] = pltpu.roll(dconv, tm - 2, 0)
        rows = lax.broadcasted_iota(jnp.int32, (8, D_MODEL), 0)
        n8, c8 = next_ref[...], dconv[tm - 8:tm]
        d1_ref[tm - 8:tm, :] = jnp.where(rows >= 7, pltpu.roll(n8, 7, 0), pltpu.roll(c8, 7, 0))
        d2_ref[tm - 8:tm, :] = jnp.where(rows >= 6, pltpu.roll(n8, 6, 0), pltpu.roll(c8, 6, 0))
        next_ref[...] = dconv[0:8]
        d1, d2 = d1_ref[...], d2_ref[...]
        v = cc * u
        dcw_ref[0:1, :] += jnp.sum(d2 * v, axis=0, keepdims=True)
        dcw_ref[1:2, :] += jnp.sum(d1 * v, axis=0, keepdims=True)
        dcw_ref[2:3, :] += jnp.sum(dconv * v, axis=0, keepdims=True)
        dv = cw_ref[0:1, :] * d2 + cw_ref[1:2, :] * d1 + cw_ref[2:3, :] * dconv
        dh += back_through_w_in(1, dv * u)
        dh += back_through_w_in(2, dv * cc)
        dwout_ref[...] += _dot_tn(silu * b * cv, dx1v)
        xv = x_ref[...]
        r = _rstd(xv)
        xh = xv * r
        h_ref[...] = (xh * an_ref[...]).T.astype(BF16)
        dan_ref[...] += jnp.sum(dh * xh, axis=0, keepdims=True)
        dx_ref[...] = dx1v + _norm_bwd(dh * an_ref[...], xh, r)

        @pl.when(step == n_seq * nt - 1)
        def _():
            for cp in _scatter_copies(ready_refs, part_refs, sems)[1]:
                cp.wait()

    tok = lambda width: pl.BlockSpec((tm, width), lambda s, i: (s * nt + nt - 1 - i, 0))
    t = n_seq * seq
    outs = pl.pallas_call(
        body, name="conv_mixer_bwd", grid=(n_seq, nt),
        in_specs=[tok(D_MODEL), tok(D_MODEL), tok(4 * D_MODEL), tok(D_MODEL), _const_spec((1, D_MODEL)),
                  _const_spec(w_in.shape), _const_spec((3, D_MODEL)), _const_spec(w_out.shape)] + [ANY_SPEC] * n_ready,
        out_specs=[tok(D_MODEL), pl.BlockSpec((D_MODEL, tm), lambda s, i: (0, s * nt + nt - 1 - i)),
                   tok(4 * D_MODEL), _acc_spec((D_MODEL, D_MODEL)),
                   _acc_spec((1, D_MODEL)), _acc_spec((8, D_MODEL))] + [ANY_SPEC] * n_ready,
        out_shape=[jax.ShapeDtypeStruct((t, D_MODEL), F32), jax.ShapeDtypeStruct((D_MODEL, t), BF16),
                   jax.ShapeDtypeStruct((t, 4 * D_MODEL), BF16), jax.ShapeDtypeStruct((D_MODEL, D_MODEL), F32),
                   jax.ShapeDtypeStruct((1, D_MODEL), F32), jax.ShapeDtypeStruct((8, D_MODEL), F32)]
        + [jax.ShapeDtypeStruct(a.shape, a.dtype) for a in ready_stacks],
        scratch_shapes=[pltpu.VMEM((8, D_MODEL), F32), pltpu.VMEM((tm, D_MODEL), F32), pltpu.VMEM((tm, D_MODEL), F32)]
        + _comm_sems(n_ready),
        compiler_params=_params(2),
    )(dx1, x, proj, conv, a_norm, w_in, conv_w, w_out, *ready_stacks)
    return outs[:6], outs[6:]


def _w_in_grad_exchange(a_t, b, n_col, chip, ready_stacks, small):
    r, t = a_t.shape
    bt = 4096
    nk = t // bt
    n_ready = len(ready_stacks)
    n_chip = N_DEV // 2
    n_remote = N_DEV - 2

    def body(chip_ref, a_ref, b_ref, *rest):
        ready_refs, small_ref, rest = rest[:n_ready], rest[n_ready], rest[n_ready + 1:]
        own_ref, got_ref, rest = rest[0], rest[1], rest[2:]
        part_refs, small_all_ref, rest = rest[:n_ready], rest[n_ready], rest[n_ready + 1:]
        acc_ref, land_ref, stage_ref, pair_send, pair_recv, chip_send, chip_recv = rest[:7]
        scatter_sems, gather_sems = rest[7:10], rest[10:13]
        s, k = pl.program_id(0), pl.program_id(1)
        x, y, c, _ = _place()
        chip = 2 * x + y

        @pl.when(jnp.logical_and(s == 0, k == 0))
        def _():
            for cp in _scatter_copies(ready_refs, part_refs, scatter_sems)[0]:
                cp.start()
            for cp in _gather_copies([small_ref], [small_all_ref], gather_sems)[0]:
                cp.start()

        for parity in range(2):
            @pl.when(s % 2 == parity)
            def _(parity=parity):
                @pl.when(k == 0)
                def _():
                    acc_ref[parity] = jnp.zeros((r, n_col), F32)

                acc_ref[parity] += jnp.dot(a_ref[...], b_ref[...], preferred_element_type=F32)

        def to_sibling(step):
            return pltpu.make_async_remote_copy(
                src_ref=acc_ref.at[step % 2], dst_ref=land_ref, send_sem=pair_send.at[step], recv_sem=pair_recv.at[step],
                device_id=(x, y, 1 - c), device_id_type=MESH)

        def to_owner(nth):
            owner_chip = (chip + 1 + nth) % n_chip
            slot = jnp.bitwise_xor(chip, owner_chip) - 1
            return pltpu.make_async_remote_copy(
                src_ref=stage_ref.at[nth % 2], dst_ref=got_ref.at[slot], send_sem=chip_send.at[nth],
                recv_sem=chip_recv.at[slot], device_id=(owner_chip // 2, owner_chip % 2, c), device_id_type=MESH)

        for step in range(N_DEV):
            owner_core = step % 2

            @pl.when(jnp.logical_and(s == step, k == nk - 1))
            def _(step=step, owner_core=owner_core):
                @pl.when(c != owner_core)
                def _():
                    to_sibling(step).start()

                if step >= 1:
                    @pl.when(c == owner_core)
                    def _():
                        to_sibling(step - 1).wait_send()

            if step < N_DEV - 1:
                pickup = jnp.logical_and(s == step + 1, k == min(PICKUP_BLOCK, nk - 1))
            else:
                pickup = jnp.logical_and(s == step, k == nk - 1)

            @pl.when(jnp.logical_and(pickup, c == owner_core))
            def _(step=step):
                to_sibling(step).wait_recv()
                total = acc_ref[step % 2] + land_ref[...]
                if step < n_remote:
                    nth = step // 2
                    if nth >= 2:
                        to_owner(nth - 2).wait_send()
                    stage_ref[nth % 2] = total.astype(BF16)
                    to_owner(nth).start()
                else:
                    own_ref[...] = total

        @pl.when(jnp.logical_and(s == N_DEV - 1, k == nk - 1))
        def _():
            @pl.when(c != (N_DEV - 1) % 2)
            def _():
                to_sibling(N_DEV - 1).wait_send()

            to_owner(1).wait_send()
            to_owner(2).wait_send()
            for slot in range(n_chip - 1):
                pltpu.make_async_remote_copy(
                    src_ref=stage_ref.at[0], dst_ref=got_ref.at[slot], send_sem=chip_send.at[0],
                    recv_sem=chip_recv.at[slot], device_id=(x, y, c), device_id_type=MESH).wait_recv()
            for cp in _scatter_copies(ready_refs, part_refs, scatter_sems)[1]:
                cp.wait()
            for cp in _gather_copies([small_ref], [small_all_ref], gather_sems)[1]:
                cp.wait()

    def owner_block(s, k, chip):
        return (k, 2 * ((chip[0] + 1 + s // 2) % n_chip) + s % 2)

    grid_spec = pltpu.PrefetchScalarGridSpec(
        num_scalar_prefetch=1, grid=(N_DEV, nk),
        in_specs=[pl.BlockSpec((r, bt), lambda s, k, chip: (0, k)), pl.BlockSpec((bt, n_col), owner_block)]
        + [ANY_SPEC] * (n_ready + 1),
        out_specs=[pl.BlockSpec((r, n_col), lambda s, k, chip: (0, 0)), ANY_SPEC] + [ANY_SPEC] * (n_ready + 1),
        scratch_shapes=[pltpu.VMEM((2, r, n_col), F32), pltpu.VMEM((r, n_col), F32), pltpu.VMEM((2, r, n_col), BF16),
                        pltpu.SemaphoreType.DMA((N_DEV,)), pltpu.SemaphoreType.DMA((N_DEV,)),
                        pltpu.SemaphoreType.DMA((n_chip - 1,)), pltpu.SemaphoreType.DMA((n_chip - 1,))]
        + _comm_sems(n_ready) + _comm_sems(1))
    outs = pl.pallas_call(
        body, name="w_in_grad_exchange", grid_spec=grid_spec,
        out_shape=[jax.ShapeDtypeStruct((r, n_col), F32), jax.ShapeDtypeStruct((n_chip - 1, r, n_col), BF16)]
        + [jax.ShapeDtypeStruct(p.shape, p.dtype) for p in ready_stacks] + _stacked([small]),
        compiler_params=_params(2),
    )(chip, a_t, b, *ready_stacks, small)
    return outs[0], outs[1], outs[2:2 + n_ready], outs[2 + n_ready]


def _proj_fwd(x1, pos, rc, kv_norm, w_dkv, ckv_norm, w_ukv, b_norm, b_w_in, q_norm, w_uq):
    t = x1.shape[0]
    tm = LATENT_FWD_TILE

    def body(x1_ref, pos_ref, rc_ref, kvn_ref, wdkv_ref, ckvn_ref, wukv_ref, bn_ref, bwin_ref, qn_ref, wuq_ref,
             ckr_ref, cq_ref, gb_ref, q_ref, k_ref, v_ref, cos_ref, sin_ref):
        xv = x1_ref[...]
        xh = xv * _rstd(xv)
        ckr = _dot(xh * kvn_ref[...], wdkv_ref[...])
        pb = _dot(xh * bn_ref[...], bwin_ref[...])
        ckr_ref[...] = ckr
        ckv = ckr[:, 0:KV_RANK]
        kv = _dot(ckv * _rstd(ckv) * ckvn_ref[...], wukv_ref[...])
        cosv, sinv = _rope_angles(pos_ref[...], rc_ref[...])
        cos_ref[...] = cosv
        sin_ref[...] = sinv
        ct, s1, s2 = _rope_tables(cosv, sinv, rc_ref[...])
        k_rope = _rope(ckr[:, KV_RANK:CKR_PAD], ct, s1, s2)
        lane = lax.broadcasted_iota(jnp.int32, (tm, HEAD_PAD), 1)
        low = lane < QK_NOPE
        for h in range(N_HEADS):
            kv_h = kv[:, h * HEAD_PAD:(h + 1) * HEAD_PAD]
            k_ref[:, h * HEAD_PAD:(h + 1) * HEAD_PAD] = jnp.where(low, kv_h, k_rope).astype(BF16)
            v_ref[:, h * HEAD_PAD:(h + 1) * HEAD_PAD] = jnp.where(low, pltpu.roll(kv_h, V_HEAD, 1), 1.0).astype(BF16)
        cq = pb[:, 0:Q_RANK]
        cq_ref[...] = cq
        gb_ref[...] = pb[:, Q_RANK:Q_RANK + E_B]
        q = _dot(cq * _rstd(cq) * qn_ref[...], wuq_ref[...])
        for h in range(N_HEADS):
            q_ref[:, h * HEAD_PAD:(h + 1) * HEAD_PAD] = _rope(
                q[:, h * HEAD_PAD:(h + 1) * HEAD_PAD], ct, s1, s2).astype(BF16)

    tok = lambda width: pl.BlockSpec((tm, width), lambda i: (i, 0))
    weights = [kv_norm, w_dkv, ckv_norm, w_ukv, b_norm, b_w_in, q_norm, w_uq]
    wide = N_HEADS * HEAD_PAD
    return pl.pallas_call(
        body, name="latent_proj_fwd", grid=(t // tm,),
        in_specs=[tok(D_MODEL), tok(1), _const_spec(rc.shape)] + [_const_spec(w.shape) for w in weights],
        out_specs=[tok(CKR_PAD), tok(Q_RANK), tok(E_B), tok(wide), tok(wide), tok(wide), tok(HEAD_PAD), tok(HEAD_PAD)],
        out_shape=[jax.ShapeDtypeStruct((t, CKR_PAD), F32), jax.ShapeDtypeStruct((t, Q_RANK), F32),
                   jax.ShapeDtypeStruct((t, E_B), F32), jax.ShapeDtypeStruct((t, wide), BF16),
                   jax.ShapeDtypeStruct((t, wide), BF16), jax.ShapeDtypeStruct((t, wide), BF16),
                   jax.ShapeDtypeStruct((t, HEAD_PAD), F32), jax.ShapeDtypeStruct((t, HEAD_PAD), F32)],
        compiler_params=_params(1),
    )(x1, pos, rc, *weights)


def _proj_bwd(dq, dk, dv, dgb, cq, ckr, x1, dx2, cosv, sinv, rc, kv_norm, w_dkv, ckv_norm, w_ukv, b_norm, b_w_in, q_norm, w_uq):
    t = x1.shape[0]
    tm = LATENT_BWD_TILE
    wide = N_HEADS * HEAD_PAD

    def body(dq_ref, dk_ref, dv_ref, dgb_ref, cq_ref, ckr_ref, x1_ref, dx2_ref, cos_ref, sin_ref, rc_ref,
             kvn_ref, wdkv_ref, ckvn_ref, wukv_ref, bn_ref, bwin_ref, qn_ref, wuq_ref,
             dx1_ref, dwuq_ref, dwbin_ref, dwukv_ref, dwdkv_ref, dqn_ref, dbn_ref, dckvn_ref, dkvn_ref, dqu_ref, dkv_ref):
        @pl.when(pl.program_id(0) == 0)
        def _():
            for ref in (dwuq_ref, dwbin_ref, dwukv_ref, dwdkv_ref, dqn_ref, dbn_ref, dckvn_ref, dkvn_ref):
                ref[...] = jnp.zeros_like(ref)

        ct, s1, s2 = _rope_tables(cos_ref[...], sin_ref[...], rc_ref[...])
        lane = lax.broadcasted_iota(jnp.int32, (tm, HEAD_PAD), 1)
        low = lane < QK_NOPE
        xv = x1_ref[...]
        r = _rstd(xv)
        xh = xv * r
        dk_rope = jnp.zeros((tm, HEAD_PAD), F32)
        for h in range(N_HEADS):
            dk_h = dk_ref[:, h * HEAD_PAD:(h + 1) * HEAD_PAD].astype(F32)
            dv_h = dv_ref[:, h * HEAD_PAD:(h + 1) * HEAD_PAD].astype(F32)
            dkv_ref[:, h * HEAD_PAD:(h + 1) * HEAD_PAD] = jnp.where(low, dk_h, pltpu.roll(dv_h, V_HEAD, 1)).astype(BF16)
            dk_rope += dk_h
        rope_lanes = jnp.logical_and(lane >= ROPE_LO, lane < ROPE_LO + QK_ROPE)
        dk_rope = jnp.where(rope_lanes, _rope(dk_rope, ct, -s1, -s2), 0.0)
        ckv = ckr_ref[:, 0:KV_RANK]
        rk = _rstd(ckv)
        ckh = ckv * rk
        ckvn = (ckh * ckvn_ref[...]).astype(BF16)
        dwukv = lax.dot_general(ckvn, dkv_ref[...], TN, preferred_element_type=F32)
        for h in range(N_HEADS):
            dwukv_ref[h] += dwukv[:, h * HEAD_PAD:(h + 1) * HEAD_PAD]
        dckvn = lax.dot_general(dkv_ref[...], wukv_ref[...], NT, preferred_element_type=F32)
        dckvn_ref[...] += jnp.sum(dckvn * ckh, axis=0, keepdims=True)
        dckv = _norm_bwd(dckvn * ckvn_ref[...], ckh, rk)
        dckr = jnp.concatenate([dckv, dk_rope], axis=1).astype(BF16)
        dwdkv_ref[...] += _dot_tn(xh * kvn_ref[...], dckr)
        dh2 = lax.dot_general(dckr, wdkv_ref[...], NT, preferred_element_type=F32)
        for h in range(N_HEADS):
            dqu_ref[:, h * HEAD_PAD:(h + 1) * HEAD_PAD] = _rope(
                dq_ref[:, h * HEAD_PAD:(h + 1) * HEAD_PAD].astype(F32), ct, -s1, -s2).astype(BF16)
        cq = cq_ref[...]
        rq = _rstd(cq)
        cqh = cq * rq
        cqn = (cqh * qn_ref[...]).astype(BF16)
        dwuq = lax.dot_general(cqn, dqu_ref[...], TN, preferred_element_type=F32)
        for h in range(N_HEADS):
            dwuq_ref[h] += dwuq[:, h * HEAD_PAD:(h + 1) * HEAD_PAD]
        dcqn = lax.dot_general(dqu_ref[...], wuq_ref[...], NT, preferred_element_type=F32)
        dqn_ref[...] += jnp.sum(dcqn * cqh, axis=0, keepdims=True)
        dcq = _norm_bwd(dcqn * qn_ref[...], cqh, rq)
        dpb = jnp.concatenate([dcq.astype(BF16), dgb_ref[...]], axis=1)
        dwbin_ref[...] += _dot_tn(xh * bn_ref[...], dpb)
        dh3 = lax.dot_general(dpb, bwin_ref[...], NT, preferred_element_type=F32)
        dkvn_ref[...] += jnp.sum(dh2 * xh, axis=0, keepdims=True)
        dbn_ref[...] += jnp.sum(dh3 * xh, axis=0, keepdims=True)
        dx1_ref[...] = dx2_ref[...] + _norm_bwd(dh2 * kvn_ref[...] + dh3 * bn_ref[...], xh, r)

    tok = lambda width: pl.BlockSpec((tm, width), lambda i: (i, 0))
    weights = [kv_norm, w_dkv, ckv_norm, w_ukv, b_norm, b_w_in, q_norm, w_uq]
    acc_shapes = [(N_HEADS, Q_RANK, HEAD_PAD), (D_MODEL, Q_RANK + E_B), (N_HEADS, KV_RANK, HEAD_PAD), (D_MODEL, CKR_PAD),
                  (1, Q_RANK), (1, D_MODEL), (1, KV_RANK), (1, D_MODEL)]
    return pl.pallas_call(
        body, name="latent_proj_bwd", grid=(t // tm,),
        in_specs=[tok(wide), tok(wide), tok(wide), tok(E_B), tok(Q_RANK), tok(CKR_PAD), tok(D_MODEL), tok(D_MODEL),
                  tok(HEAD_PAD), tok(HEAD_PAD), _const_spec(rc.shape)] + [_const_spec(w.shape) for w in weights],
        out_specs=[tok(D_MODEL)] + [_acc_spec(s) for s in acc_shapes],
        out_shape=[jax.ShapeDtypeStruct((t, D_MODEL), F32)] + [jax.ShapeDtypeStruct(s, F32) for s in acc_shapes],
        scratch_shapes=[pltpu.VMEM((tm, wide), BF16), pltpu.VMEM((tm, wide), BF16)],
        compiler_params=_params(1),
    )(dq, dk, dv, dgb, cq, ckr, x1, dx2, cosv, sinv, rc, *weights)


def _attn_fwd(q, k, v, n_seq, seq):
    tb, group = ATTN_TILE, ATTN_GROUP
    rows = tb * group
    nq = seq // rows
    pair = 2 * HEAD_PAD

    def body(q_ref, k_ref, v_ref, o_ref, lr_ref):
        i = pl.program_id(2)
        row = lax.broadcasted_iota(jnp.int32, (tb, tb), 0)
        col = lax.broadcasted_iota(jnp.int32, (tb, tb), 1)
        causal = col <= row

        def step(j, carry, kinds):
            start = pl.multiple_of(j * tb, tb)
            out = []
            for g in range(group):
                for hh in range(2):
                    m, acc = carry[2 * g + hh]
                    if kinds[g] != "none":
                        heads = slice(hh * HEAD_PAD, (hh + 1) * HEAD_PAD)
                        s = lax.dot_general(q_ref[g * tb:(g + 1) * tb, heads], k_ref[pl.ds(start, tb), heads], NT,
                                            preferred_element_type=F32) * (SOFTMAX_SCALE * LOG2_E)
                        if kinds[g] == "diagonal":
                            s = jnp.where(causal, s, -jnp.inf)
                        m_new = jnp.maximum(m, jnp.max(s, axis=-1, keepdims=True))
                        p = jnp.exp2(s - m_new)
                        acc = jnp.exp2(m - m_new) * acc + jnp.dot(
                            p.astype(BF16), v_ref[pl.ds(start, tb), heads], preferred_element_type=F32)
                        m = m_new
                    out.append((m, acc))
            return tuple(out)

        one = (jnp.full((tb, 1), -jnp.inf, F32), jnp.zeros((tb, HEAD_PAD), F32))
        carry = lax.fori_loop(0, i * group, functools.partial(step, kinds=("full",) * group), (one,) * (2 * group))
        for d in range(group):
            kinds = tuple("none" if g < d else ("diagonal" if g == d else "full") for g in range(group))
            carry = step(i * group + d, carry, kinds)
        lane = lax.broadcasted_iota(jnp.int32, (tb, HEAD_PAD), 1)
        low = lane < V_HEAD
        for g in range(group):
            halves = []
            for hh in range(2):
                m, acc = carry[2 * g + hh]
                swapped = pltpu.roll(acc, V_HEAD, 1)
                halves.append(acc / swapped)
                lse = (m * LN_2) + jnp.log(jnp.where(low, swapped, acc))
                lr_ref[hh, :, g * tb:(g + 1) * tb] = lse.T[0:8, :]
            o_ref[g * tb:(g + 1) * tb, :] = jnp.where(low, halves[0], pltpu.roll(halves[1], V_HEAD, 1))

    t = n_seq * seq
    return pl.pallas_call(
        body, name="attention_fwd", grid=(n_seq, N_HEADS // 2, nq),
        in_specs=[pl.BlockSpec((rows, pair), lambda s, p, i: (s * nq + i, p)),
                  pl.BlockSpec((seq, pair), lambda s, p, i: (s, p)),
                  pl.BlockSpec((seq, pair), lambda s, p, i: (s, p))],
        out_specs=[pl.BlockSpec((rows, HEAD_PAD), lambda s, p, i: (s * nq + i, p)),
                   pl.BlockSpec((None, 2, 8, rows), lambda s, p, i: (s, p, 0, i))],
        out_shape=[jax.ShapeDtypeStruct((t, E_B), F32), jax.ShapeDtypeStruct((n_seq, N_HEADS, 8, seq), F32)],
        compiler_params=_params(3),
    )(q, k, v)


def _attn_bwd(q, k, v, do, lse_row, delta_row, n_seq, seq):
    tb, group = ATTN_TILE, ATTN_BWD_GROUP
    rows = tb * group
    nk = seq // rows
    n_inner = seq // tb
    pair = 2 * HEAD_PAD

    def body(q_ref, k_ref, v_ref, do_ref, lr_ref, dr_ref, dk_ref, dv_ref, dq_out_ref, dq_ref):
        j = pl.program_id(2)

        @pl.when(j == 0)
        def _():
            dq_ref[...] = jnp.zeros_like(dq_ref)

        row = lax.broadcasted_iota(jnp.int32, (tb, tb), 0)
        col = lax.broadcasted_iota(jnp.int32, (tb, tb), 1)
        causal = col >= row

        def step(i, carry, kinds):
            start = pl.multiple_of(i * tb, tb)
            out = []
            for hh in range(2):
                heads = slice(hh * HEAD_PAD, (hh + 1) * HEAD_PAD)
                qi = q_ref[pl.ds(start, tb), heads]
                doi = do_ref[pl.ds(start, tb), heads]
                dq = None
                for g in range(group):
                    dk_acc, dv_acc = carry[hh * group + g]
                    if kinds[g] != "none":
                        kg = k_ref[g * tb:(g + 1) * tb, heads]
                        st = lax.dot_general(kg, qi, NT, preferred_element_type=F32) * SOFTMAX_SCALE
                        pt = jnp.exp(st - lr_ref[hh, 0:1, pl.ds(start, tb)])
                        if kinds[g] == "diagonal":
                            pt = jnp.where(causal, pt, 0.0)
                        dv_acc = dv_acc + jnp.dot(pt.astype(BF16), doi, preferred_element_type=F32)
                        dpt = lax.dot_general(v_ref[g * tb:(g + 1) * tb, heads], doi, NT, preferred_element_type=F32)
                        dst = (pt * (dpt - dr_ref[hh, 0:1, pl.ds(start, tb)]) * SOFTMAX_SCALE).astype(BF16)
                        dk_acc = dk_acc + jnp.dot(dst, qi, preferred_element_type=F32)
                        term = lax.dot_general(dst, kg, TN, preferred_element_type=F32)
                        dq = term if dq is None else dq + term
                    out.append((dk_acc, dv_acc))
                dq_ref[pl.ds(start, tb), heads] += dq
            return tuple(out)

        one = (jnp.zeros((tb, HEAD_PAD), F32), jnp.zeros((tb, HEAD_PAD), F32))
        carry = (one,) * (2 * group)
        for d in range(group):
            kinds = tuple("full" if g < d else ("diagonal" if g == d else "none") for g in range(group))
            carry = step(j * group + d, carry, kinds)
        carry = lax.fori_loop((j + 1) * group, n_inner, functools.partial(step, kinds=("full",) * group), carry)
        for hh in range(2):
            for g in range(group):
                dk_ref[g * tb:(g + 1) * tb, hh * HEAD_PAD:(hh + 1) * HEAD_PAD] = carry[hh * group + g][0].astype(BF16)
                dv_ref[g * tb:(g + 1) * tb, hh * HEAD_PAD:(hh + 1) * HEAD_PAD] = carry[hh * group + g][1].astype(BF16)

        @pl.when(j == nk - 1)
        def _():
            dq_out_ref[...] = dq_ref[...].astype(BF16)

    t = n_seq * seq
    wide = N_HEADS * HEAD_PAD
    return pl.pallas_call(
        body, name="attention_bwd", grid=(n_seq, N_HEADS // 2, nk),
        in_specs=[pl.BlockSpec((seq, pair), lambda s, p, j: (s, p)),
                  pl.BlockSpec((rows, pair), lambda s, p, j: (s * nk + j, p)),
                  pl.BlockSpec((rows, pair), lambda s, p, j: (s * nk + j, p)),
                  pl.BlockSpec((seq, pair), lambda s, p, j: (s, p)),
                  pl.BlockSpec((None, 2, 8, seq), lambda s, p, j: (s, p, 0, 0)),
                  pl.BlockSpec((None, 2, 8, seq), lambda s, p, j: (s, p, 0, 0))],
        out_specs=[pl.BlockSpec((rows, pair), lambda s, p, j: (s * nk + j, p)),
                   pl.BlockSpec((rows, pair), lambda s, p, j: (s * nk + j, p)),
                   pl.BlockSpec((seq, pair), lambda s, p, j: (s, p))],
        out_shape=[jax.ShapeDtypeStruct((t, wide), BF16), jax.ShapeDtypeStruct((t, wide), BF16),
                   jax.ShapeDtypeStruct((t, wide), BF16)],
        scratch_shapes=[pltpu.VMEM((seq, pair), F32)],
        compiler_params=_params(3),
    )(q, k, v, do, lse_row, delta_row)


def _head_tail(o, gb, x1, w_out, final_norm, target, n_seq, seq):
    tm = HEAD_TAIL_TILE
    nt = seq // tm
    n_col = D_MODEL // N_DEV

    head_of_column = np.arange(E_B) // V_HEAD
    selector = jnp.asarray((head_of_column[None, :] == np.arange(N_HEADS)[:, None]).astype(np.float32))

    def body(o_ref, gb_ref, x1_ref, wout_ref, fn_ref, tgt_ref, sel_ref,
             dx2_ref, do_ref, dgb_ref, dr_ref, dwout_ref, dfn_ref, loss_ref):
        first = jnp.logical_and(pl.program_id(0) == 0, pl.program_id(1) == 0)

        @pl.when(first)
        def _():
            dwout_ref[...] = jnp.zeros_like(dwout_ref)
            dfn_ref[...] = jnp.zeros_like(dfn_ref)
            loss_ref[...] = jnp.zeros_like(loss_ref)

        ov, g = o_ref[...], gb_ref[...]
        silu, dsilu = _silu_parts(g)
        gated = (ov * silu).astype(BF16)
        x2 = x1_ref[...] + jnp.dot(gated, wout_ref[...], preferred_element_type=F32)
        r = _rstd(x2)
        xh = x2 * r
        err = xh * fn_ref[...] - tgt_ref[...]
        loss_ref[...] += 0.5 * jnp.sum(jnp.mean(err * err, axis=-1, keepdims=True), axis=0, keepdims=True)
        dy = err / D_MODEL
        dfn_ref[...] += jnp.sum(dy * xh, axis=0, keepdims=True)
        dx2 = _norm_bwd(dy * fn_ref[...], xh, r)
        dx2_ref[...] = dx2
        dx2b = dx2.astype(BF16)
        dw = lax.dot_general(gated, dx2b, TN, preferred_element_type=F32)
        for d in range(N_DEV):
            dwout_ref[d] += dw[:, d * n_col:(d + 1) * n_col]
        dgated = lax.dot_general(dx2b, wout_ref[...], NT, preferred_element_type=F32)
        do = dgated * silu
        dgb_ref[...] = (dgated * ov * dsilu).astype(BF16)
        delta = lax.dot_general(sel_ref[...], do * ov, NT, precision=lax.Precision.HIGHEST, preferred_element_type=F32)
        lane = lax.broadcasted_iota(jnp.int32, (tm, HEAD_PAD), 1)
        low = lane < V_HEAD
        for p in range(N_HEADS // 2):
            do_pair = do[:, p * HEAD_PAD:(p + 1) * HEAD_PAD]
            for hh in range(2):
                h = 2 * p + hh
                mine = do_pair if hh == 0 else pltpu.roll(do_pair, V_HEAD, 1)
                do_ref[:, h * HEAD_PAD:(h + 1) * HEAD_PAD] = jnp.where(low, mine, 0.0).astype(BF16)
                dr_ref[h] = jnp.broadcast_to(delta[h:h + 1, :], (8, tm))

    tok = lambda width: pl.BlockSpec((tm, width), lambda s, i: (s * nt + i, 0))
    t = n_seq * seq
    return pl.pallas_call(
        body, name="head_tail", grid=(n_seq, nt),
        in_specs=[tok(E_B), tok(E_B), tok(D_MODEL), _const_spec(w_out.shape), _const_spec((1, D_MODEL)), tok(D_MODEL),
                  _const_spec((N_HEADS, E_B))],
        out_specs=[tok(D_MODEL), tok(N_HEADS * HEAD_PAD), tok(E_B),
                   pl.BlockSpec((None, N_HEADS, 8, tm), lambda s, i: (s, 0, 0, i)),
                   _acc_spec((N_DEV, E_B, n_col)), _acc_spec((1, D_MODEL)), _acc_spec((1, 1))],
        out_shape=[jax.ShapeDtypeStruct((t, D_MODEL), F32), jax.ShapeDtypeStruct((t, N_HEADS * HEAD_PAD), BF16),
                   jax.ShapeDtypeStruct((t, E_B), BF16),
                   jax.ShapeDtypeStruct((n_seq, N_HEADS, 8, seq), F32), jax.ShapeDtypeStruct((N_DEV, E_B, n_col), F32),
                   jax.ShapeDtypeStruct((1, D_MODEL), F32), jax.ShapeDtypeStruct((1, 1), F32)],
        compiler_params=_params(2),
    )(o, gb, x1, w_out, final_norm, target, selector)


def _adamw_math(w, g, m, v):
    m = ADAM_B1 * m + (1.0 - ADAM_B1) * g
    v = ADAM_B2 * v + (1.0 - ADAM_B2) * jnp.square(g)
    m_hat = m / (1.0 - ADAM_B1 ** ADAM_STEP)
    v_hat = v / (1.0 - ADAM_B2 ** ADAM_STEP)
    delta = -ADAM_LR * (m_hat / (jnp.sqrt(v_hat) + ADAM_EPS) + ADAM_WD * w)
    return delta, m, v


def _adamw_reduce(own, parts, w, m, v):
    rows, cols = w.shape
    br = 256
    n_parts = parts.shape[0]

    def body(own_ref, p_ref, w_ref, m_ref, v_ref, g_ref, d_ref, nm_ref, nv_ref):
        g = own_ref[...]
        for k in range(n_parts):
            g = g + p_ref[k].astype(F32)
        g_ref[...] = g
        d_ref[...], nm_ref[...], nv_ref[...] = _adamw_math(w_ref[...], g, m_ref[...], v_ref[...])

    blk = pl.BlockSpec((br, cols), lambda i: (i, 0))
    return pl.pallas_call(
        body, name="adamw_reduce", grid=(rows // br,),
        in_specs=[blk, pl.BlockSpec((n_parts, br, cols), lambda i: (0, i, 0)), blk, blk, blk],
        out_specs=[blk] * 4, out_shape=[jax.ShapeDtypeStruct((rows, cols), F32)] * 4,
        compiler_params=_params(1),
    )(own, parts, w, m, v)


def _adamw_reduce_many(parts_list, states):
    n = len(parts_list)

    def body(*refs):
        ins, outs = refs[:4 * n], refs[4 * n:]
        for j in range(n):
            p_ref, w_ref, m_ref, v_ref = ins[4 * j:4 * j + 4]
            g = p_ref[0].astype(F32)
            for k in range(1, p_ref.shape[0]):
                g = g + p_ref[k].astype(F32)
            outs[4 * j][...] = g
            outs[4 * j + 1][...], outs[4 * j + 2][...], outs[4 * j + 3][...] = _adamw_math(
                w_ref[...], g, m_ref[...], v_ref[...])

    flat_in = [a for p, st in zip(parts_list, states) for a in (p,) + tuple(st)]
    shapes = [jax.ShapeDtypeStruct(st[0].shape, F32) for st in states for _ in range(4)]
    flat = pl.pallas_call(body, name="adamw_reduce_many", out_shape=shapes,
                          compiler_params=pltpu.CompilerParams(vmem_limit_bytes=VMEM_LIMIT))(*flat_in)
    return [tuple(flat[4 * j:4 * j + 4]) for j in range(n)]


def _sum_parts(parts):
    def body(p_ref, o_ref):
        g = p_ref[0]
        for k in range(1, N_DEV):
            g = g + p_ref[k]
        o_ref[...] = g

    return pl.pallas_call(body, name="small_grad_sum", out_shape=jax.ShapeDtypeStruct(parts.shape[1:], F32))(parts)


def _adamw_small(gs, ws, ms, vs):
    n = len(gs)

    def body(*refs):
        ins, outs = refs[:4 * n], refs[4 * n:]
        for j in range(n):
            g_ref, w_ref, m_ref, v_ref = ins[j], ins[n + j], ins[2 * n + j], ins[3 * n + j]
            outs[3 * j][...], outs[3 * j + 1][...], outs[3 * j + 2][...] = _adamw_math(
                w_ref[...], g_ref[...], m_ref[...], v_ref[...])

    shapes = [jax.ShapeDtypeStruct(g.shape, F32) for g in gs for _ in range(3)]
    flat = pl.pallas_call(body, name="adamw_small", out_shape=shapes)(*gs, *ws, *ms, *vs)
    return [tuple(flat[3 * j:3 * j + 3]) for j in range(n)]


SMALL_ROWS = 8
LOSS_LANE = D_MODEL - 1


def _pack_small(g_fn, g_kvn, g_bn, g_an, g_ckvn, g_qn, g_conv, loss_part):
    def body(fn_ref, kvn_ref, bn_ref, an_ref, ckvn_ref, qn_ref, conv_ref, loss_ref, o_ref):
        o_ref[0:1, :] = fn_ref[...]
        o_ref[1:2, :] = kvn_ref[...]
        o_ref[2:3, :] = bn_ref[...]
        o_ref[3:4, :] = an_ref[...]
        lane = lax.broadcasted_iota(jnp.int32, (1, D_MODEL), 1)
        o_ref[4:5, :] = jnp.where(lane == LOSS_LANE, loss_ref[...], 0.0)
        o_ref[4:5, 0:KV_RANK] = ckvn_ref[...]
        o_ref[4:5, KV_RANK:KV_RANK + Q_RANK] = qn_ref[...]
        o_ref[5:8, :] = conv_ref[0:3, :]

    return pl.pallas_call(body, name="pack_small", out_shape=jax.ShapeDtypeStruct((SMALL_ROWS, D_MODEL), F32))(
        g_fn, g_kvn, g_bn, g_an, g_ckvn, g_qn, g_conv, loss_part)


def _pad_cols(a, width):
    return jnp.pad(a, ((0, 0), (0, width - a.shape[1])))


def _dkv_to_padded(a):
    r = a.shape[0]
    z = jnp.zeros((r, ROPE_LO), a.dtype)
    z2 = jnp.zeros((r, HEAD_PAD - ROPE_LO - QK_ROPE), a.dtype)
    return jnp.concatenate([a[:, :KV_RANK], z, a[:, KV_RANK:], z2], axis=1)


def _dkv_from_padded(a):
    return jnp.concatenate([a[:, :KV_RANK], a[:, KV_RANK + ROPE_LO:KV_RANK + ROPE_LO + QK_ROPE]], axis=1)


def _unstack_cols(a):
    return jnp.transpose(a, (1, 0, 2)).reshape(a.shape[1], N_DEV * a.shape[2])


def kernel(x, positions, a_norm, a_w_in, a_conv, a_w_out, kv_norm, w_dkv, ckv_norm, w_ukv, b_norm, b_w_in, b_q_norm, b_w_uq, b_w_out, final_norm, loss_target, m_a_norm, m_a_w_in, m_a_conv, m_a_w_out, m_kv_norm, m_w_dkv, m_ckv_norm, m_w_ukv, m_b_norm, m_b_w_in, m_b_q_norm, m_b_w_uq, m_b_w_out, m_final_norm, v_a_norm, v_a_w_in, v_a_conv, v_a_w_out, v_kv_norm, v_w_dkv, v_ckv_norm, v_w_ukv, v_b_norm, v_b_w_in, v_b_q_norm, v_b_w_uq, v_b_w_out, v_final_norm):
    n_seq, seq, _ = x.shape
    t = n_seq * seq
    me = 4 * lax.axis_index("x") + 2 * lax.axis_index("y") + lax.axis_index("c")

    big = {
        "a_w_in": (a_w_in[0], m_a_w_in[0], v_a_w_in[0]),
        "a_w_out": (a_w_out[0], m_a_w_out[0], v_a_w_out[0]),
        "w_dkv": tuple(_dkv_to_padded(a) for a in (w_dkv, m_w_dkv, v_w_dkv)),
        "w_ukv": (w_ukv, m_w_ukv, v_w_ukv),
        "b_w_in": (b_w_in[0], m_b_w_in[0], v_b_w_in[0]),
        "b_w_uq": tuple(_pad_cols(a[0], HEAD_PAD) for a in (b_w_uq, m_b_w_uq, v_b_w_uq)),
        "b_w_out": (b_w_out[0], m_b_w_out[0], v_b_w_out[0]),
    }
    names = list(big)
    first_names, later_names = names[:2], names[2:]
    gathered = _all_gather([big[n][0].astype(BF16) for n in first_names] + [a_norm, a_conv[0]])
    a_norm_f = gathered[2].reshape(1, D_MODEL)
    a_conv_f = _unstack_cols(gathered[3])
    w_a_in = gathered[0]
    w_a_out = gathered[1].reshape(D_MODEL, D_MODEL)

    x2d = x.reshape(t, D_MODEL)
    tgt = loss_target.reshape(t, D_MODEL)
    pos = positions.astype(F32).reshape(t, 1)
    rc = _rope_consts()
    kvn, ckvn, bn, qn, fn = kv_norm.reshape(1, -1), ckv_norm.reshape(1, -1), b_norm, b_q_norm, final_norm.reshape(1, -1)

    x1, proj, conv, later = _conv_fwd(x2d, a_norm_f, w_a_in, a_conv_f, w_a_out, n_seq, seq,
                                      [big[n][0].astype(BF16) for n in later_names])
    full = dict(zip(later_names, later))
    w_dkv_f = full["w_dkv"].reshape(D_MODEL, CKR_PAD)
    w_ukv_f = _unstack_cols(full["w_ukv"])
    w_b_in = full["b_w_in"].reshape(D_MODEL, Q_RANK + E_B)
    w_uq_f = _unstack_cols(full["b_w_uq"])
    w_b_out = _unstack_cols(full["b_w_out"])
    layer_b = (kvn, w_dkv_f, ckvn, w_ukv_f, bn, w_b_in, qn, w_uq_f)
    ckr, cq, gb, q, k, v, cosv, sinv = _proj_fwd(x1, pos, rc, *layer_b)
    o, lse_row = _attn_fwd(q, k, v, n_seq, seq)
    dx2, do, dgb, delta_row, g_b_out, g_fn, loss_part = _head_tail(o, gb, x1, w_b_out, fn, tgt, n_seq, seq)
    dk, dv, dq = _attn_bwd(q, k, v, do, lse_row, delta_row, n_seq, seq)
    dx1, g_uq, g_b_in, g_ukv, g_dkv, g_qn, g_bn, g_ckvn, g_kvn = _proj_bwd(
        dq, dk, dv, dgb, cq, ckr, x1, dx2, cosv, sinv, rc, *layer_b)
    stacks = {
        "w_dkv": g_dkv.reshape(N_DEV, D_MODEL // N_DEV, CKR_PAD),
        "w_ukv": g_ukv,
        "b_w_in": g_b_in.reshape(N_DEV, D_MODEL // N_DEV, Q_RANK + E_B),
        "b_w_uq": g_uq,
        "b_w_out": g_b_out,
    }
    (dx, h_a, dproj, g_a_out, g_an, g_conv), later_parts = _conv_bwd(
        dx1, x2d, proj, conv, a_norm_f, w_a_in, a_conv_f, w_a_out, n_seq, seq, [stacks[n] for n in later_names])
    parts = dict(zip(later_names, later_parts))
    small = _pack_small(g_fn, g_kvn, g_bn, g_an, g_ckvn, g_qn, g_conv, loss_part)
    a_in_own, a_in_got, (parts["a_w_out"],), small_parts = _w_in_grad_exchange(
        h_a, dproj, D_MODEL * 4 // N_DEV, (me // 2).reshape(1).astype(jnp.int32),
        [g_a_out.astype(BF16).reshape(N_DEV, D_MODEL // N_DEV, D_MODEL)], small)

    outs = {"a_w_in": _adamw_reduce(a_in_own, a_in_got, *big["a_w_in"])}
    outs.update(zip(names[1:], _adamw_reduce_many([parts[n] for n in names[1:]], [big[n] for n in names[1:]])))
    outs["w_dkv"] = tuple(_dkv_from_padded(a) for a in outs["w_dkv"])
    outs["b_w_uq"] = tuple(a[:, :QK_NOPE + QK_ROPE] for a in outs["b_w_uq"])
    for n in ("a_w_in", "a_w_out", "b_w_in", "b_w_uq", "b_w_out"):
        outs[n] = tuple(a[None] for a in outs[n])

    total = _sum_parts(small_parts)
    loss = total[4, LOSS_LANE]
    shard = D_MODEL // N_DEV
    g_small = {
        "final_norm": total[0], "kv_norm": total[1], "b_norm": total[2:3],
        "a_norm": lax.dynamic_slice_in_dim(total[3:4], me * shard, shard, axis=1),
        "ckv_norm": total[4, 0:KV_RANK], "b_q_norm": total[4:5, KV_RANK:KV_RANK + Q_RANK],
        "a_conv": lax.dynamic_slice_in_dim(total[5:8], me * shard, shard, axis=1)[None],
    }
    small_state = {
        "final_norm": (final_norm, m_final_norm, v_final_norm), "kv_norm": (kv_norm, m_kv_norm, v_kv_norm),
        "b_norm": (b_norm, m_b_norm, v_b_norm), "a_norm": (a_norm, m_a_norm, v_a_norm),
        "ckv_norm": (ckv_norm, m_ckv_norm, v_ckv_norm), "b_q_norm": (b_q_norm, m_b_q_norm, v_b_q_norm),
        "a_conv": (a_conv, m_a_conv, v_a_conv),
    }
    small_names = list(g_small)
    as2d = lambda a: a.reshape(-1, a.shape[-1])
    upd = _adamw_small([as2d(g_small[n]) for n in small_names],
                       *[[as2d(small_state[n][j]) for n in small_names] for j in range(3)])
    for n, u in zip(small_names, upd):
        outs[n] = (g_small[n],) + tuple(a.reshape(g_small[n].shape) for a in u)

    order = ["a_norm", "a_w_in", "a_conv", "a_w_out", "kv_norm", "w_dkv", "ckv_norm", "w_ukv", "b_norm", "b_w_in",
             "b_q_norm", "b_w_uq", "b_w_out", "final_norm"]
    result = [loss, dx.reshape(n_seq, seq, D_MODEL)]
    for j in range(4):
        result += [outs[n][j] for n in order]
    return tuple(result)
```

```python
import functools
import math

import numpy as np
import jax
import jax.numpy as jnp
from jax import lax
from jax.experimental import pallas as pl
from jax.experimental.pallas import tpu as pltpu

F32 = jnp.float32
BF16 = jnp.bfloat16
SAVED = BF16

D_MODEL = 1024
N_HEADS = 8
QK_NOPE = 64
QK_ROPE = 32
V_HEAD = 64
KV_RANK = 256
Q_RANK = 384
E_B = N_HEADS * V_HEAD
HEAD_PAD = 128
CKR_PAD = KV_RANK + HEAD_PAD
ROPE_LO = QK_NOPE
ROPE_HALF = QK_ROPE // 2
ROPE_THETA = 10000.0
SOFTMAX_SCALE = 1.0 / math.sqrt(QK_NOPE + QK_ROPE)
LOG2_E = math.log2(math.e)
LN_2 = math.log(2.0)
EPS = 1e-6
N_DEV = 8

ADAM_LR = 0.001
ADAM_B1 = 0.9
ADAM_B2 = 0.999
ADAM_EPS = 1e-08
ADAM_WD = 0.01
ADAM_STEP = 10

CONV_BWD_TILE = 256
CONV_FWD_TILE = 512
LATENT_BWD_TILE = 512
LATENT_FWD_TILE = 1024
HEAD_TAIL_TILE = 512
PICKUP_BLOCK = 1
ATTN_TILE = 512
ATTN_GROUP = 4
ATTN_BWD_GROUP = 4
VMEM_LIMIT = 56 * 1024 * 1024

MESH = pl.DeviceIdType.MESH
NT = (((1,), (1,)), ((), ()))
TN = (((0,), (0,)), ((), ()))


def _dot(a, b):
    return jnp.dot(a.astype(BF16), b.astype(BF16), preferred_element_type=F32)


def _dot_nt(a, b):
    return lax.dot_general(a.astype(BF16), b.astype(BF16), NT, preferred_element_type=F32)


def _dot_tn(a, b):
    return lax.dot_general(a.astype(BF16), b.astype(BF16), TN, preferred_element_type=F32)


def _rstd(x):
    return lax.rsqrt(jnp.mean(x * x, axis=-1, keepdims=True) + EPS)


def _norm_bwd(a, xh, r):
    return r * (a - xh * jnp.mean(a * xh, axis=-1, keepdims=True))


def _silu_parts(g):
    sg = jax.nn.sigmoid(g)
    return g * sg, sg * (1.0 + g * (1.0 - sg))


def _rope_consts():
    inv = (ROPE_THETA ** (-np.arange(0, QK_ROPE, 2, dtype=np.float32) / QK_ROPE)).astype(np.float32)
    t = np.zeros((8, HEAD_PAD), np.float32)
    t[0, ROPE_LO:ROPE_LO + ROPE_HALF] = inv
    t[0, ROPE_LO + ROPE_HALF:ROPE_LO + QK_ROPE] = inv
    t[1, ROPE_LO:ROPE_LO + ROPE_HALF] = -1.0
    t[2, ROPE_LO + ROPE_HALF:ROPE_LO + QK_ROPE] = 1.0
    return jnp.asarray(t)


def _rope_angles(pos, rc):
    ang = pos * rc[0:1, :]
    return jnp.cos(ang), jnp.sin(ang)


def _rope_tables(cosv, sinv, rc):
    return cosv, sinv * rc[1:2, :], sinv * rc[2:3, :]


def _rope(x, ct, s1, s2):
    up = pltpu.roll(x, HEAD_PAD - ROPE_HALF, 1)
    dn = pltpu.roll(x, ROPE_HALF, 1)
    return x * ct + up * s1 + dn * s2


def _const_spec(shape):
    nd = len(shape)
    return pl.BlockSpec(shape, lambda *_: (0,) * nd, pipeline_mode=pl.Buffered(1))


def _acc_spec(shape):
    nd = len(shape)
    return pl.BlockSpec(shape, lambda *_: (0,) * nd)


def _params(n_axes):
    return pltpu.CompilerParams(dimension_semantics=("arbitrary",) * n_axes, vmem_limit_bytes=VMEM_LIMIT)


def _place():
    x, y, c = lax.axis_index("x"), lax.axis_index("y"), lax.axis_index("c")
    return x, y, c, 4 * x + 2 * y + c


def _peer(x, y, c, mask):
    px = 1 - x if mask & 4 else x
    py = 1 - y if mask & 2 else y
    pc = 1 - c if mask & 1 else c
    return (px, py, pc), 4 * px + 2 * py + pc


ANY_SPEC = pl.BlockSpec(memory_space=pl.ANY)


def _comm_sems(n):
    return [pltpu.SemaphoreType.DMA((n, N_DEV - 1)), pltpu.SemaphoreType.DMA((n, N_DEV - 1)), pltpu.SemaphoreType.DMA((n,))]


def _gather_copies(ins, outs, sems):
    send_sems, recv_sems, local_sems = sems
    x, y, c, me = _place()
    starts, waits = [], []
    for w in range(len(ins)):
        mine = pltpu.make_async_copy(ins[w], outs[w].at[me], local_sems.at[w])
        starts.append(mine)
        waits.append(mine)
        for mask in range(1, N_DEV):
            peer, peer_idx = _peer(x, y, c, mask)
            starts.append(pltpu.make_async_remote_copy(
                src_ref=ins[w], dst_ref=outs[w].at[me], send_sem=send_sems.at[w, mask - 1],
                recv_sem=recv_sems.at[w, mask - 1], device_id=peer, device_id_type=MESH))
            waits.append(pltpu.make_async_remote_copy(
                src_ref=ins[w], dst_ref=outs[w].at[peer_idx], send_sem=send_sems.at[w, mask - 1],
                recv_sem=recv_sems.at[w, mask - 1], device_id=peer, device_id_type=MESH))
    return starts, waits


def _scatter_copies(ins, outs, sems):
    send_sems, recv_sems, local_sems = sems
    x, y, c, me = _place()
    copies = []
    for w in range(len(ins)):
        copies.append(pltpu.make_async_copy(ins[w].at[me], outs[w].at[0], local_sems.at[w]))
        for mask in range(1, N_DEV):
            peer, peer_idx = _peer(x, y, c, mask)
            copies.append(pltpu.make_async_remote_copy(
                src_ref=ins[w].at[peer_idx], dst_ref=outs[w].at[mask], send_sem=send_sems.at[w, mask - 1],
                recv_sem=recv_sems.at[w, mask - 1], device_id=peer, device_id_type=MESH))
    return copies, copies


def _stacked(arrays):
    return [jax.ShapeDtypeStruct((N_DEV,) + a.shape, a.dtype) for a in arrays]


def _all_gather(shards):
    n = len(shards)

    def body(*refs):
        ins, outs = refs[:n], refs[n:2 * n]
        send_sems, recv_sems, local_sems = refs[2 * n:]
        x, y, c, me = _place()
        sibling = (x, y, 1 - c)
        chips = [(1 - x, y), (x, 1 - y), (1 - x, 1 - y)]

        def copy(w, k, block, to, src=None):
            idx = 4 * block[0] + 2 * block[1] + block[2]
            return pltpu.make_async_remote_copy(
                src_ref=outs[w].at[idx] if src is None else src, dst_ref=outs[w].at[idx],
                send_sem=send_sems.at[w, k], recv_sem=recv_sems.at[w, k], device_id=to, device_id_type=MESH)

        local, sent = [], []
        for w in range(n):
            mine = pltpu.make_async_copy(ins[w], outs[w].at[me], local_sems.at[w])
            mine.start()
            local.append(mine)
            first = [copy(w, 0, (x, y, c), sibling, src=ins[w])]
            first += [copy(w, 1 + j, (x, y, c), (*chip, c), src=ins[w]) for j, chip in enumerate(chips)]
            for cp in first:
                cp.start()
            sent += first
        for w in range(n):
            for j, chip in enumerate(chips):
                copy(w, 1 + j, (*chip, c), (x, y, c)).wait_recv()
                onward = copy(w, 4 + j, (*chip, c), sibling)
                onward.start()
                sent.append(onward)
        for w in range(n):
            copy(w, 0, sibling, (x, y, c)).wait_recv()
            for j, chip in enumerate(chips):
                copy(w, 4 + j, (*chip, 1 - c), (x, y, c)).wait_recv()
        for cp in sent:
            cp.wait_send()
        for cp in local:
            cp.wait()

    return pl.pallas_call(
        body, name="weight_all_gather", out_shape=_stacked(shards),
        in_specs=[ANY_SPEC] * n, out_specs=[ANY_SPEC] * n, scratch_shapes=_comm_sems(n),
    )(*shards)


def _conv_fwd(x, a_norm, w_in, conv_w, w_out, n_seq, seq, later_shards):
    tm = CONV_FWD_TILE
    nt = seq // tm
    n_col = w_in.shape[2]
    n_later = len(later_shards)

    def body(x_ref, an_ref, win_ref, cw_ref, wout_ref, *rest):
        shard_refs, rest = rest[:n_later], rest[n_later:]
        x1_ref, proj_ref, conv_ref = rest[:3]
        stack_refs, rest = rest[3:3 + n_later], rest[3 + n_later:]
        prev_ref, wide_ref, sems = rest[0], rest[1], rest[2:]
        step = pl.program_id(0) * nt + pl.program_id(1)

        @pl.when(step == 0)
        def _():
            for cp in _gather_copies(shard_refs, stack_refs, sems)[0]:
                cp.start()

        @pl.when(pl.program_id(1) == 0)
        def _():
            prev_ref[...] = jnp.zeros_like(prev_ref)

        xv = x_ref[...]
        h = (xv * _rstd(xv) * an_ref[...]).astype(BF16)
        for d in range(N_DEV):
            part = jnp.dot(h, win_ref[d], preferred_element_type=F32)
            wide_ref[:, d * n_col:(d + 1) * n_col] = part
            proj_ref[:, d * n_col:(d + 1) * n_col] = part.astype(SAVED)
        b = wide_ref[:, 0:D_MODEL]
        v = wide_ref[:, D_MODEL:2 * D_MODEL] * wide_ref[:, 2 * D_MODEL:3 * D_MODEL]
        g = wide_ref[:, 3 * D_MODEL:4 * D_MODEL]
        w0, w1, w2 = cw_ref[0:1, :], cw_ref[1:2, :], cw_ref[2:3, :]
        conv_ref[...] = w0 * pltpu.roll(v, 2, 0) + w1 * pltpu.roll(v, 1, 0) + w2 * v
        rows = lax.broadcasted_iota(jnp.int32, (8, D_MODEL), 0)
        p8, v8 = prev_ref[...], v[0:8]
        back1 = jnp.where(rows < 1, pltpu.roll(p8, 1, 0), pltpu.roll(v8, 1, 0))
        back2 = jnp.where(rows < 2, pltpu.roll(p8, 2, 0), pltpu.roll(v8, 2, 0))
        conv_ref[0:8, :] = w0 * back2 + w1 * back1 + w2 * v8
        prev_ref[...] = v[tm - 8:tm]
        silu, _ = _silu_parts(g)
        yv = silu * b * conv_ref[...]
        x1_ref[...] = xv + _dot(yv, wout_ref[...])

        @pl.when(step == n_seq * nt - 1)
        def _():
            for cp in _gather_copies(shard_refs, stack_refs, sems)[1]:
                cp.wait()

    tok = lambda width: pl.BlockSpec((tm, width), lambda s, i: (s * nt + i, 0))
    t = n_seq * seq
    outs = pl.pallas_call(
        body, name="conv_mixer_fwd", grid=(n_seq, nt),
        in_specs=[tok(D_MODEL), _const_spec((1, D_MODEL)), _const_spec(w_in.shape), _const_spec((3, D_MODEL)),
                  _const_spec(w_out.shape)] + [ANY_SPEC] * n_later,
        out_specs=[tok(D_MODEL), tok(4 * D_MODEL), tok(D_MODEL)] + [ANY_SPEC] * n_later,
        out_shape=[jax.ShapeDtypeStruct((t, D_MODEL), F32), jax.ShapeDtypeStruct((t, 4 * D_MODEL), SAVED),
                   jax.ShapeDtypeStruct((t, D_MODEL), F32)] + _stacked(later_shards),
        scratch_shapes=[pltpu.VMEM((8, D_MODEL), F32), pltpu.VMEM((tm, 4 * D_MODEL), F32)] + _comm_sems(n_later),
        compiler_params=_params(2),
    )(x, a_norm, w_in, conv_w, w_out, *later_shards)
    return outs[0], outs[1], outs[2], outs[3:]


def _conv_bwd(dx1, x, proj, conv, a_norm, w_in, conv_w, w_out, n_seq, seq, ready_stacks):
    tm = CONV_BWD_TILE
    nt = seq // tm
    n_col = w_in.shape[2]
    n_ready = len(ready_stacks)

    def body(dx1_ref, x_ref, proj_ref, conv_ref, an_ref, win_ref, cw_ref, wout_ref, *rest):
        ready_refs, rest = rest[:n_ready], rest[n_ready:]
        dx_ref, h_ref, dproj_ref, dwout_ref, dan_ref, dcw_ref = rest[:6]
        part_refs, rest = rest[6:6 + n_ready], rest[6 + n_ready:]
        next_ref, d1_ref, d2_ref = rest[:3]
        sems = rest[3:]
        step = pl.program_id(0) * nt + pl.program_id(1)
        first = step == 0

        @pl.when(first)
        def _():
            for cp in _scatter_copies(ready_refs, part_refs, sems)[0]:
                cp.start()
            dwout_ref[...] = jnp.zeros_like(dwout_ref)
            dan_ref[...] = jnp.zeros_like(dan_ref)
            dcw_ref[...] = jnp.zeros_like(dcw_ref)

        @pl.when(pl.program_id(1) == 0)
        def _():
            next_ref[...] = jnp.zeros_like(next_ref)

        dx1v = dx1_ref[...]
        dy = _dot_nt(dx1v, wout_ref[...])
        b = proj_ref[:, 0:D_MODEL].astype(F32)
        cc = proj_ref[:, D_MODEL:2 * D_MODEL].astype(F32)
        u = proj_ref[:, 2 * D_MODEL:3 * D_MODEL].astype(F32)
        g = proj_ref[:, 3 * D_MODEL:4 * D_MODEL].astype(F32)
        cv = conv_ref[...]
        silu, dsilu = _silu_parts(g)
        per_part = D_MODEL // n_col

        def back_through_w_in(part, grad):
            grad = grad.astype(BF16)
            dproj_ref[:, part * D_MODEL:(part + 1) * D_MODEL] = grad
            term = None
            for j in range(per_part):
                d = part * per_part + j
                piece = lax.dot_general(grad[:, j * n_col:(j + 1) * n_col], win_ref[d], NT, preferred_element_type=F32)
                term = piece if term is None else term + piece
            return term

        dh = back_through_w_in(0, dy * silu * cv)
        dh += back_through_w_in(3, dy * b * cv * dsilu)
        dwout_ref[...] += _dot_tn(silu * b * cv, dx1v)
        dconv = dy * silu * b
        d1_ref[...] = pltpu.roll(dconv, tm - 1, 0)
        d2_ref[...] = pltpu.roll(dconv, tm - 2, 0)
        rows = lax.broadcasted_iota(jnp.int32, (8, D_MODEL), 0)
        n8, c8 = next_ref[...], dconv[tm - 8:tm]
        d1_ref[tm - 8:tm, :] = jnp.where(rows >= 7, pltpu.roll(n8, 7, 0), pltpu.roll(c8, 7, 0))
        d2_ref[tm - 8:tm, :] = jnp.where(rows >= 6, pltpu.roll(n8, 6, 0), pltpu.roll(c8, 6, 0))
        next_ref[...] = dconv[0:8]
        d1, d2 = d1_ref[...], d2_ref[...]
        v = cc * u
        dcw_ref[0:1, :] += jnp.sum(d2 * v, axis=0, keepdims=True)
        dcw_ref[1:2, :] += jnp.sum(d1 * v, axis=0, keepdims=True)
        dcw_ref[2:3, :] += jnp.sum(dconv * v, axis=0, keepdims=True)
        dv = cw_ref[0:1, :] * d2 + cw_ref[1:2, :] * d1 + cw_ref[2:3, :] * dconv
        dh += back_through_w_in(1, dv * u)
        dh += back_through_w_in(2, dv * cc)
        xv = x_ref[...]
        r = _rstd(xv)
        xh = xv * r
        h_ref[...] = (xh * an_ref[...]).T.astype(BF16)
        dan_ref[...] += jnp.sum(dh * xh, axis=0, keepdims=True)
        dx_ref[...] = dx1v + _norm_bwd(dh * an_ref[...], xh, r)

        @pl.when(step == n_seq * nt - 1)
        def _():
            for cp in _scatter_copies(ready_refs, part_refs, sems)[1]:
                cp.wait()

    tok = lambda width: pl.BlockSpec((tm, width), lambda s, i: (s * nt + nt - 1 - i, 0))
    t = n_seq * seq
    outs = pl.pallas_call(
        body, name="conv_mixer_bwd", grid=(n_seq, nt),
        in_specs=[tok(D_MODEL), tok(D_MODEL), tok(4 * D_MODEL), tok(D_MODEL), _const_spec((1, D_MODEL)),
                  _const_spec(w_in.shape), _const_spec((3, D_MODEL)), _const_spec(w_out.shape)] + [ANY_SPEC] * n_ready,
        out_specs=[tok(D_MODEL), pl.BlockSpec((D_MODEL, tm), lambda s, i: (0, s * nt + nt - 1 - i)),
                   tok(4 * D_MODEL), _acc_spec((D_MODEL, D_MODEL)),
                   _acc_spec((1, D_MODEL)), _acc_spec((8, D_MODEL))] + [ANY_SPEC] * n_ready,
        out_shape=[jax.ShapeDtypeStruct((t, D_MODEL), F32), jax.ShapeDtypeStruct((D_MODEL, t), BF16),
                   jax.ShapeDtypeStruct((t, 4 * D_MODEL), BF16), jax.ShapeDtypeStruct((D_MODEL, D_MODEL), F32),
                   jax.ShapeDtypeStruct((1, D_MODEL), F32), jax.ShapeDtypeStruct((8, D_MODEL), F32)]
        + [jax.ShapeDtypeStruct(a.shape, a.dtype) for a in ready_stacks],
        scratch_shapes=[pltpu.VMEM((8, D_MODEL), F32), pltpu.VMEM((tm, D_MODEL), F32), pltpu.VMEM((tm, D_MODEL), F32)]
        + _comm_sems(n_ready),
        compiler_params=_params(2),
    )(dx1, x, proj, conv, a_norm, w_in, conv_w, w_out, *ready_stacks)
    return outs[:6], outs[6:]


def _w_in_grad_exchange(a_t, b, n_col, chip, ready_stacks, small):
    r, t = a_t.shape
    bt = 4096
    nk = t // bt
    n_ready = len(ready_stacks)
    n_chip = N_DEV // 2
    n_remote = N_DEV - 2

    def body(chip_ref, a_ref, b_ref, *rest):
        ready_refs, small_ref, rest = rest[:n_ready], rest[n_ready], rest[n_ready + 1:]
        own_ref, got_ref, rest = rest[0], rest[1], rest[2:]
        part_refs, small_all_ref, rest = rest[:n_ready], rest[n_ready], rest[n_ready + 1:]
        acc_ref, land_ref, stage_ref, pair_send, pair_recv, chip_send, chip_recv = rest[:7]
        scatter_sems, gather_sems = rest[7:10], rest[10:13]
        s, k = pl.program_id(0), pl.program_id(1)
        x, y, c, _ = _place()
        chip = 2 * x + y

        @pl.when(jnp.logical_and(s == 0, k == 0))
        def _():
            for cp in _scatter_copies(ready_refs, part_refs, scatter_sems)[0]:
                cp.start()
            for cp in _gather_copies([small_ref], [small_all_ref], gather_sems)[0]:
                cp.start()

        for parity in range(2):
            @pl.when(s % 2 == parity)
            def _(parity=parity):
                @pl.when(k == 0)
                def _():
                    acc_ref[parity] = jnp.zeros((r, n_col), F32)

                acc_ref[parity] += jnp.dot(a_ref[...], b_ref[...], preferred_element_type=F32)

        def to_sibling(step):
            return pltpu.make_async_remote_copy(
                src_ref=acc_ref.at[step % 2], dst_ref=land_ref, send_sem=pair_send.at[step], recv_sem=pair_recv.at[step],
                device_id=(x, y, 1 - c), device_id_type=MESH)

        def to_owner(nth):
            owner_chip = (chip + 1 + nth) % n_chip
            slot = jnp.bitwise_xor(chip, owner_chip) - 1
            return pltpu.make_async_remote_copy(
                src_ref=stage_ref.at[nth % 2], dst_ref=got_ref.at[slot], send_sem=chip_send.at[nth],
                recv_sem=chip_recv.at[slot], device_id=(owner_chip // 2, owner_chip % 2, c), device_id_type=MESH)

        for step in range(N_DEV):
            owner_core = step % 2

            @pl.when(jnp.logical_and(s == step, k == nk - 1))
            def _(step=step, owner_core=owner_core):
                @pl.when(c != owner_core)
                def _():
                    to_sibling(step).start()

                if step >= 1:
                    @pl.when(c == owner_core)
                    def _():
                        to_sibling(step - 1).wait_send()

            if step < N_DEV - 1:
                pickup = jnp.logical_and(s == step + 1, k == min(PICKUP_BLOCK, nk - 1))
            else:
                pickup = jnp.logical_and(s == step, k == nk - 1)

            @pl.when(jnp.logical_and(pickup, c == owner_core))
            def _(step=step):
                to_sibling(step).wait_recv()
                total = acc_ref[step % 2] + land_ref[...]
                if step < n_remote:
                    nth = step // 2
                    if nth >= 2:
                        to_owner(nth - 2).wait_send()
                    stage_ref[nth % 2] = total.astype(BF16)
                    to_owner(nth).start()
                else:
                    own_ref[...] = total

        @pl.when(jnp.logical_and(s == N_DEV - 1, k == nk - 1))
        def _():
            @pl.when(c != (N_DEV - 1) % 2)
            def _():
                to_sibling(N_DEV - 1).wait_send()

            to_owner(1).wait_send()
            to_owner(2).wait_send()
            for slot in range(n_chip - 1):
                pltpu.make_async_remote_copy(
                    src_ref=stage_ref.at[0], dst_ref=got_ref.at[slot], send_sem=chip_send.at[0],
                    recv_sem=chip_recv.at[slot], device_id=(x, y, c), device_id_type=MESH).wait_recv()
            for cp in _scatter_copies(ready_refs, part_refs, scatter_sems)[1]:
                cp.wait()
            for cp in _gather_copies([small_ref], [small_all_ref], gather_sems)[1]:
                cp.wait()

    def owner_block(s, k, chip):
        return (k, 2 * ((chip[0] + 1 + s // 2) % n_chip) + s % 2)

    grid_spec = pltpu.PrefetchScalarGridSpec(
        num_scalar_prefetch=1, grid=(N_DEV, nk),
        in_specs=[pl.BlockSpec((r, bt), lambda s, k, chip: (0, k)), pl.BlockSpec((bt, n_col), owner_block)]
        + [ANY_SPEC] * (n_ready + 1),
        out_specs=[pl.BlockSpec((r, n_col), lambda s, k, chip: (0, 0)), ANY_SPEC] + [ANY_SPEC] * (n_ready + 1),
        scratch_shapes=[pltpu.VMEM((2, r, n_col), F32), pltpu.VMEM((r, n_col), F32), pltpu.VMEM((2, r, n_col), BF16),
                        pltpu.SemaphoreType.DMA((N_DEV,)), pltpu.SemaphoreType.DMA((N_DEV,)),
                        pltpu.SemaphoreType.DMA((n_chip - 1,)), pltpu.SemaphoreType.DMA((n_chip - 1,))]
        + _comm_sems(n_ready) + _comm_sems(1))
    outs = pl.pallas_call(
        body, name="w_in_grad_exchange", grid_spec=grid_spec,
        out_shape=[jax.ShapeDtypeStruct((r, n_col), F32), jax.ShapeDtypeStruct((n_chip - 1, r, n_col), BF16)]
        + [jax.ShapeDtypeStruct(p.shape, p.dtype) for p in ready_stacks] + _stacked([small]),
        compiler_params=_params(2),
    )(chip, a_t, b, *ready_stacks, small)
    return outs[0], outs[1], outs[2:2 + n_ready], outs[2 + n_ready]


def _proj_fwd(x1, pos, rc, kv_norm, w_dkv, ckv_norm, w_ukv, b_norm, b_w_in, q_norm, w_uq):
    t = x1.shape[0]
    tm = LATENT_FWD_TILE

    def body(x1_ref, pos_ref, rc_ref, kvn_ref, wdkv_ref, ckvn_ref, wukv_ref, bn_ref, bwin_ref, qn_ref, wuq_ref,
             ckr_ref, cq_ref, gb_ref, q_ref, k_ref, v_ref, cos_ref, sin_ref):
        xv = x1_ref[...]
        xh = xv * _rstd(xv)
        ckr = _dot(xh * kvn_ref[...], wdkv_ref[...])
        pb = _dot(xh * bn_ref[...], bwin_ref[...])
        ckr_ref[...] = ckr
        ckv = ckr[:, 0:KV_RANK]
        kv = _dot(ckv * _rstd(ckv) * ckvn_ref[...], wukv_ref[...])
        cosv, sinv = _rope_angles(pos_ref[...], rc_ref[...])
        cos_ref[...] = cosv
        sin_ref[...] = sinv
        ct, s1, s2 = _rope_tables(cosv, sinv, rc_ref[...])
        k_rope = _rope(ckr[:, KV_RANK:CKR_PAD], ct, s1, s2)
        lane = lax.broadcasted_iota(jnp.int32, (tm, HEAD_PAD), 1)
        low = lane < QK_NOPE
        for h in range(N_HEADS):
            kv_h = kv[:, h * HEAD_PAD:(h + 1) * HEAD_PAD]
            k_ref[:, h * HEAD_PAD:(h + 1) * HEAD_PAD] = jnp.where(low, kv_h, k_rope).astype(BF16)
            v_ref[:, h * HEAD_PAD:(h + 1) * HEAD_PAD] = jnp.where(low, pltpu.roll(kv_h, V_HEAD, 1), 1.0).astype(BF16)
        cq = pb[:, 0:Q_RANK]
        cq_ref[...] = cq
        gb_ref[...] = pb[:, Q_RANK:Q_RANK + E_B]
        q = _dot(cq * _rstd(cq) * qn_ref[...], wuq_ref[...])
        for h in range(N_HEADS):
            q_ref[:, h * HEAD_PAD:(h + 1) * HEAD_PAD] = _rope(
                q[:, h * HEAD_PAD:(h + 1) * HEAD_PAD], ct, s1, s2).astype(BF16)

    tok = lambda width: pl.BlockSpec((tm, width), lambda i: (i, 0))
    weights = [kv_norm, w_dkv, ckv_norm, w_ukv, b_norm, b_w_in, q_norm, w_uq]
    wide = N_HEADS * HEAD_PAD
    return pl.pallas_call(
        body, name="latent_proj_fwd", grid=(t // tm,),
        in_specs=[tok(D_MODEL), tok(1), _const_spec(rc.shape)] + [_const_spec(w.shape) for w in weights],
        out_specs=[tok(CKR_PAD), tok(Q_RANK), tok(E_B), tok(wide), tok(wide), tok(wide), tok(HEAD_PAD), tok(HEAD_PAD)],
        out_shape=[jax.ShapeDtypeStruct((t, CKR_PAD), F32), jax.ShapeDtypeStruct((t, Q_RANK), F32),
                   jax.ShapeDtypeStruct((t, E_B), F32), jax.ShapeDtypeStruct((t, wide), BF16),
                   jax.ShapeDtypeStruct((t, wide), BF16), jax.ShapeDtypeStruct((t, wide), BF16),
                   jax.ShapeDtypeStruct((t, HEAD_PAD), F32), jax.ShapeDtypeStruct((t, HEAD_PAD), F32)],
        compiler_params=_params(1),
    )(x1, pos, rc, *weights)


def _proj_bwd(dq, dk, dv, dgb, cq, ckr, x1, dx2, cosv, sinv, rc, kv_norm, w_dkv, ckv_norm, w_ukv, b_norm, b_w_in, q_norm, w_uq):
    t = x1.shape[0]
    tm = LATENT_BWD_TILE
    wide = N_HEADS * HEAD_PAD

    def body(dq_ref, dk_ref, dv_ref, dgb_ref, cq_ref, ckr_ref, x1_ref, dx2_ref, cos_ref, sin_ref, rc_ref,
             kvn_ref, wdkv_ref, ckvn_ref, wukv_ref, bn_ref, bwin_ref, qn_ref, wuq_ref,
             dx1_ref, dwuq_ref, dwbin_ref, dwukv_ref, dwdkv_ref, dqn_ref, dbn_ref, dckvn_ref, dkvn_ref, dqu_ref, dkv_ref):
        @pl.when(pl.program_id(0) == 0)
        def _():
            for ref in (dwuq_ref, dwbin_ref, dwukv_ref, dwdkv_ref, dqn_ref, dbn_ref, dckvn_ref, dkvn_ref):
                ref[...] = jnp.zeros_like(ref)

        ct, s1, s2 = _rope_tables(cos_ref[...], sin_ref[...], rc_ref[...])
        lane = lax.broadcasted_iota(jnp.int32, (tm, HEAD_PAD), 1)
        low = lane < QK_NOPE
        xv = x1_ref[...]
        r = _rstd(xv)
        xh = xv * r
        dk_rope = jnp.zeros((tm, HEAD_PAD), F32)
        for h in range(N_HEADS):
            dk_h = dk_ref[:, h * HEAD_PAD:(h + 1) * HEAD_PAD].astype(F32)
            dv_h = dv_ref[:, h * HEAD_PAD:(h + 1) * HEAD_PAD].astype(F32)
            dkv_ref[:, h * HEAD_PAD:(h + 1) * HEAD_PAD] = jnp.where(low, dk_h, pltpu.roll(dv_h, V_HEAD, 1)).astype(BF16)
            dk_rope += dk_h
        rope_lanes = jnp.logical_and(lane >= ROPE_LO, lane < ROPE_LO + QK_ROPE)
        dk_rope = jnp.where(rope_lanes, _rope(dk_rope, ct, -s1, -s2), 0.0)
        ckv = ckr_ref[:, 0:KV_RANK]
        rk = _rstd(ckv)
        ckh = ckv * rk
        ckvn = (ckh * ckvn_ref[...]).astype(BF16)
        dwukv = lax.dot_general(ckvn, dkv_ref[...], TN, preferred_element_type=F32)
        for h in range(N_HEADS):
            dwukv_ref[h] += dwukv[:, h * HEAD_PAD:(h + 1) * HEAD_PAD]
        dckvn = lax.dot_general(dkv_ref[...], wukv_ref[...], NT, preferred_element_type=F32)
        dckvn_ref[...] += jnp.sum(dckvn * ckh, axis=0, keepdims=True)
        dckv = _norm_bwd(dckvn * ckvn_ref[...], ckh, rk)
        dckr = jnp.concatenate([dckv, dk_rope], axis=1).astype(BF16)
        dwdkv_ref[...] += _dot_tn(xh * kvn_ref[...], dckr)
        dh2 = lax.dot_general(dckr, wdkv_ref[...], NT, preferred_element_type=F32)
        for h in range(N_HEADS):
            dqu_ref[:, h * HEAD_PAD:(h + 1) * HEAD_PAD] = _rope(
                dq_ref[:, h * HEAD_PAD:(h + 1) * HEAD_PAD].astype(F32), ct, -s1, -s2).astype(BF16)
        cq = cq_ref[...]
        rq = _rstd(cq)
        cqh = cq * rq
        cqn = (cqh * qn_ref[...]).astype(BF16)
        dwuq = lax.dot_general(cqn, dqu_ref[...], TN, preferred_element_type=F32)
        for h in range(N_HEADS):
            dwuq_ref[h] += dwuq[:, h * HEAD_PAD:(h + 1) * HEAD_PAD]
        dcqn = lax.dot_general(dqu_ref[...], wuq_ref[...], NT, preferred_element_type=F32)
        dqn_ref[...] += jnp.sum(dcqn * cqh, axis=0, keepdims=True)
        dcq = _norm_bwd(dcqn * qn_ref[...], cqh, rq)
        dpb = jnp.concatenate([dcq.astype(BF16), dgb_ref[...]], axis=1)
        dwbin_ref[...] += _dot_tn(xh * bn_ref[...], dpb)
        dh3 = lax.dot_general(dpb, bwin_ref[...], NT, preferred_element_type=F32)
        dkvn_ref[...] += jnp.sum(dh2 * xh, axis=0, keepdims=True)
        dbn_ref[...] += jnp.sum(dh3 * xh, axis=0, keepdims=True)
        dx1_ref[...] = dx2_ref[...] + _norm_bwd(dh2 * kvn_ref[...] + dh3 * bn_ref[...], xh, r)

    tok = lambda width: pl.BlockSpec((tm, width), lambda i: (i, 0))
    weights = [kv_norm, w_dkv, ckv_norm, w_ukv, b_norm, b_w_in, q_norm, w_uq]
    acc_shapes = [(N_HEADS, Q_RANK, HEAD_PAD), (D_MODEL, Q_RANK + E_B), (N_HEADS, KV_RANK, HEAD_PAD), (D_MODEL, CKR_PAD),
                  (1, Q_RANK), (1, D_MODEL), (1, KV_RANK), (1, D_MODEL)]
    return pl.pallas_call(
        body, name="latent_proj_bwd", grid=(t // tm,),
        in_specs=[tok(wide), tok(wide), tok(wide), tok(E_B), tok(Q_RANK), tok(CKR_PAD), tok(D_MODEL), tok(D_MODEL),
                  tok(HEAD_PAD), tok(HEAD_PAD), _const_spec(rc.shape)] + [_const_spec(w.shape) for w in weights],
        out_specs=[tok(D_MODEL)] + [_acc_spec(s) for s in acc_shapes],
        out_shape=[jax.ShapeDtypeStruct((t, D_MODEL), F32)] + [jax.ShapeDtypeStruct(s, F32) for s in acc_shapes],
        scratch_shapes=[pltpu.VMEM((tm, wide), BF16), pltpu.VMEM((tm, wide), BF16)],
        compiler_params=_params(1),
    )(dq, dk, dv, dgb, cq, ckr, x1, dx2, cosv, sinv, rc, *weights)


def _attn_fwd(q, k, v, n_seq, seq):
    tb, group = ATTN_TILE, ATTN_GROUP
    rows = tb * group
    nq = seq // rows
    pair = 2 * HEAD_PAD

    def body(q_ref, k_ref, v_ref, o_ref, lr_ref):
        i = pl.program_id(2)
        row = lax.broadcasted_iota(jnp.int32, (tb, tb), 0)
        col = lax.broadcasted_iota(jnp.int32, (tb, tb), 1)
        causal = col <= row

        def step(j, carry, kinds):
            start = pl.multiple_of(j * tb, tb)
            out = []
            for g in range(group):
                for hh in range(2):
                    m, acc = carry[2 * g + hh]
                    if kinds[g] != "none":
                        heads = slice(hh * HEAD_PAD, (hh + 1) * HEAD_PAD)
                        s = lax.dot_general(q_ref[g * tb:(g + 1) * tb, heads], k_ref[pl.ds(start, tb), heads], NT,
                                            preferred_element_type=F32) * (SOFTMAX_SCALE * LOG2_E)
                        if kinds[g] == "diagonal":
                            s = jnp.where(causal, s, -jnp.inf)
                        m_new = jnp.maximum(m, jnp.max(s, axis=-1, keepdims=True))
                        p = jnp.exp2(s - m_new)
                        acc = jnp.exp2(m - m_new) * acc + jnp.dot(
                            p.astype(BF16), v_ref[pl.ds(start, tb), heads], preferred_element_type=F32)
                        m = m_new
                    out.append((m, acc))
            return tuple(out)

        one = (jnp.full((tb, 1), -jnp.inf, F32), jnp.zeros((tb, HEAD_PAD), F32))
        carry = lax.fori_loop(0, i * group, functools.partial(step, kinds=("full",) * group), (one,) * (2 * group))
        for d in range(group):
            kinds = tuple("none" if g < d else ("diagonal" if g == d else "full") for g in range(group))
            carry = step(i * group + d, carry, kinds)
        lane = lax.broadcasted_iota(jnp.int32, (tb, HEAD_PAD), 1)
        low = lane < V_HEAD
        for g in range(group):
            halves = []
            for hh in range(2):
                m, acc = carry[2 * g + hh]
                swapped = pltpu.roll(acc, V_HEAD, 1)
                halves.append(acc / swapped)
                lse = (m * LN_2) + jnp.log(jnp.where(low, swapped, acc))
                lr_ref[hh, :, g * tb:(g + 1) * tb] = lse.T[0:8, :]
            o_ref[g * tb:(g + 1) * tb, :] = jnp.where(low, halves[0], pltpu.roll(halves[1], V_HEAD, 1))

    t = n_seq * seq
    return pl.pallas_call(
        body, name="attention_fwd", grid=(n_seq, N_HEADS // 2, nq),
        in_specs=[pl.BlockSpec((rows, pair), lambda s, p, i: (s * nq + i, p)),
                  pl.BlockSpec((seq, pair), lambda s, p, i: (s, p)),
                  pl.BlockSpec((seq, pair), lambda s, p, i: (s, p))],
        out_specs=[pl.BlockSpec((rows, HEAD_PAD), lambda s, p, i: (s * nq + i, p)),
                   pl.BlockSpec((None, 2, 8, rows), lambda s, p, i: (s, p, 0, i))],
        out_shape=[jax.ShapeDtypeStruct((t, E_B), F32), jax.ShapeDtypeStruct((n_seq, N_HEADS, 8, seq), F32)],
        compiler_params=_params(3),
    )(q, k, v)


def _attn_bwd(q, k, v, do, lse_row, delta_row, n_seq, seq):
    tb, group = ATTN_TILE, ATTN_BWD_GROUP
    rows = tb * group
    nk = seq // rows
    n_inner = seq // tb
    pair = 2 * HEAD_PAD

    def body(q_ref, k_ref, v_ref, do_ref, lr_ref, dr_ref, dk_ref, dv_ref, dq_out_ref, dq_ref):
        j = pl.program_id(2)

        @pl.when(j == 0)
        def _():
            dq_ref[...] = jnp.zeros_like(dq_ref)

        row = lax.broadcasted_iota(jnp.int32, (tb, tb), 0)
        col = lax.broadcasted_iota(jnp.int32, (tb, tb), 1)
        causal = col >= row

        def step(i, carry, kinds):
            start = pl.multiple_of(i * tb, tb)
            out = []
            for hh in range(2):
                heads = slice(hh * HEAD_PAD, (hh + 1) * HEAD_PAD)
                qi = q_ref[pl.ds(start, tb), heads]
                doi = do_ref[pl.ds(start, tb), heads]
                dq = None
                for g in range(group):
                    dk_acc, dv_acc = carry[hh * group + g]
                    if kinds[g] != "none":
                        kg = k_ref[g * tb:(g + 1) * tb, heads]
                        st = lax.dot_general(kg, qi, NT, preferred_element_type=F32) * SOFTMAX_SCALE
                        pt = jnp.exp(st - lr_ref[hh, 0:1, pl.ds(start, tb)])
                        if kinds[g] == "diagonal":
                            pt = jnp.where(causal, pt, 0.0)
                        dv_acc = dv_acc + jnp.dot(pt.astype(BF16), doi, preferred_element_type=F32)
                        dpt = lax.dot_general(v_ref[g * tb:(g + 1) * tb, heads], doi, NT, preferred_element_type=F32)
                        dst = (pt * (dpt - dr_ref[hh, 0:1, pl.ds(start, tb)]) * SOFTMAX_SCALE).astype(BF16)
                        dk_acc = dk_acc + jnp.dot(dst, qi, preferred_element_type=F32)
                        term = lax.dot_general(dst, kg, TN, preferred_element_type=F32)
                        dq = term if dq is None else dq + term
                    out.append((dk_acc, dv_acc))
                dq_ref[pl.ds(start, tb), heads] += dq
            return tuple(out)

        one = (jnp.zeros((tb, HEAD_PAD), F32), jnp.zeros((tb, HEAD_PAD), F32))
        carry = (one,) * (2 * group)
        for d in range(group):
            kinds = tuple("full" if g < d else ("diagonal" if g == d else "none") for g in range(group))
            carry = step(j * group + d, carry, kinds)
        carry = lax.fori_loop((j + 1) * group, n_inner, functools.partial(step, kinds=("full",) * group), carry)
        for hh in range(2):
            for g in range(group):
                dk_ref[g * tb:(g + 1) * tb, hh * HEAD_PAD:(hh + 1) * HEAD_PAD] = carry[hh * group + g][0].astype(BF16)
                dv_ref[g * tb:(g + 1) * tb, hh * HEAD_PAD:(hh + 1) * HEAD_PAD] = carry[hh * group + g][1].astype(BF16)

        @pl.when(j == nk - 1)
        def _():
            dq_out_ref[...] = dq_ref[...].astype(BF16)

    t = n_seq * seq
    wide = N_HEADS * HEAD_PAD
    return pl.pallas_call(
        body, name="attention_bwd", grid=(n_seq, N_HEADS // 2, nk),
        in_specs=[pl.BlockSpec((seq, pair), lambda s, p, j: (s, p)),
                  pl.BlockSpec((rows, pair), lambda s, p, j: (s * nk + j, p)),
                  pl.BlockSpec((rows, pair), lambda s, p, j: (s * nk + j, p)),
                  pl.BlockSpec((seq, pair), lambda s, p, j: (s, p)),
                  pl.BlockSpec((None, 2, 8, seq), lambda s, p, j: (s, p, 0, 0)),
                  pl.BlockSpec((None, 2, 8, seq), lambda s, p, j: (s, p, 0, 0))],
        out_specs=[pl.BlockSpec((rows, pair), lambda s, p, j: (s * nk + j, p)),
                   pl.BlockSpec((rows, pair), lambda s, p, j: (s * nk + j, p)),
                   pl.BlockSpec((seq, pair), lambda s, p, j: (s, p))],
        out_shape=[jax.ShapeDtypeStruct((t, wide), BF16), jax.ShapeDtypeStruct((t, wide), BF16),
                   jax.ShapeDtypeStruct((t, wide), BF16)],
        scratch_shapes=[pltpu.VMEM((seq, pair), F32)],
        compiler_params=_params(3),
    )(q, k, v, do, lse_row, delta_row)


def _head_tail(o, gb, x1, w_out, final_norm, target, n_seq, seq):
    tm = HEAD_TAIL_TILE
    nt = seq // tm
    n_col = D_MODEL // N_DEV

    head_of_column = np.arange(E_B) // V_HEAD
    selector = jnp.asarray((head_of_column[None, :] == np.arange(N_HEADS)[:, None]).astype(np.float32))

    def body(o_ref, gb_ref, x1_ref, wout_ref, fn_ref, tgt_ref, sel_ref,
             dx2_ref, do_ref, dgb_ref, dr_ref, dwout_ref, dfn_ref, loss_ref):
        first = jnp.logical_and(pl.program_id(0) == 0, pl.program_id(1) == 0)

        @pl.when(first)
        def _():
            dwout_ref[...] = jnp.zeros_like(dwout_ref)
            dfn_ref[...] = jnp.zeros_like(dfn_ref)
            loss_ref[...] = jnp.zeros_like(loss_ref)

        ov, g = o_ref[...], gb_ref[...]
        silu, dsilu = _silu_parts(g)
        gated = (ov * silu).astype(BF16)
        x2 = x1_ref[...] + jnp.dot(gated, wout_ref[...], preferred_element_type=F32)
        r = _rstd(x2)
        xh = x2 * r
        err = xh * fn_ref[...] - tgt_ref[...]
        loss_ref[...] += 0.5 * jnp.sum(jnp.mean(err * err, axis=-1, keepdims=True), axis=0, keepdims=True)
        dy = err / D_MODEL
        dfn_ref[...] += jnp.sum(dy * xh, axis=0, keepdims=True)
        dx2 = _norm_bwd(dy * fn_ref[...], xh, r)
        dx2_ref[...] = dx2
        dx2b = dx2.astype(BF16)
        dw = lax.dot_general(gated, dx2b, TN, preferred_element_type=F32)
        for d in range(N_DEV):
            dwout_ref[d] += dw[:, d * n_col:(d + 1) * n_col]
        dgated = lax.dot_general(dx2b, wout_ref[...], NT, preferred_element_type=F32)
        do = dgated * silu
        dgb_ref[...] = (dgated * ov * dsilu).astype(BF16)
        delta = lax.dot_general(sel_ref[...], do * ov, NT, precision=lax.Precision.HIGHEST, preferred_element_type=F32)
        lane = lax.broadcasted_iota(jnp.int32, (tm, HEAD_PAD), 1)
        low = lane < V_HEAD
        for p in range(N_HEADS // 2):
            do_pair = do[:, p * HEAD_PAD:(p + 1) * HEAD_PAD]
            for hh in range(2):
                h = 2 * p + hh
                mine = do_pair if hh == 0 else pltpu.roll(do_pair, V_HEAD, 1)
                do_ref[:, h * HEAD_PAD:(h + 1) * HEAD_PAD] = jnp.where(low, mine, 0.0).astype(BF16)
                dr_ref[h] = jnp.broadcast_to(delta[h:h + 1, :], (8, tm))

    tok = lambda width: pl.BlockSpec((tm, width), lambda s, i: (s * nt + i, 0))
    t = n_seq * seq
    return pl.pallas_call(
        body, name="head_tail", grid=(n_seq, nt),
        in_specs=[tok(E_B), tok(E_B), tok(D_MODEL), _const_spec(w_out.shape), _const_spec((1, D_MODEL)), tok(D_MODEL),
                  _const_spec((N_HEADS, E_B))],
        out_specs=[tok(D_MODEL), tok(N_HEADS * HEAD_PAD), tok(E_B),
                   pl.BlockSpec((None, N_HEADS, 8, tm), lambda s, i: (s, 0, 0, i)),
                   _acc_spec((N_DEV, E_B, n_col)), _acc_spec((1, D_MODEL)), _acc_spec((1, 1))],
        out_shape=[jax.ShapeDtypeStruct((t, D_MODEL), F32), jax.ShapeDtypeStruct((t, N_HEADS * HEAD_PAD), BF16),
                   jax.ShapeDtypeStruct((t, E_B), BF16),
                   jax.ShapeDtypeStruct((n_seq, N_HEADS, 8, seq), F32), jax.ShapeDtypeStruct((N_DEV, E_B, n_col), F32),
                   jax.ShapeDtypeStruct((1, D_MODEL), F32), jax.ShapeDtypeStruct((1, 1), F32)],
        compiler_params=_params(2),
    )(o, gb, x1, w_out, final_norm, target, selector)


def _adamw_math(w, g, m, v):
    m = ADAM_B1 * m + (1.0 - ADAM_B1) * g
    v = ADAM_B2 * v + (1.0 - ADAM_B2) * jnp.square(g)
    m_hat = m / (1.0 - ADAM_B1 ** ADAM_STEP)
    v_hat = v / (1.0 - ADAM_B2 ** ADAM_STEP)
    delta = -ADAM_LR * (m_hat / (jnp.sqrt(v_hat) + ADAM_EPS) + ADAM_WD * w)
    return delta, m, v


def _adamw_reduce(own, parts, w, m, v):
    rows, cols = w.shape
    br = 256
    n_parts = parts.shape[0]

    def body(own_ref, p_ref, w_ref, m_ref, v_ref, g_ref, d_ref, nm_ref, nv_ref):
        g = own_ref[...]
        for k in range(n_parts):
            g = g + p_ref[k].astype(F32)
        g_ref[...] = g
        d_ref[...], nm_ref[...], nv_ref[...] = _adamw_math(w_ref[...], g, m_ref[...], v_ref[...])

    blk = pl.BlockSpec((br, cols), lambda i: (i, 0))
    return pl.pallas_call(
        body, name="adamw_reduce", grid=(rows // br,),
        in_specs=[blk, pl.BlockSpec((n_parts, br, cols), lambda i: (0, i, 0)), blk, blk, blk],
        out_specs=[blk] * 4, out_shape=[jax.ShapeDtypeStruct((rows, cols), F32)] * 4,
        compiler_params=_params(1),
    )(own, parts, w, m, v)


def _adamw_reduce_many(parts_list, states):
    n = len(parts_list)

    def body(*refs):
        ins, outs = refs[:4 * n], refs[4 * n:]
        for j in range(n):
            p_ref, w_ref, m_ref, v_ref = ins[4 * j:4 * j + 4]
            g = p_ref[0].astype(F32)
            for k in range(1, p_ref.shape[0]):
                g = g + p_ref[k].astype(F32)
            outs[4 * j][...] = g
            outs[4 * j + 1][...], outs[4 * j + 2][...], outs[4 * j + 3][...] = _adamw_math(
                w_ref[...], g, m_ref[...], v_ref[...])

    flat_in = [a for p, st in zip(parts_list, states) for a in (p,) + tuple(st)]
    shapes = [jax.ShapeDtypeStruct(st[0].shape, F32) for st in states for _ in range(4)]
    flat = pl.pallas_call(body, name="adamw_reduce_many", out_shape=shapes,
                          compiler_params=pltpu.CompilerParams(vmem_limit_bytes=VMEM_LIMIT))(*flat_in)
    return [tuple(flat[4 * j:4 * j + 4]) for j in range(n)]


def _sum_parts(parts):
    def body(p_ref, o_ref):
        g = p_ref[0]
        for k in range(1, N_DEV):
            g = g + p_ref[k]
        o_ref[...] = g

    return pl.pallas_call(body, name="small_grad_sum", out_shape=jax.ShapeDtypeStruct(parts.shape[1:], F32))(parts)


def _adamw_small(gs, ws, ms, vs):
    n = len(gs)

    def body(*refs):
        ins, outs = refs[:4 * n], refs[4 * n:]
        for j in range(n):
            g_ref, w_ref, m_ref, v_ref = ins[j], ins[n + j], ins[2 * n + j], ins[3 * n + j]
            outs[3 * j][...], outs[3 * j + 1][...], outs[3 * j + 2][...] = _adamw_math(
                w_ref[...], g_ref[...], m_ref[...], v_ref[...])

    shapes = [jax.ShapeDtypeStruct(g.shape, F32) for g in gs for _ in range(3)]
    flat = pl.pallas_call(body, name="adamw_small", out_shape=shapes)(*gs, *ws, *ms, *vs)
    return [tuple(flat[3 * j:3 * j + 3]) for j in range(n)]


SMALL_ROWS = 8
LOSS_LANE = D_MODEL - 1


def _pack_small(g_fn, g_kvn, g_bn, g_an, g_ckvn, g_qn, g_conv, loss_part):
    def body(fn_ref, kvn_ref, bn_ref, an_ref, ckvn_ref, qn_ref, conv_ref, loss_ref, o_ref):
        o_ref[0:1, :] = fn_ref[...]
        o_ref[1:2, :] = kvn_ref[...]
        o_ref[2:3, :] = bn_ref[...]
        o_ref[3:4, :] = an_ref[...]
        lane = lax.broadcasted_iota(jnp.int32, (1, D_MODEL), 1)
        o_ref[4:5, :] = jnp.where(lane == LOSS_LANE, loss_ref[...], 0.0)
        o_ref[4:5, 0:KV_RANK] = ckvn_ref[...]
        o_ref[4:5, KV_RANK:KV_RANK + Q_RANK] = qn_ref[...]
        o_ref[5:8, :] = conv_ref[0:3, :]

    return pl.pallas_call(body, name="pack_small", out_shape=jax.ShapeDtypeStruct((SMALL_ROWS, D_MODEL), F32))(
        g_fn, g_kvn, g_bn, g_an, g_ckvn, g_qn, g_conv, loss_part)


def _pad_cols(a, width):
    return jnp.pad(a, ((0, 0), (0, width - a.shape[1])))


def _dkv_to_padded(a):
    r = a.shape[0]
    z = jnp.zeros((r, ROPE_LO), a.dtype)
    z2 = jnp.zeros((r, HEAD_PAD - ROPE_LO - QK_ROPE), a.dtype)
    return jnp.concatenate([a[:, :KV_RANK], z, a[:, KV_RANK:], z2], axis=1)


def _dkv_from_padded(a):
    return jnp.concatenate([a[:, :KV_RANK], a[:, KV_RANK + ROPE_LO:KV_RANK + ROPE_LO + QK_ROPE]], axis=1)


def _unstack_cols(a):
    return jnp.transpose(a, (1, 0, 2)).reshape(a.shape[1], N_DEV * a.shape[2])


def kernel(x, positions, a_norm, a_w_in, a_conv, a_w_out, kv_norm, w_dkv, ckv_norm, w_ukv, b_norm, b_w_in, b_q_norm, b_w_uq, b_w_out, final_norm, loss_target, m_a_norm, m_a_w_in, m_a_conv, m_a_w_out, m_kv_norm, m_w_dkv, m_ckv_norm, m_w_ukv, m_b_norm, m_b_w_in, m_b_q_norm, m_b_w_uq, m_b_w_out, m_final_norm, v_a_norm, v_a_w_in, v_a_conv, v_a_w_out, v_kv_norm, v_w_dkv, v_ckv_norm, v_w_ukv, v_b_norm, v_b_w_in, v_b_q_norm, v_b_w_uq, v_b_w_out, v_final_norm):
    n_seq, seq, _ = x.shape
    t = n_seq * seq
    me = 4 * lax.axis_index("x") + 2 * lax.axis_index("y") + lax.axis_index("c")

    big = {
        "a_w_in": (a_w_in[0], m_a_w_in[0], v_a_w_in[0]),
        "a_w_out": (a_w_out[0], m_a_w_out[0], v_a_w_out[0]),
        "w_dkv": tuple(_dkv_to_padded(a) for a in (w_dkv, m_w_dkv, v_w_dkv)),
        "w_ukv": (w_ukv, m_w_ukv, v_w_ukv),
        "b_w_in": (b_w_in[0], m_b_w_in[0], v_b_w_in[0]),
        "b_w_uq": tuple(_pad_cols(a[0], HEAD_PAD) for a in (b_w_uq, m_b_w_uq, v_b_w_uq)),
        "b_w_out": (b_w_out[0], m_b_w_out[0], v_b_w_out[0]),
    }
    names = list(big)
    first_names, later_names = names[:2], names[2:]
    gathered = _all_gather([big[n][0].astype(BF16) for n in first_names] + [a_norm, a_conv[0]])
    a_norm_f = gathered[2].reshape(1, D_MODEL)
    a_conv_f = _unstack_cols(gathered[3])
    w_a_in = gathered[0]
    w_a_out = gathered[1].reshape(D_MODEL, D_MODEL)

    x2d = x.reshape(t, D_MODEL)
    tgt = loss_target.reshape(t, D_MODEL)
    pos = positions.astype(F32).reshape(t, 1)
    rc = _rope_consts()
    kvn, ckvn, bn, qn, fn = kv_norm.reshape(1, -1), ckv_norm.reshape(1, -1), b_norm, b_q_norm, final_norm.reshape(1, -1)

    x1, proj, conv, later = _conv_fwd(x2d, a_norm_f, w_a_in, a_conv_f, w_a_out, n_seq, seq,
                                      [big[n][0].astype(BF16) for n in later_names])
    full = dict(zip(later_names, later))
    w_dkv_f = full["w_dkv"].reshape(D_MODEL, CKR_PAD)
    w_ukv_f = _unstack_cols(full["w_ukv"])
    w_b_in = full["b_w_in"].reshape(D_MODEL, Q_RANK + E_B)
    w_uq_f = _unstack_cols(full["b_w_uq"])
    w_b_out = _unstack_cols(full["b_w_out"])
    layer_b = (kvn, w_dkv_f, ckvn, w_ukv_f, bn, w_b_in, qn, w_uq_f)
    ckr, cq, gb, q, k, v, cosv, sinv = _proj_fwd(x1, pos, rc, *layer_b)
    o, lse_row = _attn_fwd(q, k, v, n_seq, seq)
    dx2, do, dgb, delta_row, g_b_out, g_fn, loss_part = _head_tail(o, gb, x1, w_b_out, fn, tgt, n_seq, seq)
    dk, dv, dq = _attn_bwd(q, k, v, do, lse_row, delta_row, n_seq, seq)
    dx1, g_uq, g_b_in, g_ukv, g_dkv, g_qn, g_bn, g_ckvn, g_kvn = _proj_bwd(
        dq, dk, dv, dgb, cq, ckr, x1, dx2, cosv, sinv, rc, *layer_b)
    stacks = {
        "w_dkv": g_dkv.reshape(N_DEV, D_MODEL // N_DEV, CKR_PAD),
        "w_ukv": g_ukv,
        "b_w_in": g_b_in.reshape(N_DEV, D_MODEL // N_DEV, Q_RANK + E_B),
        "b_w_uq": g_uq,
        "b_w_out": g_b_out,
    }
    (dx, h_a, dproj, g_a_out, g_an, g_conv), later_parts = _conv_bwd(
        dx1, x2d, proj, conv, a_norm_f, w_a_in, a_conv_f, w_a_out, n_seq, seq, [stacks[n] for n in later_names])
    parts = dict(zip(later_names, later_parts))
    small = _pack_small(g_fn, g_kvn, g_bn, g_an, g_ckvn, g_qn, g_conv, loss_part)
    a_in_own, a_in_got, (parts["a_w_out"],), small_parts = _w_in_grad_exchange(
        h_a, dproj, D_MODEL * 4 // N_DEV, (me // 2).reshape(1).astype(jnp.int32),
        [g_a_out.astype(BF16).reshape(N_DEV, D_MODEL // N_DEV, D_MODEL)], small)

    outs = {"a_w_in": _adamw_reduce(a_in_own, a_in_got, *big["a_w_in"])}
    outs.update(zip(names[1:], _adamw_reduce_many([parts[n] for n in names[1:]], [big[n] for n in names[1:]])))
    outs["w_dkv"] = tuple(_dkv_from_padded(a) for a in outs["w_dkv"])
    outs["b_w_uq"] = tuple(a[:, :QK_NOPE + QK_ROPE] for a in outs["b_w_uq"])
    for n in ("a_w_in", "a_w_out", "b_w_in", "b_w_uq", "b_w_out"):
        outs[n] = tuple(a[None] for a in outs[n])

    total = _sum_parts(small_parts)
    loss = total[4, LOSS_LANE]
    shard = D_MODEL // N_DEV
    g_small = {
        "final_norm": total[0], "kv_norm": total[1], "b_norm": total[2:3],
        "a_norm": lax.dynamic_slice_in_dim(total[3:4], me * shard, shard, axis=1),
        "ckv_norm": total[4, 0:KV_RANK], "b_q_norm": total[4:5, KV_RANK:KV_RANK + Q_RANK],
        "a_conv": lax.dynamic_slice_in_dim(total[5:8], me * shard, shard, axis=1)[None],
    }
    small_state = {
        "final_norm": (final_norm, m_final_norm, v_final_norm), "kv_norm": (kv_norm, m_kv_norm, v_kv_norm),
        "b_norm": (b_norm, m_b_norm, v_b_norm), "a_norm": (a_norm, m_a_norm, v_a_norm),
        "ckv_norm": (ckv_norm, m_ckv_norm, v_ckv_norm), "b_q_norm": (b_q_norm, m_b_q_norm, v_b_q_norm),
        "a_conv": (a_conv, m_a_conv, v_a_conv),
    }
    small_names = list(g_small)
    as2d = lambda a: a.reshape(-1, a.shape[-1])
    upd = _adamw_small([as2d(g_small[n]) for n in small_names],
                       *[[as2d(small_state[n][j]) for n in small_names] for j in range(3)])
    for n, u in zip(small_names, upd):
        outs[n] = (g_small[n],) + tuple(a.reshape(g_small[n].shape) for a in u)

    order = ["a_norm", "a_w_in", "a_conv", "a_w_out", "kv_norm", "w_dkv", "ckv_norm", "w_ukv", "b_norm", "b_w_in",
             "b_q_norm", "b_w_uq", "b_w_out", "final_norm"]
    result = [loss, dx.reshape(n_seq, seq, D_MODEL)]
    for j in range(4):
        result += [outs[n][j] for n in order]
    return tuple(result)
```

```python
import functools
import math

import numpy as np
import jax
import jax.numpy as jnp
from jax import lax
from jax.experimental import pallas as pl
from jax.experimental.pallas import tpu as pltpu

F32 = jnp.float32
BF16 = jnp.bfloat16
SAVED = BF16

D_MODEL = 1024
N_HEADS = 8
QK_NOPE = 64
QK_ROPE = 32
V_HEAD = 64
KV_RANK = 256
Q_RANK = 384
E_B = N_HEADS * V_HEAD
HEAD_PAD = 128
CKR_PAD = KV_RANK + HEAD_PAD
ROPE_LO = QK_NOPE
ROPE_HALF = QK_ROPE // 2
ROPE_THETA = 10000.0
SOFTMAX_SCALE = 1.0 / math.sqrt(QK_NOPE + QK_ROPE)
LOG2_E = math.log2(math.e)
LN_2 = math.log(2.0)
EPS = 1e-6
N_DEV = 8

ADAM_LR = 0.001
ADAM_B1 = 0.9
ADAM_B2 = 0.999
ADAM_EPS = 1e-08
ADAM_WD = 0.01
ADAM_STEP = 10

CONV_BWD_TILE = 256
CONV_FWD_TILE = 512
LATENT_BWD_TILE = 512
LATENT_FWD_TILE = 1024
HEAD_TAIL_TILE = 1024
PICKUP_BLOCK = 1
ATTN_TILE = 512
ATTN_GROUP = 4
ATTN_BWD_GROUP = 4
VMEM_LIMIT = 60 * 1024 * 1024

MESH = pl.DeviceIdType.MESH
NT = (((1,), (1,)), ((), ()))
TN = (((0,), (0,)), ((), ()))


def _dot(a, b):
    return jnp.dot(a.astype(BF16), b.astype(BF16), preferred_element_type=F32)


def _dot_nt(a, b):
    return lax.dot_general(a.astype(BF16), b.astype(BF16), NT, preferred_element_type=F32)


def _dot_tn(a, b):
    return lax.dot_general(a.astype(BF16), b.astype(BF16), TN, preferred_element_type=F32)


def _rstd(x):
    return lax.rsqrt(jnp.mean(x * x, axis=-1, keepdims=True) + EPS)


def _norm_bwd(a, xh, r):
    return r * (a - xh * jnp.mean(a * xh, axis=-1, keepdims=True))


def _silu_parts(g):
    sg = jax.nn.sigmoid(g)
    return g * sg, sg * (1.0 + g * (1.0 - sg))


def _rope_consts():
    inv = (ROPE_THETA ** (-np.arange(0, QK_ROPE, 2, dtype=np.float32) / QK_ROPE)).astype(np.float32)
    t = np.zeros((8, HEAD_PAD), np.float32)
    t[0, ROPE_LO:ROPE_LO + ROPE_HALF] = inv
    t[0, ROPE_LO + ROPE_HALF:ROPE_LO + QK_ROPE] = inv
    t[1, ROPE_LO:ROPE_LO + ROPE_HALF] = -1.0
    t[2, ROPE_LO + ROPE_HALF:ROPE_LO + QK_ROPE] = 1.0
    return jnp.asarray(t)


def _rope_angles(pos, rc):
    ang = pos * rc[0:1, :]
    return jnp.cos(ang), jnp.sin(ang)


def _rope_tables(cosv, sinv, rc):
    return cosv, sinv * rc[1:2, :], sinv * rc[2:3, :]


def _rope(x, ct, s1, s2):
    up = pltpu.roll(x, HEAD_PAD - ROPE_HALF, 1)
    dn = pltpu.roll(x, ROPE_HALF, 1)
    return x * ct + up * s1 + dn * s2


def _const_spec(shape):
    nd = len(shape)
    return pl.BlockSpec(shape, lambda *_: (0,) * nd, pipeline_mode=pl.Buffered(1))


def _acc_spec(shape):
    nd = len(shape)
    return pl.BlockSpec(shape, lambda *_: (0,) * nd)


def _params(n_axes):
    return pltpu.CompilerParams(dimension_semantics=("arbitrary",) * n_axes, vmem_limit_bytes=VMEM_LIMIT)


def _place():
    x, y, c = lax.axis_index("x"), lax.axis_index("y"), lax.axis_index("c")
    return x, y, c, 4 * x + 2 * y + c


def _peer(x, y, c, mask):
    px = 1 - x if mask & 4 else x
    py = 1 - y if mask & 2 else y
    pc = 1 - c if mask & 1 else c
    return (px, py, pc), 4 * px + 2 * py + pc


ANY_SPEC = pl.BlockSpec(memory_space=pl.ANY)


def _comm_sems(n):
    return [pltpu.SemaphoreType.DMA((n, N_DEV - 1)), pltpu.SemaphoreType.DMA((n, N_DEV - 1)), pltpu.SemaphoreType.DMA((n,))]


def _gather_copies(ins, outs, sems):
    send_sems, recv_sems, local_sems = sems
    x, y, c, me = _place()
    starts, waits = [], []
    for w in range(len(ins)):
        mine = pltpu.make_async_copy(ins[w], outs[w].at[me], local_sems.at[w])
        starts.append(mine)
        waits.append(mine)
        for mask in range(1, N_DEV):
            peer, peer_idx = _peer(x, y, c, mask)
            starts.append(pltpu.make_async_remote_copy(
                src_ref=ins[w], dst_ref=outs[w].at[me], send_sem=send_sems.at[w, mask - 1],
                recv_sem=recv_sems.at[w, mask - 1], device_id=peer, device_id_type=MESH))
            waits.append(pltpu.make_async_remote_copy(
                src_ref=ins[w], dst_ref=outs[w].at[peer_idx], send_sem=send_sems.at[w, mask - 1],
                recv_sem=recv_sems.at[w, mask - 1], device_id=peer, device_id_type=MESH))
    return starts, waits


def _scatter_copies(ins, outs, sems):
    send_sems, recv_sems, local_sems = sems
    x, y, c, me = _place()
    copies = []
    for w in range(len(ins)):
        copies.append(pltpu.make_async_copy(ins[w].at[me], outs[w].at[0], local_sems.at[w]))
        for mask in range(1, N_DEV):
            peer, peer_idx = _peer(x, y, c, mask)
            copies.append(pltpu.make_async_remote_copy(
                src_ref=ins[w].at[peer_idx], dst_ref=outs[w].at[mask], send_sem=send_sems.at[w, mask - 1],
                recv_sem=recv_sems.at[w, mask - 1], device_id=peer, device_id_type=MESH))
    return copies, copies


def _stacked(arrays):
    return [jax.ShapeDtypeStruct((N_DEV,) + a.shape, a.dtype) for a in arrays]


def _all_gather(shards):
    n = len(shards)

    def body(*refs):
        ins, outs = refs[:n], refs[n:2 * n]
        send_sems, recv_sems, local_sems = refs[2 * n:]
        x, y, c, me = _place()
        sibling = (x, y, 1 - c)
        chips = [(1 - x, y), (x, 1 - y), (1 - x, 1 - y)]

        def copy(w, k, block, to, src=None):
            idx = 4 * block[0] + 2 * block[1] + block[2]
            return pltpu.make_async_remote_copy(
                src_ref=outs[w].at[idx] if src is None else src, dst_ref=outs[w].at[idx],
                send_sem=send_sems.at[w, k], recv_sem=recv_sems.at[w, k], device_id=to, device_id_type=MESH)

        local, sent = [], []
        for w in range(n):
            mine = pltpu.make_async_copy(ins[w], outs[w].at[me], local_sems.at[w])
            mine.start()
            local.append(mine)
            first = [copy(w, 0, (x, y, c), sibling, src=ins[w])]
            first += [copy(w, 1 + j, (x, y, c), (*chip, c), src=ins[w]) for j, chip in enumerate(chips)]
            for cp in first:
                cp.start()
            sent += first
        for w in range(n):
            for j, chip in enumerate(chips):
                copy(w, 1 + j, (*chip, c), (x, y, c)).wait_recv()
                onward = copy(w, 4 + j, (*chip, c), sibling)
                onward.start()
                sent.append(onward)
        for w in range(n):
            copy(w, 0, sibling, (x, y, c)).wait_recv()
            for j, chip in enumerate(chips):
                copy(w, 4 + j, (*chip, 1 - c), (x, y, c)).wait_recv()
        for cp in sent:
            cp.wait_send()
        for cp in local:
            cp.wait()

    return pl.pallas_call(
        body, name="weight_all_gather", out_shape=_stacked(shards),
        in_specs=[ANY_SPEC] * n, out_specs=[ANY_SPEC] * n, scratch_shapes=_comm_sems(n),
    )(*shards)


def _conv_fwd(x, a_norm, w_in, conv_w, w_out, n_seq, seq, later_shards):
    tm = CONV_FWD_TILE
    nt = seq // tm
    n_col = w_in.shape[2]
    n_later = len(later_shards)

    def body(x_ref, an_ref, win_ref, cw_ref, wout_ref, *rest):
        shard_refs, rest = rest[:n_later], rest[n_later:]
        x1_ref, proj_ref, conv_ref = rest[:3]
        stack_refs, rest = rest[3:3 + n_later], rest[3 + n_later:]
        prev_ref, wide_ref, sems = rest[0], rest[1], rest[2:]
        step = pl.program_id(0) * nt + pl.program_id(1)

        @pl.when(step == 0)
        def _():
            for cp in _gather_copies(shard_refs, stack_refs, sems)[0]:
                cp.start()

        @pl.when(pl.program_id(1) == 0)
        def _():
            prev_ref[...] = jnp.zeros_like(prev_ref)

        xv = x_ref[...]
        h = (xv * _rstd(xv) * an_ref[...]).astype(BF16)
        for d in range(N_DEV):
            part = jnp.dot(h, win_ref[d], preferred_element_type=F32)
            wide_ref[:, d * n_col:(d + 1) * n_col] = part
            proj_ref[:, d * n_col:(d + 1) * n_col] = part.astype(SAVED)
        b = wide_ref[:, 0:D_MODEL]
        v = wide_ref[:, D_MODEL:2 * D_MODEL] * wide_ref[:, 2 * D_MODEL:3 * D_MODEL]
        g = wide_ref[:, 3 * D_MODEL:4 * D_MODEL]
        w0, w1, w2 = cw_ref[0:1, :], cw_ref[1:2, :], cw_ref[2:3, :]
        conv_ref[...] = w0 * pltpu.roll(v, 2, 0) + w1 * pltpu.roll(v, 1, 0) + w2 * v
        rows = lax.broadcasted_iota(jnp.int32, (8, D_MODEL), 0)
        p8, v8 = prev_ref[...], v[0:8]
        back1 = jnp.where(rows < 1, pltpu.roll(p8, 1, 0), pltpu.roll(v8, 1, 0))
        back2 = jnp.where(rows < 2, pltpu.roll(p8, 2, 0), pltpu.roll(v8, 2, 0))
        conv_ref[0:8, :] = w0 * back2 + w1 * back1 + w2 * v8
        prev_ref[...] = v[tm - 8:tm]
        silu, _ = _silu_parts(g)
        yv = silu * b * conv_ref[...]
        x1_ref[...] = xv + _dot(yv, wout_ref[...])

        @pl.when(step == n_seq * nt - 1)
        def _():
            for cp in _gather_copies(shard_refs, stack_refs, sems)[1]:
                cp.wait()

    tok = lambda width: pl.BlockSpec((tm, width), lambda s, i: (s * nt + i, 0))
    t = n_seq * seq
    outs = pl.pallas_call(
        body, name="conv_mixer_fwd", grid=(n_seq, nt),
        in_specs=[tok(D_MODEL), _const_spec((1, D_MODEL)), _const_spec(w_in.shape), _const_spec((3, D_MODEL)),
                  _const_spec(w_out.shape)] + [ANY_SPEC] * n_later,
        out_specs=[tok(D_MODEL), tok(4 * D_MODEL), tok(D_MODEL)] + [ANY_SPEC] * n_later,
        out_shape=[jax.ShapeDtypeStruct((t, D_MODEL), F32), jax.ShapeDtypeStruct((t, 4 * D_MODEL), SAVED),
                   jax.ShapeDtypeStruct((t, D_MODEL), F32)] + _stacked(later_shards),
        scratch_shapes=[pltpu.VMEM((8, D_MODEL), F32), pltpu.VMEM((tm, 4 * D_MODEL), F32)] + _comm_sems(n_later),
        compiler_params=_params(2),
    )(x, a_norm, w_in, conv_w, w_out, *later_shards)
    return outs[0], outs[1], outs[2], outs[3:]


def _conv_bwd(dx1, x, proj, conv, a_norm, w_in, conv_w, w_out, n_seq, seq, ready_stacks):
    tm = CONV_BWD_TILE
    nt = seq // tm
    n_col = w_in.shape[2]
    n_ready = len(ready_stacks)

    def body(dx1_ref, x_ref, proj_ref, conv_ref, an_ref, win_ref, cw_ref, wout_ref, *rest):
        ready_refs, rest = rest[:n_ready], rest[n_ready:]
        dx_ref, h_ref, dproj_ref, dwout_ref, dan_ref, dcw_ref = rest[:6]
        part_refs, rest = rest[6:6 + n_ready], rest[6 + n_ready:]
        next_ref, d1_ref, d2_ref = rest[:3]
        sems = rest[3:]
        step = pl.program_id(0) * nt + pl.program_id(1)
        first = step == 0

        @pl.when(first)
        def _():
            for cp in _scatter_copies(ready_refs, part_refs, sems)[0]:
                cp.start()
            dwout_ref[...] = jnp.zeros_like(dwout_ref)
            dan_ref[...] = jnp.zeros_like(dan_ref)
            dcw_ref[...] = jnp.zeros_like(dcw_ref)

        @pl.when(pl.program_id(1) == 0)
        def _():
            next_ref[...] = jnp.zeros_like(next_ref)

        dx1v = dx1_ref[...]
        dy = _dot_nt(dx1v, wout_ref[...])
        b = proj_ref[:, 0:D_MODEL].astype(F32)
        cc = proj_ref[:, D_MODEL:2 * D_MODEL].astype(F32)
        u = proj_ref[:, 2 * D_MODEL:3 * D_MODEL].astype(F32)
        g = proj_ref[:, 3 * D_MODEL:4 * D_MODEL].astype(F32)
        cv = conv_ref[...]
        silu, dsilu = _silu_parts(g)
        per_part = D_MODEL // n_col

        def back_through_w_in(part, grad):
            grad = grad.astype(BF16)
            dproj_ref[:, part * D_MODEL:(part + 1) * D_MODEL] = grad
            term = None
            for j in range(per_part):
                d = part * per_part + j
                piece = lax.dot_general(grad[:, j * n_col:(j + 1) * n_col], win_ref[d], NT, preferred_element_type=F32)
                term = piece if term is None else term + piece
            return term

        dh = back_through_w_in(0, dy * silu * cv)
        dh += back_through_w_in(3, dy * b * cv * dsilu)
        dwout_ref[...] += _dot_tn(silu * b * cv, dx1v)
        dconv = dy * silu * b
        d1_ref[...] = pltpu.roll(dconv, tm - 1, 0)
        d2_ref[...] = pltpu.roll(dconv, tm - 2, 0)
        rows = lax.broadcasted_iota(jnp.int32, (8, D_MODEL), 0)
        n8, c8 = next_ref[...], dconv[tm - 8:tm]
        d1_ref[tm - 8:tm, :] = jnp.where(rows >= 7, pltpu.roll(n8, 7, 0), pltpu.roll(c8, 7, 0))
        d2_ref[tm - 8:tm, :] = jnp.where(rows >= 6, pltpu.roll(n8, 6, 0), pltpu.roll(c8, 6, 0))
        next_ref[...] = dconv[0:8]
        d1, d2 = d1_ref[...], d2_ref[...]
        v = cc * u
        dcw_ref[0:1, :] += jnp.sum(d2 * v, axis=0, keepdims=True)
        dcw_ref[1:2, :] += jnp.sum(d1 * v, axis=0, keepdims=True)
        dcw_ref[2:3, :] += jnp.sum(dconv * v, axis=0, keepdims=True)
        dv = cw_ref[0:1, :] * d2 + cw_ref[1:2, :] * d1 + cw_ref[2:3, :] * dconv
        dh += back_through_w_in(1, dv * u)
        dh += back_through_w_in(2, dv * cc)
        xv = x_ref[...]
        r = _rstd(xv)
        xh = xv * r
        h_ref[...] = (xh * an_ref[...]).T.astype(BF16)
        dan_ref[...] += jnp.sum(dh * xh, axis=0, keepdims=True)
        dx_ref[...] = dx1v + _norm_bwd(dh * an_ref[...], xh, r)

        @pl.when(step == n_seq * nt - 1)
        def _():
            for cp in _scatter_copies(ready_refs, part_refs, sems)[1]:
                cp.wait()

    tok = lambda width: pl.BlockSpec((tm, width), lambda s, i: (s * nt + nt - 1 - i, 0))
    t = n_seq * seq
    outs = pl.pallas_call(
        body, name="conv_mixer_bwd", grid=(n_seq, nt),
        in_specs=[tok(D_MODEL), tok(D_MODEL), tok(4 * D_MODEL), tok(D_MODEL), _const_spec((1, D_MODEL)),
                  _const_spec(w_in.shape), _const_spec((3, D_MODEL)), _const_spec(w_out.shape)] + [ANY_SPEC] * n_ready,
        out_specs=[tok(D_MODEL), pl.BlockSpec((D_MODEL, tm), lambda s, i: (0, s * nt + nt - 1 - i)),
                   tok(4 * D_MODEL), _acc_spec((D_MODEL, D_MODEL)),
                   _acc_spec((1, D_MODEL)), _acc_spec((8, D_MODEL))] + [ANY_SPEC] * n_ready,
        out_shape=[jax.ShapeDtypeStruct((t, D_MODEL), F32), jax.ShapeDtypeStruct((D_MODEL, t), BF16),
                   jax.ShapeDtypeStruct((t, 4 * D_MODEL), BF16), jax.ShapeDtypeStruct((D_MODEL, D_MODEL), F32),
                   jax.ShapeDtypeStruct((1, D_MODEL), F32), jax.ShapeDtypeStruct((8, D_MODEL), F32)]
        + [jax.ShapeDtypeStruct(a.shape, a.dtype) for a in ready_stacks],
        scratch_shapes=[pltpu.VMEM((8, D_MODEL), F32), pltpu.VMEM((tm, D_MODEL), F32), pltpu.VMEM((tm, D_MODEL), F32)]
        + _comm_sems(n_ready),
        compiler_params=_params(2),
    )(dx1, x, proj, conv, a_norm, w_in, conv_w, w_out, *ready_stacks)
    return outs[:6], outs[6:]


def _w_in_grad_exchange(a_t, b, n_col, chip, ready_stacks, small):
    r, t = a_t.shape
    bt = 4096
    nk = t // bt
    n_ready = len(ready_stacks)
    n_chip = N_DEV // 2
    n_remote = N_DEV - 2

    def body(chip_ref, a_ref, b_ref, *rest):
        ready_refs, small_ref, rest = rest[:n_ready], rest[n_ready], rest[n_ready + 1:]
        own_ref, got_ref, rest = rest[0], rest[1], rest[2:]
        part_refs, small_all_ref, rest = rest[:n_ready], rest[n_ready], rest[n_ready + 1:]
        acc_ref, land_ref, stage_ref, pair_send, pair_recv, chip_send, chip_recv = rest[:7]
        scatter_sems, gather_sems = rest[7:10], rest[10:13]
        s, k = pl.program_id(0), pl.program_id(1)
        x, y, c, _ = _place()
        chip = 2 * x + y

        @pl.when(jnp.logical_and(s == 0, k == 0))
        def _():
            for cp in _scatter_copies(ready_refs, part_refs, scatter_sems)[0]:
                cp.start()
            for cp in _gather_copies([small_ref], [small_all_ref], gather_sems)[0]:
                cp.start()

        for parity in range(2):
            @pl.when(s % 2 == parity)
            def _(parity=parity):
                @pl.when(k == 0)
                def _():
                    acc_ref[parity] = jnp.zeros((r, n_col), F32)

                acc_ref[parity] += jnp.dot(a_ref[...], b_ref[...], preferred_element_type=F32)

        def to_sibling(step):
            return pltpu.make_async_remote_copy(
                src_ref=acc_ref.at[step % 2], dst_ref=land_ref, send_sem=pair_send.at[step], recv_sem=pair_recv.at[step],
                device_id=(x, y, 1 - c), device_id_type=MESH)

        def to_owner(nth):
            owner_chip = (chip + 1 + nth) % n_chip
            slot = jnp.bitwise_xor(chip, owner_chip) - 1
            return pltpu.make_async_remote_copy(
                src_ref=stage_ref.at[nth % 2], dst_ref=got_ref.at[slot], send_sem=chip_send.at[nth],
                recv_sem=chip_recv.at[slot], device_id=(owner_chip // 2, owner_chip % 2, c), device_id_type=MESH)

        for step in range(N_DEV):
            owner_core = step % 2

            @pl.when(jnp.logical_and(s == step, k == nk - 1))
            def _(step=step, owner_core=owner_core):
                @pl.when(c != owner_core)
                def _():
                    to_sibling(step).start()

                if step >= 1:
                    @pl.when(c == owner_core)
                    def _():
                        to_sibling(step - 1).wait_send()

            if step < N_DEV - 1:
                pickup = jnp.logical_and(s == step + 1, k == min(PICKUP_BLOCK, nk - 1))
            else:
                pickup = jnp.logical_and(s == step, k == nk - 1)

            @pl.when(jnp.logical_and(pickup, c == owner_core))
            def _(step=step):
                to_sibling(step).wait_recv()
                total = acc_ref[step % 2] + land_ref[...]
                if step < n_remote:
                    nth = step // 2
                    if nth >= 2:
                        to_owner(nth - 2).wait_send()
                    stage_ref[nth % 2] = total.astype(BF16)
                    to_owner(nth).start()
                else:
                    own_ref[...] = total

        @pl.when(jnp.logical_and(s == N_DEV - 1, k == nk - 1))
        def _():
            @pl.when(c != (N_DEV - 1) % 2)
            def _():
                to_sibling(N_DEV - 1).wait_send()

            to_owner(1).wait_send()
            to_owner(2).wait_send()
            for slot in range(n_chip - 1):
                pltpu.make_async_remote_copy(
                    src_ref=stage_ref.at[0], dst_ref=got_ref.at[slot], send_sem=chip_send.at[0],
                    recv_sem=chip_recv.at[slot], device_id=(x, y, c), device_id_type=MESH).wait_recv()
            for cp in _scatter_copies(ready_refs, part_refs, scatter_sems)[1]:
                cp.wait()
            for cp in _gather_copies([small_ref], [small_all_ref], gather_sems)[1]:
                cp.wait()

    def owner_block(s, k, chip):
        return (k, 2 * ((chip[0] + 1 + s // 2) % n_chip) + s % 2)

    grid_spec = pltpu.PrefetchScalarGridSpec(
        num_scalar_prefetch=1, grid=(N_DEV, nk),
        in_specs=[pl.BlockSpec((r, bt), lambda s, k, chip: (0, k)), pl.BlockSpec((bt, n_col), owner_block)]
        + [ANY_SPEC] * (n_ready + 1),
        out_specs=[pl.BlockSpec((r, n_col), lambda s, k, chip: (0, 0)), ANY_SPEC] + [ANY_SPEC] * (n_ready + 1),
        scratch_shapes=[pltpu.VMEM((2, r, n_col), F32), pltpu.VMEM((r, n_col), F32), pltpu.VMEM((2, r, n_col), BF16),
                        pltpu.SemaphoreType.DMA((N_DEV,)), pltpu.SemaphoreType.DMA((N_DEV,)),
                        pltpu.SemaphoreType.DMA((n_chip - 1,)), pltpu.SemaphoreType.DMA((n_chip - 1,))]
        + _comm_sems(n_ready) + _comm_sems(1))
    outs = pl.pallas_call(
        body, name="w_in_grad_exchange", grid_spec=grid_spec,
        out_shape=[jax.ShapeDtypeStruct((r, n_col), F32), jax.ShapeDtypeStruct((n_chip - 1, r, n_col), BF16)]
        + [jax.ShapeDtypeStruct(p.shape, p.dtype) for p in ready_stacks] + _stacked([small]),
        compiler_params=_params(2),
    )(chip, a_t, b, *ready_stacks, small)
    return outs[0], outs[1], outs[2:2 + n_ready], outs[2 + n_ready]


def _proj_fwd(x1, pos, rc, kv_norm, w_dkv, ckv_norm, w_ukv, b_norm, b_w_in, q_norm, w_uq):
    t = x1.shape[0]
    tm = LATENT_FWD_TILE

    def body(x1_ref, pos_ref, rc_ref, kvn_ref, wdkv_ref, ckvn_ref, wukv_ref, bn_ref, bwin_ref, qn_ref, wuq_ref,
             ckr_ref, cq_ref, gb_ref, q_ref, k_ref, v_ref, cos_ref, sin_ref):
        xv = x1_ref[...]
        xh = xv * _rstd(xv)
        ckr = _dot(xh * kvn_ref[...], wdkv_ref[...])
        pb = _dot(xh * bn_ref[...], bwin_ref[...])
        ckr_ref[...] = ckr
        ckv = ckr[:, 0:KV_RANK]
        kv = _dot(ckv * _rstd(ckv) * ckvn_ref[...], wukv_ref[...])
        cosv, sinv = _rope_angles(pos_ref[...], rc_ref[...])
        cos_ref[...] = cosv
        sin_ref[...] = sinv
        ct, s1, s2 = _rope_tables(cosv, sinv, rc_ref[...])
        k_rope = _rope(ckr[:, KV_RANK:CKR_PAD], ct, s1, s2)
        lane = lax.broadcasted_iota(jnp.int32, (tm, HEAD_PAD), 1)
        low = lane < QK_NOPE
        for h in range(N_HEADS):
            kv_h = kv[:, h * HEAD_PAD:(h + 1) * HEAD_PAD]
            k_ref[:, h * HEAD_PAD:(h + 1) * HEAD_PAD] = jnp.where(low, kv_h, k_rope).astype(BF16)
            v_ref[:, h * HEAD_PAD:(h + 1) * HEAD_PAD] = jnp.where(low, pltpu.roll(kv_h, V_HEAD, 1), 1.0).astype(BF16)
        cq = pb[:, 0:Q_RANK]
        cq_ref[...] = cq
        gb_ref[...] = pb[:, Q_RANK:Q_RANK + E_B]
        q = _dot(cq * _rstd(cq) * qn_ref[...], wuq_ref[...])
        for h in range(N_HEADS):
            q_ref[:, h * HEAD_PAD:(h + 1) * HEAD_PAD] = _rope(
                q[:, h * HEAD_PAD:(h + 1) * HEAD_PAD], ct, s1, s2).astype(BF16)

    tok = lambda width: pl.BlockSpec((tm, width), lambda i: (i, 0))
    weights = [kv_norm, w_dkv, ckv_norm, w_ukv, b_norm, b_w_in, q_norm, w_uq]
    wide = N_HEADS * HEAD_PAD
    return pl.pallas_call(
        body, name="latent_proj_fwd", grid=(t // tm,),
        in_specs=[tok(D_MODEL), tok(1), _const_spec(rc.shape)] + [_const_spec(w.shape) for w in weights],
        out_specs=[tok(CKR_PAD), tok(Q_RANK), tok(E_B), tok(wide), tok(wide), tok(wide), tok(HEAD_PAD), tok(HEAD_PAD)],
        out_shape=[jax.ShapeDtypeStruct((t, CKR_PAD), F32), jax.ShapeDtypeStruct((t, Q_RANK), F32),
                   jax.ShapeDtypeStruct((t, E_B), F32), jax.ShapeDtypeStruct((t, wide), BF16),
                   jax.ShapeDtypeStruct((t, wide), BF16), jax.ShapeDtypeStruct((t, wide), BF16),
                   jax.ShapeDtypeStruct((t, HEAD_PAD), F32), jax.ShapeDtypeStruct((t, HEAD_PAD), F32)],
        compiler_params=_params(1),
    )(x1, pos, rc, *weights)


def _proj_bwd(dq, dk, dv, dgb, cq, ckr, x1, dx2, cosv, sinv, rc, kv_norm, w_dkv, ckv_norm, w_ukv, b_norm, b_w_in, q_norm, w_uq):
    t = x1.shape[0]
    tm = LATENT_BWD_TILE
    wide = N_HEADS * HEAD_PAD

    def body(dq_ref, dk_ref, dv_ref, dgb_ref, cq_ref, ckr_ref, x1_ref, dx2_ref, cos_ref, sin_ref, rc_ref,
             kvn_ref, wdkv_ref, ckvn_ref, wukv_ref, bn_ref, bwin_ref, qn_ref, wuq_ref,
             dx1_ref, dwuq_ref, dwbin_ref, dwukv_ref, dwdkv_ref, dqn_ref, dbn_ref, dckvn_ref, dkvn_ref, dqu_ref, dkv_ref):
        @pl.when(pl.program_id(0) == 0)
        def _():
            for ref in (dwuq_ref, dwbin_ref, dwukv_ref, dwdkv_ref, dqn_ref, dbn_ref, dckvn_ref, dkvn_ref):
                ref[...] = jnp.zeros_like(ref)

        ct, s1, s2 = _rope_tables(cos_ref[...], sin_ref[...], rc_ref[...])
        lane = lax.broadcasted_iota(jnp.int32, (tm, HEAD_PAD), 1)
        low = lane < QK_NOPE
        xv = x1_ref[...]
        r = _rstd(xv)
        xh = xv * r
        dk_rope = jnp.zeros((tm, HEAD_PAD), F32)
        for h in range(N_HEADS):
            dk_h = dk_ref[:, h * HEAD_PAD:(h + 1) * HEAD_PAD].astype(F32)
            dv_h = dv_ref[:, h * HEAD_PAD:(h + 1) * HEAD_PAD].astype(F32)
            dkv_ref[:, h * HEAD_PAD:(h + 1) * HEAD_PAD] = jnp.where(low, dk_h, pltpu.roll(dv_h, V_HEAD, 1)).astype(BF16)
            dk_rope += dk_h
        rope_lanes = jnp.logical_and(lane >= ROPE_LO, lane < ROPE_LO + QK_ROPE)
        dk_rope = jnp.where(rope_lanes, _rope(dk_rope, ct, -s1, -s2), 0.0)
        ckv = ckr_ref[:, 0:KV_RANK]
        rk = _rstd(ckv)
        ckh = ckv * rk
        ckvn = (ckh * ckvn_ref[...]).astype(BF16)
        dwukv = lax.dot_general(ckvn, dkv_ref[...], TN, preferred_element_type=F32)
        for h in range(N_HEADS):
            dwukv_ref[h] += dwukv[:, h * HEAD_PAD:(h + 1) * HEAD_PAD]
        dckvn = lax.dot_general(dkv_ref[...], wukv_ref[...], NT, preferred_element_type=F32)
        dckvn_ref[...] += jnp.sum(dckvn * ckh, axis=0, keepdims=True)
        dckv = _norm_bwd(dckvn * ckvn_ref[...], ckh, rk)
        dckr = jnp.concatenate([dckv, dk_rope], axis=1).astype(BF16)
        dwdkv_ref[...] += _dot_tn(xh * kvn_ref[...], dckr)
        dh2 = lax.dot_general(dckr, wdkv_ref[...], NT, preferred_element_type=F32)
        for h in range(N_HEADS):
            dqu_ref[:, h * HEAD_PAD:(h + 1) * HEAD_PAD] = _rope(
                dq_ref[:, h * HEAD_PAD:(h + 1) * HEAD_PAD].astype(F32), ct, -s1, -s2).astype(BF16)
        cq = cq_ref[...]
        rq = _rstd(cq)
        cqh = cq * rq
        cqn = (cqh * qn_ref[...]).astype(BF16)
        dwuq = lax.dot_general(cqn, dqu_ref[...], TN, preferred_element_type=F32)
        for h in range(N_HEADS):
            dwuq_ref[h] += dwuq[:, h * HEAD_PAD:(h + 1) * HEAD_PAD]
        dcqn = lax.dot_general(dqu_ref[...], wuq_ref[...], NT, preferred_element_type=F32)
        dqn_ref[...] += jnp.sum(dcqn * cqh, axis=0, keepdims=True)
        dcq = _norm_bwd(dcqn * qn_ref[...], cqh, rq)
        dpb = jnp.concatenate([dcq.astype(BF16), dgb_ref[...]], axis=1)
        dwbin_ref[...] += _dot_tn(xh * bn_ref[...], dpb)
        dh3 = lax.dot_general(dpb, bwin_ref[...], NT, preferred_element_type=F32)
        dkvn_ref[...] += jnp.sum(dh2 * xh, axis=0, keepdims=True)
        dbn_ref[...] += jnp.sum(dh3 * xh, axis=0, keepdims=True)
        dx1_ref[...] = dx2_ref[...] + _norm_bwd(dh2 * kvn_ref[...] + dh3 * bn_ref[...], xh, r)

    tok = lambda width: pl.BlockSpec((tm, width), lambda i: (i, 0))
    weights = [kv_norm, w_dkv, ckv_norm, w_ukv, b_norm, b_w_in, q_norm, w_uq]
    acc_shapes = [(N_HEADS, Q_RANK, HEAD_PAD), (D_MODEL, Q_RANK + E_B), (N_HEADS, KV_RANK, HEAD_PAD), (D_MODEL, CKR_PAD),
                  (1, Q_RANK), (1, D_MODEL), (1, KV_RANK), (1, D_MODEL)]
    return pl.pallas_call(
        body, name="latent_proj_bwd", grid=(t // tm,),
        in_specs=[tok(wide), tok(wide), tok(wide), tok(E_B), tok(Q_RANK), tok(CKR_PAD), tok(D_MODEL), tok(D_MODEL),
                  tok(HEAD_PAD), tok(HEAD_PAD), _const_spec(rc.shape)] + [_const_spec(w.shape) for w in weights],
        out_specs=[tok(D_MODEL)] + [_acc_spec(s) for s in acc_shapes],
        out_shape=[jax.ShapeDtypeStruct((t, D_MODEL), F32)] + [jax.ShapeDtypeStruct(s, F32) for s in acc_shapes],
        scratch_shapes=[pltpu.VMEM((tm, wide), BF16), pltpu.VMEM((tm, wide), BF16)],
        compiler_params=_params(1),
    )(dq, dk, dv, dgb, cq, ckr, x1, dx2, cosv, sinv, rc, *weights)


def _attn_fwd(q, k, v, n_seq, seq):
    tb, group = ATTN_TILE, ATTN_GROUP
    rows = tb * group
    nq = seq // rows
    pair = 2 * HEAD_PAD

    def body(q_ref, k_ref, v_ref, o_ref, lr_ref):
        i = pl.program_id(2)
        row = lax.broadcasted_iota(jnp.int32, (tb, tb), 0)
        col = lax.broadcasted_iota(jnp.int32, (tb, tb), 1)
        causal = col <= row

        def step(j, carry, kinds):
            start = pl.multiple_of(j * tb, tb)
            out = []
            for g in range(group):
                for hh in range(2):
                    m, acc = carry[2 * g + hh]
                    if kinds[g] != "none":
                        heads = slice(hh * HEAD_PAD, (hh + 1) * HEAD_PAD)
                        s = lax.dot_general(q_ref[g * tb:(g + 1) * tb, heads], k_ref[pl.ds(start, tb), heads], NT,
                                            preferred_element_type=F32) * (SOFTMAX_SCALE * LOG2_E)
                        if kinds[g] == "diagonal":
                            s = jnp.where(causal, s, -jnp.inf)
                        m_new = jnp.maximum(m, jnp.max(s, axis=-1, keepdims=True))
                        p = jnp.exp2(s - m_new)
                        acc = jnp.exp2(m - m_new) * acc + jnp.dot(
                            p.astype(BF16), v_ref[pl.ds(start, tb), heads], preferred_element_type=F32)
                        m = m_new
                    out.append((m, acc))
            return tuple(out)

        one = (jnp.full((tb, 1), -jnp.inf, F32), jnp.zeros((tb, HEAD_PAD), F32))
        carry = lax.fori_loop(0, i * group, functools.partial(step, kinds=("full",) * group), (one,) * (2 * group))
        for d in range(group):
            kinds = tuple("none" if g < d else ("diagonal" if g == d else "full") for g in range(group))
            carry = step(i * group + d, carry, kinds)
        lane = lax.broadcasted_iota(jnp.int32, (tb, HEAD_PAD), 1)
        low = lane < V_HEAD
        for g in range(group):
            halves = []
            for hh in range(2):
                m, acc = carry[2 * g + hh]
                swapped = pltpu.roll(acc, V_HEAD, 1)
                halves.append(acc / swapped)
                lse = (m * LN_2) + jnp.log(jnp.where(low, swapped, acc))
                lr_ref[hh, :, g * tb:(g + 1) * tb] = lse.T[0:8, :]
            o_ref[g * tb:(g + 1) * tb, :] = jnp.where(low, halves[0], pltpu.roll(halves[1], V_HEAD, 1))

    t = n_seq * seq
    return pl.pallas_call(
        body, name="attention_fwd", grid=(n_seq, N_HEADS // 2, nq),
        in_specs=[pl.BlockSpec((rows, pair), lambda s, p, i: (s * nq + i, p)),
                  pl.BlockSpec((seq, pair), lambda s, p, i: (s, p)),
                  pl.BlockSpec((seq, pair), lambda s, p, i: (s, p))],
        out_specs=[pl.BlockSpec((rows, HEAD_PAD), lambda s, p, i: (s * nq + i, p)),
                   pl.BlockSpec((None, 2, 8, rows), lambda s, p, i: (s, p, 0, i))],
        out_shape=[jax.ShapeDtypeStruct((t, E_B), F32), jax.ShapeDtypeStruct((n_seq, N_HEADS, 8, seq), F32)],
        compiler_params=_params(3),
    )(q, k, v)


def _attn_bwd(q, k, v, do, lse_row, delta_row, n_seq, seq):
    tb, group = ATTN_TILE, ATTN_BWD_GROUP
    rows = tb * group
    nk = seq // rows
    n_inner = seq // tb
    pair = 2 * HEAD_PAD

    def body(q_ref, k_ref, v_ref, do_ref, lr_ref, dr_ref, dk_ref, dv_ref, dq_out_ref, dq_ref):
        j = pl.program_id(2)

        @pl.when(j == 0)
        def _():
            dq_ref[...] = jnp.zeros_like(dq_ref)

        row = lax.broadcasted_iota(jnp.int32, (tb, tb), 0)
        col = lax.broadcasted_iota(jnp.int32, (tb, tb), 1)
        causal = col >= row

        def step(i, carry, kinds):
            start = pl.multiple_of(i * tb, tb)
            out = []
            for hh in range(2):
                heads = slice(hh * HEAD_PAD, (hh + 1) * HEAD_PAD)
                qi = q_ref[pl.ds(start, tb), heads]
                doi = do_ref[pl.ds(start, tb), heads]
                dq = None
                for g in range(group):
                    dk_acc, dv_acc = carry[hh * group + g]
                    if kinds[g] != "none":
                        kg = k_ref[g * tb:(g + 1) * tb, heads]
                        st = lax.dot_general(kg, qi, NT, preferred_element_type=F32) * SOFTMAX_SCALE
                        pt = jnp.exp(st - lr_ref[hh, 0:1, pl.ds(start, tb)])
                        if kinds[g] == "diagonal":
                            pt = jnp.where(causal, pt, 0.0)
                        dv_acc = dv_acc + jnp.dot(pt.astype(BF16), doi, preferred_element_type=F32)
                        dpt = lax.dot_general(v_ref[g * tb:(g + 1) * tb, heads], doi, NT, preferred_element_type=F32)
                        dst = (pt * (dpt - dr_ref[hh, 0:1, pl.ds(start, tb)]) * SOFTMAX_SCALE).astype(BF16)
                        dk_acc = dk_acc + jnp.dot(dst, qi, preferred_element_type=F32)
                        term = lax.dot_general(dst, kg, TN, preferred_element_type=F32)
                        dq = term if dq is None else dq + term
                    out.append((dk_acc, dv_acc))
                dq_ref[pl.ds(start, tb), heads] += dq
            return tuple(out)

        one = (jnp.zeros((tb, HEAD_PAD), F32), jnp.zeros((tb, HEAD_PAD), F32))
        carry = (one,) * (2 * group)
        for d in range(group):
            kinds = tuple("full" if g < d else ("diagonal" if g == d else "none") for g in range(group))
            carry = step(j * group + d, carry, kinds)
        carry = lax.fori_loop((j + 1) * group, n_inner, functools.partial(step, kinds=("full",) * group), carry)
        for hh in range(2):
            for g in range(group):
                dk_ref[g * tb:(g + 1) * tb, hh * HEAD_PAD:(hh + 1) * HEAD_PAD] = carry[hh * group + g][0].astype(BF16)
                dv_ref[g * tb:(g + 1) * tb, hh * HEAD_PAD:(hh + 1) * HEAD_PAD] = carry[hh * group + g][1].astype(BF16)

        @pl.when(j == nk - 1)
        def _():
            dq_out_ref[...] = dq_ref[...].astype(BF16)

    t = n_seq * seq
    wide = N_HEADS * HEAD_PAD
    return pl.pallas_call(
        body, name="attention_bwd", grid=(n_seq, N_HEADS // 2, nk),
        in_specs=[pl.BlockSpec((seq, pair), lambda s, p, j: (s, p)),
                  pl.BlockSpec((rows, pair), lambda s, p, j: (s * nk + j, p)),
                  pl.BlockSpec((rows, pair), lambda s, p, j: (s * nk + j, p)),
                  pl.BlockSpec((seq, pair), lambda s, p, j: (s, p)),
                  pl.BlockSpec((None, 2, 8, seq), lambda s, p, j: (s, p, 0, 0)),
                  pl.BlockSpec((None, 2, 8, seq), lambda s, p, j: (s, p, 0, 0))],
        out_specs=[pl.BlockSpec((rows, pair), lambda s, p, j: (s * nk + j, p)),
                   pl.BlockSpec((rows, pair), lambda s, p, j: (s * nk + j, p)),
                   pl.BlockSpec((seq, pair), lambda s, p, j: (s, p))],
        out_shape=[jax.ShapeDtypeStruct((t, wide), BF16), jax.ShapeDtypeStruct((t, wide), BF16),
                   jax.ShapeDtypeStruct((t, wide), BF16)],
        scratch_shapes=[pltpu.VMEM((seq, pair), F32)],
        compiler_params=_params(3),
    )(q, k, v, do, lse_row, delta_row)


def _head_tail(o, gb, x1, w_out, final_norm, target, n_seq, seq):
    tm = HEAD_TAIL_TILE
    nt = seq // tm
    n_col = D_MODEL // N_DEV

    head_of_column = np.arange(E_B) // V_HEAD
    selector = jnp.asarray((head_of_column[None, :] == np.arange(N_HEADS)[:, None]).astype(np.float32))

    def body(o_ref, gb_ref, x1_ref, wout_ref, fn_ref, tgt_ref, sel_ref,
             dx2_ref, do_ref, dgb_ref, dr_ref, dwout_ref, dfn_ref, loss_ref):
        first = jnp.logical_and(pl.program_id(0) == 0, pl.program_id(1) == 0)

        @pl.when(first)
        def _():
            dwout_ref[...] = jnp.zeros_like(dwout_ref)
            dfn_ref[...] = jnp.zeros_like(dfn_ref)
            loss_ref[...] = jnp.zeros_like(loss_ref)

        ov, g = o_ref[...], gb_ref[...]
        silu, dsilu = _silu_parts(g)
        gated = (ov * silu).astype(BF16)
        x2 = x1_ref[...] + jnp.dot(gated, wout_ref[...], preferred_element_type=F32)
        r = _rstd(x2)
        xh = x2 * r
        err = xh * fn_ref[...] - tgt_ref[...]
        loss_ref[...] += 0.5 * jnp.sum(jnp.mean(err * err, axis=-1, keepdims=True), axis=0, keepdims=True)
        dy = err / D_MODEL
        dfn_ref[...] += jnp.sum(dy * xh, axis=0, keepdims=True)
        dx2 = _norm_bwd(dy * fn_ref[...], xh, r)
        dx2_ref[...] = dx2
        dx2b = dx2.astype(BF16)
        dw = lax.dot_general(gated, dx2b, TN, preferred_element_type=F32)
        for d in range(N_DEV):
            dwout_ref[d] += dw[:, d * n_col:(d + 1) * n_col]
        dgated = lax.dot_general(dx2b, wout_ref[...], NT, preferred_element_type=F32)
        do = dgated * silu
        dgb_ref[...] = (dgated * ov * dsilu).astype(BF16)
        delta = lax.dot_general(sel_ref[...], do * ov, NT, precision=lax.Precision.HIGHEST, preferred_element_type=F32)
        lane = lax.broadcasted_iota(jnp.int32, (tm, HEAD_PAD), 1)
        low = lane < V_HEAD
        for p in range(N_HEADS // 2):
            do_pair = do[:, p * HEAD_PAD:(p + 1) * HEAD_PAD]
            for hh in range(2):
                h = 2 * p + hh
                mine = do_pair if hh == 0 else pltpu.roll(do_pair, V_HEAD, 1)
                do_ref[:, h * HEAD_PAD:(h + 1) * HEAD_PAD] = jnp.where(low, mine, 0.0).astype(BF16)
                dr_ref[h] = jnp.broadcast_to(delta[h:h + 1, :], (8, tm))

    tok = lambda width: pl.BlockSpec((tm, width), lambda s, i: (s * nt + i, 0))
    t = n_seq * seq
    return pl.pallas_call(
        body, name="head_tail", grid=(n_seq, nt),
        in_specs=[tok(E_B), tok(E_B), tok(D_MODEL), _const_spec(w_out.shape), _const_spec((1, D_MODEL)), tok(D_MODEL),
                  _const_spec((N_HEADS, E_B))],
        out_specs=[tok(D_MODEL), tok(N_HEADS * HEAD_PAD), tok(E_B),
                   pl.BlockSpec((None, N_HEADS, 8, tm), lambda s, i: (s, 0, 0, i)),
                   _acc_spec((N_DEV, E_B, n_col)), _acc_spec((1, D_MODEL)), _acc_spec((1, 1))],
        out_shape=[jax.ShapeDtypeStruct((t, D_MODEL), F32), jax.ShapeDtypeStruct((t, N_HEADS * HEAD_PAD), BF16),
                   jax.ShapeDtypeStruct((t, E_B), BF16),
                   jax.ShapeDtypeStruct((n_seq, N_HEADS, 8, seq), F32), jax.ShapeDtypeStruct((N_DEV, E_B, n_col), F32),
                   jax.ShapeDtypeStruct((1, D_MODEL), F32), jax.ShapeDtypeStruct((1, 1), F32)],
        compiler_params=_params(2),
    )(o, gb, x1, w_out, final_norm, target, selector)


def _adamw_math(w, g, m, v):
    m = ADAM_B1 * m + (1.0 - ADAM_B1) * g
    v = ADAM_B2 * v + (1.0 - ADAM_B2) * jnp.square(g)
    m_hat = m / (1.0 - ADAM_B1 ** ADAM_STEP)
    v_hat = v / (1.0 - ADAM_B2 ** ADAM_STEP)
    delta = -ADAM_LR * (m_hat / (jnp.sqrt(v_hat) + ADAM_EPS) + ADAM_WD * w)
    return delta, m, v


def _adamw_reduce(own, parts, w, m, v):
    rows, cols = w.shape
    br = 256
    n_parts = parts.shape[0]

    def body(own_ref, p_ref, w_ref, m_ref, v_ref, g_ref, d_ref, nm_ref, nv_ref):
        g = own_ref[...]
        for k in range(n_parts):
            g = g + p_ref[k].astype(F32)
        g_ref[...] = g
        d_ref[...], nm_ref[...], nv_ref[...] = _adamw_math(w_ref[...], g, m_ref[...], v_ref[...])

    blk = pl.BlockSpec((br, cols), lambda i: (i, 0))
    return pl.pallas_call(
        body, name="adamw_reduce", grid=(rows // br,),
        in_specs=[blk, pl.BlockSpec((n_parts, br, cols), lambda i: (0, i, 0)), blk, blk, blk],
        out_specs=[blk] * 4, out_shape=[jax.ShapeDtypeStruct((rows, cols), F32)] * 4,
        compiler_params=_params(1),
    )(own, parts, w, m, v)


def _adamw_reduce_many(parts_list, states):
    n = len(parts_list)

    def body(*refs):
        ins, outs = refs[:4 * n], refs[4 * n:]
        for j in range(n):
            p_ref, w_ref, m_ref, v_ref = ins[4 * j:4 * j + 4]
            g = p_ref[0].astype(F32)
            for k in range(1, p_ref.shape[0]):
                g = g + p_ref[k].astype(F32)
            outs[4 * j][...] = g
            outs[4 * j + 1][...], outs[4 * j + 2][...], outs[4 * j + 3][...] = _adamw_math(
                w_ref[...], g, m_ref[...], v_ref[...])

    flat_in = [a for p, st in zip(parts_list, states) for a in (p,) + tuple(st)]
    shapes = [jax.ShapeDtypeStruct(st[0].shape, F32) for st in states for _ in range(4)]
    flat = pl.pallas_call(body, name="adamw_reduce_many", out_shape=shapes,
                          compiler_params=pltpu.CompilerParams(vmem_limit_bytes=VMEM_LIMIT))(*flat_in)
    return [tuple(flat[4 * j:4 * j + 4]) for j in range(n)]


def _sum_parts(parts):
    def body(p_ref, o_ref):
        g = p_ref[0]
        for k in range(1, N_DEV):
            g = g + p_ref[k]
        o_ref[...] = g

    return pl.pallas_call(body, name="small_grad_sum", out_shape=jax.ShapeDtypeStruct(parts.shape[1:], F32))(parts)


def _adamw_small(gs, ws, ms, vs):
    n = len(gs)

    def body(*refs):
        ins, outs = refs[:4 * n], refs[4 * n:]
        for j in range(n):
            g_ref, w_ref, m_ref, v_ref = ins[j], ins[n + j], ins[2 * n + j], ins[3 * n + j]
            outs[3 * j][...], outs[3 * j + 1][...], outs[3 * j + 2][...] = _adamw_math(
                w_ref[...], g_ref[...], m_ref[...], v_ref[...])

    shapes = [jax.ShapeDtypeStruct(g.shape, F32) for g in gs for _ in range(3)]
    flat = pl.pallas_call(body, name="adamw_small", out_shape=shapes)(*gs, *ws, *ms, *vs)
    return [tuple(flat[3 * j:3 * j + 3]) for j in range(n)]


SMALL_ROWS = 8
LOSS_LANE = D_MODEL - 1


def _pack_small(g_fn, g_kvn, g_bn, g_an, g_ckvn, g_qn, g_conv, loss_part):
    def body(fn_ref, kvn_ref, bn_ref, an_ref, ckvn_ref, qn_ref, conv_ref, loss_ref, o_ref):
        o_ref[0:1, :] = fn_ref[...]
        o_ref[1:2, :] = kvn_ref[...]
        o_ref[2:3, :] = bn_ref[...]
        o_ref[3:4, :] = an_ref[...]
        lane = lax.broadcasted_iota(jnp.int32, (1, D_MODEL), 1)
        o_ref[4:5, :] = jnp.where(lane == LOSS_LANE, loss_ref[...], 0.0)
        o_ref[4:5, 0:KV_RANK] = ckvn_ref[...]
        o_ref[4:5, KV_RANK:KV_RANK + Q_RANK] = qn_ref[...]
        o_ref[5:8, :] = conv_ref[0:3, :]

    return pl.pallas_call(body, name="pack_small", out_shape=jax.ShapeDtypeStruct((SMALL_ROWS, D_MODEL), F32))(
        g_fn, g_kvn, g_bn, g_an, g_ckvn, g_qn, g_conv, loss_part)


def _pad_cols(a, width):
    return jnp.pad(a, ((0, 0), (0, width - a.shape[1])))


def _dkv_to_padded(a):
    r = a.shape[0]
    z = jnp.zeros((r, ROPE_LO), a.dtype)
    z2 = jnp.zeros((r, HEAD_PAD - ROPE_LO - QK_ROPE), a.dtype)
    return jnp.concatenate([a[:, :KV_RANK], z, a[:, KV_RANK:], z2], axis=1)


def _dkv_from_padded(a):
    return jnp.concatenate([a[:, :KV_RANK], a[:, KV_RANK + ROPE_LO:KV_RANK + ROPE_LO + QK_ROPE]], axis=1)


def _unstack_cols(a):
    return jnp.transpose(a, (1, 0, 2)).reshape(a.shape[1], N_DEV * a.shape[2])


def kernel(x, positions, a_norm, a_w_in, a_conv, a_w_out, kv_norm, w_dkv, ckv_norm, w_ukv, b_norm, b_w_in, b_q_norm, b_w_uq, b_w_out, final_norm, loss_target, m_a_norm, m_a_w_in, m_a_conv, m_a_w_out, m_kv_norm, m_w_dkv, m_ckv_norm, m_w_ukv, m_b_norm, m_b_w_in, m_b_q_norm, m_b_w_uq, m_b_w_out, m_final_norm, v_a_norm, v_a_w_in, v_a_conv, v_a_w_out, v_kv_norm, v_w_dkv, v_ckv_norm, v_w_ukv, v_b_norm, v_b_w_in, v_b_q_norm, v_b_w_uq, v_b_w_out, v_final_norm):
    n_seq, seq, _ = x.shape
    t = n_seq * seq
    me = 4 * lax.axis_index("x") + 2 * lax.axis_index("y") + lax.axis_index("c")

    big = {
        "a_w_in": (a_w_in[0], m_a_w_in[0], v_a_w_in[0]),
        "a_w_out": (a_w_out[0], m_a_w_out[0], v_a_w_out[0]),
        "w_dkv": tuple(_dkv_to_padded(a) for a in (w_dkv, m_w_dkv, v_w_dkv)),
        "w_ukv": (w_ukv, m_w_ukv, v_w_ukv),
        "b_w_in": (b_w_in[0], m_b_w_in[0], v_b_w_in[0]),
        "b_w_uq": tuple(_pad_cols(a[0], HEAD_PAD) for a in (b_w_uq, m_b_w_uq, v_b_w_uq)),
        "b_w_out": (b_w_out[0], m_b_w_out[0], v_b_w_out[0]),
    }
    names = list(big)
    first_names, later_names = names[:2], names[2:]
    gathered = _all_gather([big[n][0].astype(BF16) for n in first_names] + [a_norm, a_conv[0]])
    a_norm_f = gathered[2].reshape(1, D_MODEL)
    a_conv_f = _unstack_cols(gathered[3])
    w_a_in = gathered[0]
    w_a_out = gathered[1].reshape(D_MODEL, D_MODEL)

    x2d = x.reshape(t, D_MODEL)
    tgt = loss_target.reshape(t, D_MODEL)
    pos = positions.astype(F32).reshape(t, 1)
    rc = _rope_consts()
    kvn, ckvn, bn, qn, fn = kv_norm.reshape(1, -1), ckv_norm.reshape(1, -1), b_norm, b_q_norm, final_norm.reshape(1, -1)

    x1, proj, conv, later = _conv_fwd(x2d, a_norm_f, w_a_in, a_conv_f, w_a_out, n_seq, seq,
                                      [big[n][0].astype(BF16) for n in later_names])
    full = dict(zip(later_names, later))
    w_dkv_f = full["w_dkv"].reshape(D_MODEL, CKR_PAD)
    w_ukv_f = _unstack_cols(full["w_ukv"])
    w_b_in = full["b_w_in"].reshape(D_MODEL, Q_RANK + E_B)
    w_uq_f = _unstack_cols(full["b_w_uq"])
    w_b_out = _unstack_cols(full["b_w_out"])
    layer_b = (kvn, w_dkv_f, ckvn, w_ukv_f, bn, w_b_in, qn, w_uq_f)
    ckr, cq, gb, q, k, v, cosv, sinv = _proj_fwd(x1, pos, rc, *layer_b)
    o, lse_row = _attn_fwd(q, k, v, n_seq, seq)
    dx2, do, dgb, delta_row, g_b_out, g_fn, loss_part = _head_tail(o, gb, x1, w_b_out, fn, tgt, n_seq, seq)
    dk, dv, dq = _attn_bwd(q, k, v, do, lse_row, delta_row, n_seq, seq)
    dx1, g_uq, g_b_in, g_ukv, g_dkv, g_qn, g_bn, g_ckvn, g_kvn = _proj_bwd(
        dq, dk, dv, dgb, cq, ckr, x1, dx2, cosv, sinv, rc, *layer_b)
    stacks = {
        "w_dkv": g_dkv.reshape(N_DEV, D_MODEL // N_DEV, CKR_PAD),
        "w_ukv": g_ukv,
        "b_w_in": g_b_in.reshape(N_DEV, D_MODEL // N_DEV, Q_RANK + E_B),
        "b_w_uq": g_uq,
        "b_w_out": g_b_out,
    }
    (dx, h_a, dproj, g_a_out, g_an, g_conv), later_parts = _conv_bwd(
        dx1, x2d, proj, conv, a_norm_f, w_a_in, a_conv_f, w_a_out, n_seq, seq, [stacks[n] for n in later_names])
    parts = dict(zip(later_names, later_parts))
    small = _pack_small(g_fn, g_kvn, g_bn, g_an, g_ckvn, g_qn, g_conv, loss_part)
    a_in_own, a_in_got, (parts["a_w_out"],), small_parts = _w_in_grad_exchange(
        h_a, dproj, D_MODEL * 4 // N_DEV, (me // 2).reshape(1).astype(jnp.int32),
        [g_a_out.astype(BF16).reshape(N_DEV, D_MODEL // N_DEV, D_MODEL)], small)

    outs = {"a_w_in": _adamw_reduce(a_in_own, a_in_got, *big["a_w_in"])}
    outs.update(zip(names[1:], _adamw_reduce_many([parts[n] for n in names[1:]], [big[n] for n in names[1:]])))
    outs["w_dkv"] = tuple(_dkv_from_padded(a) for a in outs["w_dkv"])
    outs["b_w_uq"] = tuple(a[:, :QK_NOPE + QK_ROPE] for a in outs["b_w_uq"])
    for n in ("a_w_in", "a_w_out", "b_w_in", "b_w_uq", "b_w_out"):
        outs[n] = tuple(a[None] for a in outs[n])

    total = _sum_parts(small_parts)
    loss = total[4, LOSS_LANE]
    shard = D_MODEL // N_DEV
    g_small = {
        "final_norm": total[0], "kv_norm": total[1], "b_norm": total[2:3],
        "a_norm": lax.dynamic_slice_in_dim(total[3:4], me * shard, shard, axis=1),
        "ckv_norm": total[4, 0:KV_RANK], "b_q_norm": total[4:5, KV_RANK:KV_RANK + Q_RANK],
        "a_conv": lax.dynamic_slice_in_dim(total[5:8], me * shard, shard, axis=1)[None],
    }
    small_state = {
        "final_norm": (final_norm, m_final_norm, v_final_norm), "kv_norm": (kv_norm, m_kv_norm, v_kv_norm),
        "b_norm": (b_norm, m_b_norm, v_b_norm), "a_norm": (a_norm, m_a_norm, v_a_norm),
        "ckv_norm": (ckv_norm, m_ckv_norm, v_ckv_norm), "b_q_norm": (b_q_norm, m_b_q_norm, v_b_q_norm),
        "a_conv": (a_conv, m_a_conv, v_a_conv),
    }
    small_names = list(g_small)
    as2d = lambda a: a.reshape(-1, a.shape[-1])
    upd = _adamw_small([as2d(g_small[n]) for n in small_names],
                       *[[as2d(small_state[n][j]) for n in small_names] for j in range(3)])
    for n, u in zip(small_names, upd):
        outs[n] = (g_small[n],) + tuple(a.reshape(g_small[n].shape) for a in u)

    order = ["a_norm", "a_w_in", "a_conv", "a_w_out", "kv_norm", "w_dkv", "ckv_norm", "w_ukv", "b_norm", "b_w_in",
             "b_q_norm", "b_w_uq", "b_w_out", "final_norm"]
    result = [loss, dx.reshape(n_seq, seq, D_MODEL)]
    for j in range(4):
        result += [outs[n][j] for n in order]
    return tuple(result)
```

```python
import functools
import math

import numpy as np
import jax
import jax.numpy as jnp
from jax import lax
from jax.experimental import pallas as pl
from jax.experimental.pallas import tpu as pltpu

F32 = jnp.float32
BF16 = jnp.bfloat16
SAVED = BF16

D_MODEL = 1024
N_HEADS = 8
QK_NOPE = 64
QK_ROPE = 32
V_HEAD = 64
KV_RANK = 256
Q_RANK = 384
E_B = N_HEADS * V_HEAD
HEAD_PAD = 128
CKR_PAD = KV_RANK + HEAD_PAD
ROPE_LO = QK_NOPE
ROPE_HALF = QK_ROPE // 2
ROPE_THETA = 10000.0
SOFTMAX_SCALE = 1.0 / math.sqrt(QK_NOPE + QK_ROPE)
LOG2_E = math.log2(math.e)
LN_2 = math.log(2.0)
EPS = 1e-6
N_DEV = 8

ADAM_LR = 0.001
ADAM_B1 = 0.9
ADAM_B2 = 0.999
ADAM_EPS = 1e-08
ADAM_WD = 0.01
ADAM_STEP = 10

CONV_BWD_TILE = 256
CONV_FWD_TILE = 512
LATENT_BWD_TILE = 512
LATENT_FWD_TILE = 1024
HEAD_TAIL_TILE = 1024
RING_SLOTS = 3
PICKUP_BLOCK = 1
ATTN_TILE = 512
ATTN_GROUP = 4
ATTN_BWD_GROUP = 4
VMEM_LIMIT = 60 * 1024 * 1024

MESH = pl.DeviceIdType.MESH
NT = (((1,), (1,)), ((), ()))
TN = (((0,), (0,)), ((), ()))


def _dot(a, b):
    return jnp.dot(a.astype(BF16), b.astype(BF16), preferred_element_type=F32)


def _dot_nt(a, b):
    return lax.dot_general(a.astype(BF16), b.astype(BF16), NT, preferred_element_type=F32)


def _dot_tn(a, b):
    return lax.dot_general(a.astype(BF16), b.astype(BF16), TN, preferred_element_type=F32)


def _rstd(x):
    return lax.rsqrt(jnp.mean(x * x, axis=-1, keepdims=True) + EPS)


def _norm_bwd(a, xh, r):
    return r * (a - xh * jnp.mean(a * xh, axis=-1, keepdims=True))


def _silu_parts(g):
    sg = jax.nn.sigmoid(g)
    return g * sg, sg * (1.0 + g * (1.0 - sg))


def _rope_consts():
    inv = (ROPE_THETA ** (-np.arange(0, QK_ROPE, 2, dtype=np.float32) / QK_ROPE)).astype(np.float32)
    t = np.zeros((8, HEAD_PAD), np.float32)
    t[0, ROPE_LO:ROPE_LO + ROPE_HALF] = inv
    t[0, ROPE_LO + ROPE_HALF:ROPE_LO + QK_ROPE] = inv
    t[1, ROPE_LO:ROPE_LO + ROPE_HALF] = -1.0
    t[2, ROPE_LO + ROPE_HALF:ROPE_LO + QK_ROPE] = 1.0
    return jnp.asarray(t)


def _rope_angles(pos, rc):
    ang = pos * rc[0:1, :]
    return jnp.cos(ang), jnp.sin(ang)


def _rope_tables(cosv, sinv, rc):
    return cosv, sinv * rc[1:2, :], sinv * rc[2:3, :]


def _rope(x, ct, s1, s2):
    up = pltpu.roll(x, HEAD_PAD - ROPE_HALF, 1)
    dn = pltpu.roll(x, ROPE_HALF, 1)
    return x * ct + up * s1 + dn * s2


def _const_spec(shape):
    nd = len(shape)
    return pl.BlockSpec(shape, lambda *_: (0,) * nd, pipeline_mode=pl.Buffered(1))


def _acc_spec(shape):
    nd = len(shape)
    return pl.BlockSpec(shape, lambda *_: (0,) * nd)


def _params(n_axes):
    return pltpu.CompilerParams(dimension_semantics=("arbitrary",) * n_axes, vmem_limit_bytes=VMEM_LIMIT)


def _place():
    x, y, c = lax.axis_index("x"), lax.axis_index("y"), lax.axis_index("c")
    return x, y, c, 4 * x + 2 * y + c


def _peer(x, y, c, mask):
    px = 1 - x if mask & 4 else x
    py = 1 - y if mask & 2 else y
    pc = 1 - c if mask & 1 else c
    return (px, py, pc), 4 * px + 2 * py + pc


ANY_SPEC = pl.BlockSpec(memory_space=pl.ANY)


def _comm_sems(n):
    return [pltpu.SemaphoreType.DMA((n, N_DEV - 1)), pltpu.SemaphoreType.DMA((n, N_DEV - 1)), pltpu.SemaphoreType.DMA((n,))]


def _gather_copies(ins, outs, sems):
    send_sems, recv_sems, local_sems = sems
    x, y, c, me = _place()
    starts, waits = [], []
    for w in range(len(ins)):
        mine = pltpu.make_async_copy(ins[w], outs[w].at[me], local_sems.at[w])
        starts.append(mine)
        waits.append(mine)
        for mask in range(1, N_DEV):
            peer, peer_idx = _peer(x, y, c, mask)
            starts.append(pltpu.make_async_remote_copy(
                src_ref=ins[w], dst_ref=outs[w].at[me], send_sem=send_sems.at[w, mask - 1],
                recv_sem=recv_sems.at[w, mask - 1], device_id=peer, device_id_type=MESH))
            waits.append(pltpu.make_async_remote_copy(
                src_ref=ins[w], dst_ref=outs[w].at[peer_idx], send_sem=send_sems.at[w, mask - 1],
                recv_sem=recv_sems.at[w, mask - 1], device_id=peer, device_id_type=MESH))
    return starts, waits


def _scatter_copies(ins, outs, sems):
    send_sems, recv_sems, local_sems = sems
    x, y, c, me = _place()
    copies = []
    for w in range(len(ins)):
        copies.append(pltpu.make_async_copy(ins[w].at[me], outs[w].at[0], local_sems.at[w]))
        for mask in range(1, N_DEV):
            peer, peer_idx = _peer(x, y, c, mask)
            copies.append(pltpu.make_async_remote_copy(
                src_ref=ins[w].at[peer_idx], dst_ref=outs[w].at[mask], send_sem=send_sems.at[w, mask - 1],
                recv_sem=recv_sems.at[w, mask - 1], device_id=peer, device_id_type=MESH))
    return copies, copies


def _stacked(arrays):
    return [jax.ShapeDtypeStruct((N_DEV,) + a.shape, a.dtype) for a in arrays]


def _all_gather(shards):
    n = len(shards)

    def body(*refs):
        ins, outs = refs[:n], refs[n:2 * n]
        send_sems, recv_sems, local_sems = refs[2 * n:]
        x, y, c, me = _place()
        sibling = (x, y, 1 - c)
        chips = [(1 - x, y), (x, 1 - y), (1 - x, 1 - y)]

        def copy(w, k, block, to, src=None):
            idx = 4 * block[0] + 2 * block[1] + block[2]
            return pltpu.make_async_remote_copy(
                src_ref=outs[w].at[idx] if src is None else src, dst_ref=outs[w].at[idx],
                send_sem=send_sems.at[w, k], recv_sem=recv_sems.at[w, k], device_id=to, device_id_type=MESH)

        local, sent = [], []
        for w in range(n):
            mine = pltpu.make_async_copy(ins[w], outs[w].at[me], local_sems.at[w])
            mine.start()
            local.append(mine)
            first = [copy(w, 0, (x, y, c), sibling, src=ins[w])]
            first += [copy(w, 1 + j, (x, y, c), (*chip, c), src=ins[w]) for j, chip in enumerate(chips)]
            for cp in first:
                cp.start()
            sent += first
        for w in range(n):
            for j, chip in enumerate(chips):
                copy(w, 1 + j, (*chip, c), (x, y, c)).wait_recv()
                onward = copy(w, 4 + j, (*chip, c), sibling)
                onward.start()
                sent.append(onward)
        for w in range(n):
            copy(w, 0, sibling, (x, y, c)).wait_recv()
            for j, chip in enumerate(chips):
                copy(w, 4 + j, (*chip, 1 - c), (x, y, c)).wait_recv()
        for cp in sent:
            cp.wait_send()
        for cp in local:
            cp.wait()

    return pl.pallas_call(
        body, name="weight_all_gather", out_shape=_stacked(shards),
        in_specs=[ANY_SPEC] * n, out_specs=[ANY_SPEC] * n, scratch_shapes=_comm_sems(n),
    )(*shards)


def _conv_fwd(x, a_norm, w_in, conv_w, w_out, n_seq, seq, later_shards):
    tm = CONV_FWD_TILE
    nt = seq // tm
    n_col = w_in.shape[2]
    n_later = len(later_shards)

    def body(x_ref, an_ref, win_ref, cw_ref, wout_ref, *rest):
        shard_refs, rest = rest[:n_later], rest[n_later:]
        x1_ref, proj_ref, conv_ref = rest[:3]
        stack_refs, rest = rest[3:3 + n_later], rest[3 + n_later:]
        prev_ref, wide_ref, sems = rest[0], rest[1], rest[2:]
        step = pl.program_id(0) * nt + pl.program_id(1)

        @pl.when(step == 0)
        def _():
            for cp in _gather_copies(shard_refs, stack_refs, sems)[0]:
                cp.start()

        @pl.when(pl.program_id(1) == 0)
        def _():
            prev_ref[...] = jnp.zeros_like(prev_ref)

        xv = x_ref[...]
        h = (xv * _rstd(xv) * an_ref[...]).astype(BF16)
        for d in range(N_DEV):
            part = jnp.dot(h, win_ref[d], preferred_element_type=F32)
            wide_ref[:, d * n_col:(d + 1) * n_col] = part
            proj_ref[:, d * n_col:(d + 1) * n_col] = part.astype(SAVED)
        b = wide_ref[:, 0:D_MODEL]
        v = wide_ref[:, D_MODEL:2 * D_MODEL] * wide_ref[:, 2 * D_MODEL:3 * D_MODEL]
        g = wide_ref[:, 3 * D_MODEL:4 * D_MODEL]
        w0, w1, w2 = cw_ref[0:1, :], cw_ref[1:2, :], cw_ref[2:3, :]
        conv_ref[...] = w0 * pltpu.roll(v, 2, 0) + w1 * pltpu.roll(v, 1, 0) + w2 * v
        rows = lax.broadcasted_iota(jnp.int32, (8, D_MODEL), 0)
        p8, v8 = prev_ref[...], v[0:8]
        back1 = jnp.where(rows < 1, pltpu.roll(p8, 1, 0), pltpu.roll(v8, 1, 0))
        back2 = jnp.where(rows < 2, pltpu.roll(p8, 2, 0), pltpu.roll(v8, 2, 0))
        conv_ref[0:8, :] = w0 * back2 + w1 * back1 + w2 * v8
        prev_ref[...] = v[tm - 8:tm]
        silu, _ = _silu_parts(g)
        yv = silu * b * conv_ref[...]
        x1_ref[...] = xv + _dot(yv, wout_ref[...])

        @pl.when(step == n_seq * nt - 1)
        def _():
            for cp in _gather_copies(shard_refs, stack_refs, sems)[1]:
                cp.wait()

    tok = lambda width: pl.BlockSpec((tm, width), lambda s, i: (s * nt + i, 0))
    t = n_seq * seq
    outs = pl.pallas_call(
        body, name="conv_mixer_fwd", grid=(n_seq, nt),
        in_specs=[tok(D_MODEL), _const_spec((1, D_MODEL)), _const_spec(w_in.shape), _const_spec((3, D_MODEL)),
                  _const_spec(w_out.shape)] + [ANY_SPEC] * n_later,
        out_specs=[tok(D_MODEL), tok(4 * D_MODEL), tok(D_MODEL)] + [ANY_SPEC] * n_later,
        out_shape=[jax.ShapeDtypeStruct((t, D_MODEL), F32), jax.ShapeDtypeStruct((t, 4 * D_MODEL), SAVED),
                   jax.ShapeDtypeStruct((t, D_MODEL), F32)] + _stacked(later_shards),
        scratch_shapes=[pltpu.VMEM((8, D_MODEL), F32), pltpu.VMEM((tm, 4 * D_MODEL), F32)] + _comm_sems(n_later),
        compiler_params=_params(2),
    )(x, a_norm, w_in, conv_w, w_out, *later_shards)
    return outs[0], outs[1], outs[2], outs[3:]


def _conv_bwd(dx1, x, proj, conv, a_norm, w_in, conv_w, w_out, n_seq, seq, ready_stacks):
    tm = CONV_BWD_TILE
    nt = seq // tm
    n_col = w_in.shape[2]
    n_ready = len(ready_stacks)

    def body(dx1_ref, x_ref, proj_ref, conv_ref, an_ref, win_ref, cw_ref, wout_ref, *rest):
        ready_refs, rest = rest[:n_ready], rest[n_ready:]
        dx_ref, h_ref, dproj_ref, dwout_ref, dan_ref, dcw_ref = rest[:6]
        part_refs, rest = rest[6:6 + n_ready], rest[6 + n_ready:]
        next_ref, d1_ref, d2_ref = rest[:3]
        sems = rest[3:]
        step = pl.program_id(0) * nt + pl.program_id(1)
        first = step == 0

        @pl.when(first)
        def _():
            for cp in _scatter_copies(ready_refs, part_refs, sems)[0]:
                cp.start()
            dwout_ref[...] = jnp.zeros_like(dwout_ref)
            dan_ref[...] = jnp.zeros_like(dan_ref)
            dcw_ref[...] = jnp.zeros_like(dcw_ref)

        @pl.when(pl.program_id(1) == 0)
        def _():
            next_ref[...] = jnp.zeros_like(next_ref)

        dx1v = dx1_ref[...]
        dy = _dot_nt(dx1v, wout_ref[...])
        b = proj_ref[:, 0:D_MODEL].astype(F32)
        cc = proj_ref[:, D_MODEL:2 * D_MODEL].astype(F32)
        u = proj_ref[:, 2 * D_MODEL:3 * D_MODEL].astype(F32)
        g = proj_ref[:, 3 * D_MODEL:4 * D_MODEL].astype(F32)
        cv = conv_ref[...]
        silu, dsilu = _silu_parts(g)
        per_part = D_MODEL // n_col

        def back_through_w_in(part, grad):
            grad = grad.astype(BF16)
            dproj_ref[:, part * D_MODEL:(part + 1) * D_MODEL] = grad
            term = None
            for j in range(per_part):
                d = part * per_part + j
                piece = lax.dot_general(grad[:, j * n_col:(j + 1) * n_col], win_ref[d], NT, preferred_element_type=F32)
                term = piece if term is None else term + piece
            return term

        dh = back_through_w_in(0, dy * silu * cv)
        dh += back_through_w_in(3, dy * b * cv * dsilu)
        dwout_ref[...] += _dot_tn(silu * b * cv, dx1v)
        dconv = dy * silu * b
        d1_ref[...] = pltpu.roll(dconv, tm - 1, 0)
        d2_ref[...] = pltpu.roll(dconv, tm - 2, 0)
        rows = lax.broadcasted_iota(jnp.int32, (8, D_MODEL), 0)
        n8, c8 = next_ref[...], dconv[tm - 8:tm]
        d1_ref[tm - 8:tm, :] = jnp.where(rows >= 7, pltpu.roll(n8, 7, 0), pltpu.roll(c8, 7, 0))
        d2_ref[tm - 8:tm, :] = jnp.where(rows >= 6, pltpu.roll(n8, 6, 0), pltpu.roll(c8, 6, 0))
        next_ref[...] = dconv[0:8]
        d1, d2 = d1_ref[...], d2_ref[...]
        v = cc * u
        dcw_ref[0:1, :] += jnp.sum(d2 * v, axis=0, keepdims=True)
        dcw_ref[1:2, :] += jnp.sum(d1 * v, axis=0, keepdims=True)
        dcw_ref[2:3, :] += jnp.sum(dconv * v, axis=0, keepdims=True)
        dv = cw_ref[0:1, :] * d2 + cw_ref[1:2, :] * d1 + cw_ref[2:3, :] * dconv
        dh += back_through_w_in(1, dv * u)
        dh += back_through_w_in(2, dv * cc)
        xv = x_ref[...]
        r = _rstd(xv)
        xh = xv * r
        h_ref[...] = (xh * an_ref[...]).T.astype(BF16)
        dan_ref[...] += jnp.sum(dh * xh, axis=0, keepdims=True)
        dx_ref[...] = dx1v + _norm_bwd(dh * an_ref[...], xh, r)

        @pl.when(step == n_seq * nt - 1)
        def _():
            for cp in _scatter_copies(ready_refs, part_refs, sems)[1]:
                cp.wait()

    tok = lambda width: pl.BlockSpec((tm, width), lambda s, i: (s * nt + nt - 1 - i, 0))
    t = n_seq * seq
    outs = pl.pallas_call(
        body, name="conv_mixer_bwd", grid=(n_seq, nt),
        in_specs=[tok(D_MODEL), tok(D_MODEL), tok(4 * D_MODEL), tok(D_MODEL), _const_spec((1, D_MODEL)),
                  _const_spec(w_in.shape), _const_spec((3, D_MODEL)), _const_spec(w_out.shape)] + [ANY_SPEC] * n_ready,
        out_specs=[tok(D_MODEL), pl.BlockSpec((D_MODEL, tm), lambda s, i: (0, s * nt + nt - 1 - i)),
                   tok(4 * D_MODEL), _acc_spec((D_MODEL, D_MODEL)),
                   _acc_spec((1, D_MODEL)), _acc_spec((8, D_MODEL))] + [ANY_SPEC] * n_ready,
        out_shape=[jax.ShapeDtypeStruct((t, D_MODEL), F32), jax.ShapeDtypeStruct((D_MODEL, t), BF16),
                   jax.ShapeDtypeStruct((t, 4 * D_MODEL), BF16), jax.ShapeDtypeStruct((D_MODEL, D_MODEL), F32),
                   jax.ShapeDtypeStruct((1, D_MODEL), F32), jax.ShapeDtypeStruct((8, D_MODEL), F32)]
        + [jax.ShapeDtypeStruct(a.shape, a.dtype) for a in ready_stacks],
        scratch_shapes=[pltpu.VMEM((8, D_MODEL), F32), pltpu.VMEM((tm, D_MODEL), F32), pltpu.VMEM((tm, D_MODEL), F32)]
        + _comm_sems(n_ready),
        compiler_params=_params(2),
    )(dx1, x, proj, conv, a_norm, w_in, conv_w, w_out, *ready_stacks)
    return outs[:6], outs[6:]


def _w_in_grad_exchange(a_t, b, n_col, chip, ready_stacks, small):
    r, t = a_t.shape
    bt = 4096
    nk = t // bt
    n_ready = len(ready_stacks)
    n_chip = N_DEV // 2
    n_remote = N_DEV - 2

    def body(chip_ref, a_ref, b_ref, *rest):
        ready_refs, small_ref, rest = rest[:n_ready], rest[n_ready], rest[n_ready + 1:]
        own_ref, got_ref, rest = rest[0], rest[1], rest[2:]
        part_refs, small_all_ref, rest = rest[:n_ready], rest[n_ready], rest[n_ready + 1:]
        acc_ref, land_ref, stage_ref, pair_send, pair_recv, chip_send, chip_recv = rest[:7]
        scatter_sems, gather_sems = rest[7:10], rest[10:13]
        s, k = pl.program_id(0), pl.program_id(1)
        x, y, c, _ = _place()
        chip = 2 * x + y

        @pl.when(jnp.logical_and(s == 0, k == 0))
        def _():
            for cp in _scatter_copies(ready_refs, part_refs, scatter_sems)[0]:
                cp.start()
            for cp in _gather_copies([small_ref], [small_all_ref], gather_sems)[0]:
                cp.start()

        for parity in range(2):
            @pl.when(s % 2 == parity)
            def _(parity=parity):
                @pl.when(k == 0)
                def _():
                    acc_ref[parity] = jnp.zeros((r, n_col), F32)

                acc_ref[parity] += jnp.dot(a_ref[...], b_ref[...], preferred_element_type=F32)

        def to_sibling(step):
            return pltpu.make_async_remote_copy(
                src_ref=acc_ref.at[step % 2], dst_ref=land_ref, send_sem=pair_send.at[step], recv_sem=pair_recv.at[step],
                device_id=(x, y, 1 - c), device_id_type=MESH)

        def to_owner(nth):
            owner_chip = (chip + 1 + nth) % n_chip
            slot = jnp.bitwise_xor(chip, owner_chip) - 1
            return pltpu.make_async_remote_copy(
                src_ref=stage_ref.at[nth % 2], dst_ref=got_ref.at[slot], send_sem=chip_send.at[nth],
                recv_sem=chip_recv.at[slot], device_id=(owner_chip // 2, owner_chip % 2, c), device_id_type=MESH)

        for step in range(N_DEV):
            owner_core = step % 2

            @pl.when(jnp.logical_and(s == step, k == nk - 1))
            def _(step=step, owner_core=owner_core):
                @pl.when(c != owner_core)
                def _():
                    to_sibling(step).start()

                if step >= 1:
                    @pl.when(c == owner_core)
                    def _():
                        to_sibling(step - 1).wait_send()

            if step < N_DEV - 1:
                pickup = jnp.logical_and(s == step + 1, k == min(PICKUP_BLOCK, nk - 1))
            else:
                pickup = jnp.logical_and(s == step, k == nk - 1)

            @pl.when(jnp.logical_and(pickup, c == owner_core))
            def _(step=step):
                to_sibling(step).wait_recv()
                total = acc_ref[step % 2] + land_ref[...]
                if step < n_remote:
                    nth = step // 2
                    if nth >= 2:
                        to_owner(nth - 2).wait_send()
                    stage_ref[nth % 2] = total.astype(BF16)
                    to_owner(nth).start()
                else:
                    own_ref[...] = total

        @pl.when(jnp.logical_and(s == N_DEV - 1, k == nk - 1))
        def _():
            @pl.when(c != (N_DEV - 1) % 2)
            def _():
                to_sibling(N_DEV - 1).wait_send()

            to_owner(1).wait_send()
            to_owner(2).wait_send()
            for slot in range(n_chip - 1):
                pltpu.make_async_remote_copy(
                    src_ref=stage_ref.at[0], dst_ref=got_ref.at[slot], send_sem=chip_send.at[0],
                    recv_sem=chip_recv.at[slot], device_id=(x, y, c), device_id_type=MESH).wait_recv()
            for cp in _scatter_copies(ready_refs, part_refs, scatter_sems)[1]:
                cp.wait()
            for cp in _gather_copies([small_ref], [small_all_ref], gather_sems)[1]:
                cp.wait()

    def owner_block(s, k, chip):
        return (k, 2 * ((chip[0] + 1 + s // 2) % n_chip) + s % 2)

    grid_spec = pltpu.PrefetchScalarGridSpec(
        num_scalar_prefetch=1, grid=(N_DEV, nk),
        in_specs=[pl.BlockSpec((r, bt), lambda s, k, chip: (0, k)), pl.BlockSpec((bt, n_col), owner_block)]
        + [ANY_SPEC] * (n_ready + 1),
        out_specs=[pl.BlockSpec((r, n_col), lambda s, k, chip: (0, 0)), ANY_SPEC] + [ANY_SPEC] * (n_ready + 1),
        scratch_shapes=[pltpu.VMEM((2, r, n_col), F32), pltpu.VMEM((r, n_col), F32), pltpu.VMEM((2, r, n_col), BF16),
                        pltpu.SemaphoreType.DMA((N_DEV,)), pltpu.SemaphoreType.DMA((N_DEV,)),
                        pltpu.SemaphoreType.DMA((n_chip - 1,)), pltpu.SemaphoreType.DMA((n_chip - 1,))]
        + _comm_sems(n_ready) + _comm_sems(1))
    outs = pl.pallas_call(
        body, name="w_in_grad_exchange", grid_spec=grid_spec,
        out_shape=[jax.ShapeDtypeStruct((r, n_col), F32), jax.ShapeDtypeStruct((n_chip - 1, r, n_col), BF16)]
        + [jax.ShapeDtypeStruct(p.shape, p.dtype) for p in ready_stacks] + _stacked([small]),
        compiler_params=_params(2),
    )(chip, a_t, b, *ready_stacks, small)
    return outs[0], outs[1], outs[2:2 + n_ready], outs[2 + n_ready]


def _proj_fwd(x1, pos, rc, kv_norm, w_dkv, ckv_norm, w_ukv, b_norm, b_w_in, q_norm, w_uq):
    t = x1.shape[0]
    tm = LATENT_FWD_TILE

    def body(x1_ref, pos_ref, rc_ref, kvn_ref, wdkv_ref, ckvn_ref, wukv_ref, bn_ref, bwin_ref, qn_ref, wuq_ref,
             ckr_ref, cq_ref, gb_ref, q_ref, k_ref, v_ref, cos_ref, sin_ref):
        xv = x1_ref[...]
        xh = xv * _rstd(xv)
        ckr = _dot(xh * kvn_ref[...], wdkv_ref[...])
        pb = _dot(xh * bn_ref[...], bwin_ref[...])
        ckr_ref[...] = ckr
        ckv = ckr[:, 0:KV_RANK]
        kv = _dot(ckv * _rstd(ckv) * ckvn_ref[...], wukv_ref[...])
        cosv, sinv = _rope_angles(pos_ref[...], rc_ref[...])
        cos_ref[...] = cosv
        sin_ref[...] = sinv
        ct, s1, s2 = _rope_tables(cosv, sinv, rc_ref[...])
        k_rope = _rope(ckr[:, KV_RANK:CKR_PAD], ct, s1, s2)
        lane = lax.broadcasted_iota(jnp.int32, (tm, HEAD_PAD), 1)
        low = lane < QK_NOPE
        for h in range(N_HEADS):
            kv_h = kv[:, h * HEAD_PAD:(h + 1) * HEAD_PAD]
            k_ref[:, h * HEAD_PAD:(h + 1) * HEAD_PAD] = jnp.where(low, kv_h, k_rope).astype(BF16)
            v_ref[:, h * HEAD_PAD:(h + 1) * HEAD_PAD] = jnp.where(low, pltpu.roll(kv_h, V_HEAD, 1), 1.0).astype(BF16)
        cq = pb[:, 0:Q_RANK]
        cq_ref[...] = cq
        gb_ref[...] = pb[:, Q_RANK:Q_RANK + E_B]
        q = _dot(cq * _rstd(cq) * qn_ref[...], wuq_ref[...])
        for h in range(N_HEADS):
            q_ref[:, h * HEAD_PAD:(h + 1) * HEAD_PAD] = _rope(
                q[:, h * HEAD_PAD:(h + 1) * HEAD_PAD], ct, s1, s2).astype(BF16)

    tok = lambda width: pl.BlockSpec((tm, width), lambda i: (i, 0))
    weights = [kv_norm, w_dkv, ckv_norm, w_ukv, b_norm, b_w_in, q_norm, w_uq]
    wide = N_HEADS * HEAD_PAD
    return pl.pallas_call(
        body, name="latent_proj_fwd", grid=(t // tm,),
        in_specs=[tok(D_MODEL), tok(1), _const_spec(rc.shape)] + [_const_spec(w.shape) for w in weights],
        out_specs=[tok(CKR_PAD), tok(Q_RANK), tok(E_B), tok(wide), tok(wide), tok(wide), tok(HEAD_PAD), tok(HEAD_PAD)],
        out_shape=[jax.ShapeDtypeStruct((t, CKR_PAD), F32), jax.ShapeDtypeStruct((t, Q_RANK), F32),
                   jax.ShapeDtypeStruct((t, E_B), F32), jax.ShapeDtypeStruct((t, wide), BF16),
                   jax.ShapeDtypeStruct((t, wide), BF16), jax.ShapeDtypeStruct((t, wide), BF16),
                   jax.ShapeDtypeStruct((t, HEAD_PAD), F32), jax.ShapeDtypeStruct((t, HEAD_PAD), F32)],
        compiler_params=_params(1),
    )(x1, pos, rc, *weights)


def _proj_bwd(dq, dk, dv, dgb, cq, ckr, x1, dx2, cosv, sinv, rc, kv_norm, w_dkv, ckv_norm, w_ukv, b_norm, b_w_in, q_norm, w_uq):
    t = x1.shape[0]
    tm = LATENT_BWD_TILE
    wide = N_HEADS * HEAD_PAD

    def body(dq_ref, dk_ref, dv_ref, dgb_ref, cq_ref, ckr_ref, x1_ref, dx2_ref, cos_ref, sin_ref, rc_ref,
             kvn_ref, wdkv_ref, ckvn_ref, wukv_ref, bn_ref, bwin_ref, qn_ref, wuq_ref,
             dx1_ref, dwuq_ref, dwbin_ref, dwukv_ref, dwdkv_ref, dqn_ref, dbn_ref, dckvn_ref, dkvn_ref, dqu_ref, dkv_ref,
             x1_ring, dx2_ring, ring_sems):
        step = pl.program_id(0)
        n_steps = t // tm

        def fetch(at_step):
            slot = at_step % RING_SLOTS
            rows = pl.ds(pl.multiple_of(at_step * tm, tm), tm)
            return [pltpu.make_async_copy(x1_ref.at[rows, :], x1_ring.at[slot], ring_sems.at[0, slot]),
                    pltpu.make_async_copy(dx2_ref.at[rows, :], dx2_ring.at[slot], ring_sems.at[1, slot])]

        @pl.when(step == 0)
        def _():
            for ref in (dwuq_ref, dwbin_ref, dwukv_ref, dwdkv_ref, dqn_ref, dbn_ref, dckvn_ref, dkvn_ref):
                ref[...] = jnp.zeros_like(ref)
            for ahead in range(min(RING_SLOTS - 1, n_steps)):
                for cp in fetch(ahead):
                    cp.start()

        @pl.when(step + RING_SLOTS - 1 < n_steps)
        def _():
            for cp in fetch(step + RING_SLOTS - 1):
                cp.start()

        for cp in fetch(step):
            cp.wait()
        ct, s1, s2 = _rope_tables(cos_ref[...], sin_ref[...], rc_ref[...])
        lane = lax.broadcasted_iota(jnp.int32, (tm, HEAD_PAD), 1)
        low = lane < QK_NOPE
        xv = x1_ring[step % RING_SLOTS]
        r = _rstd(xv)
        xh = xv * r
        dk_rope = jnp.zeros((tm, HEAD_PAD), F32)
        for h in range(N_HEADS):
            dk_h = dk_ref[:, h * HEAD_PAD:(h + 1) * HEAD_PAD].astype(F32)
            dv_h = dv_ref[:, h * HEAD_PAD:(h + 1) * HEAD_PAD].astype(F32)
            dkv_ref[:, h * HEAD_PAD:(h + 1) * HEAD_PAD] = jnp.where(low, dk_h, pltpu.roll(dv_h, V_HEAD, 1)).astype(BF16)
            dk_rope += dk_h
        rope_lanes = jnp.logical_and(lane >= ROPE_LO, lane < ROPE_LO + QK_ROPE)
        dk_rope = jnp.where(rope_lanes, _rope(dk_rope, ct, -s1, -s2), 0.0)
        ckv = ckr_ref[:, 0:KV_RANK]
        rk = _rstd(ckv)
        ckh = ckv * rk
        ckvn = (ckh * ckvn_ref[...]).astype(BF16)
        dwukv = lax.dot_general(ckvn, dkv_ref[...], TN, preferred_element_type=F32)
        for h in range(N_HEADS):
            dwukv_ref[h] += dwukv[:, h * HEAD_PAD:(h + 1) * HEAD_PAD]
        dckvn = lax.dot_general(dkv_ref[...], wukv_ref[...], NT, preferred_element_type=F32)
        dckvn_ref[...] += jnp.sum(dckvn * ckh, axis=0, keepdims=True)
        dckv = _norm_bwd(dckvn * ckvn_ref[...], ckh, rk)
        dckr = jnp.concatenate([dckv, dk_rope], axis=1).astype(BF16)
        dwdkv_ref[...] += _dot_tn(xh * kvn_ref[...], dckr)
        dh2 = lax.dot_general(dckr, wdkv_ref[...], NT, preferred_element_type=F32)
        for h in range(N_HEADS):
            dqu_ref[:, h * HEAD_PAD:(h + 1) * HEAD_PAD] = _rope(
                dq_ref[:, h * HEAD_PAD:(h + 1) * HEAD_PAD].astype(F32), ct, -s1, -s2).astype(BF16)
        cq = cq_ref[...]
        rq = _rstd(cq)
        cqh = cq * rq
        cqn = (cqh * qn_ref[...]).astype(BF16)
        dwuq = lax.dot_general(cqn, dqu_ref[...], TN, preferred_element_type=F32)
        for h in range(N_HEADS):
            dwuq_ref[h] += dwuq[:, h * HEAD_PAD:(h + 1) * HEAD_PAD]
        dcqn = lax.dot_general(dqu_ref[...], wuq_ref[...], NT, preferred_element_type=F32)
        dqn_ref[...] += jnp.sum(dcqn * cqh, axis=0, keepdims=True)
        dcq = _norm_bwd(dcqn * qn_ref[...], cqh, rq)
        dpb = jnp.concatenate([dcq.astype(BF16), dgb_ref[...]], axis=1)
        dwbin_ref[...] += _dot_tn(xh * bn_ref[...], dpb)
        dh3 = lax.dot_general(dpb, bwin_ref[...], NT, preferred_element_type=F32)
        dkvn_ref[...] += jnp.sum(dh2 * xh, axis=0, keepdims=True)
        dbn_ref[...] += jnp.sum(dh3 * xh, axis=0, keepdims=True)
        dx1_ref[...] = dx2_ring[step % RING_SLOTS] + _norm_bwd(dh2 * kvn_ref[...] + dh3 * bn_ref[...], xh, r)

    tok = lambda width: pl.BlockSpec((tm, width), lambda i: (i, 0))
    weights = [kv_norm, w_dkv, ckv_norm, w_ukv, b_norm, b_w_in, q_norm, w_uq]
    acc_shapes = [(N_HEADS, Q_RANK, HEAD_PAD), (D_MODEL, Q_RANK + E_B), (N_HEADS, KV_RANK, HEAD_PAD), (D_MODEL, CKR_PAD),
                  (1, Q_RANK), (1, D_MODEL), (1, KV_RANK), (1, D_MODEL)]
    return pl.pallas_call(
        body, name="latent_proj_bwd", grid=(t // tm,),
        in_specs=[tok(wide), tok(wide), tok(wide), tok(E_B), tok(Q_RANK), tok(CKR_PAD), ANY_SPEC, ANY_SPEC,
                  tok(HEAD_PAD), tok(HEAD_PAD), _const_spec(rc.shape)] + [_const_spec(w.shape) for w in weights],
        out_specs=[tok(D_MODEL)] + [_acc_spec(s) for s in acc_shapes],
        out_shape=[jax.ShapeDtypeStruct((t, D_MODEL), F32)] + [jax.ShapeDtypeStruct(s, F32) for s in acc_shapes],
        scratch_shapes=[pltpu.VMEM((tm, wide), BF16), pltpu.VMEM((tm, wide), BF16),
                        pltpu.VMEM((RING_SLOTS, tm, D_MODEL), F32), pltpu.VMEM((RING_SLOTS, tm, D_MODEL), F32),
                        pltpu.SemaphoreType.DMA((2, RING_SLOTS))],
        compiler_params=_params(1),
    )(dq, dk, dv, dgb, cq, ckr, x1, dx2, cosv, sinv, rc, *weights)


def _attn_fwd(q, k, v, n_seq, seq):
    tb, group = ATTN_TILE, ATTN_GROUP
    rows = tb * group
    nq = seq // rows
    pair = 2 * HEAD_PAD

    def body(q_ref, k_ref, v_ref, o_ref, lr_ref):
        i = pl.program_id(2)
        row = lax.broadcasted_iota(jnp.int32, (tb, tb), 0)
        col = lax.broadcasted_iota(jnp.int32, (tb, tb), 1)
        causal = col <= row

        def step(j, carry, kinds):
            start = pl.multiple_of(j * tb, tb)
            out = []
            for g in range(group):
                for hh in range(2):
                    m, acc = carry[2 * g + hh]
                    if kinds[g] != "none":
                        heads = slice(hh * HEAD_PAD, (hh + 1) * HEAD_PAD)
                        s = lax.dot_general(q_ref[g * tb:(g + 1) * tb, heads], k_ref[pl.ds(start, tb), heads], NT,
                                            preferred_element_type=F32) * (SOFTMAX_SCALE * LOG2_E)
                        if kinds[g] == "diagonal":
                            s = jnp.where(causal, s, -jnp.inf)
                        m_new = jnp.maximum(m, jnp.max(s, axis=-1, keepdims=True))
                        p = jnp.exp2(s - m_new)
                        acc = jnp.exp2(m - m_new) * acc + jnp.dot(
                            p.astype(BF16), v_ref[pl.ds(start, tb), heads], preferred_element_type=F32)
                        m = m_new
                    out.append((m, acc))
            return tuple(out)

        one = (jnp.full((tb, 1), -jnp.inf, F32), jnp.zeros((tb, HEAD_PAD), F32))
        carry = lax.fori_loop(0, i * group, functools.partial(step, kinds=("full",) * group), (one,) * (2 * group))
        for d in range(group):
            kinds = tuple("none" if g < d else ("diagonal" if g == d else "full") for g in range(group))
            carry = step(i * group + d, carry, kinds)
        lane = lax.broadcasted_iota(jnp.int32, (tb, HEAD_PAD), 1)
        low = lane < V_HEAD
        for g in range(group):
            halves = []
            for hh in range(2):
                m, acc = carry[2 * g + hh]
                swapped = pltpu.roll(acc, V_HEAD, 1)
                halves.append(acc / swapped)
                lse = (m * LN_2) + jnp.log(jnp.where(low, swapped, acc))
                lr_ref[hh, :, g * tb:(g + 1) * tb] = lse.T[0:8, :]
            o_ref[g * tb:(g + 1) * tb, :] = jnp.where(low, halves[0], pltpu.roll(halves[1], V_HEAD, 1))

    t = n_seq * seq
    return pl.pallas_call(
        body, name="attention_fwd", grid=(n_seq, N_HEADS // 2, nq),
        in_specs=[pl.BlockSpec((rows, pair), lambda s, p, i: (s * nq + i, p)),
                  pl.BlockSpec((seq, pair), lambda s, p, i: (s, p)),
                  pl.BlockSpec((seq, pair), lambda s, p, i: (s, p))],
        out_specs=[pl.BlockSpec((rows, HEAD_PAD), lambda s, p, i: (s * nq + i, p)),
                   pl.BlockSpec((None, 2, 8, rows), lambda s, p, i: (s, p, 0, i))],
        out_shape=[jax.ShapeDtypeStruct((t, E_B), F32), jax.ShapeDtypeStruct((n_seq, N_HEADS, 8, seq), F32)],
        compiler_params=_params(3),
    )(q, k, v)


def _attn_bwd(q, k, v, do, lse_row, delta_row, n_seq, seq):
    tb, group = ATTN_TILE, ATTN_BWD_GROUP
    rows = tb * group
    nk = seq // rows
    n_inner = seq // tb
    pair = 2 * HEAD_PAD

    def body(q_ref, k_ref, v_ref, do_ref, lr_ref, dr_ref, dk_ref, dv_ref, dq_out_ref, dq_ref):
        j = pl.program_id(2)

        @pl.when(j == 0)
        def _():
            dq_ref[...] = jnp.zeros_like(dq_ref)

        row = lax.broadcasted_iota(jnp.int32, (tb, tb), 0)
        col = lax.broadcasted_iota(jnp.int32, (tb, tb), 1)
        causal = col >= row

        def step(i, carry, kinds):
            start = pl.multiple_of(i * tb, tb)
            out = []
            for hh in range(2):
                heads = slice(hh * HEAD_PAD, (hh + 1) * HEAD_PAD)
                qi = q_ref[pl.ds(start, tb), heads]
                doi = do_ref[pl.ds(start, tb), heads]
                dq = None
                for g in range(group):
                    dk_acc, dv_acc = carry[hh * group + g]
                    if kinds[g] != "none":
                        kg = k_ref[g * tb:(g + 1) * tb, heads]
                        st = lax.dot_general(kg, qi, NT, preferred_element_type=F32) * SOFTMAX_SCALE
                        pt = jnp.exp(st - lr_ref[hh, 0:1, pl.ds(start, tb)])
                        if kinds[g] == "diagonal":
                            pt = jnp.where(causal, pt, 0.0)
                        dv_acc = dv_acc + jnp.dot(pt.astype(BF16), doi, preferred_element_type=F32)
                        dpt = lax.dot_general(v_ref[g * tb:(g + 1) * tb, heads], doi, NT, preferred_element_type=F32)
                        dst = (pt * (dpt - dr_ref[hh, 0:1, pl.ds(start, tb)]) * SOFTMAX_SCALE).astype(BF16)
                        dk_acc = dk_acc + jnp.dot(dst, qi, preferred_element_type=F32)
                        term = lax.dot_general(dst, kg, TN, preferred_element_type=F32)
                        dq = term if dq is None else dq + term
                    out.append((dk_acc, dv_acc))
                dq_ref[pl.ds(start, tb), heads] += dq
            return tuple(out)

        one = (jnp.zeros((tb, HEAD_PAD), F32), jnp.zeros((tb, HEAD_PAD), F32))
        carry = (one,) * (2 * group)
        for d in range(group):
            kinds = tuple("full" if g < d else ("diagonal" if g == d else "none") for g in range(group))
            carry = step(j * group + d, carry, kinds)
        carry = lax.fori_loop((j + 1) * group, n_inner, functools.partial(step, kinds=("full",) * group), carry)
        for hh in range(2):
            for g in range(group):
                dk_ref[g * tb:(g + 1) * tb, hh * HEAD_PAD:(hh + 1) * HEAD_PAD] = carry[hh * group + g][0].astype(BF16)
                dv_ref[g * tb:(g + 1) * tb, hh * HEAD_PAD:(hh + 1) * HEAD_PAD] = carry[hh * group + g][1].astype(BF16)

        @pl.when(j == nk - 1)
        def _():
            dq_out_ref[...] = dq_ref[...].astype(BF16)

    t = n_seq * seq
    wide = N_HEADS * HEAD_PAD
    return pl.pallas_call(
        body, name="attention_bwd", grid=(n_seq, N_HEADS // 2, nk),
        in_specs=[pl.BlockSpec((seq, pair), lambda s, p, j: (s, p)),
                  pl.BlockSpec((rows, pair), lambda s, p, j: (s * nk + j, p)),
                  pl.BlockSpec((rows, pair), lambda s, p, j: (s * nk + j, p)),
                  pl.BlockSpec((seq, pair), lambda s, p, j: (s, p)),
                  pl.BlockSpec((None, 2, 8, seq), lambda s, p, j: (s, p, 0, 0)),
                  pl.BlockSpec((None, 2, 8, seq), lambda s, p, j: (s, p, 0, 0))],
        out_specs=[pl.BlockSpec((rows, pair), lambda s, p, j: (s * nk + j, p)),
                   pl.BlockSpec((rows, pair), lambda s, p, j: (s * nk + j, p)),
                   pl.BlockSpec((seq, pair), lambda s, p, j: (s, p))],
        out_shape=[jax.ShapeDtypeStruct((t, wide), BF16), jax.ShapeDtypeStruct((t, wide), BF16),
                   jax.ShapeDtypeStruct((t, wide), BF16)],
        scratch_shapes=[pltpu.VMEM((seq, pair), F32)],
        compiler_params=_params(3),
    )(q, k, v, do, lse_row, delta_row)


def _head_tail(o, gb, x1, w_out, final_norm, target, n_seq, seq):
    tm = HEAD_TAIL_TILE
    nt = seq // tm
    n_col = D_MODEL // N_DEV

    head_of_column = np.arange(E_B) // V_HEAD
    selector = jnp.asarray((head_of_column[None, :] == np.arange(N_HEADS)[:, None]).astype(np.float32))

    def body(o_ref, gb_ref, x1_ref, wout_ref, fn_ref, tgt_ref, sel_ref,
             dx2_ref, do_ref, dgb_ref, dr_ref, dwout_ref, dfn_ref, loss_ref):
        first = jnp.logical_and(pl.program_id(0) == 0, pl.program_id(1) == 0)

        @pl.when(first)
        def _():
            dwout_ref[...] = jnp.zeros_like(dwout_ref)
            dfn_ref[...] = jnp.zeros_like(dfn_ref)
            loss_ref[...] = jnp.zeros_like(loss_ref)

        ov, g = o_ref[...], gb_ref[...]
        silu, dsilu = _silu_parts(g)
        gated = (ov * silu).astype(BF16)
        x2 = x1_ref[...] + jnp.dot(gated, wout_ref[...], preferred_element_type=F32)
        r = _rstd(x2)
        xh = x2 * r
        err = xh * fn_ref[...] - tgt_ref[...]
        loss_ref[...] += 0.5 * jnp.sum(jnp.mean(err * err, axis=-1, keepdims=True), axis=0, keepdims=True)
        dy = err / D_MODEL
        dfn_ref[...] += jnp.sum(dy * xh, axis=0, keepdims=True)
        dx2 = _norm_bwd(dy * fn_ref[...], xh, r)
        dx2_ref[...] = dx2
        dx2b = dx2.astype(BF16)
        dw = lax.dot_general(gated, dx2b, TN, preferred_element_type=F32)
        for d in range(N_DEV):
            dwout_ref[d] += dw[:, d * n_col:(d + 1) * n_col]
        dgated = lax.dot_general(dx2b, wout_ref[...], NT, preferred_element_type=F32)
        do = dgated * silu
        dgb_ref[...] = (dgated * ov * dsilu).astype(BF16)
        delta = lax.dot_general(sel_ref[...], do * ov, NT, precision=lax.Precision.HIGHEST, preferred_element_type=F32)
        lane = lax.broadcasted_iota(jnp.int32, (tm, HEAD_PAD), 1)
        low = lane < V_HEAD
        for p in range(N_HEADS // 2):
            do_pair = do[:, p * HEAD_PAD:(p + 1) * HEAD_PAD]
            for hh in range(2):
                h = 2 * p + hh
                mine = do_pair if hh == 0 else pltpu.roll(do_pair, V_HEAD, 1)
                do_ref[:, h * HEAD_PAD:(h + 1) * HEAD_PAD] = jnp.where(low, mine, 0.0).astype(BF16)
                dr_ref[h] = jnp.broadcast_to(delta[h:h + 1, :], (8, tm))

    tok = lambda width: pl.BlockSpec((tm, width), lambda s, i: (s * nt + i, 0))
    t = n_seq * seq
    return pl.pallas_call(
        body, name="head_tail", grid=(n_seq, nt),
        in_specs=[tok(E_B), tok(E_B), tok(D_MODEL), _const_spec(w_out.shape), _const_spec((1, D_MODEL)), tok(D_MODEL),
                  _const_spec((N_HEADS, E_B))],
        out_specs=[tok(D_MODEL), tok(N_HEADS * HEAD_PAD), tok(E_B),
                   pl.BlockSpec((None, N_HEADS, 8, tm), lambda s, i: (s, 0, 0, i)),
                   _acc_spec((N_DEV, E_B, n_col)), _acc_spec((1, D_MODEL)), _acc_spec((1, 1))],
        out_shape=[jax.ShapeDtypeStruct((t, D_MODEL), F32), jax.ShapeDtypeStruct((t, N_HEADS * HEAD_PAD), BF16),
                   jax.ShapeDtypeStruct((t, E_B), BF16),
                   jax.ShapeDtypeStruct((n_seq, N_HEADS, 8, seq), F32), jax.ShapeDtypeStruct((N_DEV, E_B, n_col), F32),
                   jax.ShapeDtypeStruct((1, D_MODEL), F32), jax.ShapeDtypeStruct((1, 1), F32)],
        compiler_params=_params(2),
    )(o, gb, x1, w_out, final_norm, target, selector)


def _adamw_math(w, g, m, v):
    m = ADAM_B1 * m + (1.0 - ADAM_B1) * g
    v = ADAM_B2 * v + (1.0 - ADAM_B2) * jnp.square(g)
    m_hat = m / (1.0 - ADAM_B1 ** ADAM_STEP)
    v_hat = v / (1.0 - ADAM_B2 ** ADAM_STEP)
    delta = -ADAM_LR * (m_hat / (jnp.sqrt(v_hat) + ADAM_EPS) + ADAM_WD * w)
    return delta, m, v


def _adamw_reduce(own, parts, w, m, v):
    rows, cols = w.shape
    br = 256
    n_parts = parts.shape[0]

    def body(own_ref, p_ref, w_ref, m_ref, v_ref, g_ref, d_ref, nm_ref, nv_ref):
        g = own_ref[...]
        for k in range(n_parts):
            g = g + p_ref[k].astype(F32)
        g_ref[...] = g
        d_ref[...], nm_ref[...], nv_ref[...] = _adamw_math(w_ref[...], g, m_ref[...], v_ref[...])

    blk = pl.BlockSpec((br, cols), lambda i: (i, 0))
    return pl.pallas_call(
        body, name="adamw_reduce", grid=(rows // br,),
        in_specs=[blk, pl.BlockSpec((n_parts, br, cols), lambda i: (0, i, 0)), blk, blk, blk],
        out_specs=[blk] * 4, out_shape=[jax.ShapeDtypeStruct((rows, cols), F32)] * 4,
        compiler_params=_params(1),
    )(own, parts, w, m, v)


def _adamw_reduce_many(parts_list, states):
    n = len(parts_list)

    def body(*refs):
        ins, outs = refs[:4 * n], refs[4 * n:]
        for j in range(n):
            p_ref, w_ref, m_ref, v_ref = ins[4 * j:4 * j + 4]
            g = p_ref[0].astype(F32)
            for k in range(1, p_ref.shape[0]):
                g = g + p_ref[k].astype(F32)
            outs[4 * j][...] = g
            outs[4 * j + 1][...], outs[4 * j + 2][...], outs[4 * j + 3][...] = _adamw_math(
                w_ref[...], g, m_ref[...], v_ref[...])

    flat_in = [a for p, st in zip(parts_list, states) for a in (p,) + tuple(st)]
    shapes = [jax.ShapeDtypeStruct(st[0].shape, F32) for st in states for _ in range(4)]
    flat = pl.pallas_call(body, name="adamw_reduce_many", out_shape=shapes,
                          compiler_params=pltpu.CompilerParams(vmem_limit_bytes=VMEM_LIMIT))(*flat_in)
    return [tuple(flat[4 * j:4 * j + 4]) for j in range(n)]


def _sum_parts(parts):
    def body(p_ref, o_ref):
        g = p_ref[0]
        for k in range(1, N_DEV):
            g = g + p_ref[k]
        o_ref[...] = g

    return pl.pallas_call(body, name="small_grad_sum", out_shape=jax.ShapeDtypeStruct(parts.shape[1:], F32))(parts)


def _adamw_small(gs, ws, ms, vs):
    n = len(gs)

    def body(*refs):
        ins, outs = refs[:4 * n], refs[4 * n:]
        for j in range(n):
            g_ref, w_ref, m_ref, v_ref = ins[j], ins[n + j], ins[2 * n + j], ins[3 * n + j]
            outs[3 * j][...], outs[3 * j + 1][...], outs[3 * j + 2][...] = _adamw_math(
                w_ref[...], g_ref[...], m_ref[...], v_ref[...])

    shapes = [jax.ShapeDtypeStruct(g.shape, F32) for g in gs for _ in range(3)]
    flat = pl.pallas_call(body, name="adamw_small", out_shape=shapes)(*gs, *ws, *ms, *vs)
    return [tuple(flat[3 * j:3 * j + 3]) for j in range(n)]


SMALL_ROWS = 8
LOSS_LANE = D_MODEL - 1


def _pack_small(g_fn, g_kvn, g_bn, g_an, g_ckvn, g_qn, g_conv, loss_part):
    def body(fn_ref, kvn_ref, bn_ref, an_ref, ckvn_ref, qn_ref, conv_ref, loss_ref, o_ref):
        o_ref[0:1, :] = fn_ref[...]
        o_ref[1:2, :] = kvn_ref[...]
        o_ref[2:3, :] = bn_ref[...]
        o_ref[3:4, :] = an_ref[...]
        lane = lax.broadcasted_iota(jnp.int32, (1, D_MODEL), 1)
        o_ref[4:5, :] = jnp.where(lane == LOSS_LANE, loss_ref[...], 0.0)
        o_ref[4:5, 0:KV_RANK] = ckvn_ref[...]
        o_ref[4:5, KV_RANK:KV_RANK + Q_RANK] = qn_ref[...]
        o_ref[5:8, :] = conv_ref[0:3, :]

    return pl.pallas_call(body, name="pack_small", out_shape=jax.ShapeDtypeStruct((SMALL_ROWS, D_MODEL), F32))(
        g_fn, g_kvn, g_bn, g_an, g_ckvn, g_qn, g_conv, loss_part)


def _pad_cols(a, width):
    return jnp.pad(a, ((0, 0), (0, width - a.shape[1])))


def _dkv_to_padded(a):
    r = a.shape[0]
    z = jnp.zeros((r, ROPE_LO), a.dtype)
    z2 = jnp.zeros((r, HEAD_PAD - ROPE_LO - QK_ROPE), a.dtype)
    return jnp.concatenate([a[:, :KV_RANK], z, a[:, KV_RANK:], z2], axis=1)


def _dkv_from_padded(a):
    return jnp.concatenate([a[:, :KV_RANK], a[:, KV_RANK + ROPE_LO:KV_RANK + ROPE_LO + QK_ROPE]], axis=1)


def _unstack_cols(a):
    return jnp.transpose(a, (1, 0, 2)).reshape(a.shape[1], N_DEV * a.shape[2])


def kernel(x, positions, a_norm, a_w_in, a_conv, a_w_out, kv_norm, w_dkv, ckv_norm, w_ukv, b_norm, b_w_in, b_q_norm, b_w_uq, b_w_out, final_norm, loss_target, m_a_norm, m_a_w_in, m_a_conv, m_a_w_out, m_kv_norm, m_w_dkv, m_ckv_norm, m_w_ukv, m_b_norm, m_b_w_in, m_b_q_norm, m_b_w_uq, m_b_w_out, m_final_norm, v_a_norm, v_a_w_in, v_a_conv, v_a_w_out, v_kv_norm, v_w_dkv, v_ckv_norm, v_w_ukv, v_b_norm, v_b_w_in, v_b_q_norm, v_b_w_uq, v_b_w_out, v_final_norm):
    n_seq, seq, _ = x.shape
    t = n_seq * seq
    me = 4 * lax.axis_index("x") + 2 * lax.axis_index("y") + lax.axis_index("c")

    big = {
        "a_w_in": (a_w_in[0], m_a_w_in[0], v_a_w_in[0]),
        "a_w_out": (a_w_out[0], m_a_w_out[0], v_a_w_out[0]),
        "w_dkv": tuple(_dkv_to_padded(a) for a in (w_dkv, m_w_dkv, v_w_dkv)),
        "w_ukv": (w_ukv, m_w_ukv, v_w_ukv),
        "b_w_in": (b_w_in[0], m_b_w_in[0], v_b_w_in[0]),
        "b_w_uq": tuple(_pad_cols(a[0], HEAD_PAD) for a in (b_w_uq, m_b_w_uq, v_b_w_uq)),
        "b_w_out": (b_w_out[0], m_b_w_out[0], v_b_w_out[0]),
    }
    names = list(big)
    first_names, later_names = names[:2], names[2:]
    gathered = _all_gather([big[n][0].astype(BF16) for n in first_names] + [a_norm, a_conv[0]])
    a_norm_f = gathered[2].reshape(1, D_MODEL)
    a_conv_f = _unstack_cols(gathered[3])
    w_a_in = gathered[0]
    w_a_out = gathered[1].reshape(D_MODEL, D_MODEL)

    x2d = x.reshape(t, D_MODEL)
    tgt = loss_target.reshape(t, D_MODEL)
    pos = positions.astype(F32).reshape(t, 1)
    rc = _rope_consts()
    kvn, ckvn, bn, qn, fn = kv_norm.reshape(1, -1), ckv_norm.reshape(1, -1), b_norm, b_q_norm, final_norm.reshape(1, -1)

    x1, proj, conv, later = _conv_fwd(x2d, a_norm_f, w_a_in, a_conv_f, w_a_out, n_seq, seq,
                                      [big[n][0].astype(BF16) for n in later_names])
    full = dict(zip(later_names, later))
    w_dkv_f = full["w_dkv"].reshape(D_MODEL, CKR_PAD)
    w_ukv_f = _unstack_cols(full["w_ukv"])
    w_b_in = full["b_w_in"].reshape(D_MODEL, Q_RANK + E_B)
    w_uq_f = _unstack_cols(full["b_w_uq"])
    w_b_out = _unstack_cols(full["b_w_out"])
    layer_b = (kvn, w_dkv_f, ckvn, w_ukv_f, bn, w_b_in, qn, w_uq_f)
    ckr, cq, gb, q, k, v, cosv, sinv = _proj_fwd(x1, pos, rc, *layer_b)
    o, lse_row = _attn_fwd(q, k, v, n_seq, seq)
    dx2, do, dgb, delta_row, g_b_out, g_fn, loss_part = _head_tail(o, gb, x1, w_b_out, fn, tgt, n_seq, seq)
    dk, dv, dq = _attn_bwd(q, k, v, do, lse_row, delta_row, n_seq, seq)
    dx1, g_uq, g_b_in, g_ukv, g_dkv, g_qn, g_bn, g_ckvn, g_kvn = _proj_bwd(
        dq, dk, dv, dgb, cq, ckr, x1, dx2, cosv, sinv, rc, *layer_b)
    stacks = {
        "w_dkv": g_dkv.reshape(N_DEV, D_MODEL // N_DEV, CKR_PAD),
        "w_ukv": g_ukv,
        "b_w_in": g_b_in.reshape(N_DEV, D_MODEL // N_DEV, Q_RANK + E_B),
        "b_w_uq": g_uq,
        "b_w_out": g_b_out,
    }
    (dx, h_a, dproj, g_a_out, g_an, g_conv), later_parts = _conv_bwd(
        dx1, x2d, proj, conv, a_norm_f, w_a_in, a_conv_f, w_a_out, n_seq, seq, [stacks[n] for n in later_names])
    parts = dict(zip(later_names, later_parts))
    small = _pack_small(g_fn, g_kvn, g_bn, g_an, g_ckvn, g_qn, g_conv, loss_part)
    a_in_own, a_in_got, (parts["a_w_out"],), small_parts = _w_in_grad_exchange(
        h_a, dproj, D_MODEL * 4 // N_DEV, (me // 2).reshape(1).astype(jnp.int32),
        [g_a_out.astype(BF16).reshape(N_DEV, D_MODEL // N_DEV, D_MODEL)], small)

    outs = {"a_w_in": _adamw_reduce(a_in_own, a_in_got, *big["a_w_in"])}
    outs.update(zip(names[1:], _adamw_reduce_many([parts[n] for n in names[1:]], [big[n] for n in names[1:]])))
    outs["w_dkv"] = tuple(_dkv_from_padded(a) for a in outs["w_dkv"])
    outs["b_w_uq"] = tuple(a[:, :QK_NOPE + QK_ROPE] for a in outs["b_w_uq"])
    for n in ("a_w_in", "a_w_out", "b_w_in", "b_w_uq", "b_w_out"):
        outs[n] = tuple(a[None] for a in outs[n])

    total = _sum_parts(small_parts)
    loss = total[4, LOSS_LANE]
    shard = D_MODEL // N_DEV
    g_small = {
        "final_norm": total[0], "kv_norm": total[1], "b_norm": total[2:3],
        "a_norm": lax.dynamic_slice_in_dim(total[3:4], me * shard, shard, axis=1),
        "ckv_norm": total[4, 0:KV_RANK], "b_q_norm": total[4:5, KV_RANK:KV_RANK + Q_RANK],
        "a_conv": lax.dynamic_slice_in_dim(total[5:8], me * shard, shard, axis=1)[None],
    }
    small_state = {
        "final_norm": (final_norm, m_final_norm, v_final_norm), "kv_norm": (kv_norm, m_kv_norm, v_kv_norm),
        "b_norm": (b_norm, m_b_norm, v_b_norm), "a_norm": (a_norm, m_a_norm, v_a_norm),
        "ckv_norm": (ckv_norm, m_ckv_norm, v_ckv_norm), "b_q_norm": (b_q_norm, m_b_q_norm, v_b_q_norm),
        "a_conv": (a_conv, m_a_conv, v_a_conv),
    }
    small_names = list(g_small)
    as2d = lambda a: a.reshape(-1, a.shape[-1])
    upd = _adamw_small([as2d(g_small[n]) for n in small_names],
                       *[[as2d(small_state[n][j]) for n in small_names] for j in range(3)])
    for n, u in zip(small_names, upd):
        outs[n] = (g_small[n],) + tuple(a.reshape(g_small[n].shape) for a in u)

    order = ["a_norm", "a_w_in", "a_conv", "a_w_out", "kv_norm", "w_dkv", "ckv_norm", "w_ukv", "b_norm", "b_w_in",
             "b_q_norm", "b_w_uq", "b_w_out", "final_norm"]
    result = [loss, dx.reshape(n_seq, seq, D_MODEL)]
    for j in range(4):
        result += [outs[n][j] for n in order]
    return tuple(result)
```

```python
import functools
import math

import numpy as np
import jax
import jax.numpy as jnp
from jax import lax
from jax.experimental import pallas as pl
from jax.experimental.pallas import tpu as pltpu

F32 = jnp.float32
BF16 = jnp.bfloat16
SAVED = BF16

D_MODEL = 1024
N_HEADS = 8
QK_NOPE = 64
QK_ROPE = 32
V_HEAD = 64
KV_RANK = 256
Q_RANK = 384
E_B = N_HEADS * V_HEAD
HEAD_PAD = 128
CKR_PAD = KV_RANK + HEAD_PAD
ROPE_LO = QK_NOPE
ROPE_HALF = QK_ROPE // 2
ROPE_THETA = 10000.0
SOFTMAX_SCALE = 1.0 / math.sqrt(QK_NOPE + QK_ROPE)
LOG2_E = math.log2(math.e)
LN_2 = math.log(2.0)
EPS = 1e-6
N_DEV = 8

ADAM_LR = 0.001
ADAM_B1 = 0.9
ADAM_B2 = 0.999
ADAM_EPS = 1e-08
ADAM_WD = 0.01
ADAM_STEP = 10

CONV_BWD_TILE = 256
CONV_FWD_TILE = 512
LATENT_BWD_TILE = 512
LATENT_FWD_TILE = 1024
HEAD_TAIL_TILE = 1024
PICKUP_BLOCK = 1
ATTN_TILE = 512
ATTN_GROUP = 4
ATTN_BWD_GROUP = 4
VMEM_LIMIT = 60 * 1024 * 1024

MESH = pl.DeviceIdType.MESH
NT = (((1,), (1,)), ((), ()))
TN = (((0,), (0,)), ((), ()))


def _dot(a, b):
    return jnp.dot(a.astype(BF16), b.astype(BF16), preferred_element_type=F32)


def _dot_nt(a, b):
    return lax.dot_general(a.astype(BF16), b.astype(BF16), NT, preferred_element_type=F32)


def _dot_tn(a, b):
    return lax.dot_general(a.astype(BF16), b.astype(BF16), TN, preferred_element_type=F32)


def _rstd(x):
    return lax.rsqrt(jnp.mean(x * x, axis=-1, keepdims=True) + EPS)


def _norm_bwd(a, xh, r):
    return r * (a - xh * jnp.mean(a * xh, axis=-1, keepdims=True))


def _silu_parts(g):
    sg = jax.nn.sigmoid(g)
    return g * sg, sg * (1.0 + g * (1.0 - sg))


def _rope_consts():
    inv = (ROPE_THETA ** (-np.arange(0, QK_ROPE, 2, dtype=np.float32) / QK_ROPE)).astype(np.float32)
    t = np.zeros((8, HEAD_PAD), np.float32)
    t[0, ROPE_LO:ROPE_LO + ROPE_HALF] = inv
    t[0, ROPE_LO + ROPE_HALF:ROPE_LO + QK_ROPE] = inv
    t[1, ROPE_LO:ROPE_LO + ROPE_HALF] = -1.0
    t[2, ROPE_LO + ROPE_HALF:ROPE_LO + QK_ROPE] = 1.0
    return jnp.asarray(t)


def _rope_angles(pos, rc):
    ang = pos * rc[0:1, :]
    return jnp.cos(ang), jnp.sin(ang)


def _rope_tables(cosv, sinv, rc):
    return cosv, sinv * rc[1:2, :], sinv * rc[2:3, :]


def _rope(x, ct, s1, s2):
    up = pltpu.roll(x, HEAD_PAD - ROPE_HALF, 1)
    dn = pltpu.roll(x, ROPE_HALF, 1)
    return x * ct + up * s1 + dn * s2


def _const_spec(shape):
    nd = len(shape)
    return pl.BlockSpec(shape, lambda *_: (0,) * nd, pipeline_mode=pl.Buffered(1))


def _acc_spec(shape):
    nd = len(shape)
    return pl.BlockSpec(shape, lambda *_: (0,) * nd)


def _params(n_axes, n_independent=0):
    semantics = ("parallel",) * n_independent + ("arbitrary",) * (n_axes - n_independent)
    return pltpu.CompilerParams(dimension_semantics=semantics, vmem_limit_bytes=VMEM_LIMIT)


def _place():
    x, y, c = lax.axis_index("x"), lax.axis_index("y"), lax.axis_index("c")
    return x, y, c, 4 * x + 2 * y + c


def _peer(x, y, c, mask):
    px = 1 - x if mask & 4 else x
    py = 1 - y if mask & 2 else y
    pc = 1 - c if mask & 1 else c
    return (px, py, pc), 4 * px + 2 * py + pc


ANY_SPEC = pl.BlockSpec(memory_space=pl.ANY)


def _comm_sems(n):
    return [pltpu.SemaphoreType.DMA((n, N_DEV - 1)), pltpu.SemaphoreType.DMA((n, N_DEV - 1)), pltpu.SemaphoreType.DMA((n,))]


def _gather_copies(ins, outs, sems):
    send_sems, recv_sems, local_sems = sems
    x, y, c, me = _place()
    starts, waits = [], []
    for w in range(len(ins)):
        mine = pltpu.make_async_copy(ins[w], outs[w].at[me], local_sems.at[w])
        starts.append(mine)
        waits.append(mine)
        for mask in range(1, N_DEV):
            peer, peer_idx = _peer(x, y, c, mask)
            starts.append(pltpu.make_async_remote_copy(
                src_ref=ins[w], dst_ref=outs[w].at[me], send_sem=send_sems.at[w, mask - 1],
                recv_sem=recv_sems.at[w, mask - 1], device_id=peer, device_id_type=MESH))
            waits.append(pltpu.make_async_remote_copy(
                src_ref=ins[w], dst_ref=outs[w].at[peer_idx], send_sem=send_sems.at[w, mask - 1],
                recv_sem=recv_sems.at[w, mask - 1], device_id=peer, device_id_type=MESH))
    return starts, waits


def _scatter_copies(ins, outs, sems):
    send_sems, recv_sems, local_sems = sems
    x, y, c, me = _place()
    copies = []
    for w in range(len(ins)):
        copies.append(pltpu.make_async_copy(ins[w].at[me], outs[w].at[0], local_sems.at[w]))
        for mask in range(1, N_DEV):
            peer, peer_idx = _peer(x, y, c, mask)
            copies.append(pltpu.make_async_remote_copy(
                src_ref=ins[w].at[peer_idx], dst_ref=outs[w].at[mask], send_sem=send_sems.at[w, mask - 1],
                recv_sem=recv_sems.at[w, mask - 1], device_id=peer, device_id_type=MESH))
    return copies, copies


def _stacked(arrays):
    return [jax.ShapeDtypeStruct((N_DEV,) + a.shape, a.dtype) for a in arrays]


def _all_gather(shards):
    n = len(shards)

    def body(*refs):
        ins, outs = refs[:n], refs[n:2 * n]
        send_sems, recv_sems, local_sems = refs[2 * n:]
        x, y, c, me = _place()
        sibling = (x, y, 1 - c)
        chips = [(1 - x, y), (x, 1 - y), (1 - x, 1 - y)]

        def copy(w, k, block, to, src=None):
            idx = 4 * block[0] + 2 * block[1] + block[2]
            return pltpu.make_async_remote_copy(
                src_ref=outs[w].at[idx] if src is None else src, dst_ref=outs[w].at[idx],
                send_sem=send_sems.at[w, k], recv_sem=recv_sems.at[w, k], device_id=to, device_id_type=MESH)

        local, sent = [], []
        for w in range(n):
            mine = pltpu.make_async_copy(ins[w], outs[w].at[me], local_sems.at[w])
            mine.start()
            local.append(mine)
            first = [copy(w, 0, (x, y, c), sibling, src=ins[w])]
            first += [copy(w, 1 + j, (x, y, c), (*chip, c), src=ins[w]) for j, chip in enumerate(chips)]
            for cp in first:
                cp.start()
            sent += first
        for w in range(n):
            for j, chip in enumerate(chips):
                copy(w, 1 + j, (*chip, c), (x, y, c)).wait_recv()
                onward = copy(w, 4 + j, (*chip, c), sibling)
                onward.start()
                sent.append(onward)
        for w in range(n):
            copy(w, 0, sibling, (x, y, c)).wait_recv()
            for j, chip in enumerate(chips):
                copy(w, 4 + j, (*chip, 1 - c), (x, y, c)).wait_recv()
        for cp in sent:
            cp.wait_send()
        for cp in local:
            cp.wait()

    return pl.pallas_call(
        body, name="weight_all_gather", out_shape=_stacked(shards),
        in_specs=[ANY_SPEC] * n, out_specs=[ANY_SPEC] * n, scratch_shapes=_comm_sems(n),
    )(*shards)


def _conv_fwd(x, a_norm, w_in, conv_w, w_out, n_seq, seq, later_shards):
    tm = CONV_FWD_TILE
    nt = seq // tm
    n_col = w_in.shape[2]
    n_later = len(later_shards)

    def body(x_ref, an_ref, win_ref, cw_ref, wout_ref, *rest):
        shard_refs, rest = rest[:n_later], rest[n_later:]
        x1_ref, proj_ref, conv_ref = rest[:3]
        stack_refs, rest = rest[3:3 + n_later], rest[3 + n_later:]
        prev_ref, wide_ref, sems = rest[0], rest[1], rest[2:]
        step = pl.program_id(0) * nt + pl.program_id(1)

        @pl.when(step == 0)
        def _():
            for cp in _gather_copies(shard_refs, stack_refs, sems)[0]:
                cp.start()

        @pl.when(pl.program_id(1) == 0)
        def _():
            prev_ref[...] = jnp.zeros_like(prev_ref)

        xv = x_ref[...]
        h = (xv * _rstd(xv) * an_ref[...]).astype(BF16)
        for d in range(N_DEV):
            part = jnp.dot(h, win_ref[d], preferred_element_type=F32)
            wide_ref[:, d * n_col:(d + 1) * n_col] = part
            proj_ref[:, d * n_col:(d + 1) * n_col] = part.astype(SAVED)
        b = wide_ref[:, 0:D_MODEL]
        v = wide_ref[:, D_MODEL:2 * D_MODEL] * wide_ref[:, 2 * D_MODEL:3 * D_MODEL]
        g = wide_ref[:, 3 * D_MODEL:4 * D_MODEL]
        w0, w1, w2 = cw_ref[0:1, :], cw_ref[1:2, :], cw_ref[2:3, :]
        conv_ref[...] = w0 * pltpu.roll(v, 2, 0) + w1 * pltpu.roll(v, 1, 0) + w2 * v
        rows = lax.broadcasted_iota(jnp.int32, (8, D_MODEL), 0)
        p8, v8 = prev_ref[...], v[0:8]
        back1 = jnp.where(rows < 1, pltpu.roll(p8, 1, 0), pltpu.roll(v8, 1, 0))
        back2 = jnp.where(rows < 2, pltpu.roll(p8, 2, 0), pltpu.roll(v8, 2, 0))
        conv_ref[0:8, :] = w0 * back2 + w1 * back1 + w2 * v8
        prev_ref[...] = v[tm - 8:tm]
        silu, _ = _silu_parts(g)
        yv = silu * b * conv_ref[...]
        x1_ref[...] = xv + _dot(yv, wout_ref[...])

        @pl.when(step == n_seq * nt - 1)
        def _():
            for cp in _gather_copies(shard_refs, stack_refs, sems)[1]:
                cp.wait()

    tok = lambda width: pl.BlockSpec((tm, width), lambda s, i: (s * nt + i, 0))
    t = n_seq * seq
    outs = pl.pallas_call(
        body, name="conv_mixer_fwd", grid=(n_seq, nt),
        in_specs=[tok(D_MODEL), _const_spec((1, D_MODEL)), _const_spec(w_in.shape), _const_spec((3, D_MODEL)),
                  _const_spec(w_out.shape)] + [ANY_SPEC] * n_later,
        out_specs=[tok(D_MODEL), tok(4 * D_MODEL), tok(D_MODEL)] + [ANY_SPEC] * n_later,
        out_shape=[jax.ShapeDtypeStruct((t, D_MODEL), F32), jax.ShapeDtypeStruct((t, 4 * D_MODEL), SAVED),
                   jax.ShapeDtypeStruct((t, D_MODEL), F32)] + _stacked(later_shards),
        scratch_shapes=[pltpu.VMEM((8, D_MODEL), F32), pltpu.VMEM((tm, 4 * D_MODEL), F32)] + _comm_sems(n_later),
        compiler_params=_params(2),
    )(x, a_norm, w_in, conv_w, w_out, *later_shards)
    return outs[0], outs[1], outs[2], outs[3:]


def _conv_bwd(dx1, x, proj, conv, a_norm, w_in, conv_w, w_out, n_seq, seq, ready_stacks):
    tm = CONV_BWD_TILE
    nt = seq // tm
    n_col = w_in.shape[2]
    n_ready = len(ready_stacks)

    def body(dx1_ref, x_ref, proj_ref, conv_ref, an_ref, win_ref, cw_ref, wout_ref, *rest):
        ready_refs, rest = rest[:n_ready], rest[n_ready:]
        dx_ref, h_ref, dproj_ref, dwout_ref, dan_ref, dcw_ref = rest[:6]
        part_refs, rest = rest[6:6 + n_ready], rest[6 + n_ready:]
        next_ref, d1_ref, d2_ref = rest[:3]
        sems = rest[3:]
        step = pl.program_id(0) * nt + pl.program_id(1)
        first = step == 0

        @pl.when(first)
        def _():
            for cp in _scatter_copies(ready_refs, part_refs, sems)[0]:
                cp.start()
            dwout_ref[...] = jnp.zeros_like(dwout_ref)
            dan_ref[...] = jnp.zeros_like(dan_ref)
            dcw_ref[...] = jnp.zeros_like(dcw_ref)

        @pl.when(pl.program_id(1) == 0)
        def _():
            next_ref[...] = jnp.zeros_like(next_ref)

        dx1v = dx1_ref[...]
        dy = _dot_nt(dx1v, wout_ref[...])
        b = proj_ref[:, 0:D_MODEL].astype(F32)
        cc = proj_ref[:, D_MODEL:2 * D_MODEL].astype(F32)
        u = proj_ref[:, 2 * D_MODEL:3 * D_MODEL].astype(F32)
        g = proj_ref[:, 3 * D_MODEL:4 * D_MODEL].astype(F32)
        cv = conv_ref[...]
        silu, dsilu = _silu_parts(g)
        per_part = D_MODEL // n_col

        def back_through_w_in(part, grad):
            grad = grad.astype(BF16)
            dproj_ref[:, part * D_MODEL:(part + 1) * D_MODEL] = grad
            term = None
            for j in range(per_part):
                d = part * per_part + j
                piece = lax.dot_general(grad[:, j * n_col:(j + 1) * n_col], win_ref[d], NT, preferred_element_type=F32)
                term = piece if term is None else term + piece
            return term

        dh = back_through_w_in(0, dy * silu * cv)
        dh += back_through_w_in(3, dy * b * cv * dsilu)
        dwout_ref[...] += _dot_tn(silu * b * cv, dx1v)
        dconv = dy * silu * b
        d1_ref[...] = pltpu.roll(dconv, tm - 1, 0)
        d2_ref[...] = pltpu.roll(dconv, tm - 2, 0)
        rows = lax.broadcasted_iota(jnp.int32, (8, D_MODEL), 0)
        n8, c8 = next_ref[...], dconv[tm - 8:tm]
        d1_ref[tm - 8:tm, :] = jnp.where(rows >= 7, pltpu.roll(n8, 7, 0), pltpu.roll(c8, 7, 0))
        d2_ref[tm - 8:tm, :] = jnp.where(rows >= 6, pltpu.roll(n8, 6, 0), pltpu.roll(c8, 6, 0))
        next_ref[...] = dconv[0:8]
        d1, d2 = d1_ref[...], d2_ref[...]
        v = cc * u
        dcw_ref[0:1, :] += jnp.sum(d2 * v, axis=0, keepdims=True)
        dcw_ref[1:2, :] += jnp.sum(d1 * v, axis=0, keepdims=True)
        dcw_ref[2:3, :] += jnp.sum(dconv * v, axis=0, keepdims=True)
        dv = cw_ref[0:1, :] * d2 + cw_ref[1:2, :] * d1 + cw_ref[2:3, :] * dconv
        dh += back_through_w_in(1, dv * u)
        dh += back_through_w_in(2, dv * cc)
        xv = x_ref[...]
        r = _rstd(xv)
        xh = xv * r
        h_ref[...] = (xh * an_ref[...]).T.astype(BF16)
        dan_ref[...] += jnp.sum(dh * xh, axis=0, keepdims=True)
        dx_ref[...] = dx1v + _norm_bwd(dh * an_ref[...], xh, r)

        @pl.when(step == n_seq * nt - 1)
        def _():
            for cp in _scatter_copies(ready_refs, part_refs, sems)[1]:
                cp.wait()

    tok = lambda width: pl.BlockSpec((tm, width), lambda s, i: (s * nt + nt - 1 - i, 0))
    t = n_seq * seq
    outs = pl.pallas_call(
        body, name="conv_mixer_bwd", grid=(n_seq, nt),
        in_specs=[tok(D_MODEL), tok(D_MODEL), tok(4 * D_MODEL), tok(D_MODEL), _const_spec((1, D_MODEL)),
                  _const_spec(w_in.shape), _const_spec((3, D_MODEL)), _const_spec(w_out.shape)] + [ANY_SPEC] * n_ready,
        out_specs=[tok(D_MODEL), pl.BlockSpec((D_MODEL, tm), lambda s, i: (0, s * nt + nt - 1 - i)),
                   tok(4 * D_MODEL), _acc_spec((D_MODEL, D_MODEL)),
                   _acc_spec((1, D_MODEL)), _acc_spec((8, D_MODEL))] + [ANY_SPEC] * n_ready,
        out_shape=[jax.ShapeDtypeStruct((t, D_MODEL), F32), jax.ShapeDtypeStruct((D_MODEL, t), BF16),
                   jax.ShapeDtypeStruct((t, 4 * D_MODEL), BF16), jax.ShapeDtypeStruct((D_MODEL, D_MODEL), F32),
                   jax.ShapeDtypeStruct((1, D_MODEL), F32), jax.ShapeDtypeStruct((8, D_MODEL), F32)]
        + [jax.ShapeDtypeStruct(a.shape, a.dtype) for a in ready_stacks],
        scratch_shapes=[pltpu.VMEM((8, D_MODEL), F32), pltpu.VMEM((tm, D_MODEL), F32), pltpu.VMEM((tm, D_MODEL), F32)]
        + _comm_sems(n_ready),
        compiler_params=_params(2),
    )(dx1, x, proj, conv, a_norm, w_in, conv_w, w_out, *ready_stacks)
    return outs[:6], outs[6:]


def _w_in_grad_exchange(a_t, b, n_col, chip, ready_stacks, small):
    r, t = a_t.shape
    bt = 4096
    nk = t // bt
    n_ready = len(ready_stacks)
    n_chip = N_DEV // 2
    n_remote = N_DEV - 2

    def body(chip_ref, a_ref, b_ref, *rest):
        ready_refs, small_ref, rest = rest[:n_ready], rest[n_ready], rest[n_ready + 1:]
        own_ref, got_ref, rest = rest[0], rest[1], rest[2:]
        part_refs, small_all_ref, rest = rest[:n_ready], rest[n_ready], rest[n_ready + 1:]
        acc_ref, land_ref, stage_ref, pair_send, pair_recv, chip_send, chip_recv = rest[:7]
        scatter_sems, gather_sems = rest[7:10], rest[10:13]
        s, k = pl.program_id(0), pl.program_id(1)
        x, y, c, _ = _place()
        chip = 2 * x + y

        @pl.when(jnp.logical_and(s == 0, k == 0))
        def _():
            for cp in _scatter_copies(ready_refs, part_refs, scatter_sems)[0]:
                cp.start()
            for cp in _gather_copies([small_ref], [small_all_ref], gather_sems)[0]:
                cp.start()

        for parity in range(2):
            @pl.when(s % 2 == parity)
            def _(parity=parity):
                @pl.when(k == 0)
                def _():
                    acc_ref[parity] = jnp.zeros((r, n_col), F32)

                acc_ref[parity] += jnp.dot(a_ref[...], b_ref[...], preferred_element_type=F32)

        def to_sibling(step):
            return pltpu.make_async_remote_copy(
                src_ref=acc_ref.at[step % 2], dst_ref=land_ref, send_sem=pair_send.at[step], recv_sem=pair_recv.at[step],
                device_id=(x, y, 1 - c), device_id_type=MESH)

        def to_owner(nth):
            owner_chip = (chip + 1 + nth) % n_chip
            slot = jnp.bitwise_xor(chip, owner_chip) - 1
            return pltpu.make_async_remote_copy(
                src_ref=stage_ref.at[nth % 2], dst_ref=got_ref.at[slot], send_sem=chip_send.at[nth],
                recv_sem=chip_recv.at[slot], device_id=(owner_chip // 2, owner_chip % 2, c), device_id_type=MESH)

        for step in range(N_DEV):
            owner_core = step % 2

            @pl.when(jnp.logical_and(s == step, k == nk - 1))
            def _(step=step, owner_core=owner_core):
                @pl.when(c != owner_core)
                def _():
                    to_sibling(step).start()

                if step >= 1:
                    @pl.when(c == owner_core)
                    def _():
                        to_sibling(step - 1).wait_send()

            if step < N_DEV - 1:
                pickup = jnp.logical_and(s == step + 1, k == min(PICKUP_BLOCK, nk - 1))
            else:
                pickup = jnp.logical_and(s == step, k == nk - 1)

            @pl.when(jnp.logical_and(pickup, c == owner_core))
            def _(step=step):
                to_sibling(step).wait_recv()
                total = acc_ref[step % 2] + land_ref[...]
                if step < n_remote:
                    nth = step // 2
                    if nth >= 2:
                        to_owner(nth - 2).wait_send()
                    stage_ref[nth % 2] = total.astype(BF16)
                    to_owner(nth).start()
                else:
                    own_ref[...] = total

        @pl.when(jnp.logical_and(s == N_DEV - 1, k == nk - 1))
        def _():
            @pl.when(c != (N_DEV - 1) % 2)
            def _():
                to_sibling(N_DEV - 1).wait_send()

            to_owner(1).wait_send()
            to_owner(2).wait_send()
            for slot in range(n_chip - 1):
                pltpu.make_async_remote_copy(
                    src_ref=stage_ref.at[0], dst_ref=got_ref.at[slot], send_sem=chip_send.at[0],
                    recv_sem=chip_recv.at[slot], device_id=(x, y, c), device_id_type=MESH).wait_recv()
            for cp in _scatter_copies(ready_refs, part_refs, scatter_sems)[1]:
                cp.wait()
            for cp in _gather_copies([small_ref], [small_all_ref], gather_sems)[1]:
                cp.wait()

    def owner_block(s, k, chip):
        return (k, 2 * ((chip[0] + 1 + s // 2) % n_chip) + s % 2)

    grid_spec = pltpu.PrefetchScalarGridSpec(
        num_scalar_prefetch=1, grid=(N_DEV, nk),
        in_specs=[pl.BlockSpec((r, bt), lambda s, k, chip: (0, k)), pl.BlockSpec((bt, n_col), owner_block)]
        + [ANY_SPEC] * (n_ready + 1),
        out_specs=[pl.BlockSpec((r, n_col), lambda s, k, chip: (0, 0)), ANY_SPEC] + [ANY_SPEC] * (n_ready + 1),
        scratch_shapes=[pltpu.VMEM((2, r, n_col), F32), pltpu.VMEM((r, n_col), F32), pltpu.VMEM((2, r, n_col), BF16),
                        pltpu.SemaphoreType.DMA((N_DEV,)), pltpu.SemaphoreType.DMA((N_DEV,)),
                        pltpu.SemaphoreType.DMA((n_chip - 1,)), pltpu.SemaphoreType.DMA((n_chip - 1,))]
        + _comm_sems(n_ready) + _comm_sems(1))
    outs = pl.pallas_call(
        body, name="w_in_grad_exchange", grid_spec=grid_spec,
        out_shape=[jax.ShapeDtypeStruct((r, n_col), F32), jax.ShapeDtypeStruct((n_chip - 1, r, n_col), BF16)]
        + [jax.ShapeDtypeStruct(p.shape, p.dtype) for p in ready_stacks] + _stacked([small]),
        compiler_params=_params(2),
    )(chip, a_t, b, *ready_stacks, small)
    return outs[0], outs[1], outs[2:2 + n_ready], outs[2 + n_ready]


def _proj_fwd(x1, pos, rc, kv_norm, w_dkv, ckv_norm, w_ukv, b_norm, b_w_in, q_norm, w_uq):
    t = x1.shape[0]
    tm = LATENT_FWD_TILE

    def body(x1_ref, pos_ref, rc_ref, kvn_ref, wdkv_ref, ckvn_ref, wukv_ref, bn_ref, bwin_ref, qn_ref, wuq_ref,
             ckr_ref, cq_ref, gb_ref, q_ref, k_ref, v_ref, cos_ref, sin_ref):
        xv = x1_ref[...]
        xh = xv * _rstd(xv)
        ckr = _dot(xh * kvn_ref[...], wdkv_ref[...])
        pb = _dot(xh * bn_ref[...], bwin_ref[...])
        ckr_ref[...] = ckr
        ckv = ckr[:, 0:KV_RANK]
        kv = _dot(ckv * _rstd(ckv) * ckvn_ref[...], wukv_ref[...])
        cosv, sinv = _rope_angles(pos_ref[...], rc_ref[...])
        cos_ref[...] = cosv
        sin_ref[...] = sinv
        ct, s1, s2 = _rope_tables(cosv, sinv, rc_ref[...])
        k_rope = _rope(ckr[:, KV_RANK:CKR_PAD], ct, s1, s2)
        lane = lax.broadcasted_iota(jnp.int32, (tm, HEAD_PAD), 1)
        low = lane < QK_NOPE
        for h in range(N_HEADS):
            kv_h = kv[:, h * HEAD_PAD:(h + 1) * HEAD_PAD]
            k_ref[:, h * HEAD_PAD:(h + 1) * HEAD_PAD] = jnp.where(low, kv_h, k_rope).astype(BF16)
            v_ref[:, h * HEAD_PAD:(h + 1) * HEAD_PAD] = jnp.where(low, pltpu.roll(kv_h, V_HEAD, 1), 1.0).astype(BF16)
        cq = pb[:, 0:Q_RANK]
        cq_ref[...] = cq
        gb_ref[...] = pb[:, Q_RANK:Q_RANK + E_B]
        q = _dot(cq * _rstd(cq) * qn_ref[...], wuq_ref[...])
        for h in range(N_HEADS):
            q_ref[:, h * HEAD_PAD:(h + 1) * HEAD_PAD] = _rope(
                q[:, h * HEAD_PAD:(h + 1) * HEAD_PAD], ct, s1, s2).astype(BF16)

    tok = lambda width: pl.BlockSpec((tm, width), lambda i: (i, 0))
    weights = [kv_norm, w_dkv, ckv_norm, w_ukv, b_norm, b_w_in, q_norm, w_uq]
    wide = N_HEADS * HEAD_PAD
    return pl.pallas_call(
        body, name="latent_proj_fwd", grid=(t // tm,),
        in_specs=[tok(D_MODEL), tok(1), _const_spec(rc.shape)] + [_const_spec(w.shape) for w in weights],
        out_specs=[tok(CKR_PAD), tok(Q_RANK), tok(E_B), tok(wide), tok(wide), tok(wide), tok(HEAD_PAD), tok(HEAD_PAD)],
        out_shape=[jax.ShapeDtypeStruct((t, CKR_PAD), F32), jax.ShapeDtypeStruct((t, Q_RANK), F32),
                   jax.ShapeDtypeStruct((t, E_B), F32), jax.ShapeDtypeStruct((t, wide), BF16),
                   jax.ShapeDtypeStruct((t, wide), BF16), jax.ShapeDtypeStruct((t, wide), BF16),
                   jax.ShapeDtypeStruct((t, HEAD_PAD), F32), jax.ShapeDtypeStruct((t, HEAD_PAD), F32)],
        compiler_params=_params(1),
    )(x1, pos, rc, *weights)


def _proj_bwd(dq, dk, dv, dgb, cq, ckr, x1, dx2, cosv, sinv, rc, kv_norm, w_dkv, ckv_norm, w_ukv, b_norm, b_w_in, q_norm, w_uq):
    t = x1.shape[0]
    tm = LATENT_BWD_TILE
    wide = N_HEADS * HEAD_PAD

    def body(dq_ref, dk_ref, dv_ref, dgb_ref, cq_ref, ckr_ref, x1_ref, dx2_ref, cos_ref, sin_ref, rc_ref,
             kvn_ref, wdkv_ref, ckvn_ref, wukv_ref, bn_ref, bwin_ref, qn_ref, wuq_ref,
             dx1_ref, dwuq_ref, dwbin_ref, dwukv_ref, dwdkv_ref, dqn_ref, dbn_ref, dckvn_ref, dkvn_ref, dqu_ref, dkv_ref):
        @pl.when(pl.program_id(0) == 0)
        def _():
            for ref in (dwuq_ref, dwbin_ref, dwukv_ref, dwdkv_ref, dqn_ref, dbn_ref, dckvn_ref, dkvn_ref):
                ref[...] = jnp.zeros_like(ref)

        ct, s1, s2 = _rope_tables(cos_ref[...], sin_ref[...], rc_ref[...])
        lane = lax.broadcasted_iota(jnp.int32, (tm, HEAD_PAD), 1)
        low = lane < QK_NOPE
        xv = x1_ref[...]
        r = _rstd(xv)
        xh = xv * r
        dk_rope = jnp.zeros((tm, HEAD_PAD), F32)
        for h in range(N_HEADS):
            dk_h = dk_ref[:, h * HEAD_PAD:(h + 1) * HEAD_PAD].astype(F32)
            dv_h = dv_ref[:, h * HEAD_PAD:(h + 1) * HEAD_PAD].astype(F32)
            dkv_ref[:, h * HEAD_PAD:(h + 1) * HEAD_PAD] = jnp.where(low, dk_h, pltpu.roll(dv_h, V_HEAD, 1)).astype(BF16)
            dk_rope += dk_h
        rope_lanes = jnp.logical_and(lane >= ROPE_LO, lane < ROPE_LO + QK_ROPE)
        dk_rope = jnp.where(rope_lanes, _rope(dk_rope, ct, -s1, -s2), 0.0)
        ckv = ckr_ref[:, 0:KV_RANK]
        rk = _rstd(ckv)
        ckh = ckv * rk
        ckvn = (ckh * ckvn_ref[...]).astype(BF16)
        dwukv = lax.dot_general(ckvn, dkv_ref[...], TN, preferred_element_type=F32)
        for h in range(N_HEADS):
            dwukv_ref[h] += dwukv[:, h * HEAD_PAD:(h + 1) * HEAD_PAD]
        dckvn = lax.dot_general(dkv_ref[...], wukv_ref[...], NT, preferred_element_type=F32)
        dckvn_ref[...] += jnp.sum(dckvn * ckh, axis=0, keepdims=True)
        dckv = _norm_bwd(dckvn * ckvn_ref[...], ckh, rk)
        dckr = jnp.concatenate([dckv, dk_rope], axis=1).astype(BF16)
        dwdkv_ref[...] += _dot_tn(xh * kvn_ref[...], dckr)
        dh2 = lax.dot_general(dckr, wdkv_ref[...], NT, preferred_element_type=F32)
        for h in range(N_HEADS):
            dqu_ref[:, h * HEAD_PAD:(h + 1) * HEAD_PAD] = _rope(
                dq_ref[:, h * HEAD_PAD:(h + 1) * HEAD_PAD].astype(F32), ct, -s1, -s2).astype(BF16)
        cq = cq_ref[...]
        rq = _rstd(cq)
        cqh = cq * rq
        cqn = (cqh * qn_ref[...]).astype(BF16)
        dwuq = lax.dot_general(cqn, dqu_ref[...], TN, preferred_element_type=F32)
        for h in range(N_HEADS):
            dwuq_ref[h] += dwuq[:, h * HEAD_PAD:(h + 1) * HEAD_PAD]
        dcqn = lax.dot_general(dqu_ref[...], wuq_ref[...], NT, preferred_element_type=F32)
        dqn_ref[...] += jnp.sum(dcqn * cqh, axis=0, keepdims=True)
        dcq = _norm_bwd(dcqn * qn_ref[...], cqh, rq)
        dpb = jnp.concatenate([dcq.astype(BF16), dgb_ref[...]], axis=1)
        dwbin_ref[...] += _dot_tn(xh * bn_ref[...], dpb)
        dh3 = lax.dot_general(dpb, bwin_ref[...], NT, preferred_element_type=F32)
        dkvn_ref[...] += jnp.sum(dh2 * xh, axis=0, keepdims=True)
        dbn_ref[...] += jnp.sum(dh3 * xh, axis=0, keepdims=True)
        dx1_ref[...] = dx2_ref[...] + _norm_bwd(dh2 * kvn_ref[...] + dh3 * bn_ref[...], xh, r)

    tok = lambda width: pl.BlockSpec((tm, width), lambda i: (i, 0))
    weights = [kv_norm, w_dkv, ckv_norm, w_ukv, b_norm, b_w_in, q_norm, w_uq]
    acc_shapes = [(N_HEADS, Q_RANK, HEAD_PAD), (D_MODEL, Q_RANK + E_B), (N_HEADS, KV_RANK, HEAD_PAD), (D_MODEL, CKR_PAD),
                  (1, Q_RANK), (1, D_MODEL), (1, KV_RANK), (1, D_MODEL)]
    return pl.pallas_call(
        body, name="latent_proj_bwd", grid=(t // tm,),
        in_specs=[tok(wide), tok(wide), tok(wide), tok(E_B), tok(Q_RANK), tok(CKR_PAD), tok(D_MODEL), tok(D_MODEL),
                  tok(HEAD_PAD), tok(HEAD_PAD), _const_spec(rc.shape)] + [_const_spec(w.shape) for w in weights],
        out_specs=[tok(D_MODEL)] + [_acc_spec(s) for s in acc_shapes],
        out_shape=[jax.ShapeDtypeStruct((t, D_MODEL), F32)] + [jax.ShapeDtypeStruct(s, F32) for s in acc_shapes],
        scratch_shapes=[pltpu.VMEM((tm, wide), BF16), pltpu.VMEM((tm, wide), BF16)],
        compiler_params=_params(1),
    )(dq, dk, dv, dgb, cq, ckr, x1, dx2, cosv, sinv, rc, *weights)


def _attn_fwd(q, k, v, n_seq, seq):
    tb, group = ATTN_TILE, ATTN_GROUP
    rows = tb * group
    nq = seq // rows
    pair = 2 * HEAD_PAD

    def body(q_ref, k_ref, v_ref, o_ref, lr_ref):
        i = pl.program_id(2)
        row = lax.broadcasted_iota(jnp.int32, (tb, tb), 0)
        col = lax.broadcasted_iota(jnp.int32, (tb, tb), 1)
        causal = col <= row

        def step(j, carry, kinds):
            start = pl.multiple_of(j * tb, tb)
            out = []
            for g in range(group):
                for hh in range(2):
                    m, acc = carry[2 * g + hh]
                    if kinds[g] != "none":
                        heads = slice(hh * HEAD_PAD, (hh + 1) * HEAD_PAD)
                        s = lax.dot_general(q_ref[g * tb:(g + 1) * tb, heads], k_ref[pl.ds(start, tb), heads], NT,
                                            preferred_element_type=F32) * (SOFTMAX_SCALE * LOG2_E)
                        if kinds[g] == "diagonal":
                            s = jnp.where(causal, s, -jnp.inf)
                        m_new = jnp.maximum(m, jnp.max(s, axis=-1, keepdims=True))
                        p = jnp.exp2(s - m_new)
                        acc = jnp.exp2(m - m_new) * acc + jnp.dot(
                            p.astype(BF16), v_ref[pl.ds(start, tb), heads], preferred_element_type=F32)
                        m = m_new
                    out.append((m, acc))
            return tuple(out)

        one = (jnp.full((tb, 1), -jnp.inf, F32), jnp.zeros((tb, HEAD_PAD), F32))
        carry = lax.fori_loop(0, i * group, functools.partial(step, kinds=("full",) * group), (one,) * (2 * group))
        for d in range(group):
            kinds = tuple("none" if g < d else ("diagonal" if g == d else "full") for g in range(group))
            carry = step(i * group + d, carry, kinds)
        lane = lax.broadcasted_iota(jnp.int32, (tb, HEAD_PAD), 1)
        low = lane < V_HEAD
        for g in range(group):
            halves = []
            for hh in range(2):
                m, acc = carry[2 * g + hh]
                swapped = pltpu.roll(acc, V_HEAD, 1)
                halves.append(acc / swapped)
                lse = (m * LN_2) + jnp.log(jnp.where(low, swapped, acc))
                lr_ref[hh, :, g * tb:(g + 1) * tb] = lse.T[0:8, :]
            o_ref[g * tb:(g + 1) * tb, :] = jnp.where(low, halves[0], pltpu.roll(halves[1], V_HEAD, 1))

    t = n_seq * seq
    return pl.pallas_call(
        body, name="attention_fwd", grid=(n_seq, N_HEADS // 2, nq),
        in_specs=[pl.BlockSpec((rows, pair), lambda s, p, i: (s * nq + i, p)),
                  pl.BlockSpec((seq, pair), lambda s, p, i: (s, p)),
                  pl.BlockSpec((seq, pair), lambda s, p, i: (s, p))],
        out_specs=[pl.BlockSpec((rows, HEAD_PAD), lambda s, p, i: (s * nq + i, p)),
                   pl.BlockSpec((None, 2, 8, rows), lambda s, p, i: (s, p, 0, i))],
        out_shape=[jax.ShapeDtypeStruct((t, E_B), F32), jax.ShapeDtypeStruct((n_seq, N_HEADS, 8, seq), F32)],
        compiler_params=_params(3, n_independent=3),
    )(q, k, v)


def _attn_bwd(q, k, v, do, lse_row, delta_row, n_seq, seq):
    tb, group = ATTN_TILE, ATTN_BWD_GROUP
    rows = tb * group
    nk = seq // rows
    n_inner = seq // tb
    pair = 2 * HEAD_PAD

    def body(q_ref, k_ref, v_ref, do_ref, lr_ref, dr_ref, dk_ref, dv_ref, dq_out_ref, dq_ref):
        j = pl.program_id(2)

        @pl.when(j == 0)
        def _():
            dq_ref[...] = jnp.zeros_like(dq_ref)

        row = lax.broadcasted_iota(jnp.int32, (tb, tb), 0)
        col = lax.broadcasted_iota(jnp.int32, (tb, tb), 1)
        causal = col >= row

        def step(i, carry, kinds):
            start = pl.multiple_of(i * tb, tb)
            out = []
            for hh in range(2):
                heads = slice(hh * HEAD_PAD, (hh + 1) * HEAD_PAD)
                qi = q_ref[pl.ds(start, tb), heads]
                doi = do_ref[pl.ds(start, tb), heads]
                dq = None
                for g in range(group):
                    dk_acc, dv_acc = carry[hh * group + g]
                    if kinds[g] != "none":
                        kg = k_ref[g * tb:(g + 1) * tb, heads]
                        st = lax.dot_general(kg, qi, NT, preferred_element_type=F32) * SOFTMAX_SCALE
                        pt = jnp.exp(st - lr_ref[hh, 0:1, pl.ds(start, tb)])
                        if kinds[g] == "diagonal":
                            pt = jnp.where(causal, pt, 0.0)
                        dv_acc = dv_acc + jnp.dot(pt.astype(BF16), doi, preferred_element_type=F32)
                        dpt = lax.dot_general(v_ref[g * tb:(g + 1) * tb, heads], doi, NT, preferred_element_type=F32)
                        dst = (pt * (dpt - dr_ref[hh, 0:1, pl.ds(start, tb)]) * SOFTMAX_SCALE).astype(BF16)
                        dk_acc = dk_acc + jnp.dot(dst, qi, preferred_element_type=F32)
                        term = lax.dot_general(dst, kg, TN, preferred_element_type=F32)
                        dq = term if dq is None else dq + term
                    out.append((dk_acc, dv_acc))
                dq_ref[pl.ds(start, tb), heads] += dq
            return tuple(out)

        one = (jnp.zeros((tb, HEAD_PAD), F32), jnp.zeros((tb, HEAD_PAD), F32))
        carry = (one,) * (2 * group)
        for d in range(group):
            kinds = tuple("full" if g < d else ("diagonal" if g == d else "none") for g in range(group))
            carry = step(j * group + d, carry, kinds)
        carry = lax.fori_loop((j + 1) * group, n_inner, functools.partial(step, kinds=("full",) * group), carry)
        for hh in range(2):
            for g in range(group):
                dk_ref[g * tb:(g + 1) * tb, hh * HEAD_PAD:(hh + 1) * HEAD_PAD] = carry[hh * group + g][0].astype(BF16)
                dv_ref[g * tb:(g + 1) * tb, hh * HEAD_PAD:(hh + 1) * HEAD_PAD] = carry[hh * group + g][1].astype(BF16)

        @pl.when(j == nk - 1)
        def _():
            dq_out_ref[...] = dq_ref[...].astype(BF16)

    t = n_seq * seq
    wide = N_HEADS * HEAD_PAD
    return pl.pallas_call(
        body, name="attention_bwd", grid=(n_seq, N_HEADS // 2, nk),
        in_specs=[pl.BlockSpec((seq, pair), lambda s, p, j: (s, p)),
                  pl.BlockSpec((rows, pair), lambda s, p, j: (s * nk + j, p)),
                  pl.BlockSpec((rows, pair), lambda s, p, j: (s * nk + j, p)),
                  pl.BlockSpec((seq, pair), lambda s, p, j: (s, p)),
                  pl.BlockSpec((None, 2, 8, seq), lambda s, p, j: (s, p, 0, 0)),
                  pl.BlockSpec((None, 2, 8, seq), lambda s, p, j: (s, p, 0, 0))],
        out_specs=[pl.BlockSpec((rows, pair), lambda s, p, j: (s * nk + j, p)),
                   pl.BlockSpec((rows, pair), lambda s, p, j: (s * nk + j, p)),
                   pl.BlockSpec((seq, pair), lambda s, p, j: (s, p))],
        out_shape=[jax.ShapeDtypeStruct((t, wide), BF16), jax.ShapeDtypeStruct((t, wide), BF16),
                   jax.ShapeDtypeStruct((t, wide), BF16)],
        scratch_shapes=[pltpu.VMEM((seq, pair), F32)],
        compiler_params=_params(3, n_independent=2),
    )(q, k, v, do, lse_row, delta_row)


def _head_tail(o, gb, x1, w_out, final_norm, target, n_seq, seq):
    tm = HEAD_TAIL_TILE
    nt = seq // tm
    n_col = D_MODEL // N_DEV

    head_of_column = np.arange(E_B) // V_HEAD
    selector = jnp.asarray((head_of_column[None, :] == np.arange(N_HEADS)[:, None]).astype(np.float32))

    def body(o_ref, gb_ref, x1_ref, wout_ref, fn_ref, tgt_ref, sel_ref,
             dx2_ref, do_ref, dgb_ref, dr_ref, dwout_ref, dfn_ref, loss_ref):
        first = jnp.logical_and(pl.program_id(0) == 0, pl.program_id(1) == 0)

        @pl.when(first)
        def _():
            dwout_ref[...] = jnp.zeros_like(dwout_ref)
            dfn_ref[...] = jnp.zeros_like(dfn_ref)
            loss_ref[...] = jnp.zeros_like(loss_ref)

        ov, g = o_ref[...], gb_ref[...]
        silu, dsilu = _silu_parts(g)
        gated = (ov * silu).astype(BF16)
        x2 = x1_ref[...] + jnp.dot(gated, wout_ref[...], preferred_element_type=F32)
        r = _rstd(x2)
        xh = x2 * r
        err = xh * fn_ref[...] - tgt_ref[...]
        loss_ref[...] += 0.5 * jnp.sum(jnp.mean(err * err, axis=-1, keepdims=True), axis=0, keepdims=True)
        dy = err / D_MODEL
        dfn_ref[...] += jnp.sum(dy * xh, axis=0, keepdims=True)
        dx2 = _norm_bwd(dy * fn_ref[...], xh, r)
        dx2_ref[...] = dx2
        dx2b = dx2.astype(BF16)
        dw = lax.dot_general(gated, dx2b, TN, preferred_element_type=F32)
        for d in range(N_DEV):
            dwout_ref[d] += dw[:, d * n_col:(d + 1) * n_col]
        dgated = lax.dot_general(dx2b, wout_ref[...], NT, preferred_element_type=F32)
        do = dgated * silu
        dgb_ref[...] = (dgated * ov * dsilu).astype(BF16)
        delta = lax.dot_general(sel_ref[...], do * ov, NT, precision=lax.Precision.HIGHEST, preferred_element_type=F32)
        lane = lax.broadcasted_iota(jnp.int32, (tm, HEAD_PAD), 1)
        low = lane < V_HEAD
        for p in range(N_HEADS // 2):
            do_pair = do[:, p * HEAD_PAD:(p + 1) * HEAD_PAD]
            for hh in range(2):
                h = 2 * p + hh
                mine = do_pair if hh == 0 else pltpu.roll(do_pair, V_HEAD, 1)
                do_ref[:, h * HEAD_PAD:(h + 1) * HEAD_PAD] = jnp.where(low, mine, 0.0).astype(BF16)
                dr_ref[h] = jnp.broadcast_to(delta[h:h + 1, :], (8, tm))

    tok = lambda width: pl.BlockSpec((tm, width), lambda s, i: (s * nt + i, 0))
    t = n_seq * seq
    return pl.pallas_call(
        body, name="head_tail", grid=(n_seq, nt),
        in_specs=[tok(E_B), tok(E_B), tok(D_MODEL), _const_spec(w_out.shape), _const_spec((1, D_MODEL)), tok(D_MODEL),
                  _const_spec((N_HEADS, E_B))],
        out_specs=[tok(D_MODEL), tok(N_HEADS * HEAD_PAD), tok(E_B),
                   pl.BlockSpec((None, N_HEADS, 8, tm), lambda s, i: (s, 0, 0, i)),
                   _acc_spec((N_DEV, E_B, n_col)), _acc_spec((1, D_MODEL)), _acc_spec((1, 1))],
        out_shape=[jax.ShapeDtypeStruct((t, D_MODEL), F32), jax.ShapeDtypeStruct((t, N_HEADS * HEAD_PAD), BF16),
                   jax.ShapeDtypeStruct((t, E_B), BF16),
                   jax.ShapeDtypeStruct((n_seq, N_HEADS, 8, seq), F32), jax.ShapeDtypeStruct((N_DEV, E_B, n_col), F32),
                   jax.ShapeDtypeStruct((1, D_MODEL), F32), jax.ShapeDtypeStruct((1, 1), F32)],
        compiler_params=_params(2),
    )(o, gb, x1, w_out, final_norm, target, selector)


def _adamw_math(w, g, m, v):
    m = ADAM_B1 * m + (1.0 - ADAM_B1) * g
    v = ADAM_B2 * v + (1.0 - ADAM_B2) * jnp.square(g)
    m_hat = m / (1.0 - ADAM_B1 ** ADAM_STEP)
    v_hat = v / (1.0 - ADAM_B2 ** ADAM_STEP)
    delta = -ADAM_LR * (m_hat / (jnp.sqrt(v_hat) + ADAM_EPS) + ADAM_WD * w)
    return delta, m, v


def _adamw_reduce(own, parts, w, m, v):
    rows, cols = w.shape
    br = 256
    n_parts = parts.shape[0]

    def body(own_ref, p_ref, w_ref, m_ref, v_ref, g_ref, d_ref, nm_ref, nv_ref):
        g = own_ref[...]
        for k in range(n_parts):
            g = g + p_ref[k].astype(F32)
        g_ref[...] = g
        d_ref[...], nm_ref[...], nv_ref[...] = _adamw_math(w_ref[...], g, m_ref[...], v_ref[...])

    blk = pl.BlockSpec((br, cols), lambda i: (i, 0))
    return pl.pallas_call(
        body, name="adamw_reduce", grid=(rows // br,),
        in_specs=[blk, pl.BlockSpec((n_parts, br, cols), lambda i: (0, i, 0)), blk, blk, blk],
        out_specs=[blk] * 4, out_shape=[jax.ShapeDtypeStruct((rows, cols), F32)] * 4,
        compiler_params=_params(1),
    )(own, parts, w, m, v)


def _adamw_reduce_many(parts_list, states):
    n = len(parts_list)

    def body(*refs):
        ins, outs = refs[:4 * n], refs[4 * n:]
        for j in range(n):
            p_ref, w_ref, m_ref, v_ref = ins[4 * j:4 * j + 4]
            g = p_ref[0].astype(F32)
            for k in range(1, p_ref.shape[0]):
                g = g + p_ref[k].astype(F32)
            outs[4 * j][...] = g
            outs[4 * j + 1][...], outs[4 * j + 2][...], outs[4 * j + 3][...] = _adamw_math(
                w_ref[...], g, m_ref[...], v_ref[...])

    flat_in = [a for p, st in zip(parts_list, states) for a in (p,) + tuple(st)]
    shapes = [jax.ShapeDtypeStruct(st[0].shape, F32) for st in states for _ in range(4)]
    flat = pl.pallas_call(body, name="adamw_reduce_many", out_shape=shapes,
                          compiler_params=pltpu.CompilerParams(vmem_limit_bytes=VMEM_LIMIT))(*flat_in)
    return [tuple(flat[4 * j:4 * j + 4]) for j in range(n)]


def _sum_parts(parts):
    def body(p_ref, o_ref):
        g = p_ref[0]
        for k in range(1, N_DEV):
            g = g + p_ref[k]
        o_ref[...] = g

    return pl.pallas_call(body, name="small_grad_sum", out_shape=jax.ShapeDtypeStruct(parts.shape[1:], F32))(parts)


def _adamw_small(gs, ws, ms, vs):
    n = len(gs)

    def body(*refs):
        ins, outs = refs[:4 * n], refs[4 * n:]
        for j in range(n):
            g_ref, w_ref, m_ref, v_ref = ins[j], ins[n + j], ins[2 * n + j], ins[3 * n + j]
            outs[3 * j][...], outs[3 * j + 1][...], outs[3 * j + 2][...] = _adamw_math(
                w_ref[...], g_ref[...], m_ref[...], v_ref[...])

    shapes = [jax.ShapeDtypeStruct(g.shape, F32) for g in gs for _ in range(3)]
    flat = pl.pallas_call(body, name="adamw_small", out_shape=shapes)(*gs, *ws, *ms, *vs)
    return [tuple(flat[3 * j:3 * j + 3]) for j in range(n)]


SMALL_ROWS = 8
LOSS_LANE = D_MODEL - 1


def _pack_small(g_fn, g_kvn, g_bn, g_an, g_ckvn, g_qn, g_conv, loss_part):
    def body(fn_ref, kvn_ref, bn_ref, an_ref, ckvn_ref, qn_ref, conv_ref, loss_ref, o_ref):
        o_ref[0:1, :] = fn_ref[...]
        o_ref[1:2, :] = kvn_ref[...]
        o_ref[2:3, :] = bn_ref[...]
        o_ref[3:4, :] = an_ref[...]
        lane = lax.broadcasted_iota(jnp.int32, (1, D_MODEL), 1)
        o_ref[4:5, :] = jnp.where(lane == LOSS_LANE, loss_ref[...], 0.0)
        o_ref[4:5, 0:KV_RANK] = ckvn_ref[...]
        o_ref[4:5, KV_RANK:KV_RANK + Q_RANK] = qn_ref[...]
        o_ref[5:8, :] = conv_ref[0:3, :]

    return pl.pallas_call(body, name="pack_small", out_shape=jax.ShapeDtypeStruct((SMALL_ROWS, D_MODEL), F32))(
        g_fn, g_kvn, g_bn, g_an, g_ckvn, g_qn, g_conv, loss_part)


def _pad_cols(a, width):
    return jnp.pad(a, ((0, 0), (0, width - a.shape[1])))


def _dkv_to_padded(a):
    r = a.shape[0]
    z = jnp.zeros((r, ROPE_LO), a.dtype)
    z2 = jnp.zeros((r, HEAD_PAD - ROPE_LO - QK_ROPE), a.dtype)
    return jnp.concatenate([a[:, :KV_RANK], z, a[:, KV_RANK:], z2], axis=1)


def _dkv_from_padded(a):
    return jnp.concatenate([a[:, :KV_RANK], a[:, KV_RANK + ROPE_LO:KV_RANK + ROPE_LO + QK_ROPE]], axis=1)


def _unstack_cols(a):
    return jnp.transpose(a, (1, 0, 2)).reshape(a.shape[1], N_DEV * a.shape[2])


def kernel(x, positions, a_norm, a_w_in, a_conv, a_w_out, kv_norm, w_dkv, ckv_norm, w_ukv, b_norm, b_w_in, b_q_norm, b_w_uq, b_w_out, final_norm, loss_target, m_a_norm, m_a_w_in, m_a_conv, m_a_w_out, m_kv_norm, m_w_dkv, m_ckv_norm, m_w_ukv, m_b_norm, m_b_w_in, m_b_q_norm, m_b_w_uq, m_b_w_out, m_final_norm, v_a_norm, v_a_w_in, v_a_conv, v_a_w_out, v_kv_norm, v_w_dkv, v_ckv_norm, v_w_ukv, v_b_norm, v_b_w_in, v_b_q_norm, v_b_w_uq, v_b_w_out, v_final_norm):
    n_seq, seq, _ = x.shape
    t = n_seq * seq
    me = 4 * lax.axis_index("x") + 2 * lax.axis_index("y") + lax.axis_index("c")

    big = {
        "a_w_in": (a_w_in[0], m_a_w_in[0], v_a_w_in[0]),
        "a_w_out": (a_w_out[0], m_a_w_out[0], v_a_w_out[0]),
        "w_dkv": tuple(_dkv_to_padded(a) for a in (w_dkv, m_w_dkv, v_w_dkv)),
        "w_ukv": (w_ukv, m_w_ukv, v_w_ukv),
        "b_w_in": (b_w_in[0], m_b_w_in[0], v_b_w_in[0]),
        "b_w_uq": tuple(_pad_cols(a[0], HEAD_PAD) for a in (b_w_uq, m_b_w_uq, v_b_w_uq)),
        "b_w_out": (b_w_out[0], m_b_w_out[0], v_b_w_out[0]),
    }
    names = list(big)
    first_names, later_names = names[:2], names[2:]
    gathered = _all_gather([big[n][0].astype(BF16) for n in first_names] + [a_norm, a_conv[0]])
    a_norm_f = gathered[2].reshape(1, D_MODEL)
    a_conv_f = _unstack_cols(gathered[3])
    w_a_in = gathered[0]
    w_a_out = gathered[1].reshape(D_MODEL, D_MODEL)

    x2d = x.reshape(t, D_MODEL)
    tgt = loss_target.reshape(t, D_MODEL)
    pos = positions.astype(F32).reshape(t, 1)
    rc = _rope_consts()
    kvn, ckvn, bn, qn, fn = kv_norm.reshape(1, -1), ckv_norm.reshape(1, -1), b_norm, b_q_norm, final_norm.reshape(1, -1)

    x1, proj, conv, later = _conv_fwd(x2d, a_norm_f, w_a_in, a_conv_f, w_a_out, n_seq, seq,
                                      [big[n][0].astype(BF16) for n in later_names])
    full = dict(zip(later_names, later))
    w_dkv_f = full["w_dkv"].reshape(D_MODEL, CKR_PAD)
    w_ukv_f = _unstack_cols(full["w_ukv"])
    w_b_in = full["b_w_in"].reshape(D_MODEL, Q_RANK + E_B)
    w_uq_f = _unstack_cols(full["b_w_uq"])
    w_b_out = _unstack_cols(full["b_w_out"])
    layer_b = (kvn, w_dkv_f, ckvn, w_ukv_f, bn, w_b_in, qn, w_uq_f)
    ckr, cq, gb, q, k, v, cosv, sinv = _proj_fwd(x1, pos, rc, *layer_b)
    o, lse_row = _attn_fwd(q, k, v, n_seq, seq)
    dx2, do, dgb, delta_row, g_b_out, g_fn, loss_part = _head_tail(o, gb, x1, w_b_out, fn, tgt, n_seq, seq)
    dk, dv, dq = _attn_bwd(q, k, v, do, lse_row, delta_row, n_seq, seq)
    dx1, g_uq, g_b_in, g_ukv, g_dkv, g_qn, g_bn, g_ckvn, g_kvn = _proj_bwd(
        dq, dk, dv, dgb, cq, ckr, x1, dx2, cosv, sinv, rc, *layer_b)
    stacks = {
        "w_dkv": g_dkv.reshape(N_DEV, D_MODEL // N_DEV, CKR_PAD),
        "w_ukv": g_ukv,
        "b_w_in": g_b_in.reshape(N_DEV, D_MODEL // N_DEV, Q_RANK + E_B),
        "b_w_uq": g_uq,
        "b_w_out": g_b_out,
    }
    (dx, h_a, dproj, g_a_out, g_an, g_conv), later_parts = _conv_bwd(
        dx1, x2d, proj, conv, a_norm_f, w_a_in, a_conv_f, w_a_out, n_seq, seq, [stacks[n] for n in later_names])
    parts = dict(zip(later_names, later_parts))
    small = _pack_small(g_fn, g_kvn, g_bn, g_an, g_ckvn, g_qn, g_conv, loss_part)
    a_in_own, a_in_got, (parts["a_w_out"],), small_parts = _w_in_grad_exchange(
        h_a, dproj, D_MODEL * 4 // N_DEV, (me // 2).reshape(1).astype(jnp.int32),
        [g_a_out.astype(BF16).reshape(N_DEV, D_MODEL // N_DEV, D_MODEL)], small)

    outs = {"a_w_in": _adamw_reduce(a_in_own, a_in_got, *big["a_w_in"])}
    outs.update(zip(names[1:], _adamw_reduce_many([parts[n] for n in names[1:]], [big[n] for n in names[1:]])))
    outs["w_dkv"] = tuple(_dkv_from_padded(a) for a in outs["w_dkv"])
    outs["b_w_uq"] = tuple(a[:, :QK_NOPE + QK_ROPE] for a in outs["b_w_uq"])
    for n in ("a_w_in", "a_w_out", "b_w_in", "b_w_uq", "b_w_out"):
        outs[n] = tuple(a[None] for a in outs[n])

    total = _sum_parts(small_parts)
    loss = total[4, LOSS_LANE]
    shard = D_MODEL // N_DEV
    g_small = {
        "final_norm": total[0], "kv_norm": total[1], "b_norm": total[2:3],
        "a_norm": lax.dynamic_slice_in_dim(total[3:4], me * shard, shard, axis=1),
        "ckv_norm": total[4, 0:KV_RANK], "b_q_norm": total[4:5, KV_RANK:KV_RANK + Q_RANK],
        "a_conv": lax.dynamic_slice_in_dim(total[5:8], me * shard, shard, axis=1)[None],
    }
    small_state = {
        "final_norm": (final_norm, m_final_norm, v_final_norm), "kv_norm": (kv_norm, m_kv_norm, v_kv_norm),
        "b_norm": (b_norm, m_b_norm, v_b_norm), "a_norm": (a_norm, m_a_norm, v_a_norm),
        "ckv_norm": (ckv_norm, m_ckv_norm, v_ckv_norm), "b_q_norm": (b_q_norm, m_b_q_norm, v_b_q_norm),
        "a_conv": (a_conv, m_a_conv, v_a_conv),
    }
    small_names = list(g_small)
    as2d = lambda a: a.reshape(-1, a.shape[-1])
    upd = _adamw_small([as2d(g_small[n]) for n in small_names],
                       *[[as2d(small_state[n][j]) for n in small_names] for j in range(3)])
    for n, u in zip(small_names, upd):
        outs[n] = (g_small[n],) + tuple(a.reshape(g_small[n].shape) for a in u)

    order = ["a_norm", "a_w_in", "a_conv", "a_w_out", "kv_norm", "w_dkv", "ckv_norm", "w_ukv", "b_norm", "b_w_in",
             "b_q_norm", "b_w_uq", "b_w_out", "final_norm"]
    result = [loss, dx.reshape(n_seq, seq, D_MODEL)]
    for j in range(4):
        result += [outs[n][j] for n in order]
    return tuple(result)
```
